```python
import jax, jax.numpy as jnp
from jax import lax
import numpy as np


D_MODEL = 1024
BATCH = 16
SEQ = 2048
DEPTH = 2

CHUNK = 64
Q_BLOCK = 128
D_MIX = D_MODEL
EPS = 1e-6
N_MOD = 6

GDN_HEADS = 4
GDN_DK = D_MIX // 4 // GDN_HEADS
GDN_DV = D_MIX // 4 // GDN_HEADS
GDN_QK = GDN_HEADS * GDN_DK
GDN_VW = GDN_HEADS * GDN_DV
GDN_CONV = 4

RG_WIDTH = D_MIX // 2
RG_BLOCKS = 8
RG_BLOCK = RG_WIDTH // RG_BLOCKS
RG_CONV = 4
RG_C = 8.0

MLA_HEADS = 4
MLA_NOPE = 64
MLA_ROPE = 32
MLA_V = D_MIX // 4 // MLA_HEADS
MLA_Q_RANK = D_MODEL // 4
MLA_KV_RANK = D_MODEL // 8
ROPE_THETA = 10000.0

D_FF = 4 * D_MODEL

IN_SIZES = (GDN_QK, GDN_QK, GDN_VW, GDN_VW, GDN_HEADS, GDN_HEADS,
            RG_WIDTH, RG_WIDTH,
            MLA_Q_RANK, MLA_KV_RANK, MLA_ROPE)
D_IN = sum(IN_SIZES)

kernel_name = 'hybrid_gdn_rglru_mla_adaln_encoder'


def rmsnorm(x, g):
    xf = x.astype(jnp.float32)
    y = xf * lax.rsqrt(jnp.mean(xf * xf, axis=-1, keepdims=True) + EPS)
    return (y * g.astype(jnp.float32)).astype(x.dtype)


def l2norm(x):
    return x * lax.rsqrt(jnp.sum(x * x, axis=-1, keepdims=True) + EPS)


def causal_depthwise_conv(x, w):
    width = w.shape[0]
    return lax.conv_general_dilated(
        x, w[:, None, :].astype(x.dtype), window_strides=(1,),
        padding=[(width - 1, 0)], dimension_numbers=('NWC', 'WIO', 'NWC'),
        feature_group_count=x.shape[-1])


def gated_delta_rule(q, k, v, g, beta):
    bsz, seq, heads, dk = q.shape
    dv = v.shape[-1]
    n = seq // CHUNK

    def to_chunks(t):
        t = t.reshape((bsz, n, CHUNK) + t.shape[2:])
        return jnp.moveaxis(t, 3, 1)

    q, k, v, g, beta = (to_chunks(t) for t in (q, k, v, g, beta))
    q = q * (dk ** -0.5)
    g = jnp.cumsum(g, axis=-1)
    k_beta = k * beta[..., None]
    v_beta = v * beta[..., None]
    lower_incl = jnp.tril(jnp.ones((CHUNK, CHUNK), bool))
    strict_lower = jnp.tril(jnp.ones((CHUNK, CHUNK), bool), -1)
    diff = g[..., :, None] - g[..., None, :]
    decay = jnp.where(lower_incl, jnp.exp(jnp.where(lower_incl, diff, 0.0)), 0.0)
    lmat = jnp.where(strict_lower, jnp.einsum('bhncd,bhnsd->bhncs', k_beta, k) * decay, 0.0)
    tmat = jnp.eye(CHUNK, dtype=lmat.dtype) + lmat
    u = lax.linalg.triangular_solve(tmat, v_beta, left_side=True, lower=True)
    w = lax.linalg.triangular_solve(tmat, k_beta * jnp.exp(g)[..., None], left_side=True, lower=True)
    qk = jnp.einsum('bhncd,bhnsd->bhncs', q, k) * decay
    q_decay = q * jnp.exp(g)[..., None]
    k_tail = k * jnp.exp(g[..., -1:] - g)[..., None]
    chunk_decay = jnp.exp(g[..., -1])

    def step(state, xs):
        qk_c, qd_c, u_c, w_c, kt_c, cd_c = xs
        v_new = u_c - jnp.einsum('bhcd,bhde->bhce', w_c, state)
        o = jnp.einsum('bhcd,bhde->bhce', qd_c, state) + jnp.einsum('bhcs,bhse->bhce', qk_c, v_new)
        state = state * cd_c[..., None, None] + jnp.einsum('bhcd,bhce->bhde', kt_c, v_new)
        return state, o

    xs = tuple(jnp.moveaxis(t, 2, 0) for t in (qk, q_decay, u, w, k_tail, chunk_decay))
    s0 = jnp.zeros((bsz, heads, dk, dv), jnp.float32)
    _, o = lax.scan(step, s0, xs)
    o = jnp.moveaxis(o, 0, 2)
    return jnp.moveaxis(o, 1, 3).reshape(bsz, seq, heads, dv)


def gdn_mixer(q, k, v, z, a, b, conv_w, a_log, dt_bias, norm_g):
    dtype = q.dtype
    bsz, seq, _ = q.shape
    qkv = jax.nn.silu(causal_depthwise_conv(jnp.concatenate([q, k, v], axis=-1), conv_w))
    qkv = qkv.astype(jnp.float32)
    q, k, v = jnp.split(qkv, [GDN_QK, 2 * GDN_QK], axis=-1)
    q = l2norm(q.reshape(bsz, seq, GDN_HEADS, GDN_DK))
    k = l2norm(k.reshape(bsz, seq, GDN_HEADS, GDN_DK))
    v = v.reshape(bsz, seq, GDN_HEADS, GDN_DV)
    g = -jnp.exp(a_log.astype(jnp.float32)) * jax.nn.softplus(a.astype(jnp.float32) + dt_bias.astype(jnp.float32))
    beta = jax.nn.sigmoid(b.astype(jnp.float32))
    o = gated_delta_rule(q, k, v, g, beta)
    zg = jax.nn.silu(z.astype(jnp.float32).reshape(bsz, seq, GDN_HEADS, GDN_DV))
    o = rmsnorm(o, norm_g) * zg
    return o.reshape(bsz, seq, GDN_VW).astype(dtype)


def rglru_mixer(xb, gate, conv_w, conv_b, w_a, b_a, w_x, b_x, lam):
    dtype = xb.dtype
    bsz, seq, _ = xb.shape
    xc = causal_depthwise_conv(xb, conv_w) + conv_b
    xblk = xc.reshape(bsz, seq, RG_BLOCKS, RG_BLOCK)
    r = jax.nn.sigmoid(jnp.einsum('bsgi,gij->bsgj', xblk, w_a).reshape(bsz, seq, RG_WIDTH) + b_a)
    i = jax.nn.sigmoid(jnp.einsum('bsgi,gij->bsgj', xblk, w_x).reshape(bsz, seq, RG_WIDTH) + b_x)
    log_a = -RG_C * r.astype(jnp.float32) * jax.nn.softplus(-lam.astype(jnp.float32))
    a = jnp.exp(log_a)
    mult = jnp.sqrt(-jnp.expm1(2.0 * log_a))
    bterm = mult * (i * xc).astype(jnp.float32)

    def combine(left, right):
        a1, b1 = left
        a2, b2 = right
        return a1 * a2, a2 * b1 + b2

    _, h = lax.associative_scan(combine, (a, bterm), axis=1)
    return (h * jax.nn.gelu(gate.astype(jnp.float32))).astype(dtype)


def rope(x, cos, sin):
    x1, x2 = jnp.split(x, 2, axis=-1)
    return jnp.concatenate([x1 * cos - x2 * sin, x2 * cos + x1 * sin], axis=-1)


def mla_mixer(q_lat, kv_lat, k_rope, positions, q_norm_g, w_qb, kv_norm_g, w_kvb):
    dtype = q_lat.dtype
    bsz, seq, _ = q_lat.shape
    q = (rmsnorm(q_lat, q_norm_g) @ w_qb).reshape(bsz, seq, MLA_HEADS, MLA_NOPE + MLA_ROPE)
    q_nope, q_pe = jnp.split(q.astype(jnp.float32), [MLA_NOPE], axis=-1)
    kv = (rmsnorm(kv_lat, kv_norm_g) @ w_kvb).reshape(bsz, seq, MLA_HEADS, MLA_NOPE + MLA_V)
    k_nope, v = jnp.split(kv.astype(jnp.float32), [MLA_NOPE], axis=-1)
    inv_freq = ROPE_THETA ** (-jnp.arange(0, MLA_ROPE, 2, dtype=jnp.float32) / MLA_ROPE)
    ang = positions.astype(jnp.float32)[..., None] * inv_freq
    cos = jnp.cos(ang)[:, :, None, :]
    sin = jnp.sin(ang)[:, :, None, :]
    q_pe = rope(q_pe, cos, sin)
    k_pe = rope(k_rope.astype(jnp.float32)[:, :, None, :], cos, sin)[:, :, 0]
    scale = (MLA_NOPE + MLA_ROPE) ** -0.5
    chunk_id = jnp.arange(seq) // CHUNK
    outs = []
    for blk in range(seq // Q_BLOCK):
        q0, q1 = blk * Q_BLOCK, (blk + 1) * Q_BLOCK
        s = (jnp.einsum('bqhd,bkhd->bhqk', q_nope[:, q0:q1], k_nope[:, :q1])
             + jnp.einsum('bqhr,bkr->bhqk', q_pe[:, q0:q1], k_pe[:, :q1])) * scale
        mask = chunk_id[None, :q1] <= chunk_id[q0:q1, None]
        p = jax.nn.softmax(jnp.where(mask, s, -jnp.inf), axis=-1)
        outs.append(jnp.einsum('bhqk,bkhd->bqhd', p, v[:, :q1]))
    o = jnp.concatenate(outs, axis=1)
    return o.reshape(bsz, seq, MLA_HEADS * MLA_V).astype(dtype)


def _fwd_setup_inputs(seed: int = 0) -> dict:
    key = jax.random.key(seed)
    ks = jax.random.split(key, 32)
    f32 = jnp.float32
    nrm = lambda k, shape, s: jax.random.normal(k, shape, f32) * s
    x = jax.random.normal(ks[0], (BATCH, SEQ, D_MODEL), f32)
    c = jax.random.normal(ks[1], (BATCH, D_MODEL), f32)
    offsets = jax.random.randint(ks[2], (BATCH, 1), 0, 4096, dtype=jnp.int32)
    positions = offsets + jnp.arange(SEQ, dtype=jnp.int32)[None, :]
    w_mod = nrm(ks[3], (DEPTH, D_MODEL, N_MOD * D_MODEL), 0.5 * D_MODEL ** -0.5)
    b_mod = nrm(ks[4], (DEPTH, N_MOD * D_MODEL), 0.02)
    norm_mix_g = 1.0 + nrm(ks[5], (DEPTH, D_MODEL), 0.02)
    w_in = nrm(ks[6], (DEPTH, D_MODEL, D_IN), D_MODEL ** -0.5)
    gdn_conv_w = nrm(ks[7], (DEPTH, GDN_CONV, 2 * GDN_QK + GDN_VW), GDN_CONV ** -0.5)
    gdn_a_log = jnp.log(jax.random.uniform(ks[8], (DEPTH, GDN_HEADS), f32, 1.0, 16.0))
    dt = jnp.exp(jax.random.uniform(ks[9], (DEPTH, GDN_HEADS), f32, np.log(1e-3), np.log(1e-1)))
    gdn_dt_bias = dt + jnp.log(-jnp.expm1(-dt))
    gdn_norm_g = 1.0 + nrm(ks[10], (DEPTH, GDN_DV), 0.02)
    rg_conv_w = nrm(ks[11], (DEPTH, RG_CONV, RG_WIDTH), RG_CONV ** -0.5)
    rg_conv_b = nrm(ks[12], (DEPTH, RG_WIDTH), 0.02)
    rg_w_a = nrm(ks[13], (DEPTH, RG_BLOCKS, RG_BLOCK, RG_BLOCK), RG_BLOCK ** -0.5)
    rg_b_a = nrm(ks[14], (DEPTH, RG_WIDTH), 0.02)
    rg_w_x = nrm(ks[15], (DEPTH, RG_BLOCKS, RG_BLOCK, RG_BLOCK), RG_BLOCK ** -0.5)
    rg_b_x = nrm(ks[16], (DEPTH, RG_WIDTH), 0.02)
    a0 = jax.random.uniform(ks[17], (DEPTH, RG_WIDTH), f32, 0.9, 0.999) ** (1.0 / RG_C)
    rg_lambda = jnp.log(a0) - jnp.log1p(-a0)
    mla_q_norm_g = 1.0 + nrm(ks[18], (DEPTH, MLA_Q_RANK), 0.02)
    mla_w_qb = nrm(ks[19], (DEPTH, MLA_Q_RANK, MLA_HEADS * (MLA_NOPE + MLA_ROPE)), MLA_Q_RANK ** -0.5)
    mla_kv_norm_g = 1.0 + nrm(ks[20], (DEPTH, MLA_KV_RANK), 0.02)
    mla_w_kvb = nrm(ks[21], (DEPTH, MLA_KV_RANK, MLA_HEADS * (MLA_NOPE + MLA_V)), MLA_KV_RANK ** -0.5)
    w_out = nrm(ks[22], (DEPTH, D_MIX, D_MODEL), D_MIX ** -0.5)
    norm_mlp_g = 1.0 + nrm(ks[23], (DEPTH, D_MODEL), 0.02)
    w_mlp_in = nrm(ks[24], (DEPTH, D_MODEL, D_FF), D_MODEL ** -0.5)
    w_mlp_out = nrm(ks[25], (DEPTH, D_FF, D_MODEL), D_FF ** -0.5)
    final_norm_g = 1.0 + nrm(ks[26], (D_MODEL,), 0.02)
    return {'x': x, 'c': c, 'positions': positions, 'w_mod': w_mod, 'b_mod': b_mod,
            'norm_mix_g': norm_mix_g, 'w_in': w_in, 'gdn_conv_w': gdn_conv_w,
            'gdn_a_log': gdn_a_log, 'gdn_dt_bias': gdn_dt_bias, 'gdn_norm_g': gdn_norm_g,
            'rg_conv_w': rg_conv_w, 'rg_conv_b': rg_conv_b, 'rg_w_a': rg_w_a, 'rg_b_a': rg_b_a,
            'rg_w_x': rg_w_x, 'rg_b_x': rg_b_x, 'rg_lambda': rg_lambda,
            'mla_q_norm_g': mla_q_norm_g, 'mla_w_qb': mla_w_qb, 'mla_kv_norm_g': mla_kv_norm_g,
            'mla_w_kvb': mla_w_kvb, 'w_out': w_out, 'norm_mlp_g': norm_mlp_g,
            'w_mlp_in': w_mlp_in, 'w_mlp_out': w_mlp_out, 'final_norm_g': final_norm_g}


def _fwd_reference(x, c, positions, w_mod, b_mod, norm_mix_g, w_in, gdn_conv_w, gdn_a_log, gdn_dt_bias,
              gdn_norm_g, rg_conv_w, rg_conv_b, rg_w_a, rg_b_a, rg_w_x, rg_b_x, rg_lambda,
              mla_q_norm_g, mla_w_qb, mla_kv_norm_g, mla_w_kvb, w_out, norm_mlp_g,
              w_mlp_in, w_mlp_out, final_norm_g):
    split_points = np.cumsum(IN_SIZES)[:-1].tolist()
    h = x
    c_act = jax.nn.silu(c)
    for l in range(DEPTH):
        mod = c_act @ w_mod[l] + b_mod[l]
        sh_m, sc_m, gt_m, sh_f, sc_f, gt_f = (m[:, None, :] for m in jnp.split(mod, N_MOD, axis=-1))
        u = rmsnorm(h, norm_mix_g[l]) * (1.0 + sc_m) + sh_m
        proj = u @ w_in[l]
        gq, gk, gv, gz, ga, gb, rx, rgate, mq, mkv, mkr = jnp.split(proj, split_points, axis=-1)
        o_a = gdn_mixer(gq, gk, gv, gz, ga, gb, gdn_conv_w[l], gdn_a_log[l], gdn_dt_bias[l], gdn_norm_g[l])
        o_b = rglru_mixer(rx, rgate, rg_conv_w[l], rg_conv_b[l], rg_w_a[l], rg_b_a[l],
                          rg_w_x[l], rg_b_x[l], rg_lambda[l])
        o_c = mla_mixer(mq, mkv, mkr, positions, mla_q_norm_g[l], mla_w_qb[l],
                        mla_kv_norm_g[l], mla_w_kvb[l])
        mix = jnp.concatenate([o_a, o_b, o_c], axis=-1) @ w_out[l]
        h = h + gt_m * mix
        u = rmsnorm(h, norm_mlp_g[l]) * (1.0 + sc_f) + sh_f
        f = jnp.square(jax.nn.relu(u @ w_mlp_in[l])) @ w_mlp_out[l]
        h = h + gt_f * f
    return rmsnorm(h, final_norm_g)


import jax as _jax
import jax.numpy as _jnp

TWIN_FORMAT = 'train_step'
FWD_PARAMS = ['x', 'c', 'positions', 'w_mod', 'b_mod', 'norm_mix_g', 'w_in', 'gdn_conv_w', 'gdn_a_log', 'gdn_dt_bias', 'gdn_norm_g', 'rg_conv_w', 'rg_conv_b', 'rg_w_a', 'rg_b_a', 'rg_w_x', 'rg_b_x', 'rg_lambda', 'mla_q_norm_g', 'mla_w_qb', 'mla_kv_norm_g', 'mla_w_kvb', 'w_out', 'norm_mlp_g', 'w_mlp_in', 'w_mlp_out', 'final_norm_g']
TWIN_WEIGHTS = ['w_mod', 'b_mod', 'norm_mix_g', 'w_in', 'gdn_conv_w', 'gdn_a_log', 'gdn_dt_bias', 'gdn_norm_g', 'rg_conv_w', 'rg_conv_b', 'rg_w_a', 'rg_b_a', 'rg_w_x', 'rg_b_x', 'rg_lambda', 'mla_q_norm_g', 'mla_w_qb', 'mla_kv_norm_g', 'mla_w_kvb', 'w_out', 'norm_mlp_g', 'w_mlp_in', 'w_mlp_out', 'final_norm_g']
TWIN_DIFF_INPUT = 'x'
TWIN_INPUTS = ['x', 'c', 'positions', 'w_mod', 'b_mod', 'norm_mix_g', 'w_in', 'gdn_conv_w', 'gdn_a_log', 'gdn_dt_bias', 'gdn_norm_g', 'rg_conv_w', 'rg_conv_b', 'rg_w_a', 'rg_b_a', 'rg_w_x', 'rg_b_x', 'rg_lambda', 'mla_q_norm_g', 'mla_w_qb', 'mla_kv_norm_g', 'mla_w_kvb', 'w_out', 'norm_mlp_g', 'w_mlp_in', 'w_mlp_out', 'final_norm_g', 'loss_target', 'm_w_mod', 'm_b_mod', 'm_norm_mix_g', 'm_w_in', 'm_gdn_conv_w', 'm_gdn_a_log', 'm_gdn_dt_bias', 'm_gdn_norm_g', 'm_rg_conv_w', 'm_rg_conv_b', 'm_rg_w_a', 'm_rg_b_a', 'm_rg_w_x', 'm_rg_b_x', 'm_rg_lambda', 'm_mla_q_norm_g', 'm_mla_w_qb', 'm_mla_kv_norm_g', 'm_mla_w_kvb', 'm_w_out', 'm_norm_mlp_g', 'm_w_mlp_in', 'm_w_mlp_out', 'm_final_norm_g', 'v_w_mod', 'v_b_mod', 'v_norm_mix_g', 'v_w_in', 'v_gdn_conv_w', 'v_gdn_a_log', 'v_gdn_dt_bias', 'v_gdn_norm_g', 'v_rg_conv_w', 'v_rg_conv_b', 'v_rg_w_a', 'v_rg_b_a', 'v_rg_w_x', 'v_rg_b_x', 'v_rg_lambda', 'v_mla_q_norm_g', 'v_mla_w_qb', 'v_mla_kv_norm_g', 'v_mla_w_kvb', 'v_w_out', 'v_norm_mlp_g', 'v_w_mlp_in', 'v_w_mlp_out', 'v_final_norm_g']
TWIN_OUTPUTS = ['loss', 'grad_x', 'grad_w_mod', 'grad_b_mod', 'grad_norm_mix_g', 'grad_w_in', 'grad_gdn_conv_w', 'grad_gdn_a_log', 'grad_gdn_dt_bias', 'grad_gdn_norm_g', 'grad_rg_conv_w', 'grad_rg_conv_b', 'grad_rg_w_a', 'grad_rg_b_a', 'grad_rg_w_x', 'grad_rg_b_x', 'grad_rg_lambda', 'grad_mla_q_norm_g', 'grad_mla_w_qb', 'grad_mla_kv_norm_g', 'grad_mla_w_kvb', 'grad_w_out', 'grad_norm_mlp_g', 'grad_w_mlp_in', 'grad_w_mlp_out', 'grad_final_norm_g', 'delta_w_mod', 'delta_b_mod', 'delta_norm_mix_g', 'delta_w_in', 'delta_gdn_conv_w', 'delta_gdn_a_log', 'delta_gdn_dt_bias', 'delta_gdn_norm_g', 'delta_rg_conv_w', 'delta_rg_conv_b', 'delta_rg_w_a', 'delta_rg_b_a', 'delta_rg_w_x', 'delta_rg_b_x', 'delta_rg_lambda', 'delta_mla_q_norm_g', 'delta_mla_w_qb', 'delta_mla_kv_norm_g', 'delta_mla_w_kvb', 'delta_w_out', 'delta_norm_mlp_g', 'delta_w_mlp_in', 'delta_w_mlp_out', 'delta_final_norm_g', 'new_m_w_mod', 'new_m_b_mod', 'new_m_norm_mix_g', 'new_m_w_in', 'new_m_gdn_conv_w', 'new_m_gdn_a_log', 'new_m_gdn_dt_bias', 'new_m_gdn_norm_g', 'new_m_rg_conv_w', 'new_m_rg_conv_b', 'new_m_rg_w_a', 'new_m_rg_b_a', 'new_m_rg_w_x', 'new_m_rg_b_x', 'new_m_rg_lambda', 'new_m_mla_q_norm_g', 'new_m_mla_w_qb', 'new_m_mla_kv_norm_g', 'new_m_mla_w_kvb', 'new_m_w_out', 'new_m_norm_mlp_g', 'new_m_w_mlp_in', 'new_m_w_mlp_out', 'new_m_final_norm_g', 'new_v_w_mod', 'new_v_b_mod', 'new_v_norm_mix_g', 'new_v_w_in', 'new_v_gdn_conv_w', 'new_v_gdn_a_log', 'new_v_gdn_dt_bias', 'new_v_gdn_norm_g', 'new_v_rg_conv_w', 'new_v_rg_conv_b', 'new_v_rg_w_a', 'new_v_rg_b_a', 'new_v_rg_w_x', 'new_v_rg_b_x', 'new_v_rg_lambda', 'new_v_mla_q_norm_g', 'new_v_mla_w_qb', 'new_v_mla_kv_norm_g', 'new_v_mla_w_kvb', 'new_v_w_out', 'new_v_norm_mlp_g', 'new_v_w_mlp_in', 'new_v_w_mlp_out', 'new_v_final_norm_g']
TWIN_LEAF_KINDS = {'loss': 'loss', 'grad_x': 'grad_x', 'grad_w_mod': 'grad_w', 'grad_b_mod': 'grad_w', 'grad_norm_mix_g': 'grad_w', 'grad_w_in': 'grad_w', 'grad_gdn_conv_w': 'grad_w', 'grad_gdn_a_log': 'grad_w', 'grad_gdn_dt_bias': 'grad_w', 'grad_gdn_norm_g': 'grad_w', 'grad_rg_conv_w': 'grad_w', 'grad_rg_conv_b': 'grad_w', 'grad_rg_w_a': 'grad_w', 'grad_rg_b_a': 'grad_w', 'grad_rg_w_x': 'grad_w', 'grad_rg_b_x': 'grad_w', 'grad_rg_lambda': 'grad_w', 'grad_mla_q_norm_g': 'grad_w', 'grad_mla_w_qb': 'grad_w', 'grad_mla_kv_norm_g': 'grad_w', 'grad_mla_w_kvb': 'grad_w', 'grad_w_out': 'grad_w', 'grad_norm_mlp_g': 'grad_w', 'grad_w_mlp_in': 'grad_w', 'grad_w_mlp_out': 'grad_w', 'grad_final_norm_g': 'grad_w', 'delta_w_mod': 'delta_w', 'delta_b_mod': 'delta_w', 'delta_norm_mix_g': 'delta_w', 'delta_w_in': 'delta_w', 'delta_gdn_conv_w': 'delta_w', 'delta_gdn_a_log': 'delta_w', 'delta_gdn_dt_bias': 'delta_w', 'delta_gdn_norm_g': 'delta_w', 'delta_rg_conv_w': 'delta_w', 'delta_rg_conv_b': 'delta_w', 'delta_rg_w_a': 'delta_w', 'delta_rg_b_a': 'delta_w', 'delta_rg_w_x': 'delta_w', 'delta_rg_b_x': 'delta_w', 'delta_rg_lambda': 'delta_w', 'delta_mla_q_norm_g': 'delta_w', 'delta_mla_w_qb': 'delta_w', 'delta_mla_kv_norm_g': 'delta_w', 'delta_mla_w_kvb': 'delta_w', 'delta_w_out': 'delta_w', 'delta_norm_mlp_g': 'delta_w', 'delta_w_mlp_in': 'delta_w', 'delta_w_mlp_out': 'delta_w', 'delta_final_norm_g': 'delta_w', 'new_m_w_mod': 'new_m', 'new_m_b_mod': 'new_m', 'new_m_norm_mix_g': 'new_m', 'new_m_w_in': 'new_m', 'new_m_gdn_conv_w': 'new_m', 'new_m_gdn_a_log': 'new_m', 'new_m_gdn_dt_bias': 'new_m', 'new_m_gdn_norm_g': 'new_m', 'new_m_rg_conv_w': 'new_m', 'new_m_rg_conv_b': 'new_m', 'new_m_rg_w_a': 'new_m', 'new_m_rg_b_a': 'new_m', 'new_m_rg_w_x': 'new_m', 'new_m_rg_b_x': 'new_m', 'new_m_rg_lambda': 'new_m', 'new_m_mla_q_norm_g': 'new_m', 'new_m_mla_w_qb': 'new_m', 'new_m_mla_kv_norm_g': 'new_m', 'new_m_mla_w_kvb': 'new_m', 'new_m_w_out': 'new_m', 'new_m_norm_mlp_g': 'new_m', 'new_m_w_mlp_in': 'new_m', 'new_m_w_mlp_out': 'new_m', 'new_m_final_norm_g': 'new_m', 'new_v_w_mod': 'new_v', 'new_v_b_mod': 'new_v', 'new_v_norm_mix_g': 'new_v', 'new_v_w_in': 'new_v', 'new_v_gdn_conv_w': 'new_v', 'new_v_gdn_a_log': 'new_v', 'new_v_gdn_dt_bias': 'new_v', 'new_v_gdn_norm_g': 'new_v', 'new_v_rg_conv_w': 'new_v', 'new_v_rg_conv_b': 'new_v', 'new_v_rg_w_a': 'new_v', 'new_v_rg_b_a': 'new_v', 'new_v_rg_w_x': 'new_v', 'new_v_rg_b_x': 'new_v', 'new_v_rg_lambda': 'new_v', 'new_v_mla_q_norm_g': 'new_v', 'new_v_mla_w_qb': 'new_v', 'new_v_mla_kv_norm_g': 'new_v', 'new_v_mla_w_kvb': 'new_v', 'new_v_w_out': 'new_v', 'new_v_norm_mlp_g': 'new_v', 'new_v_w_mlp_in': 'new_v', 'new_v_w_mlp_out': 'new_v', 'new_v_final_norm_g': 'new_v'}


def _forward(args):
    return _fwd_reference(*[args[k] for k in FWD_PARAMS])


def _output_shape():
    out = _jax.eval_shape(lambda: _forward(_fwd_setup_inputs(0)))
    return out.shape, out.dtype

N_MICROBATCH = 1
ADAM_LR = 0.001
ADAM_B1 = 0.9
ADAM_B2 = 0.999
ADAM_EPS = 1e-08
ADAM_WD = 0.01
ADAM_STEP = 10
PER_EXAMPLE_BATCH_AXIS = {'x': 0, 'c': 0, 'positions': 0, 'loss_target': 0}
SHARED_INPUTS = []
_WEIGHT_DTYPES = {'w_mod': _jnp.float32, 'b_mod': _jnp.float32, 'norm_mix_g': _jnp.float32, 'w_in': _jnp.float32, 'gdn_conv_w': _jnp.float32, 'gdn_a_log': _jnp.float32, 'gdn_dt_bias': _jnp.float32, 'gdn_norm_g': _jnp.float32, 'rg_conv_w': _jnp.float32, 'rg_conv_b': _jnp.float32, 'rg_w_a': _jnp.float32, 'rg_b_a': _jnp.float32, 'rg_w_x': _jnp.float32, 'rg_b_x': _jnp.float32, 'rg_lambda': _jnp.float32, 'mla_q_norm_g': _jnp.float32, 'mla_w_qb': _jnp.float32, 'mla_kv_norm_g': _jnp.float32, 'mla_w_kvb': _jnp.float32, 'w_out': _jnp.float32, 'norm_mlp_g': _jnp.float32, 'w_mlp_in': _jnp.float32, 'w_mlp_out': _jnp.float32, 'final_norm_g': _jnp.float32}
MOMENT_SCALE = {'w_mod': 1.210094e-01, 'b_mod': 2.049890e-01, 'norm_mix_g': 8.056655e-02, 'w_in': 7.242337e-02, 'gdn_conv_w': 3.079034e-02, 'gdn_a_log': 1.276666e-01, 'gdn_dt_bias': 1.305678e-01, 'gdn_norm_g': 6.755057e-02, 'rg_conv_w': 1.190215e-01, 'rg_conv_b': 3.499212e-01, 'rg_w_a': 1.311654e-02, 'rg_b_a': 2.449196e-02, 'rg_w_x': 3.135883e-02, 'rg_b_x': 4.050657e-02, 'rg_lambda': 6.059985e-02, 'mla_q_norm_g': 7.804309e-03, 'mla_w_qb': 6.318363e-03, 'mla_kv_norm_g': 2.793708e-02, 'mla_w_kvb': 1.421468e-02, 'w_out': 8.291055e-02, 'norm_mlp_g': 7.763139e-02, 'w_mlp_in': 4.010788e-02, 'w_mlp_out': 7.541077e-02, 'final_norm_g': 3.251775e+01}


def _to_microbatches(a, axis):
    t = _jnp.moveaxis(a, axis, 0)
    t = t.reshape((N_MICROBATCH, t.shape[0] // N_MICROBATCH) + t.shape[1:])
    return _jnp.moveaxis(t, 1, axis + 1)


def setup_inputs(seed: int = 0) -> dict:
    inp = _fwd_setup_inputs(seed)
    key = _jax.random.fold_in(_jax.random.key(seed), 7919)
    shape, _ = _output_shape()
    out = dict(inp)
    out["loss_target"] = _jax.random.normal(_jax.random.fold_in(key, 0), shape, _jnp.float32)
    for i, name in enumerate(TWIN_WEIGHTS):
        w = inp[name].astype(_jnp.float32)
        if MOMENT_SCALE is None:
            s = _jnp.sqrt(_jnp.mean(_jnp.square(w)) + 1e-30)
        else:
            s = MOMENT_SCALE[name]
        km, kv = _jax.random.split(_jax.random.fold_in(key, i + 1))
        out[name] = w
        out["m_" + name] = s * _jax.random.normal(km, w.shape, _jnp.float32)
        out["v_" + name] = (s * s) * _jax.random.uniform(kv, w.shape, _jnp.float32, 0.5, 1.5)
    if N_MICROBATCH > 1:
        for name, axis in PER_EXAMPLE_BATCH_AXIS.items():
            out[name] = _to_microbatches(out[name], axis)
    return {'x': out['x'], 'c': out['c'], 'positions': out['positions'], 'w_mod': out['w_mod'], 'b_mod': out['b_mod'], 'norm_mix_g': out['norm_mix_g'], 'w_in': out['w_in'], 'gdn_conv_w': out['gdn_conv_w'], 'gdn_a_log': out['gdn_a_log'], 'gdn_dt_bias': out['gdn_dt_bias'], 'gdn_norm_g': out['gdn_norm_g'], 'rg_conv_w': out['rg_conv_w'], 'rg_conv_b': out['rg_conv_b'], 'rg_w_a': out['rg_w_a'], 'rg_b_a': out['rg_b_a'], 'rg_w_x': out['rg_w_x'], 'rg_b_x': out['rg_b_x'], 'rg_lambda': out['rg_lambda'], 'mla_q_norm_g': out['mla_q_norm_g'], 'mla_w_qb': out['mla_w_qb'], 'mla_kv_norm_g': out['mla_kv_norm_g'], 'mla_w_kvb': out['mla_w_kvb'], 'w_out': out['w_out'], 'norm_mlp_g': out['norm_mlp_g'], 'w_mlp_in': out['w_mlp_in'], 'w_mlp_out': out['w_mlp_out'], 'final_norm_g': out['final_norm_g'], 'loss_target': out['loss_target'], 'm_w_mod': out['m_w_mod'], 'm_b_mod': out['m_b_mod'], 'm_norm_mix_g': out['m_norm_mix_g'], 'm_w_in': out['m_w_in'], 'm_gdn_conv_w': out['m_gdn_conv_w'], 'm_gdn_a_log': out['m_gdn_a_log'], 'm_gdn_dt_bias': out['m_gdn_dt_bias'], 'm_gdn_norm_g': out['m_gdn_norm_g'], 'm_rg_conv_w': out['m_rg_conv_w'], 'm_rg_conv_b': out['m_rg_conv_b'], 'm_rg_w_a': out['m_rg_w_a'], 'm_rg_b_a': out['m_rg_b_a'], 'm_rg_w_x': out['m_rg_w_x'], 'm_rg_b_x': out['m_rg_b_x'], 'm_rg_lambda': out['m_rg_lambda'], 'm_mla_q_norm_g': out['m_mla_q_norm_g'], 'm_mla_w_qb': out['m_mla_w_qb'], 'm_mla_kv_norm_g': out['m_mla_kv_norm_g'], 'm_mla_w_kvb': out['m_mla_w_kvb'], 'm_w_out': out['m_w_out'], 'm_norm_mlp_g': out['m_norm_mlp_g'], 'm_w_mlp_in': out['m_w_mlp_in'], 'm_w_mlp_out': out['m_w_mlp_out'], 'm_final_norm_g': out['m_final_norm_g'], 'v_w_mod': out['v_w_mod'], 'v_b_mod': out['v_b_mod'], 'v_norm_mix_g': out['v_norm_mix_g'], 'v_w_in': out['v_w_in'], 'v_gdn_conv_w': out['v_gdn_conv_w'], 'v_gdn_a_log': out['v_gdn_a_log'], 'v_gdn_dt_bias': out['v_gdn_dt_bias'], 'v_gdn_norm_g': out['v_gdn_norm_g'], 'v_rg_conv_w': out['v_rg_conv_w'], 'v_rg_conv_b': out['v_rg_conv_b'], 'v_rg_w_a': out['v_rg_w_a'], 'v_rg_b_a': out['v_rg_b_a'], 'v_rg_w_x': out['v_rg_w_x'], 'v_rg_b_x': out['v_rg_b_x'], 'v_rg_lambda': out['v_rg_lambda'], 'v_mla_q_norm_g': out['v_mla_q_norm_g'], 'v_mla_w_qb': out['v_mla_w_qb'], 'v_mla_kv_norm_g': out['v_mla_kv_norm_g'], 'v_mla_w_kvb': out['v_mla_w_kvb'], 'v_w_out': out['v_w_out'], 'v_norm_mlp_g': out['v_norm_mlp_g'], 'v_w_mlp_in': out['v_w_mlp_in'], 'v_w_mlp_out': out['v_w_mlp_out'], 'v_final_norm_g': out['v_final_norm_g']}


def _loss(weights, diff, rest, loss_target):
    with _jax.named_scope("forward"):
        args = {**rest, TWIN_DIFF_INPUT: diff, **{k: w.astype(_WEIGHT_DTYPES[k]) for k, w in weights.items()}}
        y = _forward(args)
    with _jax.named_scope("loss_head"):
        err = _jnp.square(y.astype(_jnp.float32) - loss_target)
        return 0.5 * _jnp.sum(_jnp.mean(err, axis=-1)) if err.ndim else 0.5 * err


def _adamw(w, g, m, v):
    m = ADAM_B1 * m + (1.0 - ADAM_B1) * g
    v = ADAM_B2 * v + (1.0 - ADAM_B2) * _jnp.square(g)
    m_hat = m / (1.0 - ADAM_B1 ** ADAM_STEP)
    v_hat = v / (1.0 - ADAM_B2 ** ADAM_STEP)
    delta = -ADAM_LR * (m_hat / (_jnp.sqrt(v_hat) + ADAM_EPS) + ADAM_WD * w)
    return delta, m, v


def reference(x, c, positions, w_mod, b_mod, norm_mix_g, w_in, gdn_conv_w, gdn_a_log, gdn_dt_bias, gdn_norm_g, rg_conv_w, rg_conv_b, rg_w_a, rg_b_a, rg_w_x, rg_b_x, rg_lambda, mla_q_norm_g, mla_w_qb, mla_kv_norm_g, mla_w_kvb, w_out, norm_mlp_g, w_mlp_in, w_mlp_out, final_norm_g, loss_target, m_w_mod, m_b_mod, m_norm_mix_g, m_w_in, m_gdn_conv_w, m_gdn_a_log, m_gdn_dt_bias, m_gdn_norm_g, m_rg_conv_w, m_rg_conv_b, m_rg_w_a, m_rg_b_a, m_rg_w_x, m_rg_b_x, m_rg_lambda, m_mla_q_norm_g, m_mla_w_qb, m_mla_kv_norm_g, m_mla_w_kvb, m_w_out, m_norm_mlp_g, m_w_mlp_in, m_w_mlp_out, m_final_norm_g, v_w_mod, v_b_mod, v_norm_mix_g, v_w_in, v_gdn_conv_w, v_gdn_a_log, v_gdn_dt_bias, v_gdn_norm_g, v_rg_conv_w, v_rg_conv_b, v_rg_w_a, v_rg_b_a, v_rg_w_x, v_rg_b_x, v_rg_lambda, v_mla_q_norm_g, v_mla_w_qb, v_mla_kv_norm_g, v_mla_w_kvb, v_w_out, v_norm_mlp_g, v_w_mlp_in, v_w_mlp_out, v_final_norm_g):
    given = dict(x=x, c=c, positions=positions, w_mod=w_mod, b_mod=b_mod, norm_mix_g=norm_mix_g, w_in=w_in, gdn_conv_w=gdn_conv_w, gdn_a_log=gdn_a_log, gdn_dt_bias=gdn_dt_bias, gdn_norm_g=gdn_norm_g, rg_conv_w=rg_conv_w, rg_conv_b=rg_conv_b, rg_w_a=rg_w_a, rg_b_a=rg_b_a, rg_w_x=rg_w_x, rg_b_x=rg_b_x, rg_lambda=rg_lambda, mla_q_norm_g=mla_q_norm_g, mla_w_qb=mla_w_qb, mla_kv_norm_g=mla_kv_norm_g, mla_w_kvb=mla_w_kvb, w_out=w_out, norm_mlp_g=norm_mlp_g, w_mlp_in=w_mlp_in, w_mlp_out=w_mlp_out, final_norm_g=final_norm_g, loss_target=loss_target, m_w_mod=m_w_mod, m_b_mod=m_b_mod, m_norm_mix_g=m_norm_mix_g, m_w_in=m_w_in, m_gdn_conv_w=m_gdn_conv_w, m_gdn_a_log=m_gdn_a_log, m_gdn_dt_bias=m_gdn_dt_bias, m_gdn_norm_g=m_gdn_norm_g, m_rg_conv_w=m_rg_conv_w, m_rg_conv_b=m_rg_conv_b, m_rg_w_a=m_rg_w_a, m_rg_b_a=m_rg_b_a, m_rg_w_x=m_rg_w_x, m_rg_b_x=m_rg_b_x, m_rg_lambda=m_rg_lambda, m_mla_q_norm_g=m_mla_q_norm_g, m_mla_w_qb=m_mla_w_qb, m_mla_kv_norm_g=m_mla_kv_norm_g, m_mla_w_kvb=m_mla_w_kvb, m_w_out=m_w_out, m_norm_mlp_g=m_norm_mlp_g, m_w_mlp_in=m_w_mlp_in, m_w_mlp_out=m_w_mlp_out, m_final_norm_g=m_final_norm_g, v_w_mod=v_w_mod, v_b_mod=v_b_mod, v_norm_mix_g=v_norm_mix_g, v_w_in=v_w_in, v_gdn_conv_w=v_gdn_conv_w, v_gdn_a_log=v_gdn_a_log, v_gdn_dt_bias=v_gdn_dt_bias, v_gdn_norm_g=v_gdn_norm_g, v_rg_conv_w=v_rg_conv_w, v_rg_conv_b=v_rg_conv_b, v_rg_w_a=v_rg_w_a, v_rg_b_a=v_rg_b_a, v_rg_w_x=v_rg_w_x, v_rg_b_x=v_rg_b_x, v_rg_lambda=v_rg_lambda, v_mla_q_norm_g=v_mla_q_norm_g, v_mla_w_qb=v_mla_w_qb, v_mla_kv_norm_g=v_mla_kv_norm_g, v_mla_w_kvb=v_mla_w_kvb, v_w_out=v_w_out, v_norm_mlp_g=v_norm_mlp_g, v_w_mlp_in=v_w_mlp_in, v_w_mlp_out=v_w_mlp_out, v_final_norm_g=v_final_norm_g)
    weights = {n: given[n] for n in TWIN_WEIGHTS}
    shared = {n: given[n] for n in SHARED_INPUTS}
    per_example = {n: given[n] for n in ['x', 'c', 'positions']}
    grad_fn = _jax.value_and_grad(_loss, argnums=(0, 1))

    def one_microbatch(ex, loss_target):
        ex = dict(ex)
        diff = ex.pop(TWIN_DIFF_INPUT)
        return grad_fn(weights, diff, {**shared, **ex}, loss_target)

    if N_MICROBATCH == 1:
        loss, (grad_w, grad_x) = one_microbatch(per_example, given["loss_target"])
    else:
        def body(carry, xs):
            loss_sum, grad_sum = carry
            l_k, (gw_k, gx_k) = one_microbatch(xs[0], xs[1])
            with _jax.named_scope("update"):
                return (loss_sum + l_k, _jax.tree.map(_jnp.add, grad_sum, gw_k)), gx_k

        init = (_jnp.zeros((), _jnp.float32), _jax.tree.map(_jnp.zeros_like, weights))
        (loss, grad_w), grad_x = _jax.lax.scan(body, init, (per_example, given["loss_target"]))
    with _jax.named_scope("update"):
        delta_w, new_m, new_v = {}, {}, {}
        for n in TWIN_WEIGHTS:
            delta_w[n], new_m[n], new_v[n] = _adamw(weights[n], grad_w[n], given["m_" + n], given["v_" + n])
    return (loss, grad_x, *[grad_w[n] for n in TWIN_WEIGHTS], *[delta_w[n] for n in TWIN_WEIGHTS],
            *[new_m[n] for n in TWIN_WEIGHTS], *[new_v[n] for n in TWIN_WEIGHTS])
```

```python
import functools

import jax
import jax.numpy as jnp
import numpy as np
from jax import lax
from jax.experimental import pallas as pl
from jax.experimental.pallas import tpu as pltpu

F32, BF16 = jnp.float32, jnp.bfloat16
HI = lax.Precision.HIGHEST
MESH = pl.DeviceIdType.MESH

EPS = 1e-6
CHUNK = 64
GDN_HEADS = 4
MLA_HEADS = 4
RG_C = 8.0
ROPE_THETA = 10000.0
N_DEV = 8
N_CHIP = 4
V7X_VMEM_LIMIT = 60 * 1024 * 1024
ADAM_LR, ADAM_B1, ADAM_B2, ADAM_EPS, ADAM_WD, ADAM_STEP = 0.001, 0.9, 0.999, 1e-08, 0.01, 10


def _params(n_grid):
    return pltpu.CompilerParams(dimension_semantics=("arbitrary",) * n_grid, vmem_limit_bytes=V7X_VMEM_LIMIT)


def _dot(a, b, dims=(((1,), (0,)), ((), ()))):
    return lax.dot_general(a.astype(BF16), b.astype(BF16), dims, preferred_element_type=F32)


@jax.custom_vjp
def _mm_probe(x, w, probe):
    return _dot(x, w)


def _mm_probe_fwd(x, w, probe):
    return _dot(x, w), (x, w)


def _mm_probe_bwd(res, dy):
    x, w = res
    dx = _dot(dy, w, (((1,), (1,)), ((), ())))
    dw = _dot(x, dy, (((0,), (0,)), ((), ())))
    return dx, jnp.zeros_like(w), dw


_mm_probe.defvjp(_mm_probe_fwd, _mm_probe_bwd)


@jax.custom_vjp
def mmw(x, w):
    return _dot(x, w)


def _mmw_fwd(x, w):
    return _dot(x, w), (x, w)


def _mmw_bwd(res, dy):
    x, w = res
    return _dot(dy, w, (((1,), (1,)), ((), ()))), _dot(x, dy, (((0,), (0,)), ((), ())))


mmw.defvjp(_mmw_fwd, _mmw_bwd)


def rms(x, g):
    return x * lax.rsqrt(jnp.mean(x * x, axis=-1, keepdims=True) + EPS) * g


def _rows(shape):
    return lax.broadcasted_iota(jnp.int32, shape, 0)


def _shift_down(x, s, fill):
    return jnp.where(_rows(x.shape) < s, fill, pltpu.roll(x, s, 0))


def _shift_up(x, s, fill):
    n = x.shape[0]
    return jnp.where(_rows(x.shape) >= n - s, fill, pltpu.roll(x, n - s, 0))


def _make_tshift(s):
    @jax.custom_vjp
    def tshift(x):
        return _shift_down(x, s, 0.0)

    tshift.defvjp(lambda x: (_shift_down(x, s, 0.0), None), lambda _, dy: (_shift_up(dy, s, 0.0),))
    return tshift


_TSHIFT = {s: _make_tshift(s) for s in (1, 2, 3)}


def causal_conv4(x, w):
    y = x * w[3:4, :]
    for j in range(3):
        y = y + _TSHIFT[3 - j](x) * w[j:j + 1, :]
    return y


def _scan_steps(n):
    d = 1
    while d < n:
        yield d
        d *= 2


@jax.custom_vjp
def linscan(a, b):
    return _linscan_fwd_impl(a, b)


def _linscan_fwd_impl(a, b):
    for d in _scan_steps(a.shape[0]):
        b = a * _shift_down(b, d, 0.0) + b
        a = a * _shift_down(a, d, 1.0)
    return b


def _linscan_fwd(a, b):
    h = _linscan_fwd_impl(a, b)
    return h, (a, h)


def _linscan_bwd(res, dh):
    a, h = res
    an = _shift_up(a, 1, 0.0)
    lam = dh
    for d in _scan_steps(a.shape[0]):
        lam = an * _shift_up(lam, d, 0.0) + lam
        an = an * _shift_up(an, d, 1.0)
    return lam * _shift_down(h, 1, 0.0), lam


linscan.defvjp(_linscan_fwd, _linscan_bwd)


def _bmm(a, b, precision=None):
    return jnp.einsum('nij,njk->nik', a, b, precision=precision, preferred_element_type=F32)


@jax.custom_vjp
def inv_unit_lower(l):
    return _inv_impl(l)


def _inv_impl(l):
    n = l.shape[-1]
    eye = (_rows((n, n)) == lax.broadcasted_iota(jnp.int32, (n, n), 1)).astype(F32)
    p = -l
    a = eye + p
    k = 1
    while 2 * k < n:
        p = _bmm(p, p, HI)
        a = a + _bmm(a, p, HI)
        k *= 2
    return a


def _inv_fwd(l):
    a = _inv_impl(l)
    return a, a


def _inv_bwd(a, da):
    at = jnp.swapaxes(a, 1, 2)
    return (-_bmm(_bmm(at, da, HI), at, HI),)


inv_unit_lower.defvjp(_inv_fwd, _inv_bwd)


def neg_expm1(y):
    series = -(y * (1.0 + y * (0.5 + y * (1.0 / 6.0 + y * (1.0 / 24.0)))))
    return jnp.where(y > -0.05, series, 1.0 - jnp.exp(y))


def run_stage(name, fn, *, tok, tok_nd=(), ex=(), par=(), par_tiled=(), big=(), out_ch, ts, nc=1, cot=None, addin=None):
    tok, tok_nd, ex, par, par_tiled, big = map(list, (tok, tok_nd, ex, par, par_tiled, big))
    bsz, seq, _ = tok[0].shape
    ts = min(ts, seq)
    ns = seq // ts
    grid = (nc, bsz, ns)

    def tok_spec(a):
        cb = a.shape[-1] // nc
        return pl.BlockSpec((None, ts, cb), lambda c, b, s: (b, s, c))

    def ex_spec(a):
        cb = a.shape[-1] // nc
        return pl.BlockSpec((None, 1, cb), lambda c, b, s: (b, 0, c))

    def full_spec(a, single=False):
        nd = a.ndim
        kw = dict(pipeline_mode=pl.Buffered(1)) if single else {}
        return pl.BlockSpec(a.shape, lambda c, b, s: (0,) * nd, **kw)

    def tiled_spec(a):
        if a.ndim == 2:
            return pl.BlockSpec((a.shape[0], a.shape[1] // nc), lambda c, b, s: (0, c))
        return pl.BlockSpec((None,) + a.shape[1:], lambda c, b, s: (c, 0, 0))

    n_tok, n_nd, n_ex, n_par, n_pt, n_big = map(len, (tok, tok_nd, ex, par, par_tiled, big))
    in_arrays = tok + tok_nd + ex + par + par_tiled + big
    in_specs = ([tok_spec(a) for a in tok + tok_nd] + [ex_spec(a) for a in ex] + [full_spec(a) for a in par]
                + [tiled_spec(a) for a in par_tiled] + [full_spec(a, True) for a in big])
    out_tok_shapes = [jax.ShapeDtypeStruct((bsz, seq, ch), F32) for ch in out_ch]
    n_in = len(in_arrays)

    def split(vals):
        i = 0
        groups = []
        for n in (n_tok, n_nd, n_ex, n_par, n_pt, n_big):
            groups.append(list(vals[i:i + n]))
            i += n
        return groups

    def split_grads(vals):
        i = 0
        groups = []
        for n in (n_tok, n_ex, n_par, n_pt, n_big):
            groups.append(list(vals[i:i + n]))
            i += n
        return groups

    if cot is None:
        def body(*refs):
            tv, ndv, ev, pv, ptv, bv = split([r[...] for r in refs[:n_in]])
            outs = fn(tv, ndv, ev, pv, ptv, lambda x, i: _dot(x, bv[i]))
            for r, o in zip(refs[n_in:], outs):
                r[...] = o

        return pl.pallas_call(
            body, name=name, grid=grid, in_specs=in_specs, out_specs=[tok_spec(a) for a in out_tok_shapes],
            out_shape=out_tok_shapes, compiler_params=_params(3))(*in_arrays)

    cot = list(cot)
    has_addin = addin is not None
    extra = cot + ([addin] if has_addin else [])
    n_cot = len(cot)
    g_shapes = ([jax.ShapeDtypeStruct(a.shape, F32) for a in tok + ex + par + par_tiled + big])
    g_specs = ([tok_spec(a) for a in tok] + [ex_spec(a) for a in ex] + [full_spec(a) for a in par]
               + [tiled_spec(a) for a in par_tiled] + [full_spec(a, True) for a in big])

    def body(*refs):
        c, b, s = pl.program_id(0), pl.program_id(1), pl.program_id(2)
        tv, ndv, ev, pv, ptv, bv = split([r[...] for r in refs[:n_in]])
        cots = [r[...] for r in refs[n_in:n_in + n_cot]]
        g_refs = refs[n_in + len(extra):]
        probes = [jnp.zeros(w.shape, F32) for w in bv]

        def f(tv_, ev_, pv_, ptv_, probes_):
            return fn(tv_, ndv, ev_, pv_, ptv_, lambda x, i: _mm_probe(x, bv[i], probes_[i]))

        _, vjp = jax.vjp(f, tv, ev, pv, ptv, probes)
        dt, de, dp, dpt, dbg = vjp(cots)
        if has_addin:
            dt = [dt[0] + refs[n_in + n_cot][...]] + list(dt[1:])
        gt_r, ge_r, gp_r, gpt_r, gb_r = split_grads(g_refs)
        for r, g in zip(gt_r, dt):
            r[...] = g

        def accumulate(r, g, first):
            @pl.when(first)
            def _():
                r[...] = g

            @pl.when(jnp.logical_not(first))
            def _():
                r[...] += g

        for r, g in zip(ge_r, de):
            accumulate(r, g, s == 0)
        first_all = jnp.logical_and(jnp.logical_and(c == 0, b == 0), s == 0)
        for r, g in zip(gp_r, dp):
            accumulate(r, g, first_all)
        for r, g in zip(gpt_r, dpt):
            accumulate(r, g, jnp.logical_and(b == 0, s == 0))
        for r, g in zip(gb_r, dbg):
            accumulate(r, g, first_all)

    res = pl.pallas_call(
        body, name=name + "_bwd", grid=grid, in_specs=in_specs + [tok_spec(a) for a in extra], out_specs=g_specs,
        out_shape=g_shapes, compiler_params=_params(3))(*in_arrays, *extra)
    return split_grads(res)


PROJ_PIECES = (("qkv", 768), ("z", 256), ("rx", 512), ("rgate", 512), ("mq", 256), ("mkv", 128), ("misc", 128))
PROJ_WIDTH = sum(w for _, w in PROJ_PIECES)
MISC_KR, MISC_A, MISC_B = 0, 32, 36


def fn_mixer_in(tok, nd, ex, par, pt, mm):
    (h,), (sc, sh), (g,) = tok, ex, par
    proj = mm(rms(h, g) * (1.0 + sc) + sh, 0)
    outs, o = [], 0
    for _, w in PROJ_PIECES:
        outs.append(proj[:, o:o + w])
        o += w
    return outs


def fn_gdn_conv(tok, nd, ex, par, pt, mm):
    return [jax.nn.silu(causal_conv4(tok[0], pt[0]))]


def _tri_masks():
    r = _rows((CHUNK, CHUNK))
    c = lax.broadcasted_iota(jnp.int32, (CHUNK, CHUNK), 1)
    return (c <= r), (c < r)


def fn_gdn_local(tok, nd, ex, par, pt, mm):
    (qkv, misc), (a_log, dt_bias) = tok, par
    ts = qkv.shape[0]
    nb = ts // CHUNK
    lower, strict = _tri_masks()
    tril = jnp.broadcast_to(lower.astype(F32), (nb, CHUNK, CHUNK))
    ones = jnp.ones((nb, CHUNK, CHUNK), F32)
    outs = [[] for _ in range(6)]
    for hd in range(GDN_HEADS):
        def head(x, base):
            return x[:, base + 64 * hd: base + 64 * hd + 64]

        def l2n(x):
            return x * lax.rsqrt(jnp.sum(x * x, axis=-1, keepdims=True) + EPS)

        q = (l2n(head(qkv, 0)) * (64.0 ** -0.5)).reshape(nb, CHUNK, 64)
        k = l2n(head(qkv, 256)).reshape(nb, CHUNK, 64)
        v = head(qkv, 512).reshape(nb, CHUNK, 64)
        a = misc[:, MISC_A + hd: MISC_A + hd + 1]
        b = misc[:, MISC_B + hd: MISC_B + hd + 1]
        g = -jnp.exp(a_log[:, hd:hd + 1]) * jax.nn.softplus(a + dt_bias[:, hd:hd + 1])
        beta = jax.nn.sigmoid(b).reshape(nb, CHUNK, 1)
        gb = jnp.broadcast_to(g.reshape(nb, CHUNK, 1), (nb, CHUNK, CHUNK))
        gi = _bmm(tril, gb, HI)
        gl = _bmm(ones, gb, HI)
        diff = gi - jnp.swapaxes(gi, 1, 2)
        decay = jnp.where(lower, jnp.exp(jnp.where(lower, diff, 0.0)), 0.0)
        kb = k * beta
        vb = v * beta
        kk = jnp.einsum('ncd,nsd->ncs', kb.astype(BF16), k.astype(BF16), preferred_element_type=F32)
        amat = inv_unit_lower(jnp.where(strict, kk * decay, 0.0))
        eg = jnp.exp(gi)
        u = _bmm(amat, vb, HI)
        w = _bmm(amat, kb * eg, HI)
        qk = jnp.einsum('ncd,nsd->ncs', q.astype(BF16), k.astype(BF16), preferred_element_type=F32) * decay
        qd = q * eg
        kt = k * jnp.exp(gl - gi)
        cd = jnp.exp(gl)
        for lst, val in zip(outs, (qk, qd, u, w, kt, cd)):
            lst.append(val.reshape(ts, 64))
    return [jnp.concatenate(lst, axis=-1) for lst in outs]


def fn_rglru(tok, nd, ex, par, pt, mm):
    (rx, rgate), (conv_w, conv_b, b_a, b_x, lam, bd_a, bd_x) = tok, pt
    xc = causal_conv4(rx, conv_w) + conv_b
    r = jax.nn.sigmoid(mmw(xc, bd_a) + b_a)
    i = jax.nn.sigmoid(mmw(xc, bd_x) + b_x)
    log_a = -RG_C * r * jax.nn.softplus(-lam)
    a = jnp.exp(log_a)
    bterm = jnp.sqrt(neg_expm1(2.0 * log_a)) * (i * xc)
    return [linscan(a, bterm) * jax.nn.gelu(rgate)]


def _rope32(x, cos, sin):
    x1, x2 = x[:, :16], x[:, 16:32]
    return jnp.concatenate([x1 * cos - x2 * sin, x2 * cos + x1 * sin], axis=-1)


def fn_mla_pre(tok, nd, ex, par, pt, mm):
    (mq, mkv, misc), (cs,), (g_q, g_kv, w_q, w_kv) = tok, nd, par
    q = mmw(rms(mq, g_q), w_q)
    kv = mmw(rms(mkv, g_kv), w_kv)
    cos, sin = cs[:, 0:16], cs[:, 16:32]
    qp = jnp.concatenate([_rope32(q[:, 256 + 32 * h: 288 + 32 * h], cos, sin) for h in range(MLA_HEADS)], axis=-1)
    kp = _rope32(misc[:, MISC_KR:MISC_KR + 32], cos, sin)
    kp = jnp.concatenate([kp, jnp.zeros((kp.shape[0], 96), F32)], axis=-1)
    return [q[:, 0:256], qp, kv[:, 0:256], kv[:, 256:512], kp]


def fn_out_proj(tok, nd, ex, par, pt, mm):
    (h, o_a, o_b, o_c), (gt,) = tok, ex
    return [h + gt * mm(jnp.concatenate([o_a, o_b, o_c], axis=-1), 0)]


def fn_mlp_in(tok, nd, ex, par, pt, mm):
    (h,), (sc, sh), (g,) = tok, ex, par
    return [mm(rms(h, g) * (1.0 + sc) + sh, 0)]


def fn_mlp_out(tok, nd, ex, par, pt, mm):
    (h, a), (gt,) = tok, ex
    return [h + gt * mm(jnp.square(jax.nn.relu(a)), 0)]


def _gdn_step(states, qk, qd, u, w, kt, cd, z, norm_g):
    new_states, outs = [], []
    for hd in range(GDN_HEADS):
        sl = slice(64 * hd, 64 * hd + 64)
        st = states[hd]
        v_new = u[:, sl] - _dot(w[:, sl], st)
        o = _dot(qd[:, sl], st) + _dot(qk[:, sl], v_new)
        new_states.append(st * cd[:, sl] + _dot(kt[:, sl], v_new, (((0,), (0,)), ((), ()))))
        outs.append(rms(o, norm_g) * jax.nn.silu(z[:, sl]))
    return new_states, jnp.concatenate(outs, axis=-1)


def gdn_scan(xs, z, norm_g):
    bsz, seq, _ = z.shape
    n = seq // CHUNK
    blk = pl.BlockSpec((None, CHUNK, 256), lambda b, i: (b, i, 0))

    def body(qk, qd, u, w, kt, cd, z_ref, g_ref, o_ref, st_out, st):
        @pl.when(pl.program_id(1) == 0)
        def _():
            st[...] = jnp.zeros_like(st)

        states = [st[hd] for hd in range(GDN_HEADS)]
        st_out[...] = jnp.concatenate(states, axis=-1)
        new_states, o = _gdn_step(states, qk[...], qd[...], u[...], w[...], kt[...], cd[...], z_ref[...], g_ref[...])
        o_ref[...] = o
        for hd in range(GDN_HEADS):
            st[hd] = new_states[hd]

    return pl.pallas_call(
        body, name="gdn_scan", grid=(bsz, n), in_specs=[blk] * 7 + [pl.BlockSpec((1, 64), lambda b, i: (0, 0))],
        out_specs=[blk, blk], out_shape=[jax.ShapeDtypeStruct((bsz, seq, 256), F32)] * 2,
        scratch_shapes=[pltpu.VMEM((GDN_HEADS, 64, 64), F32)], compiler_params=_params(2))(*xs, z, norm_g)


def gdn_scan_bwd(xs, z, norm_g, st_in, do):
    bsz, seq, _ = z.shape
    n = seq // CHUNK
    blk = pl.BlockSpec((None, CHUNK, 256), lambda b, i: (b, n - 1 - i, 0))
    gspec = pl.BlockSpec((1, 64), lambda b, i: (0, 0))

    def body(qk, qd, u, w, kt, cd, z_ref, g_ref, st_ref, do_ref, dqk, dqd, du, dw, dkt, dcd, dz, dg, dst):
        first = jnp.logical_and(pl.program_id(0) == 0, pl.program_id(1) == 0)

        @pl.when(pl.program_id(1) == 0)
        def _():
            dst[...] = jnp.zeros_like(dst)

        stv = st_ref[...]
        states = [stv[:, 64 * hd: 64 * hd + 64] for hd in range(GDN_HEADS)]
        _, vjp = jax.vjp(_gdn_step, states, qk[...], qd[...], u[...], w[...], kt[...], cd[...], z_ref[...], g_ref[...])
        grads = vjp(([dst[hd] for hd in range(GDN_HEADS)], do_ref[...]))
        for hd in range(GDN_HEADS):
            dst[hd] = grads[0][hd]
        for r, g in zip((dqk, dqd, du, dw, dkt, dcd, dz), grads[1:8]):
            r[...] = g

        @pl.when(first)
        def _():
            dg[...] = grads[8]

        @pl.when(jnp.logical_not(first))
        def _():
            dg[...] += grads[8]

    res = pl.pallas_call(
        body, name="gdn_scan_bwd", grid=(bsz, n), in_specs=[blk] * 7 + [gspec, blk, blk],
        out_specs=[blk] * 7 + [gspec], out_shape=[jax.ShapeDtypeStruct((bsz, seq, 256), F32)] * 7
        + [jax.ShapeDtypeStruct((1, 64), F32)],
        scratch_shapes=[pltpu.VMEM((GDN_HEADS, 64, 64), F32)], compiler_params=_params(2))(*xs, z, norm_g, st_in, do)
    return list(res[:6]), res[6], res[7]


ATTN_TQ = 256
ATTN_SCALE = 96.0 ** -0.5


def _attn_head(qn, qp, kn, kp, v, q0):
    nt = (((1,), (1,)), ((), ()))
    s = (_dot(qn, kn, nt) + _dot(qp, kp, nt)) * ATTN_SCALE
    qc = (q0 + _rows(s.shape)) // CHUNK
    kc = lax.broadcasted_iota(jnp.int32, s.shape, 1) // CHUNK
    s = jnp.where(kc <= qc, s, -1e30)
    p = jnp.exp(s - jnp.max(s, axis=-1, keepdims=True))
    p = p / jnp.sum(p, axis=-1, keepdims=True)
    return _dot(p, v)


def mla_attention(qn, qp, kn, kp, v):
    bsz, seq, _ = qn.shape
    nq = seq // ATTN_TQ

    def qspec(ch):
        return pl.BlockSpec((None, ATTN_TQ, ch), lambda b, i: (b, i, 0))

    def kspec(ch):
        return pl.BlockSpec((None, seq, ch), lambda b, i: (b, 0, 0))

    def body(qn_r, qp_r, kn_r, kp_r, v_r, o_r):
        q0 = pl.program_id(1) * ATTN_TQ
        kpv = kp_r[...][:, 0:32]
        outs = []
        for h in range(MLA_HEADS):
            sl = slice(64 * h, 64 * h + 64)
            outs.append(_attn_head(qn_r[...][:, sl], qp_r[...][:, 32 * h: 32 * h + 32], kn_r[...][:, sl], kpv,
                                   v_r[...][:, sl], q0))
        o_r[...] = jnp.concatenate(outs, axis=-1)

    return pl.pallas_call(
        body, name="mla_attention", grid=(bsz, nq), in_specs=[qspec(256), qspec(128), kspec(256), kspec(128), kspec(256)],
        out_specs=qspec(256), out_shape=jax.ShapeDtypeStruct((bsz, seq, 256), F32), compiler_params=_params(2))(
            qn, qp, kn, kp, v)


def mla_attention_bwd(qn, qp, kn, kp, v, do):
    bsz, seq, _ = qn.shape
    nq = seq // ATTN_TQ

    def qspec(ch):
        return pl.BlockSpec((None, ATTN_TQ, ch), lambda b, i: (b, i, 0))

    def kspec(ch):
        return pl.BlockSpec((None, seq, ch), lambda b, i: (b, 0, 0))

    def body(qn_r, qp_r, kn_r, kp_r, v_r, do_r, dqn_r, dqp_r, dkn_r, dkp_r, dv_r):
        i = pl.program_id(1)
        q0 = i * ATTN_TQ
        kpv = kp_r[...][:, 0:32]
        dqn, dqp, dkn, dv = [], [], [], []
        dkp = jnp.zeros((seq, 32), F32)
        for h in range(MLA_HEADS):
            sl = slice(64 * h, 64 * h + 64)
            _, vjp = jax.vjp(functools.partial(_attn_head, q0=q0), qn_r[...][:, sl], qp_r[...][:, 32 * h: 32 * h + 32],
                             kn_r[...][:, sl], kpv, v_r[...][:, sl])
            a, b, c, d, e = vjp(do_r[...][:, sl])
            dqn.append(a)
            dqp.append(b)
            dkn.append(c)
            dkp = dkp + d
            dv.append(e)
        dqn_r[...] = jnp.concatenate(dqn, axis=-1)
        dqp_r[...] = jnp.concatenate(dqp, axis=-1)
        gk = jnp.concatenate(dkn, axis=-1)
        gv = jnp.concatenate(dv, axis=-1)
        gkp = jnp.concatenate([dkp, jnp.zeros((seq, 96), F32)], axis=-1)

        @pl.when(i == 0)
        def _():
            dkn_r[...] = gk
            dkp_r[...] = gkp
            dv_r[...] = gv

        @pl.when(i != 0)
        def _():
            dkn_r[...] += gk
            dkp_r[...] += gkp
            dv_r[...] += gv

    shp = lambda ch: jax.ShapeDtypeStruct((bsz, seq, ch), F32)
    return pl.pallas_call(
        body, name="mla_attention_bwd", grid=(bsz, nq),
        in_specs=[qspec(256), qspec(128), kspec(256), kspec(128), kspec(256), qspec(256)],
        out_specs=[qspec(256), qspec(128), kspec(256), kspec(128), kspec(256)],
        out_shape=[shp(256), shp(128), shp(256), shp(128), shp(256)], compiler_params=_params(2))(qn, qp, kn, kp, v, do)


LOSS_TS = 512


def loss_head(h, g, target):
    bsz, seq, d = h.shape
    ts = min(LOSS_TS, seq)
    tok = pl.BlockSpec((None, ts, d), lambda b, s: (b, s, 0))
    gspec = pl.BlockSpec((1, d), lambda b, s: (0, 0))
    lspec = pl.BlockSpec((1, 128), lambda b, s: (0, 0))

    def body(h_r, g_r, t_r, loss_r, dh_r, dg_r):
        first = jnp.logical_and(pl.program_id(0) == 0, pl.program_id(1) == 0)
        tv = t_r[...]

        def f(hv, gv):
            return 0.5 * jnp.sum(jnp.mean(jnp.square(rms(hv, gv) - tv), axis=-1, keepdims=True), axis=0, keepdims=True)

        val, vjp = jax.vjp(f, h_r[...], g_r[...])
        dh, dg = vjp(jnp.ones((1, 1), F32))
        dh_r[...] = dh
        lv = jnp.broadcast_to(val, (1, 128))

        @pl.when(first)
        def _():
            loss_r[...] = lv
            dg_r[...] = dg

        @pl.when(jnp.logical_not(first))
        def _():
            loss_r[...] += lv
            dg_r[...] += dg

    return pl.pallas_call(
        body, name="loss_head", grid=(bsz, seq // ts), in_specs=[tok, gspec, tok], out_specs=[lspec, tok, gspec],
        out_shape=[jax.ShapeDtypeStruct((1, 128), F32), jax.ShapeDtypeStruct(h.shape, F32), jax.ShapeDtypeStruct((1, d), F32)],
        compiler_params=_params(2))(h, g, target)


def _adamw_math(w, g, m, v):
    m = ADAM_B1 * m + (1.0 - ADAM_B1) * g
    v = ADAM_B2 * v + (1.0 - ADAM_B2) * jnp.square(g)
    m_hat = m / (1.0 - ADAM_B1 ** ADAM_STEP)
    v_hat = v / (1.0 - ADAM_B2 ** ADAM_STEP)
    return -ADAM_LR * (m_hat / (jnp.sqrt(v_hat) + ADAM_EPS) + ADAM_WD * w), m, v


def _row_block(rows, cols):
    want = max(8, (1 << 18) // cols)
    best = rows
    for r in range(8, rows + 1, 8):
        if rows % r == 0 and r <= want:
            best = r
    return best if rows % 8 == 0 else rows


def adamw(name, w, g, m, v):
    rows, cols = w.shape
    rb = _row_block(rows, cols)
    spec = pl.BlockSpec((rb, cols), lambda i: (i, 0))

    def body(w_r, g_r, m_r, v_r, d_o, m_o, v_o):
        d, mn, vn = _adamw_math(w_r[...], g_r[...], m_r[...], v_r[...])
        d_o[...] = d
        m_o[...] = mn
        v_o[...] = vn

    return pl.pallas_call(body, name=name, grid=(rows // rb,), in_specs=[spec] * 4, out_specs=[spec] * 3,
                          out_shape=[jax.ShapeDtypeStruct(w.shape, F32)] * 3, compiler_params=_params(1))(w, g, m, v)


def adamw_reduce(name, parts, w, m, v):
    rows, cols = w.shape
    rb = _row_block(rows, cols)
    spec = pl.BlockSpec((rb, cols), lambda i: (i, 0))
    pspec = pl.BlockSpec((N_DEV, rb, cols), lambda i: (0, i, 0))

    def body(p_r, w_r, m_r, v_r, g_o, d_o, m_o, v_o):
        g = p_r[0]
        for k in range(1, N_DEV):
            g = g + p_r[k]
        d, mn, vn = _adamw_math(w_r[...], g, m_r[...], v_r[...])
        g_o[...] = g
        d_o[...] = d
        m_o[...] = mn
        v_o[...] = vn

    return pl.pallas_call(body, name=name, grid=(rows // rb,), in_specs=[pspec, spec, spec, spec], out_specs=[spec] * 4,
                          out_shape=[jax.ShapeDtypeStruct(w.shape, F32)] * 4, compiler_params=_params(1))(parts, w, m, v)


MOD_CB = 512


def mod_matmul(c_rows, w_mod, b_mod):
    nl, d, cols = w_mod.shape

    def body(c_r, w_r, b_r, o_r):
        o_r[...] = _dot(jax.nn.silu(c_r[...]), w_r[...]) + b_r[...]

    return pl.pallas_call(
        body, name="mod_matmul", grid=(nl, cols // MOD_CB),
        in_specs=[pl.BlockSpec((8, d), lambda l, j: (0, 0)), pl.BlockSpec((None, d, MOD_CB), lambda l, j: (l, 0, j)),
                  pl.BlockSpec((None, 1, MOD_CB), lambda l, j: (l, 0, j))],
        out_specs=pl.BlockSpec((None, 8, MOD_CB), lambda l, j: (l, 0, j)),
        out_shape=jax.ShapeDtypeStruct((nl, 8, cols), F32), compiler_params=_params(2))(c_rows, w_mod, b_mod)


def mod_weight_grad(c_all, dmod):
    nl, nb, cols = dmod.shape
    d = c_all.shape[1]

    def body(c_r, g_r, o_r):
        o_r[...] = _dot(jax.nn.silu(c_r[...]), g_r[...], (((0,), (0,)), ((), ())))

    return pl.pallas_call(
        body, name="mod_weight_grad", grid=(nl, cols // MOD_CB),
        in_specs=[pl.BlockSpec((nb, d), lambda l, j: (0, 0)), pl.BlockSpec((None, nb, MOD_CB), lambda l, j: (l, 0, j))],
        out_specs=pl.BlockSpec((None, d, MOD_CB), lambda l, j: (l, 0, j)),
        out_shape=jax.ShapeDtypeStruct((nl, d, cols), F32), compiler_params=_params(2))(c_all, dmod)


def add_pair(name, a, b):
    rows, cols = a.shape
    rb = _row_block(rows, cols)
    spec = pl.BlockSpec((rb, cols), lambda i: (i, 0))

    def body(a_r, b_r, o_r, ob_r):
        s = a_r[...] + b_r[...]
        o_r[...] = s
        ob_r[...] = s.astype(BF16)

    return pl.pallas_call(body, name=name, grid=(rows // rb,), in_specs=[spec, spec], out_specs=[spec, spec],
                          out_shape=[jax.ShapeDtypeStruct(a.shape, F32), jax.ShapeDtypeStruct(a.shape, BF16)],
                          compiler_params=_params(1))(a, b)


def sum_chips(name, own, recv, skip):
    rows, cols = own.shape
    rb = _row_block(rows, cols)
    spec = pl.BlockSpec((rb, cols), lambda i, s: (i, 0))
    rspec = pl.BlockSpec((N_CHIP, rb, cols), lambda i, s: (0, i, 0))

    def body(s_r, o_r, r_r, out_r):
        acc = o_r[...]
        for k in range(N_CHIP):
            acc = acc + jnp.where(s_r[0] == k, 0.0, r_r[k].astype(F32))
        out_r[...] = acc

    return pl.pallas_call(
        body, name=name, grid_spec=pltpu.PrefetchScalarGridSpec(num_scalar_prefetch=1, grid=(rows // rb,),
                                                                in_specs=[spec, rspec], out_specs=spec),
        out_shape=jax.ShapeDtypeStruct(own.shape, F32), compiler_params=_params(1))(skip, own, recv)


def _me():
    return lax.axis_index("x"), lax.axis_index("y"), lax.axis_index("c")


def all_gather8(name, x_shard, in_vmem):
    m_per, n = x_shard.shape
    space = pltpu.VMEM if in_vmem else pl.ANY

    def body(x_ref, out_ref, send_sems, recv_sems, local_sem):
        x, y, c = _me()
        me, sibling = (x, y, c), (x, y, 1 - c)
        chips = [(1 - x, y), (x, 1 - y), (1 - x, 1 - y)]

        def rows(px, py, pc):
            return out_ref.at[pl.ds((4 * px + 2 * py + pc) * m_per, m_per), :]

        def copy(k, block, to, src=None):
            return pltpu.make_async_remote_copy(
                src_ref=rows(*block) if src is None else src, dst_ref=rows(*block), send_sem=send_sems.at[k],
                recv_sem=recv_sems.at[k], device_id=to, device_id_type=MESH)

        mine = pltpu.make_async_copy(x_ref, rows(*me), local_sem)
        mine.start()
        first = [copy(0, me, sibling, src=x_ref)]
        first += [copy(1 + j, me, (*chip, c), src=x_ref) for j, chip in enumerate(chips)]
        for cp in first:
            cp.start()
        passed = [copy(4 + j, (*chip, c), sibling) for j, chip in enumerate(chips)]
        for j, chip in enumerate(chips):
            copy(1 + j, (*chip, c), me).wait_recv()
            passed[j].start()
        copy(0, sibling, me).wait_recv()
        for j, chip in enumerate(chips):
            copy(4 + j, (*chip, 1 - c), me).wait_recv()
        for cp in first + passed:
            cp.wait_send()
        mine.wait()

    return pl.pallas_call(
        body, name=name, out_shape=jax.ShapeDtypeStruct((N_DEV * m_per, n), x_shard.dtype),
        in_specs=[pl.BlockSpec(memory_space=space)], out_specs=pl.BlockSpec(memory_space=space),
        scratch_shapes=[pltpu.SemaphoreType.DMA((7,)), pltpu.SemaphoreType.DMA((7,)), pltpu.SemaphoreType.DMA],
    )(x_shard)


def sibling_swap(name, x):
    def body(x_ref, out_ref, send_sem, recv_sem):
        mx, my, mc = _me()
        cp = pltpu.make_async_remote_copy(src_ref=x_ref, dst_ref=out_ref, send_sem=send_sem, recv_sem=recv_sem,
                                          device_id=(mx, my, 1 - mc), device_id_type=MESH)
        cp.start()
        cp.wait()

    return pl.pallas_call(
        body, name=name, out_shape=jax.ShapeDtypeStruct(x.shape, x.dtype), in_specs=[pl.BlockSpec(memory_space=pl.ANY)],
        out_specs=pl.BlockSpec(memory_space=pl.ANY), scratch_shapes=[pltpu.SemaphoreType.DMA, pltpu.SemaphoreType.DMA],
    )(x)


def chip_all_to_all(name, p):
    def body(p_ref, r_ref, send_sems, recv_sems, local_sem):
        mx, my, mc = _me()
        ci = 2 * mx + my
        chips = [(1 - mx, my), (mx, 1 - my), (1 - mx, 1 - my)]
        own = pltpu.make_async_copy(p_ref.at[ci], r_ref.at[ci], local_sem)
        own.start()
        sends = []
        for k, (px, py) in enumerate(chips):
            sends.append(pltpu.make_async_remote_copy(
                src_ref=p_ref.at[2 * px + py], dst_ref=r_ref.at[ci], send_sem=send_sems.at[k], recv_sem=recv_sems.at[k],
                device_id=(px, py, mc), device_id_type=MESH))
        for cp in sends:
            cp.start()
        for k, (px, py) in enumerate(chips):
            pltpu.make_async_remote_copy(
                src_ref=p_ref.at[ci], dst_ref=r_ref.at[2 * px + py], send_sem=send_sems.at[k], recv_sem=recv_sems.at[k],
                device_id=(px, py, mc), device_id_type=MESH).wait_recv()
        for cp in sends:
            cp.wait_send()
        own.wait()

    return pl.pallas_call(
        body, name=name, out_shape=jax.ShapeDtypeStruct(p.shape, p.dtype), in_specs=[pl.BlockSpec(memory_space=pl.ANY)],
        out_specs=pl.BlockSpec(memory_space=pl.ANY),
        scratch_shapes=[pltpu.SemaphoreType.DMA((3,)), pltpu.SemaphoreType.DMA((3,)), pltpu.SemaphoreType.DMA],
    )(p)


WEIGHTS = ['w_mod', 'b_mod', 'norm_mix_g', 'w_in', 'gdn_conv_w', 'gdn_a_log', 'gdn_dt_bias', 'gdn_norm_g', 'rg_conv_w',
           'rg_conv_b', 'rg_w_a', 'rg_b_a', 'rg_w_x', 'rg_b_x', 'rg_lambda', 'mla_q_norm_g', 'mla_w_qb', 'mla_kv_norm_g',
           'mla_w_kvb', 'w_out', 'norm_mlp_g', 'w_mlp_in', 'w_mlp_out', 'final_norm_g']
SHARDED = {'w_in': 2, 'gdn_conv_w': 2, 'rg_conv_w': 2, 'mla_w_qb': 2, 'mla_w_kvb': 2, 'w_out': 1, 'w_mlp_in': 2, 'w_mlp_out': 1}
GATHER_BF16 = ('w_in', 'mla_w_qb', 'mla_w_kvb', 'w_out', 'w_mlp_in', 'w_mlp_out')
REPLICATED = [n for n in WEIGHTS if n not in SHARDED and n != 'w_mod']
PACK_COLS = 1024


def _pack(arrays, multiple):
    flat = jnp.concatenate([a.reshape(-1) for a in arrays])
    pad = (-flat.shape[0]) % multiple
    return jnp.pad(flat, (0, pad)) if pad else flat


def _unpack(flat, shapes):
    out, o = [], 0
    for shp in shapes:
        n = int(np.prod(shp))
        out.append(flat[o:o + n].reshape(shp))
        o += n
    return out


def _unshard(stacked, axis):
    moved = jnp.moveaxis(stacked, 0, axis)
    shp = list(moved.shape)
    shp[axis:axis + 2] = [shp[axis] * shp[axis + 1]]
    return moved.reshape(shp)


def _shard(full, axis):
    shp = list(full.shape)
    shp[axis:axis + 1] = [N_CHIP, shp[axis] // N_CHIP]
    return jnp.moveaxis(full.reshape(shp), axis, 0)


def _proj_cols(w):
    pad = jnp.zeros(w.shape[:-1] + (PROJ_WIDTH - w.shape[-1],), w.dtype)
    return jnp.concatenate([w[..., 0:1024], w[..., 1032:2472], w[..., 1024:1032], pad], axis=-1)


def _proj_cols_back(d):
    return jnp.concatenate([d[..., 0:1024], d[..., 2464:2472], d[..., 1024:2464]], axis=-1)


def _heads_split(w, heads, first):
    per = w.shape[-1] // heads
    r = w.reshape(w.shape[:-1] + (heads, per))
    lead = w.shape[:-1]
    return jnp.concatenate([r[..., :first].reshape(lead + (heads * first,)),
                            r[..., first:].reshape(lead + (heads * (per - first),))], axis=-1)


def _heads_merge(d, heads, first):
    lead = d.shape[:-1]
    per = d.shape[-1] // heads
    a = d[..., :heads * first].reshape(lead + (heads, first))
    b = d[..., heads * first:].reshape(lead + (heads, per - first))
    return jnp.concatenate([a, b], axis=-1).reshape(lead + (heads * per,))


def _block_diag(w):
    nl = w.shape[0]
    eye = jnp.eye(2, dtype=w.dtype)
    return jnp.einsum('lcoij,op->lcoipj', w.reshape(nl, 4, 2, 64, 64), eye).reshape(nl, 4, 128, 128)


def _block_diag_back(g):
    nl = g.shape[0]
    return jnp.einsum('lcoipj,op->lcoij', g.reshape(nl, 4, 2, 64, 2, 64), jnp.eye(2, dtype=g.dtype)).reshape(nl, 8, 64, 64)


def kernel(x, c, positions, w_mod, b_mod, norm_mix_g, w_in, gdn_conv_w, gdn_a_log, gdn_dt_bias, gdn_norm_g, rg_conv_w, rg_conv_b, rg_w_a, rg_b_a, rg_w_x, rg_b_x, rg_lambda, mla_q_norm_g, mla_w_qb, mla_kv_norm_g, mla_w_kvb, w_out, norm_mlp_g, w_mlp_in, w_mlp_out, final_norm_g, loss_target, m_w_mod, m_b_mod, m_norm_mix_g, m_w_in, m_gdn_conv_w, m_gdn_a_log, m_gdn_dt_bias, m_gdn_norm_g, m_rg_conv_w, m_rg_conv_b, m_rg_w_a, m_rg_b_a, m_rg_w_x, m_rg_b_x, m_rg_lambda, m_mla_q_norm_g, m_mla_w_qb, m_mla_kv_norm_g, m_mla_w_kvb, m_w_out, m_norm_mlp_g, m_w_mlp_in, m_w_mlp_out, m_final_norm_g, v_w_mod, v_b_mod, v_norm_mix_g, v_w_in, v_gdn_conv_w, v_gdn_a_log, v_gdn_dt_bias, v_gdn_norm_g, v_rg_conv_w, v_rg_conv_b, v_rg_w_a, v_rg_b_a, v_rg_w_x, v_rg_b_x, v_rg_lambda, v_mla_q_norm_g, v_mla_w_qb, v_mla_kv_norm_g, v_mla_w_kvb, v_w_out, v_norm_mlp_g, v_w_mlp_in, v_w_mlp_out, v_final_norm_g):
    given = dict(locals())
    wts = {n: given[n] for n in WEIGHTS}
    mom_m = {n: given["m_" + n] for n in WEIGHTS}
    mom_v = {n: given["v_" + n] for n in WEIGHTS}
    bsz, seq, d = x.shape
    depth = w_mod.shape[0]
    mx, my, mc = lax.axis_index("x"), lax.axis_index("y"), lax.axis_index("c")
    chip = 2 * mx + my
    dev = 2 * chip + mc

    conv_shapes = [wts['gdn_conv_w'].shape, wts['rg_conv_w'].shape]
    conv_flat = _pack([wts['gdn_conv_w'], wts['rg_conv_w']], d)
    conv_rows = conv_flat.shape[0] // d
    assert bsz + conv_rows <= 8
    c_pad = jnp.concatenate([c, conv_flat.reshape(conv_rows, d), jnp.zeros((8 - bsz - conv_rows, d), F32)], axis=0)
    gath = all_gather8("gather_c", c_pad, True).reshape(N_DEV, 8, d)
    c_all = gath[:, :bsz].reshape(N_DEV * bsz, d)
    conv_all = gath[0::2, bsz:bsz + conv_rows].reshape(N_CHIP, conv_rows * d)
    gdn_conv_full, rg_conv_full = [
        _unshard(jnp.stack([_unpack(conv_all[s], conv_shapes)[i] for s in range(N_CHIP)]), 2) for i in range(2)]

    n_half = N_DEV * bsz // 2
    mod_cols = w_mod.shape[2]
    c_rows = lax.dynamic_slice(c_all, (n_half * mc, 0), (n_half, d))
    b_mod_mine = lax.dynamic_slice(b_mod, (0, chip * mod_cols), (depth, mod_cols)).reshape(depth, 1, mod_cols)
    mod_piece = mod_matmul(c_rows, w_mod, b_mod_mine)
    mod_g = all_gather8("gather_mod", mod_piece.reshape(depth * n_half, mod_cols), True)
    mod_all = mod_g.reshape(N_CHIP, 2, depth, n_half, mod_cols).transpose(2, 1, 3, 0, 4).reshape(depth, 2 * n_half, 6 * d)
    mod_mine = lax.dynamic_slice(mod_all, (0, bsz * dev, 0), (depth, bsz, 6 * d)).reshape(depth, bsz, 6, 1, d)

    shard_shapes = [wts[n].shape for n in GATHER_BF16]
    flat = _pack([wts[n].astype(BF16) for n in GATHER_BF16], 2 * 16 * PACK_COLS)
    n_pad = flat.shape[0]
    half_rows = n_pad // 2 // PACK_COLS
    my_half = lax.dynamic_slice(flat.reshape(2, half_rows, PACK_COLS), (mc, 0, 0), (1, half_rows, PACK_COLS))[0]
    gathered = all_gather8("gather_weights", my_half, False).reshape(N_CHIP, n_pad)
    full = {}
    for i, n in enumerate(GATHER_BF16):
        full[n] = _unshard(jnp.stack([_unpack(gathered[s], shard_shapes)[i] for s in range(N_CHIP)]), SHARDED[n])
    w_cat = _proj_cols(full['w_in'])
    w_q = _heads_split(full['mla_w_qb'].astype(F32), MLA_HEADS, 64)
    w_kv = _heads_split(full['mla_w_kvb'].astype(F32), MLA_HEADS, 64)
    bd_a, bd_x = _block_diag(rg_w_a), _block_diag(rg_w_x)

    inv_freq = ROPE_THETA ** (-jnp.arange(0, 32, 2, dtype=F32) / 32.0)
    ang = positions.astype(F32)[..., None] * inv_freq
    cs = jnp.concatenate([jnp.cos(ang), jnp.sin(ang)], axis=-1)

    proj_ch = [w for _, w in PROJ_PIECES]

    def row(a, l):
        return a[l].reshape(1, -1)

    def layer_args(l):
        sh_m, sc_m, gt_m, sh_f, sc_f, gt_f = (mod_mine[l, :, k] for k in range(6))
        return dict(
            mods=(sh_m, sc_m, gt_m, sh_f, sc_f, gt_f),
            mixer_in=dict(ex=[sc_m, sh_m], par=[row(norm_mix_g, l)], big=[w_cat[l]], out_ch=proj_ch, ts=512),
            gdn_conv=dict(par_tiled=[gdn_conv_full[l]], out_ch=[768], ts=seq, nc=3),
            gdn_local=dict(par=[row(gdn_a_log, l), row(gdn_dt_bias, l)], out_ch=[256] * 6, ts=512),
            rglru=dict(par_tiled=[rg_conv_full[l], row(rg_conv_b, l), row(rg_b_a, l), row(rg_b_x, l), row(rg_lambda, l),
                                  bd_a[l], bd_x[l]], out_ch=[512], ts=seq, nc=4),
            mla_pre=dict(tok_nd=[cs], par=[row(mla_q_norm_g, l), row(mla_kv_norm_g, l), w_q[l], w_kv[l]],
                         out_ch=[256, 128, 256, 256, 128], ts=512),
            out_proj=dict(ex=[gt_m], big=[full['w_out'][l]], out_ch=[d], ts=512),
            mlp_in=dict(ex=[sc_f, sh_f], par=[row(norm_mlp_g, l)], big=[full['w_mlp_in'][l]], out_ch=[4 * d], ts=256),
            mlp_out=dict(ex=[gt_f], big=[full['w_mlp_out'][l]], out_ch=[d], ts=128),
        )

    saved = []
    h = x
    for l in range(depth):
        a = layer_args(l)
        sfx = str(l)
        qkv_raw, z, rx, rgate, mq, mkv, misc = run_stage("mixer_in" + sfx, fn_mixer_in, tok=[h], **a['mixer_in'])
        (qkv_act,) = run_stage("gdn_conv" + sfx, fn_gdn_conv, tok=[qkv_raw], **a['gdn_conv'])
        xs = run_stage("gdn_local" + sfx, fn_gdn_local, tok=[qkv_act, misc], **a['gdn_local'])
        o_a, st_in = gdn_scan(xs, z, row(gdn_norm_g, l))
        (o_b,) = run_stage("rglru" + sfx, fn_rglru, tok=[rx, rgate], **a['rglru'])
        qn, qp, kn, vv, kp = run_stage("mla_pre" + sfx, fn_mla_pre, tok=[mq, mkv, misc], **a['mla_pre'])
        o_c = mla_attention(qn, qp, kn, kp, vv)
        (h_mid,) = run_stage("out_proj" + sfx, fn_out_proj, tok=[h, o_a, o_b, o_c], **a['out_proj'])
        (a_mlp,) = run_stage("mlp_in" + sfx, fn_mlp_in, tok=[h_mid], **a['mlp_in'])
        (h_out,) = run_stage("mlp_out" + sfx, fn_mlp_out, tok=[h_mid, a_mlp], **a['mlp_out'])
        saved.append(dict(h=h, qkv_raw=qkv_raw, z=z, rx=rx, rgate=rgate, mq=mq, mkv=mkv, misc=misc, qkv_act=qkv_act, xs=xs,
                          st_in=st_in, o_a=o_a, o_b=o_b, o_c=o_c, qn=qn, qp=qp, kn=kn, vv=vv, kp=kp, h_mid=h_mid, a_mlp=a_mlp))
        h = h_out

    loss_part, dh, d_final_g = loss_head(h, final_norm_g.reshape(1, d), loss_target)
    loss = lax.psum(loss_part[0, 0], ("x", "y", "c"))

    g_full = {n: [None] * depth for n in SHARDED}
    g_rep = {n: [None] * depth for n in REPLICATED if n not in ('final_norm_g', 'b_mod')}
    dmod = [None] * depth
    for l in reversed(range(depth)):
        a, sv = layer_args(l), saved[l]
        sfx = str(l)
        (dh_mid, da_mlp), (dgt_f,), _, _, (g_full['w_mlp_out'][l],) = run_stage(
            "mlp_out" + sfx, fn_mlp_out, tok=[sv['h_mid'], sv['a_mlp']], cot=[dh], **a['mlp_out'])
        (dh_mid,), (dsc_f, dsh_f), (g_rep['norm_mlp_g'][l],), _, (g_full['w_mlp_in'][l],) = run_stage(
            "mlp_in" + sfx, fn_mlp_in, tok=[sv['h_mid']], cot=[da_mlp], addin=dh_mid, **a['mlp_in'])
        (dh_in, do_a, do_b, do_c), (dgt_m,), _, _, (g_full['w_out'][l],) = run_stage(
            "out_proj" + sfx, fn_out_proj, tok=[sv['h'], sv['o_a'], sv['o_b'], sv['o_c']], cot=[dh_mid], **a['out_proj'])
        dqn, dqp, dkn, dkp, dvv = mla_attention_bwd(sv['qn'], sv['qp'], sv['kn'], sv['kp'], sv['vv'], do_c)
        (dmq, dmkv, dmisc_c), _, (g_rep['mla_q_norm_g'][l], g_rep['mla_kv_norm_g'][l], dw_q, dw_kv), _, _ = run_stage(
            "mla_pre" + sfx, fn_mla_pre, tok=[sv['mq'], sv['mkv'], sv['misc']], cot=[dqn, dqp, dkn, dvv, dkp], **a['mla_pre'])
        g_full['mla_w_qb'][l] = _heads_merge(dw_q, MLA_HEADS, 64)
        g_full['mla_w_kvb'][l] = _heads_merge(dw_kv, MLA_HEADS, 64)
        (drx, drgate), _, _, rg_g, _ = run_stage("rglru" + sfx, fn_rglru, tok=[sv['rx'], sv['rgate']], cot=[do_b], **a['rglru'])
        (g_full['rg_conv_w'][l], g_rep['rg_conv_b'][l], g_rep['rg_b_a'][l], g_rep['rg_b_x'][l], g_rep['rg_lambda'][l],
         g_rep['rg_w_a'][l], g_rep['rg_w_x'][l]) = rg_g
        dxs, dz, g_rep['gdn_norm_g'][l] = gdn_scan_bwd(sv['xs'], sv['z'], row(gdn_norm_g, l), sv['st_in'], do_a)
        (dqkv_act, dmisc_a), _, (g_rep['gdn_a_log'][l], g_rep['gdn_dt_bias'][l]), _, _ = run_stage(
            "gdn_local" + sfx, fn_gdn_local, tok=[sv['qkv_act'], sv['misc']], cot=dxs, **a['gdn_local'])
        (dqkv_raw,), _, _, (g_full['gdn_conv_w'][l],), _ = run_stage(
            "gdn_conv" + sfx, fn_gdn_conv, tok=[sv['qkv_raw']], cot=[dqkv_act], **a['gdn_conv'])
        (dh,), (dsc_m, dsh_m), (g_rep['norm_mix_g'][l],), _, (dw_cat,) = run_stage(
            "mixer_in" + sfx, fn_mixer_in, tok=[sv['h']], cot=[dqkv_raw, dz, drx, drgate, dmq, dmkv, dmisc_a + dmisc_c],
            addin=dh_in, **a['mixer_in'])
        g_full['w_in'][l] = _proj_cols_back(dw_cat)
        dmod[l] = jnp.concatenate([dsh_m, dsc_m, dgt_m, dsh_f, dsc_f, dgt_f], axis=-1).reshape(bsz, 6 * d)
    grad_x = dh

    dmod = jnp.stack(dmod)
    dmod_pad = jnp.concatenate([dmod.reshape(depth * bsz, 6 * d), jnp.zeros((8 - depth * bsz, 6 * d), F32)], axis=0)
    dmod_all = all_gather8("gather_dmod", dmod_pad, True).reshape(N_DEV, 8, 6 * d)[:, :depth * bsz]
    dmod_all = dmod_all.reshape(N_DEV, depth, bsz, 6 * d).transpose(1, 0, 2, 3).reshape(depth, N_DEV * bsz, 6 * d)
    g_w_mod = mod_weight_grad(c_all, lax.dynamic_slice(dmod_all, (0, 0, chip * mod_cols), (depth, N_DEV * bsz, mod_cols)))

    g_rep = {n: jnp.stack(v) for n, v in g_rep.items()}
    g_rep['rg_w_a'] = _block_diag_back(g_rep['rg_w_a'])
    g_rep['rg_w_x'] = _block_diag_back(g_rep['rg_w_x'])
    g_rep['final_norm_g'] = d_final_g
    g_rep['b_mod'] = jnp.sum(dmod, axis=1)
    rep_shapes = [wts[n].shape for n in REPLICATED]
    rep_mult = 8 * PACK_COLS
    rep_part = _pack([g_rep[n].reshape(wts[n].shape) for n in REPLICATED], rep_mult).reshape(-1, PACK_COLS)
    rep_rows = rep_part.shape[0]
    rep_all = all_gather8("gather_small_grads", rep_part, True).reshape(N_DEV, rep_rows, PACK_COLS)
    rep_out = adamw_reduce("adamw_small", rep_all, *[
        _pack([src[n] for n in REPLICATED], rep_mult).reshape(-1, PACK_COLS) for src in (wts, mom_m, mom_v)])
    rep_g, rep_d, rep_m, rep_v = [dict(zip(REPLICATED, _unpack(o.reshape(-1), rep_shapes))) for o in rep_out]

    sh_names = list(SHARDED)
    sh_shapes = [wts[n].shape for n in sh_names]
    g_sh = jnp.stack([_pack([_shard(jnp.stack(g_full[n]).astype(F32), SHARDED[n])[s] for n in sh_names], 2 * 16 * PACK_COLS)
                      for s in range(N_CHIP)])
    n_sh = g_sh.shape[1]
    hrows = n_sh // 2 // PACK_COLS
    g_sh = g_sh.reshape(N_CHIP, 2, hrows, PACK_COLS)
    keep = lax.dynamic_slice(g_sh, (0, mc, 0, 0), (N_CHIP, 1, hrows, PACK_COLS)).reshape(N_CHIP * hrows, PACK_COLS)
    send = lax.dynamic_slice(g_sh, (0, 1 - mc, 0, 0), (N_CHIP, 1, hrows, PACK_COLS)).reshape(N_CHIP * hrows, PACK_COLS)
    from_sibling = sibling_swap("grad_sibling_swap", send)
    pair32, pair16 = add_pair("grad_sibling_add", keep, from_sibling)
    from_chips = chip_all_to_all("grad_chip_exchange", pair16.reshape(N_CHIP, hrows, PACK_COLS))
    own = lax.dynamic_slice(pair32.reshape(N_CHIP, hrows, PACK_COLS), (chip, 0, 0), (1, hrows, PACK_COLS))[0]
    reduced = sum_chips("grad_chip_sum", own, from_chips, chip.reshape(1).astype(jnp.int32))
    other = sibling_swap("grad_half_swap", reduced)
    halves = jnp.stack([reduced, other])
    g_shard_flat = jnp.where(mc == 0, halves, halves[::-1]).reshape(n_sh)
    sh_g = dict(zip(sh_names, _unpack(g_shard_flat, sh_shapes)))

    def as2d(t):
        return t.reshape(-1, t.shape[-1])

    sh_d, sh_m, sh_v = {}, {}, {}
    for n in sh_names + ['w_mod']:
        g = g_w_mod if n == 'w_mod' else sh_g[n]
        res = adamw("adamw_" + n, as2d(wts[n]), as2d(g), as2d(mom_m[n]), as2d(mom_v[n]))
        sh_d[n], sh_m[n], sh_v[n] = (r.reshape(wts[n].shape) for r in res)
    sh_g['w_mod'] = g_w_mod

    def pick(shd, rep):
        return [shd[n] if n in shd else rep[n] for n in WEIGHTS]

    return (loss, grad_x, *pick(sh_g, rep_g), *pick(sh_d, rep_d), *pick(sh_m, rep_m), *pick(sh_v, rep_v))
```

```python
import functools

import jax
import jax.numpy as jnp
import numpy as np
from jax import lax
from jax.experimental import pallas as pl
from jax.experimental.pallas import tpu as pltpu

F32, BF16 = jnp.float32, jnp.bfloat16
HI = lax.Precision.HIGHEST
MESH = pl.DeviceIdType.MESH

EPS = 1e-6
CHUNK = 64
GDN_HEADS = 4
MLA_HEADS = 4
RG_C = 8.0
ROPE_THETA = 10000.0
N_DEV = 8
N_CHIP = 4
V7X_VMEM_LIMIT = 60 * 1024 * 1024
ADAM_LR, ADAM_B1, ADAM_B2, ADAM_EPS, ADAM_WD, ADAM_STEP = 0.001, 0.9, 0.999, 1e-08, 0.01, 10


def _params(n_grid):
    return pltpu.CompilerParams(dimension_semantics=("arbitrary",) * n_grid, vmem_limit_bytes=V7X_VMEM_LIMIT)


def _dot(a, b, dims=(((1,), (0,)), ((), ()))):
    return lax.dot_general(a.astype(BF16), b.astype(BF16), dims, preferred_element_type=F32)


@jax.custom_vjp
def _mm_probe(x, w, probe):
    return _dot(x, w)


def _mm_probe_fwd(x, w, probe):
    return _dot(x, w), (x, w)


def _mm_probe_bwd(res, dy):
    x, w = res
    dx = _dot(dy, w, (((1,), (1,)), ((), ())))
    dw = _dot(x, dy, (((0,), (0,)), ((), ())))
    return dx, jnp.zeros_like(w), dw


_mm_probe.defvjp(_mm_probe_fwd, _mm_probe_bwd)


@jax.custom_vjp
def _probe_only(x, probe):
    return jnp.zeros((x.shape[0], probe.shape[1]), F32)


def _probe_only_fwd(x, probe):
    return jnp.zeros((x.shape[0], probe.shape[1]), F32), x


def _probe_only_bwd(x, dy):
    return jnp.zeros_like(x), _dot(x, dy, (((0,), (0,)), ((), ())))


_probe_only.defvjp(_probe_only_fwd, _probe_only_bwd)


@jax.custom_vjp
def mmw(x, w):
    return _dot(x, w)


def _mmw_fwd(x, w):
    return _dot(x, w), (x, w)


def _mmw_bwd(res, dy):
    x, w = res
    return _dot(dy, w, (((1,), (1,)), ((), ()))), _dot(x, dy, (((0,), (0,)), ((), ())))


mmw.defvjp(_mmw_fwd, _mmw_bwd)


def rms(x, g):
    return x * lax.rsqrt(jnp.mean(x * x, axis=-1, keepdims=True) + EPS) * g


def _rows(shape):
    return lax.broadcasted_iota(jnp.int32, shape, 0)


def _shift_down(x, s, fill):
    return jnp.where(_rows(x.shape) < s, fill, pltpu.roll(x, s, 0))


def _shift_up(x, s, fill):
    n = x.shape[0]
    return jnp.where(_rows(x.shape) >= n - s, fill, pltpu.roll(x, n - s, 0))


def _make_tshift(s):
    @jax.custom_vjp
    def tshift(x):
        return _shift_down(x, s, 0.0)

    tshift.defvjp(lambda x: (_shift_down(x, s, 0.0), None), lambda _, dy: (_shift_up(dy, s, 0.0),))
    return tshift


_TSHIFT = {s: _make_tshift(s) for s in (1, 2, 3)}


def causal_conv4(x, w):
    y = x * w[3:4, :]
    for j in range(3):
        y = y + _TSHIFT[3 - j](x) * w[j:j + 1, :]
    return y


def _scan_steps(n):
    d = 1
    while d < n:
        yield d
        d *= 2


@jax.custom_vjp
def linscan(a, b):
    return _linscan_fwd_impl(a, b)


def _linscan_fwd_impl(a, b):
    for d in _scan_steps(a.shape[0]):
        b = a * _shift_down(b, d, 0.0) + b
        a = a * _shift_down(a, d, 1.0)
    return b


def _linscan_fwd(a, b):
    h = _linscan_fwd_impl(a, b)
    return h, (a, h)


def _linscan_bwd(res, dh):
    a, h = res
    an = _shift_up(a, 1, 0.0)
    lam = dh
    for d in _scan_steps(a.shape[0]):
        lam = an * _shift_up(lam, d, 0.0) + lam
        an = an * _shift_up(an, d, 1.0)
    return lam * _shift_down(h, 1, 0.0), lam


linscan.defvjp(_linscan_fwd, _linscan_bwd)


def _bmm(a, b, precision=None):
    return jnp.einsum('nij,njk->nik', a, b, precision=precision, preferred_element_type=F32)


@jax.custom_vjp
def inv_unit_lower(l):
    return _inv_impl(l)


def _inv_impl(l):
    n = l.shape[-1]
    eye = (_rows((n, n)) == lax.broadcasted_iota(jnp.int32, (n, n), 1)).astype(F32)
    p = -l
    a = eye + p
    k = 1
    while 2 * k < n:
        p = _bmm(p, p, HI)
        a = a + _bmm(a, p, HI)
        k *= 2
    return a


def _inv_fwd(l):
    a = _inv_impl(l)
    return a, a


def _inv_bwd(a, da):
    at = jnp.swapaxes(a, 1, 2)
    return (-_bmm(_bmm(at, da, HI), at, HI),)


inv_unit_lower.defvjp(_inv_fwd, _inv_bwd)


def neg_expm1(y):
    series = -(y * (1.0 + y * (0.5 + y * (1.0 / 6.0 + y * (1.0 / 24.0)))))
    return jnp.where(y > -0.05, series, 1.0 - jnp.exp(y))


def run_stage(name, fn, *, tok, tok_nd=(), ex=(), par=(), par_tiled=(), big=(), out_ch, ts, nc=1, cot=None, addin=None,
              which="all"):
    tok, tok_nd, ex, par, par_tiled, big = map(list, (tok, tok_nd, ex, par, par_tiled, big))
    bsz, seq, _ = tok[0].shape
    ts = min(ts, seq)
    ns = seq // ts
    grid = (nc, bsz, ns)

    def tok_spec(a):
        cb = a.shape[-1] // nc
        return pl.BlockSpec((None, ts, cb), lambda c, b, s: (b, s, c))

    def ex_spec(a):
        cb = a.shape[-1] // nc
        return pl.BlockSpec((None, 1, cb), lambda c, b, s: (b, 0, c))

    def full_spec(a, single=False):
        nd = a.ndim
        kw = dict(pipeline_mode=pl.Buffered(1)) if single else {}
        return pl.BlockSpec(a.shape, lambda c, b, s: (0,) * nd, **kw)

    def tiled_spec(a):
        if a.ndim == 2:
            return pl.BlockSpec((a.shape[0], a.shape[1] // nc), lambda c, b, s: (0, c))
        return pl.BlockSpec((None,) + a.shape[1:], lambda c, b, s: (c, 0, 0))

    n_tok, n_nd, n_ex, n_par, n_pt, n_big = map(len, (tok, tok_nd, ex, par, par_tiled, big))
    in_arrays = tok + tok_nd + ex + par + par_tiled + big
    in_specs = ([tok_spec(a) for a in tok + tok_nd] + [ex_spec(a) for a in ex] + [full_spec(a) for a in par]
                + [tiled_spec(a) for a in par_tiled] + [full_spec(a, True) for a in big])
    out_tok_shapes = [jax.ShapeDtypeStruct((bsz, seq, ch), F32) for ch in out_ch]
    n_in = len(in_arrays)

    def split(vals):
        i = 0
        groups = []
        for n in (n_tok, n_nd, n_ex, n_par, n_pt, n_big):
            groups.append(list(vals[i:i + n]))
            i += n
        return groups

    def split_grads(vals):
        i = 0
        groups = []
        for n in (n_tok, n_ex, n_par, n_pt, n_big):
            groups.append(list(vals[i:i + n]))
            i += n
        return groups

    if cot is None:
        def body(*refs):
            tv, ndv, ev, pv, ptv, _ = split([r[...] for r in refs[:n_in - n_big]] + [None] * n_big)
            b_refs = refs[n_in - n_big:n_in]
            outs = fn(tv, ndv, ev, pv, ptv, lambda x, i, j=None: _dot(x, b_refs[i][...] if j is None else b_refs[i][j]))
            for r, o in zip(refs[n_in:], outs):
                r[...] = o

        return pl.pallas_call(
            body, name=name, grid=grid, in_specs=in_specs, out_specs=[tok_spec(a) for a in out_tok_shapes],
            out_shape=out_tok_shapes, compiler_params=_params(3))(*in_arrays)

    cot = list(cot)
    has_addin = addin is not None
    extra = cot + ([addin] if has_addin else [])
    n_cot = len(cot)
    want_small, want_big = which in ("all", "small"), which in ("all", "big")
    if not want_small:
        in_arrays, in_specs, n_in = in_arrays[:n_in - n_big], in_specs[:n_in - n_big], n_in - n_big
    small_arrays = tok + ex + par + par_tiled
    g_shapes = [jax.ShapeDtypeStruct(a.shape, F32) for a in (small_arrays if want_small else []) + (big if want_big else [])]
    g_specs = (([tok_spec(a) for a in tok] + [ex_spec(a) for a in ex] + [full_spec(a) for a in par]
                + [tiled_spec(a) for a in par_tiled]) if want_small else []) + (
                    [full_spec(a, True) for a in big] if want_big else [])

    def body(*refs):
        c, b, s = pl.program_id(0), pl.program_id(1), pl.program_id(2)
        n_small_in = n_tok + n_nd + n_ex + n_par + n_pt
        tv, ndv, ev, pv, ptv, _ = split([r[...] for r in refs[:n_small_in]] + [None] * n_big)
        b_refs = refs[n_small_in:n_in]
        cots = [r[...] for r in refs[n_in:n_in + n_cot]]
        g_refs = list(refs[n_in + len(extra):])
        probes = [jnp.zeros(w.shape, F32) if w.ndim == 2 else [jnp.zeros(w.shape[1:], F32) for _ in range(w.shape[0])]
                  for w in big]

        def f(tv_, ev_, pv_, ptv_, probes_):
            def mm(x, i, j=None):
                probe = None if probes_ is None else (probes_[i] if j is None else probes_[i][j])
                if not want_small:
                    return _probe_only(x, probe)
                w = b_refs[i][...] if j is None else b_refs[i][j]
                return _dot(x, w) if probe is None else _mm_probe(x, w, probe)

            return fn(tv_, ndv, ev_, pv_, ptv_, mm)

        dt = de = dp = dpt = dbg = ()
        if which == "all":
            dt, de, dp, dpt, dbg = jax.vjp(f, tv, ev, pv, ptv, probes)[1](cots)
        elif which == "small":
            dt, de, dp, dpt = jax.vjp(lambda *a: f(*a, None), tv, ev, pv, ptv)[1](cots)
        else:
            (dbg,) = jax.vjp(lambda p: f(tv, ev, pv, ptv, p), probes)[1](cots)
        if has_addin:
            dt = [dt[0] + refs[n_in + n_cot][...]] + list(dt[1:])
        if want_small:
            gt_r, ge_r, gp_r, gpt_r, gb_r = split_grads(g_refs + ([] if want_big else [None] * n_big))
        else:
            gt_r, ge_r, gp_r, gpt_r, gb_r = [], [], [], [], g_refs
        for r, g in zip(gt_r, dt):
            r[...] = g

        def accumulate(r, g, first):
            @pl.when(first)
            def _():
                r[...] = g

            @pl.when(jnp.logical_not(first))
            def _():
                r[...] += g

        for r, g in zip(ge_r, de):
            accumulate(r, g, s == 0)
        first_all = jnp.logical_and(jnp.logical_and(c == 0, b == 0), s == 0)
        for r, g in zip(gp_r, dp):
            accumulate(r, g, first_all)
        for r, g in zip(gpt_r, dpt):
            accumulate(r, g, jnp.logical_and(b == 0, s == 0))
        for r, g in zip(gb_r, dbg):
            if isinstance(g, (list, tuple)):
                for j, gj in enumerate(g):
                    accumulate(r.at[j], gj, first_all)
            else:
                accumulate(r, g, first_all)

    res = pl.pallas_call(
        body, name=name + "_bwd" + ("" if which == "all" else "_" + which), grid=grid,
        in_specs=in_specs + [tok_spec(a) for a in extra], out_specs=g_specs, out_shape=g_shapes,
        compiler_params=_params(3))(*in_arrays, *extra)
    res = list(res)
    if not want_small:
        return [[], [], [], [], res]
    return split_grads(res + ([] if want_big else [None] * n_big))


PROJ_PIECES = (("qkv", 768), ("z", 256), ("rx", 512), ("rgate", 512), ("mq", 256), ("mkv", 128), ("misc", 128))
PROJ_WIDTH = sum(w for _, w in PROJ_PIECES)
MISC_KR, MISC_A, MISC_B = 0, 32, 36


def fn_mixer_in(tok, nd, ex, par, pt, mm):
    (h,), (sc, sh), (g,) = tok, ex, par
    proj = mm(rms(h, g) * (1.0 + sc) + sh, 0)
    outs, o = [], 0
    for _, w in PROJ_PIECES:
        outs.append(proj[:, o:o + w])
        o += w
    return outs


def fn_gdn_conv(tok, nd, ex, par, pt, mm):
    return [jax.nn.silu(causal_conv4(tok[0], pt[0]))]


def _tri_masks():
    r = _rows((CHUNK, CHUNK))
    c = lax.broadcasted_iota(jnp.int32, (CHUNK, CHUNK), 1)
    return (c <= r), (c < r)


def fn_gdn_local(tok, nd, ex, par, pt, mm):
    (qkv, misc), (a_log, dt_bias) = tok, par
    ts = qkv.shape[0]
    nb = ts // CHUNK
    lower, strict = _tri_masks()
    tril = jnp.broadcast_to(lower.astype(F32), (nb, CHUNK, CHUNK))
    ones = jnp.ones((nb, CHUNK, CHUNK), F32)
    outs = [[] for _ in range(6)]
    for hd in range(GDN_HEADS):
        def head(x, base):
            return x[:, base + 64 * hd: base + 64 * hd + 64]

        def l2n(x):
            return x * lax.rsqrt(jnp.sum(x * x, axis=-1, keepdims=True) + EPS)

        q = (l2n(head(qkv, 0)) * (64.0 ** -0.5)).reshape(nb, CHUNK, 64)
        k = l2n(head(qkv, 256)).reshape(nb, CHUNK, 64)
        v = head(qkv, 512).reshape(nb, CHUNK, 64)
        a = misc[:, MISC_A + hd: MISC_A + hd + 1]
        b = misc[:, MISC_B + hd: MISC_B + hd + 1]
        g = -jnp.exp(a_log[:, hd:hd + 1]) * jax.nn.softplus(a + dt_bias[:, hd:hd + 1])
        beta = jax.nn.sigmoid(b).reshape(nb, CHUNK, 1)
        gb = jnp.broadcast_to(g.reshape(nb, CHUNK, 1), (nb, CHUNK, CHUNK))
        gi = _bmm(tril, gb, HI)
        gl = _bmm(ones, gb, HI)
        diff = gi - jnp.swapaxes(gi, 1, 2)
        decay = jnp.where(lower, jnp.exp(jnp.where(lower, diff, 0.0)), 0.0)
        kb = k * beta
        vb = v * beta
        kk = jnp.einsum('ncd,nsd->ncs', kb.astype(BF16), k.astype(BF16), preferred_element_type=F32)
        amat = inv_unit_lower(jnp.where(strict, kk * decay, 0.0))
        eg = jnp.exp(gi)
        u = _bmm(amat, vb, HI)
        w = _bmm(amat, kb * eg, HI)
        qk = jnp.einsum('ncd,nsd->ncs', q.astype(BF16), k.astype(BF16), preferred_element_type=F32) * decay
        qd = q * eg
        kt = k * jnp.exp(gl - gi)
        cd = jnp.exp(gl)
        for lst, val in zip(outs, (qk, qd, u, w, kt, cd)):
            lst.append(val.reshape(ts, 64))
    return [jnp.concatenate(lst, axis=-1) for lst in outs]


def fn_rglru(tok, nd, ex, par, pt, mm):
    (rx, rgate), (conv_w, conv_b, b_a, b_x, lam, bd_a, bd_x) = tok, pt
    xc = causal_conv4(rx, conv_w) + conv_b
    r = jax.nn.sigmoid(mmw(xc, bd_a) + b_a)
    i = jax.nn.sigmoid(mmw(xc, bd_x) + b_x)
    log_a = -RG_C * r * jax.nn.softplus(-lam)
    a = jnp.exp(log_a)
    bterm = jnp.sqrt(neg_expm1(2.0 * log_a)) * (i * xc)
    return [linscan(a, bterm) * jax.nn.gelu(rgate)]


def _rope32(x, cos, sin):
    x1, x2 = x[:, :16], x[:, 16:32]
    return jnp.concatenate([x1 * cos - x2 * sin, x2 * cos + x1 * sin], axis=-1)


def fn_mla_pre(tok, nd, ex, par, pt, mm):
    (mq, mkv, misc), (cs,), (g_q, g_kv, w_q, w_kv) = tok, nd, par
    q = mmw(rms(mq, g_q), w_q)
    kv = mmw(rms(mkv, g_kv), w_kv)
    cos, sin = cs[:, 0:16], cs[:, 16:32]
    qp = jnp.concatenate([_rope32(q[:, 256 + 32 * h: 288 + 32 * h], cos, sin) for h in range(MLA_HEADS)], axis=-1)
    kp = _rope32(misc[:, MISC_KR:MISC_KR + 32], cos, sin)
    kp = jnp.concatenate([kp, jnp.zeros((kp.shape[0], 96), F32)], axis=-1)
    return [q[:, 0:256], qp, kv[:, 0:256], kv[:, 256:512], kp]


def fn_out_proj(tok, nd, ex, par, pt, mm):
    (h, o_a, o_b, o_c), (gt,) = tok, ex
    return [h + gt * mm(jnp.concatenate([o_a, o_b, o_c], axis=-1), 0)]


def fn_mlp_in(tok, nd, ex, par, pt, mm):
    (h,), (sc, sh), (g,) = tok, ex, par
    u = rms(h, g) * (1.0 + sc) + sh
    return [jnp.concatenate([mm(u, 0, j) for j in range(N_CHIP)], axis=-1)]


def fn_mlp_out(tok, nd, ex, par, pt, mm):
    (h, a), (gt,) = tok, ex
    return [h + gt * mm(jnp.square(jax.nn.relu(a)), 0)]


def _gdn_step(states, qk, qd, u, w, kt, cd, z, norm_g):
    new_states, outs = [], []
    for hd in range(GDN_HEADS):
        sl = slice(64 * hd, 64 * hd + 64)
        st = states[hd]
        v_new = u[:, sl] - _dot(w[:, sl], st)
        o = _dot(qd[:, sl], st) + _dot(qk[:, sl], v_new)
        new_states.append(st * cd[:, sl] + _dot(kt[:, sl], v_new, (((0,), (0,)), ((), ()))))
        outs.append(rms(o, norm_g) * jax.nn.silu(z[:, sl]))
    return new_states, jnp.concatenate(outs, axis=-1)


def gdn_scan(xs, z, norm_g):
    bsz, seq, _ = z.shape
    n = seq // CHUNK
    blk = pl.BlockSpec((bsz, CHUNK, 256), lambda i: (0, i, 0))

    def body(qk, qd, u, w, kt, cd, z_ref, g_ref, o_ref, st_out, st):
        @pl.when(pl.program_id(0) == 0)
        def _():
            st[...] = jnp.zeros_like(st)

        for b in range(bsz):
            states = [st[b * GDN_HEADS + hd] for hd in range(GDN_HEADS)]
            st_out[b] = jnp.concatenate(states, axis=-1)
            new_states, o = _gdn_step(states, qk[b], qd[b], u[b], w[b], kt[b], cd[b], z_ref[b], g_ref[...])
            o_ref[b] = o
            for hd in range(GDN_HEADS):
                st[b * GDN_HEADS + hd] = new_states[hd]

    return pl.pallas_call(
        body, name="gdn_scan", grid=(n,), in_specs=[blk] * 7 + [pl.BlockSpec((1, 64), lambda i: (0, 0))],
        out_specs=[blk, blk], out_shape=[jax.ShapeDtypeStruct((bsz, seq, 256), F32)] * 2,
        scratch_shapes=[pltpu.VMEM((bsz * GDN_HEADS, 64, 64), F32)], compiler_params=_params(1))(*xs, z, norm_g)


def gdn_scan_bwd(xs, z, norm_g, st_in, do):
    bsz, seq, _ = z.shape
    n = seq // CHUNK
    blk = pl.BlockSpec((bsz, CHUNK, 256), lambda i: (0, n - 1 - i, 0))
    gspec = pl.BlockSpec((1, 64), lambda i: (0, 0))

    def body(qk, qd, u, w, kt, cd, z_ref, g_ref, st_ref, do_ref, dqk, dqd, du, dw, dkt, dcd, dz, dg, dst):
        first = pl.program_id(0) == 0

        @pl.when(first)
        def _():
            dst[...] = jnp.zeros_like(dst)

        dg_sum = None
        for b in range(bsz):
            stv = st_ref[b]
            states = [stv[:, 64 * hd: 64 * hd + 64] for hd in range(GDN_HEADS)]
            _, vjp = jax.vjp(_gdn_step, states, qk[b], qd[b], u[b], w[b], kt[b], cd[b], z_ref[b], g_ref[...])
            grads = vjp(([dst[b * GDN_HEADS + hd] for hd in range(GDN_HEADS)], do_ref[b]))
            for hd in range(GDN_HEADS):
                dst[b * GDN_HEADS + hd] = grads[0][hd]
            for r, g in zip((dqk, dqd, du, dw, dkt, dcd, dz), grads[1:8]):
                r[b] = g
            dg_sum = grads[8] if dg_sum is None else dg_sum + grads[8]

        @pl.when(first)
        def _():
            dg[...] = dg_sum

        @pl.when(jnp.logical_not(first))
        def _():
            dg[...] += dg_sum

    res = pl.pallas_call(
        body, name="gdn_scan_bwd", grid=(n,), in_specs=[blk] * 7 + [gspec, blk, blk],
        out_specs=[blk] * 7 + [gspec], out_shape=[jax.ShapeDtypeStruct((bsz, seq, 256), F32)] * 7
        + [jax.ShapeDtypeStruct((1, 64), F32)],
        scratch_shapes=[pltpu.VMEM((bsz * GDN_HEADS, 64, 64), F32)], compiler_params=_params(1))(*xs, z, norm_g, st_in, do)
    return list(res[:6]), res[6], res[7]


ATTN_TQ = 256
ATTN_SCALE = 96.0 ** -0.5


def _attn_head(qn, qp, kn, kp, v, q0):
    nt = (((1,), (1,)), ((), ()))
    s = (_dot(qn, kn, nt) + _dot(qp, kp, nt)) * ATTN_SCALE
    qc = (q0 + _rows(s.shape)) // CHUNK
    kc = lax.broadcasted_iota(jnp.int32, s.shape, 1) // CHUNK
    s = jnp.where(kc <= qc, s, -1e30)
    p = jnp.exp(s - jnp.max(s, axis=-1, keepdims=True))
    p = p / jnp.sum(p, axis=-1, keepdims=True)
    return _dot(p, v)


def _key_lengths(seq):
    n_var = min(4, seq // ATTN_TQ)
    return [(j + 1) * (seq // n_var) for j in range(n_var)]


def _key_variant(i, seq):
    return ((i + 1) * ATTN_TQ - 1) // _key_lengths(seq)[0]


def mla_attention(qn, qp, kn, kp, v):
    bsz, seq, _ = qn.shape
    nq = seq // ATTN_TQ

    def qspec(ch):
        return pl.BlockSpec((None, ATTN_TQ, ch), lambda b, i: (b, i, 0))

    def kspec(ch):
        return pl.BlockSpec((None, seq, ch), lambda b, i: (b, 0, 0))

    def body(qn_r, qp_r, kn_r, kp_r, v_r, o_r):
        i = pl.program_id(1)
        q0 = i * ATTN_TQ

        def with_keys(klen):
            kpv = kp_r[0:klen, 0:32]
            outs = []
            for h in range(MLA_HEADS):
                sl = slice(64 * h, 64 * h + 64)
                outs.append(_attn_head(qn_r[:, sl], qp_r[:, 32 * h: 32 * h + 32], kn_r[0:klen, sl], kpv,
                                       v_r[0:klen, sl], q0))
            o_r[...] = jnp.concatenate(outs, axis=-1)

        for j, klen in enumerate(_key_lengths(seq)):
            pl.when(_key_variant(i, seq) == j)(functools.partial(with_keys, klen))

    return pl.pallas_call(
        body, name="mla_attention", grid=(bsz, nq), in_specs=[qspec(256), qspec(128), kspec(256), kspec(128), kspec(256)],
        out_specs=qspec(256), out_shape=jax.ShapeDtypeStruct((bsz, seq, 256), F32), compiler_params=_params(2))(
            qn, qp, kn, kp, v)


def mla_attention_bwd(qn, qp, kn, kp, v, do):
    bsz, seq, _ = qn.shape
    nq = seq // ATTN_TQ

    def qspec(ch):
        return pl.BlockSpec((None, ATTN_TQ, ch), lambda b, i: (b, i, 0))

    def kspec(ch):
        return pl.BlockSpec((None, seq, ch), lambda b, i: (b, 0, 0))

    def body(qn_r, qp_r, kn_r, kp_r, v_r, do_r, dqn_r, dqp_r, dkn_r, dkp_r, dv_r):
        i = pl.program_id(1)
        q0 = i * ATTN_TQ

        @pl.when(i == 0)
        def _():
            dkn_r[...] = jnp.zeros_like(dkn_r)
            dkp_r[...] = jnp.zeros_like(dkp_r)
            dv_r[...] = jnp.zeros_like(dv_r)

        def with_keys(klen):
            kpv = kp_r[0:klen, 0:32]
            dqn, dqp, dkn, dv = [], [], [], []
            dkp = jnp.zeros((klen, 32), F32)
            for h in range(MLA_HEADS):
                sl = slice(64 * h, 64 * h + 64)
                _, vjp = jax.vjp(functools.partial(_attn_head, q0=q0), qn_r[:, sl], qp_r[:, 32 * h: 32 * h + 32],
                                 kn_r[0:klen, sl], kpv, v_r[0:klen, sl])
                a, b, c, d, e = vjp(do_r[:, sl])
                dqn.append(a)
                dqp.append(b)
                dkn.append(c)
                dkp = dkp + d
                dv.append(e)
            dqn_r[...] = jnp.concatenate(dqn, axis=-1)
            dqp_r[...] = jnp.concatenate(dqp, axis=-1)
            dkn_r[0:klen, :] += jnp.concatenate(dkn, axis=-1)
            dv_r[0:klen, :] += jnp.concatenate(dv, axis=-1)
            dkp_r[0:klen, 0:32] += dkp

        for j, klen in enumerate(_key_lengths(seq)):
            pl.when(_key_variant(i, seq) == j)(functools.partial(with_keys, klen))

    shp = lambda ch: jax.ShapeDtypeStruct((bsz, seq, ch), F32)
    return pl.pallas_call(
        body, name="mla_attention_bwd", grid=(bsz, nq),
        in_specs=[qspec(256), qspec(128), kspec(256), kspec(128), kspec(256), qspec(256)],
        out_specs=[qspec(256), qspec(128), kspec(256), kspec(128), kspec(256)],
        out_shape=[shp(256), shp(128), shp(256), shp(128), shp(256)], compiler_params=_params(2))(qn, qp, kn, kp, v, do)


LOSS_TS = 512


def loss_head(h, g, target):
    bsz, seq, d = h.shape
    ts = min(LOSS_TS, seq)
    tok = pl.BlockSpec((None, ts, d), lambda b, s: (b, s, 0))
    gspec = pl.BlockSpec((1, d), lambda b, s: (0, 0))
    lspec = pl.BlockSpec((1, 128), lambda b, s: (0, 0))

    def body(h_r, g_r, t_r, loss_r, dh_r, dg_r):
        first = jnp.logical_and(pl.program_id(0) == 0, pl.program_id(1) == 0)
        tv = t_r[...]

        def f(hv, gv):
            return 0.5 * jnp.sum(jnp.mean(jnp.square(rms(hv, gv) - tv), axis=-1, keepdims=True), axis=0, keepdims=True)

        val, vjp = jax.vjp(f, h_r[...], g_r[...])
        dh, dg = vjp(jnp.ones((1, 1), F32))
        dh_r[...] = dh
        lv = jnp.broadcast_to(val, (1, 128))

        @pl.when(first)
        def _():
            loss_r[...] = lv
            dg_r[...] = dg

        @pl.when(jnp.logical_not(first))
        def _():
            loss_r[...] += lv
            dg_r[...] += dg

    return pl.pallas_call(
        body, name="loss_head", grid=(bsz, seq // ts), in_specs=[tok, gspec, tok], out_specs=[lspec, tok, gspec],
        out_shape=[jax.ShapeDtypeStruct((1, 128), F32), jax.ShapeDtypeStruct(h.shape, F32), jax.ShapeDtypeStruct((1, d), F32)],
        compiler_params=_params(2))(h, g, target)


def _adamw_math(w, g, m, v):
    m = ADAM_B1 * m + (1.0 - ADAM_B1) * g
    v = ADAM_B2 * v + (1.0 - ADAM_B2) * jnp.square(g)
    m_hat = m / (1.0 - ADAM_B1 ** ADAM_STEP)
    v_hat = v / (1.0 - ADAM_B2 ** ADAM_STEP)
    return -ADAM_LR * (m_hat / (jnp.sqrt(v_hat) + ADAM_EPS) + ADAM_WD * w), m, v


def _row_block(rows, cols):
    want = max(8, (1 << 18) // cols)
    best = rows
    for r in range(8, rows + 1, 8):
        if rows % r == 0 and r <= want:
            best = r
    return best if rows % 8 == 0 else rows


def adamw(name, w, g, m, v):
    rows, cols = w.shape
    rb = _row_block(rows, cols)
    spec = pl.BlockSpec((rb, cols), lambda i: (i, 0))

    def body(w_r, g_r, m_r, v_r, d_o, m_o, v_o):
        d, mn, vn = _adamw_math(w_r[...], g_r[...], m_r[...], v_r[...])
        d_o[...] = d
        m_o[...] = mn
        v_o[...] = vn

    return pl.pallas_call(body, name=name, grid=(rows // rb,), in_specs=[spec] * 4, out_specs=[spec] * 3,
                          out_shape=[jax.ShapeDtypeStruct(w.shape, F32)] * 3, compiler_params=_params(1))(w, g, m, v)


def adamw_reduce(name, parts, w, m, v):
    rows, cols = w.shape
    rb = _row_block(rows, cols)
    spec = pl.BlockSpec((rb, cols), lambda i: (i, 0))
    pspec = pl.BlockSpec((N_DEV, rb, cols), lambda i: (0, i, 0))

    def body(p_r, w_r, m_r, v_r, g_o, d_o, m_o, v_o):
        g = p_r[0]
        for k in range(1, N_DEV):
            g = g + p_r[k]
        d, mn, vn = _adamw_math(w_r[...], g, m_r[...], v_r[...])
        g_o[...] = g
        d_o[...] = d
        m_o[...] = mn
        v_o[...] = vn

    return pl.pallas_call(body, name=name, grid=(rows // rb,), in_specs=[pspec, spec, spec, spec], out_specs=[spec] * 4,
                          out_shape=[jax.ShapeDtypeStruct(w.shape, F32)] * 4, compiler_params=_params(1))(parts, w, m, v)


MOD_CB = 512


def mod_matmul(c_rows, w_mod, b_mod):
    nl, d, cols = w_mod.shape

    def body(c_r, w_r, b_r, o_r):
        o_r[...] = _dot(jax.nn.silu(c_r[...]), w_r[...]) + b_r[...]

    return pl.pallas_call(
        body, name="mod_matmul", grid=(nl, cols // MOD_CB),
        in_specs=[pl.BlockSpec((8, d), lambda l, j: (0, 0)), pl.BlockSpec((None, d, MOD_CB), lambda l, j: (l, 0, j)),
                  pl.BlockSpec((None, 1, MOD_CB), lambda l, j: (l, 0, j))],
        out_specs=pl.BlockSpec((None, 8, MOD_CB), lambda l, j: (l, 0, j)),
        out_shape=jax.ShapeDtypeStruct((nl, 8, cols), F32), compiler_params=_params(2))(c_rows, w_mod, b_mod)


def mod_weight_grad(c_all, dmod):
    nl, nb, cols = dmod.shape
    d = c_all.shape[1]

    def body(c_r, g_r, o_r):
        o_r[...] = _dot(jax.nn.silu(c_r[...]), g_r[...], (((0,), (0,)), ((), ())))

    return pl.pallas_call(
        body, name="mod_weight_grad", grid=(nl, cols // MOD_CB),
        in_specs=[pl.BlockSpec((nb, d), lambda l, j: (0, 0)), pl.BlockSpec((None, nb, MOD_CB), lambda l, j: (l, 0, j))],
        out_specs=pl.BlockSpec((None, d, MOD_CB), lambda l, j: (l, 0, j)),
        out_shape=jax.ShapeDtypeStruct((nl, d, cols), F32), compiler_params=_params(2))(c_all, dmod)


def _half_block(hr, cols):
    rb = _row_block(hr, cols)
    return rb if rb % 16 == 0 else hr


def add_half(name, g, s, core):
    _, r, cols = g.shape
    hr = r // 2
    rb = _half_block(hr, cols)
    nblk = hr // rb
    gspec = pl.BlockSpec((None, rb, cols), lambda k, i, c: (k, c[0] * nblk + i, 0))
    spec = pl.BlockSpec((None, rb, cols), lambda k, i, c: (k, i, 0))

    def body(c_r, g_r, s_r, o_r, ob_r):
        t = g_r[...] + s_r[...]
        o_r[...] = t
        ob_r[...] = t.astype(BF16)

    return pl.pallas_call(
        body, name=name, grid_spec=pltpu.PrefetchScalarGridSpec(num_scalar_prefetch=1, grid=(N_CHIP, nblk),
                                                                in_specs=[gspec, spec], out_specs=[spec, spec]),
        out_shape=[jax.ShapeDtypeStruct((N_CHIP, hr, cols), F32), jax.ShapeDtypeStruct((N_CHIP, hr, cols), BF16)],
        compiler_params=_params(2))(core, g, s)


def sum_peers(name, p32, recv, chip):
    _, hr, cols = p32.shape
    rb = _half_block(hr, cols)

    def slot(k):
        return pl.BlockSpec((None, rb, cols), lambda i, c: ((c[0] + k) % N_CHIP, i, 0))

    def body(c_r, o_r, r1, r2, r3, out_r):
        out_r[...] = ((o_r[...] + r1[...].astype(F32)) + r2[...].astype(F32)) + r3[...].astype(F32)

    return pl.pallas_call(
        body, name=name, grid_spec=pltpu.PrefetchScalarGridSpec(
            num_scalar_prefetch=1, grid=(hr // rb,), in_specs=[slot(0), slot(1), slot(2), slot(3)],
            out_specs=pl.BlockSpec((rb, cols), lambda i, c: (i, 0))),
        out_shape=jax.ShapeDtypeStruct((hr, cols), F32), compiler_params=_params(1))(chip, p32, recv, recv, recv)


def _me():
    return lax.axis_index("x"), lax.axis_index("y"), lax.axis_index("c")


def all_gather8(name, x_shard, in_vmem):
    m_per, n = x_shard.shape
    space = pltpu.VMEM if in_vmem else pl.ANY

    def body(x_ref, out_ref, send_sems, recv_sems, local_sem):
        x, y, c = _me()
        me, sibling = (x, y, c), (x, y, 1 - c)
        chips = [(1 - x, y), (x, 1 - y), (1 - x, 1 - y)]

        def rows(px, py, pc):
            return out_ref.at[pl.ds((4 * px + 2 * py + pc) * m_per, m_per), :]

        def copy(k, block, to, src=None):
            return pltpu.make_async_remote_copy(
                src_ref=rows(*block) if src is None else src, dst_ref=rows(*block), send_sem=send_sems.at[k],
                recv_sem=recv_sems.at[k], device_id=to, device_id_type=MESH)

        mine = pltpu.make_async_copy(x_ref, rows(*me), local_sem)
        mine.start()
        first = [copy(0, me, sibling, src=x_ref)]
        first += [copy(1 + j, me, (*chip, c), src=x_ref) for j, chip in enumerate(chips)]
        for cp in first:
            cp.start()
        passed = [copy(4 + j, (*chip, c), sibling) for j, chip in enumerate(chips)]
        for j, chip in enumerate(chips):
            copy(1 + j, (*chip, c), me).wait_recv()
            passed[j].start()
        copy(0, sibling, me).wait_recv()
        for j, chip in enumerate(chips):
            copy(4 + j, (*chip, 1 - c), me).wait_recv()
        for cp in first + passed:
            cp.wait_send()
        mine.wait()

    return pl.pallas_call(
        body, name=name, out_shape=jax.ShapeDtypeStruct((N_DEV * m_per, n), x_shard.dtype),
        in_specs=[pl.BlockSpec(memory_space=space)], out_specs=pl.BlockSpec(memory_space=space),
        scratch_shapes=[pltpu.SemaphoreType.DMA((7,)), pltpu.SemaphoreType.DMA((7,)), pltpu.SemaphoreType.DMA],
    )(x_shard)


_ANY = pl.BlockSpec(memory_space=pl.ANY)


def all_gather_weights(name, pieces):
    n = len(pieces)

    def body(*refs):
        ins, outs = refs[:n], refs[n:2 * n]
        send_sems, recv_sems, local_sems = refs[2 * n:]
        x, y, c = _me()
        me, sibling = (x, y, c), (x, y, 1 - c)
        chips = [(1 - x, y), (x, 1 - y), (1 - x, 1 - y)]

        def view(i, px, py, pc):
            hr = pieces[i].shape[1]
            return outs[i].at[:, 2 * px + py, pl.ds(pc * hr, hr), :]

        def copy(i, k, block, to, own=False):
            return pltpu.make_async_remote_copy(
                src_ref=ins[i] if own else view(i, *block), dst_ref=view(i, *block), send_sem=send_sems.at[i, k],
                recv_sem=recv_sems.at[i, k], device_id=to, device_id_type=MESH)

        mine = [pltpu.make_async_copy(ins[i], view(i, *me), local_sems.at[i]) for i in range(n)]
        first = []
        for i in range(n):
            mine[i].start()
            first.append(copy(i, 0, me, sibling, own=True))
            first += [copy(i, 1 + j, me, (*chip, c), own=True) for j, chip in enumerate(chips)]
        for cp in first:
            cp.start()
        passed = []
        for j, chip in enumerate(chips):
            for i in range(n):
                copy(i, 1 + j, (*chip, c), me).wait_recv()
                passed.append(copy(i, 4 + j, (*chip, c), sibling))
                passed[-1].start()
        for i in range(n):
            copy(i, 0, sibling, me).wait_recv()
            for j, chip in enumerate(chips):
                copy(i, 4 + j, (*chip, 1 - c), me).wait_recv()
        for cp in first + passed:
            cp.wait_send()
        for cp in mine:
            cp.wait()

    return pl.pallas_call(
        body, name=name,
        out_shape=[jax.ShapeDtypeStruct((p.shape[0], N_CHIP, 2 * p.shape[1], p.shape[2]), p.dtype) for p in pieces],
        in_specs=[_ANY] * n, out_specs=[_ANY] * n,
        scratch_shapes=[pltpu.SemaphoreType.DMA((n, 7)), pltpu.SemaphoreType.DMA((n, 7)), pltpu.SemaphoreType.DMA((n,))],
    )(*pieces)


def grad_sibling_exchange(name, gs):
    n = len(gs)

    def body(*refs):
        ins, outs = refs[:n], refs[n:2 * n]
        send_sems, recv_sems = refs[2 * n:]
        mx, my, mc = _me()
        cps = []
        for i in range(n):
            hr = gs[i].shape[1] // 2
            cps.append(pltpu.make_async_remote_copy(
                src_ref=ins[i].at[:, pl.ds((1 - mc) * hr, hr), :], dst_ref=outs[i], send_sem=send_sems.at[i],
                recv_sem=recv_sems.at[i], device_id=(mx, my, 1 - mc), device_id_type=MESH))
            cps[-1].start()
        for cp in cps:
            cp.wait()

    return pl.pallas_call(
        body, name=name, out_shape=[jax.ShapeDtypeStruct((N_CHIP, g.shape[1] // 2, g.shape[2]), g.dtype) for g in gs],
        in_specs=[_ANY] * n, out_specs=[_ANY] * n,
        scratch_shapes=[pltpu.SemaphoreType.DMA((n,)), pltpu.SemaphoreType.DMA((n,))],
    )(*gs)


def grad_chip_exchange(name, ps):
    n = len(ps)

    def body(*refs):
        ins, outs = refs[:n], refs[n:2 * n]
        send_sems, recv_sems = refs[2 * n:]
        mx, my, mc = _me()
        ci = 2 * mx + my
        chips = [(1 - mx, my), (mx, 1 - my), (1 - mx, 1 - my)]
        sends = []
        for i in range(n):
            for k, (px, py) in enumerate(chips):
                sends.append(pltpu.make_async_remote_copy(
                    src_ref=ins[i].at[2 * px + py], dst_ref=outs[i].at[ci], send_sem=send_sems.at[i, k],
                    recv_sem=recv_sems.at[i, k], device_id=(px, py, mc), device_id_type=MESH))
                sends[-1].start()
        for i in range(n):
            for k, (px, py) in enumerate(chips):
                pltpu.make_async_remote_copy(
                    src_ref=ins[i].at[ci], dst_ref=outs[i].at[2 * px + py], send_sem=send_sems.at[i, k],
                    recv_sem=recv_sems.at[i, k], device_id=(px, py, mc), device_id_type=MESH).wait_recv()
        for cp in sends:
            cp.wait_send()

    return pl.pallas_call(
        body, name=name, out_shape=[jax.ShapeDtypeStruct(p.shape, p.dtype) for p in ps], in_specs=[_ANY] * n,
        out_specs=[_ANY] * n, scratch_shapes=[pltpu.SemaphoreType.DMA((n, 3)), pltpu.SemaphoreType.DMA((n, 3))],
    )(*ps)


def grad_half_exchange(name, qs, layers, out_shapes):
    n = len(qs)
    n_out = len(out_shapes)

    def body(*refs):
        ins, outs = refs[:n], refs[n:n + n_out]
        send_sems, recv_sems, local_sems = refs[n + n_out:]
        mx, my, mc = _me()
        local, remote = [], []
        for i in range(n):
            w, l = layers[i]
            hr = qs[i].shape[0]

            def rows(core):
                return outs[w].at[l, pl.ds(core * hr, hr), :]

            local.append(pltpu.make_async_copy(ins[i], rows(mc), local_sems.at[i]))
            local[-1].start()
            remote.append(pltpu.make_async_remote_copy(
                src_ref=ins[i], dst_ref=rows(mc), send_sem=send_sems.at[i], recv_sem=recv_sems.at[i],
                device_id=(mx, my, 1 - mc), device_id_type=MESH))
            remote[-1].start()
        for i in range(n):
            w, l = layers[i]
            hr = qs[i].shape[0]
            pltpu.make_async_remote_copy(
                src_ref=ins[i], dst_ref=outs[w].at[l, pl.ds((1 - mc) * hr, hr), :], send_sem=send_sems.at[i],
                recv_sem=recv_sems.at[i], device_id=(mx, my, 1 - mc), device_id_type=MESH).wait_recv()
        for cp in remote:
            cp.wait_send()
        for cp in local:
            cp.wait()

    return pl.pallas_call(
        body, name=name, out_shape=[jax.ShapeDtypeStruct(s, F32) for s in out_shapes], in_specs=[_ANY] * n,
        out_specs=[_ANY] * n_out,
        scratch_shapes=[pltpu.SemaphoreType.DMA((n,)), pltpu.SemaphoreType.DMA((n,)), pltpu.SemaphoreType.DMA((n,))],
    )(*qs)


WEIGHTS = ['w_mod', 'b_mod', 'norm_mix_g', 'w_in', 'gdn_conv_w', 'gdn_a_log', 'gdn_dt_bias', 'gdn_norm_g', 'rg_conv_w',
           'rg_conv_b', 'rg_w_a', 'rg_b_a', 'rg_w_x', 'rg_b_x', 'rg_lambda', 'mla_q_norm_g', 'mla_w_qb', 'mla_kv_norm_g',
           'mla_w_kvb', 'w_out', 'norm_mlp_g', 'w_mlp_in', 'w_mlp_out', 'final_norm_g']
SHARDED = {'w_in': 2, 'gdn_conv_w': 2, 'rg_conv_w': 2, 'mla_w_qb': 2, 'mla_w_kvb': 2, 'w_out': 1, 'w_mlp_in': 2, 'w_mlp_out': 1}
GATHER_BF16 = ('w_in', 'mla_w_qb', 'mla_w_kvb', 'w_out', 'w_mlp_in', 'w_mlp_out')
REPLICATED = [n for n in WEIGHTS if n not in SHARDED and n != 'w_mod']
PACK_COLS = 1024


def _pack(arrays, multiple):
    flat = jnp.concatenate([a.reshape(-1) for a in arrays])
    pad = (-flat.shape[0]) % multiple
    return jnp.pad(flat, (0, pad)) if pad else flat


def _unpack(flat, shapes):
    out, o = [], 0
    for shp in shapes:
        n = int(np.prod(shp))
        out.append(flat[o:o + n].reshape(shp))
        o += n
    return out


def _unshard(stacked, axis):
    moved = jnp.moveaxis(stacked, 0, axis)
    shp = list(moved.shape)
    shp[axis:axis + 2] = [shp[axis] * shp[axis + 1]]
    return moved.reshape(shp)


def _shard(full, axis):
    shp = list(full.shape)
    shp[axis:axis + 1] = [N_CHIP, shp[axis] // N_CHIP]
    return jnp.moveaxis(full.reshape(shp), axis, 0)


def _proj_cols(w):
    pad = jnp.zeros(w.shape[:-1] + (PROJ_WIDTH - w.shape[-1],), w.dtype)
    return jnp.concatenate([w[..., 0:1024], w[..., 1032:2472], w[..., 1024:1032], pad], axis=-1)


def _proj_cols_back(d):
    return jnp.concatenate([d[..., 0:1024], d[..., 2464:2472], d[..., 1024:2464]], axis=-1)


def _heads_split(w, heads, first):
    per = w.shape[-1] // heads
    r = w.reshape(w.shape[:-1] + (heads, per))
    lead = w.shape[:-1]
    return jnp.concatenate([r[..., :first].reshape(lead + (heads * first,)),
                            r[..., first:].reshape(lead + (heads * (per - first),))], axis=-1)


def _heads_merge(d, heads, first):
    lead = d.shape[:-1]
    per = d.shape[-1] // heads
    a = d[..., :heads * first].reshape(lead + (heads, first))
    b = d[..., heads * first:].reshape(lead + (heads, per - first))
    return jnp.concatenate([a, b], axis=-1).reshape(lead + (heads * per,))


def _block_diag(w):
    nl = w.shape[0]
    eye = jnp.eye(2, dtype=w.dtype)
    return jnp.einsum('lcoij,op->lcoipj', w.reshape(nl, 4, 2, 64, 64), eye).reshape(nl, 4, 128, 128)


def _block_diag_back(g):
    nl = g.shape[0]
    return jnp.einsum('lcoipj,op->lcoij', g.reshape(nl, 4, 2, 64, 2, 64), jnp.eye(2, dtype=g.dtype)).reshape(nl, 8, 64, 64)


def kernel(x, c, positions, w_mod, b_mod, norm_mix_g, w_in, gdn_conv_w, gdn_a_log, gdn_dt_bias, gdn_norm_g, rg_conv_w, rg_conv_b, rg_w_a, rg_b_a, rg_w_x, rg_b_x, rg_lambda, mla_q_norm_g, mla_w_qb, mla_kv_norm_g, mla_w_kvb, w_out, norm_mlp_g, w_mlp_in, w_mlp_out, final_norm_g, loss_target, m_w_mod, m_b_mod, m_norm_mix_g, m_w_in, m_gdn_conv_w, m_gdn_a_log, m_gdn_dt_bias, m_gdn_norm_g, m_rg_conv_w, m_rg_conv_b, m_rg_w_a, m_rg_b_a, m_rg_w_x, m_rg_b_x, m_rg_lambda, m_mla_q_norm_g, m_mla_w_qb, m_mla_kv_norm_g, m_mla_w_kvb, m_w_out, m_norm_mlp_g, m_w_mlp_in, m_w_mlp_out, m_final_norm_g, v_w_mod, v_b_mod, v_norm_mix_g, v_w_in, v_gdn_conv_w, v_gdn_a_log, v_gdn_dt_bias, v_gdn_norm_g, v_rg_conv_w, v_rg_conv_b, v_rg_w_a, v_rg_b_a, v_rg_w_x, v_rg_b_x, v_rg_lambda, v_mla_q_norm_g, v_mla_w_qb, v_mla_kv_norm_g, v_mla_w_kvb, v_w_out, v_norm_mlp_g, v_w_mlp_in, v_w_mlp_out, v_final_norm_g):
    given = dict(locals())
    wts = {n: given[n] for n in WEIGHTS}
    mom_m = {n: given["m_" + n] for n in WEIGHTS}
    mom_v = {n: given["v_" + n] for n in WEIGHTS}
    bsz, seq, d = x.shape
    depth = w_mod.shape[0]
    mx, my, mc = lax.axis_index("x"), lax.axis_index("y"), lax.axis_index("c")
    chip = 2 * mx + my
    dev = 2 * chip + mc

    conv_shapes = [wts['gdn_conv_w'].shape, wts['rg_conv_w'].shape]
    conv_flat = _pack([wts['gdn_conv_w'], wts['rg_conv_w']], d)
    conv_rows = conv_flat.shape[0] // d
    assert bsz + conv_rows <= 8
    c_pad = jnp.concatenate([c, conv_flat.reshape(conv_rows, d), jnp.zeros((8 - bsz - conv_rows, d), F32)], axis=0)
    gath = all_gather8("gather_c", c_pad, True).reshape(N_DEV, 8, d)
    c_all = gath[:, :bsz].reshape(N_DEV * bsz, d)
    conv_all = gath[0::2, bsz:bsz + conv_rows].reshape(N_CHIP, conv_rows * d)
    gdn_conv_full, rg_conv_full = [
        _unshard(jnp.stack([_unpack(conv_all[s], conv_shapes)[i] for s in range(N_CHIP)]), 2) for i in range(2)]

    n_half = N_DEV * bsz // 2
    mod_cols = w_mod.shape[2]
    c_rows = lax.dynamic_slice(c_all, (n_half * mc, 0), (n_half, d))
    b_mod_mine = lax.dynamic_slice(b_mod, (0, chip * mod_cols), (depth, mod_cols)).reshape(depth, 1, mod_cols)
    mod_piece = mod_matmul(c_rows, w_mod, b_mod_mine)
    mod_g = all_gather8("gather_mod", mod_piece.reshape(depth * n_half, mod_cols), True)
    mod_all = mod_g.reshape(N_CHIP, 2, depth, n_half, mod_cols).transpose(2, 1, 3, 0, 4).reshape(depth, 2 * n_half, 6 * d)
    mod_mine = lax.dynamic_slice(mod_all, (0, bsz * dev, 0), (depth, bsz, 6 * d)).reshape(depth, bsz, 6, 1, d)

    def my_rows(w):
        hr = w.shape[1] // 2
        return lax.dynamic_slice(w, (0, mc * hr, 0), (w.shape[0], hr, w.shape[2])).astype(BF16)

    slabs = dict(zip(GATHER_BF16, all_gather_weights("gather_weights", [my_rows(wts[n]) for n in GATHER_BF16])))

    def columns(g):
        return g.transpose(0, 2, 1, 3).reshape(g.shape[0], g.shape[2], N_CHIP * g.shape[3])

    def rows_of(g):
        return g.reshape(g.shape[0], N_CHIP * g.shape[2], g.shape[3])

    w_cat = _proj_cols(columns(slabs['w_in']))
    w_q = _heads_split(columns(slabs['mla_w_qb']).astype(F32), MLA_HEADS, 64)
    w_kv = _heads_split(columns(slabs['mla_w_kvb']).astype(F32), MLA_HEADS, 64)
    w_out_full, w_mlp_out_full = rows_of(slabs['w_out']), rows_of(slabs['w_mlp_out'])
    bd_a, bd_x = _block_diag(rg_w_a), _block_diag(rg_w_x)

    inv_freq = ROPE_THETA ** (-jnp.arange(0, 32, 2, dtype=F32) / 32.0)
    ang = positions.astype(F32)[..., None] * inv_freq
    cs = jnp.concatenate([jnp.cos(ang), jnp.sin(ang)], axis=-1)

    proj_ch = [w for _, w in PROJ_PIECES]

    def row(a, l):
        return a[l].reshape(1, -1)

    def layer_args(l):
        sh_m, sc_m, gt_m, sh_f, sc_f, gt_f = (mod_mine[l, :, k] for k in range(6))
        return dict(
            mods=(sh_m, sc_m, gt_m, sh_f, sc_f, gt_f),
            mixer_in=dict(ex=[sc_m, sh_m], par=[row(norm_mix_g, l)], big=[w_cat[l]], out_ch=proj_ch, ts=512),
            gdn_conv=dict(par_tiled=[gdn_conv_full[l]], out_ch=[768], ts=seq, nc=3),
            gdn_local=dict(par=[row(gdn_a_log, l), row(gdn_dt_bias, l)], out_ch=[256] * 6, ts=512),
            rglru=dict(par_tiled=[rg_conv_full[l], row(rg_conv_b, l), row(rg_b_a, l), row(rg_b_x, l), row(rg_lambda, l),
                                  bd_a[l], bd_x[l]], out_ch=[512], ts=seq, nc=4),
            mla_pre=dict(tok_nd=[cs], par=[row(mla_q_norm_g, l), row(mla_kv_norm_g, l), w_q[l], w_kv[l]],
                         out_ch=[256, 128, 256, 256, 128], ts=512),
            out_proj=dict(ex=[gt_m], big=[w_out_full[l]], out_ch=[d], ts=512),
            mlp_in=dict(ex=[sc_f, sh_f], par=[row(norm_mlp_g, l)], big=[slabs['w_mlp_in'][l]], out_ch=[4 * d], ts=256),
            mlp_out=dict(ex=[gt_f], big=[w_mlp_out_full[l]], out_ch=[d], ts=256),
        )

    saved = []
    h = x
    for l in range(depth):
        a = layer_args(l)
        sfx = str(l)
        qkv_raw, z, rx, rgate, mq, mkv, misc = run_stage("mixer_in" + sfx, fn_mixer_in, tok=[h], **a['mixer_in'])
        (qkv_act,) = run_stage("gdn_conv" + sfx, fn_gdn_conv, tok=[qkv_raw], **a['gdn_conv'])
        xs = run_stage("gdn_local" + sfx, fn_gdn_local, tok=[qkv_act, misc], **a['gdn_local'])
        o_a, st_in = gdn_scan(xs, z, row(gdn_norm_g, l))
        (o_b,) = run_stage("rglru" + sfx, fn_rglru, tok=[rx, rgate], **a['rglru'])
        qn, qp, kn, vv, kp = run_stage("mla_pre" + sfx, fn_mla_pre, tok=[mq, mkv, misc], **a['mla_pre'])
        o_c = mla_attention(qn, qp, kn, kp, vv)
        (h_mid,) = run_stage("out_proj" + sfx, fn_out_proj, tok=[h, o_a, o_b, o_c], **a['out_proj'])
        (a_mlp,) = run_stage("mlp_in" + sfx, fn_mlp_in, tok=[h_mid], **a['mlp_in'])
        (h_out,) = run_stage("mlp_out" + sfx, fn_mlp_out, tok=[h_mid, a_mlp], **a['mlp_out'])
        saved.append(dict(h=h, qkv_raw=qkv_raw, z=z, rx=rx, rgate=rgate, mq=mq, mkv=mkv, misc=misc, qkv_act=qkv_act, xs=xs,
                          st_in=st_in, o_a=o_a, o_b=o_b, o_c=o_c, qn=qn, qp=qp, kn=kn, vv=vv, kp=kp, h_mid=h_mid, a_mlp=a_mlp))
        h = h_out

    loss_part, dh, d_final_g = loss_head(h, final_norm_g.reshape(1, d), loss_target)
    loss = lax.psum(loss_part[0, 0], ("x", "y", "c"))

    g_full = {n: [None] * depth for n in SHARDED}
    g_rep = {n: [None] * depth for n in REPLICATED if n not in ('final_norm_g', 'b_mod')}

    def column_slabs(g):
        return g.reshape(g.shape[0], N_CHIP, g.shape[1] // N_CHIP).transpose(1, 0, 2)

    def row_slabs(g):
        return g.reshape(N_CHIP, g.shape[0] // N_CHIP, g.shape[1])
    dmod = [None] * depth
    for l in reversed(range(depth)):
        a, sv = layer_args(l), saved[l]
        sfx = str(l)
        mlp_out_tok = dict(tok=[sv['h_mid'], sv['a_mlp']], cot=[dh])
        (dh_mid, da_mlp), (dgt_f,), _, _, _ = run_stage(
            "mlp_out" + sfx, fn_mlp_out, which="small", **mlp_out_tok, **{**a['mlp_out'], 'ts': 256})
        _, _, _, _, (dw_mlp_out,) = run_stage(
            "mlp_out" + sfx, fn_mlp_out, which="big", **mlp_out_tok, **{**a['mlp_out'], 'ts': 512})
        g_full['w_mlp_out'][l] = row_slabs(dw_mlp_out)
        (dh_mid,), (dsc_f, dsh_f), (g_rep['norm_mlp_g'][l],), _, (g_full['w_mlp_in'][l],) = run_stage(
            "mlp_in" + sfx, fn_mlp_in, tok=[sv['h_mid']], cot=[da_mlp], addin=dh_mid, **a['mlp_in'])
        (dh_in, do_a, do_b, do_c), (dgt_m,), _, _, (dw_out,) = run_stage(
            "out_proj" + sfx, fn_out_proj, tok=[sv['h'], sv['o_a'], sv['o_b'], sv['o_c']], cot=[dh_mid], **a['out_proj'])
        g_full['w_out'][l] = row_slabs(dw_out)
        dqn, dqp, dkn, dkp, dvv = mla_attention_bwd(sv['qn'], sv['qp'], sv['kn'], sv['kp'], sv['vv'], do_c)
        (dmq, dmkv, dmisc_c), _, (g_rep['mla_q_norm_g'][l], g_rep['mla_kv_norm_g'][l], dw_q, dw_kv), _, _ = run_stage(
            "mla_pre" + sfx, fn_mla_pre, tok=[sv['mq'], sv['mkv'], sv['misc']], cot=[dqn, dqp, dkn, dvv, dkp], **a['mla_pre'])
        g_full['mla_w_qb'][l] = column_slabs(_heads_merge(dw_q, MLA_HEADS, 64))
        g_full['mla_w_kvb'][l] = column_slabs(_heads_merge(dw_kv, MLA_HEADS, 64))
        (drx, drgate), _, _, rg_g, _ = run_stage("rglru" + sfx, fn_rglru, tok=[sv['rx'], sv['rgate']], cot=[do_b], **a['rglru'])
        (g_full['rg_conv_w'][l], g_rep['rg_conv_b'][l], g_rep['rg_b_a'][l], g_rep['rg_b_x'][l], g_rep['rg_lambda'][l],
         g_rep['rg_w_a'][l], g_rep['rg_w_x'][l]) = rg_g
        dxs, dz, g_rep['gdn_norm_g'][l] = gdn_scan_bwd(sv['xs'], sv['z'], row(gdn_norm_g, l), sv['st_in'], do_a)
        (dqkv_act, dmisc_a), _, (g_rep['gdn_a_log'][l], g_rep['gdn_dt_bias'][l]), _, _ = run_stage(
            "gdn_local" + sfx, fn_gdn_local, tok=[sv['qkv_act'], sv['misc']], cot=dxs, **a['gdn_local'])
        (dqkv_raw,), _, _, (g_full['gdn_conv_w'][l],), _ = run_stage(
            "gdn_conv" + sfx, fn_gdn_conv, tok=[sv['qkv_raw']], cot=[dqkv_act], **a['gdn_conv'])
        (dh,), (dsc_m, dsh_m), (g_rep['norm_mix_g'][l],), _, (dw_cat,) = run_stage(
            "mixer_in" + sfx, fn_mixer_in, tok=[sv['h']], cot=[dqkv_raw, dz, drx, drgate, dmq, dmkv, dmisc_a + dmisc_c],
            addin=dh_in, **a['mixer_in'])
        g_full['w_in'][l] = column_slabs(_proj_cols_back(dw_cat))
        dmod[l] = jnp.concatenate([dsh_m, dsc_m, dgt_m, dsh_f, dsc_f, dgt_f], axis=-1).reshape(bsz, 6 * d)
    grad_x = dh

    dmod = jnp.stack(dmod)
    dmod_pad = jnp.concatenate([dmod.reshape(depth * bsz, 6 * d), jnp.zeros((8 - depth * bsz, 6 * d), F32)], axis=0)
    dmod_all = all_gather8("gather_dmod", dmod_pad, True).reshape(N_DEV, 8, 6 * d)[:, :depth * bsz]
    dmod_all = dmod_all.reshape(N_DEV, depth, bsz, 6 * d).transpose(1, 0, 2, 3).reshape(depth, N_DEV * bsz, 6 * d)
    g_w_mod = mod_weight_grad(c_all, lax.dynamic_slice(dmod_all, (0, 0, chip * mod_cols), (depth, N_DEV * bsz, mod_cols)))

    g_rep = {n: jnp.stack(v) for n, v in g_rep.items()}
    g_rep['rg_w_a'] = _block_diag_back(g_rep['rg_w_a'])
    g_rep['rg_w_x'] = _block_diag_back(g_rep['rg_w_x'])
    g_rep['final_norm_g'] = d_final_g
    g_rep['b_mod'] = jnp.sum(dmod, axis=1)
    conv_names = ['gdn_conv_w', 'rg_conv_w']
    conv_full_shapes = [(depth,) + g_full[n][0].shape for n in conv_names]
    small_shapes = [wts[n].shape for n in REPLICATED] + conv_full_shapes
    rep_mult = 8 * PACK_COLS
    rep_part = _pack([g_rep[n].reshape(wts[n].shape) for n in REPLICATED] + [jnp.stack(g_full[n]) for n in conv_names],
                     rep_mult).reshape(-1, PACK_COLS)
    rep_rows = rep_part.shape[0]
    rep_all = all_gather8("gather_small_grads", rep_part, True).reshape(N_DEV, rep_rows, PACK_COLS)
    conv_zeros = [jnp.zeros(s, F32) for s in conv_full_shapes]
    rep_out = adamw_reduce("adamw_small", rep_all, *[
        _pack([src[n] for n in REPLICATED] + conv_zeros, rep_mult).reshape(-1, PACK_COLS) for src in (wts, mom_m, mom_v)])
    small_names = REPLICATED + conv_names
    rep_g, rep_d, rep_m, rep_v = [dict(zip(small_names, _unpack(o.reshape(-1), small_shapes))) for o in rep_out]
    sh_g = {}
    for n in conv_names:
        cols = wts[n].shape[2]
        sh_g[n] = lax.dynamic_slice(rep_g.pop(n), (0, 0, chip * cols), wts[n].shape)
        for dct in (rep_d, rep_m, rep_v):
            dct.pop(n)

    core_id = mc.reshape(1).astype(jnp.int32)
    chip_id = chip.reshape(1).astype(jnp.int32)
    units = [(i, l) for i in range(len(GATHER_BF16)) for l in range(depth)]
    gs = [g_full[GATHER_BF16[i]][l] for i, l in units]
    from_sibling = grad_sibling_exchange("grad_sibling_exchange", gs)
    sums32, sums16 = zip(*[add_half("grad_add_%s%d" % (GATHER_BF16[i], l), g, s, core_id)
                           for (i, l), g, s in zip(units, gs, from_sibling)])
    from_chips = grad_chip_exchange("grad_chip_exchange", list(sums16))
    reduced = [sum_peers("grad_sum_%s%d" % (GATHER_BF16[i], l), p, r, chip_id)
               for (i, l), p, r in zip(units, sums32, from_chips)]
    shards = grad_half_exchange("grad_half_exchange", reduced, units, [wts[n].shape for n in GATHER_BF16])
    sh_g.update(zip(GATHER_BF16, shards))
    sh_names = list(SHARDED)

    def as2d(t):
        return t.reshape(-1, t.shape[-1])

    sh_d, sh_m, sh_v = {}, {}, {}
    for n in sh_names + ['w_mod']:
        g = g_w_mod if n == 'w_mod' else sh_g[n]
        res = adamw("adamw_" + n, as2d(wts[n]), as2d(g), as2d(mom_m[n]), as2d(mom_v[n]))
        sh_d[n], sh_m[n], sh_v[n] = (r.reshape(wts[n].shape) for r in res)
    sh_g['w_mod'] = g_w_mod

    def pick(shd, rep):
        return [shd[n] if n in shd else rep[n] for n in WEIGHTS]

    return (loss, grad_x, *pick(sh_g, rep_g), *pick(sh_d, rep_d), *pick(sh_m, rep_m), *pick(sh_v, rep_v))
```

```python
import functools

import jax
import jax.numpy as jnp
import numpy as np
from jax import lax
from jax.experimental import pallas as pl
from jax.experimental.pallas import tpu as pltpu

F32, BF16 = jnp.float32, jnp.bfloat16
HI = lax.Precision.HIGH
MESH = pl.DeviceIdType.MESH

EPS = 1e-6
CHUNK = 64
GDN_HEADS = 4
MLA_HEADS = 4
RG_C = 8.0
ROPE_THETA = 10000.0
N_DEV = 8
N_CHIP = 4
V7X_VMEM_LIMIT = 60 * 1024 * 1024
ADAM_LR, ADAM_B1, ADAM_B2, ADAM_EPS, ADAM_WD, ADAM_STEP = 0.001, 0.9, 0.999, 1e-08, 0.01, 10


def _params(n_grid):
    return pltpu.CompilerParams(dimension_semantics=("arbitrary",) * n_grid, vmem_limit_bytes=V7X_VMEM_LIMIT)


def _dot(a, b, dims=(((1,), (0,)), ((), ()))):
    return lax.dot_general(a.astype(BF16), b.astype(BF16), dims, preferred_element_type=F32)


@jax.custom_vjp
def _mm_probe(x, w, probe):
    return _dot(x, w)


def _mm_probe_fwd(x, w, probe):
    return _dot(x, w), (x, w)


def _mm_probe_bwd(res, dy):
    x, w = res
    dx = _dot(dy, w, (((1,), (1,)), ((), ())))
    dw = _dot(x, dy, (((0,), (0,)), ((), ())))
    return dx, jnp.zeros_like(w), dw


_mm_probe.defvjp(_mm_probe_fwd, _mm_probe_bwd)


@jax.custom_vjp
def _probe_only(x, probe):
    return jnp.zeros((x.shape[0], probe.shape[1]), F32)


def _probe_only_fwd(x, probe):
    return jnp.zeros((x.shape[0], probe.shape[1]), F32), x


def _probe_only_bwd(x, dy):
    return jnp.zeros_like(x), _dot(x, dy, (((0,), (0,)), ((), ())))


_probe_only.defvjp(_probe_only_fwd, _probe_only_bwd)


@jax.custom_vjp
def mmw(x, w):
    return _dot(x, w)


def _mmw_fwd(x, w):
    return _dot(x, w), (x, w)


def _mmw_bwd(res, dy):
    x, w = res
    return _dot(dy, w, (((1,), (1,)), ((), ()))), _dot(x, dy, (((0,), (0,)), ((), ())))


mmw.defvjp(_mmw_fwd, _mmw_bwd)


def rms(x, g):
    return x * lax.rsqrt(jnp.mean(x * x, axis=-1, keepdims=True) + EPS) * g


def _rows(shape):
    return lax.broadcasted_iota(jnp.int32, shape, 0)


def _shift_down(x, s, fill):
    return jnp.where(_rows(x.shape) < s, fill, pltpu.roll(x, s, 0))


def _shift_up(x, s, fill):
    n = x.shape[0]
    return jnp.where(_rows(x.shape) >= n - s, fill, pltpu.roll(x, n - s, 0))


def _make_tshift(s):
    @jax.custom_vjp
    def tshift(x):
        return _shift_down(x, s, 0.0)

    tshift.defvjp(lambda x: (_shift_down(x, s, 0.0), None), lambda _, dy: (_shift_up(dy, s, 0.0),))
    return tshift


_TSHIFT = {s: _make_tshift(s) for s in (1, 2, 3)}


def causal_conv4(x, w):
    y = x * w[3:4, :]
    for j in range(3):
        y = y + _TSHIFT[3 - j](x) * w[j:j + 1, :]
    return y


def _scan_steps(n):
    d = 1
    while d < n:
        yield d
        d *= 2


@jax.custom_vjp
def linscan(a, b):
    return _linscan_fwd_impl(a, b)


def _linscan_fwd_impl(a, b):
    for d in _scan_steps(a.shape[0]):
        b = a * _shift_down(b, d, 0.0) + b
        a = a * _shift_down(a, d, 1.0)
    return b


def _linscan_fwd(a, b):
    h = _linscan_fwd_impl(a, b)
    return h, (a, h)


def _linscan_bwd(res, dh):
    a, h = res
    an = _shift_up(a, 1, 0.0)
    lam = dh
    for d in _scan_steps(a.shape[0]):
        lam = an * _shift_up(lam, d, 0.0) + lam
        an = an * _shift_up(an, d, 1.0)
    return lam * _shift_down(h, 1, 0.0), lam


linscan.defvjp(_linscan_fwd, _linscan_bwd)


def _bmm(a, b, precision=None):
    return jnp.einsum('nij,njk->nik', a, b, precision=precision, preferred_element_type=F32)


@jax.custom_vjp
def inv_unit_lower(l):
    return _inv_impl(l)


def _inv_impl(l):
    n = l.shape[-1]
    eye = (_rows((n, n)) == lax.broadcasted_iota(jnp.int32, (n, n), 1)).astype(F32)
    p = -l
    a = eye + p
    k = 1
    while 2 * k < n:
        p = _bmm(p, p, HI)
        a = a + _bmm(a, p, HI)
        k *= 2
    return a


def _inv_fwd(l):
    a = _inv_impl(l)
    return a, a


def _inv_bwd(a, da):
    at = jnp.swapaxes(a, 1, 2)
    return (-_bmm(_bmm(at, da, HI), at, HI),)


inv_unit_lower.defvjp(_inv_fwd, _inv_bwd)


def neg_expm1(y):
    series = -(y * (1.0 + y * (0.5 + y * (1.0 / 6.0 + y * (1.0 / 24.0)))))
    return jnp.where(y > -0.05, series, 1.0 - jnp.exp(y))


def run_stage(name, fn, *, tok, tok_nd=(), ex=(), par=(), par_tiled=(), big=(), out_ch, ts, nc=1, cot=None, addin=None,
              which="all"):
    tok, tok_nd, ex, par, par_tiled, big = map(list, (tok, tok_nd, ex, par, par_tiled, big))
    bsz, seq, _ = tok[0].shape
    ts = min(ts, seq)
    ns = seq // ts
    grid = (nc, bsz, ns)

    def tok_spec(a):
        cb = a.shape[-1] // nc
        return pl.BlockSpec((None, ts, cb), lambda c, b, s: (b, s, c))

    def ex_spec(a):
        cb = a.shape[-1] // nc
        return pl.BlockSpec((None, 1, cb), lambda c, b, s: (b, 0, c))

    def full_spec(a, single=False):
        nd = a.ndim
        kw = dict(pipeline_mode=pl.Buffered(1)) if single else {}
        return pl.BlockSpec(a.shape, lambda c, b, s: (0,) * nd, **kw)

    def tiled_spec(a):
        if a.ndim == 2:
            return pl.BlockSpec((a.shape[0], a.shape[1] // nc), lambda c, b, s: (0, c))
        return pl.BlockSpec((None,) + a.shape[1:], lambda c, b, s: (c, 0, 0))

    n_tok, n_nd, n_ex, n_par, n_pt, n_big = map(len, (tok, tok_nd, ex, par, par_tiled, big))
    in_arrays = tok + tok_nd + ex + par + par_tiled + big
    in_specs = ([tok_spec(a) for a in tok + tok_nd] + [ex_spec(a) for a in ex] + [full_spec(a) for a in par]
                + [tiled_spec(a) for a in par_tiled] + [full_spec(a, True) for a in big])
    out_tok_shapes = [jax.ShapeDtypeStruct((bsz, seq, ch), F32) for ch in out_ch]
    n_in = len(in_arrays)

    def split(vals):
        i = 0
        groups = []
        for n in (n_tok, n_nd, n_ex, n_par, n_pt, n_big):
            groups.append(list(vals[i:i + n]))
            i += n
        return groups

    def split_grads(vals):
        i = 0
        groups = []
        for n in (n_tok, n_ex, n_par, n_pt, n_big):
            groups.append(list(vals[i:i + n]))
            i += n
        return groups

    if cot is None:
        def body(*refs):
            tv, ndv, ev, pv, ptv, _ = split([r[...] for r in refs[:n_in - n_big]] + [None] * n_big)
            b_refs = refs[n_in - n_big:n_in]
            outs = fn(tv, ndv, ev, pv, ptv, lambda x, i, j=None: _dot(x, b_refs[i][...] if j is None else b_refs[i][j]))
            for r, o in zip(refs[n_in:], outs):
                r[...] = o

        return pl.pallas_call(
            body, name=name, grid=grid, in_specs=in_specs, out_specs=[tok_spec(a) for a in out_tok_shapes],
            out_shape=out_tok_shapes, compiler_params=_params(3))(*in_arrays)

    cot = list(cot)
    has_addin = addin is not None
    extra = cot + ([addin] if has_addin else [])
    n_cot = len(cot)
    want_small, want_big = which in ("all", "small"), which in ("all", "big")
    if not want_small:
        in_arrays, in_specs, n_in = in_arrays[:n_in - n_big], in_specs[:n_in - n_big], n_in - n_big
    small_arrays = tok + ex + par + par_tiled
    g_shapes = [jax.ShapeDtypeStruct(a.shape, F32) for a in (small_arrays if want_small else []) + (big if want_big else [])]
    g_specs = (([tok_spec(a) for a in tok] + [ex_spec(a) for a in ex] + [full_spec(a) for a in par]
                + [tiled_spec(a) for a in par_tiled]) if want_small else []) + (
                    [full_spec(a, True) for a in big] if want_big else [])

    def body(*refs):
        c, b, s = pl.program_id(0), pl.program_id(1), pl.program_id(2)
        n_small_in = n_tok + n_nd + n_ex + n_par + n_pt
        tv, ndv, ev, pv, ptv, _ = split([r[...] for r in refs[:n_small_in]] + [None] * n_big)
        b_refs = refs[n_small_in:n_in]
        cots = [r[...] for r in refs[n_in:n_in + n_cot]]
        g_refs = list(refs[n_in + len(extra):])
        probes = [jnp.zeros(w.shape, F32) if w.ndim == 2 else [jnp.zeros(w.shape[1:], F32) for _ in range(w.shape[0])]
                  for w in big]

        def f(tv_, ev_, pv_, ptv_, probes_):
            def mm(x, i, j=None):
                probe = None if probes_ is None else (probes_[i] if j is None else probes_[i][j])
                if not want_small:
                    return _probe_only(x, probe)
                w = b_refs[i][...] if j is None else b_refs[i][j]
                return _dot(x, w) if probe is None else _mm_probe(x, w, probe)

            return fn(tv_, ndv, ev_, pv_, ptv_, mm)

        dt = de = dp = dpt = dbg = ()
        if which == "all":
            dt, de, dp, dpt, dbg = jax.vjp(f, tv, ev, pv, ptv, probes)[1](cots)
        elif which == "small":
            dt, de, dp, dpt = jax.vjp(lambda *a: f(*a, None), tv, ev, pv, ptv)[1](cots)
        else:
            (dbg,) = jax.vjp(lambda p: f(tv, ev, pv, ptv, p), probes)[1](cots)
        if has_addin:
            dt = [dt[0] + refs[n_in + n_cot][...]] + list(dt[1:])
        if want_small:
            gt_r, ge_r, gp_r, gpt_r, gb_r = split_grads(g_refs + ([] if want_big else [None] * n_big))
        else:
            gt_r, ge_r, gp_r, gpt_r, gb_r = [], [], [], [], g_refs
        for r, g in zip(gt_r, dt):
            r[...] = g

        def accumulate(r, g, first):
            @pl.when(first)
            def _():
                r[...] = g

            @pl.when(jnp.logical_not(first))
            def _():
                r[...] += g

        for r, g in zip(ge_r, de):
            accumulate(r, g, s == 0)
        first_all = jnp.logical_and(jnp.logical_and(c == 0, b == 0), s == 0)
        for r, g in zip(gp_r, dp):
            accumulate(r, g, first_all)
        for r, g in zip(gpt_r, dpt):
            accumulate(r, g, jnp.logical_and(b == 0, s == 0))
        for r, g in zip(gb_r, dbg):
            if isinstance(g, (list, tuple)):
                for j, gj in enumerate(g):
                    accumulate(r.at[j], gj, first_all)
            else:
                accumulate(r, g, first_all)

    res = pl.pallas_call(
        body, name=name + "_bwd" + ("" if which == "all" else "_" + which), grid=grid,
        in_specs=in_specs + [tok_spec(a) for a in extra], out_specs=g_specs, out_shape=g_shapes,
        compiler_params=_params(3))(*in_arrays, *extra)
    res = list(res)
    if not want_small:
        return [[], [], [], [], res]
    return split_grads(res + ([] if want_big else [None] * n_big))


PROJ_PIECES = (("qkv", 768), ("z", 256), ("rx", 512), ("rgate", 512), ("mq", 256), ("mkv", 128), ("misc", 128))
PROJ_WIDTH = sum(w for _, w in PROJ_PIECES)
MISC_KR, MISC_A, MISC_B = 0, 32, 36


def fn_mixer_in(tok, nd, ex, par, pt, mm):
    (h,), (sc, sh), (g,) = tok, ex, par
    proj = mm(rms(h, g) * (1.0 + sc) + sh, 0)
    outs, o = [], 0
    for _, w in PROJ_PIECES:
        outs.append(proj[:, o:o + w])
        o += w
    return outs


def fn_gdn_conv(tok, nd, ex, par, pt, mm):
    return [jax.nn.silu(causal_conv4(tok[0], pt[0]))]


def _tri_masks():
    r = _rows((CHUNK, CHUNK))
    c = lax.broadcasted_iota(jnp.int32, (CHUNK, CHUNK), 1)
    return (c <= r), (c < r)


def fn_gdn_local(tok, nd, ex, par, pt, mm):
    (qkv, misc), (a_log, dt_bias) = tok, par
    ts = qkv.shape[0]
    nb = ts // CHUNK
    lower, strict = _tri_masks()
    tril = jnp.broadcast_to(lower.astype(F32), (nb, CHUNK, CHUNK))
    ones = jnp.ones((nb, CHUNK, CHUNK), F32)
    outs = [[] for _ in range(6)]
    for hd in range(GDN_HEADS):
        def head(x, base):
            return x[:, base + 64 * hd: base + 64 * hd + 64]

        def l2n(x):
            return x * lax.rsqrt(jnp.sum(x * x, axis=-1, keepdims=True) + EPS)

        q = (l2n(head(qkv, 0)) * (64.0 ** -0.5)).reshape(nb, CHUNK, 64)
        k = l2n(head(qkv, 256)).reshape(nb, CHUNK, 64)
        v = head(qkv, 512).reshape(nb, CHUNK, 64)
        a = misc[:, MISC_A + hd: MISC_A + hd + 1]
        b = misc[:, MISC_B + hd: MISC_B + hd + 1]
        g = -jnp.exp(a_log[:, hd:hd + 1]) * jax.nn.softplus(a + dt_bias[:, hd:hd + 1])
        beta = jax.nn.sigmoid(b).reshape(nb, CHUNK, 1)
        gb = jnp.broadcast_to(g.reshape(nb, CHUNK, 1), (nb, CHUNK, CHUNK))
        gi = _bmm(tril, gb, HI)
        gl = _bmm(ones, gb, HI)
        diff = gi - jnp.swapaxes(gi, 1, 2)
        decay = jnp.where(lower, jnp.exp(jnp.where(lower, diff, 0.0)), 0.0)
        kb = k * beta
        vb = v * beta
        kk = jnp.einsum('ncd,nsd->ncs', kb.astype(BF16), k.astype(BF16), preferred_element_type=F32)
        amat = inv_unit_lower(jnp.where(strict, kk * decay, 0.0))
        eg = jnp.exp(gi)
        u = _bmm(amat, vb, HI)
        w = _bmm(amat, kb * eg, HI)
        qk = jnp.einsum('ncd,nsd->ncs', q.astype(BF16), k.astype(BF16), preferred_element_type=F32) * decay
        qd = q * eg
        kt = k * jnp.exp(gl - gi)
        cd = jnp.exp(gl)
        for lst, val in zip(outs, (qk, qd, u, w, kt, cd)):
            lst.append(val.reshape(ts, 64))
    return [jnp.concatenate(lst, axis=-1) for lst in outs]


def fn_rglru(tok, nd, ex, par, pt, mm):
    (rx, rgate), (conv_w, conv_b, b_a, b_x, lam, bd_a, bd_x) = tok, pt
    xc = causal_conv4(rx, conv_w) + conv_b
    r = jax.nn.sigmoid(mmw(xc, bd_a) + b_a)
    i = jax.nn.sigmoid(mmw(xc, bd_x) + b_x)
    log_a = -RG_C * r * jax.nn.softplus(-lam)
    a = jnp.exp(log_a)
    bterm = jnp.sqrt(neg_expm1(2.0 * log_a)) * (i * xc)
    return [linscan(a, bterm) * jax.nn.gelu(rgate)]


def _rope32(x, cos, sin):
    x1, x2 = x[:, :16], x[:, 16:32]
    return jnp.concatenate([x1 * cos - x2 * sin, x2 * cos + x1 * sin], axis=-1)


def fn_mla_pre(tok, nd, ex, par, pt, mm):
    (mq, mkv, misc), (cs,), (g_q, g_kv, w_q, w_kv) = tok, nd, par
    q = mmw(rms(mq, g_q), w_q)
    kv = mmw(rms(mkv, g_kv), w_kv)
    cos, sin = cs[:, 0:16], cs[:, 16:32]
    qp = jnp.concatenate([_rope32(q[:, 256 + 32 * h: 288 + 32 * h], cos, sin) for h in range(MLA_HEADS)], axis=-1)
    kp = _rope32(misc[:, MISC_KR:MISC_KR + 32], cos, sin)
    kp = jnp.concatenate([kp, jnp.zeros((kp.shape[0], 96), F32)], axis=-1)
    return [q[:, 0:256], qp, kv[:, 0:256], kv[:, 256:512], kp]


def fn_out_proj(tok, nd, ex, par, pt, mm):
    (h, o_a, o_b, o_c), (gt,) = tok, ex
    return [h + gt * mm(jnp.concatenate([o_a, o_b, o_c], axis=-1), 0)]


def fn_mlp_in(tok, nd, ex, par, pt, mm):
    (h,), (sc, sh), (g,) = tok, ex, par
    u = rms(h, g) * (1.0 + sc) + sh
    return [jnp.concatenate([mm(u, 0, j) for j in range(N_CHIP)], axis=-1)]


def fn_mlp_out(tok, nd, ex, par, pt, mm):
    (h, a), (gt,) = tok, ex
    return [h + gt * mm(jnp.square(jax.nn.relu(a)), 0)]


def _gdn_step(states, qk, qd, u, w, kt, cd, z, norm_g):
    new_states, outs = [], []
    for hd in range(GDN_HEADS):
        sl = slice(64 * hd, 64 * hd + 64)
        st = states[hd]
        v_new = u[:, sl] - _dot(w[:, sl], st)
        o = _dot(qd[:, sl], st) + _dot(qk[:, sl], v_new)
        new_states.append(st * cd[:, sl] + _dot(kt[:, sl], v_new, (((0,), (0,)), ((), ()))))
        outs.append(rms(o, norm_g) * jax.nn.silu(z[:, sl]))
    return new_states, jnp.concatenate(outs, axis=-1)


def gdn_scan(xs, z, norm_g):
    bsz, seq, _ = z.shape
    n = seq // CHUNK
    blk = pl.BlockSpec((bsz, CHUNK, 256), lambda i: (0, i, 0))

    def body(qk, qd, u, w, kt, cd, z_ref, g_ref, o_ref, st_out, st):
        @pl.when(pl.program_id(0) == 0)
        def _():
            st[...] = jnp.zeros_like(st)

        for b in range(bsz):
            states = [st[b * GDN_HEADS + hd] for hd in range(GDN_HEADS)]
            st_out[b] = jnp.concatenate(states, axis=-1)
            new_states, o = _gdn_step(states, qk[b], qd[b], u[b], w[b], kt[b], cd[b], z_ref[b], g_ref[...])
            o_ref[b] = o
            for hd in range(GDN_HEADS):
                st[b * GDN_HEADS + hd] = new_states[hd]

    return pl.pallas_call(
        body, name="gdn_scan", grid=(n,), in_specs=[blk] * 7 + [pl.BlockSpec((1, 64), lambda i: (0, 0))],
        out_specs=[blk, blk], out_shape=[jax.ShapeDtypeStruct((bsz, seq, 256), F32)] * 2,
        scratch_shapes=[pltpu.VMEM((bsz * GDN_HEADS, 64, 64), F32)], compiler_params=_params(1))(*xs, z, norm_g)


def gdn_scan_bwd(xs, z, norm_g, st_in, do):
    bsz, seq, _ = z.shape
    n = seq // CHUNK
    blk = pl.BlockSpec((bsz, CHUNK, 256), lambda i: (0, n - 1 - i, 0))
    gspec = pl.BlockSpec((1, 64), lambda i: (0, 0))

    def body(qk, qd, u, w, kt, cd, z_ref, g_ref, st_ref, do_ref, dqk, dqd, du, dw, dkt, dcd, dz, dg, dst):
        first = pl.program_id(0) == 0

        @pl.when(first)
        def _():
            dst[...] = jnp.zeros_like(dst)

        dg_sum = None
        for b in range(bsz):
            stv = st_ref[b]
            states = [stv[:, 64 * hd: 64 * hd + 64] for hd in range(GDN_HEADS)]
            _, vjp = jax.vjp(_gdn_step, states, qk[b], qd[b], u[b], w[b], kt[b], cd[b], z_ref[b], g_ref[...])
            grads = vjp(([dst[b * GDN_HEADS + hd] for hd in range(GDN_HEADS)], do_ref[b]))
            for hd in range(GDN_HEADS):
                dst[b * GDN_HEADS + hd] = grads[0][hd]
            for r, g in zip((dqk, dqd, du, dw, dkt, dcd, dz), grads[1:8]):
                r[b] = g
            dg_sum = grads[8] if dg_sum is None else dg_sum + grads[8]

        @pl.when(first)
        def _():
            dg[...] = dg_sum

        @pl.when(jnp.logical_not(first))
        def _():
            dg[...] += dg_sum

    res = pl.pallas_call(
        body, name="gdn_scan_bwd", grid=(n,), in_specs=[blk] * 7 + [gspec, blk, blk],
        out_specs=[blk] * 7 + [gspec], out_shape=[jax.ShapeDtypeStruct((bsz, seq, 256), F32)] * 7
        + [jax.ShapeDtypeStruct((1, 64), F32)],
        scratch_shapes=[pltpu.VMEM((bsz * GDN_HEADS, 64, 64), F32)], compiler_params=_params(1))(*xs, z, norm_g, st_in, do)
    return list(res[:6]), res[6], res[7]


ATTN_TQ = 256
ATTN_SCALE = 96.0 ** -0.5


def _attn_head(qn, qp, kn, kp, v, q0):
    nt = (((1,), (1,)), ((), ()))
    s = (_dot(qn, kn, nt) + _dot(qp, kp, nt)) * ATTN_SCALE
    qc = (q0 + _rows(s.shape)) // CHUNK
    kc = lax.broadcasted_iota(jnp.int32, s.shape, 1) // CHUNK
    s = jnp.where(kc <= qc, s, -1e30)
    p = jnp.exp(s - jnp.max(s, axis=-1, keepdims=True))
    p = p / jnp.sum(p, axis=-1, keepdims=True)
    return _dot(p, v)


def _key_lengths(seq):
    n_var = min(2, seq // ATTN_TQ)
    return [(j + 1) * (seq // n_var) for j in range(n_var)]


def _key_variant(i, seq):
    return ((i + 1) * ATTN_TQ - 1) // _key_lengths(seq)[0]


def mla_attention(qn, qp, kn, kp, v):
    bsz, seq, _ = qn.shape
    nq = seq // ATTN_TQ

    def qspec(ch):
        return pl.BlockSpec((None, ATTN_TQ, ch), lambda b, i: (b, i, 0))

    def kspec(ch):
        return pl.BlockSpec((None, seq, ch), lambda b, i: (b, 0, 0))

    def body(qn_r, qp_r, kn_r, kp_r, v_r, o_r):
        i = pl.program_id(1)
        q0 = i * ATTN_TQ

        def with_keys(klen):
            kpv = kp_r[0:klen, 0:32]
            outs = []
            for h in range(MLA_HEADS):
                sl = slice(64 * h, 64 * h + 64)
                outs.append(_attn_head(qn_r[:, sl], qp_r[:, 32 * h: 32 * h + 32], kn_r[0:klen, sl], kpv,
                                       v_r[0:klen, sl], q0))
            o_r[...] = jnp.concatenate(outs, axis=-1)

        for j, klen in enumerate(_key_lengths(seq)):
            pl.when(_key_variant(i, seq) == j)(functools.partial(with_keys, klen))

    return pl.pallas_call(
        body, name="mla_attention", grid=(bsz, nq), in_specs=[qspec(256), qspec(128), kspec(256), kspec(128), kspec(256)],
        out_specs=qspec(256), out_shape=jax.ShapeDtypeStruct((bsz, seq, 256), F32), compiler_params=_params(2))(
            qn, qp, kn, kp, v)


def mla_attention_bwd(qn, qp, kn, kp, v, do):
    bsz, seq, _ = qn.shape
    nq = seq // ATTN_TQ

    def qspec(ch):
        return pl.BlockSpec((None, ATTN_TQ, ch), lambda b, i: (b, i, 0))

    def kspec(ch):
        return pl.BlockSpec((None, seq, ch), lambda b, i: (b, 0, 0))

    def body(qn_r, qp_r, kn_r, kp_r, v_r, do_r, dqn_r, dqp_r, dkn_r, dkp_r, dv_r):
        i = pl.program_id(1)
        q0 = i * ATTN_TQ

        @pl.when(i == 0)
        def _():
            dkn_r[...] = jnp.zeros_like(dkn_r)
            dkp_r[...] = jnp.zeros_like(dkp_r)
            dv_r[...] = jnp.zeros_like(dv_r)

        def with_keys(klen):
            kpv = kp_r[0:klen, 0:32]
            dqn, dqp, dkn, dv = [], [], [], []
            dkp = jnp.zeros((klen, 32), F32)
            for h in range(MLA_HEADS):
                sl = slice(64 * h, 64 * h + 64)
                _, vjp = jax.vjp(functools.partial(_attn_head, q0=q0), qn_r[:, sl], qp_r[:, 32 * h: 32 * h + 32],
                                 kn_r[0:klen, sl], kpv, v_r[0:klen, sl])
                a, b, c, d, e = vjp(do_r[:, sl])
                dqn.append(a)
                dqp.append(b)
                dkn.append(c)
                dkp = dkp + d
                dv.append(e)
            dqn_r[...] = jnp.concatenate(dqn, axis=-1)
            dqp_r[...] = jnp.concatenate(dqp, axis=-1)
            dkn_r[0:klen, :] += jnp.concatenate(dkn, axis=-1)
            dv_r[0:klen, :] += jnp.concatenate(dv, axis=-1)
            dkp_r[0:klen, 0:32] += dkp

        for j, klen in enumerate(_key_lengths(seq)):
            pl.when(_key_variant(i, seq) == j)(functools.partial(with_keys, klen))

    shp = lambda ch: jax.ShapeDtypeStruct((bsz, seq, ch), F32)
    return pl.pallas_call(
        body, name="mla_attention_bwd", grid=(bsz, nq),
        in_specs=[qspec(256), qspec(128), kspec(256), kspec(128), kspec(256), qspec(256)],
        out_specs=[qspec(256), qspec(128), kspec(256), kspec(128), kspec(256)],
        out_shape=[shp(256), shp(128), shp(256), shp(128), shp(256)], compiler_params=_params(2))(qn, qp, kn, kp, v, do)


LOSS_TS = 512


def loss_head(h, g, target):
    bsz, seq, d = h.shape
    ts = min(LOSS_TS, seq)
    tok = pl.BlockSpec((None, ts, d), lambda b, s: (b, s, 0))
    gspec = pl.BlockSpec((1, d), lambda b, s: (0, 0))
    lspec = pl.BlockSpec((1, 128), lambda b, s: (0, 0))

    def body(h_r, g_r, t_r, loss_r, dh_r, dg_r):
        first = jnp.logical_and(pl.program_id(0) == 0, pl.program_id(1) == 0)
        tv = t_r[...]

        def f(hv, gv):
            return 0.5 * jnp.sum(jnp.mean(jnp.square(rms(hv, gv) - tv), axis=-1, keepdims=True), axis=0, keepdims=True)

        val, vjp = jax.vjp(f, h_r[...], g_r[...])
        dh, dg = vjp(jnp.ones((1, 1), F32))
        dh_r[...] = dh
        lv = jnp.broadcast_to(val, (1, 128))

        @pl.when(first)
        def _():
            loss_r[...] = lv
            dg_r[...] = dg

        @pl.when(jnp.logical_not(first))
        def _():
            loss_r[...] += lv
            dg_r[...] += dg

    return pl.pallas_call(
        body, name="loss_head", grid=(bsz, seq // ts), in_specs=[tok, gspec, tok], out_specs=[lspec, tok, gspec],
        out_shape=[jax.ShapeDtypeStruct((1, 128), F32), jax.ShapeDtypeStruct(h.shape, F32), jax.ShapeDtypeStruct((1, d), F32)],
        compiler_params=_params(2))(h, g, target)


def _adamw_math(w, g, m, v):
    m = ADAM_B1 * m + (1.0 - ADAM_B1) * g
    v = ADAM_B2 * v + (1.0 - ADAM_B2) * jnp.square(g)
    m_hat = m / (1.0 - ADAM_B1 ** ADAM_STEP)
    v_hat = v / (1.0 - ADAM_B2 ** ADAM_STEP)
    return -ADAM_LR * (m_hat / (jnp.sqrt(v_hat) + ADAM_EPS) + ADAM_WD * w), m, v


def _row_block(rows, cols):
    want = max(8, (1 << 18) // cols)
    best = rows
    for r in range(8, rows + 1, 8):
        if rows % r == 0 and r <= want:
            best = r
    return best if rows % 8 == 0 else rows


def adamw(name, w, g, m, v):
    rows, cols = w.shape
    rb = _row_block(rows, cols)
    spec = pl.BlockSpec((rb, cols), lambda i: (i, 0))

    def body(w_r, g_r, m_r, v_r, d_o, m_o, v_o):
        d, mn, vn = _adamw_math(w_r[...], g_r[...], m_r[...], v_r[...])
        d_o[...] = d
        m_o[...] = mn
        v_o[...] = vn

    return pl.pallas_call(body, name=name, grid=(rows // rb,), in_specs=[spec] * 4, out_specs=[spec] * 3,
                          out_shape=[jax.ShapeDtypeStruct(w.shape, F32)] * 3, compiler_params=_params(1))(w, g, m, v)


def adamw_reduce(name, parts, w, m, v):
    rows, cols = w.shape
    rb = _row_block(rows, cols)
    spec = pl.BlockSpec((rb, cols), lambda i: (i, 0))
    pspec = pl.BlockSpec((N_DEV, rb, cols), lambda i: (0, i, 0))

    def body(p_r, w_r, m_r, v_r, g_o, d_o, m_o, v_o):
        g = p_r[0]
        for k in range(1, N_DEV):
            g = g + p_r[k]
        d, mn, vn = _adamw_math(w_r[...], g, m_r[...], v_r[...])
        g_o[...] = g
        d_o[...] = d
        m_o[...] = mn
        v_o[...] = vn

    return pl.pallas_call(body, name=name, grid=(rows // rb,), in_specs=[pspec, spec, spec, spec], out_specs=[spec] * 4,
                          out_shape=[jax.ShapeDtypeStruct(w.shape, F32)] * 4, compiler_params=_params(1))(parts, w, m, v)


MOD_CB = 512


def mod_matmul(c_rows, w_mod, b_mod):
    nl, d, cols = w_mod.shape

    def body(c_r, w_r, b_r, o_r):
        o_r[...] = _dot(jax.nn.silu(c_r[...]), w_r[...]) + b_r[...]

    return pl.pallas_call(
        body, name="mod_matmul", grid=(nl, cols // MOD_CB),
        in_specs=[pl.BlockSpec((8, d), lambda l, j: (0, 0)), pl.BlockSpec((None, d, MOD_CB), lambda l, j: (l, 0, j)),
                  pl.BlockSpec((None, 1, MOD_CB), lambda l, j: (l, 0, j))],
        out_specs=pl.BlockSpec((None, 8, MOD_CB), lambda l, j: (l, 0, j)),
        out_shape=jax.ShapeDtypeStruct((nl, 8, cols), F32), compiler_params=_params(2))(c_rows, w_mod, b_mod)


def mod_weight_grad(c_all, dmod):
    nl, nb, cols = dmod.shape
    d = c_all.shape[1]

    def body(c_r, g_r, o_r):
        o_r[...] = _dot(jax.nn.silu(c_r[...]), g_r[...], (((0,), (0,)), ((), ())))

    return pl.pallas_call(
        body, name="mod_weight_grad", grid=(nl, cols // MOD_CB),
        in_specs=[pl.BlockSpec((nb, d), lambda l, j: (0, 0)), pl.BlockSpec((None, nb, MOD_CB), lambda l, j: (l, 0, j))],
        out_specs=pl.BlockSpec((None, d, MOD_CB), lambda l, j: (l, 0, j)),
        out_shape=jax.ShapeDtypeStruct((nl, d, cols), F32), compiler_params=_params(2))(c_all, dmod)


def _half_block(hr, cols):
    rb = _row_block(hr, cols)
    return rb if rb % 16 == 0 else hr


def add_half(name, g, s, core):
    _, r, cols = g.shape
    hr = r // 2
    rb = _half_block(hr, cols)
    nblk = hr // rb
    gspec = pl.BlockSpec((None, rb, cols), lambda k, i, c: (k, c[0] * nblk + i, 0))
    spec = pl.BlockSpec((None, rb, cols), lambda k, i, c: (k, i, 0))

    def body(c_r, g_r, s_r, o_r, ob_r):
        t = g_r[...] + s_r[...]
        o_r[...] = t
        ob_r[...] = t.astype(BF16)

    return pl.pallas_call(
        body, name=name, grid_spec=pltpu.PrefetchScalarGridSpec(num_scalar_prefetch=1, grid=(N_CHIP, nblk),
                                                                in_specs=[gspec, spec], out_specs=[spec, spec]),
        out_shape=[jax.ShapeDtypeStruct((N_CHIP, hr, cols), F32), jax.ShapeDtypeStruct((N_CHIP, hr, cols), BF16)],
        compiler_params=_params(2))(core, g, s)


def sum_peers(name, p32, recv, ids, shard_shape, layer, acc=None):
    _, hr, cols = p32.shape
    rb = _half_block(hr, cols)
    nblk = hr // rb

    def slot(k):
        return pl.BlockSpec((None, rb, cols), lambda i, c: ((c[0] + k) % N_CHIP, i, 0))

    def body(c_r, o_r, r1, r2, r3, *rest):
        rest[-1][...] = ((o_r[...] + r1[...].astype(F32)) + r2[...].astype(F32)) + r3[...].astype(F32)

    args = (ids, p32, recv, recv, recv) + (() if acc is None else (acc,))
    return pl.pallas_call(
        body, name=name, grid_spec=pltpu.PrefetchScalarGridSpec(
            num_scalar_prefetch=1, grid=(nblk,),
            in_specs=[slot(0), slot(1), slot(2), slot(3)] + ([] if acc is None else [_ANY]),
            out_specs=pl.BlockSpec((None, rb, cols), lambda i, c: (layer, c[1] * nblk + i, 0))),
        out_shape=jax.ShapeDtypeStruct(shard_shape, F32), input_output_aliases={} if acc is None else {5: 0},
        compiler_params=_params(1))(*args)


def cast_into_slab(name, w, ids):
    nl, r, cols = w.shape
    hr = r // 2
    rb = _half_block(hr, cols)
    nblk = hr // rb

    def body(c_r, w_r, o_r):
        o_r[...] = w_r[...].astype(BF16)

    return pl.pallas_call(
        body, name=name, grid_spec=pltpu.PrefetchScalarGridSpec(
            num_scalar_prefetch=1, grid=(nl, nblk),
            in_specs=[pl.BlockSpec((None, rb, cols), lambda l, i, c: (l, c[1] * nblk + i, 0))],
            out_specs=pl.BlockSpec((None, None, rb, cols), lambda l, i, c: (l, c[0], c[1] * nblk + i, 0))),
        out_shape=jax.ShapeDtypeStruct((nl, N_CHIP, r, cols), BF16), compiler_params=_params(2))(ids, w)


def _me():
    return lax.axis_index("x"), lax.axis_index("y"), lax.axis_index("c")


def all_gather8(name, x_shard, in_vmem):
    m_per, n = x_shard.shape
    space = pltpu.VMEM if in_vmem else pl.ANY

    def body(x_ref, out_ref, send_sems, recv_sems, local_sem):
        x, y, c = _me()
        me, sibling = (x, y, c), (x, y, 1 - c)
        chips = [(1 - x, y), (x, 1 - y), (1 - x, 1 - y)]

        def rows(px, py, pc):
            return out_ref.at[pl.ds((4 * px + 2 * py + pc) * m_per, m_per), :]

        def copy(k, block, to, src=None):
            return pltpu.make_async_remote_copy(
                src_ref=rows(*block) if src is None else src, dst_ref=rows(*block), send_sem=send_sems.at[k],
                recv_sem=recv_sems.at[k], device_id=to, device_id_type=MESH)

        mine = pltpu.make_async_copy(x_ref, rows(*me), local_sem)
        mine.start()
        first = [copy(0, me, sibling, src=x_ref)]
        first += [copy(1 + j, me, (*chip, c), src=x_ref) for j, chip in enumerate(chips)]
        for cp in first:
            cp.start()
        passed = [copy(4 + j, (*chip, c), sibling) for j, chip in enumerate(chips)]
        for j, chip in enumerate(chips):
            copy(1 + j, (*chip, c), me).wait_recv()
            passed[j].start()
        copy(0, sibling, me).wait_recv()
        for j, chip in enumerate(chips):
            copy(4 + j, (*chip, 1 - c), me).wait_recv()
        for cp in first + passed:
            cp.wait_send()
        mine.wait()

    return pl.pallas_call(
        body, name=name, out_shape=jax.ShapeDtypeStruct((N_DEV * m_per, n), x_shard.dtype),
        in_specs=[pl.BlockSpec(memory_space=space)], out_specs=pl.BlockSpec(memory_space=space),
        scratch_shapes=[pltpu.SemaphoreType.DMA((7,)), pltpu.SemaphoreType.DMA((7,)), pltpu.SemaphoreType.DMA],
    )(x_shard)


_ANY = pl.BlockSpec(memory_space=pl.ANY)


def all_gather_weights(name, slabs):
    n = len(slabs)

    def body(*refs):
        outs = refs[n:2 * n]
        send_sems, recv_sems = refs[2 * n:]
        x, y, c = _me()
        me, sibling = (x, y, c), (x, y, 1 - c)
        chips = [(1 - x, y), (x, 1 - y), (1 - x, 1 - y)]

        def view(i, px, py, pc):
            hr = slabs[i].shape[2] // 2
            return outs[i].at[:, 2 * px + py, pl.ds(pc * hr, hr), :]

        def copy(i, k, block, to):
            return pltpu.make_async_remote_copy(
                src_ref=view(i, *block), dst_ref=view(i, *block), send_sem=send_sems.at[i, k],
                recv_sem=recv_sems.at[i, k], device_id=to, device_id_type=MESH)

        first = []
        for i in range(n):
            first.append(copy(i, 0, me, sibling))
            first += [copy(i, 1 + j, me, (*chip, c)) for j, chip in enumerate(chips)]
        for cp in first:
            cp.start()
        passed = []
        for j, chip in enumerate(chips):
            for i in range(n):
                copy(i, 1 + j, (*chip, c), me).wait_recv()
                passed.append(copy(i, 4 + j, (*chip, c), sibling))
                passed[-1].start()
        for i in range(n):
            copy(i, 0, sibling, me).wait_recv()
            for j, chip in enumerate(chips):
                copy(i, 4 + j, (*chip, 1 - c), me).wait_recv()
        for cp in first + passed:
            cp.wait_send()

    return pl.pallas_call(
        body, name=name, out_shape=[jax.ShapeDtypeStruct(s.shape, s.dtype) for s in slabs],
        in_specs=[_ANY] * n, out_specs=[_ANY] * n, input_output_aliases={i: i for i in range(n)},
        scratch_shapes=[pltpu.SemaphoreType.DMA((n, 7)), pltpu.SemaphoreType.DMA((n, 7))],
    )(*slabs)


def grad_sibling_exchange(name, gs):
    n = len(gs)

    def body(*refs):
        ins, outs = refs[:n], refs[n:2 * n]
        send_sems, recv_sems = refs[2 * n:]
        mx, my, mc = _me()
        cps = []
        for i in range(n):
            hr = gs[i].shape[1] // 2
            cps.append(pltpu.make_async_remote_copy(
                src_ref=ins[i].at[:, pl.ds((1 - mc) * hr, hr), :], dst_ref=outs[i], send_sem=send_sems.at[i],
                recv_sem=recv_sems.at[i], device_id=(mx, my, 1 - mc), device_id_type=MESH))
            cps[-1].start()
        for cp in cps:
            cp.wait()

    return pl.pallas_call(
        body, name=name, out_shape=[jax.ShapeDtypeStruct((N_CHIP, g.shape[1] // 2, g.shape[2]), g.dtype) for g in gs],
        in_specs=[_ANY] * n, out_specs=[_ANY] * n,
        scratch_shapes=[pltpu.SemaphoreType.DMA((n,)), pltpu.SemaphoreType.DMA((n,))],
    )(*gs)


def grad_chip_exchange(name, ps):
    n = len(ps)

    def body(*refs):
        ins, outs = refs[:n], refs[n:2 * n]
        send_sems, recv_sems = refs[2 * n:]
        mx, my, mc = _me()
        ci = 2 * mx + my
        chips = [(1 - mx, my), (mx, 1 - my), (1 - mx, 1 - my)]
        sends = []
        for i in range(n):
            for k, (px, py) in enumerate(chips):
                sends.append(pltpu.make_async_remote_copy(
                    src_ref=ins[i].at[2 * px + py], dst_ref=outs[i].at[ci], send_sem=send_sems.at[i, k],
                    recv_sem=recv_sems.at[i, k], device_id=(px, py, mc), device_id_type=MESH))
                sends[-1].start()
        for i in range(n):
            for k, (px, py) in enumerate(chips):
                pltpu.make_async_remote_copy(
                    src_ref=ins[i].at[ci], dst_ref=outs[i].at[2 * px + py], send_sem=send_sems.at[i, k],
                    recv_sem=recv_sems.at[i, k], device_id=(px, py, mc), device_id_type=MESH).wait_recv()
        for cp in sends:
            cp.wait_send()

    return pl.pallas_call(
        body, name=name, out_shape=[jax.ShapeDtypeStruct(p.shape, p.dtype) for p in ps], in_specs=[_ANY] * n,
        out_specs=[_ANY] * n, scratch_shapes=[pltpu.SemaphoreType.DMA((n, 3)), pltpu.SemaphoreType.DMA((n, 3))],
    )(*ps)


def grad_half_exchange(name, shards):
    n = len(shards)

    def body(*refs):
        outs = refs[n:2 * n]
        send_sems, recv_sems = refs[2 * n:]
        mx, my, mc = _me()

        def copy(i, core):
            hr = shards[i].shape[1] // 2
            rows = outs[i].at[:, pl.ds(core * hr, hr), :]
            return pltpu.make_async_remote_copy(src_ref=rows, dst_ref=rows, send_sem=send_sems.at[i],
                                                recv_sem=recv_sems.at[i], device_id=(mx, my, 1 - mc), device_id_type=MESH)

        sends = [copy(i, mc) for i in range(n)]
        for cp in sends:
            cp.start()
        for i in range(n):
            copy(i, 1 - mc).wait_recv()
        for cp in sends:
            cp.wait_send()

    return pl.pallas_call(
        body, name=name, out_shape=[jax.ShapeDtypeStruct(s.shape, s.dtype) for s in shards], in_specs=[_ANY] * n,
        out_specs=[_ANY] * n, input_output_aliases={i: i for i in range(n)},
        scratch_shapes=[pltpu.SemaphoreType.DMA((n,)), pltpu.SemaphoreType.DMA((n,))],
    )(*shards)


WEIGHTS = ['w_mod', 'b_mod', 'norm_mix_g', 'w_in', 'gdn_conv_w', 'gdn_a_log', 'gdn_dt_bias', 'gdn_norm_g', 'rg_conv_w',
           'rg_conv_b', 'rg_w_a', 'rg_b_a', 'rg_w_x', 'rg_b_x', 'rg_lambda', 'mla_q_norm_g', 'mla_w_qb', 'mla_kv_norm_g',
           'mla_w_kvb', 'w_out', 'norm_mlp_g', 'w_mlp_in', 'w_mlp_out', 'final_norm_g']
SHARDED = {'w_in': 2, 'gdn_conv_w': 2, 'rg_conv_w': 2, 'mla_w_qb': 2, 'mla_w_kvb': 2, 'w_out': 1, 'w_mlp_in': 2, 'w_mlp_out': 1}
GATHER_BF16 = ('w_in', 'mla_w_qb', 'mla_w_kvb', 'w_out', 'w_mlp_in', 'w_mlp_out')
REPLICATED = [n for n in WEIGHTS if n not in SHARDED and n != 'w_mod']
PACK_COLS = 1024


def _pack(arrays, multiple):
    flat = jnp.concatenate([a.reshape(-1) for a in arrays])
    pad = (-flat.shape[0]) % multiple
    return jnp.pad(flat, (0, pad)) if pad else flat


def _unpack(flat, shapes):
    out, o = [], 0
    for shp in shapes:
        n = int(np.prod(shp))
        out.append(flat[o:o + n].reshape(shp))
        o += n
    return out


def _unshard(stacked, axis):
    moved = jnp.moveaxis(stacked, 0, axis)
    shp = list(moved.shape)
    shp[axis:axis + 2] = [shp[axis] * shp[axis + 1]]
    return moved.reshape(shp)


def _shard(full, axis):
    shp = list(full.shape)
    shp[axis:axis + 1] = [N_CHIP, shp[axis] // N_CHIP]
    return jnp.moveaxis(full.reshape(shp), axis, 0)


def _proj_cols(w):
    pad = jnp.zeros(w.shape[:-1] + (PROJ_WIDTH - w.shape[-1],), w.dtype)
    return jnp.concatenate([w[..., 0:1024], w[..., 1032:2472], w[..., 1024:1032], pad], axis=-1)


def _proj_cols_back(d):
    return jnp.concatenate([d[..., 0:1024], d[..., 2464:2472], d[..., 1024:2464]], axis=-1)


def _heads_split(w, heads, first):
    per = w.shape[-1] // heads
    r = w.reshape(w.shape[:-1] + (heads, per))
    lead = w.shape[:-1]
    return jnp.concatenate([r[..., :first].reshape(lead + (heads * first,)),
                            r[..., first:].reshape(lead + (heads * (per - first),))], axis=-1)


def _heads_merge(d, heads, first):
    lead = d.shape[:-1]
    per = d.shape[-1] // heads
    a = d[..., :heads * first].reshape(lead + (heads, first))
    b = d[..., heads * first:].reshape(lead + (heads, per - first))
    return jnp.concatenate([a, b], axis=-1).reshape(lead + (heads * per,))


def _block_diag(w):
    nl = w.shape[0]
    eye = jnp.eye(2, dtype=w.dtype)
    return jnp.einsum('lcoij,op->lcoipj', w.reshape(nl, 4, 2, 64, 64), eye).reshape(nl, 4, 128, 128)


def _block_diag_back(g):
    nl = g.shape[0]
    return jnp.einsum('lcoipj,op->lcoij', g.reshape(nl, 4, 2, 64, 2, 64), jnp.eye(2, dtype=g.dtype)).reshape(nl, 8, 64, 64)


def kernel(x, c, positions, w_mod, b_mod, norm_mix_g, w_in, gdn_conv_w, gdn_a_log, gdn_dt_bias, gdn_norm_g, rg_conv_w, rg_conv_b, rg_w_a, rg_b_a, rg_w_x, rg_b_x, rg_lambda, mla_q_norm_g, mla_w_qb, mla_kv_norm_g, mla_w_kvb, w_out, norm_mlp_g, w_mlp_in, w_mlp_out, final_norm_g, loss_target, m_w_mod, m_b_mod, m_norm_mix_g, m_w_in, m_gdn_conv_w, m_gdn_a_log, m_gdn_dt_bias, m_gdn_norm_g, m_rg_conv_w, m_rg_conv_b, m_rg_w_a, m_rg_b_a, m_rg_w_x, m_rg_b_x, m_rg_lambda, m_mla_q_norm_g, m_mla_w_qb, m_mla_kv_norm_g, m_mla_w_kvb, m_w_out, m_norm_mlp_g, m_w_mlp_in, m_w_mlp_out, m_final_norm_g, v_w_mod, v_b_mod, v_norm_mix_g, v_w_in, v_gdn_conv_w, v_gdn_a_log, v_gdn_dt_bias, v_gdn_norm_g, v_rg_conv_w, v_rg_conv_b, v_rg_w_a, v_rg_b_a, v_rg_w_x, v_rg_b_x, v_rg_lambda, v_mla_q_norm_g, v_mla_w_qb, v_mla_kv_norm_g, v_mla_w_kvb, v_w_out, v_norm_mlp_g, v_w_mlp_in, v_w_mlp_out, v_final_norm_g):
    given = dict(locals())
    wts = {n: given[n] for n in WEIGHTS}
    mom_m = {n: given["m_" + n] for n in WEIGHTS}
    mom_v = {n: given["v_" + n] for n in WEIGHTS}
    bsz, seq, d = x.shape
    depth = w_mod.shape[0]
    mx, my, mc = lax.axis_index("x"), lax.axis_index("y"), lax.axis_index("c")
    chip = 2 * mx + my
    dev = 2 * chip + mc

    conv_shapes = [wts['gdn_conv_w'].shape, wts['rg_conv_w'].shape]
    conv_flat = _pack([wts['gdn_conv_w'], wts['rg_conv_w']], d)
    conv_rows = conv_flat.shape[0] // d
    assert bsz + conv_rows <= 8
    c_pad = jnp.concatenate([c, conv_flat.reshape(conv_rows, d), jnp.zeros((8 - bsz - conv_rows, d), F32)], axis=0)
    gath = all_gather8("gather_c", c_pad, True).reshape(N_DEV, 8, d)
    c_all = gath[:, :bsz].reshape(N_DEV * bsz, d)
    conv_all = gath[0::2, bsz:bsz + conv_rows].reshape(N_CHIP, conv_rows * d)
    gdn_conv_full, rg_conv_full = [
        _unshard(jnp.stack([_unpack(conv_all[s], conv_shapes)[i] for s in range(N_CHIP)]), 2) for i in range(2)]

    n_half = N_DEV * bsz // 2
    mod_cols = w_mod.shape[2]
    c_rows = lax.dynamic_slice(c_all, (n_half * mc, 0), (n_half, d))
    b_mod_mine = lax.dynamic_slice(b_mod, (0, chip * mod_cols), (depth, mod_cols)).reshape(depth, 1, mod_cols)
    mod_piece = mod_matmul(c_rows, w_mod, b_mod_mine)
    mod_g = all_gather8("gather_mod", mod_piece.reshape(depth * n_half, mod_cols), True)
    mod_all = mod_g.reshape(N_CHIP, 2, depth, n_half, mod_cols).transpose(2, 1, 3, 0, 4).reshape(depth, 2 * n_half, 6 * d)
    mod_mine = lax.dynamic_slice(mod_all, (0, bsz * dev, 0), (depth, bsz, 6 * d)).reshape(depth, bsz, 6, 1, d)

    ids = jnp.stack([chip, mc]).astype(jnp.int32)
    slabs = dict(zip(GATHER_BF16, all_gather_weights(
        "gather_weights", [cast_into_slab("cast_" + n, wts[n], ids) for n in GATHER_BF16])))

    def columns(g):
        return g.transpose(0, 2, 1, 3).reshape(g.shape[0], g.shape[2], N_CHIP * g.shape[3])

    def rows_of(g):
        return g.reshape(g.shape[0], N_CHIP * g.shape[2], g.shape[3])

    w_cat = _proj_cols(columns(slabs['w_in']))
    w_q = _heads_split(columns(slabs['mla_w_qb']).astype(F32), MLA_HEADS, 64)
    w_kv = _heads_split(columns(slabs['mla_w_kvb']).astype(F32), MLA_HEADS, 64)
    w_out_full, w_mlp_out_full = rows_of(slabs['w_out']), rows_of(slabs['w_mlp_out'])
    bd_a, bd_x = _block_diag(rg_w_a), _block_diag(rg_w_x)

    inv_freq = ROPE_THETA ** (-jnp.arange(0, 32, 2, dtype=F32) / 32.0)
    ang = positions.astype(F32)[..., None] * inv_freq
    cs = jnp.concatenate([jnp.cos(ang), jnp.sin(ang)], axis=-1)

    proj_ch = [w for _, w in PROJ_PIECES]

    def row(a, l):
        return a[l].reshape(1, -1)

    def layer_args(l):
        sh_m, sc_m, gt_m, sh_f, sc_f, gt_f = (mod_mine[l, :, k] for k in range(6))
        return dict(
            mods=(sh_m, sc_m, gt_m, sh_f, sc_f, gt_f),
            mixer_in=dict(ex=[sc_m, sh_m], par=[row(norm_mix_g, l)], big=[w_cat[l]], out_ch=proj_ch, ts=512),
            gdn_conv=dict(par_tiled=[gdn_conv_full[l]], out_ch=[768], ts=seq, nc=3),
            gdn_local=dict(par=[row(gdn_a_log, l), row(gdn_dt_bias, l)], out_ch=[256] * 6, ts=512),
            rglru=dict(par_tiled=[rg_conv_full[l], row(rg_conv_b, l), row(rg_b_a, l), row(rg_b_x, l), row(rg_lambda, l),
                                  bd_a[l], bd_x[l]], out_ch=[512], ts=seq, nc=4),
            mla_pre=dict(tok_nd=[cs], par=[row(mla_q_norm_g, l), row(mla_kv_norm_g, l), w_q[l], w_kv[l]],
                         out_ch=[256, 128, 256, 256, 128], ts=512),
            out_proj=dict(ex=[gt_m], big=[w_out_full[l]], out_ch=[d], ts=512),
            mlp_in=dict(ex=[sc_f, sh_f], par=[row(norm_mlp_g, l)], big=[slabs['w_mlp_in'][l]], out_ch=[4 * d], ts=256),
            mlp_out=dict(ex=[gt_f], big=[w_mlp_out_full[l]], out_ch=[d], ts=256),
        )

    saved = []
    h = x
    for l in range(depth):
        a = layer_args(l)
        sfx = str(l)
        qkv_raw, z, rx, rgate, mq, mkv, misc = run_stage("mixer_in" + sfx, fn_mixer_in, tok=[h], **a['mixer_in'])
        (qkv_act,) = run_stage("gdn_conv" + sfx, fn_gdn_conv, tok=[qkv_raw], **a['gdn_conv'])
        xs = run_stage("gdn_local" + sfx, fn_gdn_local, tok=[qkv_act, misc], **a['gdn_local'])
        o_a, st_in = gdn_scan(xs, z, row(gdn_norm_g, l))
        (o_b,) = run_stage("rglru" + sfx, fn_rglru, tok=[rx, rgate], **a['rglru'])
        qn, qp, kn, vv, kp = run_stage("mla_pre" + sfx, fn_mla_pre, tok=[mq, mkv, misc], **a['mla_pre'])
        o_c = mla_attention(qn, qp, kn, kp, vv)
        (h_mid,) = run_stage("out_proj" + sfx, fn_out_proj, tok=[h, o_a, o_b, o_c], **a['out_proj'])
        (a_mlp,) = run_stage("mlp_in" + sfx, fn_mlp_in, tok=[h_mid], **a['mlp_in'])
        (h_out,) = run_stage("mlp_out" + sfx, fn_mlp_out, tok=[h_mid, a_mlp], **a['mlp_out'])
        saved.append(dict(h=h, qkv_raw=qkv_raw, z=z, rx=rx, rgate=rgate, mq=mq, mkv=mkv, misc=misc, qkv_act=qkv_act, xs=xs,
                          st_in=st_in, o_a=o_a, o_b=o_b, o_c=o_c, qn=qn, qp=qp, kn=kn, vv=vv, kp=kp, h_mid=h_mid, a_mlp=a_mlp))
        h = h_out

    loss_part, dh, d_final_g = loss_head(h, final_norm_g.reshape(1, d), loss_target)
    loss = lax.psum(loss_part[0, 0], ("x", "y", "c"))

    g_full = {n: [None] * depth for n in SHARDED}
    g_rep = {n: [None] * depth for n in REPLICATED if n not in ('final_norm_g', 'b_mod')}

    def column_slabs(g):
        return g.reshape(g.shape[0], N_CHIP, g.shape[1] // N_CHIP).transpose(1, 0, 2)

    def row_slabs(g):
        return g.reshape(N_CHIP, g.shape[0] // N_CHIP, g.shape[1])
    dmod = [None] * depth
    for l in reversed(range(depth)):
        a, sv = layer_args(l), saved[l]
        sfx = str(l)
        mlp_out_tok = dict(tok=[sv['h_mid'], sv['a_mlp']], cot=[dh])
        (dh_mid, da_mlp), (dgt_f,), _, _, _ = run_stage(
            "mlp_out" + sfx, fn_mlp_out, which="small", **mlp_out_tok, **{**a['mlp_out'], 'ts': 256})
        _, _, _, _, (dw_mlp_out,) = run_stage(
            "mlp_out" + sfx, fn_mlp_out, which="big", **mlp_out_tok, **{**a['mlp_out'], 'ts': 512})
        g_full['w_mlp_out'][l] = row_slabs(dw_mlp_out)
        _, _, _, _, (g_full['w_mlp_in'][l],) = run_stage(
            "mlp_in" + sfx, fn_mlp_in, tok=[sv['h_mid']], cot=[da_mlp], which="big", **{**a['mlp_in'], 'ts': 512})
        (dh_mid,), (dsc_f, dsh_f), (g_rep['norm_mlp_g'][l],), _, _ = run_stage(
            "mlp_in" + sfx, fn_mlp_in, tok=[sv['h_mid']], cot=[da_mlp], addin=dh_mid, which="small", **a['mlp_in'])
        (dh_in, do_a, do_b, do_c), (dgt_m,), _, _, (dw_out,) = run_stage(
            "out_proj" + sfx, fn_out_proj, tok=[sv['h'], sv['o_a'], sv['o_b'], sv['o_c']], cot=[dh_mid], **a['out_proj'])
        g_full['w_out'][l] = row_slabs(dw_out)
        dqn, dqp, dkn, dkp, dvv = mla_attention_bwd(sv['qn'], sv['qp'], sv['kn'], sv['kp'], sv['vv'], do_c)
        (dmq, dmkv, dmisc_c), _, (g_rep['mla_q_norm_g'][l], g_rep['mla_kv_norm_g'][l], dw_q, dw_kv), _, _ = run_stage(
            "mla_pre" + sfx, fn_mla_pre, tok=[sv['mq'], sv['mkv'], sv['misc']], cot=[dqn, dqp, dkn, dvv, dkp], **a['mla_pre'])
        g_full['mla_w_qb'][l] = column_slabs(_heads_merge(dw_q, MLA_HEADS, 64))
        g_full['mla_w_kvb'][l] = column_slabs(_heads_merge(dw_kv, MLA_HEADS, 64))
        (drx, drgate), _, _, rg_g, _ = run_stage("rglru" + sfx, fn_rglru, tok=[sv['rx'], sv['rgate']], cot=[do_b], **a['rglru'])
        (g_full['rg_conv_w'][l], g_rep['rg_conv_b'][l], g_rep['rg_b_a'][l], g_rep['rg_b_x'][l], g_rep['rg_lambda'][l],
         g_rep['rg_w_a'][l], g_rep['rg_w_x'][l]) = rg_g
        dxs, dz, g_rep['gdn_norm_g'][l] = gdn_scan_bwd(sv['xs'], sv['z'], row(gdn_norm_g, l), sv['st_in'], do_a)
        (dqkv_act, dmisc_a), _, (g_rep['gdn_a_log'][l], g_rep['gdn_dt_bias'][l]), _, _ = run_stage(
            "gdn_local" + sfx, fn_gdn_local, tok=[sv['qkv_act'], sv['misc']], cot=dxs, **a['gdn_local'])
        (dqkv_raw,), _, _, (g_full['gdn_conv_w'][l],), _ = run_stage(
            "gdn_conv" + sfx, fn_gdn_conv, tok=[sv['qkv_raw']], cot=[dqkv_act], **a['gdn_conv'])
        (dh,), (dsc_m, dsh_m), (g_rep['norm_mix_g'][l],), _, (dw_cat,) = run_stage(
            "mixer_in" + sfx, fn_mixer_in, tok=[sv['h']], cot=[dqkv_raw, dz, drx, drgate, dmq, dmkv, dmisc_a + dmisc_c],
            addin=dh_in, **a['mixer_in'])
        g_full['w_in'][l] = column_slabs(_proj_cols_back(dw_cat))
        dmod[l] = jnp.concatenate([dsh_m, dsc_m, dgt_m, dsh_f, dsc_f, dgt_f], axis=-1).reshape(bsz, 6 * d)
    grad_x = dh

    dmod = jnp.stack(dmod)
    dmod_pad = jnp.concatenate([dmod.reshape(depth * bsz, 6 * d), jnp.zeros((8 - depth * bsz, 6 * d), F32)], axis=0)
    dmod_all = all_gather8("gather_dmod", dmod_pad, True).reshape(N_DEV, 8, 6 * d)[:, :depth * bsz]
    dmod_all = dmod_all.reshape(N_DEV, depth, bsz, 6 * d).transpose(1, 0, 2, 3).reshape(depth, N_DEV * bsz, 6 * d)
    g_w_mod = mod_weight_grad(c_all, lax.dynamic_slice(dmod_all, (0, 0, chip * mod_cols), (depth, N_DEV * bsz, mod_cols)))

    g_rep = {n: jnp.stack(v) for n, v in g_rep.items()}
    g_rep['rg_w_a'] = _block_diag_back(g_rep['rg_w_a'])
    g_rep['rg_w_x'] = _block_diag_back(g_rep['rg_w_x'])
    g_rep['final_norm_g'] = d_final_g
    g_rep['b_mod'] = jnp.sum(dmod, axis=1)
    conv_names = ['gdn_conv_w', 'rg_conv_w']
    conv_full_shapes = [(depth,) + g_full[n][0].shape for n in conv_names]
    small_shapes = [wts[n].shape for n in REPLICATED] + conv_full_shapes
    rep_mult = 8 * PACK_COLS
    rep_part = _pack([g_rep[n].reshape(wts[n].shape) for n in REPLICATED] + [jnp.stack(g_full[n]) for n in conv_names],
                     rep_mult).reshape(-1, PACK_COLS)
    rep_rows = rep_part.shape[0]
    rep_all = all_gather8("gather_small_grads", rep_part, True).reshape(N_DEV, rep_rows, PACK_COLS)
    conv_zeros = [jnp.zeros(s, F32) for s in conv_full_shapes]
    rep_out = adamw_reduce("adamw_small", rep_all, *[
        _pack([src[n] for n in REPLICATED] + conv_zeros, rep_mult).reshape(-1, PACK_COLS) for src in (wts, mom_m, mom_v)])
    small_names = REPLICATED + conv_names
    rep_g, rep_d, rep_m, rep_v = [dict(zip(small_names, _unpack(o.reshape(-1), small_shapes))) for o in rep_out]
    sh_g = {}
    for n in conv_names:
        cols = wts[n].shape[2]
        sh_g[n] = lax.dynamic_slice(rep_g.pop(n), (0, 0, chip * cols), wts[n].shape)
        for dct in (rep_d, rep_m, rep_v):
            dct.pop(n)

    core_id = mc.reshape(1).astype(jnp.int32)
    units = [(i, l) for i in range(len(GATHER_BF16)) for l in range(depth)]
    gs = [g_full[GATHER_BF16[i]][l] for i, l in units]
    from_sibling = grad_sibling_exchange("grad_sibling_exchange", gs)
    sums32, sums16 = zip(*[add_half("grad_add_%s%d" % (GATHER_BF16[i], l), g, s, core_id)
                           for (i, l), g, s in zip(units, gs, from_sibling)])
    from_chips = grad_chip_exchange("grad_chip_exchange", list(sums16))
    shards = [None] * len(GATHER_BF16)
    for (i, l), p, r in zip(units, sums32, from_chips):
        n = GATHER_BF16[i]
        shards[i] = sum_peers("grad_sum_%s%d" % (n, l), p, r, ids, wts[n].shape, l, acc=shards[i])
    sh_g.update(zip(GATHER_BF16, grad_half_exchange("grad_half_exchange", shards)))
    sh_names = list(SHARDED)

    def as2d(t):
        return t.reshape(-1, t.shape[-1])

    sh_d, sh_m, sh_v = {}, {}, {}
    for n in sh_names + ['w_mod']:
        g = g_w_mod if n == 'w_mod' else sh_g[n]
        res = adamw("adamw_" + n, as2d(wts[n]), as2d(g), as2d(mom_m[n]), as2d(mom_v[n]))
        sh_d[n], sh_m[n], sh_v[n] = (r.reshape(wts[n].shape) for r in res)
    sh_g['w_mod'] = g_w_mod

    def pick(shd, rep):
        return [shd[n] if n in shd else rep[n] for n in WEIGHTS]

    return (loss, grad_x, *pick(sh_g, rep_g), *pick(sh_d, rep_d), *pick(sh_m, rep_m), *pick(sh_v, rep_v))
```

```python
import functools

import jax
import jax.numpy as jnp
import numpy as np
from jax import lax
from jax.experimental import pallas as pl
from jax.experimental.pallas import tpu as pltpu

F32, BF16 = jnp.float32, jnp.bfloat16
HI = lax.Precision.HIGH
MESH = pl.DeviceIdType.MESH

EPS = 1e-6
CHUNK = 64
GDN_HEADS = 4
MLA_HEADS = 4
RG_C = 8.0
ROPE_THETA = 10000.0
N_DEV = 8
N_CHIP = 4
V7X_VMEM_LIMIT = 60 * 1024 * 1024
ADAM_LR, ADAM_B1, ADAM_B2, ADAM_EPS, ADAM_WD, ADAM_STEP = 0.001, 0.9, 0.999, 1e-08, 0.01, 10


def _params(n_grid):
    return pltpu.CompilerParams(dimension_semantics=("arbitrary",) * n_grid, vmem_limit_bytes=V7X_VMEM_LIMIT)


def _dot(a, b, dims=(((1,), (0,)), ((), ()))):
    return lax.dot_general(a.astype(BF16), b.astype(BF16), dims, preferred_element_type=F32)


@jax.custom_vjp
def _mm_probe(x, w, probe):
    return _dot(x, w)


def _mm_probe_fwd(x, w, probe):
    return _dot(x, w), (x, w)


def _mm_probe_bwd(res, dy):
    x, w = res
    dx = _dot(dy, w, (((1,), (1,)), ((), ())))
    dw = _dot(x, dy, (((0,), (0,)), ((), ())))
    return dx, jnp.zeros_like(w), dw


_mm_probe.defvjp(_mm_probe_fwd, _mm_probe_bwd)


@jax.custom_vjp
def _probe_only(x, probe):
    return jnp.zeros((x.shape[0], probe.shape[1]), F32)


def _probe_only_fwd(x, probe):
    return jnp.zeros((x.shape[0], probe.shape[1]), F32), x


def _probe_only_bwd(x, dy):
    return jnp.zeros_like(x), _dot(x, dy, (((0,), (0,)), ((), ())))


_probe_only.defvjp(_probe_only_fwd, _probe_only_bwd)


@jax.custom_vjp
def mmw(x, w):
    return _dot(x, w)


def _mmw_fwd(x, w):
    return _dot(x, w), (x, w)


def _mmw_bwd(res, dy):
    x, w = res
    return _dot(dy, w, (((1,), (1,)), ((), ()))), _dot(x, dy, (((0,), (0,)), ((), ())))


mmw.defvjp(_mmw_fwd, _mmw_bwd)


def rms(x, g):
    return x * lax.rsqrt(jnp.mean(x * x, axis=-1, keepdims=True) + EPS) * g


def _rows(shape):
    return lax.broadcasted_iota(jnp.int32, shape, 0)


def _shift_down(x, s, fill):
    return jnp.where(_rows(x.shape) < s, fill, pltpu.roll(x, s, 0))


def _shift_up(x, s, fill):
    n = x.shape[0]
    return jnp.where(_rows(x.shape) >= n - s, fill, pltpu.roll(x, n - s, 0))


def _make_tshift(s):
    @jax.custom_vjp
    def tshift(x):
        return _shift_down(x, s, 0.0)

    tshift.defvjp(lambda x: (_shift_down(x, s, 0.0), None), lambda _, dy: (_shift_up(dy, s, 0.0),))
    return tshift


_TSHIFT = {s: _make_tshift(s) for s in (1, 2, 3)}


def causal_conv4(x, w):
    y = x * w[3:4, :]
    for j in range(3):
        y = y + _TSHIFT[3 - j](x) * w[j:j + 1, :]
    return y


def _scan_steps(n):
    d = 1
    while d < n:
        yield d
        d *= 2


@jax.custom_vjp
def linscan(a, b):
    return _linscan_fwd_impl(a, b)


def _linscan_fwd_impl(a, b):
    for d in _scan_steps(a.shape[0]):
        b = a * _shift_down(b, d, 0.0) + b
        a = a * _shift_down(a, d, 1.0)
    return b


def _linscan_fwd(a, b):
    h = _linscan_fwd_impl(a, b)
    return h, (a, h)


def _linscan_bwd(res, dh):
    a, h = res
    an = _shift_up(a, 1, 0.0)
    lam = dh
    for d in _scan_steps(a.shape[0]):
        lam = an * _shift_up(lam, d, 0.0) + lam
        an = an * _shift_up(an, d, 1.0)
    return lam * _shift_down(h, 1, 0.0), lam


linscan.defvjp(_linscan_fwd, _linscan_bwd)


def _bmm(a, b, precision=None):
    return jnp.einsum('nij,njk->nik', a, b, precision=precision, preferred_element_type=F32)


@jax.custom_vjp
def inv_unit_lower(l):
    return _inv_impl(l)


def _inv_impl(l):
    n = l.shape[-1]
    eye = (_rows((n, n)) == lax.broadcasted_iota(jnp.int32, (n, n), 1)).astype(F32)
    p = -l
    a = eye + p
    k = 1
    while 2 * k < n:
        p = _bmm(p, p, HI)
        a = a + _bmm(a, p, HI)
        k *= 2
    return a


def _inv_fwd(l):
    a = _inv_impl(l)
    return a, a


def _inv_bwd(a, da):
    at = jnp.swapaxes(a, 1, 2)
    return (-_bmm(_bmm(at, da, HI), at, HI),)


inv_unit_lower.defvjp(_inv_fwd, _inv_bwd)


def neg_expm1(y):
    series = -(y * (1.0 + y * (0.5 + y * (1.0 / 6.0 + y * (1.0 / 24.0)))))
    return jnp.where(y > -0.05, series, 1.0 - jnp.exp(y))


def run_stage(name, fn, *, tok, tok_nd=(), ex=(), par=(), par_tiled=(), big=(), out_ch, ts, nc=1, cot=None, addin=None,
              which="all"):
    tok, tok_nd, ex, par, par_tiled, big = map(list, (tok, tok_nd, ex, par, par_tiled, big))
    big_layer = [b[1] if isinstance(b, tuple) else None for b in big]
    big_arrays = [b[0] if isinstance(b, tuple) else b for b in big]
    big = [jax.ShapeDtypeStruct(a.shape if lyr is None else a.shape[1:], a.dtype) for a, lyr in zip(big_arrays, big_layer)]
    bsz, seq, _ = tok[0].shape
    ts = min(ts, seq)
    ns = seq // ts
    grid = (nc, bsz, ns)

    def tok_spec(a):
        cb = a.shape[-1] // nc
        return pl.BlockSpec((None, ts, cb), lambda c, b, s: (b, s, c))

    def ex_spec(a):
        cb = a.shape[-1] // nc
        return pl.BlockSpec((None, 1, cb), lambda c, b, s: (b, 0, c))

    def full_spec(a, single=False):
        nd = a.ndim
        kw = dict(pipeline_mode=pl.Buffered(1)) if single else {}
        return pl.BlockSpec(a.shape, lambda c, b, s: (0,) * nd, **kw)

    def tiled_spec(a):
        if a.ndim == 2:
            return pl.BlockSpec((a.shape[0], a.shape[1] // nc), lambda c, b, s: (0, c))
        return pl.BlockSpec((None,) + a.shape[1:], lambda c, b, s: (c, 0, 0))

    def big_spec(a, lyr):
        if lyr is None:
            return full_spec(a, True)
        nd = a.ndim
        return pl.BlockSpec((None,) + a.shape[1:], lambda c, b, s: (lyr,) + (0,) * (nd - 1), pipeline_mode=pl.Buffered(1))

    n_tok, n_nd, n_ex, n_par, n_pt, n_big = map(len, (tok, tok_nd, ex, par, par_tiled, big))
    in_arrays = tok + tok_nd + ex + par + par_tiled + big_arrays
    in_specs = ([tok_spec(a) for a in tok + tok_nd] + [ex_spec(a) for a in ex] + [full_spec(a) for a in par]
                + [tiled_spec(a) for a in par_tiled] + [big_spec(a, lyr) for a, lyr in zip(big_arrays, big_layer)])
    out_tok_shapes = [jax.ShapeDtypeStruct((bsz, seq, ch), F32) for ch in out_ch]
    n_in = len(in_arrays)

    def split(vals):
        i = 0
        groups = []
        for n in (n_tok, n_nd, n_ex, n_par, n_pt, n_big):
            groups.append(list(vals[i:i + n]))
            i += n
        return groups

    def split_grads(vals):
        i = 0
        groups = []
        for n in (n_tok, n_ex, n_par, n_pt, n_big):
            groups.append(list(vals[i:i + n]))
            i += n
        return groups

    if cot is None:
        def body(*refs):
            tv, ndv, ev, pv, ptv, _ = split([r[...] for r in refs[:n_in - n_big]] + [None] * n_big)
            b_refs = refs[n_in - n_big:n_in]
            outs = fn(tv, ndv, ev, pv, ptv, lambda x, i, j=None: _dot(x, b_refs[i][...] if j is None else b_refs[i][j]))
            for r, o in zip(refs[n_in:], outs):
                r[...] = o

        return pl.pallas_call(
            body, name=name, grid=grid, in_specs=in_specs, out_specs=[tok_spec(a) for a in out_tok_shapes],
            out_shape=out_tok_shapes, compiler_params=_params(3))(*in_arrays)

    cot = list(cot)
    has_addin = addin is not None
    extra = cot + ([addin] if has_addin else [])
    n_cot = len(cot)
    want_small, want_big = which in ("all", "small"), which in ("all", "big")
    if not want_small:
        in_arrays, in_specs, n_in = in_arrays[:n_in - n_big], in_specs[:n_in - n_big], n_in - n_big
    small_arrays = tok + ex + par + par_tiled
    g_shapes = [jax.ShapeDtypeStruct(a.shape, F32) for a in (small_arrays if want_small else []) + (big if want_big else [])]
    g_specs = (([tok_spec(a) for a in tok] + [ex_spec(a) for a in ex] + [full_spec(a) for a in par]
                + [tiled_spec(a) for a in par_tiled]) if want_small else []) + (
                    [full_spec(a, True) for a in big] if want_big else [])

    def body(*refs):
        c, b, s = pl.program_id(0), pl.program_id(1), pl.program_id(2)
        n_small_in = n_tok + n_nd + n_ex + n_par + n_pt
        tv, ndv, ev, pv, ptv, _ = split([r[...] for r in refs[:n_small_in]] + [None] * n_big)
        b_refs = refs[n_small_in:n_in]
        cots = [r[...] for r in refs[n_in:n_in + n_cot]]
        g_refs = list(refs[n_in + len(extra):])
        probes = [jnp.zeros(w.shape, F32) if w.ndim == 2 else [jnp.zeros(w.shape[1:], F32) for _ in range(w.shape[0])]
                  for w in big]

        def f(tv_, ev_, pv_, ptv_, probes_):
            def mm(x, i, j=None):
                probe = None if probes_ is None else (probes_[i] if j is None else probes_[i][j])
                if not want_small:
                    return _probe_only(x, probe)
                w = b_refs[i][...] if j is None else b_refs[i][j]
                return _dot(x, w) if probe is None else _mm_probe(x, w, probe)

            return fn(tv_, ndv, ev_, pv_, ptv_, mm)

        dt = de = dp = dpt = dbg = ()
        if which == "all":
            dt, de, dp, dpt, dbg = jax.vjp(f, tv, ev, pv, ptv, probes)[1](cots)
        elif which == "small":
            dt, de, dp, dpt = jax.vjp(lambda *a: f(*a, None), tv, ev, pv, ptv)[1](cots)
        else:
            (dbg,) = jax.vjp(lambda p: f(tv, ev, pv, ptv, p), probes)[1](cots)
        if has_addin:
            dt = [dt[0] + refs[n_in + n_cot][...]] + list(dt[1:])
        if want_small:
            gt_r, ge_r, gp_r, gpt_r, gb_r = split_grads(g_refs + ([] if want_big else [None] * n_big))
        else:
            gt_r, ge_r, gp_r, gpt_r, gb_r = [], [], [], [], g_refs
        for r, g in zip(gt_r, dt):
            r[...] = g

        def accumulate(r, g, first):
            @pl.when(first)
            def _():
                r[...] = g

            @pl.when(jnp.logical_not(first))
            def _():
                r[...] += g

        for r, g in zip(ge_r, de):
            accumulate(r, g, s == 0)
        first_all = jnp.logical_and(jnp.logical_and(c == 0, b == 0), s == 0)
        for r, g in zip(gp_r, dp):
            accumulate(r, g, first_all)
        for r, g in zip(gpt_r, dpt):
            accumulate(r, g, jnp.logical_and(b == 0, s == 0))
        for r, g in zip(gb_r, dbg):
            if isinstance(g, (list, tuple)):
                for j, gj in enumerate(g):
                    accumulate(r.at[j], gj, first_all)
            else:
                accumulate(r, g, first_all)

    res = pl.pallas_call(
        body, name=name + "_bwd" + ("" if which == "all" else "_" + which), grid=grid,
        in_specs=in_specs + [tok_spec(a) for a in extra], out_specs=g_specs, out_shape=g_shapes,
        compiler_params=_params(3))(*in_arrays, *extra)
    res = list(res)
    if not want_small:
        return [[], [], [], [], res]
    return split_grads(res + ([] if want_big else [None] * n_big))


PROJ_PIECES = (("qkv", 768), ("z", 256), ("rx", 512), ("rgate", 512), ("mq", 256), ("mkv", 128), ("misc", 128))
PROJ_WIDTH = sum(w for _, w in PROJ_PIECES)
MISC_KR, MISC_A, MISC_B = 0, 32, 36


def fn_mixer_in(tok, nd, ex, par, pt, mm):
    (h,), (sc, sh), (g,) = tok, ex, par
    proj = mm(rms(h, g) * (1.0 + sc) + sh, 0)
    outs, o = [], 0
    for _, w in PROJ_PIECES:
        outs.append(proj[:, o:o + w])
        o += w
    return outs


def fn_gdn_conv(tok, nd, ex, par, pt, mm):
    return [jax.nn.silu(causal_conv4(tok[0], pt[0]))]


def _tri_masks():
    r = _rows((CHUNK, CHUNK))
    c = lax.broadcasted_iota(jnp.int32, (CHUNK, CHUNK), 1)
    return (c <= r), (c < r)


def fn_gdn_local(tok, nd, ex, par, pt, mm):
    (qkv, misc), (a_log, dt_bias) = tok, par
    ts = qkv.shape[0]
    nb = ts // CHUNK
    lower, strict = _tri_masks()
    tril = jnp.broadcast_to(lower.astype(F32), (nb, CHUNK, CHUNK))
    ones = jnp.ones((nb, CHUNK, CHUNK), F32)
    outs = [[] for _ in range(6)]
    for hd in range(GDN_HEADS):
        def head(x, base):
            return x[:, base + 64 * hd: base + 64 * hd + 64]

        def l2n(x):
            return x * lax.rsqrt(jnp.sum(x * x, axis=-1, keepdims=True) + EPS)

        q = (l2n(head(qkv, 0)) * (64.0 ** -0.5)).reshape(nb, CHUNK, 64)
        k = l2n(head(qkv, 256)).reshape(nb, CHUNK, 64)
        v = head(qkv, 512).reshape(nb, CHUNK, 64)
        a = misc[:, MISC_A + hd: MISC_A + hd + 1]
        b = misc[:, MISC_B + hd: MISC_B + hd + 1]
        g = -jnp.exp(a_log[:, hd:hd + 1]) * jax.nn.softplus(a + dt_bias[:, hd:hd + 1])
        beta = jax.nn.sigmoid(b).reshape(nb, CHUNK, 1)
        gb = jnp.broadcast_to(g.reshape(nb, CHUNK, 1), (nb, CHUNK, CHUNK))
        gi = _bmm(tril, gb, HI)
        gl = _bmm(ones, gb, HI)
        diff = gi - jnp.swapaxes(gi, 1, 2)
        decay = jnp.where(lower, jnp.exp(jnp.where(lower, diff, 0.0)), 0.0)
        kb = k * beta
        vb = v * beta
        kk = jnp.einsum('ncd,nsd->ncs', kb.astype(BF16), k.astype(BF16), preferred_element_type=F32)
        amat = inv_unit_lower(jnp.where(strict, kk * decay, 0.0))
        eg = jnp.exp(gi)
        u = _bmm(amat, vb, HI)
        w = _bmm(amat, kb * eg, HI)
        qk = jnp.einsum('ncd,nsd->ncs', q.astype(BF16), k.astype(BF16), preferred_element_type=F32) * decay
        qd = q * eg
        kt = k * jnp.exp(gl - gi)
        cd = jnp.exp(gl)
        for lst, val in zip(outs, (qk, qd, u, w, kt, cd)):
            lst.append(val.reshape(ts, 64))
    return [jnp.concatenate(lst, axis=-1) for lst in outs]


def fn_rglru(tok, nd, ex, par, pt, mm):
    (rx, rgate), (conv_w, conv_b, b_a, b_x, lam, bd_a, bd_x) = tok, pt
    xc = causal_conv4(rx, conv_w) + conv_b
    r = jax.nn.sigmoid(mmw(xc, bd_a) + b_a)
    i = jax.nn.sigmoid(mmw(xc, bd_x) + b_x)
    log_a = -RG_C * r * jax.nn.softplus(-lam)
    a = jnp.exp(log_a)
    bterm = jnp.sqrt(neg_expm1(2.0 * log_a)) * (i * xc)
    return [linscan(a, bterm) * jax.nn.gelu(rgate)]


def _rope32(x, cos, sin):
    x1, x2 = x[:, :16], x[:, 16:32]
    return jnp.concatenate([x1 * cos - x2 * sin, x2 * cos + x1 * sin], axis=-1)


def fn_mla_pre(tok, nd, ex, par, pt, mm):
    (mq, mkv, misc), (cs,), (g_q, g_kv, w_q, w_kv) = tok, nd, par
    q = mmw(rms(mq, g_q), w_q)
    kv = mmw(rms(mkv, g_kv), w_kv)
    cos, sin = cs[:, 0:16], cs[:, 16:32]
    qp = jnp.concatenate([_rope32(q[:, 256 + 32 * h: 288 + 32 * h], cos, sin) for h in range(MLA_HEADS)], axis=-1)
    kp = _rope32(misc[:, MISC_KR:MISC_KR + 32], cos, sin)
    kp = jnp.concatenate([kp, jnp.zeros((kp.shape[0], 96), F32)], axis=-1)
    return [q[:, 0:256], qp, kv[:, 0:256], kv[:, 256:512], kp]


def fn_out_proj(tok, nd, ex, par, pt, mm):
    (h, o_a, o_b, o_c), (gt,) = tok, ex
    return [h + gt * mm(jnp.concatenate([o_a, o_b, o_c], axis=-1), 0)]


def fn_mlp_in(tok, nd, ex, par, pt, mm):
    (h,), (sc, sh), (g,) = tok, ex, par
    u = rms(h, g) * (1.0 + sc) + sh
    return [jnp.concatenate([mm(u, 0, j) for j in range(N_CHIP)], axis=-1)]


def fn_mlp_out(tok, nd, ex, par, pt, mm):
    (h, a), (gt,) = tok, ex
    return [h + gt * mm(jnp.square(jax.nn.relu(a)), 0)]


GDN_W = GDN_HEADS * 64


def _head_mask():
    r = _rows((GDN_W, GDN_W)) // 64
    c = lax.broadcasted_iota(jnp.int32, (GDN_W, GDN_W), 1) // 64
    return r == c


def _heads_diag(x):
    return jnp.where(_head_mask(), jnp.concatenate([x] * GDN_HEADS, axis=0), 0.0)


def _heads_compact(s):
    return s[0:64] + s[64:128] + s[128:192] + s[192:256]


def _gdn_step(state, qk, qd, u, w, kt, cd, z, norm_g):
    v_new = u - _dot(w, state)
    o = _dot(qd, state) + _dot(qk, _heads_diag(v_new))
    update = _dot(kt, v_new, (((0,), (0,)), ((), ())))
    new_state = state * jnp.broadcast_to(cd[0:1, :], (GDN_W, GDN_W)) + jnp.where(_head_mask(), update, 0.0)
    outs = [rms(o[:, 64 * hd: 64 * hd + 64], norm_g) * jax.nn.silu(z[:, 64 * hd: 64 * hd + 64]) for hd in range(GDN_HEADS)]
    return new_state, jnp.concatenate(outs, axis=-1)


def gdn_scan(xs, z, norm_g):
    bsz, seq, _ = z.shape
    n = seq // CHUNK
    blk = pl.BlockSpec((bsz, CHUNK, 256), lambda i: (0, i, 0))

    def body(qk, qd, u, w, kt, cd, z_ref, g_ref, o_ref, st_out, st):
        @pl.when(pl.program_id(0) == 0)
        def _():
            st[...] = jnp.zeros_like(st)

        for b in range(bsz):
            state = st[b]
            st_out[b] = _heads_compact(state)
            st[b], o_ref[b] = _gdn_step(state, qk[b], qd[b], u[b], w[b], kt[b], cd[b], z_ref[b], g_ref[...])

    return pl.pallas_call(
        body, name="gdn_scan", grid=(n,), in_specs=[blk] * 7 + [pl.BlockSpec((1, 64), lambda i: (0, 0))],
        out_specs=[blk, blk], out_shape=[jax.ShapeDtypeStruct((bsz, seq, 256), F32)] * 2,
        scratch_shapes=[pltpu.VMEM((bsz, GDN_W, GDN_W), F32)], compiler_params=_params(1))(*xs, z, norm_g)


def gdn_scan_bwd(xs, z, norm_g, st_in, do):
    bsz, seq, _ = z.shape
    n = seq // CHUNK
    blk = pl.BlockSpec((bsz, CHUNK, 256), lambda i: (0, n - 1 - i, 0))
    gspec = pl.BlockSpec((1, 64), lambda i: (0, 0))

    def body(qk, qd, u, w, kt, cd, z_ref, g_ref, st_ref, do_ref, dqk, dqd, du, dw, dkt, dcd, dz, dg, dst):
        first = pl.program_id(0) == 0

        @pl.when(first)
        def _():
            dst[...] = jnp.zeros_like(dst)

        dg_sum = None
        for b in range(bsz):
            _, vjp = jax.vjp(_gdn_step, _heads_diag(st_ref[b]), qk[b], qd[b], u[b], w[b], kt[b], cd[b], z_ref[b], g_ref[...])
            grads = vjp((dst[b], do_ref[b]))
            dst[b] = jnp.where(_head_mask(), grads[0], 0.0)
            for r, g in zip((dqk, dqd, du, dw, dkt, dcd, dz), grads[1:8]):
                r[b] = g
            dg_sum = grads[8] if dg_sum is None else dg_sum + grads[8]

        @pl.when(first)
        def _():
            dg[...] = dg_sum

        @pl.when(jnp.logical_not(first))
        def _():
            dg[...] += dg_sum

    res = pl.pallas_call(
        body, name="gdn_scan_bwd", grid=(n,), in_specs=[blk] * 7 + [gspec, blk, blk],
        out_specs=[blk] * 7 + [gspec], out_shape=[jax.ShapeDtypeStruct((bsz, seq, 256), F32)] * 7
        + [jax.ShapeDtypeStruct((1, 64), F32)],
        scratch_shapes=[pltpu.VMEM((bsz, GDN_W, GDN_W), F32)], compiler_params=_params(1))(*xs, z, norm_g, st_in, do)
    return list(res[:6]), res[6], res[7]


ATTN_TQ = 256
ATTN_SCALE = 96.0 ** -0.5


def _attn_head(qn, qp, kn, kp, v, q0):
    nt = (((1,), (1,)), ((), ()))
    s = (_dot(qn, kn, nt) + _dot(qp, kp, nt)) * ATTN_SCALE
    qc = (q0 + _rows(s.shape)) // CHUNK
    kc = lax.broadcasted_iota(jnp.int32, s.shape, 1) // CHUNK
    s = jnp.where(kc <= qc, s, -1e30)
    p = jnp.exp(s - jnp.max(s, axis=-1, keepdims=True))
    p = p / jnp.sum(p, axis=-1, keepdims=True)
    return _dot(p, v)


def _key_lengths(seq):
    n_var = min(2, seq // ATTN_TQ)
    return [(j + 1) * (seq // n_var) for j in range(n_var)]


def _key_variant(i, seq):
    return ((i + 1) * ATTN_TQ - 1) // _key_lengths(seq)[0]


def mla_attention(qn, qp, kn, kp, v):
    bsz, seq, _ = qn.shape
    nq = seq // ATTN_TQ

    def qspec(ch):
        return pl.BlockSpec((None, ATTN_TQ, ch), lambda b, i: (b, i, 0))

    def kspec(ch):
        return pl.BlockSpec((None, seq, ch), lambda b, i: (b, 0, 0))

    def body(qn_r, qp_r, kn_r, kp_r, v_r, o_r):
        i = pl.program_id(1)
        q0 = i * ATTN_TQ

        def with_keys(klen):
            kpv = kp_r[0:klen, 0:32]
            outs = []
            for h in range(MLA_HEADS):
                sl = slice(64 * h, 64 * h + 64)
                outs.append(_attn_head(qn_r[:, sl], qp_r[:, 32 * h: 32 * h + 32], kn_r[0:klen, sl], kpv,
                                       v_r[0:klen, sl], q0))
            o_r[...] = jnp.concatenate(outs, axis=-1)

        for j, klen in enumerate(_key_lengths(seq)):
            pl.when(_key_variant(i, seq) == j)(functools.partial(with_keys, klen))

    return pl.pallas_call(
        body, name="mla_attention", grid=(bsz, nq), in_specs=[qspec(256), qspec(128), kspec(256), kspec(128), kspec(256)],
        out_specs=qspec(256), out_shape=jax.ShapeDtypeStruct((bsz, seq, 256), F32), compiler_params=_params(2))(
            qn, qp, kn, kp, v)


def mla_attention_bwd(qn, qp, kn, kp, v, do):
    bsz, seq, _ = qn.shape
    nq = seq // ATTN_TQ

    def qspec(ch):
        return pl.BlockSpec((None, ATTN_TQ, ch), lambda b, i: (b, i, 0))

    def kspec(ch):
        return pl.BlockSpec((None, seq, ch), lambda b, i: (b, 0, 0))

    def body(qn_r, qp_r, kn_r, kp_r, v_r, do_r, dqn_r, dqp_r, dkn_r, dkp_r, dv_r):
        i = pl.program_id(1)
        q0 = i * ATTN_TQ

        @pl.when(i == 0)
        def _():
            dkn_r[...] = jnp.zeros_like(dkn_r)
            dkp_r[...] = jnp.zeros_like(dkp_r)
            dv_r[...] = jnp.zeros_like(dv_r)

        def with_keys(klen):
            kpv = kp_r[0:klen, 0:32]
            dqn, dqp, dkn, dv = [], [], [], []
            dkp = jnp.zeros((klen, 32), F32)
            for h in range(MLA_HEADS):
                sl = slice(64 * h, 64 * h + 64)
                _, vjp = jax.vjp(functools.partial(_attn_head, q0=q0), qn_r[:, sl], qp_r[:, 32 * h: 32 * h + 32],
                                 kn_r[0:klen, sl], kpv, v_r[0:klen, sl])
                a, b, c, d, e = vjp(do_r[:, sl])
                dqn.append(a)
                dqp.append(b)
                dkn.append(c)
                dkp = dkp + d
                dv.append(e)
            dqn_r[...] = jnp.concatenate(dqn, axis=-1)
            dqp_r[...] = jnp.concatenate(dqp, axis=-1)
            dkn_r[0:klen, :] += jnp.concatenate(dkn, axis=-1)
            dv_r[0:klen, :] += jnp.concatenate(dv, axis=-1)
            dkp_r[0:klen, 0:32] += dkp

        for j, klen in enumerate(_key_lengths(seq)):
            pl.when(_key_variant(i, seq) == j)(functools.partial(with_keys, klen))

    shp = lambda ch: jax.ShapeDtypeStruct((bsz, seq, ch), F32)
    return pl.pallas_call(
        body, name="mla_attention_bwd", grid=(bsz, nq),
        in_specs=[qspec(256), qspec(128), kspec(256), kspec(128), kspec(256), qspec(256)],
        out_specs=[qspec(256), qspec(128), kspec(256), kspec(128), kspec(256)],
        out_shape=[shp(256), shp(128), shp(256), shp(128), shp(256)], compiler_params=_params(2))(qn, qp, kn, kp, v, do)


LOSS_TS = 512


def loss_head(h, g, target):
    bsz, seq, d = h.shape
    ts = min(LOSS_TS, seq)
    tok = pl.BlockSpec((None, ts, d), lambda b, s: (b, s, 0))
    gspec = pl.BlockSpec((1, d), lambda b, s: (0, 0))
    lspec = pl.BlockSpec((1, 128), lambda b, s: (0, 0))

    def body(h_r, g_r, t_r, loss_r, dh_r, dg_r):
        first = jnp.logical_and(pl.program_id(0) == 0, pl.program_id(1) == 0)
        tv = t_r[...]

        def f(hv, gv):
            return 0.5 * jnp.sum(jnp.mean(jnp.square(rms(hv, gv) - tv), axis=-1, keepdims=True), axis=0, keepdims=True)

        val, vjp = jax.vjp(f, h_r[...], g_r[...])
        dh, dg = vjp(jnp.ones((1, 1), F32))
        dh_r[...] = dh
        lv = jnp.broadcast_to(val, (1, 128))

        @pl.when(first)
        def _():
            loss_r[...] = lv
            dg_r[...] = dg

        @pl.when(jnp.logical_not(first))
        def _():
            loss_r[...] += lv
            dg_r[...] += dg

    return pl.pallas_call(
        body, name="loss_head", grid=(bsz, seq // ts), in_specs=[tok, gspec, tok], out_specs=[lspec, tok, gspec],
        out_shape=[jax.ShapeDtypeStruct((1, 128), F32), jax.ShapeDtypeStruct(h.shape, F32), jax.ShapeDtypeStruct((1, d), F32)],
        compiler_params=_params(2))(h, g, target)


def _adamw_math(w, g, m, v):
    m = ADAM_B1 * m + (1.0 - ADAM_B1) * g
    v = ADAM_B2 * v + (1.0 - ADAM_B2) * jnp.square(g)
    m_hat = m / (1.0 - ADAM_B1 ** ADAM_STEP)
    v_hat = v / (1.0 - ADAM_B2 ** ADAM_STEP)
    return -ADAM_LR * (m_hat / (jnp.sqrt(v_hat) + ADAM_EPS) + ADAM_WD * w), m, v


def _row_block(rows, cols):
    want = max(8, (1 << 18) // cols)
    best = rows
    for r in range(8, rows + 1, 8):
        if rows % r == 0 and r <= want:
            best = r
    return best if rows % 8 == 0 else rows


def adamw(name, w, g, m, v):
    rows, cols = w.shape
    rb = _row_block(rows, cols)
    spec = pl.BlockSpec((rb, cols), lambda i: (i, 0))

    def body(w_r, g_r, m_r, v_r, d_o, m_o, v_o):
        d, mn, vn = _adamw_math(w_r[...], g_r[...], m_r[...], v_r[...])
        d_o[...] = d
        m_o[...] = mn
        v_o[...] = vn

    return pl.pallas_call(body, name=name, grid=(rows // rb,), in_specs=[spec] * 4, out_specs=[spec] * 3,
                          out_shape=[jax.ShapeDtypeStruct(w.shape, F32)] * 3, compiler_params=_params(1))(w, g, m, v)


def adamw_reduce(name, parts, w, m, v):
    rows, cols = w.shape
    rb = _row_block(rows, cols)
    spec = pl.BlockSpec((rb, cols), lambda i: (i, 0))
    pspec = pl.BlockSpec((N_DEV, rb, cols), lambda i: (0, i, 0))

    def body(p_r, w_r, m_r, v_r, g_o, d_o, m_o, v_o):
        g = p_r[0]
        for k in range(1, N_DEV):
            g = g + p_r[k]
        d, mn, vn = _adamw_math(w_r[...], g, m_r[...], v_r[...])
        g_o[...] = g
        d_o[...] = d
        m_o[...] = mn
        v_o[...] = vn

    return pl.pallas_call(body, name=name, grid=(rows // rb,), in_specs=[pspec, spec, spec, spec], out_specs=[spec] * 4,
                          out_shape=[jax.ShapeDtypeStruct(w.shape, F32)] * 4, compiler_params=_params(1))(parts, w, m, v)


MOD_CB = 512


def mod_matmul(c_rows, w_mod, b_mod):
    nl, d, cols = w_mod.shape

    def body(c_r, w_r, b_r, o_r):
        o_r[...] = _dot(jax.nn.silu(c_r[...]), w_r[...]) + b_r[...]

    return pl.pallas_call(
        body, name="mod_matmul", grid=(nl, cols // MOD_CB),
        in_specs=[pl.BlockSpec((8, d), lambda l, j: (0, 0)), pl.BlockSpec((None, d, MOD_CB), lambda l, j: (l, 0, j)),
                  pl.BlockSpec((None, 1, MOD_CB), lambda l, j: (l, 0, j))],
        out_specs=pl.BlockSpec((None, 8, MOD_CB), lambda l, j: (l, 0, j)),
        out_shape=jax.ShapeDtypeStruct((nl, 8, cols), F32), compiler_params=_params(2))(c_rows, w_mod, b_mod)


def mod_weight_grad(c_all, dmod):
    nl, nb, cols = dmod.shape
    d = c_all.shape[1]

    def body(c_r, g_r, o_r):
        o_r[...] = _dot(jax.nn.silu(c_r[...]), g_r[...], (((0,), (0,)), ((), ())))

    return pl.pallas_call(
        body, name="mod_weight_grad", grid=(nl, cols // MOD_CB),
        in_specs=[pl.BlockSpec((nb, d), lambda l, j: (0, 0)), pl.BlockSpec((None, nb, MOD_CB), lambda l, j: (l, 0, j))],
        out_specs=pl.BlockSpec((None, d, MOD_CB), lambda l, j: (l, 0, j)),
        out_shape=jax.ShapeDtypeStruct((nl, d, cols), F32), compiler_params=_params(2))(c_all, dmod)


def _half_block(hr, cols):
    rb = _row_block(hr, cols)
    return rb if rb % 16 == 0 else hr


def add_half(name, g, s, core):
    _, r, cols = g.shape
    hr = r // 2
    rb = _half_block(hr, cols)
    nblk = hr // rb
    gspec = pl.BlockSpec((None, rb, cols), lambda k, i, c: (k, c[0] * nblk + i, 0))
    spec = pl.BlockSpec((None, rb, cols), lambda k, i, c: (k, i, 0))

    def body(c_r, g_r, s_r, o_r, ob_r):
        t = g_r[...] + s_r[...]
        o_r[...] = t
        ob_r[...] = t.astype(BF16)

    return pl.pallas_call(
        body, name=name, grid_spec=pltpu.PrefetchScalarGridSpec(num_scalar_prefetch=1, grid=(N_CHIP, nblk),
                                                                in_specs=[gspec, spec], out_specs=[spec, spec]),
        out_shape=[jax.ShapeDtypeStruct((N_CHIP, hr, cols), F32), jax.ShapeDtypeStruct((N_CHIP, hr, cols), BF16)],
        compiler_params=_params(2))(core, g, s)


def sum_peers(name, p32, recv, ids, shard_shape, layer, acc=None):
    _, hr, cols = p32.shape
    rb = _half_block(hr, cols)
    nblk = hr // rb

    def slot(k):
        return pl.BlockSpec((None, rb, cols), lambda i, c: ((c[0] + k) % N_CHIP, i, 0))

    def body(c_r, o_r, r1, r2, r3, *rest):
        rest[-1][...] = ((o_r[...] + r1[...].astype(F32)) + r2[...].astype(F32)) + r3[...].astype(F32)

    args = (ids, p32, recv, recv, recv) + (() if acc is None else (acc,))
    return pl.pallas_call(
        body, name=name, grid_spec=pltpu.PrefetchScalarGridSpec(
            num_scalar_prefetch=1, grid=(nblk,),
            in_specs=[slot(0), slot(1), slot(2), slot(3)] + ([] if acc is None else [_ANY]),
            out_specs=pl.BlockSpec((None, rb, cols), lambda i, c: (layer, c[1] * nblk + i, 0))),
        out_shape=jax.ShapeDtypeStruct(shard_shape, F32), input_output_aliases={} if acc is None else {5: 0},
        compiler_params=_params(1))(*args)


def cast_into_slab(name, w, ids):
    nl, r, cols = w.shape
    hr = r // 2
    rb = _half_block(hr, cols)
    nblk = hr // rb

    def body(c_r, w_r, o_r):
        o_r[...] = w_r[...].astype(BF16)

    return pl.pallas_call(
        body, name=name, grid_spec=pltpu.PrefetchScalarGridSpec(
            num_scalar_prefetch=1, grid=(nl, nblk),
            in_specs=[pl.BlockSpec((None, rb, cols), lambda l, i, c: (l, c[1] * nblk + i, 0))],
            out_specs=pl.BlockSpec((None, None, rb, cols), lambda l, i, c: (l, c[0], c[1] * nblk + i, 0))),
        out_shape=jax.ShapeDtypeStruct((nl, N_CHIP, r, cols), BF16), compiler_params=_params(2))(ids, w)


def _me():
    return lax.axis_index("x"), lax.axis_index("y"), lax.axis_index("c")


def all_gather8(name, x_shard, in_vmem):
    m_per, n = x_shard.shape
    space = pltpu.VMEM if in_vmem else pl.ANY

    def body(x_ref, out_ref, send_sems, recv_sems, local_sem):
        x, y, c = _me()
        me, sibling = (x, y, c), (x, y, 1 - c)
        chips = [(1 - x, y), (x, 1 - y), (1 - x, 1 - y)]

        def rows(px, py, pc):
            return out_ref.at[pl.ds((4 * px + 2 * py + pc) * m_per, m_per), :]

        def copy(k, block, to, src=None):
            return pltpu.make_async_remote_copy(
                src_ref=rows(*block) if src is None else src, dst_ref=rows(*block), send_sem=send_sems.at[k],
                recv_sem=recv_sems.at[k], device_id=to, device_id_type=MESH)

        mine = pltpu.make_async_copy(x_ref, rows(*me), local_sem)
        mine.start()
        first = [copy(0, me, sibling, src=x_ref)]
        first += [copy(1 + j, me, (*chip, c), src=x_ref) for j, chip in enumerate(chips)]
        for cp in first:
            cp.start()
        passed = [copy(4 + j, (*chip, c), sibling) for j, chip in enumerate(chips)]
        for j, chip in enumerate(chips):
            copy(1 + j, (*chip, c), me).wait_recv()
            passed[j].start()
        copy(0, sibling, me).wait_recv()
        for j, chip in enumerate(chips):
            copy(4 + j, (*chip, 1 - c), me).wait_recv()
        for cp in first + passed:
            cp.wait_send()
        mine.wait()

    return pl.pallas_call(
        body, name=name, out_shape=jax.ShapeDtypeStruct((N_DEV * m_per, n), x_shard.dtype),
        in_specs=[pl.BlockSpec(memory_space=space)], out_specs=pl.BlockSpec(memory_space=space),
        scratch_shapes=[pltpu.SemaphoreType.DMA((7,)), pltpu.SemaphoreType.DMA((7,)), pltpu.SemaphoreType.DMA],
    )(x_shard)


_ANY = pl.BlockSpec(memory_space=pl.ANY)


def all_gather_weights(name, slabs):
    n = len(slabs)

    def body(*refs):
        outs = refs[n:2 * n]
        send_sems, recv_sems = refs[2 * n:]
        x, y, c = _me()
        me, sibling = (x, y, c), (x, y, 1 - c)
        chips = [(1 - x, y), (x, 1 - y), (1 - x, 1 - y)]

        def view(i, px, py, pc):
            hr = slabs[i].shape[2] // 2
            return outs[i].at[:, 2 * px + py, pl.ds(pc * hr, hr), :]

        def copy(i, k, block, to):
            return pltpu.make_async_remote_copy(
                src_ref=view(i, *block), dst_ref=view(i, *block), send_sem=send_sems.at[i, k],
                recv_sem=recv_sems.at[i, k], device_id=to, device_id_type=MESH)

        first = []
        for i in range(n):
            first.append(copy(i, 0, me, sibling))
            first += [copy(i, 1 + j, me, (*chip, c)) for j, chip in enumerate(chips)]
        for cp in first:
            cp.start()
        passed = []
        for j, chip in enumerate(chips):
            for i in range(n):
                copy(i, 1 + j, (*chip, c), me).wait_recv()
                passed.append(copy(i, 4 + j, (*chip, c), sibling))
                passed[-1].start()
        for i in range(n):
            copy(i, 0, sibling, me).wait_recv()
            for j, chip in enumerate(chips):
                copy(i, 4 + j, (*chip, 1 - c), me).wait_recv()
        for cp in first + passed:
            cp.wait_send()

    return pl.pallas_call(
        body, name=name, out_shape=[jax.ShapeDtypeStruct(s.shape, s.dtype) for s in slabs],
        in_specs=[_ANY] * n, out_specs=[_ANY] * n, input_output_aliases={i: i for i in range(n)},
        scratch_shapes=[pltpu.SemaphoreType.DMA((n, 7)), pltpu.SemaphoreType.DMA((n, 7))],
    )(*slabs)


def grad_sibling_exchange(name, gs):
    n = len(gs)

    def body(*refs):
        ins, outs = refs[:n], refs[n:2 * n]
        send_sems, recv_sems = refs[2 * n:]
        mx, my, mc = _me()
        cps = []
        for i in range(n):
            hr = gs[i].shape[1] // 2
            cps.append(pltpu.make_async_remote_copy(
                src_ref=ins[i].at[:, pl.ds((1 - mc) * hr, hr), :], dst_ref=outs[i], send_sem=send_sems.at[i],
                recv_sem=recv_sems.at[i], device_id=(mx, my, 1 - mc), device_id_type=MESH))
            cps[-1].start()
        for cp in cps:
            cp.wait()

    return pl.pallas_call(
        body, name=name, out_shape=[jax.ShapeDtypeStruct((N_CHIP, g.shape[1] // 2, g.shape[2]), g.dtype) for g in gs],
        in_specs=[_ANY] * n, out_specs=[_ANY] * n,
        scratch_shapes=[pltpu.SemaphoreType.DMA((n,)), pltpu.SemaphoreType.DMA((n,))],
    )(*gs)


def grad_chip_exchange(name, ps):
    n = len(ps)

    def body(*refs):
        ins, outs = refs[:n], refs[n:2 * n]
        send_sems, recv_sems = refs[2 * n:]
        mx, my, mc = _me()
        ci = 2 * mx + my
        chips = [(1 - mx, my), (mx, 1 - my), (1 - mx, 1 - my)]
        sends = []
        for i in range(n):
            for k, (px, py) in enumerate(chips):
                sends.append(pltpu.make_async_remote_copy(
                    src_ref=ins[i].at[2 * px + py], dst_ref=outs[i].at[ci], send_sem=send_sems.at[i, k],
                    recv_sem=recv_sems.at[i, k], device_id=(px, py, mc), device_id_type=MESH))
                sends[-1].start()
        for i in range(n):
            for k, (px, py) in enumerate(chips):
                pltpu.make_async_remote_copy(
                    src_ref=ins[i].at[ci], dst_ref=outs[i].at[2 * px + py], send_sem=send_sems.at[i, k],
                    recv_sem=recv_sems.at[i, k], device_id=(px, py, mc), device_id_type=MESH).wait_recv()
        for cp in sends:
            cp.wait_send()

    return pl.pallas_call(
        body, name=name, out_shape=[jax.ShapeDtypeStruct(p.shape, p.dtype) for p in ps], in_specs=[_ANY] * n,
        out_specs=[_ANY] * n, scratch_shapes=[pltpu.SemaphoreType.DMA((n, 3)), pltpu.SemaphoreType.DMA((n, 3))],
    )(*ps)


def grad_half_exchange(name, shards):
    n = len(shards)

    def body(*refs):
        outs = refs[n:2 * n]
        send_sems, recv_sems = refs[2 * n:]
        mx, my, mc = _me()

        def copy(i, core):
            hr = shards[i].shape[1] // 2
            rows = outs[i].at[:, pl.ds(core * hr, hr), :]
            return pltpu.make_async_remote_copy(src_ref=rows, dst_ref=rows, send_sem=send_sems.at[i],
                                                recv_sem=recv_sems.at[i], device_id=(mx, my, 1 - mc), device_id_type=MESH)

        sends = [copy(i, mc) for i in range(n)]
        for cp in sends:
            cp.start()
        for i in range(n):
            copy(i, 1 - mc).wait_recv()
        for cp in sends:
            cp.wait_send()

    return pl.pallas_call(
        body, name=name, out_shape=[jax.ShapeDtypeStruct(s.shape, s.dtype) for s in shards], in_specs=[_ANY] * n,
        out_specs=[_ANY] * n, input_output_aliases={i: i for i in range(n)},
        scratch_shapes=[pltpu.SemaphoreType.DMA((n,)), pltpu.SemaphoreType.DMA((n,))],
    )(*shards)


WEIGHTS = ['w_mod', 'b_mod', 'norm_mix_g', 'w_in', 'gdn_conv_w', 'gdn_a_log', 'gdn_dt_bias', 'gdn_norm_g', 'rg_conv_w',
           'rg_conv_b', 'rg_w_a', 'rg_b_a', 'rg_w_x', 'rg_b_x', 'rg_lambda', 'mla_q_norm_g', 'mla_w_qb', 'mla_kv_norm_g',
           'mla_w_kvb', 'w_out', 'norm_mlp_g', 'w_mlp_in', 'w_mlp_out', 'final_norm_g']
SHARDED = {'w_in': 2, 'gdn_conv_w': 2, 'rg_conv_w': 2, 'mla_w_qb': 2, 'mla_w_kvb': 2, 'w_out': 1, 'w_mlp_in': 2, 'w_mlp_out': 1}
GATHER_BF16 = ('w_in', 'mla_w_qb', 'mla_w_kvb', 'w_out', 'w_mlp_in', 'w_mlp_out')
REPLICATED = [n for n in WEIGHTS if n not in SHARDED and n != 'w_mod']
PACK_COLS = 1024


def _pack(arrays, multiple):
    flat = jnp.concatenate([a.reshape(-1) for a in arrays])
    pad = (-flat.shape[0]) % multiple
    return jnp.pad(flat, (0, pad)) if pad else flat


def _unpack(flat, shapes):
    out, o = [], 0
    for shp in shapes:
        n = int(np.prod(shp))
        out.append(flat[o:o + n].reshape(shp))
        o += n
    return out


def _unshard(stacked, axis):
    moved = jnp.moveaxis(stacked, 0, axis)
    shp = list(moved.shape)
    shp[axis:axis + 2] = [shp[axis] * shp[axis + 1]]
    return moved.reshape(shp)


def _shard(full, axis):
    shp = list(full.shape)
    shp[axis:axis + 1] = [N_CHIP, shp[axis] // N_CHIP]
    return jnp.moveaxis(full.reshape(shp), axis, 0)


def _proj_cols(w):
    pad = jnp.zeros(w.shape[:-1] + (PROJ_WIDTH - w.shape[-1],), w.dtype)
    return jnp.concatenate([w[..., 0:1024], w[..., 1032:2472], w[..., 1024:1032], pad], axis=-1)


def _proj_cols_back(d):
    return jnp.concatenate([d[..., 0:1024], d[..., 2464:2472], d[..., 1024:2464]], axis=-1)


def _heads_split(w, heads, first):
    per = w.shape[-1] // heads
    r = w.reshape(w.shape[:-1] + (heads, per))
    lead = w.shape[:-1]
    return jnp.concatenate([r[..., :first].reshape(lead + (heads * first,)),
                            r[..., first:].reshape(lead + (heads * (per - first),))], axis=-1)


def _heads_merge(d, heads, first):
    lead = d.shape[:-1]
    per = d.shape[-1] // heads
    a = d[..., :heads * first].reshape(lead + (heads, first))
    b = d[..., heads * first:].reshape(lead + (heads, per - first))
    return jnp.concatenate([a, b], axis=-1).reshape(lead + (heads * per,))


def _block_diag(w):
    nl = w.shape[0]
    eye = jnp.eye(2, dtype=w.dtype)
    return jnp.einsum('lcoij,op->lcoipj', w.reshape(nl, 4, 2, 64, 64), eye).reshape(nl, 4, 128, 128)


def _block_diag_back(g):
    nl = g.shape[0]
    return jnp.einsum('lcoipj,op->lcoij', g.reshape(nl, 4, 2, 64, 2, 64), jnp.eye(2, dtype=g.dtype)).reshape(nl, 8, 64, 64)


def kernel(x, c, positions, w_mod, b_mod, norm_mix_g, w_in, gdn_conv_w, gdn_a_log, gdn_dt_bias, gdn_norm_g, rg_conv_w, rg_conv_b, rg_w_a, rg_b_a, rg_w_x, rg_b_x, rg_lambda, mla_q_norm_g, mla_w_qb, mla_kv_norm_g, mla_w_kvb, w_out, norm_mlp_g, w_mlp_in, w_mlp_out, final_norm_g, loss_target, m_w_mod, m_b_mod, m_norm_mix_g, m_w_in, m_gdn_conv_w, m_gdn_a_log, m_gdn_dt_bias, m_gdn_norm_g, m_rg_conv_w, m_rg_conv_b, m_rg_w_a, m_rg_b_a, m_rg_w_x, m_rg_b_x, m_rg_lambda, m_mla_q_norm_g, m_mla_w_qb, m_mla_kv_norm_g, m_mla_w_kvb, m_w_out, m_norm_mlp_g, m_w_mlp_in, m_w_mlp_out, m_final_norm_g, v_w_mod, v_b_mod, v_norm_mix_g, v_w_in, v_gdn_conv_w, v_gdn_a_log, v_gdn_dt_bias, v_gdn_norm_g, v_rg_conv_w, v_rg_conv_b, v_rg_w_a, v_rg_b_a, v_rg_w_x, v_rg_b_x, v_rg_lambda, v_mla_q_norm_g, v_mla_w_qb, v_mla_kv_norm_g, v_mla_w_kvb, v_w_out, v_norm_mlp_g, v_w_mlp_in, v_w_mlp_out, v_final_norm_g):
    given = dict(locals())
    wts = {n: given[n] for n in WEIGHTS}
    mom_m = {n: given["m_" + n] for n in WEIGHTS}
    mom_v = {n: given["v_" + n] for n in WEIGHTS}
    bsz, seq, d = x.shape
    depth = w_mod.shape[0]
    mx, my, mc = lax.axis_index("x"), lax.axis_index("y"), lax.axis_index("c")
    chip = 2 * mx + my
    dev = 2 * chip + mc

    conv_shapes = [wts['gdn_conv_w'].shape, wts['rg_conv_w'].shape]
    conv_flat = _pack([wts['gdn_conv_w'], wts['rg_conv_w']], d)
    conv_rows = conv_flat.shape[0] // d
    assert bsz + conv_rows <= 8
    c_pad = jnp.concatenate([c, conv_flat.reshape(conv_rows, d), jnp.zeros((8 - bsz - conv_rows, d), F32)], axis=0)
    gath = all_gather8("gather_c", c_pad, True).reshape(N_DEV, 8, d)
    c_all = gath[:, :bsz].reshape(N_DEV * bsz, d)
    conv_all = gath[0::2, bsz:bsz + conv_rows].reshape(N_CHIP, conv_rows * d)
    gdn_conv_full, rg_conv_full = [
        _unshard(jnp.stack([_unpack(conv_all[s], conv_shapes)[i] for s in range(N_CHIP)]), 2) for i in range(2)]

    n_half = N_DEV * bsz // 2
    mod_cols = w_mod.shape[2]
    c_rows = lax.dynamic_slice(c_all, (n_half * mc, 0), (n_half, d))
    b_mod_mine = lax.dynamic_slice(b_mod, (0, chip * mod_cols), (depth, mod_cols)).reshape(depth, 1, mod_cols)
    mod_piece = mod_matmul(c_rows, w_mod, b_mod_mine)
    mod_g = all_gather8("gather_mod", mod_piece.reshape(depth * n_half, mod_cols), True)
    mod_all = mod_g.reshape(N_CHIP, 2, depth, n_half, mod_cols).transpose(2, 1, 3, 0, 4).reshape(depth, 2 * n_half, 6 * d)
    mod_mine = lax.dynamic_slice(mod_all, (0, bsz * dev, 0), (depth, bsz, 6 * d)).reshape(depth, bsz, 6, 1, d)

    ids = jnp.stack([chip, mc]).astype(jnp.int32)
    slabs = dict(zip(GATHER_BF16, all_gather_weights(
        "gather_weights", [cast_into_slab("cast_" + n, wts[n], ids) for n in GATHER_BF16])))

    def columns(g):
        return g.transpose(0, 2, 1, 3).reshape(g.shape[0], g.shape[2], N_CHIP * g.shape[3])

    def rows_of(g):
        return g.reshape(g.shape[0], N_CHIP * g.shape[2], g.shape[3])

    w_cat = _proj_cols(columns(slabs['w_in']))
    w_q = _heads_split(columns(slabs['mla_w_qb']).astype(F32), MLA_HEADS, 64)
    w_kv = _heads_split(columns(slabs['mla_w_kvb']).astype(F32), MLA_HEADS, 64)
    w_out_full, w_mlp_out_full = rows_of(slabs['w_out']), rows_of(slabs['w_mlp_out'])
    bd_a, bd_x = _block_diag(rg_w_a), _block_diag(rg_w_x)

    inv_freq = ROPE_THETA ** (-jnp.arange(0, 32, 2, dtype=F32) / 32.0)
    ang = positions.astype(F32)[..., None] * inv_freq
    cs = jnp.concatenate([jnp.cos(ang), jnp.sin(ang)], axis=-1)

    proj_ch = [w for _, w in PROJ_PIECES]

    def row(a, l):
        return a[l].reshape(1, -1)

    def layer_args(l):
        sh_m, sc_m, gt_m, sh_f, sc_f, gt_f = (mod_mine[l, :, k] for k in range(6))
        return dict(
            mods=(sh_m, sc_m, gt_m, sh_f, sc_f, gt_f),
            mixer_in=dict(ex=[sc_m, sh_m], par=[row(norm_mix_g, l)], big=[(w_cat, l)], out_ch=proj_ch, ts=512),
            gdn_conv=dict(par_tiled=[gdn_conv_full[l]], out_ch=[768], ts=seq, nc=3),
            gdn_local=dict(par=[row(gdn_a_log, l), row(gdn_dt_bias, l)], out_ch=[256] * 6, ts=512),
            rglru=dict(par_tiled=[rg_conv_full[l], row(rg_conv_b, l), row(rg_b_a, l), row(rg_b_x, l), row(rg_lambda, l),
                                  bd_a[l], bd_x[l]], out_ch=[512], ts=seq, nc=4),
            mla_pre=dict(tok_nd=[cs], par=[row(mla_q_norm_g, l), row(mla_kv_norm_g, l), w_q[l], w_kv[l]],
                         out_ch=[256, 128, 256, 256, 128], ts=512),
            out_proj=dict(ex=[gt_m], big=[(w_out_full, l)], out_ch=[d], ts=512),
            mlp_in=dict(ex=[sc_f, sh_f], par=[row(norm_mlp_g, l)], big=[(slabs['w_mlp_in'], l)], out_ch=[4 * d], ts=256),
            mlp_out=dict(ex=[gt_f], big=[(w_mlp_out_full, l)], out_ch=[d], ts=256),
        )

    saved = []
    h = x
    for l in range(depth):
        a = layer_args(l)
        sfx = str(l)
        qkv_raw, z, rx, rgate, mq, mkv, misc = run_stage("mixer_in" + sfx, fn_mixer_in, tok=[h], **a['mixer_in'])
        (qkv_act,) = run_stage("gdn_conv" + sfx, fn_gdn_conv, tok=[qkv_raw], **a['gdn_conv'])
        xs = run_stage("gdn_local" + sfx, fn_gdn_local, tok=[qkv_act, misc], **a['gdn_local'])
        o_a, st_in = gdn_scan(xs, z, row(gdn_norm_g, l))
        (o_b,) = run_stage("rglru" + sfx, fn_rglru, tok=[rx, rgate], **a['rglru'])
        qn, qp, kn, vv, kp = run_stage("mla_pre" + sfx, fn_mla_pre, tok=[mq, mkv, misc], **a['mla_pre'])
        o_c = mla_attention(qn, qp, kn, kp, vv)
        (h_mid,) = run_stage("out_proj" + sfx, fn_out_proj, tok=[h, o_a, o_b, o_c], **a['out_proj'])
        (a_mlp,) = run_stage("mlp_in" + sfx, fn_mlp_in, tok=[h_mid], **a['mlp_in'])
        (h_out,) = run_stage("mlp_out" + sfx, fn_mlp_out, tok=[h_mid, a_mlp], **a['mlp_out'])
        saved.append(dict(h=h, qkv_raw=qkv_raw, z=z, rx=rx, rgate=rgate, mq=mq, mkv=mkv, misc=misc, qkv_act=qkv_act, xs=xs,
                          st_in=st_in, o_a=o_a, o_b=o_b, o_c=o_c, qn=qn, qp=qp, kn=kn, vv=vv, kp=kp, h_mid=h_mid, a_mlp=a_mlp))
        h = h_out

    loss_part, dh, d_final_g = loss_head(h, final_norm_g.reshape(1, d), loss_target)
    loss = lax.psum(loss_part[0, 0], ("x", "y", "c"))

    g_full = {n: [None] * depth for n in SHARDED}
    g_rep = {n: [None] * depth for n in REPLICATED if n not in ('final_norm_g', 'b_mod')}

    def column_slabs(g):
        return g.reshape(g.shape[0], N_CHIP, g.shape[1] // N_CHIP).transpose(1, 0, 2)

    def row_slabs(g):
        return g.reshape(N_CHIP, g.shape[0] // N_CHIP, g.shape[1])
    dmod = [None] * depth
    for l in reversed(range(depth)):
        a, sv = layer_args(l), saved[l]
        sfx = str(l)
        mlp_out_tok = dict(tok=[sv['h_mid'], sv['a_mlp']], cot=[dh])
        (dh_mid, da_mlp), (dgt_f,), _, _, _ = run_stage(
            "mlp_out" + sfx, fn_mlp_out, which="small", **mlp_out_tok, **{**a['mlp_out'], 'ts': 256})
        _, _, _, _, (dw_mlp_out,) = run_stage(
            "mlp_out" + sfx, fn_mlp_out, which="big", **mlp_out_tok, **{**a['mlp_out'], 'ts': 512})
        g_full['w_mlp_out'][l] = row_slabs(dw_mlp_out)
        _, _, _, _, (g_full['w_mlp_in'][l],) = run_stage(
            "mlp_in" + sfx, fn_mlp_in, tok=[sv['h_mid']], cot=[da_mlp], which="big", **{**a['mlp_in'], 'ts': 512})
        (dh_mid,), (dsc_f, dsh_f), (g_rep['norm_mlp_g'][l],), _, _ = run_stage(
            "mlp_in" + sfx, fn_mlp_in, tok=[sv['h_mid']], cot=[da_mlp], addin=dh_mid, which="small", **a['mlp_in'])
        (dh_in, do_a, do_b, do_c), (dgt_m,), _, _, (dw_out,) = run_stage(
            "out_proj" + sfx, fn_out_proj, tok=[sv['h'], sv['o_a'], sv['o_b'], sv['o_c']], cot=[dh_mid], **a['out_proj'])
        g_full['w_out'][l] = row_slabs(dw_out)
        dqn, dqp, dkn, dkp, dvv = mla_attention_bwd(sv['qn'], sv['qp'], sv['kn'], sv['kp'], sv['vv'], do_c)
        (dmq, dmkv, dmisc_c), _, (g_rep['mla_q_norm_g'][l], g_rep['mla_kv_norm_g'][l], dw_q, dw_kv), _, _ = run_stage(
            "mla_pre" + sfx, fn_mla_pre, tok=[sv['mq'], sv['mkv'], sv['misc']], cot=[dqn, dqp, dkn, dvv, dkp], **a['mla_pre'])
        g_full['mla_w_qb'][l] = column_slabs(_heads_merge(dw_q, MLA_HEADS, 64))
        g_full['mla_w_kvb'][l] = column_slabs(_heads_merge(dw_kv, MLA_HEADS, 64))
        (drx, drgate), _, _, rg_g, _ = run_stage("rglru" + sfx, fn_rglru, tok=[sv['rx'], sv['rgate']], cot=[do_b], **a['rglru'])
        (g_full['rg_conv_w'][l], g_rep['rg_conv_b'][l], g_rep['rg_b_a'][l], g_rep['rg_b_x'][l], g_rep['rg_lambda'][l],
         g_rep['rg_w_a'][l], g_rep['rg_w_x'][l]) = rg_g
        dxs, dz, g_rep['gdn_norm_g'][l] = gdn_scan_bwd(sv['xs'], sv['z'], row(gdn_norm_g, l), sv['st_in'], do_a)
        (dqkv_act, dmisc_a), _, (g_rep['gdn_a_log'][l], g_rep['gdn_dt_bias'][l]), _, _ = run_stage(
            "gdn_local" + sfx, fn_gdn_local, tok=[sv['qkv_act'], sv['misc']], cot=dxs, **a['gdn_local'])
        (dqkv_raw,), _, _, (g_full['gdn_conv_w'][l],), _ = run_stage(
            "gdn_conv" + sfx, fn_gdn_conv, tok=[sv['qkv_raw']], cot=[dqkv_act], **a['gdn_conv'])
        (dh,), (dsc_m, dsh_m), (g_rep['norm_mix_g'][l],), _, (dw_cat,) = run_stage(
            "mixer_in" + sfx, fn_mixer_in, tok=[sv['h']], cot=[dqkv_raw, dz, drx, drgate, dmq, dmkv, dmisc_a + dmisc_c],
            addin=dh_in, **a['mixer_in'])
        g_full['w_in'][l] = column_slabs(_proj_cols_back(dw_cat))
        dmod[l] = jnp.concatenate([dsh_m, dsc_m, dgt_m, dsh_f, dsc_f, dgt_f], axis=-1).reshape(bsz, 6 * d)
    grad_x = dh

    dmod = jnp.stack(dmod)
    dmod_pad = jnp.concatenate([dmod.reshape(depth * bsz, 6 * d), jnp.zeros((8 - depth * bsz, 6 * d), F32)], axis=0)
    dmod_all = all_gather8("gather_dmod", dmod_pad, True).reshape(N_DEV, 8, 6 * d)[:, :depth * bsz]
    dmod_all = dmod_all.reshape(N_DEV, depth, bsz, 6 * d).transpose(1, 0, 2, 3).reshape(depth, N_DEV * bsz, 6 * d)
    g_w_mod = mod_weight_grad(c_all, lax.dynamic_slice(dmod_all, (0, 0, chip * mod_cols), (depth, N_DEV * bsz, mod_cols)))

    g_rep = {n: jnp.stack(v) for n, v in g_rep.items()}
    g_rep['rg_w_a'] = _block_diag_back(g_rep['rg_w_a'])
    g_rep['rg_w_x'] = _block_diag_back(g_rep['rg_w_x'])
    g_rep['final_norm_g'] = d_final_g
    g_rep['b_mod'] = jnp.sum(dmod, axis=1)
    conv_names = ['gdn_conv_w', 'rg_conv_w']
    conv_full_shapes = [(depth,) + g_full[n][0].shape for n in conv_names]
    small_shapes = [wts[n].shape for n in REPLICATED] + conv_full_shapes
    rep_mult = 8 * PACK_COLS
    rep_part = _pack([g_rep[n].reshape(wts[n].shape) for n in REPLICATED] + [jnp.stack(g_full[n]) for n in conv_names],
                     rep_mult).reshape(-1, PACK_COLS)
    rep_rows = rep_part.shape[0]
    rep_all = all_gather8("gather_small_grads", rep_part, True).reshape(N_DEV, rep_rows, PACK_COLS)
    conv_zeros = [jnp.zeros(s, F32) for s in conv_full_shapes]
    rep_out = adamw_reduce("adamw_small", rep_all, *[
        _pack([src[n] for n in REPLICATED] + conv_zeros, rep_mult).reshape(-1, PACK_COLS) for src in (wts, mom_m, mom_v)])
    small_names = REPLICATED + conv_names
    rep_g, rep_d, rep_m, rep_v = [dict(zip(small_names, _unpack(o.reshape(-1), small_shapes))) for o in rep_out]
    sh_g = {}
    for n in conv_names:
        cols = wts[n].shape[2]
        sh_g[n] = lax.dynamic_slice(rep_g.pop(n), (0, 0, chip * cols), wts[n].shape)
        for dct in (rep_d, rep_m, rep_v):
            dct.pop(n)

    core_id = mc.reshape(1).astype(jnp.int32)
    units = [(i, l) for i in range(len(GATHER_BF16)) for l in range(depth)]
    gs = [g_full[GATHER_BF16[i]][l] for i, l in units]
    from_sibling = grad_sibling_exchange("grad_sibling_exchange", gs)
    sums32, sums16 = zip(*[add_half("grad_add_%s%d" % (GATHER_BF16[i], l), g, s, core_id)
                           for (i, l), g, s in zip(units, gs, from_sibling)])
    from_chips = grad_chip_exchange("grad_chip_exchange", list(sums16))
    shards = [None] * len(GATHER_BF16)
    for (i, l), p, r in zip(units, sums32, from_chips):
        n = GATHER_BF16[i]
        shards[i] = sum_peers("grad_sum_%s%d" % (n, l), p, r, ids, wts[n].shape, l, acc=shards[i])
    sh_g.update(zip(GATHER_BF16, grad_half_exchange("grad_half_exchange", shards)))
    sh_names = list(SHARDED)

    def as2d(t):
        return t.reshape(-1, t.shape[-1])

    sh_d, sh_m, sh_v = {}, {}, {}
    for n in sh_names + ['w_mod']:
        g = g_w_mod if n == 'w_mod' else sh_g[n]
        res = adamw("adamw_" + n, as2d(wts[n]), as2d(g), as2d(mom_m[n]), as2d(mom_v[n]))
        sh_d[n], sh_m[n], sh_v[n] = (r.reshape(wts[n].shape) for r in res)
    sh_g['w_mod'] = g_w_mod

    def pick(shd, rep):
        return [shd[n] if n in shd else rep[n] for n in WEIGHTS]

    return (loss, grad_x, *pick(sh_g, rep_g), *pick(sh_d, rep_d), *pick(sh_m, rep_m), *pick(sh_v, rep_v))
```

```python
import functools

import jax
import jax.numpy as jnp
import numpy as np
from jax import lax
from jax.experimental import pallas as pl
from jax.experimental.pallas import tpu as pltpu

F32, BF16 = jnp.float32, jnp.bfloat16
HI = lax.Precision.HIGH
MESH = pl.DeviceIdType.MESH

EPS = 1e-6
CHUNK = 64
GDN_HEADS = 4
MLA_HEADS = 4
RG_C = 8.0
ROPE_THETA = 10000.0
N_DEV = 8
N_CHIP = 4
V7X_VMEM_LIMIT = 60 * 1024 * 1024
ADAM_LR, ADAM_B1, ADAM_B2, ADAM_EPS, ADAM_WD, ADAM_STEP = 0.001, 0.9, 0.999, 1e-08, 0.01, 10


def _params(n_grid):
    return pltpu.CompilerParams(dimension_semantics=("arbitrary",) * n_grid, vmem_limit_bytes=V7X_VMEM_LIMIT)


def _dot(a, b, dims=(((1,), (0,)), ((), ()))):
    return lax.dot_general(a.astype(BF16), b.astype(BF16), dims, preferred_element_type=F32)


@jax.custom_vjp
def _mm_probe(x, w, probe):
    return _dot(x, w)


def _mm_probe_fwd(x, w, probe):
    return _dot(x, w), (x, w)


def _mm_probe_bwd(res, dy):
    x, w = res
    dx = _dot(dy, w, (((1,), (1,)), ((), ())))
    dw = _dot(x, dy, (((0,), (0,)), ((), ())))
    return dx, jnp.zeros_like(w), dw


_mm_probe.defvjp(_mm_probe_fwd, _mm_probe_bwd)


@jax.custom_vjp
def _probe_only(x, probe):
    return jnp.zeros((x.shape[0], probe.shape[1]), F32)


def _probe_only_fwd(x, probe):
    return jnp.zeros((x.shape[0], probe.shape[1]), F32), x


def _probe_only_bwd(x, dy):
    return jnp.zeros_like(x), _dot(x, dy, (((0,), (0,)), ((), ())))


_probe_only.defvjp(_probe_only_fwd, _probe_only_bwd)


@jax.custom_vjp
def mmw(x, w):
    return _dot(x, w)


def _mmw_fwd(x, w):
    return _dot(x, w), (x, w)


def _mmw_bwd(res, dy):
    x, w = res
    return _dot(dy, w, (((1,), (1,)), ((), ()))), _dot(x, dy, (((0,), (0,)), ((), ())))


mmw.defvjp(_mmw_fwd, _mmw_bwd)


def rms(x, g):
    return x * lax.rsqrt(jnp.mean(x * x, axis=-1, keepdims=True) + EPS) * g


def _rows(shape):
    return lax.broadcasted_iota(jnp.int32, shape, 0)


def _shift_down(x, s, fill):
    return jnp.where(_rows(x.shape) < s, fill, pltpu.roll(x, s, 0))


def _shift_up(x, s, fill):
    n = x.shape[0]
    return jnp.where(_rows(x.shape) >= n - s, fill, pltpu.roll(x, n - s, 0))


def _make_tshift(s):
    @jax.custom_vjp
    def tshift(x):
        return _shift_down(x, s, 0.0)

    tshift.defvjp(lambda x: (_shift_down(x, s, 0.0), None), lambda _, dy: (_shift_up(dy, s, 0.0),))
    return tshift


_TSHIFT = {s: _make_tshift(s) for s in (1, 2, 3)}


def causal_conv4(x, w):
    y = x * w[3:4, :]
    for j in range(3):
        y = y + _TSHIFT[3 - j](x) * w[j:j + 1, :]
    return y


def _scan_steps(n):
    d = 1
    while d < n:
        yield d
        d *= 2


@jax.custom_vjp
def linscan(a, b):
    return _linscan_fwd_impl(a, b)


def _linscan_fwd_impl(a, b):
    for d in _scan_steps(a.shape[0]):
        b = a * _shift_down(b, d, 0.0) + b
        a = a * _shift_down(a, d, 1.0)
    return b


def _linscan_fwd(a, b):
    h = _linscan_fwd_impl(a, b)
    return h, (a, h)


def _linscan_bwd(res, dh):
    a, h = res
    an = _shift_up(a, 1, 0.0)
    lam = dh
    for d in _scan_steps(a.shape[0]):
        lam = an * _shift_up(lam, d, 0.0) + lam
        an = an * _shift_up(an, d, 1.0)
    return lam * _shift_down(h, 1, 0.0), lam


linscan.defvjp(_linscan_fwd, _linscan_bwd)


def _bmm(a, b, precision=None):
    return jnp.einsum('nij,njk->nik', a, b, precision=precision, preferred_element_type=F32)


@jax.custom_vjp
def inv_unit_lower(l):
    return _inv_impl(l)


def _inv_impl(l):
    n = l.shape[-1]
    eye = (_rows((n, n)) == lax.broadcasted_iota(jnp.int32, (n, n), 1)).astype(F32)
    p = -l
    a = eye + p
    k = 1
    while 2 * k < n:
        p = _bmm(p, p, HI)
        a = a + _bmm(a, p, HI)
        k *= 2
    return a


def _inv_fwd(l):
    a = _inv_impl(l)
    return a, a


def _inv_bwd(a, da):
    at = jnp.swapaxes(a, 1, 2)
    return (-_bmm(_bmm(at, da, HI), at, HI),)


inv_unit_lower.defvjp(_inv_fwd, _inv_bwd)


def neg_expm1(y):
    series = -(y * (1.0 + y * (0.5 + y * (1.0 / 6.0 + y * (1.0 / 24.0)))))
    return jnp.where(y > -0.05, series, 1.0 - jnp.exp(y))


def run_stage(name, fn, *, tok, tok_nd=(), ex=(), par=(), par_tiled=(), big=(), out_ch, ts, nc=1, cot=None, addin=None,
              which="all", dtok_dtype=None):
    tok, tok_nd, ex, par, par_tiled, big = map(list, (tok, tok_nd, ex, par, par_tiled, big))
    big_layer = [b[1] if isinstance(b, tuple) else None for b in big]
    big_arrays = [b[0] if isinstance(b, tuple) else b for b in big]
    big = [jax.ShapeDtypeStruct(a.shape if lyr is None else a.shape[1:], a.dtype) for a, lyr in zip(big_arrays, big_layer)]
    bsz, seq, _ = tok[0].shape
    ts = min(ts, seq)
    ns = seq // ts
    grid = (nc, bsz, ns)

    def tok_spec(a):
        cb = a.shape[-1] // nc
        return pl.BlockSpec((None, ts, cb), lambda c, b, s: (b, s, c))

    def ex_spec(a):
        cb = a.shape[-1] // nc
        return pl.BlockSpec((None, 1, cb), lambda c, b, s: (b, 0, c))

    def full_spec(a, single=False):
        nd = a.ndim
        kw = dict(pipeline_mode=pl.Buffered(1)) if single else {}
        return pl.BlockSpec(a.shape, lambda c, b, s: (0,) * nd, **kw)

    def tiled_spec(a):
        if a.ndim == 2:
            return pl.BlockSpec((a.shape[0], a.shape[1] // nc), lambda c, b, s: (0, c))
        return pl.BlockSpec((None,) + a.shape[1:], lambda c, b, s: (c, 0, 0))

    def big_spec(a, lyr):
        if lyr is None:
            return full_spec(a, True)
        nd = a.ndim
        return pl.BlockSpec((None,) + a.shape[1:], lambda c, b, s: (lyr,) + (0,) * (nd - 1), pipeline_mode=pl.Buffered(1))

    n_tok, n_nd, n_ex, n_par, n_pt, n_big = map(len, (tok, tok_nd, ex, par, par_tiled, big))
    in_arrays = tok + tok_nd + ex + par + par_tiled + big_arrays
    in_specs = ([tok_spec(a) for a in tok + tok_nd] + [ex_spec(a) for a in ex] + [full_spec(a) for a in par]
                + [tiled_spec(a) for a in par_tiled] + [big_spec(a, lyr) for a, lyr in zip(big_arrays, big_layer)])
    out_tok_shapes = [jax.ShapeDtypeStruct((bsz, seq, ch), F32) for ch in out_ch]
    n_in = len(in_arrays)

    def split(vals):
        i = 0
        groups = []
        for n in (n_tok, n_nd, n_ex, n_par, n_pt, n_big):
            groups.append(list(vals[i:i + n]))
            i += n
        return groups

    def split_grads(vals):
        i = 0
        groups = []
        for n in (n_tok, n_ex, n_par, n_pt, n_big):
            groups.append(list(vals[i:i + n]))
            i += n
        return groups

    if cot is None:
        def body(*refs):
            tv, ndv, ev, pv, ptv, _ = split([r[...] for r in refs[:n_in - n_big]] + [None] * n_big)
            b_refs = refs[n_in - n_big:n_in]
            outs = fn(tv, ndv, ev, pv, ptv, lambda x, i, j=None: _dot(x, b_refs[i][...] if j is None else b_refs[i][j]))
            for r, o in zip(refs[n_in:], outs):
                r[...] = o

        return pl.pallas_call(
            body, name=name, grid=grid, in_specs=in_specs, out_specs=[tok_spec(a) for a in out_tok_shapes],
            out_shape=out_tok_shapes, compiler_params=_params(3))(*in_arrays)

    cot = list(cot)
    has_addin = addin is not None
    extra = cot + ([addin] if has_addin else [])
    n_cot = len(cot)
    want_small, want_big = which in ("all", "small"), which in ("all", "big")
    if not want_small:
        in_arrays, in_specs, n_in = in_arrays[:n_in - n_big], in_specs[:n_in - n_big], n_in - n_big
    small_arrays = tok + ex + par + par_tiled
    g_shapes = [jax.ShapeDtypeStruct(a.shape, F32) for a in (small_arrays if want_small else []) + (big if want_big else [])]
    for i, dt_ in (dtok_dtype or {}).items():
        g_shapes[i] = jax.ShapeDtypeStruct(g_shapes[i].shape, dt_)
    g_specs = (([tok_spec(a) for a in tok] + [ex_spec(a) for a in ex] + [full_spec(a) for a in par]
                + [tiled_spec(a) for a in par_tiled]) if want_small else []) + (
                    [full_spec(a, True) for a in big] if want_big else [])

    def body(*refs):
        c, b, s = pl.program_id(0), pl.program_id(1), pl.program_id(2)
        n_small_in = n_tok + n_nd + n_ex + n_par + n_pt
        tv, ndv, ev, pv, ptv, _ = split([r[...] for r in refs[:n_small_in]] + [None] * n_big)
        b_refs = refs[n_small_in:n_in]
        cots = [r[...].astype(F32) for r in refs[n_in:n_in + n_cot]]
        g_refs = list(refs[n_in + len(extra):])
        probes = [jnp.zeros(w.shape, F32) if w.ndim == 2 else [jnp.zeros(w.shape[1:], F32) for _ in range(w.shape[0])]
                  for w in big]

        def f(tv_, ev_, pv_, ptv_, probes_):
            def mm(x, i, j=None):
                probe = None if probes_ is None else (probes_[i] if j is None else probes_[i][j])
                if not want_small:
                    return _probe_only(x, probe)
                w = b_refs[i][...] if j is None else b_refs[i][j]
                return _dot(x, w) if probe is None else _mm_probe(x, w, probe)

            return fn(tv_, ndv, ev_, pv_, ptv_, mm)

        dt = de = dp = dpt = dbg = ()
        if which == "all":
            dt, de, dp, dpt, dbg = jax.vjp(f, tv, ev, pv, ptv, probes)[1](cots)
        elif which == "small":
            dt, de, dp, dpt = jax.vjp(lambda *a: f(*a, None), tv, ev, pv, ptv)[1](cots)
        else:
            (dbg,) = jax.vjp(lambda p: f(tv, ev, pv, ptv, p), probes)[1](cots)
        if has_addin:
            dt = [dt[0] + refs[n_in + n_cot][...]] + list(dt[1:])
        if want_small:
            gt_r, ge_r, gp_r, gpt_r, gb_r = split_grads(g_refs + ([] if want_big else [None] * n_big))
        else:
            gt_r, ge_r, gp_r, gpt_r, gb_r = [], [], [], [], g_refs
        for r, g in zip(gt_r, dt):
            r[...] = g.astype(r.dtype)

        def accumulate(r, g, first):
            @pl.when(first)
            def _():
                r[...] = g

            @pl.when(jnp.logical_not(first))
            def _():
                r[...] += g

        for r, g in zip(ge_r, de):
            accumulate(r, g, s == 0)
        first_all = jnp.logical_and(jnp.logical_and(c == 0, b == 0), s == 0)
        for r, g in zip(gp_r, dp):
            accumulate(r, g, first_all)
        for r, g in zip(gpt_r, dpt):
            accumulate(r, g, jnp.logical_and(b == 0, s == 0))
        for r, g in zip(gb_r, dbg):
            if isinstance(g, (list, tuple)):
                for j, gj in enumerate(g):
                    accumulate(r.at[j], gj, first_all)
            else:
                accumulate(r, g, first_all)

    res = pl.pallas_call(
        body, name=name + "_bwd" + ("" if which == "all" else "_" + which), grid=grid,
        in_specs=in_specs + [tok_spec(a) for a in extra], out_specs=g_specs, out_shape=g_shapes,
        compiler_params=_params(3))(*in_arrays, *extra)
    res = list(res)
    if not want_small:
        return [[], [], [], [], res]
    return split_grads(res + ([] if want_big else [None] * n_big))


PROJ_PIECES = (("qkv", 768), ("z", 256), ("rx", 512), ("rgate", 512), ("mq", 256), ("mkv", 128), ("misc", 128))
PROJ_WIDTH = sum(w for _, w in PROJ_PIECES)
MISC_KR, MISC_A, MISC_B = 0, 32, 36


def fn_mixer_in(tok, nd, ex, par, pt, mm):
    (h,), (sc, sh), (g,) = tok, ex, par
    proj = mm(rms(h, g) * (1.0 + sc) + sh, 0)
    outs, o = [], 0
    for _, w in PROJ_PIECES:
        outs.append(proj[:, o:o + w])
        o += w
    return outs


def fn_gdn_conv(tok, nd, ex, par, pt, mm):
    return [jax.nn.silu(causal_conv4(tok[0], pt[0]))]


def _tri_masks():
    r = _rows((CHUNK, CHUNK))
    c = lax.broadcasted_iota(jnp.int32, (CHUNK, CHUNK), 1)
    return (c <= r), (c < r)


def fn_gdn_local(tok, nd, ex, par, pt, mm):
    (qkv, misc), (a_log, dt_bias) = tok, par
    ts = qkv.shape[0]
    nb = ts // CHUNK
    lower, strict = _tri_masks()
    tril = jnp.broadcast_to(lower.astype(F32), (nb, CHUNK, CHUNK))
    ones = jnp.ones((nb, CHUNK, CHUNK), F32)
    outs = [[] for _ in range(6)]
    for hd in range(GDN_HEADS):
        def head(x, base):
            return x[:, base + 64 * hd: base + 64 * hd + 64]

        def l2n(x):
            return x * lax.rsqrt(jnp.sum(x * x, axis=-1, keepdims=True) + EPS)

        q = (l2n(head(qkv, 0)) * (64.0 ** -0.5)).reshape(nb, CHUNK, 64)
        k = l2n(head(qkv, 256)).reshape(nb, CHUNK, 64)
        v = head(qkv, 512).reshape(nb, CHUNK, 64)
        a = misc[:, MISC_A + hd: MISC_A + hd + 1]
        b = misc[:, MISC_B + hd: MISC_B + hd + 1]
        g = -jnp.exp(a_log[:, hd:hd + 1]) * jax.nn.softplus(a + dt_bias[:, hd:hd + 1])
        beta = jax.nn.sigmoid(b).reshape(nb, CHUNK, 1)
        gb = jnp.broadcast_to(g.reshape(nb, CHUNK, 1), (nb, CHUNK, CHUNK))
        gi = _bmm(tril, gb, HI)
        gl = _bmm(ones, gb, HI)
        diff = gi - jnp.swapaxes(gi, 1, 2)
        decay = jnp.where(lower, jnp.exp(jnp.where(lower, diff, 0.0)), 0.0)
        kb = k * beta
        vb = v * beta
        kk = jnp.einsum('ncd,nsd->ncs', kb.astype(BF16), k.astype(BF16), preferred_element_type=F32)
        amat = inv_unit_lower(jnp.where(strict, kk * decay, 0.0))
        eg = jnp.exp(gi)
        u = _bmm(amat, vb, HI)
        w = _bmm(amat, kb * eg, HI)
        qk = jnp.einsum('ncd,nsd->ncs', q.astype(BF16), k.astype(BF16), preferred_element_type=F32) * decay
        qd = q * eg
        kt = k * jnp.exp(gl - gi)
        cd = jnp.exp(gl)
        for lst, val in zip(outs, (qk, qd, u, w, kt, cd)):
            lst.append(val.reshape(ts, 64))
    return [jnp.concatenate(lst, axis=-1) for lst in outs]


def fn_rglru(tok, nd, ex, par, pt, mm):
    (rx, rgate), (conv_w, conv_b, b_a, b_x, lam, bd_a, bd_x) = tok, pt
    xc = causal_conv4(rx, conv_w) + conv_b
    r = jax.nn.sigmoid(mmw(xc, bd_a) + b_a)
    i = jax.nn.sigmoid(mmw(xc, bd_x) + b_x)
    log_a = -RG_C * r * jax.nn.softplus(-lam)
    a = jnp.exp(log_a)
    bterm = jnp.sqrt(neg_expm1(2.0 * log_a)) * (i * xc)
    return [linscan(a, bterm) * jax.nn.gelu(rgate)]


def _rope32(x, cos, sin):
    x1, x2 = x[:, :16], x[:, 16:32]
    return jnp.concatenate([x1 * cos - x2 * sin, x2 * cos + x1 * sin], axis=-1)


MLA_QK = 96


def fn_mla_pre(tok, nd, ex, par, pt, mm):
    (mq, mkv, misc), (cs,), (g_q, g_kv, w_q, w_kv) = tok, nd, par
    q = mmw(rms(mq, g_q), w_q)
    kv = mmw(rms(mkv, g_kv), w_kv)
    cos, sin = cs[:, 0:16], cs[:, 16:32]
    kp = _rope32(misc[:, MISC_KR:MISC_KR + 32], cos, sin)
    qs, ks, vs = [], [], []
    for h in range(MLA_HEADS):
        qs += [q[:, MLA_QK * h: MLA_QK * h + 64], _rope32(q[:, MLA_QK * h + 64: MLA_QK * h + 96], cos, sin)]
        ks += [kv[:, 128 * h: 128 * h + 64], kp]
        vs.append(kv[:, 128 * h + 64: 128 * h + 128])
    return [jnp.concatenate(qs, axis=-1), jnp.concatenate(ks, axis=-1), jnp.concatenate(vs, axis=-1)]


def fn_out_proj(tok, nd, ex, par, pt, mm):
    (h, o_a, o_b, o_c), (gt,) = tok, ex
    return [h + gt * mm(jnp.concatenate([o_a, o_b, o_c], axis=-1), 0)]


def fn_mlp_in(tok, nd, ex, par, pt, mm):
    (h,), (sc, sh), (g,) = tok, ex, par
    u = rms(h, g) * (1.0 + sc) + sh
    return [jnp.concatenate([mm(u, 0, j) for j in range(N_CHIP)], axis=-1)]


def fn_mlp_out(tok, nd, ex, par, pt, mm):
    (h, a), (gt,) = tok, ex
    return [h + gt * mm(jnp.square(jax.nn.relu(a)), 0)]


GDN_W = GDN_HEADS * 64


def _head_mask():
    r = _rows((GDN_W, GDN_W)) // 64
    c = lax.broadcasted_iota(jnp.int32, (GDN_W, GDN_W), 1) // 64
    return r == c


def _heads_diag(x):
    return jnp.where(_head_mask(), jnp.concatenate([x] * GDN_HEADS, axis=0), 0.0)


def _heads_compact(s):
    return s[0:64] + s[64:128] + s[128:192] + s[192:256]


def _gdn_step(state, qk, qd, u, w, kt, cd, z, norm_g):
    v_new = u - _dot(w, state)
    o = _dot(qd, state) + _dot(qk, _heads_diag(v_new))
    update = _dot(kt, v_new, (((0,), (0,)), ((), ())))
    new_state = state * jnp.broadcast_to(cd[0:1, :], (GDN_W, GDN_W)) + jnp.where(_head_mask(), update, 0.0)
    outs = [rms(o[:, 64 * hd: 64 * hd + 64], norm_g) * jax.nn.silu(z[:, 64 * hd: 64 * hd + 64]) for hd in range(GDN_HEADS)]
    return new_state, jnp.concatenate(outs, axis=-1)


def gdn_scan(xs, z, norm_g):
    bsz, seq, _ = z.shape
    n = seq // CHUNK
    blk = pl.BlockSpec((bsz, CHUNK, 256), lambda i: (0, i, 0))

    def body(qk, qd, u, w, kt, cd, z_ref, g_ref, o_ref, st_out, st):
        @pl.when(pl.program_id(0) == 0)
        def _():
            st[...] = jnp.zeros_like(st)

        for b in range(bsz):
            state = st[b]
            st_out[b] = _heads_compact(state)
            st[b], o_ref[b] = _gdn_step(state, qk[b], qd[b], u[b], w[b], kt[b], cd[b], z_ref[b], g_ref[...])

    return pl.pallas_call(
        body, name="gdn_scan", grid=(n,), in_specs=[blk] * 7 + [pl.BlockSpec((1, 64), lambda i: (0, 0))],
        out_specs=[blk, blk], out_shape=[jax.ShapeDtypeStruct((bsz, seq, 256), F32)] * 2,
        scratch_shapes=[pltpu.VMEM((bsz, GDN_W, GDN_W), F32)], compiler_params=_params(1))(*xs, z, norm_g)


def gdn_scan_bwd(xs, z, norm_g, st_in, do):
    bsz, seq, _ = z.shape
    n = seq // CHUNK
    blk = pl.BlockSpec((bsz, CHUNK, 256), lambda i: (0, n - 1 - i, 0))
    gspec = pl.BlockSpec((1, 64), lambda i: (0, 0))

    def body(qk, qd, u, w, kt, cd, z_ref, g_ref, st_ref, do_ref, dqk, dqd, du, dw, dkt, dcd, dz, dg, dst):
        first = pl.program_id(0) == 0

        @pl.when(first)
        def _():
            dst[...] = jnp.zeros_like(dst)

        dg_sum = None
        for b in range(bsz):
            _, vjp = jax.vjp(_gdn_step, _heads_diag(st_ref[b]), qk[b], qd[b], u[b], w[b], kt[b], cd[b], z_ref[b], g_ref[...])
            grads = vjp((dst[b], do_ref[b]))
            dst[b] = jnp.where(_head_mask(), grads[0], 0.0)
            for r, g in zip((dqk, dqd, du, dw, dkt, dcd, dz), grads[1:8]):
                r[b] = g
            dg_sum = grads[8] if dg_sum is None else dg_sum + grads[8]

        @pl.when(first)
        def _():
            dg[...] = dg_sum

        @pl.when(jnp.logical_not(first))
        def _():
            dg[...] += dg_sum

    res = pl.pallas_call(
        body, name="gdn_scan_bwd", grid=(n,), in_specs=[blk] * 7 + [gspec, blk, blk],
        out_specs=[blk] * 7 + [gspec], out_shape=[jax.ShapeDtypeStruct((bsz, seq, 256), F32)] * 7
        + [jax.ShapeDtypeStruct((1, 64), F32)],
        scratch_shapes=[pltpu.VMEM((bsz, GDN_W, GDN_W), F32)], compiler_params=_params(1))(*xs, z, norm_g, st_in, do)
    return list(res[:6]), res[6], res[7]


ATTN_TQ = 256
ATTN_SCALE = 96.0 ** -0.5


def _attn_head(q, k, v, q0):
    s = _dot(q, k, (((1,), (1,)), ((), ()))) * ATTN_SCALE
    qc = (q0 + _rows(s.shape)) // CHUNK
    kc = lax.broadcasted_iota(jnp.int32, s.shape, 1) // CHUNK
    s = jnp.where(kc <= qc, s, -1e30)
    p = jnp.exp(s - jnp.max(s, axis=-1, keepdims=True))
    p = p / jnp.sum(p, axis=-1, keepdims=True)
    return _dot(p, v)


def _key_lengths(seq):
    n_var = min(2, seq // ATTN_TQ)
    return [(j + 1) * (seq // n_var) for j in range(n_var)]


def _key_variant(i, seq):
    return ((i + 1) * ATTN_TQ - 1) // _key_lengths(seq)[0]


ATTN_QW, ATTN_VW = MLA_HEADS * MLA_QK, MLA_HEADS * 64


def _attn_specs(seq):
    def qspec(ch):
        return pl.BlockSpec((None, ATTN_TQ, ch), lambda b, i: (b, i, 0))

    def kspec(ch):
        return pl.BlockSpec((None, seq, ch), lambda b, i: (b, 0, 0))

    return qspec, kspec


def mla_attention(q, k, v):
    bsz, seq, _ = q.shape
    qspec, kspec = _attn_specs(seq)

    def body(q_r, k_r, v_r, o_r):
        i = pl.program_id(1)
        q0 = i * ATTN_TQ

        def with_keys(klen):
            outs = [_attn_head(q_r[:, MLA_QK * h: MLA_QK * h + MLA_QK], k_r[0:klen, MLA_QK * h: MLA_QK * h + MLA_QK],
                               v_r[0:klen, 64 * h: 64 * h + 64], q0) for h in range(MLA_HEADS)]
            o_r[...] = jnp.concatenate(outs, axis=-1)

        for j, klen in enumerate(_key_lengths(seq)):
            pl.when(_key_variant(i, seq) == j)(functools.partial(with_keys, klen))

    return pl.pallas_call(
        body, name="mla_attention", grid=(bsz, seq // ATTN_TQ), in_specs=[qspec(ATTN_QW), kspec(ATTN_QW), kspec(ATTN_VW)],
        out_specs=qspec(ATTN_VW), out_shape=jax.ShapeDtypeStruct((bsz, seq, ATTN_VW), F32), compiler_params=_params(2))(
            q, k, v)


def mla_attention_bwd(q, k, v, do):
    bsz, seq, _ = q.shape
    qspec, kspec = _attn_specs(seq)

    def body(q_r, k_r, v_r, do_r, dq_r, dk_r, dv_r):
        i = pl.program_id(1)
        q0 = i * ATTN_TQ

        @pl.when(i == 0)
        def _():
            dk_r[...] = jnp.zeros_like(dk_r)
            dv_r[...] = jnp.zeros_like(dv_r)

        def with_keys(klen):
            dq, dk, dv = [], [], []
            for h in range(MLA_HEADS):
                qk = slice(MLA_QK * h, MLA_QK * h + MLA_QK)
                sl = slice(64 * h, 64 * h + 64)
                _, vjp = jax.vjp(functools.partial(_attn_head, q0=q0), q_r[:, qk], k_r[0:klen, qk], v_r[0:klen, sl])
                a, b, c = vjp(do_r[:, sl])
                dq.append(a)
                dk.append(b)
                dv.append(c)
            dq_r[...] = jnp.concatenate(dq, axis=-1)
            dk_r[0:klen, :] += jnp.concatenate(dk, axis=-1)
            dv_r[0:klen, :] += jnp.concatenate(dv, axis=-1)

        for j, klen in enumerate(_key_lengths(seq)):
            pl.when(_key_variant(i, seq) == j)(functools.partial(with_keys, klen))

    shp = lambda ch: jax.ShapeDtypeStruct((bsz, seq, ch), F32)
    return pl.pallas_call(
        body, name="mla_attention_bwd", grid=(bsz, seq // ATTN_TQ),
        in_specs=[qspec(ATTN_QW), kspec(ATTN_QW), kspec(ATTN_VW), qspec(ATTN_VW)],
        out_specs=[qspec(ATTN_QW), kspec(ATTN_QW), kspec(ATTN_VW)],
        out_shape=[shp(ATTN_QW), shp(ATTN_QW), shp(ATTN_VW)], compiler_params=_params(2))(q, k, v, do)


LOSS_TS = 512


def loss_head(h, g, target):
    bsz, seq, d = h.shape
    ts = min(LOSS_TS, seq)
    tok = pl.BlockSpec((None, ts, d), lambda b, s: (b, s, 0))
    gspec = pl.BlockSpec((1, d), lambda b, s: (0, 0))
    lspec = pl.BlockSpec((1, 128), lambda b, s: (0, 0))

    def body(h_r, g_r, t_r, loss_r, dh_r, dg_r):
        first = jnp.logical_and(pl.program_id(0) == 0, pl.program_id(1) == 0)
        tv = t_r[...]

        def f(hv, gv):
            return 0.5 * jnp.sum(jnp.mean(jnp.square(rms(hv, gv) - tv), axis=-1, keepdims=True), axis=0, keepdims=True)

        val, vjp = jax.vjp(f, h_r[...], g_r[...])
        dh, dg = vjp(jnp.ones((1, 1), F32))
        dh_r[...] = dh
        lv = jnp.broadcast_to(val, (1, 128))

        @pl.when(first)
        def _():
            loss_r[...] = lv
            dg_r[...] = dg

        @pl.when(jnp.logical_not(first))
        def _():
            loss_r[...] += lv
            dg_r[...] += dg

    return pl.pallas_call(
        body, name="loss_head", grid=(bsz, seq // ts), in_specs=[tok, gspec, tok], out_specs=[lspec, tok, gspec],
        out_shape=[jax.ShapeDtypeStruct((1, 128), F32), jax.ShapeDtypeStruct(h.shape, F32), jax.ShapeDtypeStruct((1, d), F32)],
        compiler_params=_params(2))(h, g, target)


def _adamw_math(w, g, m, v):
    m = ADAM_B1 * m + (1.0 - ADAM_B1) * g
    v = ADAM_B2 * v + (1.0 - ADAM_B2) * jnp.square(g)
    m_hat = m / (1.0 - ADAM_B1 ** ADAM_STEP)
    v_hat = v / (1.0 - ADAM_B2 ** ADAM_STEP)
    return -ADAM_LR * (m_hat / (jnp.sqrt(v_hat) + ADAM_EPS) + ADAM_WD * w), m, v


def _row_block(rows, cols):
    want = max(8, (1 << 18) // cols)
    best = rows
    for r in range(8, rows + 1, 8):
        if rows % r == 0 and r <= want:
            best = r
    return best if rows % 8 == 0 else rows


def adamw(name, w, g, m, v):
    rows, cols = w.shape
    rb = _row_block(rows, cols)
    spec = pl.BlockSpec((rb, cols), lambda i: (i, 0))

    def body(w_r, g_r, m_r, v_r, d_o, m_o, v_o):
        d, mn, vn = _adamw_math(w_r[...], g_r[...], m_r[...], v_r[...])
        d_o[...] = d
        m_o[...] = mn
        v_o[...] = vn

    return pl.pallas_call(body, name=name, grid=(rows // rb,), in_specs=[spec] * 4, out_specs=[spec] * 3,
                          out_shape=[jax.ShapeDtypeStruct(w.shape, F32)] * 3, compiler_params=_params(1))(w, g, m, v)


def adamw_reduce(name, parts, w, m, v):
    rows, cols = w.shape
    rb = _row_block(rows, cols)
    spec = pl.BlockSpec((rb, cols), lambda i: (i, 0))
    pspec = pl.BlockSpec((N_DEV, rb, cols), lambda i: (0, i, 0))

    def body(p_r, w_r, m_r, v_r, g_o, d_o, m_o, v_o):
        g = p_r[0]
        for k in range(1, N_DEV):
            g = g + p_r[k]
        d, mn, vn = _adamw_math(w_r[...], g, m_r[...], v_r[...])
        g_o[...] = g
        d_o[...] = d
        m_o[...] = mn
        v_o[...] = vn

    return pl.pallas_call(body, name=name, grid=(rows // rb,), in_specs=[pspec, spec, spec, spec], out_specs=[spec] * 4,
                          out_shape=[jax.ShapeDtypeStruct(w.shape, F32)] * 4, compiler_params=_params(1))(parts, w, m, v)


MOD_CB = 512


def mod_matmul(c_rows, w_mod, b_mod):
    nl, d, cols = w_mod.shape

    def body(c_r, w_r, b_r, o_r):
        o_r[...] = _dot(jax.nn.silu(c_r[...]), w_r[...]) + b_r[...]

    return pl.pallas_call(
        body, name="mod_matmul", grid=(nl, cols // MOD_CB),
        in_specs=[pl.BlockSpec((8, d), lambda l, j: (0, 0)), pl.BlockSpec((None, d, MOD_CB), lambda l, j: (l, 0, j)),
                  pl.BlockSpec((None, 1, MOD_CB), lambda l, j: (l, 0, j))],
        out_specs=pl.BlockSpec((None, 8, MOD_CB), lambda l, j: (l, 0, j)),
        out_shape=jax.ShapeDtypeStruct((nl, 8, cols), F32), compiler_params=_params(2))(c_rows, w_mod, b_mod)


def mod_weight_grad(c_all, dmod):
    nl, nb, cols = dmod.shape
    d = c_all.shape[1]

    def body(c_r, g_r, o_r):
        o_r[...] = _dot(jax.nn.silu(c_r[...]), g_r[...], (((0,), (0,)), ((), ())))

    return pl.pallas_call(
        body, name="mod_weight_grad", grid=(nl, cols // MOD_CB),
        in_specs=[pl.BlockSpec((nb, d), lambda l, j: (0, 0)), pl.BlockSpec((None, nb, MOD_CB), lambda l, j: (l, 0, j))],
        out_specs=pl.BlockSpec((None, d, MOD_CB), lambda l, j: (l, 0, j)),
        out_shape=jax.ShapeDtypeStruct((nl, d, cols), F32), compiler_params=_params(2))(c_all, dmod)


def _half_block(hr, cols):
    rb = _row_block(hr, cols)
    return rb if rb % 16 == 0 else hr


def add_half(name, g, s, core):
    _, r, cols = g.shape
    hr = r // 2
    rb = _half_block(hr, cols)
    nblk = hr // rb
    gspec = pl.BlockSpec((None, rb, cols), lambda k, i, c: (k, c[0] * nblk + i, 0))
    spec = pl.BlockSpec((None, rb, cols), lambda k, i, c: (k, i, 0))

    def body(c_r, g_r, s_r, o_r, ob_r):
        t = g_r[...] + s_r[...]
        o_r[...] = t
        ob_r[...] = t.astype(BF16)

    return pl.pallas_call(
        body, name=name, grid_spec=pltpu.PrefetchScalarGridSpec(num_scalar_prefetch=1, grid=(N_CHIP, nblk),
                                                                in_specs=[gspec, spec], out_specs=[spec, spec]),
        out_shape=[jax.ShapeDtypeStruct((N_CHIP, hr, cols), F32), jax.ShapeDtypeStruct((N_CHIP, hr, cols), BF16)],
        compiler_params=_params(2))(core, g, s)


def sum_peers(name, p32, recv, ids, shard_shape, layer, acc=None):
    _, hr, cols = p32.shape
    rb = _half_block(hr, cols)
    nblk = hr // rb

    def slot(k):
        return pl.BlockSpec((None, rb, cols), lambda i, c: ((c[0] + k) % N_CHIP, i, 0))

    def body(c_r, o_r, r1, r2, r3, *rest):
        rest[-1][...] = ((o_r[...] + r1[...].astype(F32)) + r2[...].astype(F32)) + r3[...].astype(F32)

    args = (ids, p32, recv, recv, recv) + (() if acc is None else (acc,))
    return pl.pallas_call(
        body, name=name, grid_spec=pltpu.PrefetchScalarGridSpec(
            num_scalar_prefetch=1, grid=(nblk,),
            in_specs=[slot(0), slot(1), slot(2), slot(3)] + ([] if acc is None else [_ANY]),
            out_specs=pl.BlockSpec((None, rb, cols), lambda i, c: (layer, c[1] * nblk + i, 0))),
        out_shape=jax.ShapeDtypeStruct(shard_shape, F32), input_output_aliases={} if acc is None else {5: 0},
        compiler_params=_params(1))(*args)


def cast_into_slab(name, w, ids):
    nl, r, cols = w.shape
    hr = r // 2
    rb = _half_block(hr, cols)
    nblk = hr // rb

    def body(c_r, w_r, o_r):
        o_r[...] = w_r[...].astype(BF16)

    return pl.pallas_call(
        body, name=name, grid_spec=pltpu.PrefetchScalarGridSpec(
            num_scalar_prefetch=1, grid=(nl, nblk),
            in_specs=[pl.BlockSpec((None, rb, cols), lambda l, i, c: (l, c[1] * nblk + i, 0))],
            out_specs=pl.BlockSpec((None, None, rb, cols), lambda l, i, c: (l, c[0], c[1] * nblk + i, 0))),
        out_shape=jax.ShapeDtypeStruct((nl, N_CHIP, r, cols), BF16), compiler_params=_params(2))(ids, w)


def _me():
    return lax.axis_index("x"), lax.axis_index("y"), lax.axis_index("c")


def all_gather8(name, x_shard, in_vmem):
    m_per, n = x_shard.shape
    space = pltpu.VMEM if in_vmem else pl.ANY

    def body(x_ref, out_ref, send_sems, recv_sems, local_sem):
        x, y, c = _me()
        me, sibling = (x, y, c), (x, y, 1 - c)
        chips = [(1 - x, y), (x, 1 - y), (1 - x, 1 - y)]

        def rows(px, py, pc):
            return out_ref.at[pl.ds((4 * px + 2 * py + pc) * m_per, m_per), :]

        def copy(k, block, to, src=None):
            return pltpu.make_async_remote_copy(
                src_ref=rows(*block) if src is None else src, dst_ref=rows(*block), send_sem=send_sems.at[k],
                recv_sem=recv_sems.at[k], device_id=to, device_id_type=MESH)

        mine = pltpu.make_async_copy(x_ref, rows(*me), local_sem)
        mine.start()
        first = [copy(0, me, sibling, src=x_ref)]
        first += [copy(1 + j, me, (*chip, c), src=x_ref) for j, chip in enumerate(chips)]
        for cp in first:
            cp.start()
        passed = [copy(4 + j, (*chip, c), sibling) for j, chip in enumerate(chips)]
        for j, chip in enumerate(chips):
            copy(1 + j, (*chip, c), me).wait_recv()
            passed[j].start()
        copy(0, sibling, me).wait_recv()
        for j, chip in enumerate(chips):
            copy(4 + j, (*chip, 1 - c), me).wait_recv()
        for cp in first + passed:
            cp.wait_send()
        mine.wait()

    return pl.pallas_call(
        body, name=name, out_shape=jax.ShapeDtypeStruct((N_DEV * m_per, n), x_shard.dtype),
        in_specs=[pl.BlockSpec(memory_space=space)], out_specs=pl.BlockSpec(memory_space=space),
        scratch_shapes=[pltpu.SemaphoreType.DMA((7,)), pltpu.SemaphoreType.DMA((7,)), pltpu.SemaphoreType.DMA],
    )(x_shard)


_ANY = pl.BlockSpec(memory_space=pl.ANY)


def all_gather_weights(name, slabs):
    n = len(slabs)

    def body(*refs):
        outs = refs[n:2 * n]
        send_sems, recv_sems = refs[2 * n:]
        x, y, c = _me()
        me, sibling = (x, y, c), (x, y, 1 - c)
        chips = [(1 - x, y), (x, 1 - y), (1 - x, 1 - y)]

        def view(i, px, py, pc):
            hr = slabs[i].shape[2] // 2
            return outs[i].at[:, 2 * px + py, pl.ds(pc * hr, hr), :]

        def copy(i, k, block, to):
            return pltpu.make_async_remote_copy(
                src_ref=view(i, *block), dst_ref=view(i, *block), send_sem=send_sems.at[i, k],
                recv_sem=recv_sems.at[i, k], device_id=to, device_id_type=MESH)

        first = []
        for i in range(n):
            first.append(copy(i, 0, me, sibling))
            first += [copy(i, 1 + j, me, (*chip, c)) for j, chip in enumerate(chips)]
        for cp in first:
            cp.start()
        passed = []
        for j, chip in enumerate(chips):
            for i in range(n):
                copy(i, 1 + j, (*chip, c), me).wait_recv()
                passed.append(copy(i, 4 + j, (*chip, c), sibling))
                passed[-1].start()
        for i in range(n):
            copy(i, 0, sibling, me).wait_recv()
            for j, chip in enumerate(chips):
                copy(i, 4 + j, (*chip, 1 - c), me).wait_recv()
        for cp in first + passed:
            cp.wait_send()

    return pl.pallas_call(
        body, name=name, out_shape=[jax.ShapeDtypeStruct(s.shape, s.dtype) for s in slabs],
        in_specs=[_ANY] * n, out_specs=[_ANY] * n, input_output_aliases={i: i for i in range(n)},
        scratch_shapes=[pltpu.SemaphoreType.DMA((n, 7)), pltpu.SemaphoreType.DMA((n, 7))],
    )(*slabs)


def grad_sibling_exchange(name, gs):
    n = len(gs)

    def body(*refs):
        ins, outs = refs[:n], refs[n:2 * n]
        send_sems, recv_sems = refs[2 * n:]
        mx, my, mc = _me()
        cps = []
        for i in range(n):
            hr = gs[i].shape[1] // 2
            cps.append(pltpu.make_async_remote_copy(
                src_ref=ins[i].at[:, pl.ds((1 - mc) * hr, hr), :], dst_ref=outs[i], send_sem=send_sems.at[i],
                recv_sem=recv_sems.at[i], device_id=(mx, my, 1 - mc), device_id_type=MESH))
            cps[-1].start()
        for cp in cps:
            cp.wait()

    return pl.pallas_call(
        body, name=name, out_shape=[jax.ShapeDtypeStruct((N_CHIP, g.shape[1] // 2, g.shape[2]), g.dtype) for g in gs],
        in_specs=[_ANY] * n, out_specs=[_ANY] * n,
        scratch_shapes=[pltpu.SemaphoreType.DMA((n,)), pltpu.SemaphoreType.DMA((n,))],
    )(*gs)


def grad_chip_exchange(name, ps):
    n = len(ps)

    def body(*refs):
        ins, outs = refs[:n], refs[n:2 * n]
        send_sems, recv_sems = refs[2 * n:]
        mx, my, mc = _me()
        ci = 2 * mx + my
        chips = [(1 - mx, my), (mx, 1 - my), (1 - mx, 1 - my)]
        sends = []
        for i in range(n):
            for k, (px, py) in enumerate(chips):
                sends.append(pltpu.make_async_remote_copy(
                    src_ref=ins[i].at[2 * px + py], dst_ref=outs[i].at[ci], send_sem=send_sems.at[i, k],
                    recv_sem=recv_sems.at[i, k], device_id=(px, py, mc), device_id_type=MESH))
                sends[-1].start()
        for i in range(n):
            for k, (px, py) in enumerate(chips):
                pltpu.make_async_remote_copy(
                    src_ref=ins[i].at[ci], dst_ref=outs[i].at[2 * px + py], send_sem=send_sems.at[i, k],
                    recv_sem=recv_sems.at[i, k], device_id=(px, py, mc), device_id_type=MESH).wait_recv()
        for cp in sends:
            cp.wait_send()

    return pl.pallas_call(
        body, name=name, out_shape=[jax.ShapeDtypeStruct(p.shape, p.dtype) for p in ps], in_specs=[_ANY] * n,
        out_specs=[_ANY] * n, scratch_shapes=[pltpu.SemaphoreType.DMA((n, 3)), pltpu.SemaphoreType.DMA((n, 3))],
    )(*ps)


def grad_half_exchange(name, shards):
    n = len(shards)

    def body(*refs):
        outs = refs[n:2 * n]
        send_sems, recv_sems = refs[2 * n:]
        mx, my, mc = _me()

        def copy(i, core):
            hr = shards[i].shape[1] // 2
            rows = outs[i].at[:, pl.ds(core * hr, hr), :]
            return pltpu.make_async_remote_copy(src_ref=rows, dst_ref=rows, send_sem=send_sems.at[i],
                                                recv_sem=recv_sems.at[i], device_id=(mx, my, 1 - mc), device_id_type=MESH)

        sends = [copy(i, mc) for i in range(n)]
        for cp in sends:
            cp.start()
        for i in range(n):
            copy(i, 1 - mc).wait_recv()
        for cp in sends:
            cp.wait_send()

    return pl.pallas_call(
        body, name=name, out_shape=[jax.ShapeDtypeStruct(s.shape, s.dtype) for s in shards], in_specs=[_ANY] * n,
        out_specs=[_ANY] * n, input_output_aliases={i: i for i in range(n)},
        scratch_shapes=[pltpu.SemaphoreType.DMA((n,)), pltpu.SemaphoreType.DMA((n,))],
    )(*shards)


WEIGHTS = ['w_mod', 'b_mod', 'norm_mix_g', 'w_in', 'gdn_conv_w', 'gdn_a_log', 'gdn_dt_bias', 'gdn_norm_g', 'rg_conv_w',
           'rg_conv_b', 'rg_w_a', 'rg_b_a', 'rg_w_x', 'rg_b_x', 'rg_lambda', 'mla_q_norm_g', 'mla_w_qb', 'mla_kv_norm_g',
           'mla_w_kvb', 'w_out', 'norm_mlp_g', 'w_mlp_in', 'w_mlp_out', 'final_norm_g']
SHARDED = {'w_in': 2, 'gdn_conv_w': 2, 'rg_conv_w': 2, 'mla_w_qb': 2, 'mla_w_kvb': 2, 'w_out': 1, 'w_mlp_in': 2, 'w_mlp_out': 1}
GATHER_BF16 = ('w_in', 'mla_w_qb', 'mla_w_kvb', 'w_out', 'w_mlp_in', 'w_mlp_out')
REPLICATED = [n for n in WEIGHTS if n not in SHARDED and n != 'w_mod']
PACK_COLS = 1024


def _pack(arrays, multiple):
    flat = jnp.concatenate([a.reshape(-1) for a in arrays])
    pad = (-flat.shape[0]) % multiple
    return jnp.pad(flat, (0, pad)) if pad else flat


def _unpack(flat, shapes):
    out, o = [], 0
    for shp in shapes:
        n = int(np.prod(shp))
        out.append(flat[o:o + n].reshape(shp))
        o += n
    return out


def _pack_rows(arrays):
    rows = []
    for a in arrays:
        flat = a.reshape(-1)
        pad = (-flat.shape[0]) % PACK_COLS
        rows.append((jnp.pad(flat, (0, pad)) if pad else flat).reshape(-1, PACK_COLS))
    out = jnp.concatenate(rows, axis=0)
    pad = (-out.shape[0]) % 8
    return jnp.pad(out, ((0, pad), (0, 0))) if pad else out


def _unpack_rows(packed, shapes):
    out, r = [], 0
    for shp in shapes:
        n = int(np.prod(shp))
        nr = -(-n // PACK_COLS)
        piece = packed[r:r + nr]
        out.append((piece if n == nr * PACK_COLS else piece.reshape(-1)[:n]).reshape(shp))
        r += nr
    return out


def _unshard(stacked, axis):
    moved = jnp.moveaxis(stacked, 0, axis)
    shp = list(moved.shape)
    shp[axis:axis + 2] = [shp[axis] * shp[axis + 1]]
    return moved.reshape(shp)


def _shard(full, axis):
    shp = list(full.shape)
    shp[axis:axis + 1] = [N_CHIP, shp[axis] // N_CHIP]
    return jnp.moveaxis(full.reshape(shp), axis, 0)


def _proj_cols(w):
    pad = jnp.zeros(w.shape[:-1] + (PROJ_WIDTH - w.shape[-1],), w.dtype)
    return jnp.concatenate([w[..., 0:1024], w[..., 1032:2472], w[..., 1024:1032], pad], axis=-1)


def _proj_cols_back(d):
    return jnp.concatenate([d[..., 0:1024], d[..., 2464:2472], d[..., 1024:2464]], axis=-1)


def _heads_split(w, heads, first):
    per = w.shape[-1] // heads
    r = w.reshape(w.shape[:-1] + (heads, per))
    lead = w.shape[:-1]
    return jnp.concatenate([r[..., :first].reshape(lead + (heads * first,)),
                            r[..., first:].reshape(lead + (heads * (per - first),))], axis=-1)


def _heads_merge(d, heads, first):
    lead = d.shape[:-1]
    per = d.shape[-1] // heads
    a = d[..., :heads * first].reshape(lead + (heads, first))
    b = d[..., heads * first:].reshape(lead + (heads, per - first))
    return jnp.concatenate([a, b], axis=-1).reshape(lead + (heads * per,))


def _block_diag(w):
    nl = w.shape[0]
    eye = jnp.eye(2, dtype=w.dtype)
    return jnp.einsum('lcoij,op->lcoipj', w.reshape(nl, 4, 2, 64, 64), eye).reshape(nl, 4, 128, 128)


def _block_diag_back(g):
    nl = g.shape[0]
    return jnp.einsum('lcoipj,op->lcoij', g.reshape(nl, 4, 2, 64, 2, 64), jnp.eye(2, dtype=g.dtype)).reshape(nl, 8, 64, 64)


def kernel(x, c, positions, w_mod, b_mod, norm_mix_g, w_in, gdn_conv_w, gdn_a_log, gdn_dt_bias, gdn_norm_g, rg_conv_w, rg_conv_b, rg_w_a, rg_b_a, rg_w_x, rg_b_x, rg_lambda, mla_q_norm_g, mla_w_qb, mla_kv_norm_g, mla_w_kvb, w_out, norm_mlp_g, w_mlp_in, w_mlp_out, final_norm_g, loss_target, m_w_mod, m_b_mod, m_norm_mix_g, m_w_in, m_gdn_conv_w, m_gdn_a_log, m_gdn_dt_bias, m_gdn_norm_g, m_rg_conv_w, m_rg_conv_b, m_rg_w_a, m_rg_b_a, m_rg_w_x, m_rg_b_x, m_rg_lambda, m_mla_q_norm_g, m_mla_w_qb, m_mla_kv_norm_g, m_mla_w_kvb, m_w_out, m_norm_mlp_g, m_w_mlp_in, m_w_mlp_out, m_final_norm_g, v_w_mod, v_b_mod, v_norm_mix_g, v_w_in, v_gdn_conv_w, v_gdn_a_log, v_gdn_dt_bias, v_gdn_norm_g, v_rg_conv_w, v_rg_conv_b, v_rg_w_a, v_rg_b_a, v_rg_w_x, v_rg_b_x, v_rg_lambda, v_mla_q_norm_g, v_mla_w_qb, v_mla_kv_norm_g, v_mla_w_kvb, v_w_out, v_norm_mlp_g, v_w_mlp_in, v_w_mlp_out, v_final_norm_g):
    given = dict(locals())
    wts = {n: given[n] for n in WEIGHTS}
    mom_m = {n: given["m_" + n] for n in WEIGHTS}
    mom_v = {n: given["v_" + n] for n in WEIGHTS}
    bsz, seq, d = x.shape
    depth = w_mod.shape[0]
    mx, my, mc = lax.axis_index("x"), lax.axis_index("y"), lax.axis_index("c")
    chip = 2 * mx + my
    dev = 2 * chip + mc

    conv_shapes = [wts['gdn_conv_w'].shape, wts['rg_conv_w'].shape]
    conv_flat = _pack([wts['gdn_conv_w'], wts['rg_conv_w']], d)
    conv_rows = conv_flat.shape[0] // d
    assert bsz + conv_rows <= 8
    c_pad = jnp.concatenate([c, conv_flat.reshape(conv_rows, d), jnp.zeros((8 - bsz - conv_rows, d), F32)], axis=0)
    gath = all_gather8("gather_c", c_pad, True).reshape(N_DEV, 8, d)
    c_all = gath[:, :bsz].reshape(N_DEV * bsz, d)
    conv_all = gath[0::2, bsz:bsz + conv_rows].reshape(N_CHIP, conv_rows * d)
    gdn_conv_full, rg_conv_full = [
        _unshard(jnp.stack([_unpack(conv_all[s], conv_shapes)[i] for s in range(N_CHIP)]), 2) for i in range(2)]

    n_half = N_DEV * bsz // 2
    mod_cols = w_mod.shape[2]
    c_rows = lax.dynamic_slice(c_all, (n_half * mc, 0), (n_half, d))
    b_mod_mine = lax.dynamic_slice(b_mod, (0, chip * mod_cols), (depth, mod_cols)).reshape(depth, 1, mod_cols)
    mod_piece = mod_matmul(c_rows, w_mod, b_mod_mine)
    mod_g = all_gather8("gather_mod", mod_piece.reshape(depth * n_half, mod_cols), True)
    mod_all = mod_g.reshape(N_CHIP, 2, depth, n_half, mod_cols).transpose(2, 1, 3, 0, 4).reshape(depth, 2 * n_half, 6 * d)
    mod_mine = lax.dynamic_slice(mod_all, (0, bsz * dev, 0), (depth, bsz, 6 * d)).reshape(depth, bsz, 6, 1, d)

    ids = jnp.stack([chip, mc]).astype(jnp.int32)
    slabs = dict(zip(GATHER_BF16, all_gather_weights(
        "gather_weights", [cast_into_slab("cast_" + n, wts[n], ids) for n in GATHER_BF16])))

    def columns(g):
        return g.transpose(0, 2, 1, 3).reshape(g.shape[0], g.shape[2], N_CHIP * g.shape[3])

    def rows_of(g):
        return g.reshape(g.shape[0], N_CHIP * g.shape[2], g.shape[3])

    w_cat = _proj_cols(columns(slabs['w_in']))
    w_q = columns(slabs['mla_w_qb']).astype(F32)
    w_kv = columns(slabs['mla_w_kvb']).astype(F32)
    w_out_full, w_mlp_out_full = rows_of(slabs['w_out']), rows_of(slabs['w_mlp_out'])
    bd_a, bd_x = _block_diag(rg_w_a), _block_diag(rg_w_x)

    inv_freq = ROPE_THETA ** (-jnp.arange(0, 32, 2, dtype=F32) / 32.0)
    ang = positions.astype(F32)[..., None] * inv_freq
    cs = jnp.concatenate([jnp.cos(ang), jnp.sin(ang)], axis=-1)

    proj_ch = [w for _, w in PROJ_PIECES]

    def row(a, l):
        return a[l].reshape(1, -1)

    def layer_args(l):
        sh_m, sc_m, gt_m, sh_f, sc_f, gt_f = (mod_mine[l, :, k] for k in range(6))
        return dict(
            mods=(sh_m, sc_m, gt_m, sh_f, sc_f, gt_f),
            mixer_in=dict(ex=[sc_m, sh_m], par=[row(norm_mix_g, l)], big=[(w_cat, l)], out_ch=proj_ch, ts=512),
            gdn_conv=dict(par_tiled=[gdn_conv_full[l]], out_ch=[768], ts=seq, nc=3),
            gdn_local=dict(par=[row(gdn_a_log, l), row(gdn_dt_bias, l)], out_ch=[256] * 6, ts=512),
            rglru=dict(par_tiled=[rg_conv_full[l], row(rg_conv_b, l), row(rg_b_a, l), row(rg_b_x, l), row(rg_lambda, l),
                                  bd_a[l], bd_x[l]], out_ch=[512], ts=seq, nc=4),
            mla_pre=dict(tok_nd=[cs], par=[row(mla_q_norm_g, l), row(mla_kv_norm_g, l), w_q[l], w_kv[l]],
                         out_ch=[ATTN_QW, ATTN_QW, ATTN_VW], ts=512),
            out_proj=dict(ex=[gt_m], big=[(w_out_full, l)], out_ch=[d], ts=512),
            mlp_in=dict(ex=[sc_f, sh_f], par=[row(norm_mlp_g, l)], big=[(slabs['w_mlp_in'], l)], out_ch=[4 * d], ts=256),
            mlp_out=dict(ex=[gt_f], big=[(w_mlp_out_full, l)], out_ch=[d], ts=256),
        )

    saved = []
    h = x
    for l in range(depth):
        a = layer_args(l)
        sfx = str(l)
        qkv_raw, z, rx, rgate, mq, mkv, misc = run_stage("mixer_in" + sfx, fn_mixer_in, tok=[h], **a['mixer_in'])
        (qkv_act,) = run_stage("gdn_conv" + sfx, fn_gdn_conv, tok=[qkv_raw], **a['gdn_conv'])
        xs = run_stage("gdn_local" + sfx, fn_gdn_local, tok=[qkv_act, misc], **a['gdn_local'])
        o_a, st_in = gdn_scan(xs, z, row(gdn_norm_g, l))
        (o_b,) = run_stage("rglru" + sfx, fn_rglru, tok=[rx, rgate], **a['rglru'])
        q_at, k_at, v_at = run_stage("mla_pre" + sfx, fn_mla_pre, tok=[mq, mkv, misc], **a['mla_pre'])
        o_c = mla_attention(q_at, k_at, v_at)
        (h_mid,) = run_stage("out_proj" + sfx, fn_out_proj, tok=[h, o_a, o_b, o_c], **a['out_proj'])
        (a_mlp,) = run_stage("mlp_in" + sfx, fn_mlp_in, tok=[h_mid], **a['mlp_in'])
        (h_out,) = run_stage("mlp_out" + sfx, fn_mlp_out, tok=[h_mid, a_mlp], **a['mlp_out'])
        saved.append(dict(h=h, qkv_raw=qkv_raw, z=z, rx=rx, rgate=rgate, mq=mq, mkv=mkv, misc=misc, qkv_act=qkv_act, xs=xs,
                          st_in=st_in, o_a=o_a, o_b=o_b, o_c=o_c, q_at=q_at, k_at=k_at, v_at=v_at, h_mid=h_mid, a_mlp=a_mlp))
        h = h_out

    loss_part, dh, d_final_g = loss_head(h, final_norm_g.reshape(1, d), loss_target)
    loss = lax.psum(loss_part[0, 0], ("x", "y", "c"))

    g_full = {n: [None] * depth for n in SHARDED}
    g_rep = {n: [None] * depth for n in REPLICATED if n not in ('final_norm_g', 'b_mod')}

    def column_slabs(g):
        return g.reshape(g.shape[0], N_CHIP, g.shape[1] // N_CHIP).transpose(1, 0, 2)

    def row_slabs(g):
        return g.reshape(N_CHIP, g.shape[0] // N_CHIP, g.shape[1])
    dmod = [None] * depth
    for l in reversed(range(depth)):
        a, sv = layer_args(l), saved[l]
        sfx = str(l)
        mlp_out_tok = dict(tok=[sv['h_mid'], sv['a_mlp']], cot=[dh])
        (dh_mid, da_mlp), (dgt_f,), _, _, _ = run_stage(
            "mlp_out" + sfx, fn_mlp_out, which="small", dtok_dtype={1: BF16}, **mlp_out_tok, **{**a['mlp_out'], 'ts': 256})
        _, _, _, _, (dw_mlp_out,) = run_stage(
            "mlp_out" + sfx, fn_mlp_out, which="big", **mlp_out_tok, **{**a['mlp_out'], 'ts': 512})
        g_full['w_mlp_out'][l] = row_slabs(dw_mlp_out)
        _, _, _, _, (g_full['w_mlp_in'][l],) = run_stage(
            "mlp_in" + sfx, fn_mlp_in, tok=[sv['h_mid']], cot=[da_mlp], which="big", **{**a['mlp_in'], 'ts': 512})
        (dh_mid,), (dsc_f, dsh_f), (g_rep['norm_mlp_g'][l],), _, _ = run_stage(
            "mlp_in" + sfx, fn_mlp_in, tok=[sv['h_mid']], cot=[da_mlp], addin=dh_mid, which="small", **a['mlp_in'])
        (dh_in, do_a, do_b, do_c), (dgt_m,), _, _, (dw_out,) = run_stage(
            "out_proj" + sfx, fn_out_proj, tok=[sv['h'], sv['o_a'], sv['o_b'], sv['o_c']], cot=[dh_mid], **a['out_proj'])
        g_full['w_out'][l] = row_slabs(dw_out)
        attn_cot = mla_attention_bwd(sv['q_at'], sv['k_at'], sv['v_at'], do_c)
        (dmq, dmkv, dmisc_c), _, (g_rep['mla_q_norm_g'][l], g_rep['mla_kv_norm_g'][l], dw_q, dw_kv), _, _ = run_stage(
            "mla_pre" + sfx, fn_mla_pre, tok=[sv['mq'], sv['mkv'], sv['misc']], cot=attn_cot, **a['mla_pre'])
        g_full['mla_w_qb'][l] = column_slabs(dw_q)
        g_full['mla_w_kvb'][l] = column_slabs(dw_kv)
        (drx, drgate), _, _, rg_g, _ = run_stage("rglru" + sfx, fn_rglru, tok=[sv['rx'], sv['rgate']], cot=[do_b], **a['rglru'])
        (g_full['rg_conv_w'][l], g_rep['rg_conv_b'][l], g_rep['rg_b_a'][l], g_rep['rg_b_x'][l], g_rep['rg_lambda'][l],
         g_rep['rg_w_a'][l], g_rep['rg_w_x'][l]) = rg_g
        dxs, dz, g_rep['gdn_norm_g'][l] = gdn_scan_bwd(sv['xs'], sv['z'], row(gdn_norm_g, l), sv['st_in'], do_a)
        (dqkv_act, dmisc_a), _, (g_rep['gdn_a_log'][l], g_rep['gdn_dt_bias'][l]), _, _ = run_stage(
            "gdn_local" + sfx, fn_gdn_local, tok=[sv['qkv_act'], sv['misc']], cot=dxs, **a['gdn_local'])
        (dqkv_raw,), _, _, (g_full['gdn_conv_w'][l],), _ = run_stage(
            "gdn_conv" + sfx, fn_gdn_conv, tok=[sv['qkv_raw']], cot=[dqkv_act], **a['gdn_conv'])
        (dh,), (dsc_m, dsh_m), (g_rep['norm_mix_g'][l],), _, (dw_cat,) = run_stage(
            "mixer_in" + sfx, fn_mixer_in, tok=[sv['h']], cot=[dqkv_raw, dz, drx, drgate, dmq, dmkv, dmisc_a + dmisc_c],
            addin=dh_in, **a['mixer_in'])
        g_full['w_in'][l] = column_slabs(_proj_cols_back(dw_cat))
        dmod[l] = jnp.concatenate([dsh_m, dsc_m, dgt_m, dsh_f, dsc_f, dgt_f], axis=-1).reshape(bsz, 6 * d)
    grad_x = dh

    dmod = jnp.stack(dmod)
    dmod_pad = jnp.concatenate([dmod.reshape(depth * bsz, 6 * d), jnp.zeros((8 - depth * bsz, 6 * d), F32)], axis=0)
    dmod_all = all_gather8("gather_dmod", dmod_pad, True).reshape(N_DEV, 8, 6 * d)[:, :depth * bsz]
    dmod_all = dmod_all.reshape(N_DEV, depth, bsz, 6 * d).transpose(1, 0, 2, 3).reshape(depth, N_DEV * bsz, 6 * d)
    g_w_mod = mod_weight_grad(c_all, lax.dynamic_slice(dmod_all, (0, 0, chip * mod_cols), (depth, N_DEV * bsz, mod_cols)))

    g_rep = {n: jnp.stack(v) for n, v in g_rep.items()}
    g_rep['rg_w_a'] = _block_diag_back(g_rep['rg_w_a'])
    g_rep['rg_w_x'] = _block_diag_back(g_rep['rg_w_x'])
    g_rep['final_norm_g'] = d_final_g
    g_rep['b_mod'] = jnp.sum(dmod, axis=1)
    conv_names = ['gdn_conv_w', 'rg_conv_w']
    conv_full_shapes = [(depth,) + g_full[n][0].shape for n in conv_names]
    small_shapes = [wts[n].shape for n in REPLICATED] + conv_full_shapes
    rep_part = _pack_rows([g_rep[n].reshape(wts[n].shape) for n in REPLICATED] + [jnp.stack(g_full[n]) for n in conv_names])
    rep_rows = rep_part.shape[0]
    rep_all = all_gather8("gather_small_grads", rep_part, True).reshape(N_DEV, rep_rows, PACK_COLS)
    conv_zeros = [jnp.zeros(s, F32) for s in conv_full_shapes]
    rep_out = adamw_reduce("adamw_small", rep_all, *[
        _pack_rows([src[n] for n in REPLICATED] + conv_zeros) for src in (wts, mom_m, mom_v)])
    small_names = REPLICATED + conv_names
    rep_g, rep_d, rep_m, rep_v = [dict(zip(small_names, _unpack_rows(o, small_shapes))) for o in rep_out]
    sh_g = {}
    for n in conv_names:
        cols = wts[n].shape[2]
        sh_g[n] = lax.dynamic_slice(rep_g.pop(n), (0, 0, chip * cols), wts[n].shape)
        for dct in (rep_d, rep_m, rep_v):
            dct.pop(n)

    core_id = mc.reshape(1).astype(jnp.int32)
    units = [(i, l) for i in range(len(GATHER_BF16)) for l in range(depth)]
    gs = [g_full[GATHER_BF16[i]][l] for i, l in units]
    from_sibling = grad_sibling_exchange("grad_sibling_exchange", gs)
    sums32, sums16 = zip(*[add_half("grad_add_%s%d" % (GATHER_BF16[i], l), g, s, core_id)
                           for (i, l), g, s in zip(units, gs, from_sibling)])
    from_chips = grad_chip_exchange("grad_chip_exchange", list(sums16))
    shards = [None] * len(GATHER_BF16)
    for (i, l), p, r in zip(units, sums32, from_chips):
        n = GATHER_BF16[i]
        shards[i] = sum_peers("grad_sum_%s%d" % (n, l), p, r, ids, wts[n].shape, l, acc=shards[i])
    sh_g.update(zip(GATHER_BF16, grad_half_exchange("grad_half_exchange", shards)))
    sh_names = list(SHARDED)

    def as2d(t):
        return t.reshape(-1, t.shape[-1])

    sh_d, sh_m, sh_v = {}, {}, {}
    for n in sh_names + ['w_mod']:
        g = g_w_mod if n == 'w_mod' else sh_g[n]
        res = adamw("adamw_" + n, as2d(wts[n]), as2d(g), as2d(mom_m[n]), as2d(mom_v[n]))
        sh_d[n], sh_m[n], sh_v[n] = (r.reshape(wts[n].shape) for r in res)
    sh_g['w_mod'] = g_w_mod

    def pick(shd, rep):
        return [shd[n] if n in shd else rep[n] for n in WEIGHTS]

    return (loss, grad_x, *pick(sh_g, rep_g), *pick(sh_d, rep_d), *pick(sh_m, rep_m), *pick(sh_v, rep_v))
```

```python
import functools

import jax
import jax.numpy as jnp
import numpy as np
from jax import lax
from jax.experimental import pallas as pl
from jax.experimental.pallas import tpu as pltpu

F32, BF16 = jnp.float32, jnp.bfloat16
HI = lax.Precision.HIGH
MESH = pl.DeviceIdType.MESH

EPS = 1e-6
CHUNK = 64
GDN_HEADS = 4
MLA_HEADS = 4
RG_C = 8.0
ROPE_THETA = 10000.0
N_DEV = 8
N_CHIP = 4
V7X_VMEM_LIMIT = 60 * 1024 * 1024
ADAM_LR, ADAM_B1, ADAM_B2, ADAM_EPS, ADAM_WD, ADAM_STEP = 0.001, 0.9, 0.999, 1e-08, 0.01, 10


def _params(n_grid):
    return pltpu.CompilerParams(dimension_semantics=("arbitrary",) * n_grid, vmem_limit_bytes=V7X_VMEM_LIMIT)


def _dot(a, b, dims=(((1,), (0,)), ((), ()))):
    return lax.dot_general(a.astype(BF16), b.astype(BF16), dims, preferred_element_type=F32)


@jax.custom_vjp
def _mm_probe(x, w, probe):
    return _dot(x, w)


def _mm_probe_fwd(x, w, probe):
    return _dot(x, w), (x, w)


def _mm_probe_bwd(res, dy):
    x, w = res
    dx = _dot(dy, w, (((1,), (1,)), ((), ())))
    dw = _dot(x, dy, (((0,), (0,)), ((), ())))
    return dx, jnp.zeros_like(w), dw


_mm_probe.defvjp(_mm_probe_fwd, _mm_probe_bwd)


@jax.custom_vjp
def _probe_only(x, probe):
    return jnp.zeros((x.shape[0], probe.shape[1]), F32)


def _probe_only_fwd(x, probe):
    return jnp.zeros((x.shape[0], probe.shape[1]), F32), x


def _probe_only_bwd(x, dy):
    return jnp.zeros_like(x), _dot(x, dy, (((0,), (0,)), ((), ())))


_probe_only.defvjp(_probe_only_fwd, _probe_only_bwd)


@jax.custom_vjp
def mmw(x, w):
    return _dot(x, w)


def _mmw_fwd(x, w):
    return _dot(x, w), (x, w)


def _mmw_bwd(res, dy):
    x, w = res
    return _dot(dy, w, (((1,), (1,)), ((), ()))), _dot(x, dy, (((0,), (0,)), ((), ())))


mmw.defvjp(_mmw_fwd, _mmw_bwd)


def rms(x, g):
    return x * lax.rsqrt(jnp.mean(x * x, axis=-1, keepdims=True) + EPS) * g


def _rows(shape):
    return lax.broadcasted_iota(jnp.int32, shape, 0)


def _shift_down(x, s, fill):
    return jnp.where(_rows(x.shape) < s, fill, pltpu.roll(x, s, 0))


def _shift_up(x, s, fill):
    n = x.shape[0]
    return jnp.where(_rows(x.shape) >= n - s, fill, pltpu.roll(x, n - s, 0))


def _make_tshift(s):
    @jax.custom_vjp
    def tshift(x):
        return _shift_down(x, s, 0.0)

    tshift.defvjp(lambda x: (_shift_down(x, s, 0.0), None), lambda _, dy: (_shift_up(dy, s, 0.0),))
    return tshift


_TSHIFT = {s: _make_tshift(s) for s in (1, 2, 3)}


def causal_conv4(x, w):
    y = x * w[3:4, :]
    for j in range(3):
        y = y + _TSHIFT[3 - j](x) * w[j:j + 1, :]
    return y


def _scan_steps(n):
    d = 1
    while d < n:
        yield d
        d *= 2


@jax.custom_vjp
def linscan(a, b):
    return _linscan_fwd_impl(a, b)


def _linscan_fwd_impl(a, b):
    for d in _scan_steps(a.shape[0]):
        b = a * _shift_down(b, d, 0.0) + b
        a = a * _shift_down(a, d, 1.0)
    return b


def _linscan_fwd(a, b):
    h = _linscan_fwd_impl(a, b)
    return h, (a, h)


def _linscan_bwd(res, dh):
    a, h = res
    an = _shift_up(a, 1, 0.0)
    lam = dh
    for d in _scan_steps(a.shape[0]):
        lam = an * _shift_up(lam, d, 0.0) + lam
        an = an * _shift_up(an, d, 1.0)
    return lam * _shift_down(h, 1, 0.0), lam


linscan.defvjp(_linscan_fwd, _linscan_bwd)


def _bmm(a, b, precision=None):
    return jnp.einsum('nij,njk->nik', a, b, precision=precision, preferred_element_type=F32)


@jax.custom_vjp
def inv_unit_lower(l):
    return _inv_impl(l)


def _inv_impl(l):
    n = l.shape[-1]
    eye = (_rows((n, n)) == lax.broadcasted_iota(jnp.int32, (n, n), 1)).astype(F32)
    p = -l
    a = eye + p
    k = 1
    while 2 * k < n:
        p = _bmm(p, p, HI)
        a = a + _bmm(a, p, HI)
        k *= 2
    return a


def _inv_fwd(l):
    a = _inv_impl(l)
    return a, a


def _inv_bwd(a, da):
    at = jnp.swapaxes(a, 1, 2)
    return (-_bmm(_bmm(at, da, HI), at, HI),)


inv_unit_lower.defvjp(_inv_fwd, _inv_bwd)


def neg_expm1(y):
    series = -(y * (1.0 + y * (0.5 + y * (1.0 / 6.0 + y * (1.0 / 24.0)))))
    return jnp.where(y > -0.05, series, 1.0 - jnp.exp(y))


def run_stage(name, fn, *, tok, tok_nd=(), ex=(), par=(), par_tiled=(), big=(), out_ch, ts, nc=1, cot=None, addin=None,
              which="all", dtok_dtype=None, side=None):
    tok, tok_nd, ex, par, par_tiled, big = map(list, (tok, tok_nd, ex, par, par_tiled, big))
    big_layer = [b[1] if isinstance(b, tuple) else None for b in big]
    big_arrays = [b[0] if isinstance(b, tuple) else b for b in big]
    big = [jax.ShapeDtypeStruct(a.shape if lyr is None else a.shape[1:], a.dtype) for a, lyr in zip(big_arrays, big_layer)]
    bsz, seq, _ = tok[0].shape
    ts = min(ts, seq)
    ns = seq // ts
    grid = (nc, bsz, ns)

    def tok_spec(a):
        cb = a.shape[-1] // nc
        return pl.BlockSpec((None, ts, cb), lambda c, b, s: (b, s, c))

    def ex_spec(a):
        cb = a.shape[-1] // nc
        return pl.BlockSpec((None, 1, cb), lambda c, b, s: (b, 0, c))

    def full_spec(a, single=False):
        nd = a.ndim
        kw = dict(pipeline_mode=pl.Buffered(1)) if single else {}
        return pl.BlockSpec(a.shape, lambda c, b, s: (0,) * nd, **kw)

    def tiled_spec(a):
        if a.ndim == 2:
            return pl.BlockSpec((a.shape[0], a.shape[1] // nc), lambda c, b, s: (0, c))
        return pl.BlockSpec((None,) + a.shape[1:], lambda c, b, s: (c, 0, 0))

    def big_spec(a, lyr):
        if lyr is None:
            return full_spec(a, True)
        nd = a.ndim
        return pl.BlockSpec((None,) + a.shape[1:], lambda c, b, s: (lyr,) + (0,) * (nd - 1), pipeline_mode=pl.Buffered(1))

    n_tok, n_nd, n_ex, n_par, n_pt, n_big = map(len, (tok, tok_nd, ex, par, par_tiled, big))
    in_arrays = tok + tok_nd + ex + par + par_tiled + big_arrays
    in_specs = ([tok_spec(a) for a in tok + tok_nd] + [ex_spec(a) for a in ex] + [full_spec(a) for a in par]
                + [tiled_spec(a) for a in par_tiled] + [big_spec(a, lyr) for a, lyr in zip(big_arrays, big_layer)])
    out_tok_shapes = [jax.ShapeDtypeStruct((bsz, seq, ch), F32) for ch in out_ch]
    n_in = len(in_arrays)

    def split(vals):
        i = 0
        groups = []
        for n in (n_tok, n_nd, n_ex, n_par, n_pt, n_big):
            groups.append(list(vals[i:i + n]))
            i += n
        return groups

    def split_grads(vals):
        i = 0
        groups = []
        for n in (n_tok, n_ex, n_par, n_pt, n_big):
            groups.append(list(vals[i:i + n]))
            i += n
        return groups

    if cot is None:
        side = list(side or [])
        side_arrays = [a for job in side for a in job[0]]
        n_side, n_out = len(side_arrays), len(out_tok_shapes)

        def body(*refs):
            tv, ndv, ev, pv, ptv, _ = split([r[...] for r in refs[:n_in - n_big]] + [None] * n_big)
            b_refs = refs[n_in - n_big:n_in]
            side_refs = refs[n_in + n_side + n_out:n_in + 2 * n_side + n_out]
            sems = refs[n_in + 2 * n_side + n_out:]
            c, b, s = pl.program_id(0), pl.program_id(1), pl.program_id(2)

            def jobs():
                o = 0
                for j, (arrs, _, issue) in enumerate(side):
                    yield issue(side_refs[o:o + len(arrs)], sems[2 * j], sems[2 * j + 1])
                    o += len(arrs)

            if side:
                @pl.when(jnp.logical_and(jnp.logical_and(c == 0, b == 0), s == 0))
                def _():
                    for starts, _, _ in jobs():
                        for cp in starts:
                            cp.start()

            outs = fn(tv, ndv, ev, pv, ptv, lambda x, i, j=None: _dot(x, b_refs[i][...] if j is None else b_refs[i][j]))
            for r, o in zip(refs[n_in + n_side:], outs):
                r[...] = o

            if side:
                @pl.when(jnp.logical_and(jnp.logical_and(c == nc - 1, b == bsz - 1), s == ns - 1))
                def _():
                    for _, recv_waits, send_waits in jobs():
                        for cp in recv_waits:
                            cp.wait_recv()
                        for cp in send_waits:
                            cp.wait_send()

        res = pl.pallas_call(
            body, name=name, grid=grid, in_specs=in_specs + [_ANY] * n_side,
            out_specs=[tok_spec(a) for a in out_tok_shapes] + [_ANY] * n_side,
            out_shape=out_tok_shapes + [jax.ShapeDtypeStruct(a.shape, a.dtype) for a in side_arrays],
            input_output_aliases={n_in + j: n_out + j for j in range(n_side)},
            scratch_shapes=[pltpu.SemaphoreType.DMA((job[1],)) for job in side for _ in range(2)],
            compiler_params=_params(3))(*in_arrays, *side_arrays)
        return (res[:n_out], res[n_out:]) if side else res

    cot = list(cot)
    has_addin = addin is not None
    extra = cot + ([addin] if has_addin else [])
    n_cot = len(cot)
    want_small, want_big = which in ("all", "small"), which in ("all", "big")
    if not want_small:
        in_arrays, in_specs, n_in = in_arrays[:n_in - n_big], in_specs[:n_in - n_big], n_in - n_big
    small_arrays = tok + ex + par + par_tiled
    g_shapes = [jax.ShapeDtypeStruct(a.shape, F32) for a in (small_arrays if want_small else []) + (big if want_big else [])]
    for i, dt_ in (dtok_dtype or {}).items():
        g_shapes[i] = jax.ShapeDtypeStruct(g_shapes[i].shape, dt_)
    g_specs = (([tok_spec(a) for a in tok] + [ex_spec(a) for a in ex] + [full_spec(a) for a in par]
                + [tiled_spec(a) for a in par_tiled]) if want_small else []) + (
                    [full_spec(a, True) for a in big] if want_big else [])

    def body(*refs):
        c, b, s = pl.program_id(0), pl.program_id(1), pl.program_id(2)
        n_small_in = n_tok + n_nd + n_ex + n_par + n_pt
        tv, ndv, ev, pv, ptv, _ = split([r[...] for r in refs[:n_small_in]] + [None] * n_big)
        b_refs = refs[n_small_in:n_in]
        cots = [r[...].astype(F32) for r in refs[n_in:n_in + n_cot]]
        g_refs = list(refs[n_in + len(extra):])
        probes = [jnp.zeros(w.shape, F32) if w.ndim == 2 else [jnp.zeros(w.shape[1:], F32) for _ in range(w.shape[0])]
                  for w in big]

        def f(tv_, ev_, pv_, ptv_, probes_):
            def mm(x, i, j=None):
                probe = None if probes_ is None else (probes_[i] if j is None else probes_[i][j])
                if not want_small:
                    return _probe_only(x, probe)
                w = b_refs[i][...] if j is None else b_refs[i][j]
                return _dot(x, w) if probe is None else _mm_probe(x, w, probe)

            return fn(tv_, ndv, ev_, pv_, ptv_, mm)

        dt = de = dp = dpt = dbg = ()
        if which == "all":
            dt, de, dp, dpt, dbg = jax.vjp(f, tv, ev, pv, ptv, probes)[1](cots)
        elif which == "small":
            dt, de, dp, dpt = jax.vjp(lambda *a: f(*a, None), tv, ev, pv, ptv)[1](cots)
        else:
            (dbg,) = jax.vjp(lambda p: f(tv, ev, pv, ptv, p), probes)[1](cots)
        if has_addin:
            dt = [dt[0] + refs[n_in + n_cot][...]] + list(dt[1:])
        if want_small:
            gt_r, ge_r, gp_r, gpt_r, gb_r = split_grads(g_refs + ([] if want_big else [None] * n_big))
        else:
            gt_r, ge_r, gp_r, gpt_r, gb_r = [], [], [], [], g_refs
        for r, g in zip(gt_r, dt):
            r[...] = g.astype(r.dtype)

        def accumulate(r, g, first):
            @pl.when(first)
            def _():
                r[...] = g

            @pl.when(jnp.logical_not(first))
            def _():
                r[...] += g

        for r, g in zip(ge_r, de):
            accumulate(r, g, s == 0)
        first_all = jnp.logical_and(jnp.logical_and(c == 0, b == 0), s == 0)
        for r, g in zip(gp_r, dp):
            accumulate(r, g, first_all)
        for r, g in zip(gpt_r, dpt):
            accumulate(r, g, jnp.logical_and(b == 0, s == 0))
        for r, g in zip(gb_r, dbg):
            if isinstance(g, (list, tuple)):
                for j, gj in enumerate(g):
                    accumulate(r.at[j], gj, first_all)
            else:
                accumulate(r, g, first_all)

    res = pl.pallas_call(
        body, name=name + "_bwd" + ("" if which == "all" else "_" + which), grid=grid,
        in_specs=in_specs + [tok_spec(a) for a in extra], out_specs=g_specs, out_shape=g_shapes,
        compiler_params=_params(3))(*in_arrays, *extra)
    res = list(res)
    if not want_small:
        return [[], [], [], [], res]
    return split_grads(res + ([] if want_big else [None] * n_big))


PROJ_PIECES = (("qkv", 768), ("z", 256), ("rx", 512), ("rgate", 512), ("mq", 256), ("mkv", 128), ("misc", 128))
PROJ_WIDTH = sum(w for _, w in PROJ_PIECES)
MISC_KR, MISC_A, MISC_B = 0, 32, 36


def fn_mixer_in(tok, nd, ex, par, pt, mm):
    (h,), (sc, sh), (g,) = tok, ex, par
    proj = mm(rms(h, g) * (1.0 + sc) + sh, 0)
    outs, o = [], 0
    for _, w in PROJ_PIECES:
        outs.append(proj[:, o:o + w])
        o += w
    return outs


def fn_gdn_conv(tok, nd, ex, par, pt, mm):
    return [jax.nn.silu(causal_conv4(tok[0], pt[0]))]


def _tri_masks():
    r = _rows((CHUNK, CHUNK))
    c = lax.broadcasted_iota(jnp.int32, (CHUNK, CHUNK), 1)
    return (c <= r), (c < r)


def fn_gdn_local(tok, nd, ex, par, pt, mm):
    (qkv, misc), (a_log, dt_bias) = tok, par
    ts = qkv.shape[0]
    nb = ts // CHUNK
    lower, strict = _tri_masks()
    tril = jnp.broadcast_to(lower.astype(F32), (nb, CHUNK, CHUNK))
    ones = jnp.ones((nb, CHUNK, CHUNK), F32)
    outs = [[] for _ in range(6)]
    for hd in range(GDN_HEADS):
        def head(x, base):
            return x[:, base + 64 * hd: base + 64 * hd + 64]

        def l2n(x):
            return x * lax.rsqrt(jnp.sum(x * x, axis=-1, keepdims=True) + EPS)

        q = (l2n(head(qkv, 0)) * (64.0 ** -0.5)).reshape(nb, CHUNK, 64)
        k = l2n(head(qkv, 256)).reshape(nb, CHUNK, 64)
        v = head(qkv, 512).reshape(nb, CHUNK, 64)
        a = misc[:, MISC_A + hd: MISC_A + hd + 1]
        b = misc[:, MISC_B + hd: MISC_B + hd + 1]
        g = -jnp.exp(a_log[:, hd:hd + 1]) * jax.nn.softplus(a + dt_bias[:, hd:hd + 1])
        beta = jax.nn.sigmoid(b).reshape(nb, CHUNK, 1)
        gb = jnp.broadcast_to(g.reshape(nb, CHUNK, 1), (nb, CHUNK, CHUNK))
        gi = _bmm(tril, gb, HI)
        gl = _bmm(ones, gb, HI)
        diff = gi - jnp.swapaxes(gi, 1, 2)
        decay = jnp.where(lower, jnp.exp(jnp.where(lower, diff, 0.0)), 0.0)
        kb = k * beta
        vb = v * beta
        kk = jnp.einsum('ncd,nsd->ncs', kb.astype(BF16), k.astype(BF16), preferred_element_type=F32)
        amat = inv_unit_lower(jnp.where(strict, kk * decay, 0.0))
        eg = jnp.exp(gi)
        u = _bmm(amat, vb, HI)
        w = _bmm(amat, kb * eg, HI)
        qk = jnp.einsum('ncd,nsd->ncs', q.astype(BF16), k.astype(BF16), preferred_element_type=F32) * decay
        qd = q * eg
        kt = k * jnp.exp(gl - gi)
        cd = jnp.exp(gl)
        for lst, val in zip(outs, (qk, qd, u, w, kt, cd)):
            lst.append(val.reshape(ts, 64))
    return [jnp.concatenate(lst, axis=-1) for lst in outs]


def fn_rglru(tok, nd, ex, par, pt, mm):
    (rx, rgate), (conv_w, conv_b, b_a, b_x, lam, bd_a, bd_x) = tok, pt
    xc = causal_conv4(rx, conv_w) + conv_b
    r = jax.nn.sigmoid(mmw(xc, bd_a) + b_a)
    i = jax.nn.sigmoid(mmw(xc, bd_x) + b_x)
    log_a = -RG_C * r * jax.nn.softplus(-lam)
    a = jnp.exp(log_a)
    bterm = jnp.sqrt(neg_expm1(2.0 * log_a)) * (i * xc)
    return [linscan(a, bterm) * jax.nn.gelu(rgate)]


def _rope32(x, cos, sin):
    x1, x2 = x[:, :16], x[:, 16:32]
    return jnp.concatenate([x1 * cos - x2 * sin, x2 * cos + x1 * sin], axis=-1)


MLA_QK = 96


def fn_mla_pre(tok, nd, ex, par, pt, mm):
    (mq, mkv, misc), (cs,), (g_q, g_kv, w_q, w_kv) = tok, nd, par
    q = mmw(rms(mq, g_q), w_q)
    kv = mmw(rms(mkv, g_kv), w_kv)
    cos, sin = cs[:, 0:16], cs[:, 16:32]
    kp = _rope32(misc[:, MISC_KR:MISC_KR + 32], cos, sin)
    qs, ks, vs = [], [], []
    for h in range(MLA_HEADS):
        qs += [q[:, MLA_QK * h: MLA_QK * h + 64], _rope32(q[:, MLA_QK * h + 64: MLA_QK * h + 96], cos, sin)]
        ks += [kv[:, 128 * h: 128 * h + 64], kp]
        vs.append(kv[:, 128 * h + 64: 128 * h + 128])
    return [jnp.concatenate(qs, axis=-1), jnp.concatenate(ks, axis=-1), jnp.concatenate(vs, axis=-1)]


def fn_out_proj(tok, nd, ex, par, pt, mm):
    (h, o_a, o_b, o_c), (gt,) = tok, ex
    return [h + gt * mm(jnp.concatenate([o_a, o_b, o_c], axis=-1), 0)]


def fn_mlp_in(tok, nd, ex, par, pt, mm):
    (h,), (sc, sh), (g,) = tok, ex, par
    u = rms(h, g) * (1.0 + sc) + sh
    return [jnp.concatenate([mm(u, 0, j) for j in range(N_CHIP)], axis=-1)]


def fn_mlp_out(tok, nd, ex, par, pt, mm):
    (h, a), (gt,) = tok, ex
    return [h + gt * mm(jnp.square(jax.nn.relu(a)), 0)]


GDN_W = GDN_HEADS * 64


def _head_mask():
    r = _rows((GDN_W, GDN_W)) // 64
    c = lax.broadcasted_iota(jnp.int32, (GDN_W, GDN_W), 1) // 64
    return r == c


def _heads_diag(x):
    return jnp.where(_head_mask(), jnp.concatenate([x] * GDN_HEADS, axis=0), 0.0)


def _heads_compact(s):
    return s[0:64] + s[64:128] + s[128:192] + s[192:256]


def _gdn_step(state, qk, qd, u, w, kt, cd, z, norm_g):
    v_new = u - _dot(w, state)
    o = _dot(qd, state) + _dot(qk, _heads_diag(v_new))
    update = _dot(kt, v_new, (((0,), (0,)), ((), ())))
    new_state = state * jnp.broadcast_to(cd[0:1, :], (GDN_W, GDN_W)) + jnp.where(_head_mask(), update, 0.0)
    outs = [rms(o[:, 64 * hd: 64 * hd + 64], norm_g) * jax.nn.silu(z[:, 64 * hd: 64 * hd + 64]) for hd in range(GDN_HEADS)]
    return new_state, jnp.concatenate(outs, axis=-1)


def gdn_scan(xs, z, norm_g):
    bsz, seq, _ = z.shape
    n = seq // CHUNK
    blk = pl.BlockSpec((bsz, CHUNK, 256), lambda i: (0, i, 0))

    def body(qk, qd, u, w, kt, cd, z_ref, g_ref, o_ref, st_out, st):
        @pl.when(pl.program_id(0) == 0)
        def _():
            st[...] = jnp.zeros_like(st)

        for b in range(bsz):
            state = st[b]
            st_out[b] = _heads_compact(state)
            st[b], o_ref[b] = _gdn_step(state, qk[b], qd[b], u[b], w[b], kt[b], cd[b], z_ref[b], g_ref[...])

    return pl.pallas_call(
        body, name="gdn_scan", grid=(n,), in_specs=[blk] * 7 + [pl.BlockSpec((1, 64), lambda i: (0, 0))],
        out_specs=[blk, blk], out_shape=[jax.ShapeDtypeStruct((bsz, seq, 256), F32)] * 2,
        scratch_shapes=[pltpu.VMEM((bsz, GDN_W, GDN_W), F32)], compiler_params=_params(1))(*xs, z, norm_g)


def gdn_scan_bwd(xs, z, norm_g, st_in, do):
    bsz, seq, _ = z.shape
    n = seq // CHUNK
    blk = pl.BlockSpec((bsz, CHUNK, 256), lambda i: (0, n - 1 - i, 0))
    gspec = pl.BlockSpec((1, 64), lambda i: (0, 0))

    def body(qk, qd, u, w, kt, cd, z_ref, g_ref, st_ref, do_ref, dqk, dqd, du, dw, dkt, dcd, dz, dg, dst):
        first = pl.program_id(0) == 0

        @pl.when(first)
        def _():
            dst[...] = jnp.zeros_like(dst)

        dg_sum = None
        for b in range(bsz):
            _, vjp = jax.vjp(_gdn_step, _heads_diag(st_ref[b]), qk[b], qd[b], u[b], w[b], kt[b], cd[b], z_ref[b], g_ref[...])
            grads = vjp((dst[b], do_ref[b]))
            dst[b] = jnp.where(_head_mask(), grads[0], 0.0)
            for r, g in zip((dqk, dqd, du, dw, dkt, dcd, dz), grads[1:8]):
                r[b] = g
            dg_sum = grads[8] if dg_sum is None else dg_sum + grads[8]

        @pl.when(first)
        def _():
            dg[...] = dg_sum

        @pl.when(jnp.logical_not(first))
        def _():
            dg[...] += dg_sum

    res = pl.pallas_call(
        body, name="gdn_scan_bwd", grid=(n,), in_specs=[blk] * 7 + [gspec, blk, blk],
        out_specs=[blk] * 7 + [gspec], out_shape=[jax.ShapeDtypeStruct((bsz, seq, 256), F32)] * 7
        + [jax.ShapeDtypeStruct((1, 64), F32)],
        scratch_shapes=[pltpu.VMEM((bsz, GDN_W, GDN_W), F32)], compiler_params=_params(1))(*xs, z, norm_g, st_in, do)
    return list(res[:6]), res[6], res[7]


ATTN_TQ = 256
ATTN_SCALE = 96.0 ** -0.5


def _attn_head(q, k, v, q0):
    s = _dot(q, k, (((1,), (1,)), ((), ()))) * ATTN_SCALE
    qc = (q0 + _rows(s.shape)) // CHUNK
    kc = lax.broadcasted_iota(jnp.int32, s.shape, 1) // CHUNK
    s = jnp.where(kc <= qc, s, -1e30)
    p = jnp.exp(s - jnp.max(s, axis=-1, keepdims=True))
    p = p / jnp.sum(p, axis=-1, keepdims=True)
    return _dot(p, v)


def _key_lengths(seq):
    n_var = min(2, seq // ATTN_TQ)
    return [(j + 1) * (seq // n_var) for j in range(n_var)]


def _key_variant(i, seq):
    return ((i + 1) * ATTN_TQ - 1) // _key_lengths(seq)[0]


ATTN_QW, ATTN_VW = MLA_HEADS * MLA_QK, MLA_HEADS * 64


def _attn_specs(seq):
    def qspec(ch):
        return pl.BlockSpec((None, ATTN_TQ, ch), lambda b, i: (b, i, 0))

    def kspec(ch):
        return pl.BlockSpec((None, seq, ch), lambda b, i: (b, 0, 0))

    return qspec, kspec


def mla_attention(q, k, v):
    bsz, seq, _ = q.shape
    qspec, kspec = _attn_specs(seq)

    def body(q_r, k_r, v_r, o_r):
        i = pl.program_id(1)
        q0 = i * ATTN_TQ

        def with_keys(klen):
            outs = [_attn_head(q_r[:, MLA_QK * h: MLA_QK * h + MLA_QK], k_r[0:klen, MLA_QK * h: MLA_QK * h + MLA_QK],
                               v_r[0:klen, 64 * h: 64 * h + 64], q0) for h in range(MLA_HEADS)]
            o_r[...] = jnp.concatenate(outs, axis=-1)

        for j, klen in enumerate(_key_lengths(seq)):
            pl.when(_key_variant(i, seq) == j)(functools.partial(with_keys, klen))

    return pl.pallas_call(
        body, name="mla_attention", grid=(bsz, seq // ATTN_TQ), in_specs=[qspec(ATTN_QW), kspec(ATTN_QW), kspec(ATTN_VW)],
        out_specs=qspec(ATTN_VW), out_shape=jax.ShapeDtypeStruct((bsz, seq, ATTN_VW), F32), compiler_params=_params(2))(
            q, k, v)


def mla_attention_bwd(q, k, v, do):
    bsz, seq, _ = q.shape
    qspec, kspec = _attn_specs(seq)

    def body(q_r, k_r, v_r, do_r, dq_r, dk_r, dv_r):
        i = pl.program_id(1)
        q0 = i * ATTN_TQ

        @pl.when(i == 0)
        def _():
            dk_r[...] = jnp.zeros_like(dk_r)
            dv_r[...] = jnp.zeros_like(dv_r)

        def with_keys(klen):
            dq, dk, dv = [], [], []
            for h in range(MLA_HEADS):
                qk = slice(MLA_QK * h, MLA_QK * h + MLA_QK)
                sl = slice(64 * h, 64 * h + 64)
                _, vjp = jax.vjp(functools.partial(_attn_head, q0=q0), q_r[:, qk], k_r[0:klen, qk], v_r[0:klen, sl])
                a, b, c = vjp(do_r[:, sl])
                dq.append(a)
                dk.append(b)
                dv.append(c)
            dq_r[...] = jnp.concatenate(dq, axis=-1)
            dk_r[0:klen, :] += jnp.concatenate(dk, axis=-1)
            dv_r[0:klen, :] += jnp.concatenate(dv, axis=-1)

        for j, klen in enumerate(_key_lengths(seq)):
            pl.when(_key_variant(i, seq) == j)(functools.partial(with_keys, klen))

    shp = lambda ch: jax.ShapeDtypeStruct((bsz, seq, ch), F32)
    return pl.pallas_call(
        body, name="mla_attention_bwd", grid=(bsz, seq // ATTN_TQ),
        in_specs=[qspec(ATTN_QW), kspec(ATTN_QW), kspec(ATTN_VW), qspec(ATTN_VW)],
        out_specs=[qspec(ATTN_QW), kspec(ATTN_QW), kspec(ATTN_VW)],
        out_shape=[shp(ATTN_QW), shp(ATTN_QW), shp(ATTN_VW)], compiler_params=_params(2))(q, k, v, do)


LOSS_TS = 512


def loss_head(h, g, target):
    bsz, seq, d = h.shape
    ts = min(LOSS_TS, seq)
    tok = pl.BlockSpec((None, ts, d), lambda b, s: (b, s, 0))
    gspec = pl.BlockSpec((1, d), lambda b, s: (0, 0))
    lspec = pl.BlockSpec((1, 128), lambda b, s: (0, 0))

    def body(h_r, g_r, t_r, loss_r, dh_r, dg_r):
        first = jnp.logical_and(pl.program_id(0) == 0, pl.program_id(1) == 0)
        tv = t_r[...]

        def f(hv, gv):
            return 0.5 * jnp.sum(jnp.mean(jnp.square(rms(hv, gv) - tv), axis=-1, keepdims=True), axis=0, keepdims=True)

        val, vjp = jax.vjp(f, h_r[...], g_r[...])
        dh, dg = vjp(jnp.ones((1, 1), F32))
        dh_r[...] = dh
        lv = jnp.broadcast_to(val, (1, 128))

        @pl.when(first)
        def _():
            loss_r[...] = lv
            dg_r[...] = dg

        @pl.when(jnp.logical_not(first))
        def _():
            loss_r[...] += lv
            dg_r[...] += dg

    return pl.pallas_call(
        body, name="loss_head", grid=(bsz, seq // ts), in_specs=[tok, gspec, tok], out_specs=[lspec, tok, gspec],
        out_shape=[jax.ShapeDtypeStruct((1, 128), F32), jax.ShapeDtypeStruct(h.shape, F32), jax.ShapeDtypeStruct((1, d), F32)],
        compiler_params=_params(2))(h, g, target)


def _adamw_math(w, g, m, v):
    m = ADAM_B1 * m + (1.0 - ADAM_B1) * g
    v = ADAM_B2 * v + (1.0 - ADAM_B2) * jnp.square(g)
    m_hat = m / (1.0 - ADAM_B1 ** ADAM_STEP)
    v_hat = v / (1.0 - ADAM_B2 ** ADAM_STEP)
    return -ADAM_LR * (m_hat / (jnp.sqrt(v_hat) + ADAM_EPS) + ADAM_WD * w), m, v


def _row_block(rows, cols):
    want = max(8, (1 << 18) // cols)
    best = rows
    for r in range(8, rows + 1, 8):
        if rows % r == 0 and r <= want:
            best = r
    return best if rows % 8 == 0 else rows


def adamw(name, w, g, m, v):
    rows, cols = w.shape
    rb = _row_block(rows, cols)
    spec = pl.BlockSpec((rb, cols), lambda i: (i, 0))

    def body(w_r, g_r, m_r, v_r, d_o, m_o, v_o):
        d, mn, vn = _adamw_math(w_r[...], g_r[...], m_r[...], v_r[...])
        d_o[...] = d
        m_o[...] = mn
        v_o[...] = vn

    return pl.pallas_call(body, name=name, grid=(rows // rb,), in_specs=[spec] * 4, out_specs=[spec] * 3,
                          out_shape=[jax.ShapeDtypeStruct(w.shape, F32)] * 3, compiler_params=_params(1))(w, g, m, v)


def adamw_reduce(name, parts, w, m, v):
    rows, cols = w.shape
    rb = _row_block(rows, cols)
    spec = pl.BlockSpec((rb, cols), lambda i: (i, 0))
    pspec = pl.BlockSpec((N_DEV, rb, cols), lambda i: (0, i, 0))

    def body(p_r, w_r, m_r, v_r, g_o, d_o, m_o, v_o):
        g = p_r[0]
        for k in range(1, N_DEV):
            g = g + p_r[k]
        d, mn, vn = _adamw_math(w_r[...], g, m_r[...], v_r[...])
        g_o[...] = g
        d_o[...] = d
        m_o[...] = mn
        v_o[...] = vn

    return pl.pallas_call(body, name=name, grid=(rows // rb,), in_specs=[pspec, spec, spec, spec], out_specs=[spec] * 4,
                          out_shape=[jax.ShapeDtypeStruct(w.shape, F32)] * 4, compiler_params=_params(1))(parts, w, m, v)


MOD_CB = 512


def mod_matmul(c_rows, w_mod, b_mod):
    nl, d, cols = w_mod.shape

    def body(c_r, w_r, b_r, o_r):
        o_r[...] = _dot(jax.nn.silu(c_r[...]), w_r[...]) + b_r[...]

    return pl.pallas_call(
        body, name="mod_matmul", grid=(nl, cols // MOD_CB),
        in_specs=[pl.BlockSpec((8, d), lambda l, j: (0, 0)), pl.BlockSpec((None, d, MOD_CB), lambda l, j: (l, 0, j)),
                  pl.BlockSpec((None, 1, MOD_CB), lambda l, j: (l, 0, j))],
        out_specs=pl.BlockSpec((None, 8, MOD_CB), lambda l, j: (l, 0, j)),
        out_shape=jax.ShapeDtypeStruct((nl, 8, cols), F32), compiler_params=_params(2))(c_rows, w_mod, b_mod)


def mod_weight_grad(c_all, dmod):
    nl, nb, cols = dmod.shape
    d = c_all.shape[1]

    def body(c_r, g_r, o_r):
        o_r[...] = _dot(jax.nn.silu(c_r[...]), g_r[...], (((0,), (0,)), ((), ())))

    return pl.pallas_call(
        body, name="mod_weight_grad", grid=(nl, cols // MOD_CB),
        in_specs=[pl.BlockSpec((nb, d), lambda l, j: (0, 0)), pl.BlockSpec((None, nb, MOD_CB), lambda l, j: (l, 0, j))],
        out_specs=pl.BlockSpec((None, d, MOD_CB), lambda l, j: (l, 0, j)),
        out_shape=jax.ShapeDtypeStruct((nl, d, cols), F32), compiler_params=_params(2))(c_all, dmod)


def _half_block(hr, cols):
    rb = _row_block(hr, cols)
    return rb if rb % 16 == 0 else hr


def add_half(name, g, s, core):
    _, r, cols = g.shape
    hr = r // 2
    rb = _half_block(hr, cols)
    nblk = hr // rb
    gspec = pl.BlockSpec((None, rb, cols), lambda k, i, c: (k, c[0] * nblk + i, 0))
    spec = pl.BlockSpec((None, rb, cols), lambda k, i, c: (k, i, 0))

    def body(c_r, g_r, s_r, o_r, ob_r):
        t = g_r[...] + s_r[...]
        o_r[...] = t
        ob_r[...] = t.astype(BF16)

    return pl.pallas_call(
        body, name=name, grid_spec=pltpu.PrefetchScalarGridSpec(num_scalar_prefetch=1, grid=(N_CHIP, nblk),
                                                                in_specs=[gspec, spec], out_specs=[spec, spec]),
        out_shape=[jax.ShapeDtypeStruct((N_CHIP, hr, cols), F32), jax.ShapeDtypeStruct((N_CHIP, hr, cols), BF16)],
        compiler_params=_params(2))(core, g, s)


def sum_peers(name, p32, recv, ids, shard_shape, layer, acc=None):
    _, hr, cols = p32.shape
    rb = _half_block(hr, cols)
    nblk = hr // rb

    def slot(k):
        return pl.BlockSpec((None, rb, cols), lambda i, c: ((c[0] + k) % N_CHIP, i, 0))

    def body(c_r, o_r, r1, r2, r3, *rest):
        rest[-1][...] = ((o_r[...] + r1[...].astype(F32)) + r2[...].astype(F32)) + r3[...].astype(F32)

    args = (ids, p32, recv, recv, recv) + (() if acc is None else (acc,))
    return pl.pallas_call(
        body, name=name, grid_spec=pltpu.PrefetchScalarGridSpec(
            num_scalar_prefetch=1, grid=(nblk,),
            in_specs=[slot(0), slot(1), slot(2), slot(3)] + ([] if acc is None else [_ANY]),
            out_specs=pl.BlockSpec((None, rb, cols), lambda i, c: (layer, c[1] * nblk + i, 0))),
        out_shape=jax.ShapeDtypeStruct(shard_shape, F32), input_output_aliases={} if acc is None else {5: 0},
        compiler_params=_params(1))(*args)


def cast_into_slab(name, w, ids, layer=None):
    nl, r, cols = w.shape
    hr = r // 2
    rb = _half_block(hr, cols)
    nblk = hr // rb

    def body(c_r, w_r, o_r):
        o_r[...] = w_r[...].astype(BF16)

    if layer is not None:
        return pl.pallas_call(
            body, name=name, grid_spec=pltpu.PrefetchScalarGridSpec(
                num_scalar_prefetch=1, grid=(nblk,),
                in_specs=[pl.BlockSpec((None, rb, cols), lambda i, c: (layer, c[1] * nblk + i, 0))],
                out_specs=pl.BlockSpec((None, rb, cols), lambda i, c: (c[0], c[1] * nblk + i, 0))),
            out_shape=jax.ShapeDtypeStruct((N_CHIP, r, cols), BF16), compiler_params=_params(1))(ids, w)
    return pl.pallas_call(
        body, name=name, grid_spec=pltpu.PrefetchScalarGridSpec(
            num_scalar_prefetch=1, grid=(nl, nblk),
            in_specs=[pl.BlockSpec((None, rb, cols), lambda l, i, c: (l, c[1] * nblk + i, 0))],
            out_specs=pl.BlockSpec((None, None, rb, cols), lambda l, i, c: (l, c[0], c[1] * nblk + i, 0))),
        out_shape=jax.ShapeDtypeStruct((nl, N_CHIP, r, cols), BF16), compiler_params=_params(2))(ids, w)


def _me():
    return lax.axis_index("x"), lax.axis_index("y"), lax.axis_index("c")


def all_gather8(name, x_shard, in_vmem):
    m_per, n = x_shard.shape
    space = pltpu.VMEM if in_vmem else pl.ANY

    def body(x_ref, out_ref, send_sems, recv_sems, local_sem):
        x, y, c = _me()
        me, sibling = (x, y, c), (x, y, 1 - c)
        chips = [(1 - x, y), (x, 1 - y), (1 - x, 1 - y)]

        def rows(px, py, pc):
            return out_ref.at[pl.ds((4 * px + 2 * py + pc) * m_per, m_per), :]

        def copy(k, block, to, src=None):
            return pltpu.make_async_remote_copy(
                src_ref=rows(*block) if src is None else src, dst_ref=rows(*block), send_sem=send_sems.at[k],
                recv_sem=recv_sems.at[k], device_id=to, device_id_type=MESH)

        mine = pltpu.make_async_copy(x_ref, rows(*me), local_sem)
        mine.start()
        first = [copy(0, me, sibling, src=x_ref)]
        first += [copy(1 + j, me, (*chip, c), src=x_ref) for j, chip in enumerate(chips)]
        for cp in first:
            cp.start()
        passed = [copy(4 + j, (*chip, c), sibling) for j, chip in enumerate(chips)]
        for j, chip in enumerate(chips):
            copy(1 + j, (*chip, c), me).wait_recv()
            passed[j].start()
        copy(0, sibling, me).wait_recv()
        for j, chip in enumerate(chips):
            copy(4 + j, (*chip, 1 - c), me).wait_recv()
        for cp in first + passed:
            cp.wait_send()
        mine.wait()

    return pl.pallas_call(
        body, name=name, out_shape=jax.ShapeDtypeStruct((N_DEV * m_per, n), x_shard.dtype),
        in_specs=[pl.BlockSpec(memory_space=space)], out_specs=pl.BlockSpec(memory_space=space),
        scratch_shapes=[pltpu.SemaphoreType.DMA((7,)), pltpu.SemaphoreType.DMA((7,)), pltpu.SemaphoreType.DMA],
    )(x_shard)


_ANY = pl.BlockSpec(memory_space=pl.ANY)


def all_gather_weights(name, slabs):
    n = len(slabs)

    def body(*refs):
        outs = refs[n:2 * n]
        send_sems, recv_sems = refs[2 * n:]
        x, y, c = _me()
        me, sibling = (x, y, c), (x, y, 1 - c)
        chips = [(1 - x, y), (x, 1 - y), (1 - x, 1 - y)]

        def view(i, px, py, pc):
            hr = slabs[i].shape[2] // 2
            return outs[i].at[:, 2 * px + py, pl.ds(pc * hr, hr), :]

        def copy(i, k, block, to):
            return pltpu.make_async_remote_copy(
                src_ref=view(i, *block), dst_ref=view(i, *block), send_sem=send_sems.at[i, k],
                recv_sem=recv_sems.at[i, k], device_id=to, device_id_type=MESH)

        first = []
        for i in range(n):
            first.append(copy(i, 0, me, sibling))
            first += [copy(i, 1 + j, me, (*chip, c)) for j, chip in enumerate(chips)]
        for cp in first:
            cp.start()
        passed = []
        for j, chip in enumerate(chips):
            for i in range(n):
                copy(i, 1 + j, (*chip, c), me).wait_recv()
                passed.append(copy(i, 4 + j, (*chip, c), sibling))
                passed[-1].start()
        for i in range(n):
            copy(i, 0, sibling, me).wait_recv()
            for j, chip in enumerate(chips):
                copy(i, 4 + j, (*chip, 1 - c), me).wait_recv()
        for cp in first + passed:
            cp.wait_send()

    return pl.pallas_call(
        body, name=name, out_shape=[jax.ShapeDtypeStruct(s.shape, s.dtype) for s in slabs],
        in_specs=[_ANY] * n, out_specs=[_ANY] * n, input_output_aliases={i: i for i in range(n)},
        scratch_shapes=[pltpu.SemaphoreType.DMA((n, 7)), pltpu.SemaphoreType.DMA((n, 7))],
    )(*slabs)


def _slab_block(slab, px, py, pc):
    hr = slab.shape[1] // 2
    return slab.at[2 * px + py, pl.ds(pc * hr, hr), :]


def gather_over_ici(slab):
    def issue(refs, send_sems, recv_sems):
        (buf,) = refs
        x, y, c = _me()
        peers = [(x, y, 1 - c), (1 - x, y, c), (x, 1 - y, c), (1 - x, 1 - y, c)]

        def copy(k, block, to):
            return pltpu.make_async_remote_copy(
                src_ref=_slab_block(buf, *block), dst_ref=_slab_block(buf, *block), send_sem=send_sems.at[k],
                recv_sem=recv_sems.at[k], device_id=to, device_id_type=MESH)

        sends = [copy(k, (x, y, c), p) for k, p in enumerate(peers)]
        arrivals = [copy(k, p, (x, y, c)) for k, p in enumerate(peers)]
        return sends, arrivals, sends

    return ([slab], 4, issue)


def gather_over_d2d(slab):
    def issue(refs, send_sems, recv_sems):
        (buf,) = refs
        x, y, c = _me()
        chips = [(1 - x, y), (x, 1 - y), (1 - x, 1 - y)]

        def copy(k, block):
            return pltpu.make_async_remote_copy(
                src_ref=_slab_block(buf, *block), dst_ref=_slab_block(buf, *block), send_sem=send_sems.at[k],
                recv_sem=recv_sems.at[k], device_id=(x, y, 1 - c), device_id_type=MESH)

        sends = [copy(k, (*chip, c)) for k, chip in enumerate(chips)]
        arrivals = [copy(k, (*chip, 1 - c)) for k, chip in enumerate(chips)]
        return sends, arrivals, sends

    return ([slab], 3, issue)


def grad_sibling_exchange(name, gs):
    n = len(gs)

    def body(*refs):
        ins, outs = refs[:n], refs[n:2 * n]
        send_sems, recv_sems = refs[2 * n:]
        mx, my, mc = _me()
        cps = []
        for i in range(n):
            hr = gs[i].shape[1] // 2
            cps.append(pltpu.make_async_remote_copy(
                src_ref=ins[i].at[:, pl.ds((1 - mc) * hr, hr), :], dst_ref=outs[i], send_sem=send_sems.at[i],
                recv_sem=recv_sems.at[i], device_id=(mx, my, 1 - mc), device_id_type=MESH))
            cps[-1].start()
        for cp in cps:
            cp.wait()

    return pl.pallas_call(
        body, name=name, out_shape=[jax.ShapeDtypeStruct((N_CHIP, g.shape[1] // 2, g.shape[2]), g.dtype) for g in gs],
        in_specs=[_ANY] * n, out_specs=[_ANY] * n,
        scratch_shapes=[pltpu.SemaphoreType.DMA((n,)), pltpu.SemaphoreType.DMA((n,))],
    )(*gs)


def grad_chip_exchange(name, ps):
    n = len(ps)

    def body(*refs):
        ins, outs = refs[:n], refs[n:2 * n]
        send_sems, recv_sems = refs[2 * n:]
        mx, my, mc = _me()
        ci = 2 * mx + my
        chips = [(1 - mx, my), (mx, 1 - my), (1 - mx, 1 - my)]
        sends = []
        for i in range(n):
            for k, (px, py) in enumerate(chips):
                sends.append(pltpu.make_async_remote_copy(
                    src_ref=ins[i].at[2 * px + py], dst_ref=outs[i].at[ci], send_sem=send_sems.at[i, k],
                    recv_sem=recv_sems.at[i, k], device_id=(px, py, mc), device_id_type=MESH))
                sends[-1].start()
        for i in range(n):
            for k, (px, py) in enumerate(chips):
                pltpu.make_async_remote_copy(
                    src_ref=ins[i].at[ci], dst_ref=outs[i].at[2 * px + py], send_sem=send_sems.at[i, k],
                    recv_sem=recv_sems.at[i, k], device_id=(px, py, mc), device_id_type=MESH).wait_recv()
        for cp in sends:
            cp.wait_send()

    return pl.pallas_call(
        body, name=name, out_shape=[jax.ShapeDtypeStruct(p.shape, p.dtype) for p in ps], in_specs=[_ANY] * n,
        out_specs=[_ANY] * n, scratch_shapes=[pltpu.SemaphoreType.DMA((n, 3)), pltpu.SemaphoreType.DMA((n, 3))],
    )(*ps)


def grad_half_exchange(name, shards):
    n = len(shards)

    def body(*refs):
        outs = refs[n:2 * n]
        send_sems, recv_sems = refs[2 * n:]
        mx, my, mc = _me()

        def copy(i, core):
            hr = shards[i].shape[1] // 2
            rows = outs[i].at[:, pl.ds(core * hr, hr), :]
            return pltpu.make_async_remote_copy(src_ref=rows, dst_ref=rows, send_sem=send_sems.at[i],
                                                recv_sem=recv_sems.at[i], device_id=(mx, my, 1 - mc), device_id_type=MESH)

        sends = [copy(i, mc) for i in range(n)]
        for cp in sends:
            cp.start()
        for i in range(n):
            copy(i, 1 - mc).wait_recv()
        for cp in sends:
            cp.wait_send()

    return pl.pallas_call(
        body, name=name, out_shape=[jax.ShapeDtypeStruct(s.shape, s.dtype) for s in shards], in_specs=[_ANY] * n,
        out_specs=[_ANY] * n, input_output_aliases={i: i for i in range(n)},
        scratch_shapes=[pltpu.SemaphoreType.DMA((n,)), pltpu.SemaphoreType.DMA((n,))],
    )(*shards)


WEIGHTS = ['w_mod', 'b_mod', 'norm_mix_g', 'w_in', 'gdn_conv_w', 'gdn_a_log', 'gdn_dt_bias', 'gdn_norm_g', 'rg_conv_w',
           'rg_conv_b', 'rg_w_a', 'rg_b_a', 'rg_w_x', 'rg_b_x', 'rg_lambda', 'mla_q_norm_g', 'mla_w_qb', 'mla_kv_norm_g',
           'mla_w_kvb', 'w_out', 'norm_mlp_g', 'w_mlp_in', 'w_mlp_out', 'final_norm_g']
SHARDED = {'w_in': 2, 'gdn_conv_w': 2, 'rg_conv_w': 2, 'mla_w_qb': 2, 'mla_w_kvb': 2, 'w_out': 1, 'w_mlp_in': 2, 'w_mlp_out': 1}
GATHER_BF16 = ('w_in', 'mla_w_qb', 'mla_w_kvb', 'w_out', 'w_mlp_in', 'w_mlp_out')
GATHER_FIRST = GATHER_BF16[:4]
REPLICATED = [n for n in WEIGHTS if n not in SHARDED and n != 'w_mod']
PACK_COLS = 1024


def _pack(arrays, multiple):
    flat = jnp.concatenate([a.reshape(-1) for a in arrays])
    pad = (-flat.shape[0]) % multiple
    return jnp.pad(flat, (0, pad)) if pad else flat


def _unpack(flat, shapes):
    out, o = [], 0
    for shp in shapes:
        n = int(np.prod(shp))
        out.append(flat[o:o + n].reshape(shp))
        o += n
    return out


def _pack_rows(arrays):
    rows = []
    for a in arrays:
        flat = a.reshape(-1)
        pad = (-flat.shape[0]) % PACK_COLS
        rows.append((jnp.pad(flat, (0, pad)) if pad else flat).reshape(-1, PACK_COLS))
    out = jnp.concatenate(rows, axis=0)
    pad = (-out.shape[0]) % 8
    return jnp.pad(out, ((0, pad), (0, 0))) if pad else out


def _unpack_rows(packed, shapes):
    out, r = [], 0
    for shp in shapes:
        n = int(np.prod(shp))
        nr = -(-n // PACK_COLS)
        piece = packed[r:r + nr]
        out.append((piece if n == nr * PACK_COLS else piece.reshape(-1)[:n]).reshape(shp))
        r += nr
    return out


def _unshard(stacked, axis):
    moved = jnp.moveaxis(stacked, 0, axis)
    shp = list(moved.shape)
    shp[axis:axis + 2] = [shp[axis] * shp[axis + 1]]
    return moved.reshape(shp)


def _shard(full, axis):
    shp = list(full.shape)
    shp[axis:axis + 1] = [N_CHIP, shp[axis] // N_CHIP]
    return jnp.moveaxis(full.reshape(shp), axis, 0)


def _proj_cols(w):
    pad = jnp.zeros(w.shape[:-1] + (PROJ_WIDTH - w.shape[-1],), w.dtype)
    return jnp.concatenate([w[..., 0:1024], w[..., 1032:2472], w[..., 1024:1032], pad], axis=-1)


def _proj_cols_back(d):
    return jnp.concatenate([d[..., 0:1024], d[..., 2464:2472], d[..., 1024:2464]], axis=-1)


def _heads_split(w, heads, first):
    per = w.shape[-1] // heads
    r = w.reshape(w.shape[:-1] + (heads, per))
    lead = w.shape[:-1]
    return jnp.concatenate([r[..., :first].reshape(lead + (heads * first,)),
                            r[..., first:].reshape(lead + (heads * (per - first),))], axis=-1)


def _heads_merge(d, heads, first):
    lead = d.shape[:-1]
    per = d.shape[-1] // heads
    a = d[..., :heads * first].reshape(lead + (heads, first))
    b = d[..., heads * first:].reshape(lead + (heads, per - first))
    return jnp.concatenate([a, b], axis=-1).reshape(lead + (heads * per,))


def _block_diag(w):
    nl = w.shape[0]
    eye = jnp.eye(2, dtype=w.dtype)
    return jnp.einsum('lcoij,op->lcoipj', w.reshape(nl, 4, 2, 64, 64), eye).reshape(nl, 4, 128, 128)


def _block_diag_back(g):
    nl = g.shape[0]
    return jnp.einsum('lcoipj,op->lcoij', g.reshape(nl, 4, 2, 64, 2, 64), jnp.eye(2, dtype=g.dtype)).reshape(nl, 8, 64, 64)


def kernel(x, c, positions, w_mod, b_mod, norm_mix_g, w_in, gdn_conv_w, gdn_a_log, gdn_dt_bias, gdn_norm_g, rg_conv_w, rg_conv_b, rg_w_a, rg_b_a, rg_w_x, rg_b_x, rg_lambda, mla_q_norm_g, mla_w_qb, mla_kv_norm_g, mla_w_kvb, w_out, norm_mlp_g, w_mlp_in, w_mlp_out, final_norm_g, loss_target, m_w_mod, m_b_mod, m_norm_mix_g, m_w_in, m_gdn_conv_w, m_gdn_a_log, m_gdn_dt_bias, m_gdn_norm_g, m_rg_conv_w, m_rg_conv_b, m_rg_w_a, m_rg_b_a, m_rg_w_x, m_rg_b_x, m_rg_lambda, m_mla_q_norm_g, m_mla_w_qb, m_mla_kv_norm_g, m_mla_w_kvb, m_w_out, m_norm_mlp_g, m_w_mlp_in, m_w_mlp_out, m_final_norm_g, v_w_mod, v_b_mod, v_norm_mix_g, v_w_in, v_gdn_conv_w, v_gdn_a_log, v_gdn_dt_bias, v_gdn_norm_g, v_rg_conv_w, v_rg_conv_b, v_rg_w_a, v_rg_b_a, v_rg_w_x, v_rg_b_x, v_rg_lambda, v_mla_q_norm_g, v_mla_w_qb, v_mla_kv_norm_g, v_mla_w_kvb, v_w_out, v_norm_mlp_g, v_w_mlp_in, v_w_mlp_out, v_final_norm_g):
    given = dict(locals())
    wts = {n: given[n] for n in WEIGHTS}
    mom_m = {n: given["m_" + n] for n in WEIGHTS}
    mom_v = {n: given["v_" + n] for n in WEIGHTS}
    bsz, seq, d = x.shape
    depth = w_mod.shape[0]
    mx, my, mc = lax.axis_index("x"), lax.axis_index("y"), lax.axis_index("c")
    chip = 2 * mx + my
    dev = 2 * chip + mc

    conv_shapes = [wts['gdn_conv_w'].shape, wts['rg_conv_w'].shape]
    conv_flat = _pack([wts['gdn_conv_w'], wts['rg_conv_w']], d)
    conv_rows = conv_flat.shape[0] // d
    assert bsz + conv_rows <= 8
    c_pad = jnp.concatenate([c, conv_flat.reshape(conv_rows, d), jnp.zeros((8 - bsz - conv_rows, d), F32)], axis=0)
    gath = all_gather8("gather_c", c_pad, True).reshape(N_DEV, 8, d)
    c_all = gath[:, :bsz].reshape(N_DEV * bsz, d)
    conv_all = gath[0::2, bsz:bsz + conv_rows].reshape(N_CHIP, conv_rows * d)
    gdn_conv_full, rg_conv_full = [
        _unshard(jnp.stack([_unpack(conv_all[s], conv_shapes)[i] for s in range(N_CHIP)]), 2) for i in range(2)]

    n_half = N_DEV * bsz // 2
    mod_cols = w_mod.shape[2]
    c_rows = lax.dynamic_slice(c_all, (n_half * mc, 0), (n_half, d))
    b_mod_mine = lax.dynamic_slice(b_mod, (0, chip * mod_cols), (depth, mod_cols)).reshape(depth, 1, mod_cols)
    mod_piece = mod_matmul(c_rows, w_mod, b_mod_mine)
    mod_g = all_gather8("gather_mod", mod_piece.reshape(depth * n_half, mod_cols), True)
    mod_all = mod_g.reshape(N_CHIP, 2, depth, n_half, mod_cols).transpose(2, 1, 3, 0, 4).reshape(depth, 2 * n_half, 6 * d)
    mod_mine = lax.dynamic_slice(mod_all, (0, bsz * dev, 0), (depth, bsz, 6 * d)).reshape(depth, bsz, 6, 1, d)

    ids = jnp.stack([chip, mc]).astype(jnp.int32)
    slabs = dict(zip(GATHER_FIRST, all_gather_weights(
        "gather_weights", [cast_into_slab("cast_" + n, wts[n], ids) for n in GATHER_FIRST])))

    def columns(g):
        return g.transpose(0, 2, 1, 3).reshape(g.shape[0], g.shape[2], N_CHIP * g.shape[3])

    def rows_of(g):
        return g.reshape(g.shape[0], N_CHIP * g.shape[2], g.shape[3])

    w_cat = _proj_cols(columns(slabs['w_in']))
    w_q = columns(slabs['mla_w_qb']).astype(F32)
    w_kv = columns(slabs['mla_w_kvb']).astype(F32)
    w_out_full = rows_of(slabs['w_out'])
    bd_a, bd_x = _block_diag(rg_w_a), _block_diag(rg_w_x)

    inv_freq = ROPE_THETA ** (-jnp.arange(0, 32, 2, dtype=F32) / 32.0)
    ang = positions.astype(F32)[..., None] * inv_freq
    cs = jnp.concatenate([jnp.cos(ang), jnp.sin(ang)], axis=-1)

    proj_ch = [w for _, w in PROJ_PIECES]

    def row(a, l):
        return a[l].reshape(1, -1)

    def layer_args(l):
        sh_m, sc_m, gt_m, sh_f, sc_f, gt_f = (mod_mine[l, :, k] for k in range(6))
        return dict(
            mods=(sh_m, sc_m, gt_m, sh_f, sc_f, gt_f),
            mixer_in=dict(ex=[sc_m, sh_m], par=[row(norm_mix_g, l)], big=[(w_cat, l)], out_ch=proj_ch, ts=512),
            gdn_conv=dict(par_tiled=[gdn_conv_full[l]], out_ch=[768], ts=seq, nc=3),
            gdn_local=dict(par=[row(gdn_a_log, l), row(gdn_dt_bias, l)], out_ch=[256] * 6, ts=512),
            rglru=dict(par_tiled=[rg_conv_full[l], row(rg_conv_b, l), row(rg_b_a, l), row(rg_b_x, l), row(rg_lambda, l),
                                  bd_a[l], bd_x[l]], out_ch=[512], ts=seq, nc=4),
            mla_pre=dict(tok_nd=[cs], par=[row(mla_q_norm_g, l), row(mla_kv_norm_g, l), w_q[l], w_kv[l]],
                         out_ch=[ATTN_QW, ATTN_QW, ATTN_VW], ts=512),
            out_proj=dict(ex=[gt_m], big=[(w_out_full, l)], out_ch=[d], ts=512),
            mlp_in=dict(ex=[sc_f, sh_f], par=[row(norm_mlp_g, l)], big=[w_mi.get(l)], out_ch=[4 * d], ts=256),
            mlp_out=dict(ex=[gt_f], big=[w_mo.get(l)], out_ch=[d], ts=256),
        )

    mi_buf = [cast_into_slab("cast_w_mlp_in%d" % l, wts['w_mlp_in'], ids, layer=l) for l in range(depth)]
    mo_buf = [cast_into_slab("cast_w_mlp_out%d" % l, wts['w_mlp_out'], ids, layer=l) for l in range(depth)]
    w_mi, w_mo = {}, {}

    def staged(name, fn, jobs, **kw):
        return run_stage(name, fn, side=jobs, **kw) if jobs else (run_stage(name, fn, **kw), [])

    saved = []
    h = x
    for l in range(depth):
        a = layer_args(l)
        sfx = str(l)
        first, more = l == 0, l + 1 < depth
        (qkv_raw, z, rx, rgate, mq, mkv, misc), bufs = staged(
            "mixer_in" + sfx, fn_mixer_in, [] if first else [gather_over_d2d(mo_buf[l])], tok=[h], **a['mixer_in'])
        if not first:
            w_mo[l] = bufs[0].reshape(N_CHIP * d, d)
        (qkv_act,) = run_stage("gdn_conv" + sfx, fn_gdn_conv, tok=[qkv_raw], **a['gdn_conv'])
        xs, bufs = staged("gdn_local" + sfx, fn_gdn_local, [gather_over_ici(mi_buf[l])] if first else [],
                          tok=[qkv_act, misc], **a['gdn_local'])
        if first:
            mi_buf[l] = bufs[0]
        o_a, st_in = gdn_scan(xs, z, row(gdn_norm_g, l))
        (o_b,), bufs = staged("rglru" + sfx, fn_rglru, [gather_over_d2d(mi_buf[l]), gather_over_ici(mo_buf[l])] if first else [],
                              tok=[rx, rgate], **a['rglru'])
        if first:
            w_mi[l], mo_buf[l] = bufs
        q_at, k_at, v_at = run_stage("mla_pre" + sfx, fn_mla_pre, tok=[mq, mkv, misc], **a['mla_pre'])
        o_c = mla_attention(q_at, k_at, v_at)
        (h_mid,), bufs = staged("out_proj" + sfx, fn_out_proj, [gather_over_d2d(mo_buf[l])] if first else [],
                                tok=[h, o_a, o_b, o_c], **a['out_proj'])
        if first:
            w_mo[l] = bufs[0].reshape(N_CHIP * d, d)
        a = layer_args(l)
        (a_mlp,), bufs = staged("mlp_in" + sfx, fn_mlp_in, [gather_over_ici(mi_buf[l + 1])] if more else [],
                                tok=[h_mid], **a['mlp_in'])
        if more:
            mi_buf[l + 1] = bufs[0]
        (h_out,), bufs = staged("mlp_out" + sfx, fn_mlp_out,
                                [gather_over_d2d(mi_buf[l + 1]), gather_over_ici(mo_buf[l + 1])] if more else [],
                                tok=[h_mid, a_mlp], **a['mlp_out'])
        if more:
            w_mi[l + 1], mo_buf[l + 1] = bufs
        saved.append(dict(h=h, qkv_raw=qkv_raw, z=z, rx=rx, rgate=rgate, mq=mq, mkv=mkv, misc=misc, qkv_act=qkv_act, xs=xs,
                          st_in=st_in, o_a=o_a, o_b=o_b, o_c=o_c, q_at=q_at, k_at=k_at, v_at=v_at, h_mid=h_mid, a_mlp=a_mlp))
        h = h_out

    loss_part, dh, d_final_g = loss_head(h, final_norm_g.reshape(1, d), loss_target)
    loss = lax.psum(loss_part[0, 0], ("x", "y", "c"))

    g_full = {n: [None] * depth for n in SHARDED}
    g_rep = {n: [None] * depth for n in REPLICATED if n not in ('final_norm_g', 'b_mod')}

    def column_slabs(g):
        return g.reshape(g.shape[0], N_CHIP, g.shape[1] // N_CHIP).transpose(1, 0, 2)

    def row_slabs(g):
        return g.reshape(N_CHIP, g.shape[0] // N_CHIP, g.shape[1])
    dmod = [None] * depth
    for l in reversed(range(depth)):
        a, sv = layer_args(l), saved[l]
        sfx = str(l)
        mlp_out_tok = dict(tok=[sv['h_mid'], sv['a_mlp']], cot=[dh])
        (dh_mid, da_mlp), (dgt_f,), _, _, _ = run_stage(
            "mlp_out" + sfx, fn_mlp_out, which="small", dtok_dtype={1: BF16}, **mlp_out_tok, **{**a['mlp_out'], 'ts': 256})
        _, _, _, _, (dw_mlp_out,) = run_stage(
            "mlp_out" + sfx, fn_mlp_out, which="big", **mlp_out_tok, **{**a['mlp_out'], 'ts': 512})
        g_full['w_mlp_out'][l] = row_slabs(dw_mlp_out)
        _, _, _, _, (g_full['w_mlp_in'][l],) = run_stage(
            "mlp_in" + sfx, fn_mlp_in, tok=[sv['h_mid']], cot=[da_mlp], which="big", **{**a['mlp_in'], 'ts': 512})
        (dh_mid,), (dsc_f, dsh_f), (g_rep['norm_mlp_g'][l],), _, _ = run_stage(
            "mlp_in" + sfx, fn_mlp_in, tok=[sv['h_mid']], cot=[da_mlp], addin=dh_mid, which="small", **a['mlp_in'])
        (dh_in, do_a, do_b, do_c), (dgt_m,), _, _, (dw_out,) = run_stage(
            "out_proj" + sfx, fn_out_proj, tok=[sv['h'], sv['o_a'], sv['o_b'], sv['o_c']], cot=[dh_mid], **a['out_proj'])
        g_full['w_out'][l] = row_slabs(dw_out)
        attn_cot = mla_attention_bwd(sv['q_at'], sv['k_at'], sv['v_at'], do_c)
        (dmq, dmkv, dmisc_c), _, (g_rep['mla_q_norm_g'][l], g_rep['mla_kv_norm_g'][l], dw_q, dw_kv), _, _ = run_stage(
            "mla_pre" + sfx, fn_mla_pre, tok=[sv['mq'], sv['mkv'], sv['misc']], cot=attn_cot, **a['mla_pre'])
        g_full['mla_w_qb'][l] = column_slabs(dw_q)
        g_full['mla_w_kvb'][l] = column_slabs(dw_kv)
        (drx, drgate), _, _, rg_g, _ = run_stage("rglru" + sfx, fn_rglru, tok=[sv['rx'], sv['rgate']], cot=[do_b], **a['rglru'])
        (g_full['rg_conv_w'][l], g_rep['rg_conv_b'][l], g_rep['rg_b_a'][l], g_rep['rg_b_x'][l], g_rep['rg_lambda'][l],
         g_rep['rg_w_a'][l], g_rep['rg_w_x'][l]) = rg_g
        dxs, dz, g_rep['gdn_norm_g'][l] = gdn_scan_bwd(sv['xs'], sv['z'], row(gdn_norm_g, l), sv['st_in'], do_a)
        (dqkv_act, dmisc_a), _, (g_rep['gdn_a_log'][l], g_rep['gdn_dt_bias'][l]), _, _ = run_stage(
            "gdn_local" + sfx, fn_gdn_local, tok=[sv['qkv_act'], sv['misc']], cot=dxs, **a['gdn_local'])
        (dqkv_raw,), _, _, (g_full['gdn_conv_w'][l],), _ = run_stage(
            "gdn_conv" + sfx, fn_gdn_conv, tok=[sv['qkv_raw']], cot=[dqkv_act], **a['gdn_conv'])
        (dh,), (dsc_m, dsh_m), (g_rep['norm_mix_g'][l],), _, (dw_cat,) = run_stage(
            "mixer_in" + sfx, fn_mixer_in, tok=[sv['h']], cot=[dqkv_raw, dz, drx, drgate, dmq, dmkv, dmisc_a + dmisc_c],
            addin=dh_in, **a['mixer_in'])
        g_full['w_in'][l] = column_slabs(_proj_cols_back(dw_cat))
        dmod[l] = jnp.concatenate([dsh_m, dsc_m, dgt_m, dsh_f, dsc_f, dgt_f], axis=-1).reshape(bsz, 6 * d)
    grad_x = dh

    dmod = jnp.stack(dmod)
    dmod_pad = jnp.concatenate([dmod.reshape(depth * bsz, 6 * d), jnp.zeros((8 - depth * bsz, 6 * d), F32)], axis=0)
    dmod_all = all_gather8("gather_dmod", dmod_pad, True).reshape(N_DEV, 8, 6 * d)[:, :depth * bsz]
    dmod_all = dmod_all.reshape(N_DEV, depth, bsz, 6 * d).transpose(1, 0, 2, 3).reshape(depth, N_DEV * bsz, 6 * d)
    g_w_mod = mod_weight_grad(c_all, lax.dynamic_slice(dmod_all, (0, 0, chip * mod_cols), (depth, N_DEV * bsz, mod_cols)))

    g_rep = {n: jnp.stack(v) for n, v in g_rep.items()}
    g_rep['rg_w_a'] = _block_diag_back(g_rep['rg_w_a'])
    g_rep['rg_w_x'] = _block_diag_back(g_rep['rg_w_x'])
    g_rep['final_norm_g'] = d_final_g
    g_rep['b_mod'] = jnp.sum(dmod, axis=1)
    conv_names = ['gdn_conv_w', 'rg_conv_w']
    conv_full_shapes = [(depth,) + g_full[n][0].shape for n in conv_names]
    small_shapes = [wts[n].shape for n in REPLICATED] + conv_full_shapes
    rep_part = _pack_rows([g_rep[n].reshape(wts[n].shape) for n in REPLICATED] + [jnp.stack(g_full[n]) for n in conv_names])
    rep_rows = rep_part.shape[0]
    rep_all = all_gather8("gather_small_grads", rep_part, True).reshape(N_DEV, rep_rows, PACK_COLS)
    conv_zeros = [jnp.zeros(s, F32) for s in conv_full_shapes]
    rep_out = adamw_reduce("adamw_small", rep_all, *[
        _pack_rows([src[n] for n in REPLICATED] + conv_zeros) for src in (wts, mom_m, mom_v)])
    small_names = REPLICATED + conv_names
    rep_g, rep_d, rep_m, rep_v = [dict(zip(small_names, _unpack_rows(o, small_shapes))) for o in rep_out]
    sh_g = {}
    for n in conv_names:
        cols = wts[n].shape[2]
        sh_g[n] = lax.dynamic_slice(rep_g.pop(n), (0, 0, chip * cols), wts[n].shape)
        for dct in (rep_d, rep_m, rep_v):
            dct.pop(n)

    core_id = mc.reshape(1).astype(jnp.int32)
    units = [(i, l) for i in range(len(GATHER_BF16)) for l in range(depth)]
    gs = [g_full[GATHER_BF16[i]][l] for i, l in units]
    from_sibling = grad_sibling_exchange("grad_sibling_exchange", gs)
    sums32, sums16 = zip(*[add_half("grad_add_%s%d" % (GATHER_BF16[i], l), g, s, core_id)
                           for (i, l), g, s in zip(units, gs, from_sibling)])
    from_chips = grad_chip_exchange("grad_chip_exchange", list(sums16))
    shards = [None] * len(GATHER_BF16)
    for (i, l), p, r in zip(units, sums32, from_chips):
        n = GATHER_BF16[i]
        shards[i] = sum_peers("grad_sum_%s%d" % (n, l), p, r, ids, wts[n].shape, l, acc=shards[i])
    sh_g.update(zip(GATHER_BF16, grad_half_exchange("grad_half_exchange", shards)))
    sh_names = list(SHARDED)

    def as2d(t):
        return t.reshape(-1, t.shape[-1])

    sh_d, sh_m, sh_v = {}, {}, {}
    for n in sh_names + ['w_mod']:
        g = g_w_mod if n == 'w_mod' else sh_g[n]
        res = adamw("adamw_" + n, as2d(wts[n]), as2d(g), as2d(mom_m[n]), as2d(mom_v[n]))
        sh_d[n], sh_m[n], sh_v[n] = (r.reshape(wts[n].shape) for r in res)
    sh_g['w_mod'] = g_w_mod

    def pick(shd, rep):
        return [shd[n] if n in shd else rep[n] for n in WEIGHTS]

    return (loss, grad_x, *pick(sh_g, rep_g), *pick(sh_d, rep_d), *pick(sh_m, rep_m), *pick(sh_v, rep_v))
```

```python
import functools

import jax
import jax.numpy as jnp
import numpy as np
from jax import lax
from jax.experimental import pallas as pl
from jax.experimental.pallas import tpu as pltpu

F32, BF16 = jnp.float32, jnp.bfloat16
HI = lax.Precision.HIGH
MESH = pl.DeviceIdType.MESH

EPS = 1e-6
CHUNK = 64
GDN_HEADS = 4
MLA_HEADS = 4
RG_C = 8.0
ROPE_THETA = 10000.0
N_DEV = 8
N_CHIP = 4
V7X_VMEM_LIMIT = 60 * 1024 * 1024
ADAM_LR, ADAM_B1, ADAM_B2, ADAM_EPS, ADAM_WD, ADAM_STEP = 0.001, 0.9, 0.999, 1e-08, 0.01, 10


def _params(n_grid):
    return pltpu.CompilerParams(dimension_semantics=("arbitrary",) * n_grid, vmem_limit_bytes=V7X_VMEM_LIMIT)


def _dot(a, b, dims=(((1,), (0,)), ((), ()))):
    return lax.dot_general(a.astype(BF16), b.astype(BF16), dims, preferred_element_type=F32)


@jax.custom_vjp
def _mm_probe(x, w, probe):
    return _dot(x, w)


def _mm_probe_fwd(x, w, probe):
    return _dot(x, w), (x, w)


def _mm_probe_bwd(res, dy):
    x, w = res
    dx = _dot(dy, w, (((1,), (1,)), ((), ())))
    dw = _dot(x, dy, (((0,), (0,)), ((), ())))
    return dx, jnp.zeros_like(w), dw


_mm_probe.defvjp(_mm_probe_fwd, _mm_probe_bwd)


@jax.custom_vjp
def _probe_only(x, probe):
    return jnp.zeros((x.shape[0], probe.shape[1]), F32)


def _probe_only_fwd(x, probe):
    return jnp.zeros((x.shape[0], probe.shape[1]), F32), x


def _probe_only_bwd(x, dy):
    return jnp.zeros_like(x), _dot(x, dy, (((0,), (0,)), ((), ())))


_probe_only.defvjp(_probe_only_fwd, _probe_only_bwd)


@jax.custom_vjp
def mmw(x, w):
    return _dot(x, w)


def _mmw_fwd(x, w):
    return _dot(x, w), (x, w)


def _mmw_bwd(res, dy):
    x, w = res
    return _dot(dy, w, (((1,), (1,)), ((), ()))), _dot(x, dy, (((0,), (0,)), ((), ())))


mmw.defvjp(_mmw_fwd, _mmw_bwd)


def rms(x, g):
    return x * lax.rsqrt(jnp.mean(x * x, axis=-1, keepdims=True) + EPS) * g


def _rows(shape):
    return lax.broadcasted_iota(jnp.int32, shape, 0)


def _shift_down(x, s, fill):
    return jnp.where(_rows(x.shape) < s, fill, pltpu.roll(x, s, 0))


def _shift_up(x, s, fill):
    n = x.shape[0]
    return jnp.where(_rows(x.shape) >= n - s, fill, pltpu.roll(x, n - s, 0))


def _make_tshift(s):
    @jax.custom_vjp
    def tshift(x):
        return _shift_down(x, s, 0.0)

    tshift.defvjp(lambda x: (_shift_down(x, s, 0.0), None), lambda _, dy: (_shift_up(dy, s, 0.0),))
    return tshift


_TSHIFT = {s: _make_tshift(s) for s in (1, 2, 3)}


def causal_conv4(x, w):
    y = x * w[3:4, :]
    for j in range(3):
        y = y + _TSHIFT[3 - j](x) * w[j:j + 1, :]
    return y


def _scan_steps(n):
    d = 1
    while d < n:
        yield d
        d *= 2


@jax.custom_vjp
def linscan(a, b):
    return _linscan_fwd_impl(a, b)


def _linscan_fwd_impl(a, b):
    for d in _scan_steps(a.shape[0]):
        b = a * _shift_down(b, d, 0.0) + b
        a = a * _shift_down(a, d, 1.0)
    return b


def _linscan_fwd(a, b):
    h = _linscan_fwd_impl(a, b)
    return h, (a, h)


def _linscan_bwd(res, dh):
    a, h = res
    an = _shift_up(a, 1, 0.0)
    lam = dh
    for d in _scan_steps(a.shape[0]):
        lam = an * _shift_up(lam, d, 0.0) + lam
        an = an * _shift_up(an, d, 1.0)
    return lam * _shift_down(h, 1, 0.0), lam


linscan.defvjp(_linscan_fwd, _linscan_bwd)


def _chunk_scan(x, reverse):
    pos = _rows(x.shape) % CHUNK
    n = x.shape[0]
    d = 1
    while d < CHUNK:
        if reverse:
            x = x + jnp.where(pos < CHUNK - d, pltpu.roll(x, n - d, 0), 0.0)
        else:
            x = x + jnp.where(pos >= d, pltpu.roll(x, d, 0), 0.0)
        d *= 2
    return x


@jax.custom_vjp
def chunk_cumsum(x):
    return _chunk_scan(x, False)


@jax.custom_vjp
def chunk_revcumsum(x):
    return _chunk_scan(x, True)


chunk_cumsum.defvjp(lambda x: (_chunk_scan(x, False), None), lambda _, g: (_chunk_scan(g, True),))
chunk_revcumsum.defvjp(lambda x: (_chunk_scan(x, True), None), lambda _, g: (_chunk_scan(g, False),))


def _bmm(a, b, precision=None):
    return jnp.einsum('nij,njk->nik', a, b, precision=precision, preferred_element_type=F32)


@jax.custom_vjp
def inv_unit_lower(l):
    return _inv_impl(l)


def _inv_impl(l):
    n = l.shape[-1]
    eye = (_rows((n, n)) == lax.broadcasted_iota(jnp.int32, (n, n), 1)).astype(F32)
    p = -l
    a = eye + p
    k = 1
    while 2 * k < n:
        p = _bmm(p, p, HI)
        a = a + _bmm(a, p, HI)
        k *= 2
    return a


def _inv_fwd(l):
    a = _inv_impl(l)
    return a, a


def _inv_bwd(a, da):
    at = jnp.swapaxes(a, 1, 2)
    return (-_bmm(_bmm(at, da, HI), at, HI),)


inv_unit_lower.defvjp(_inv_fwd, _inv_bwd)


def neg_expm1(y):
    series = -(y * (1.0 + y * (0.5 + y * (1.0 / 6.0 + y * (1.0 / 24.0)))))
    return jnp.where(y > -0.05, series, 1.0 - jnp.exp(y))


def run_stage(name, fn, *, tok, tok_nd=(), ex=(), par=(), par_tiled=(), big=(), out_ch, ts, nc=1, cot=None, addin=None,
              which="all", dtok_dtype=None, side=None):
    tok, tok_nd, ex, par, par_tiled, big = map(list, (tok, tok_nd, ex, par, par_tiled, big))
    big_layer = [b[1] if isinstance(b, tuple) else None for b in big]
    big_arrays = [b[0] if isinstance(b, tuple) else b for b in big]
    big = [jax.ShapeDtypeStruct(a.shape if lyr is None else a.shape[1:], a.dtype) for a, lyr in zip(big_arrays, big_layer)]
    bsz, seq, _ = tok[0].shape
    ts = min(ts, seq)
    ns = seq // ts
    grid = (nc, bsz, ns)

    def tok_spec(a):
        cb = a.shape[-1] // nc
        return pl.BlockSpec((None, ts, cb), lambda c, b, s: (b, s, c))

    def ex_spec(a):
        cb = a.shape[-1] // nc
        return pl.BlockSpec((None, 1, cb), lambda c, b, s: (b, 0, c))

    def full_spec(a, single=False):
        nd = a.ndim
        kw = dict(pipeline_mode=pl.Buffered(1)) if single else {}
        return pl.BlockSpec(a.shape, lambda c, b, s: (0,) * nd, **kw)

    def tiled_spec(a):
        if a.ndim == 2:
            return pl.BlockSpec((a.shape[0], a.shape[1] // nc), lambda c, b, s: (0, c))
        return pl.BlockSpec((None,) + a.shape[1:], lambda c, b, s: (c, 0, 0))

    def big_spec(a, lyr):
        if lyr is None:
            return full_spec(a, True)
        nd = a.ndim
        return pl.BlockSpec((None,) + a.shape[1:], lambda c, b, s: (lyr,) + (0,) * (nd - 1), pipeline_mode=pl.Buffered(1))

    n_tok, n_nd, n_ex, n_par, n_pt, n_big = map(len, (tok, tok_nd, ex, par, par_tiled, big))
    in_arrays = tok + tok_nd + ex + par + par_tiled + big_arrays
    in_specs = ([tok_spec(a) for a in tok + tok_nd] + [ex_spec(a) for a in ex] + [full_spec(a) for a in par]
                + [tiled_spec(a) for a in par_tiled] + [big_spec(a, lyr) for a, lyr in zip(big_arrays, big_layer)])
    out_tok_shapes = [jax.ShapeDtypeStruct((bsz, seq, ch), F32) for ch in out_ch]
    n_in = len(in_arrays)

    def split(vals):
        i = 0
        groups = []
        for n in (n_tok, n_nd, n_ex, n_par, n_pt, n_big):
            groups.append(list(vals[i:i + n]))
            i += n
        return groups

    def split_grads(vals):
        i = 0
        groups = []
        for n in (n_tok, n_ex, n_par, n_pt, n_big):
            groups.append(list(vals[i:i + n]))
            i += n
        return groups

    side = list(side or [])
    side_arrays = [a for job in side for a in job[0]]
    n_side = len(side_arrays)
    side_shapes = [jax.ShapeDtypeStruct(a.shape, a.dtype) for a in side_arrays]
    side_scratch = [pltpu.SemaphoreType.DMA((job[1],)) for job in side for _ in range(2)]

    def side_jobs(side_refs, sems):
        o = 0
        for j, (arrs, _, issue) in enumerate(side):
            yield issue(side_refs[o:o + len(arrs)], sems[2 * j], sems[2 * j + 1])
            o += len(arrs)

    def side_start(side_refs, sems):
        if side:
            c, b, s = pl.program_id(0), pl.program_id(1), pl.program_id(2)

            @pl.when(jnp.logical_and(jnp.logical_and(c == 0, b == 0), s == 0))
            def _():
                for starts, _, _ in side_jobs(side_refs, sems):
                    for cp in starts:
                        cp.start()

    def side_finish(side_refs, sems):
        if side:
            c, b, s = pl.program_id(0), pl.program_id(1), pl.program_id(2)

            @pl.when(jnp.logical_and(jnp.logical_and(c == nc - 1, b == bsz - 1), s == ns - 1))
            def _():
                for _, recv_waits, send_waits in side_jobs(side_refs, sems):
                    for cp in recv_waits:
                        cp.wait_recv()
                    for cp in send_waits:
                        cp.wait_send()

    if cot is None:
        n_out = len(out_tok_shapes)

        def body(*refs):
            tv, ndv, ev, pv, ptv, _ = split([r[...] for r in refs[:n_in - n_big]] + [None] * n_big)
            b_refs = refs[n_in - n_big:n_in]
            side_refs = refs[n_in + n_side + n_out:n_in + 2 * n_side + n_out]
            sems = refs[n_in + 2 * n_side + n_out:]
            side_start(side_refs, sems)
            outs = fn(tv, ndv, ev, pv, ptv, lambda x, i, j=None: _dot(x, b_refs[i][...] if j is None else b_refs[i][j]))
            for r, o in zip(refs[n_in + n_side:], outs):
                r[...] = o
            side_finish(side_refs, sems)

        res = pl.pallas_call(
            body, name=name, grid=grid, in_specs=in_specs + [_ANY] * n_side,
            out_specs=[tok_spec(a) for a in out_tok_shapes] + [_ANY] * n_side,
            out_shape=out_tok_shapes + side_shapes, input_output_aliases={n_in + j: n_out + j for j in range(n_side)},
            scratch_shapes=side_scratch, compiler_params=_params(3))(*in_arrays, *side_arrays)
        return (res[:n_out], res[n_out:]) if side else res

    cot = list(cot)
    has_addin = addin is not None
    extra = cot + ([addin] if has_addin else [])
    n_cot = len(cot)
    want_small, want_big = which in ("all", "small"), which in ("all", "big")
    if not want_small:
        in_arrays, in_specs, n_in = in_arrays[:n_in - n_big], in_specs[:n_in - n_big], n_in - n_big
    small_arrays = tok + ex + par + par_tiled
    g_shapes = [jax.ShapeDtypeStruct(a.shape, F32) for a in (small_arrays if want_small else []) + (big if want_big else [])]
    for i, dt_ in (dtok_dtype or {}).items():
        g_shapes[i] = jax.ShapeDtypeStruct(g_shapes[i].shape, dt_)
    g_specs = (([tok_spec(a) for a in tok] + [ex_spec(a) for a in ex] + [full_spec(a) for a in par]
                + [tiled_spec(a) for a in par_tiled]) if want_small else []) + (
                    [full_spec(a, True) for a in big] if want_big else [])

    def body(*refs):
        c, b, s = pl.program_id(0), pl.program_id(1), pl.program_id(2)
        n_small_in = n_tok + n_nd + n_ex + n_par + n_pt
        tv, ndv, ev, pv, ptv, _ = split([r[...] for r in refs[:n_small_in]] + [None] * n_big)
        b_refs = refs[n_small_in:n_in]
        cots = [r[...].astype(F32) for r in refs[n_in:n_in + n_cot]]
        n_g = len(g_shapes)
        g_refs = list(refs[n_in + len(extra) + n_side:n_in + len(extra) + n_side + n_g])
        side_refs = refs[n_in + len(extra) + n_side + n_g:n_in + len(extra) + 2 * n_side + n_g]
        sems = refs[n_in + len(extra) + 2 * n_side + n_g:]
        side_start(side_refs, sems)
        probes = [jnp.zeros(w.shape, F32) if w.ndim == 2 else [jnp.zeros(w.shape[1:], F32) for _ in range(w.shape[0])]
                  for w in big]

        def f(tv_, ev_, pv_, ptv_, probes_):
            def mm(x, i, j=None):
                probe = None if probes_ is None else (probes_[i] if j is None else probes_[i][j])
                if not want_small:
                    return _probe_only(x, probe)
                w = b_refs[i][...] if j is None else b_refs[i][j]
                return _dot(x, w) if probe is None else _mm_probe(x, w, probe)

            return fn(tv_, ndv, ev_, pv_, ptv_, mm)

        dt = de = dp = dpt = dbg = ()
        if which == "all":
            dt, de, dp, dpt, dbg = jax.vjp(f, tv, ev, pv, ptv, probes)[1](cots)
        elif which == "small":
            dt, de, dp, dpt = jax.vjp(lambda *a: f(*a, None), tv, ev, pv, ptv)[1](cots)
        else:
            (dbg,) = jax.vjp(lambda p: f(tv, ev, pv, ptv, p), probes)[1](cots)
        if has_addin:
            dt = [dt[0] + refs[n_in + n_cot][...]] + list(dt[1:])
        if want_small:
            gt_r, ge_r, gp_r, gpt_r, gb_r = split_grads(g_refs + ([] if want_big else [None] * n_big))
        else:
            gt_r, ge_r, gp_r, gpt_r, gb_r = [], [], [], [], g_refs
        for r, g in zip(gt_r, dt):
            r[...] = g.astype(r.dtype)

        def accumulate(r, g, first):
            @pl.when(first)
            def _():
                r[...] = g

            @pl.when(jnp.logical_not(first))
            def _():
                r[...] += g

        for r, g in zip(ge_r, de):
            accumulate(r, g, s == 0)
        first_all = jnp.logical_and(jnp.logical_and(c == 0, b == 0), s == 0)
        for r, g in zip(gp_r, dp):
            accumulate(r, g, first_all)
        for r, g in zip(gpt_r, dpt):
            accumulate(r, g, jnp.logical_and(b == 0, s == 0))
        for r, g in zip(gb_r, dbg):
            if isinstance(g, (list, tuple)):
                for j, gj in enumerate(g):
                    accumulate(r.at[j], gj, first_all)
            else:
                accumulate(r, g, first_all)
        side_finish(side_refs, sems)

    n_args = n_in + len(extra)
    res = pl.pallas_call(
        body, name=name + "_bwd" + ("" if which == "all" else "_" + which), grid=grid,
        in_specs=in_specs + [tok_spec(a) for a in extra] + [_ANY] * n_side, out_specs=g_specs + [_ANY] * n_side,
        out_shape=g_shapes + side_shapes, input_output_aliases={n_args + j: len(g_shapes) + j for j in range(n_side)},
        scratch_shapes=side_scratch, compiler_params=_params(3))(*in_arrays, *extra, *side_arrays)
    res, side_out = list(res[:len(g_shapes)]), list(res[len(g_shapes):])
    groups = [[], [], [], [], res] if not want_small else split_grads(res + ([] if want_big else [None] * n_big))
    return (groups, side_out) if side else groups


PROJ_PIECES = (("qkv", 768), ("z", 256), ("rx", 512), ("rgate", 512), ("mq", 256), ("mkv", 128), ("misc", 128))
PROJ_WIDTH = sum(w for _, w in PROJ_PIECES)
MISC_KR, MISC_A, MISC_B = 0, 32, 36


def fn_mixer_in(tok, nd, ex, par, pt, mm):
    (h,), (sc, sh), (g,) = tok, ex, par
    proj = mm(rms(h, g) * (1.0 + sc) + sh, 0)
    outs, o = [], 0
    for _, w in PROJ_PIECES:
        outs.append(proj[:, o:o + w])
        o += w
    return outs


def fn_gdn_conv(tok, nd, ex, par, pt, mm):
    return [jax.nn.silu(causal_conv4(tok[0], pt[0]))]


def _tri_masks():
    r = _rows((CHUNK, CHUNK))
    c = lax.broadcasted_iota(jnp.int32, (CHUNK, CHUNK), 1)
    return (c <= r), (c < r)


def fn_gdn_local(tok, nd, ex, par, pt, mm):
    (qkv, misc), (a_log, dt_bias) = tok, par
    ts = qkv.shape[0]
    nb = ts // CHUNK
    lower, strict = _tri_masks()
    g_all = -jnp.exp(a_log) * jax.nn.softplus(misc[:, MISC_A:MISC_A + GDN_HEADS] + dt_bias)
    g_cum = chunk_cumsum(g_all)
    g_tot = g_cum + chunk_revcumsum(g_all) - g_all
    outs = [[] for _ in range(6)]
    for hd in range(GDN_HEADS):
        def head(x, base):
            return x[:, base + 64 * hd: base + 64 * hd + 64]

        def l2n(x):
            return x * lax.rsqrt(jnp.sum(x * x, axis=-1, keepdims=True) + EPS)

        q = (l2n(head(qkv, 0)) * (64.0 ** -0.5)).reshape(nb, CHUNK, 64)
        k = l2n(head(qkv, 256)).reshape(nb, CHUNK, 64)
        v = head(qkv, 512).reshape(nb, CHUNK, 64)
        b = misc[:, MISC_B + hd: MISC_B + hd + 1]
        beta = jax.nn.sigmoid(b).reshape(nb, CHUNK, 1)
        gi = jnp.broadcast_to(g_cum[:, hd:hd + 1].reshape(nb, CHUNK, 1), (nb, CHUNK, CHUNK))
        gl = jnp.broadcast_to(g_tot[:, hd:hd + 1].reshape(nb, CHUNK, 1), (nb, CHUNK, CHUNK))
        diff = gi - jnp.swapaxes(gi, 1, 2)
        decay = jnp.where(lower, jnp.exp(jnp.where(lower, diff, 0.0)), 0.0)
        kb = k * beta
        vb = v * beta
        kk = jnp.einsum('ncd,nsd->ncs', kb.astype(BF16), k.astype(BF16), preferred_element_type=F32)
        amat = inv_unit_lower(jnp.where(strict, kk * decay, 0.0))
        eg = jnp.exp(gi)
        u = _bmm(amat, vb, HI)
        w = _bmm(amat, kb * eg, HI)
        qk = jnp.einsum('ncd,nsd->ncs', q.astype(BF16), k.astype(BF16), preferred_element_type=F32) * decay
        qd = q * eg
        kt = k * jnp.exp(gl - gi)
        cd = jnp.exp(gl)
        for lst, val in zip(outs, (qk, qd, u, w, kt, cd)):
            lst.append(val.reshape(ts, 64))
    return [jnp.concatenate(lst, axis=-1) for lst in outs]


def fn_rglru(tok, nd, ex, par, pt, mm):
    (rx, rgate), (conv_w, conv_b, b_a, b_x, lam, bd_a, bd_x) = tok, pt
    xc = causal_conv4(rx, conv_w) + conv_b
    r = jax.nn.sigmoid(mmw(xc, bd_a) + b_a)
    i = jax.nn.sigmoid(mmw(xc, bd_x) + b_x)
    log_a = -RG_C * r * jax.nn.softplus(-lam)
    a = jnp.exp(log_a)
    bterm = jnp.sqrt(neg_expm1(2.0 * log_a)) * (i * xc)
    return [linscan(a, bterm) * jax.nn.gelu(rgate)]


def _rope32(x, cos, sin):
    x1, x2 = x[:, :16], x[:, 16:32]
    return jnp.concatenate([x1 * cos - x2 * sin, x2 * cos + x1 * sin], axis=-1)


MLA_QK = 96


def fn_mla_pre(tok, nd, ex, par, pt, mm):
    (mq, mkv, misc), (cs,), (g_q, g_kv, w_q, w_kv) = tok, nd, par
    q = mmw(rms(mq, g_q), w_q)
    kv = mmw(rms(mkv, g_kv), w_kv)
    cos, sin = cs[:, 0:16], cs[:, 16:32]
    kp = _rope32(misc[:, MISC_KR:MISC_KR + 32], cos, sin)
    qs, ks, vs = [], [], []
    for h in range(MLA_HEADS):
        qs += [q[:, MLA_QK * h: MLA_QK * h + 64], _rope32(q[:, MLA_QK * h + 64: MLA_QK * h + 96], cos, sin)]
        ks += [kv[:, 128 * h: 128 * h + 64], kp]
        vs.append(kv[:, 128 * h + 64: 128 * h + 128])
    return [jnp.concatenate(qs, axis=-1), jnp.concatenate(ks, axis=-1), jnp.concatenate(vs, axis=-1)]


def fn_out_proj(tok, nd, ex, par, pt, mm):
    (h, o_a, o_b, o_c), (gt,) = tok, ex
    return [h + gt * mm(jnp.concatenate([o_a, o_b, o_c], axis=-1), 0)]


def fn_mlp_in(tok, nd, ex, par, pt, mm):
    (h,), (sc, sh), (g,) = tok, ex, par
    u = rms(h, g) * (1.0 + sc) + sh
    return [jnp.concatenate([mm(u, 0, j) for j in range(N_CHIP)], axis=-1)]


def fn_mlp_out(tok, nd, ex, par, pt, mm):
    (h, a), (gt,) = tok, ex
    return [h + gt * mm(jnp.square(jax.nn.relu(a)), 0)]


GDN_W = GDN_HEADS * 64


def _head_mask():
    r = _rows((GDN_W, GDN_W)) // 64
    c = lax.broadcasted_iota(jnp.int32, (GDN_W, GDN_W), 1) // 64
    return r == c


def _heads_diag(x):
    return jnp.where(_head_mask(), jnp.concatenate([x] * GDN_HEADS, axis=0), 0.0)


def _heads_compact(s):
    return s[0:64] + s[64:128] + s[128:192] + s[192:256]


def _gdn_step(state, qk, qd, u, w, kt, cd, z, norm_g):
    v_new = u - _dot(w, state)
    o = _dot(qd, state) + _dot(qk, _heads_diag(v_new))
    update = _dot(kt, v_new, (((0,), (0,)), ((), ())))
    new_state = state * jnp.broadcast_to(cd[0:1, :], (GDN_W, GDN_W)) + jnp.where(_head_mask(), update, 0.0)
    outs = [rms(o[:, 64 * hd: 64 * hd + 64], norm_g) * jax.nn.silu(z[:, 64 * hd: 64 * hd + 64]) for hd in range(GDN_HEADS)]
    return new_state, jnp.concatenate(outs, axis=-1)


def gdn_scan(xs, z, norm_g):
    bsz, seq, _ = z.shape
    n = seq // CHUNK
    blk = pl.BlockSpec((bsz, CHUNK, 256), lambda i: (0, i, 0))

    def body(qk, qd, u, w, kt, cd, z_ref, g_ref, o_ref, st_out, st):
        @pl.when(pl.program_id(0) == 0)
        def _():
            st[...] = jnp.zeros_like(st)

        for b in range(bsz):
            state = st[b]
            st_out[b] = _heads_compact(state)
            st[b], o_ref[b] = _gdn_step(state, qk[b], qd[b], u[b], w[b], kt[b], cd[b], z_ref[b], g_ref[...])

    return pl.pallas_call(
        body, name="gdn_scan", grid=(n,), in_specs=[blk] * 7 + [pl.BlockSpec((1, 64), lambda i: (0, 0))],
        out_specs=[blk, blk], out_shape=[jax.ShapeDtypeStruct((bsz, seq, 256), F32)] * 2,
        scratch_shapes=[pltpu.VMEM((bsz, GDN_W, GDN_W), F32)], compiler_params=_params(1))(*xs, z, norm_g)


def gdn_scan_bwd(xs, z, norm_g, st_in, do):
    bsz, seq, _ = z.shape
    n = seq // CHUNK
    blk = pl.BlockSpec((bsz, CHUNK, 256), lambda i: (0, n - 1 - i, 0))
    gspec = pl.BlockSpec((1, 64), lambda i: (0, 0))

    def body(qk, qd, u, w, kt, cd, z_ref, g_ref, st_ref, do_ref, dqk, dqd, du, dw, dkt, dcd, dz, dg, dst):
        first = pl.program_id(0) == 0

        @pl.when(first)
        def _():
            dst[...] = jnp.zeros_like(dst)

        dg_sum = None
        for b in range(bsz):
            _, vjp = jax.vjp(_gdn_step, _heads_diag(st_ref[b]), qk[b], qd[b], u[b], w[b], kt[b], cd[b], z_ref[b], g_ref[...])
            grads = vjp((dst[b], do_ref[b]))
            dst[b] = jnp.where(_head_mask(), grads[0], 0.0)
            for r, g in zip((dqk, dqd, du, dw, dkt, dcd, dz), grads[1:8]):
                r[b] = g
            dg_sum = grads[8] if dg_sum is None else dg_sum + grads[8]

        @pl.when(first)
        def _():
            dg[...] = dg_sum

        @pl.when(jnp.logical_not(first))
        def _():
            dg[...] += dg_sum

    res = pl.pallas_call(
        body, name="gdn_scan_bwd", grid=(n,), in_specs=[blk] * 7 + [gspec, blk, blk],
        out_specs=[blk] * 7 + [gspec], out_shape=[jax.ShapeDtypeStruct((bsz, seq, 256), F32)] * 7
        + [jax.ShapeDtypeStruct((1, 64), F32)],
        scratch_shapes=[pltpu.VMEM((bsz, GDN_W, GDN_W), F32)], compiler_params=_params(1))(*xs, z, norm_g, st_in, do)
    return list(res[:6]), res[6], res[7]


ATTN_TQ = 256
ATTN_SCALE = 96.0 ** -0.5
ATTN_KEY_FRACTIONS = (4, 2, 1)


def _attn_head(q, k, v, q0):
    s = _dot(q, k, (((1,), (1,)), ((), ()))) * ATTN_SCALE
    qc = (q0 + _rows(s.shape)) // CHUNK
    kc = lax.broadcasted_iota(jnp.int32, s.shape, 1) // CHUNK
    s = jnp.where(kc <= qc, s, -1e30)
    p = jnp.exp(s - jnp.max(s, axis=-1, keepdims=True))
    p = p / jnp.sum(p, axis=-1, keepdims=True)
    return _dot(p, v)


def _key_lengths(seq):
    return sorted({max(ATTN_TQ, seq // f) for f in ATTN_KEY_FRACTIONS})


def _key_variant(i, seq):
    need = (i + 1) * ATTN_TQ
    return sum(((need > klen).astype(jnp.int32) for klen in _key_lengths(seq)[:-1]), jnp.int32(0))


ATTN_QW, ATTN_VW = MLA_HEADS * MLA_QK, MLA_HEADS * 64


def _attn_specs(seq):
    def qspec(ch):
        return pl.BlockSpec((None, ATTN_TQ, ch), lambda b, i: (b, i, 0))

    def kspec(ch):
        return pl.BlockSpec((None, seq, ch), lambda b, i: (b, 0, 0))

    return qspec, kspec


def mla_attention(q, k, v):
    bsz, seq, _ = q.shape
    qspec, kspec = _attn_specs(seq)

    def body(q_r, k_r, v_r, o_r):
        i = pl.program_id(1)
        q0 = i * ATTN_TQ

        def with_keys(klen):
            outs = [_attn_head(q_r[:, MLA_QK * h: MLA_QK * h + MLA_QK], k_r[0:klen, MLA_QK * h: MLA_QK * h + MLA_QK],
                               v_r[0:klen, 64 * h: 64 * h + 64], q0) for h in range(MLA_HEADS)]
            o_r[...] = jnp.concatenate(outs, axis=-1)

        for j, klen in enumerate(_key_lengths(seq)):
            pl.when(_key_variant(i, seq) == j)(functools.partial(with_keys, klen))

    return pl.pallas_call(
        body, name="mla_attention", grid=(bsz, seq // ATTN_TQ), in_specs=[qspec(ATTN_QW), kspec(ATTN_QW), kspec(ATTN_VW)],
        out_specs=qspec(ATTN_VW), out_shape=jax.ShapeDtypeStruct((bsz, seq, ATTN_VW), F32), compiler_params=_params(2))(
            q, k, v)


def mla_attention_bwd(q, k, v, do):
    bsz, seq, _ = q.shape
    qspec, kspec = _attn_specs(seq)

    def body(q_r, k_r, v_r, do_r, dq_r, dk_r, dv_r):
        i = pl.program_id(1)
        q0 = i * ATTN_TQ

        @pl.when(i == 0)
        def _():
            dk_r[...] = jnp.zeros_like(dk_r)
            dv_r[...] = jnp.zeros_like(dv_r)

        def with_keys(klen):
            dq, dk, dv = [], [], []
            for h in range(MLA_HEADS):
                qk = slice(MLA_QK * h, MLA_QK * h + MLA_QK)
                sl = slice(64 * h, 64 * h + 64)
                _, vjp = jax.vjp(functools.partial(_attn_head, q0=q0), q_r[:, qk], k_r[0:klen, qk], v_r[0:klen, sl])
                a, b, c = vjp(do_r[:, sl])
                dq.append(a)
                dk.append(b)
                dv.append(c)
            dq_r[...] = jnp.concatenate(dq, axis=-1)
            dk_r[0:klen, :] += jnp.concatenate(dk, axis=-1)
            dv_r[0:klen, :] += jnp.concatenate(dv, axis=-1)

        for j, klen in enumerate(_key_lengths(seq)):
            pl.when(_key_variant(i, seq) == j)(functools.partial(with_keys, klen))

    shp = lambda ch: jax.ShapeDtypeStruct((bsz, seq, ch), F32)
    return pl.pallas_call(
        body, name="mla_attention_bwd", grid=(bsz, seq // ATTN_TQ),
        in_specs=[qspec(ATTN_QW), kspec(ATTN_QW), kspec(ATTN_VW), qspec(ATTN_VW)],
        out_specs=[qspec(ATTN_QW), kspec(ATTN_QW), kspec(ATTN_VW)],
        out_shape=[shp(ATTN_QW), shp(ATTN_QW), shp(ATTN_VW)], compiler_params=_params(2))(q, k, v, do)


LOSS_TS = 512


def loss_head(h, g, target):
    bsz, seq, d = h.shape
    ts = min(LOSS_TS, seq)
    tok = pl.BlockSpec((None, ts, d), lambda b, s: (b, s, 0))
    gspec = pl.BlockSpec((1, d), lambda b, s: (0, 0))
    lspec = pl.BlockSpec((1, 128), lambda b, s: (0, 0))

    def body(h_r, g_r, t_r, loss_r, dh_r, dg_r):
        first = jnp.logical_and(pl.program_id(0) == 0, pl.program_id(1) == 0)
        tv = t_r[...]

        def f(hv, gv):
            return 0.5 * jnp.sum(jnp.mean(jnp.square(rms(hv, gv) - tv), axis=-1, keepdims=True), axis=0, keepdims=True)

        val, vjp = jax.vjp(f, h_r[...], g_r[...])
        dh, dg = vjp(jnp.ones((1, 1), F32))
        dh_r[...] = dh
        lv = jnp.broadcast_to(val, (1, 128))

        @pl.when(first)
        def _():
            loss_r[...] = lv
            dg_r[...] = dg

        @pl.when(jnp.logical_not(first))
        def _():
            loss_r[...] += lv
            dg_r[...] += dg

    return pl.pallas_call(
        body, name="loss_head", grid=(bsz, seq // ts), in_specs=[tok, gspec, tok], out_specs=[lspec, tok, gspec],
        out_shape=[jax.ShapeDtypeStruct((1, 128), F32), jax.ShapeDtypeStruct(h.shape, F32), jax.ShapeDtypeStruct((1, d), F32)],
        compiler_params=_params(2))(h, g, target)


def _adamw_math(w, g, m, v):
    m = ADAM_B1 * m + (1.0 - ADAM_B1) * g
    v = ADAM_B2 * v + (1.0 - ADAM_B2) * jnp.square(g)
    m_hat = m / (1.0 - ADAM_B1 ** ADAM_STEP)
    v_hat = v / (1.0 - ADAM_B2 ** ADAM_STEP)
    return -ADAM_LR * (m_hat / (jnp.sqrt(v_hat) + ADAM_EPS) + ADAM_WD * w), m, v


def _row_block(rows, cols):
    want = max(8, (1 << 18) // cols)
    best = rows
    for r in range(8, rows + 1, 8):
        if rows % r == 0 and r <= want:
            best = r
    return best if rows % 8 == 0 else rows


def adamw(name, w, g, m, v):
    rows, cols = w.shape
    rb = _row_block(rows, cols)
    spec = pl.BlockSpec((rb, cols), lambda i: (i, 0))

    def body(w_r, g_r, m_r, v_r, d_o, m_o, v_o):
        d, mn, vn = _adamw_math(w_r[...], g_r[...], m_r[...], v_r[...])
        d_o[...] = d
        m_o[...] = mn
        v_o[...] = vn

    return pl.pallas_call(body, name=name, grid=(rows // rb,), in_specs=[spec] * 4, out_specs=[spec] * 3,
                          out_shape=[jax.ShapeDtypeStruct(w.shape, F32)] * 3, compiler_params=_params(1))(w, g, m, v)


def adamw_reduce(name, parts, w, m, v):
    rows, cols = w.shape
    rb = _row_block(rows, cols)
    spec = pl.BlockSpec((rb, cols), lambda i: (i, 0))
    pspec = pl.BlockSpec((N_DEV, rb, cols), lambda i: (0, i, 0))

    def body(p_r, w_r, m_r, v_r, g_o, d_o, m_o, v_o):
        g = p_r[0]
        for k in range(1, N_DEV):
            g = g + p_r[k]
        d, mn, vn = _adamw_math(w_r[...], g, m_r[...], v_r[...])
        g_o[...] = g
        d_o[...] = d
        m_o[...] = mn
        v_o[...] = vn

    return pl.pallas_call(body, name=name, grid=(rows // rb,), in_specs=[pspec, spec, spec, spec], out_specs=[spec] * 4,
                          out_shape=[jax.ShapeDtypeStruct(w.shape, F32)] * 4, compiler_params=_params(1))(parts, w, m, v)


MOD_CB = 512


def mod_matmul(c_rows, w_mod, b_mod):
    nl, d, cols = w_mod.shape

    def body(c_r, w_r, b_r, o_r):
        o_r[...] = _dot(jax.nn.silu(c_r[...]), w_r[...]) + b_r[...]

    return pl.pallas_call(
        body, name="mod_matmul", grid=(nl, cols // MOD_CB),
        in_specs=[pl.BlockSpec((8, d), lambda l, j: (0, 0)), pl.BlockSpec((None, d, MOD_CB), lambda l, j: (l, 0, j)),
                  pl.BlockSpec((None, 1, MOD_CB), lambda l, j: (l, 0, j))],
        out_specs=pl.BlockSpec((None, 8, MOD_CB), lambda l, j: (l, 0, j)),
        out_shape=jax.ShapeDtypeStruct((nl, 8, cols), F32), compiler_params=_params(2))(c_rows, w_mod, b_mod)


def mod_weight_grad(c_all, dmod):
    nl, nb, cols = dmod.shape
    d = c_all.shape[1]

    def body(c_r, g_r, o_r):
        o_r[...] = _dot(jax.nn.silu(c_r[...]), g_r[...], (((0,), (0,)), ((), ())))

    return pl.pallas_call(
        body, name="mod_weight_grad", grid=(nl, cols // MOD_CB),
        in_specs=[pl.BlockSpec((nb, d), lambda l, j: (0, 0)), pl.BlockSpec((None, nb, MOD_CB), lambda l, j: (l, 0, j))],
        out_specs=pl.BlockSpec((None, d, MOD_CB), lambda l, j: (l, 0, j)),
        out_shape=jax.ShapeDtypeStruct((nl, d, cols), F32), compiler_params=_params(2))(c_all, dmod)


def _half_block(hr, cols):
    rb = _row_block(hr, cols)
    return rb if rb % 16 == 0 else hr


def add_half(name, g, s, core):
    _, r, cols = g.shape
    hr = r // 2
    rb = _half_block(hr, cols)
    nblk = hr // rb
    gspec = pl.BlockSpec((None, rb, cols), lambda k, i, c: (k, c[0] * nblk + i, 0))
    spec = pl.BlockSpec((None, rb, cols), lambda k, i, c: (k, i, 0))

    def body(c_r, g_r, s_r, o_r, ob_r):
        t = g_r[...] + s_r[...]
        o_r[...] = t
        ob_r[...] = t.astype(BF16)

    return pl.pallas_call(
        body, name=name, grid_spec=pltpu.PrefetchScalarGridSpec(num_scalar_prefetch=1, grid=(N_CHIP, nblk),
                                                                in_specs=[gspec, spec], out_specs=[spec, spec]),
        out_shape=[jax.ShapeDtypeStruct((N_CHIP, hr, cols), F32), jax.ShapeDtypeStruct((N_CHIP, hr, cols), BF16)],
        compiler_params=_params(2))(core, g, s)


def sum_peers(name, p32, recv, ids, shard_shape, layer, acc=None):
    _, hr, cols = p32.shape
    rb = _half_block(hr, cols)
    nblk = hr // rb

    def slot(k):
        return pl.BlockSpec((None, rb, cols), lambda i, c: ((c[0] + k) % N_CHIP, i, 0))

    def body(c_r, o_r, r1, r2, r3, *rest):
        rest[-1][...] = ((o_r[...] + r1[...].astype(F32)) + r2[...].astype(F32)) + r3[...].astype(F32)

    args = (ids, p32, recv, recv, recv) + (() if acc is None else (acc,))
    return pl.pallas_call(
        body, name=name, grid_spec=pltpu.PrefetchScalarGridSpec(
            num_scalar_prefetch=1, grid=(nblk,),
            in_specs=[slot(0), slot(1), slot(2), slot(3)] + ([] if acc is None else [_ANY]),
            out_specs=pl.BlockSpec((None, rb, cols), lambda i, c: (layer, c[1] * nblk + i, 0))),
        out_shape=jax.ShapeDtypeStruct(shard_shape, F32), input_output_aliases={} if acc is None else {5: 0},
        compiler_params=_params(1))(*args)


def cast_into_slab(name, w, ids, layer=None):
    nl, r, cols = w.shape
    hr = r // 2
    rb = _half_block(hr, cols)
    nblk = hr // rb

    def body(c_r, w_r, o_r):
        o_r[...] = w_r[...].astype(BF16)

    if layer is not None:
        return pl.pallas_call(
            body, name=name, grid_spec=pltpu.PrefetchScalarGridSpec(
                num_scalar_prefetch=1, grid=(nblk,),
                in_specs=[pl.BlockSpec((None, rb, cols), lambda i, c: (layer, c[1] * nblk + i, 0))],
                out_specs=pl.BlockSpec((None, rb, cols), lambda i, c: (c[0], c[1] * nblk + i, 0))),
            out_shape=jax.ShapeDtypeStruct((N_CHIP, r, cols), BF16), compiler_params=_params(1))(ids, w)
    return pl.pallas_call(
        body, name=name, grid_spec=pltpu.PrefetchScalarGridSpec(
            num_scalar_prefetch=1, grid=(nl, nblk),
            in_specs=[pl.BlockSpec((None, rb, cols), lambda l, i, c: (l, c[1] * nblk + i, 0))],
            out_specs=pl.BlockSpec((None, None, rb, cols), lambda l, i, c: (l, c[0], c[1] * nblk + i, 0))),
        out_shape=jax.ShapeDtypeStruct((nl, N_CHIP, r, cols), BF16), compiler_params=_params(2))(ids, w)


def _me():
    return lax.axis_index("x"), lax.axis_index("y"), lax.axis_index("c")


def all_gather8(name, x_shard, in_vmem):
    m_per, n = x_shard.shape
    space = pltpu.VMEM if in_vmem else pl.ANY

    def body(x_ref, out_ref, send_sems, recv_sems, local_sem):
        x, y, c = _me()
        me, sibling = (x, y, c), (x, y, 1 - c)
        chips = [(1 - x, y), (x, 1 - y), (1 - x, 1 - y)]

        def rows(px, py, pc):
            return out_ref.at[pl.ds((4 * px + 2 * py + pc) * m_per, m_per), :]

        def copy(k, block, to, src=None):
            return pltpu.make_async_remote_copy(
                src_ref=rows(*block) if src is None else src, dst_ref=rows(*block), send_sem=send_sems.at[k],
                recv_sem=recv_sems.at[k], device_id=to, device_id_type=MESH)

        mine = pltpu.make_async_copy(x_ref, rows(*me), local_sem)
        mine.start()
        first = [copy(0, me, sibling, src=x_ref)]
        first += [copy(1 + j, me, (*chip, c), src=x_ref) for j, chip in enumerate(chips)]
        for cp in first:
            cp.start()
        passed = [copy(4 + j, (*chip, c), sibling) for j, chip in enumerate(chips)]
        for j, chip in enumerate(chips):
            copy(1 + j, (*chip, c), me).wait_recv()
            passed[j].start()
        copy(0, sibling, me).wait_recv()
        for j, chip in enumerate(chips):
            copy(4 + j, (*chip, 1 - c), me).wait_recv()
        for cp in first + passed:
            cp.wait_send()
        mine.wait()

    return pl.pallas_call(
        body, name=name, out_shape=jax.ShapeDtypeStruct((N_DEV * m_per, n), x_shard.dtype),
        in_specs=[pl.BlockSpec(memory_space=space)], out_specs=pl.BlockSpec(memory_space=space),
        scratch_shapes=[pltpu.SemaphoreType.DMA((7,)), pltpu.SemaphoreType.DMA((7,)), pltpu.SemaphoreType.DMA],
    )(x_shard)


_ANY = pl.BlockSpec(memory_space=pl.ANY)


def all_gather_weights(name, slabs):
    n = len(slabs)

    def body(*refs):
        outs = refs[n:2 * n]
        send_sems, recv_sems = refs[2 * n:]
        x, y, c = _me()
        me, sibling = (x, y, c), (x, y, 1 - c)
        chips = [(1 - x, y), (x, 1 - y), (1 - x, 1 - y)]

        def view(i, px, py, pc):
            hr = slabs[i].shape[2] // 2
            return outs[i].at[:, 2 * px + py, pl.ds(pc * hr, hr), :]

        def copy(i, k, block, to):
            return pltpu.make_async_remote_copy(
                src_ref=view(i, *block), dst_ref=view(i, *block), send_sem=send_sems.at[i, k],
                recv_sem=recv_sems.at[i, k], device_id=to, device_id_type=MESH)

        first = []
        for i in range(n):
            first.append(copy(i, 0, me, sibling))
            first += [copy(i, 1 + j, me, (*chip, c)) for j, chip in enumerate(chips)]
        for cp in first:
            cp.start()
        passed = []
        for j, chip in enumerate(chips):
            for i in range(n):
                copy(i, 1 + j, (*chip, c), me).wait_recv()
                passed.append(copy(i, 4 + j, (*chip, c), sibling))
                passed[-1].start()
        for i in range(n):
            copy(i, 0, sibling, me).wait_recv()
            for j, chip in enumerate(chips):
                copy(i, 4 + j, (*chip, 1 - c), me).wait_recv()
        for cp in first + passed:
            cp.wait_send()

    return pl.pallas_call(
        body, name=name, out_shape=[jax.ShapeDtypeStruct(s.shape, s.dtype) for s in slabs],
        in_specs=[_ANY] * n, out_specs=[_ANY] * n, input_output_aliases={i: i for i in range(n)},
        scratch_shapes=[pltpu.SemaphoreType.DMA((n, 7)), pltpu.SemaphoreType.DMA((n, 7))],
    )(*slabs)


def _slab_block(slab, px, py, pc):
    hr = slab.shape[1] // 2
    return slab.at[2 * px + py, pl.ds(pc * hr, hr), :]


def gather_over_ici(slab):
    def issue(refs, send_sems, recv_sems):
        (buf,) = refs
        x, y, c = _me()
        peers = [(x, y, 1 - c), (1 - x, y, c), (x, 1 - y, c), (1 - x, 1 - y, c)]

        def copy(k, block, to):
            return pltpu.make_async_remote_copy(
                src_ref=_slab_block(buf, *block), dst_ref=_slab_block(buf, *block), send_sem=send_sems.at[k],
                recv_sem=recv_sems.at[k], device_id=to, device_id_type=MESH)

        sends = [copy(k, (x, y, c), p) for k, p in enumerate(peers)]
        arrivals = [copy(k, p, (x, y, c)) for k, p in enumerate(peers)]
        return sends, arrivals, sends

    return ([slab], 4, issue)


def gather_over_d2d(slab):
    def issue(refs, send_sems, recv_sems):
        (buf,) = refs
        x, y, c = _me()
        chips = [(1 - x, y), (x, 1 - y), (1 - x, 1 - y)]

        def copy(k, block):
            return pltpu.make_async_remote_copy(
                src_ref=_slab_block(buf, *block), dst_ref=_slab_block(buf, *block), send_sem=send_sems.at[k],
                recv_sem=recv_sems.at[k], device_id=(x, y, 1 - c), device_id_type=MESH)

        sends = [copy(k, (*chip, c)) for k, chip in enumerate(chips)]
        arrivals = [copy(k, (*chip, 1 - c)) for k, chip in enumerate(chips)]
        return sends, arrivals, sends

    return ([slab], 3, issue)


def chip_exchange_job(ps):
    n = len(ps)

    def issue(refs, send_sems, recv_sems):
        ins, outs = refs[:n], refs[n:]
        mx, my, mc = _me()
        ci = 2 * mx + my
        chips = [(1 - mx, my), (mx, 1 - my), (1 - mx, 1 - my)]
        sends, arrivals = [], []
        for i in range(n):
            for k, (px, py) in enumerate(chips):
                sem = 3 * i + k
                sends.append(pltpu.make_async_remote_copy(
                    src_ref=ins[i].at[2 * px + py], dst_ref=outs[i].at[ci], send_sem=send_sems.at[sem],
                    recv_sem=recv_sems.at[sem], device_id=(px, py, mc), device_id_type=MESH))
                arrivals.append(pltpu.make_async_remote_copy(
                    src_ref=ins[i].at[ci], dst_ref=outs[i].at[2 * px + py], send_sem=send_sems.at[sem],
                    recv_sem=recv_sems.at[sem], device_id=(px, py, mc), device_id_type=MESH))
        return sends, arrivals, sends

    return (list(ps) + [lax.empty(p.shape, p.dtype) for p in ps], 3 * n, issue)


def grad_sibling_exchange(name, gs):
    n = len(gs)

    def body(*refs):
        ins, outs = refs[:n], refs[n:2 * n]
        send_sems, recv_sems = refs[2 * n:]
        mx, my, mc = _me()
        cps = []
        for i in range(n):
            hr = gs[i].shape[1] // 2
            cps.append(pltpu.make_async_remote_copy(
                src_ref=ins[i].at[:, pl.ds((1 - mc) * hr, hr), :], dst_ref=outs[i], send_sem=send_sems.at[i],
                recv_sem=recv_sems.at[i], device_id=(mx, my, 1 - mc), device_id_type=MESH))
            cps[-1].start()
        for cp in cps:
            cp.wait()

    return pl.pallas_call(
        body, name=name, out_shape=[jax.ShapeDtypeStruct((N_CHIP, g.shape[1] // 2, g.shape[2]), g.dtype) for g in gs],
        in_specs=[_ANY] * n, out_specs=[_ANY] * n,
        scratch_shapes=[pltpu.SemaphoreType.DMA((n,)), pltpu.SemaphoreType.DMA((n,))],
    )(*gs)


def grad_chip_exchange(name, ps):
    n = len(ps)

    def body(*refs):
        ins, outs = refs[:n], refs[n:2 * n]
        send_sems, recv_sems = refs[2 * n:]
        mx, my, mc = _me()
        ci = 2 * mx + my
        chips = [(1 - mx, my), (mx, 1 - my), (1 - mx, 1 - my)]
        sends = []
        for i in range(n):
            for k, (px, py) in enumerate(chips):
                sends.append(pltpu.make_async_remote_copy(
                    src_ref=ins[i].at[2 * px + py], dst_ref=outs[i].at[ci], send_sem=send_sems.at[i, k],
                    recv_sem=recv_sems.at[i, k], device_id=(px, py, mc), device_id_type=MESH))
                sends[-1].start()
        for i in range(n):
            for k, (px, py) in enumerate(chips):
                pltpu.make_async_remote_copy(
                    src_ref=ins[i].at[ci], dst_ref=outs[i].at[2 * px + py], send_sem=send_sems.at[i, k],
                    recv_sem=recv_sems.at[i, k], device_id=(px, py, mc), device_id_type=MESH).wait_recv()
        for cp in sends:
            cp.wait_send()

    return pl.pallas_call(
        body, name=name, out_shape=[jax.ShapeDtypeStruct(p.shape, p.dtype) for p in ps], in_specs=[_ANY] * n,
        out_specs=[_ANY] * n, scratch_shapes=[pltpu.SemaphoreType.DMA((n, 3)), pltpu.SemaphoreType.DMA((n, 3))],
    )(*ps)


def grad_half_exchange(name, shards):
    n = len(shards)

    def body(*refs):
        outs = refs[n:2 * n]
        send_sems, recv_sems = refs[2 * n:]
        mx, my, mc = _me()

        def copy(i, core):
            hr = shards[i].shape[1] // 2
            rows = outs[i].at[:, pl.ds(core * hr, hr), :]
            return pltpu.make_async_remote_copy(src_ref=rows, dst_ref=rows, send_sem=send_sems.at[i],
                                                recv_sem=recv_sems.at[i], device_id=(mx, my, 1 - mc), device_id_type=MESH)

        sends = [copy(i, mc) for i in range(n)]
        for cp in sends:
            cp.start()
        for i in range(n):
            copy(i, 1 - mc).wait_recv()
        for cp in sends:
            cp.wait_send()

    return pl.pallas_call(
        body, name=name, out_shape=[jax.ShapeDtypeStruct(s.shape, s.dtype) for s in shards], in_specs=[_ANY] * n,
        out_specs=[_ANY] * n, input_output_aliases={i: i for i in range(n)},
        scratch_shapes=[pltpu.SemaphoreType.DMA((n,)), pltpu.SemaphoreType.DMA((n,))],
    )(*shards)


WEIGHTS = ['w_mod', 'b_mod', 'norm_mix_g', 'w_in', 'gdn_conv_w', 'gdn_a_log', 'gdn_dt_bias', 'gdn_norm_g', 'rg_conv_w',
           'rg_conv_b', 'rg_w_a', 'rg_b_a', 'rg_w_x', 'rg_b_x', 'rg_lambda', 'mla_q_norm_g', 'mla_w_qb', 'mla_kv_norm_g',
           'mla_w_kvb', 'w_out', 'norm_mlp_g', 'w_mlp_in', 'w_mlp_out', 'final_norm_g']
SHARDED = {'w_in': 2, 'gdn_conv_w': 2, 'rg_conv_w': 2, 'mla_w_qb': 2, 'mla_w_kvb': 2, 'w_out': 1, 'w_mlp_in': 2, 'w_mlp_out': 1}
GATHER_BF16 = ('w_in', 'mla_w_qb', 'mla_w_kvb', 'w_out', 'w_mlp_in', 'w_mlp_out')
GATHER_FIRST = GATHER_BF16[:4]
REPLICATED = [n for n in WEIGHTS if n not in SHARDED and n != 'w_mod']
PACK_COLS = 1024


def _pack(arrays, multiple):
    flat = jnp.concatenate([a.reshape(-1) for a in arrays])
    pad = (-flat.shape[0]) % multiple
    return jnp.pad(flat, (0, pad)) if pad else flat


def _unpack(flat, shapes):
    out, o = [], 0
    for shp in shapes:
        n = int(np.prod(shp))
        out.append(flat[o:o + n].reshape(shp))
        o += n
    return out


def _pack_rows(arrays):
    rows = []
    for a in arrays:
        flat = a.reshape(-1)
        pad = (-flat.shape[0]) % PACK_COLS
        rows.append((jnp.pad(flat, (0, pad)) if pad else flat).reshape(-1, PACK_COLS))
    out = jnp.concatenate(rows, axis=0)
    pad = (-out.shape[0]) % 8
    return jnp.pad(out, ((0, pad), (0, 0))) if pad else out


def _unpack_rows(packed, shapes):
    out, r = [], 0
    for shp in shapes:
        n = int(np.prod(shp))
        nr = -(-n // PACK_COLS)
        piece = packed[r:r + nr]
        out.append((piece if n == nr * PACK_COLS else piece.reshape(-1)[:n]).reshape(shp))
        r += nr
    return out


def _unshard(stacked, axis):
    moved = jnp.moveaxis(stacked, 0, axis)
    shp = list(moved.shape)
    shp[axis:axis + 2] = [shp[axis] * shp[axis + 1]]
    return moved.reshape(shp)


def _shard(full, axis):
    shp = list(full.shape)
    shp[axis:axis + 1] = [N_CHIP, shp[axis] // N_CHIP]
    return jnp.moveaxis(full.reshape(shp), axis, 0)


def _proj_cols(w):
    pad = jnp.zeros(w.shape[:-1] + (PROJ_WIDTH - w.shape[-1],), w.dtype)
    return jnp.concatenate([w[..., 0:1024], w[..., 1032:2472], w[..., 1024:1032], pad], axis=-1)


def _proj_cols_back(d):
    return jnp.concatenate([d[..., 0:1024], d[..., 2464:2472], d[..., 1024:2464]], axis=-1)


def _heads_split(w, heads, first):
    per = w.shape[-1] // heads
    r = w.reshape(w.shape[:-1] + (heads, per))
    lead = w.shape[:-1]
    return jnp.concatenate([r[..., :first].reshape(lead + (heads * first,)),
                            r[..., first:].reshape(lead + (heads * (per - first),))], axis=-1)


def _heads_merge(d, heads, first):
    lead = d.shape[:-1]
    per = d.shape[-1] // heads
    a = d[..., :heads * first].reshape(lead + (heads, first))
    b = d[..., heads * first:].reshape(lead + (heads, per - first))
    return jnp.concatenate([a, b], axis=-1).reshape(lead + (heads * per,))


def _block_diag(w):
    nl = w.shape[0]
    eye = jnp.eye(2, dtype=w.dtype)
    return jnp.einsum('lcoij,op->lcoipj', w.reshape(nl, 4, 2, 64, 64), eye).reshape(nl, 4, 128, 128)


def _block_diag_back(g):
    nl = g.shape[0]
    return jnp.einsum('lcoipj,op->lcoij', g.reshape(nl, 4, 2, 64, 2, 64), jnp.eye(2, dtype=g.dtype)).reshape(nl, 8, 64, 64)


def kernel(x, c, positions, w_mod, b_mod, norm_mix_g, w_in, gdn_conv_w, gdn_a_log, gdn_dt_bias, gdn_norm_g, rg_conv_w, rg_conv_b, rg_w_a, rg_b_a, rg_w_x, rg_b_x, rg_lambda, mla_q_norm_g, mla_w_qb, mla_kv_norm_g, mla_w_kvb, w_out, norm_mlp_g, w_mlp_in, w_mlp_out, final_norm_g, loss_target, m_w_mod, m_b_mod, m_norm_mix_g, m_w_in, m_gdn_conv_w, m_gdn_a_log, m_gdn_dt_bias, m_gdn_norm_g, m_rg_conv_w, m_rg_conv_b, m_rg_w_a, m_rg_b_a, m_rg_w_x, m_rg_b_x, m_rg_lambda, m_mla_q_norm_g, m_mla_w_qb, m_mla_kv_norm_g, m_mla_w_kvb, m_w_out, m_norm_mlp_g, m_w_mlp_in, m_w_mlp_out, m_final_norm_g, v_w_mod, v_b_mod, v_norm_mix_g, v_w_in, v_gdn_conv_w, v_gdn_a_log, v_gdn_dt_bias, v_gdn_norm_g, v_rg_conv_w, v_rg_conv_b, v_rg_w_a, v_rg_b_a, v_rg_w_x, v_rg_b_x, v_rg_lambda, v_mla_q_norm_g, v_mla_w_qb, v_mla_kv_norm_g, v_mla_w_kvb, v_w_out, v_norm_mlp_g, v_w_mlp_in, v_w_mlp_out, v_final_norm_g):
    given = dict(locals())
    wts = {n: given[n] for n in WEIGHTS}
    mom_m = {n: given["m_" + n] for n in WEIGHTS}
    mom_v = {n: given["v_" + n] for n in WEIGHTS}
    bsz, seq, d = x.shape
    depth = w_mod.shape[0]
    mx, my, mc = lax.axis_index("x"), lax.axis_index("y"), lax.axis_index("c")
    chip = 2 * mx + my
    dev = 2 * chip + mc

    conv_shapes = [wts['gdn_conv_w'].shape, wts['rg_conv_w'].shape]
    conv_flat = _pack([wts['gdn_conv_w'], wts['rg_conv_w']], d)
    conv_rows = conv_flat.shape[0] // d
    assert bsz + conv_rows <= 8
    c_pad = jnp.concatenate([c, conv_flat.reshape(conv_rows, d), jnp.zeros((8 - bsz - conv_rows, d), F32)], axis=0)
    gath = all_gather8("gather_c", c_pad, True).reshape(N_DEV, 8, d)
    c_all = gath[:, :bsz].reshape(N_DEV * bsz, d)
    conv_all = gath[0::2, bsz:bsz + conv_rows].reshape(N_CHIP, conv_rows * d)
    gdn_conv_full, rg_conv_full = [
        _unshard(jnp.stack([_unpack(conv_all[s], conv_shapes)[i] for s in range(N_CHIP)]), 2) for i in range(2)]

    n_half = N_DEV * bsz // 2
    mod_cols = w_mod.shape[2]
    c_rows = lax.dynamic_slice(c_all, (n_half * mc, 0), (n_half, d))
    b_mod_mine = lax.dynamic_slice(b_mod, (0, chip * mod_cols), (depth, mod_cols)).reshape(depth, 1, mod_cols)
    mod_piece = mod_matmul(c_rows, w_mod, b_mod_mine)
    mod_g = all_gather8("gather_mod", mod_piece.reshape(depth * n_half, mod_cols), True)
    mod_all = mod_g.reshape(N_CHIP, 2, depth, n_half, mod_cols).transpose(2, 1, 3, 0, 4).reshape(depth, 2 * n_half, 6 * d)
    mod_mine = lax.dynamic_slice(mod_all, (0, bsz * dev, 0), (depth, bsz, 6 * d)).reshape(depth, bsz, 6, 1, d)

    ids = jnp.stack([chip, mc]).astype(jnp.int32)
    slabs = dict(zip(GATHER_FIRST, all_gather_weights(
        "gather_weights", [cast_into_slab("cast_" + n, wts[n], ids) for n in GATHER_FIRST])))

    def columns(g):
        return g.transpose(0, 2, 1, 3).reshape(g.shape[0], g.shape[2], N_CHIP * g.shape[3])

    def rows_of(g):
        return g.reshape(g.shape[0], N_CHIP * g.shape[2], g.shape[3])

    w_cat = _proj_cols(columns(slabs['w_in']))
    w_q = columns(slabs['mla_w_qb']).astype(F32)
    w_kv = columns(slabs['mla_w_kvb']).astype(F32)
    w_out_full = rows_of(slabs['w_out'])
    bd_a, bd_x = _block_diag(rg_w_a), _block_diag(rg_w_x)

    inv_freq = ROPE_THETA ** (-jnp.arange(0, 32, 2, dtype=F32) / 32.0)
    ang = positions.astype(F32)[..., None] * inv_freq
    cs = jnp.concatenate([jnp.cos(ang), jnp.sin(ang)], axis=-1)

    proj_ch = [w for _, w in PROJ_PIECES]

    def row(a, l):
        return a[l].reshape(1, -1)

    def layer_args(l):
        sh_m, sc_m, gt_m, sh_f, sc_f, gt_f = (mod_mine[l, :, k] for k in range(6))
        return dict(
            mods=(sh_m, sc_m, gt_m, sh_f, sc_f, gt_f),
            mixer_in=dict(ex=[sc_m, sh_m], par=[row(norm_mix_g, l)], big=[(w_cat, l)], out_ch=proj_ch, ts=512),
            gdn_conv=dict(par_tiled=[gdn_conv_full[l]], out_ch=[768], ts=seq, nc=3),
            gdn_local=dict(par=[row(gdn_a_log, l), row(gdn_dt_bias, l)], out_ch=[256] * 6, ts=512),
            rglru=dict(par_tiled=[rg_conv_full[l], row(rg_conv_b, l), row(rg_b_a, l), row(rg_b_x, l), row(rg_lambda, l),
                                  bd_a[l], bd_x[l]], out_ch=[512], ts=seq, nc=4),
            mla_pre=dict(tok_nd=[cs], par=[row(mla_q_norm_g, l), row(mla_kv_norm_g, l), w_q[l], w_kv[l]],
                         out_ch=[ATTN_QW, ATTN_QW, ATTN_VW], ts=512),
            out_proj=dict(ex=[gt_m], big=[(w_out_full, l)], out_ch=[d], ts=512),
            mlp_in=dict(ex=[sc_f, sh_f], par=[row(norm_mlp_g, l)], big=[w_mi.get(l)], out_ch=[4 * d], ts=256),
            mlp_out=dict(ex=[gt_f], big=[w_mo.get(l)], out_ch=[d], ts=256),
        )

    mi_buf = [cast_into_slab("cast_w_mlp_in%d" % l, wts['w_mlp_in'], ids, layer=l) for l in range(depth)]
    mo_buf = [cast_into_slab("cast_w_mlp_out%d" % l, wts['w_mlp_out'], ids, layer=l) for l in range(depth)]
    w_mi, w_mo = {}, {}

    def staged(name, fn, jobs, **kw):
        return run_stage(name, fn, side=jobs, **kw) if jobs else (run_stage(name, fn, **kw), [])

    saved = []
    h = x
    for l in range(depth):
        a = layer_args(l)
        sfx = str(l)
        first, more = l == 0, l + 1 < depth
        (qkv_raw, z, rx, rgate, mq, mkv, misc), bufs = staged(
            "mixer_in" + sfx, fn_mixer_in, [] if first else [gather_over_d2d(mo_buf[l])], tok=[h], **a['mixer_in'])
        if not first:
            w_mo[l] = bufs[0].reshape(N_CHIP * d, d)
        (qkv_act,) = run_stage("gdn_conv" + sfx, fn_gdn_conv, tok=[qkv_raw], **a['gdn_conv'])
        xs, bufs = staged("gdn_local" + sfx, fn_gdn_local, [gather_over_ici(mi_buf[l])] if first else [],
                          tok=[qkv_act, misc], **a['gdn_local'])
        if first:
            mi_buf[l] = bufs[0]
        o_a, st_in = gdn_scan(xs, z, row(gdn_norm_g, l))
        (o_b,), bufs = staged("rglru" + sfx, fn_rglru, [gather_over_d2d(mi_buf[l]), gather_over_ici(mo_buf[l])] if first else [],
                              tok=[rx, rgate], **a['rglru'])
        if first:
            w_mi[l], mo_buf[l] = bufs
        q_at, k_at, v_at = run_stage("mla_pre" + sfx, fn_mla_pre, tok=[mq, mkv, misc], **a['mla_pre'])
        o_c = mla_attention(q_at, k_at, v_at)
        (h_mid,), bufs = staged("out_proj" + sfx, fn_out_proj, [gather_over_d2d(mo_buf[l])] if first else [],
                                tok=[h, o_a, o_b, o_c], **a['out_proj'])
        if first:
            w_mo[l] = bufs[0].reshape(N_CHIP * d, d)
        a = layer_args(l)
        (a_mlp,), bufs = staged("mlp_in" + sfx, fn_mlp_in, [gather_over_ici(mi_buf[l + 1])] if more else [],
                                tok=[h_mid], **a['mlp_in'])
        if more:
            mi_buf[l + 1] = bufs[0]
        (h_out,), bufs = staged("mlp_out" + sfx, fn_mlp_out,
                                [gather_over_d2d(mi_buf[l + 1]), gather_over_ici(mo_buf[l + 1])] if more else [],
                                tok=[h_mid, a_mlp], **a['mlp_out'])
        if more:
            w_mi[l + 1], mo_buf[l + 1] = bufs
        saved.append(dict(h=h, qkv_raw=qkv_raw, z=z, rx=rx, rgate=rgate, mq=mq, mkv=mkv, misc=misc, qkv_act=qkv_act, xs=xs,
                          st_in=st_in, o_a=o_a, o_b=o_b, o_c=o_c, q_at=q_at, k_at=k_at, v_at=v_at, h_mid=h_mid, a_mlp=a_mlp))
        h = h_out

    loss_part, dh, d_final_g = loss_head(h, final_norm_g.reshape(1, d), loss_target)
    loss = lax.psum(loss_part[0, 0], ("x", "y", "c"))

    g_full = {n: [None] * depth for n in SHARDED}
    g_rep = {n: [None] * depth for n in REPLICATED if n not in ('final_norm_g', 'b_mod')}

    def column_slabs(g):
        return g.reshape(g.shape[0], N_CHIP, g.shape[1] // N_CHIP).transpose(1, 0, 2)

    def row_slabs(g):
        return g.reshape(N_CHIP, g.shape[0] // N_CHIP, g.shape[1])
    core_id = mc.reshape(1).astype(jnp.int32)
    shards = [None] * len(GATHER_BF16)
    mixer_units, mlp_in_unit, mlp_out_unit = [0, 1, 2, 3], [4], [5]

    def reduce_begin(tag, idxs, l):
        gs = [g_full[GATHER_BF16[i]][l] for i in idxs]
        from_sibling = grad_sibling_exchange("grad_sibling_exchange_" + tag, gs)
        pairs = [add_half("grad_add_%s%d" % (GATHER_BF16[i], l), g, s, core_id) for i, g, s in zip(idxs, gs, from_sibling)]
        return [p[0] for p in pairs], [p[1] for p in pairs]

    def reduce_end(idxs, l, sums32, landed):
        for i, p, r in zip(idxs, sums32, landed):
            n = GATHER_BF16[i]
            shards[i] = sum_peers("grad_sum_%s%d" % (n, l), p, r, ids, wts[n].shape, l, acc=shards[i])

    def staged_bwd(name, fn, idxs, l_units, pair, **kw):
        if pair is None:
            return run_stage(name, fn, **kw)
        groups, bufs = run_stage(name, fn, side=[chip_exchange_job(pair[1])], **kw)
        reduce_end(idxs, l_units, pair[0], bufs[len(idxs):])
        return groups

    dmod = [None] * depth
    carried = None
    for l in reversed(range(depth)):
        a, sv = layer_args(l), saved[l]
        sfx = str(l)
        mlp_out_tok = dict(tok=[sv['h_mid'], sv['a_mlp']], cot=[dh])
        (dh_mid, da_mlp), (dgt_f,), _, _, _ = staged_bwd(
            "mlp_out" + sfx, fn_mlp_out, mixer_units, l + 1, carried, which="small", dtok_dtype={1: BF16}, **mlp_out_tok,
            **{**a['mlp_out'], 'ts': 256})
        _, _, _, _, (dw_mlp_out,) = run_stage(
            "mlp_out" + sfx, fn_mlp_out, which="big", **mlp_out_tok, **{**a['mlp_out'], 'ts': 512})
        g_full['w_mlp_out'][l] = row_slabs(dw_mlp_out)
        _, _, _, _, (g_full['w_mlp_in'][l],) = run_stage(
            "mlp_in" + sfx, fn_mlp_in, tok=[sv['h_mid']], cot=[da_mlp], which="big", **{**a['mlp_in'], 'ts': 512})
        (dh_mid,), (dsc_f, dsh_f), (g_rep['norm_mlp_g'][l],), _, _ = run_stage(
            "mlp_in" + sfx, fn_mlp_in, tok=[sv['h_mid']], cot=[da_mlp], addin=dh_mid, which="small", **a['mlp_in'])
        mlp_sums32, mlp_sums16 = reduce_begin("mlp" + sfx, mlp_in_unit + mlp_out_unit, l)
        (dh_in, do_a, do_b, do_c), (dgt_m,), _, _, (dw_out,) = run_stage(
            "out_proj" + sfx, fn_out_proj, tok=[sv['h'], sv['o_a'], sv['o_b'], sv['o_c']], cot=[dh_mid], **a['out_proj'])
        g_full['w_out'][l] = row_slabs(dw_out)
        attn_cot = mla_attention_bwd(sv['q_at'], sv['k_at'], sv['v_at'], do_c)
        (dmq, dmkv, dmisc_c), _, (g_rep['mla_q_norm_g'][l], g_rep['mla_kv_norm_g'][l], dw_q, dw_kv), _, _ = run_stage(
            "mla_pre" + sfx, fn_mla_pre, tok=[sv['mq'], sv['mkv'], sv['misc']], cot=attn_cot, **a['mla_pre'])
        g_full['mla_w_qb'][l] = column_slabs(dw_q)
        g_full['mla_w_kvb'][l] = column_slabs(dw_kv)
        (drx, drgate), _, _, rg_g, _ = staged_bwd(
            "rglru" + sfx, fn_rglru, mlp_in_unit, l, (mlp_sums32[:1], mlp_sums16[:1]), tok=[sv['rx'], sv['rgate']], cot=[do_b],
            **a['rglru'])
        (g_full['rg_conv_w'][l], g_rep['rg_conv_b'][l], g_rep['rg_b_a'][l], g_rep['rg_b_x'][l], g_rep['rg_lambda'][l],
         g_rep['rg_w_a'][l], g_rep['rg_w_x'][l]) = rg_g
        dxs, dz, g_rep['gdn_norm_g'][l] = gdn_scan_bwd(sv['xs'], sv['z'], row(gdn_norm_g, l), sv['st_in'], do_a)
        (dqkv_act, dmisc_a), _, (g_rep['gdn_a_log'][l], g_rep['gdn_dt_bias'][l]), _, _ = staged_bwd(
            "gdn_local" + sfx, fn_gdn_local, mlp_out_unit, l, (mlp_sums32[1:], mlp_sums16[1:]),
            tok=[sv['qkv_act'], sv['misc']], cot=dxs, **a['gdn_local'])
        (dqkv_raw,), _, _, (g_full['gdn_conv_w'][l],), _ = run_stage(
            "gdn_conv" + sfx, fn_gdn_conv, tok=[sv['qkv_raw']], cot=[dqkv_act], **a['gdn_conv'])
        (dh,), (dsc_m, dsh_m), (g_rep['norm_mix_g'][l],), _, (dw_cat,) = run_stage(
            "mixer_in" + sfx, fn_mixer_in, tok=[sv['h']], cot=[dqkv_raw, dz, drx, drgate, dmq, dmkv, dmisc_a + dmisc_c],
            addin=dh_in, **a['mixer_in'])
        g_full['w_in'][l] = column_slabs(_proj_cols_back(dw_cat))
        dmod[l] = jnp.concatenate([dsh_m, dsc_m, dgt_m, dsh_f, dsc_f, dgt_f], axis=-1).reshape(bsz, 6 * d)
        carried = reduce_begin("mixer" + sfx, mixer_units, l)
    grad_x = dh
    reduce_end(mixer_units, 0, carried[0], grad_chip_exchange("grad_chip_exchange", carried[1]))

    dmod = jnp.stack(dmod)
    dmod_pad = jnp.concatenate([dmod.reshape(depth * bsz, 6 * d), jnp.zeros((8 - depth * bsz, 6 * d), F32)], axis=0)
    dmod_all = all_gather8("gather_dmod", dmod_pad, True).reshape(N_DEV, 8, 6 * d)[:, :depth * bsz]
    dmod_all = dmod_all.reshape(N_DEV, depth, bsz, 6 * d).transpose(1, 0, 2, 3).reshape(depth, N_DEV * bsz, 6 * d)
    g_w_mod = mod_weight_grad(c_all, lax.dynamic_slice(dmod_all, (0, 0, chip * mod_cols), (depth, N_DEV * bsz, mod_cols)))

    g_rep = {n: jnp.stack(v) for n, v in g_rep.items()}
    g_rep['rg_w_a'] = _block_diag_back(g_rep['rg_w_a'])
    g_rep['rg_w_x'] = _block_diag_back(g_rep['rg_w_x'])
    g_rep['final_norm_g'] = d_final_g
    g_rep['b_mod'] = jnp.sum(dmod, axis=1)
    conv_names = ['gdn_conv_w', 'rg_conv_w']
    conv_full_shapes = [(depth,) + g_full[n][0].shape for n in conv_names]
    small_shapes = [wts[n].shape for n in REPLICATED] + conv_full_shapes
    rep_part = _pack_rows([g_rep[n].reshape(wts[n].shape) for n in REPLICATED] + [jnp.stack(g_full[n]) for n in conv_names])
    rep_rows = rep_part.shape[0]
    rep_all = all_gather8("gather_small_grads", rep_part, True).reshape(N_DEV, rep_rows, PACK_COLS)
    conv_zeros = [jnp.zeros(s, F32) for s in conv_full_shapes]
    rep_out = adamw_reduce("adamw_small", rep_all, *[
        _pack_rows([src[n] for n in REPLICATED] + conv_zeros) for src in (wts, mom_m, mom_v)])
    small_names = REPLICATED + conv_names
    rep_g, rep_d, rep_m, rep_v = [dict(zip(small_names, _unpack_rows(o, small_shapes))) for o in rep_out]
    sh_g = {}
    for n in conv_names:
        cols = wts[n].shape[2]
        sh_g[n] = lax.dynamic_slice(rep_g.pop(n), (0, 0, chip * cols), wts[n].shape)
        for dct in (rep_d, rep_m, rep_v):
            dct.pop(n)

    sh_g.update(zip(GATHER_BF16, grad_half_exchange("grad_half_exchange", shards)))
    sh_names = list(SHARDED)

    def as2d(t):
        return t.reshape(-1, t.shape[-1])

    sh_d, sh_m, sh_v = {}, {}, {}
    for n in sh_names + ['w_mod']:
        g = g_w_mod if n == 'w_mod' else sh_g[n]
        res = adamw("adamw_" + n, as2d(wts[n]), as2d(g), as2d(mom_m[n]), as2d(mom_v[n]))
        sh_d[n], sh_m[n], sh_v[n] = (r.reshape(wts[n].shape) for r in res)
    sh_g['w_mod'] = g_w_mod

    def pick(shd, rep):
        return [shd[n] if n in shd else rep[n] for n in WEIGHTS]

    return (loss, grad_x, *pick(sh_g, rep_g), *pick(sh_d, rep_d), *pick(sh_m, rep_m), *pick(sh_v, rep_v))
```

```python
import functools

import jax
import jax.numpy as jnp
import numpy as np
from jax import lax
from jax.experimental import pallas as pl
from jax.experimental.pallas import tpu as pltpu

F32, BF16 = jnp.float32, jnp.bfloat16
HI = lax.Precision.HIGH
MESH = pl.DeviceIdType.MESH

EPS = 1e-6
CHUNK = 64
GDN_HEADS = 4
MLA_HEADS = 4
RG_C = 8.0
ROPE_THETA = 10000.0
N_DEV = 8
N_CHIP = 4
V7X_VMEM_LIMIT = 60 * 1024 * 1024
ADAM_LR, ADAM_B1, ADAM_B2, ADAM_EPS, ADAM_WD, ADAM_STEP = 0.001, 0.9, 0.999, 1e-08, 0.01, 10


def _params(n_grid):
    return pltpu.CompilerParams(dimension_semantics=("arbitrary",) * n_grid, vmem_limit_bytes=V7X_VMEM_LIMIT)


def _dot(a, b, dims=(((1,), (0,)), ((), ()))):
    return lax.dot_general(a.astype(BF16), b.astype(BF16), dims, preferred_element_type=F32)


@jax.custom_vjp
def _mm_probe(x, w, probe):
    return _dot(x, w)


def _mm_probe_fwd(x, w, probe):
    return _dot(x, w), (x, w)


def _mm_probe_bwd(res, dy):
    x, w = res
    dx = _dot(dy, w, (((1,), (1,)), ((), ())))
    dw = _dot(x, dy, (((0,), (0,)), ((), ())))
    return dx, jnp.zeros_like(w), dw


_mm_probe.defvjp(_mm_probe_fwd, _mm_probe_bwd)


@jax.custom_vjp
def _probe_only(x, probe):
    return jnp.zeros((x.shape[0], probe.shape[1]), F32)


def _probe_only_fwd(x, probe):
    return jnp.zeros((x.shape[0], probe.shape[1]), F32), x


def _probe_only_bwd(x, dy):
    return jnp.zeros_like(x), _dot(x, dy, (((0,), (0,)), ((), ())))


_probe_only.defvjp(_probe_only_fwd, _probe_only_bwd)


@jax.custom_vjp
def mmw(x, w):
    return _dot(x, w)


def _mmw_fwd(x, w):
    return _dot(x, w), (x, w)


def _mmw_bwd(res, dy):
    x, w = res
    return _dot(dy, w, (((1,), (1,)), ((), ()))), _dot(x, dy, (((0,), (0,)), ((), ())))


mmw.defvjp(_mmw_fwd, _mmw_bwd)


def rms(x, g):
    return x * lax.rsqrt(jnp.mean(x * x, axis=-1, keepdims=True) + EPS) * g


def _rows(shape):
    return lax.broadcasted_iota(jnp.int32, shape, 0)


def _shift_down(x, s, fill):
    return jnp.where(_rows(x.shape) < s, fill, pltpu.roll(x, s, 0))


def _shift_up(x, s, fill):
    n = x.shape[0]
    return jnp.where(_rows(x.shape) >= n - s, fill, pltpu.roll(x, n - s, 0))


def _make_tshift(s):
    @jax.custom_vjp
    def tshift(x):
        return _shift_down(x, s, 0.0)

    tshift.defvjp(lambda x: (_shift_down(x, s, 0.0), None), lambda _, dy: (_shift_up(dy, s, 0.0),))
    return tshift


_TSHIFT = {s: _make_tshift(s) for s in (1, 2, 3)}


def causal_conv4(x, w):
    y = x * w[3:4, :]
    for j in range(3):
        y = y + _TSHIFT[3 - j](x) * w[j:j + 1, :]
    return y


def _scan_steps(n):
    d = 1
    while d < n:
        yield d
        d *= 2


@jax.custom_vjp
def linscan(a, b):
    return _linscan_fwd_impl(a, b)


def _linscan_fwd_impl(a, b):
    for d in _scan_steps(a.shape[0]):
        b = a * _shift_down(b, d, 0.0) + b
        a = a * _shift_down(a, d, 1.0)
    return b


def _linscan_fwd(a, b):
    h = _linscan_fwd_impl(a, b)
    return h, (a, h)


def _linscan_bwd(res, dh):
    a, h = res
    an = _shift_up(a, 1, 0.0)
    lam = dh
    for d in _scan_steps(a.shape[0]):
        lam = an * _shift_up(lam, d, 0.0) + lam
        an = an * _shift_up(an, d, 1.0)
    return lam * _shift_down(h, 1, 0.0), lam


linscan.defvjp(_linscan_fwd, _linscan_bwd)


def _chunk_scan(x, reverse):
    pos = _rows(x.shape) % CHUNK
    n = x.shape[0]
    d = 1
    while d < CHUNK:
        if reverse:
            x = x + jnp.where(pos < CHUNK - d, pltpu.roll(x, n - d, 0), 0.0)
        else:
            x = x + jnp.where(pos >= d, pltpu.roll(x, d, 0), 0.0)
        d *= 2
    return x


@jax.custom_vjp
def chunk_cumsum(x):
    return _chunk_scan(x, False)


@jax.custom_vjp
def chunk_revcumsum(x):
    return _chunk_scan(x, True)


chunk_cumsum.defvjp(lambda x: (_chunk_scan(x, False), None), lambda _, g: (_chunk_scan(g, True),))
chunk_revcumsum.defvjp(lambda x: (_chunk_scan(x, True), None), lambda _, g: (_chunk_scan(g, False),))


def _bmm(a, b, precision=None):
    return jnp.einsum('nij,njk->nik', a, b, precision=precision, preferred_element_type=F32)


@jax.custom_vjp
def inv_unit_lower(l):
    return _inv_impl(l)


def _inv_impl(l):
    n = l.shape[-1]
    eye = (_rows((n, n)) == lax.broadcasted_iota(jnp.int32, (n, n), 1)).astype(F32)
    p = -l
    a = eye + p
    k = 1
    while 2 * k < n:
        p = _bmm(p, p, HI)
        a = a + _bmm(a, p, HI)
        k *= 2
    return a


def _inv_fwd(l):
    a = _inv_impl(l)
    return a, a


def _inv_bwd(a, da):
    at = jnp.swapaxes(a, 1, 2)
    return (-_bmm(_bmm(at, da, HI), at, HI),)


inv_unit_lower.defvjp(_inv_fwd, _inv_bwd)


@jax.custom_vjp
def inv_unit_lower_known(l, a):
    return a


inv_unit_lower_known.defvjp(lambda l, a: (a, a), lambda a, da: (_inv_bwd(a, da)[0], jnp.zeros_like(a)))


def neg_expm1(y):
    series = -(y * (1.0 + y * (0.5 + y * (1.0 / 6.0 + y * (1.0 / 24.0)))))
    return jnp.where(y > -0.05, series, 1.0 - jnp.exp(y))


def run_stage(name, fn, *, tok, tok_nd=(), ex=(), par=(), par_tiled=(), big=(), out_ch, ts, nc=1, cot=None, addin=None,
              which="all", dtok_dtype=None, side=None):
    tok, tok_nd, ex, par, par_tiled, big = map(list, (tok, tok_nd, ex, par, par_tiled, big))
    big_layer = [b[1] if isinstance(b, tuple) else None for b in big]
    big_arrays = [b[0] if isinstance(b, tuple) else b for b in big]
    big = [jax.ShapeDtypeStruct(a.shape if lyr is None else a.shape[1:], a.dtype) for a, lyr in zip(big_arrays, big_layer)]
    bsz, seq, _ = tok[0].shape
    ts = min(ts, seq)
    ns = seq // ts
    grid = (nc, bsz, ns)

    def tok_spec(a):
        cb = a.shape[-1] // nc
        return pl.BlockSpec((None, ts, cb), lambda c, b, s: (b, s, c))

    def ex_spec(a):
        cb = a.shape[-1] // nc
        return pl.BlockSpec((None, 1, cb), lambda c, b, s: (b, 0, c))

    def full_spec(a, single=False):
        nd = a.ndim
        kw = dict(pipeline_mode=pl.Buffered(1)) if single else {}
        return pl.BlockSpec(a.shape, lambda c, b, s: (0,) * nd, **kw)

    def tiled_spec(a):
        if a.ndim == 2:
            return pl.BlockSpec((a.shape[0], a.shape[1] // nc), lambda c, b, s: (0, c))
        return pl.BlockSpec((None,) + a.shape[1:], lambda c, b, s: (c, 0, 0))

    def big_spec(a, lyr):
        if lyr is None:
            return full_spec(a, True)
        nd = a.ndim
        return pl.BlockSpec((None,) + a.shape[1:], lambda c, b, s: (lyr,) + (0,) * (nd - 1), pipeline_mode=pl.Buffered(1))

    n_tok, n_nd, n_ex, n_par, n_pt, n_big = map(len, (tok, tok_nd, ex, par, par_tiled, big))
    in_arrays = tok + tok_nd + ex + par + par_tiled + big_arrays
    in_specs = ([tok_spec(a) for a in tok + tok_nd] + [ex_spec(a) for a in ex] + [full_spec(a) for a in par]
                + [tiled_spec(a) for a in par_tiled] + [big_spec(a, lyr) for a, lyr in zip(big_arrays, big_layer)])
    out_tok_shapes = [jax.ShapeDtypeStruct((bsz, seq, ch), F32) for ch in out_ch]
    n_in = len(in_arrays)

    def split(vals):
        i = 0
        groups = []
        for n in (n_tok, n_nd, n_ex, n_par, n_pt, n_big):
            groups.append(list(vals[i:i + n]))
            i += n
        return groups

    def split_grads(vals):
        i = 0
        groups = []
        for n in (n_tok, n_ex, n_par, n_pt, n_big):
            groups.append(list(vals[i:i + n]))
            i += n
        return groups

    side = list(side or [])
    side_arrays = [a for job in side for a in job[0]]
    n_side = len(side_arrays)
    side_shapes = [jax.ShapeDtypeStruct(a.shape, a.dtype) for a in side_arrays]
    side_scratch = [pltpu.SemaphoreType.DMA((job[1],)) for job in side for _ in range(2)]

    def side_jobs(side_refs, sems):
        o = 0
        for j, (arrs, _, issue) in enumerate(side):
            yield issue(side_refs[o:o + len(arrs)], sems[2 * j], sems[2 * j + 1])
            o += len(arrs)

    def side_start(side_refs, sems):
        if side:
            c, b, s = pl.program_id(0), pl.program_id(1), pl.program_id(2)

            @pl.when(jnp.logical_and(jnp.logical_and(c == 0, b == 0), s == 0))
            def _():
                for starts, _, _ in side_jobs(side_refs, sems):
                    for cp in starts:
                        cp.start()

    def side_finish(side_refs, sems):
        if side:
            c, b, s = pl.program_id(0), pl.program_id(1), pl.program_id(2)

            @pl.when(jnp.logical_and(jnp.logical_and(c == nc - 1, b == bsz - 1), s == ns - 1))
            def _():
                for _, recv_waits, send_waits in side_jobs(side_refs, sems):
                    for cp in recv_waits:
                        cp.wait_recv()
                    for cp in send_waits:
                        cp.wait_send()

    if cot is None:
        n_out = len(out_tok_shapes)

        def body(*refs):
            tv, ndv, ev, pv, ptv, _ = split([r[...] for r in refs[:n_in - n_big]] + [None] * n_big)
            b_refs = refs[n_in - n_big:n_in]
            side_refs = refs[n_in + n_side + n_out:n_in + 2 * n_side + n_out]
            sems = refs[n_in + 2 * n_side + n_out:]
            side_start(side_refs, sems)
            outs = fn(tv, ndv, ev, pv, ptv, lambda x, i, j=None: _dot(x, b_refs[i][...] if j is None else b_refs[i][j]))
            for r, o in zip(refs[n_in + n_side:], outs):
                r[...] = o
            side_finish(side_refs, sems)

        res = pl.pallas_call(
            body, name=name, grid=grid, in_specs=in_specs + [_ANY] * n_side,
            out_specs=[tok_spec(a) for a in out_tok_shapes] + [_ANY] * n_side,
            out_shape=out_tok_shapes + side_shapes, input_output_aliases={n_in + j: n_out + j for j in range(n_side)},
            scratch_shapes=side_scratch, compiler_params=_params(3))(*in_arrays, *side_arrays)
        return (res[:n_out], res[n_out:]) if side else res

    cot = list(cot)
    has_addin = addin is not None
    extra = cot + ([addin] if has_addin else [])
    n_cot = len(cot)
    want_small, want_big = which in ("all", "small"), which in ("all", "big")
    if not want_small:
        in_arrays, in_specs, n_in = in_arrays[:n_in - n_big], in_specs[:n_in - n_big], n_in - n_big
    small_arrays = tok + ex + par + par_tiled
    g_shapes = [jax.ShapeDtypeStruct(a.shape, F32) for a in (small_arrays if want_small else []) + (big if want_big else [])]
    for i, dt_ in (dtok_dtype or {}).items():
        g_shapes[i] = jax.ShapeDtypeStruct(g_shapes[i].shape, dt_)
    g_specs = (([tok_spec(a) for a in tok] + [ex_spec(a) for a in ex] + [full_spec(a) for a in par]
                + [tiled_spec(a) for a in par_tiled]) if want_small else []) + (
                    [full_spec(a, True) for a in big] if want_big else [])

    def body(*refs):
        c, b, s = pl.program_id(0), pl.program_id(1), pl.program_id(2)
        n_small_in = n_tok + n_nd + n_ex + n_par + n_pt
        tv, ndv, ev, pv, ptv, _ = split([r[...] for r in refs[:n_small_in]] + [None] * n_big)
        b_refs = refs[n_small_in:n_in]
        cots = [r[...].astype(F32) for r in refs[n_in:n_in + n_cot]]
        n_g = len(g_shapes)
        g_refs = list(refs[n_in + len(extra) + n_side:n_in + len(extra) + n_side + n_g])
        side_refs = refs[n_in + len(extra) + n_side + n_g:n_in + len(extra) + 2 * n_side + n_g]
        sems = refs[n_in + len(extra) + 2 * n_side + n_g:]
        side_start(side_refs, sems)
        probes = [jnp.zeros(w.shape, F32) if w.ndim == 2 else [jnp.zeros(w.shape[1:], F32) for _ in range(w.shape[0])]
                  for w in big]

        def f(tv_, ev_, pv_, ptv_, probes_):
            def mm(x, i, j=None):
                probe = None if probes_ is None else (probes_[i] if j is None else probes_[i][j])
                if not want_small:
                    return _probe_only(x, probe)
                w = b_refs[i][...] if j is None else b_refs[i][j]
                return _dot(x, w) if probe is None else _mm_probe(x, w, probe)

            return fn(tv_, ndv, ev_, pv_, ptv_, mm)

        dt = de = dp = dpt = dbg = ()
        if which == "all":
            dt, de, dp, dpt, dbg = jax.vjp(f, tv, ev, pv, ptv, probes)[1](cots)
        elif which == "small":
            dt, de, dp, dpt = jax.vjp(lambda *a: f(*a, None), tv, ev, pv, ptv)[1](cots)
        else:
            (dbg,) = jax.vjp(lambda p: f(tv, ev, pv, ptv, p), probes)[1](cots)
        if has_addin:
            dt = [dt[0] + refs[n_in + n_cot][...]] + list(dt[1:])
        if want_small:
            gt_r, ge_r, gp_r, gpt_r, gb_r = split_grads(g_refs + ([] if want_big else [None] * n_big))
        else:
            gt_r, ge_r, gp_r, gpt_r, gb_r = [], [], [], [], g_refs
        for r, g in zip(gt_r, dt):
            r[...] = g.astype(r.dtype)

        def accumulate(r, g, first):
            @pl.when(first)
            def _():
                r[...] = g

            @pl.when(jnp.logical_not(first))
            def _():
                r[...] += g

        for r, g in zip(ge_r, de):
            accumulate(r, g, s == 0)
        first_all = jnp.logical_and(jnp.logical_and(c == 0, b == 0), s == 0)
        for r, g in zip(gp_r, dp):
            accumulate(r, g, first_all)
        for r, g in zip(gpt_r, dpt):
            accumulate(r, g, jnp.logical_and(b == 0, s == 0))
        for r, g in zip(gb_r, dbg):
            if isinstance(g, (list, tuple)):
                for j, gj in enumerate(g):
                    accumulate(r.at[j], gj, first_all)
            else:
                accumulate(r, g, first_all)
        side_finish(side_refs, sems)

    n_args = n_in + len(extra)
    res = pl.pallas_call(
        body, name=name + "_bwd" + ("" if which == "all" else "_" + which), grid=grid,
        in_specs=in_specs + [tok_spec(a) for a in extra] + [_ANY] * n_side, out_specs=g_specs + [_ANY] * n_side,
        out_shape=g_shapes + side_shapes, input_output_aliases={n_args + j: len(g_shapes) + j for j in range(n_side)},
        scratch_shapes=side_scratch, compiler_params=_params(3))(*in_arrays, *extra, *side_arrays)
    res, side_out = list(res[:len(g_shapes)]), list(res[len(g_shapes):])
    groups = [[], [], [], [], res] if not want_small else split_grads(res + ([] if want_big else [None] * n_big))
    return (groups, side_out) if side else groups


PROJ_PIECES = (("qkv", 768), ("z", 256), ("rx", 512), ("rgate", 512), ("mq", 256), ("mkv", 128), ("misc", 128))
PROJ_WIDTH = sum(w for _, w in PROJ_PIECES)
MISC_KR, MISC_A, MISC_B = 0, 32, 36


def fn_mixer_in(tok, nd, ex, par, pt, mm):
    (h,), (sc, sh), (g,) = tok, ex, par
    proj = mm(rms(h, g) * (1.0 + sc) + sh, 0)
    outs, o = [], 0
    for _, w in PROJ_PIECES:
        outs.append(proj[:, o:o + w])
        o += w
    return outs


def fn_gdn_conv(tok, nd, ex, par, pt, mm):
    return [jax.nn.silu(causal_conv4(tok[0], pt[0]))]


def _tri_masks():
    r = _rows((CHUNK, CHUNK))
    c = lax.broadcasted_iota(jnp.int32, (CHUNK, CHUNK), 1)
    return (c <= r), (c < r)


def fn_gdn_local(tok, nd, ex, par, pt, mm):
    (qkv, misc), (a_log, dt_bias) = tok, par
    known = nd[0] if nd else None
    ts = qkv.shape[0]
    nb = ts // CHUNK
    lower, strict = _tri_masks()
    g_all = -jnp.exp(a_log) * jax.nn.softplus(misc[:, MISC_A:MISC_A + GDN_HEADS] + dt_bias)
    g_cum = chunk_cumsum(g_all)
    g_tot = g_cum + chunk_revcumsum(g_all) - g_all
    outs = [[] for _ in range(7)]
    for hd in range(GDN_HEADS):
        def head(x, base):
            return x[:, base + 64 * hd: base + 64 * hd + 64]

        def l2n(x):
            return x * lax.rsqrt(jnp.sum(x * x, axis=-1, keepdims=True) + EPS)

        q = (l2n(head(qkv, 0)) * (64.0 ** -0.5)).reshape(nb, CHUNK, 64)
        k = l2n(head(qkv, 256)).reshape(nb, CHUNK, 64)
        v = head(qkv, 512).reshape(nb, CHUNK, 64)
        b = misc[:, MISC_B + hd: MISC_B + hd + 1]
        beta = jax.nn.sigmoid(b).reshape(nb, CHUNK, 1)
        gi = jnp.broadcast_to(g_cum[:, hd:hd + 1].reshape(nb, CHUNK, 1), (nb, CHUNK, CHUNK))
        gl = jnp.broadcast_to(g_tot[:, hd:hd + 1].reshape(nb, CHUNK, 1), (nb, CHUNK, CHUNK))
        diff = gi - jnp.swapaxes(gi, 1, 2)
        decay = jnp.where(lower, jnp.exp(jnp.where(lower, diff, 0.0)), 0.0)
        kb = k * beta
        vb = v * beta
        kk = jnp.einsum('ncd,nsd->ncs', kb.astype(BF16), k.astype(BF16), preferred_element_type=F32)
        lmat = jnp.where(strict, kk * decay, 0.0)
        if known is None:
            amat = inv_unit_lower(lmat)
        else:
            amat = inv_unit_lower_known(lmat, known[:, 64 * hd: 64 * hd + 64].reshape(nb, CHUNK, 64))
        eg = jnp.exp(gi)
        u = _bmm(amat, vb, HI)
        w = _bmm(amat, kb * eg, HI)
        qk = jnp.einsum('ncd,nsd->ncs', q.astype(BF16), k.astype(BF16), preferred_element_type=F32) * decay
        qd = q * eg
        kt = k * jnp.exp(gl - gi)
        cd = jnp.exp(gl)
        for lst, val in zip(outs, (qk, qd, u, w, kt, cd) + (() if known is not None else (amat,))):
            lst.append(val.reshape(ts, 64))
    return [jnp.concatenate(lst, axis=-1) for lst in outs if lst]


def fn_rglru(tok, nd, ex, par, pt, mm):
    (rx, rgate), (conv_w, conv_b, b_a, b_x, lam, bd_a, bd_x) = tok, pt
    xc = causal_conv4(rx, conv_w) + conv_b
    r = jax.nn.sigmoid(mmw(xc, bd_a) + b_a)
    i = jax.nn.sigmoid(mmw(xc, bd_x) + b_x)
    log_a = -RG_C * r * jax.nn.softplus(-lam)
    a = jnp.exp(log_a)
    bterm = jnp.sqrt(neg_expm1(2.0 * log_a)) * (i * xc)
    return [linscan(a, bterm) * jax.nn.gelu(rgate)]


def _rope32(x, cos, sin):
    x1, x2 = x[:, :16], x[:, 16:32]
    return jnp.concatenate([x1 * cos - x2 * sin, x2 * cos + x1 * sin], axis=-1)


MLA_QK = 96


def fn_mla_pre(tok, nd, ex, par, pt, mm):
    (mq, mkv, misc), (cs,), (g_q, g_kv, w_q, w_kv) = tok, nd, par
    q = mmw(rms(mq, g_q), w_q)
    kv = mmw(rms(mkv, g_kv), w_kv)
    cos, sin = cs[:, 0:16], cs[:, 16:32]
    kp = _rope32(misc[:, MISC_KR:MISC_KR + 32], cos, sin)
    qs, ks, vs = [], [], []
    for h in range(MLA_HEADS):
        qs += [q[:, MLA_QK * h: MLA_QK * h + 64], _rope32(q[:, MLA_QK * h + 64: MLA_QK * h + 96], cos, sin)]
        ks += [kv[:, 128 * h: 128 * h + 64], kp]
        vs.append(kv[:, 128 * h + 64: 128 * h + 128])
    return [jnp.concatenate(qs, axis=-1), jnp.concatenate(ks, axis=-1), jnp.concatenate(vs, axis=-1)]


def fn_out_proj(tok, nd, ex, par, pt, mm):
    (h, o_a, o_b, o_c), (gt,) = tok, ex
    return [h + gt * mm(jnp.concatenate([o_a, o_b, o_c], axis=-1), 0)]


def fn_mlp_in(tok, nd, ex, par, pt, mm):
    (h,), (sc, sh), (g,) = tok, ex, par
    u = rms(h, g) * (1.0 + sc) + sh
    return [jnp.concatenate([mm(u, 0, j) for j in range(N_CHIP)], axis=-1)]


def fn_mlp_out(tok, nd, ex, par, pt, mm):
    (h, a), (gt,) = tok, ex
    return [h + gt * mm(jnp.square(jax.nn.relu(a)), 0)]


GDN_W = GDN_HEADS * 64


def _head_mask():
    r = _rows((GDN_W, GDN_W)) // 64
    c = lax.broadcasted_iota(jnp.int32, (GDN_W, GDN_W), 1) // 64
    return r == c


def _heads_diag(x):
    return jnp.where(_head_mask(), jnp.concatenate([x] * GDN_HEADS, axis=0), 0.0)


def _heads_compact(s):
    return s[0:64] + s[64:128] + s[128:192] + s[192:256]


def _gdn_step(state, qk, qd, u, w, kt, cd, z, norm_g):
    v_new = u - _dot(w, state)
    o = _dot(qd, state) + _dot(qk, _heads_diag(v_new))
    update = _dot(kt, v_new, (((0,), (0,)), ((), ())))
    new_state = state * jnp.broadcast_to(cd[0:1, :], (GDN_W, GDN_W)) + jnp.where(_head_mask(), update, 0.0)
    outs = [rms(o[:, 64 * hd: 64 * hd + 64], norm_g) * jax.nn.silu(z[:, 64 * hd: 64 * hd + 64]) for hd in range(GDN_HEADS)]
    return new_state, jnp.concatenate(outs, axis=-1)


def gdn_scan(xs, z, norm_g):
    bsz, seq, _ = z.shape
    n = seq // CHUNK
    blk = pl.BlockSpec((bsz, CHUNK, 256), lambda i: (0, i, 0))

    def body(qk, qd, u, w, kt, cd, z_ref, g_ref, o_ref, st_out, st):
        @pl.when(pl.program_id(0) == 0)
        def _():
            st[...] = jnp.zeros_like(st)

        for b in range(bsz):
            state = st[b]
            st_out[b] = _heads_compact(state)
            st[b], o_ref[b] = _gdn_step(state, qk[b], qd[b], u[b], w[b], kt[b], cd[b], z_ref[b], g_ref[...])

    return pl.pallas_call(
        body, name="gdn_scan", grid=(n,), in_specs=[blk] * 7 + [pl.BlockSpec((1, 64), lambda i: (0, 0))],
        out_specs=[blk, blk], out_shape=[jax.ShapeDtypeStruct((bsz, seq, 256), F32)] * 2,
        scratch_shapes=[pltpu.VMEM((bsz, GDN_W, GDN_W), F32)], compiler_params=_params(1))(*xs, z, norm_g)


def gdn_scan_bwd(xs, z, norm_g, st_in, do):
    bsz, seq, _ = z.shape
    n = seq // CHUNK
    blk = pl.BlockSpec((bsz, CHUNK, 256), lambda i: (0, n - 1 - i, 0))
    gspec = pl.BlockSpec((1, 64), lambda i: (0, 0))

    def body(qk, qd, u, w, kt, cd, z_ref, g_ref, st_ref, do_ref, dqk, dqd, du, dw, dkt, dcd, dz, dg, dst):
        first = pl.program_id(0) == 0

        @pl.when(first)
        def _():
            dst[...] = jnp.zeros_like(dst)

        dg_sum = None
        for b in range(bsz):
            _, vjp = jax.vjp(_gdn_step, _heads_diag(st_ref[b]), qk[b], qd[b], u[b], w[b], kt[b], cd[b], z_ref[b], g_ref[...])
            grads = vjp((dst[b], do_ref[b]))
            dst[b] = jnp.where(_head_mask(), grads[0], 0.0)
            for r, g in zip((dqk, dqd, du, dw, dkt, dcd, dz), grads[1:8]):
                r[b] = g
            dg_sum = grads[8] if dg_sum is None else dg_sum + grads[8]

        @pl.when(first)
        def _():
            dg[...] = dg_sum

        @pl.when(jnp.logical_not(first))
        def _():
            dg[...] += dg_sum

    res = pl.pallas_call(
        body, name="gdn_scan_bwd", grid=(n,), in_specs=[blk] * 7 + [gspec, blk, blk],
        out_specs=[blk] * 7 + [gspec], out_shape=[jax.ShapeDtypeStruct((bsz, seq, 256), F32)] * 7
        + [jax.ShapeDtypeStruct((1, 64), F32)],
        scratch_shapes=[pltpu.VMEM((bsz, GDN_W, GDN_W), F32)], compiler_params=_params(1))(*xs, z, norm_g, st_in, do)
    return list(res[:6]), res[6], res[7]


ATTN_TQ = 256
ATTN_SCALE = 96.0 ** -0.5
ATTN_KEY_FRACTIONS = (4, 2, 1)


def _attn_head(q, k, v, q0):
    s = _dot(q, k, (((1,), (1,)), ((), ()))) * ATTN_SCALE
    qc = (q0 + _rows(s.shape)) // CHUNK
    kc = lax.broadcasted_iota(jnp.int32, s.shape, 1) // CHUNK
    s = jnp.where(kc <= qc, s, -1e30)
    p = jnp.exp(s - jnp.max(s, axis=-1, keepdims=True))
    p = p / jnp.sum(p, axis=-1, keepdims=True)
    return _dot(p, v)


def _key_lengths(seq):
    return sorted({max(ATTN_TQ, seq // f) for f in ATTN_KEY_FRACTIONS})


def _key_variant(i, seq):
    need = (i + 1) * ATTN_TQ
    return sum(((need > klen).astype(jnp.int32) for klen in _key_lengths(seq)[:-1]), jnp.int32(0))


ATTN_QW, ATTN_VW = MLA_HEADS * MLA_QK, MLA_HEADS * 64


def _attn_specs(seq):
    def qspec(ch):
        return pl.BlockSpec((None, ATTN_TQ, ch), lambda b, i: (b, i, 0))

    def kspec(ch):
        return pl.BlockSpec((None, seq, ch), lambda b, i: (b, 0, 0))

    return qspec, kspec


def mla_attention(q, k, v):
    bsz, seq, _ = q.shape
    qspec, kspec = _attn_specs(seq)

    def body(q_r, k_r, v_r, o_r):
        i = pl.program_id(1)
        q0 = i * ATTN_TQ

        def with_keys(klen):
            outs = [_attn_head(q_r[:, MLA_QK * h: MLA_QK * h + MLA_QK], k_r[0:klen, MLA_QK * h: MLA_QK * h + MLA_QK],
                               v_r[0:klen, 64 * h: 64 * h + 64], q0) for h in range(MLA_HEADS)]
            o_r[...] = jnp.concatenate(outs, axis=-1)

        for j, klen in enumerate(_key_lengths(seq)):
            pl.when(_key_variant(i, seq) == j)(functools.partial(with_keys, klen))

    return pl.pallas_call(
        body, name="mla_attention", grid=(bsz, seq // ATTN_TQ), in_specs=[qspec(ATTN_QW), kspec(ATTN_QW), kspec(ATTN_VW)],
        out_specs=qspec(ATTN_VW), out_shape=jax.ShapeDtypeStruct((bsz, seq, ATTN_VW), F32), compiler_params=_params(2))(
            q, k, v)


def mla_attention_bwd(q, k, v, do):
    bsz, seq, _ = q.shape
    qspec, kspec = _attn_specs(seq)

    def body(q_r, k_r, v_r, do_r, dq_r, dk_r, dv_r):
        i = pl.program_id(1)
        q0 = i * ATTN_TQ

        @pl.when(i == 0)
        def _():
            dk_r[...] = jnp.zeros_like(dk_r)
            dv_r[...] = jnp.zeros_like(dv_r)

        def with_keys(klen):
            dq, dk, dv = [], [], []
            for h in range(MLA_HEADS):
                qk = slice(MLA_QK * h, MLA_QK * h + MLA_QK)
                sl = slice(64 * h, 64 * h + 64)
                _, vjp = jax.vjp(functools.partial(_attn_head, q0=q0), q_r[:, qk], k_r[0:klen, qk], v_r[0:klen, sl])
                a, b, c = vjp(do_r[:, sl])
                dq.append(a)
                dk.append(b)
                dv.append(c)
            dq_r[...] = jnp.concatenate(dq, axis=-1)
            dk_r[0:klen, :] += jnp.concatenate(dk, axis=-1)
            dv_r[0:klen, :] += jnp.concatenate(dv, axis=-1)

        for j, klen in enumerate(_key_lengths(seq)):
            pl.when(_key_variant(i, seq) == j)(functools.partial(with_keys, klen))

    shp = lambda ch: jax.ShapeDtypeStruct((bsz, seq, ch), F32)
    return pl.pallas_call(
        body, name="mla_attention_bwd", grid=(bsz, seq // ATTN_TQ),
        in_specs=[qspec(ATTN_QW), kspec(ATTN_QW), kspec(ATTN_VW), qspec(ATTN_VW)],
        out_specs=[qspec(ATTN_QW), kspec(ATTN_QW), kspec(ATTN_VW)],
        out_shape=[shp(ATTN_QW), shp(ATTN_QW), shp(ATTN_VW)], compiler_params=_params(2))(q, k, v, do)


LOSS_TS = 512


def loss_head(h, g, target):
    bsz, seq, d = h.shape
    ts = min(LOSS_TS, seq)
    tok = pl.BlockSpec((None, ts, d), lambda b, s: (b, s, 0))
    gspec = pl.BlockSpec((1, d), lambda b, s: (0, 0))
    lspec = pl.BlockSpec((1, 128), lambda b, s: (0, 0))

    def body(h_r, g_r, t_r, loss_r, dh_r, dg_r):
        first = jnp.logical_and(pl.program_id(0) == 0, pl.program_id(1) == 0)
        tv = t_r[...]

        def f(hv, gv):
            return 0.5 * jnp.sum(jnp.mean(jnp.square(rms(hv, gv) - tv), axis=-1, keepdims=True), axis=0, keepdims=True)

        val, vjp = jax.vjp(f, h_r[...], g_r[...])
        dh, dg = vjp(jnp.ones((1, 1), F32))
        dh_r[...] = dh
        lv = jnp.broadcast_to(val, (1, 128))

        @pl.when(first)
        def _():
            loss_r[...] = lv
            dg_r[...] = dg

        @pl.when(jnp.logical_not(first))
        def _():
            loss_r[...] += lv
            dg_r[...] += dg

    return pl.pallas_call(
        body, name="loss_head", grid=(bsz, seq // ts), in_specs=[tok, gspec, tok], out_specs=[lspec, tok, gspec],
        out_shape=[jax.ShapeDtypeStruct((1, 128), F32), jax.ShapeDtypeStruct(h.shape, F32), jax.ShapeDtypeStruct((1, d), F32)],
        compiler_params=_params(2))(h, g, target)


def _adamw_math(w, g, m, v):
    m = ADAM_B1 * m + (1.0 - ADAM_B1) * g
    v = ADAM_B2 * v + (1.0 - ADAM_B2) * jnp.square(g)
    m_hat = m / (1.0 - ADAM_B1 ** ADAM_STEP)
    v_hat = v / (1.0 - ADAM_B2 ** ADAM_STEP)
    return -ADAM_LR * (m_hat / (jnp.sqrt(v_hat) + ADAM_EPS) + ADAM_WD * w), m, v


def _row_block(rows, cols):
    want = max(8, (1 << 18) // cols)
    best = rows
    for r in range(8, rows + 1, 8):
        if rows % r == 0 and r <= want:
            best = r
    return best if rows % 8 == 0 else rows


def adamw(name, w, g, m, v):
    rows, cols = w.shape
    rb = _row_block(rows, cols)
    spec = pl.BlockSpec((rb, cols), lambda i: (i, 0))

    def body(w_r, g_r, m_r, v_r, d_o, m_o, v_o):
        d, mn, vn = _adamw_math(w_r[...], g_r[...], m_r[...], v_r[...])
        d_o[...] = d
        m_o[...] = mn
        v_o[...] = vn

    return pl.pallas_call(body, name=name, grid=(rows // rb,), in_specs=[spec] * 4, out_specs=[spec] * 3,
                          out_shape=[jax.ShapeDtypeStruct(w.shape, F32)] * 3, compiler_params=_params(1))(w, g, m, v)


def adamw_reduce(name, parts, w, m, v):
    rows, cols = w.shape
    rb = _row_block(rows, cols)
    spec = pl.BlockSpec((rb, cols), lambda i: (i, 0))
    pspec = pl.BlockSpec((N_DEV, rb, cols), lambda i: (0, i, 0))

    def body(p_r, w_r, m_r, v_r, g_o, d_o, m_o, v_o):
        g = p_r[0]
        for k in range(1, N_DEV):
            g = g + p_r[k]
        d, mn, vn = _adamw_math(w_r[...], g, m_r[...], v_r[...])
        g_o[...] = g
        d_o[...] = d
        m_o[...] = mn
        v_o[...] = vn

    return pl.pallas_call(body, name=name, grid=(rows // rb,), in_specs=[pspec, spec, spec, spec], out_specs=[spec] * 4,
                          out_shape=[jax.ShapeDtypeStruct(w.shape, F32)] * 4, compiler_params=_params(1))(parts, w, m, v)


MOD_CB = 512


def mod_matmul(c_rows, w_mod, b_mod):
    nl, d, cols = w_mod.shape

    def body(c_r, w_r, b_r, o_r):
        o_r[...] = _dot(jax.nn.silu(c_r[...]), w_r[...]) + b_r[...]

    return pl.pallas_call(
        body, name="mod_matmul", grid=(nl, cols // MOD_CB),
        in_specs=[pl.BlockSpec((8, d), lambda l, j: (0, 0)), pl.BlockSpec((None, d, MOD_CB), lambda l, j: (l, 0, j)),
                  pl.BlockSpec((None, 1, MOD_CB), lambda l, j: (l, 0, j))],
        out_specs=pl.BlockSpec((None, 8, MOD_CB), lambda l, j: (l, 0, j)),
        out_shape=jax.ShapeDtypeStruct((nl, 8, cols), F32), compiler_params=_params(2))(c_rows, w_mod, b_mod)


def mod_weight_grad(c_all, dmod):
    nl, nb, cols = dmod.shape
    d = c_all.shape[1]

    def body(c_r, g_r, o_r):
        o_r[...] = _dot(jax.nn.silu(c_r[...]), g_r[...], (((0,), (0,)), ((), ())))

    return pl.pallas_call(
        body, name="mod_weight_grad", grid=(nl, cols // MOD_CB),
        in_specs=[pl.BlockSpec((nb, d), lambda l, j: (0, 0)), pl.BlockSpec((None, nb, MOD_CB), lambda l, j: (l, 0, j))],
        out_specs=pl.BlockSpec((None, d, MOD_CB), lambda l, j: (l, 0, j)),
        out_shape=jax.ShapeDtypeStruct((nl, d, cols), F32), compiler_params=_params(2))(c_all, dmod)


def _half_block(hr, cols):
    rb = _row_block(hr, cols)
    return rb if rb % 16 == 0 else hr


def add_half(name, g, s, core):
    _, r, cols = g.shape
    hr = r // 2
    rb = _half_block(hr, cols)
    nblk = hr // rb
    gspec = pl.BlockSpec((None, rb, cols), lambda k, i, c: (k, c[0] * nblk + i, 0))
    spec = pl.BlockSpec((None, rb, cols), lambda k, i, c: (k, i, 0))

    def body(c_r, g_r, s_r, o_r, ob_r):
        t = g_r[...] + s_r[...]
        o_r[...] = t
        ob_r[...] = t.astype(BF16)

    return pl.pallas_call(
        body, name=name, grid_spec=pltpu.PrefetchScalarGridSpec(num_scalar_prefetch=1, grid=(N_CHIP, nblk),
                                                                in_specs=[gspec, spec], out_specs=[spec, spec]),
        out_shape=[jax.ShapeDtypeStruct((N_CHIP, hr, cols), F32), jax.ShapeDtypeStruct((N_CHIP, hr, cols), BF16)],
        compiler_params=_params(2))(core, g, s)


def sum_peers(name, p32, recv, ids, shard_shape, layer, acc=None):
    _, hr, cols = p32.shape
    rb = _half_block(hr, cols)
    nblk = hr // rb

    def slot(k):
        return pl.BlockSpec((None, rb, cols), lambda i, c: ((c[0] + k) % N_CHIP, i, 0))

    def body(c_r, o_r, r1, r2, r3, *rest):
        rest[-1][...] = ((o_r[...] + r1[...].astype(F32)) + r2[...].astype(F32)) + r3[...].astype(F32)

    args = (ids, p32, recv, recv, recv) + (() if acc is None else (acc,))
    return pl.pallas_call(
        body, name=name, grid_spec=pltpu.PrefetchScalarGridSpec(
            num_scalar_prefetch=1, grid=(nblk,),
            in_specs=[slot(0), slot(1), slot(2), slot(3)] + ([] if acc is None else [_ANY]),
            out_specs=pl.BlockSpec((None, rb, cols), lambda i, c: (layer, c[1] * nblk + i, 0))),
        out_shape=jax.ShapeDtypeStruct(shard_shape, F32), input_output_aliases={} if acc is None else {5: 0},
        compiler_params=_params(1))(*args)


def cast_into_slab(name, w, ids, layer=None):
    nl, r, cols = w.shape
    hr = r // 2
    rb = _half_block(hr, cols)
    nblk = hr // rb

    def body(c_r, w_r, o_r):
        o_r[...] = w_r[...].astype(BF16)

    if layer is not None:
        return pl.pallas_call(
            body, name=name, grid_spec=pltpu.PrefetchScalarGridSpec(
                num_scalar_prefetch=1, grid=(nblk,),
                in_specs=[pl.BlockSpec((None, rb, cols), lambda i, c: (layer, c[1] * nblk + i, 0))],
                out_specs=pl.BlockSpec((None, rb, cols), lambda i, c: (c[0], c[1] * nblk + i, 0))),
            out_shape=jax.ShapeDtypeStruct((N_CHIP, r, cols), BF16), compiler_params=_params(1))(ids, w)
    return pl.pallas_call(
        body, name=name, grid_spec=pltpu.PrefetchScalarGridSpec(
            num_scalar_prefetch=1, grid=(nl, nblk),
            in_specs=[pl.BlockSpec((None, rb, cols), lambda l, i, c: (l, c[1] * nblk + i, 0))],
            out_specs=pl.BlockSpec((None, None, rb, cols), lambda l, i, c: (l, c[0], c[1] * nblk + i, 0))),
        out_shape=jax.ShapeDtypeStruct((nl, N_CHIP, r, cols), BF16), compiler_params=_params(2))(ids, w)


def _me():
    return lax.axis_index("x"), lax.axis_index("y"), lax.axis_index("c")


def all_gather8(name, x_shard, in_vmem):
    m_per, n = x_shard.shape
    space = pltpu.VMEM if in_vmem else pl.ANY

    def body(x_ref, out_ref, send_sems, recv_sems, local_sem):
        x, y, c = _me()
        me, sibling = (x, y, c), (x, y, 1 - c)
        chips = [(1 - x, y), (x, 1 - y), (1 - x, 1 - y)]

        def rows(px, py, pc):
            return out_ref.at[pl.ds((4 * px + 2 * py + pc) * m_per, m_per), :]

        def copy(k, block, to, src=None):
            return pltpu.make_async_remote_copy(
                src_ref=rows(*block) if src is None else src, dst_ref=rows(*block), send_sem=send_sems.at[k],
                recv_sem=recv_sems.at[k], device_id=to, device_id_type=MESH)

        mine = pltpu.make_async_copy(x_ref, rows(*me), local_sem)
        mine.start()
        first = [copy(0, me, sibling, src=x_ref)]
        first += [copy(1 + j, me, (*chip, c), src=x_ref) for j, chip in enumerate(chips)]
        for cp in first:
            cp.start()
        passed = [copy(4 + j, (*chip, c), sibling) for j, chip in enumerate(chips)]
        for j, chip in enumerate(chips):
            copy(1 + j, (*chip, c), me).wait_recv()
            passed[j].start()
        copy(0, sibling, me).wait_recv()
        for j, chip in enumerate(chips):
            copy(4 + j, (*chip, 1 - c), me).wait_recv()
        for cp in first + passed:
            cp.wait_send()
        mine.wait()

    return pl.pallas_call(
        body, name=name, out_shape=jax.ShapeDtypeStruct((N_DEV * m_per, n), x_shard.dtype),
        in_specs=[pl.BlockSpec(memory_space=space)], out_specs=pl.BlockSpec(memory_space=space),
        scratch_shapes=[pltpu.SemaphoreType.DMA((7,)), pltpu.SemaphoreType.DMA((7,)), pltpu.SemaphoreType.DMA],
    )(x_shard)


_ANY = pl.BlockSpec(memory_space=pl.ANY)


def all_gather_weights(name, slabs):
    n = len(slabs)

    def body(*refs):
        outs = refs[n:2 * n]
        send_sems, recv_sems = refs[2 * n:]
        x, y, c = _me()
        me, sibling = (x, y, c), (x, y, 1 - c)
        chips = [(1 - x, y), (x, 1 - y), (1 - x, 1 - y)]

        def view(i, px, py, pc):
            hr = slabs[i].shape[2] // 2
            return outs[i].at[:, 2 * px + py, pl.ds(pc * hr, hr), :]

        def copy(i, k, block, to):
            return pltpu.make_async_remote_copy(
                src_ref=view(i, *block), dst_ref=view(i, *block), send_sem=send_sems.at[i, k],
                recv_sem=recv_sems.at[i, k], device_id=to, device_id_type=MESH)

        first = []
        for i in range(n):
            first.append(copy(i, 0, me, sibling))
            first += [copy(i, 1 + j, me, (*chip, c)) for j, chip in enumerate(chips)]
        for cp in first:
            cp.start()
        passed = []
        for j, chip in enumerate(chips):
            for i in range(n):
                copy(i, 1 + j, (*chip, c), me).wait_recv()
                passed.append(copy(i, 4 + j, (*chip, c), sibling))
                passed[-1].start()
        for i in range(n):
            copy(i, 0, sibling, me).wait_recv()
            for j, chip in enumerate(chips):
                copy(i, 4 + j, (*chip, 1 - c), me).wait_recv()
        for cp in first + passed:
            cp.wait_send()

    return pl.pallas_call(
        body, name=name, out_shape=[jax.ShapeDtypeStruct(s.shape, s.dtype) for s in slabs],
        in_specs=[_ANY] * n, out_specs=[_ANY] * n, input_output_aliases={i: i for i in range(n)},
        scratch_shapes=[pltpu.SemaphoreType.DMA((n, 7)), pltpu.SemaphoreType.DMA((n, 7))],
    )(*slabs)


def _slab_block(slab, px, py, pc):
    hr = slab.shape[1] // 2
    return slab.at[2 * px + py, pl.ds(pc * hr, hr), :]


def gather_over_ici(slab):
    def issue(refs, send_sems, recv_sems):
        (buf,) = refs
        x, y, c = _me()
        peers = [(x, y, 1 - c), (1 - x, y, c), (x, 1 - y, c), (1 - x, 1 - y, c)]

        def copy(k, block, to):
            return pltpu.make_async_remote_copy(
                src_ref=_slab_block(buf, *block), dst_ref=_slab_block(buf, *block), send_sem=send_sems.at[k],
                recv_sem=recv_sems.at[k], device_id=to, device_id_type=MESH)

        sends = [copy(k, (x, y, c), p) for k, p in enumerate(peers)]
        arrivals = [copy(k, p, (x, y, c)) for k, p in enumerate(peers)]
        return sends, arrivals, sends

    return ([slab], 4, issue)


def gather_over_d2d(slab):
    def issue(refs, send_sems, recv_sems):
        (buf,) = refs
        x, y, c = _me()
        chips = [(1 - x, y), (x, 1 - y), (1 - x, 1 - y)]

        def copy(k, block):
            return pltpu.make_async_remote_copy(
                src_ref=_slab_block(buf, *block), dst_ref=_slab_block(buf, *block), send_sem=send_sems.at[k],
                recv_sem=recv_sems.at[k], device_id=(x, y, 1 - c), device_id_type=MESH)

        sends = [copy(k, (*chip, c)) for k, chip in enumerate(chips)]
        arrivals = [copy(k, (*chip, 1 - c)) for k, chip in enumerate(chips)]
        return sends, arrivals, sends

    return ([slab], 3, issue)


def chip_exchange_job(ps):
    n = len(ps)

    def issue(refs, send_sems, recv_sems):
        ins, outs = refs[:n], refs[n:]
        mx, my, mc = _me()
        ci = 2 * mx + my
        chips = [(1 - mx, my), (mx, 1 - my), (1 - mx, 1 - my)]
        sends, arrivals = [], []
        for i in range(n):
            for k, (px, py) in enumerate(chips):
                sem = 3 * i + k
                sends.append(pltpu.make_async_remote_copy(
                    src_ref=ins[i].at[2 * px + py], dst_ref=outs[i].at[ci], send_sem=send_sems.at[sem],
                    recv_sem=recv_sems.at[sem], device_id=(px, py, mc), device_id_type=MESH))
                arrivals.append(pltpu.make_async_remote_copy(
                    src_ref=ins[i].at[ci], dst_ref=outs[i].at[2 * px + py], send_sem=send_sems.at[sem],
                    recv_sem=recv_sems.at[sem], device_id=(px, py, mc), device_id_type=MESH))
        return sends, arrivals, sends

    return (list(ps) + [lax.empty(p.shape, p.dtype) for p in ps], 3 * n, issue)


def grad_sibling_exchange(name, gs):
    n = len(gs)

    def body(*refs):
        ins, outs = refs[:n], refs[n:2 * n]
        send_sems, recv_sems = refs[2 * n:]
        mx, my, mc = _me()
        cps = []
        for i in range(n):
            hr = gs[i].shape[1] // 2
            cps.append(pltpu.make_async_remote_copy(
                src_ref=ins[i].at[:, pl.ds((1 - mc) * hr, hr), :], dst_ref=outs[i], send_sem=send_sems.at[i],
                recv_sem=recv_sems.at[i], device_id=(mx, my, 1 - mc), device_id_type=MESH))
            cps[-1].start()
        for cp in cps:
            cp.wait()

    return pl.pallas_call(
        body, name=name, out_shape=[jax.ShapeDtypeStruct((N_CHIP, g.shape[1] // 2, g.shape[2]), g.dtype) for g in gs],
        in_specs=[_ANY] * n, out_specs=[_ANY] * n,
        scratch_shapes=[pltpu.SemaphoreType.DMA((n,)), pltpu.SemaphoreType.DMA((n,))],
    )(*gs)


def grad_chip_exchange(name, ps):
    n = len(ps)

    def body(*refs):
        ins, outs = refs[:n], refs[n:2 * n]
        send_sems, recv_sems = refs[2 * n:]
        mx, my, mc = _me()
        ci = 2 * mx + my
        chips = [(1 - mx, my), (mx, 1 - my), (1 - mx, 1 - my)]
        sends = []
        for i in range(n):
            for k, (px, py) in enumerate(chips):
                sends.append(pltpu.make_async_remote_copy(
                    src_ref=ins[i].at[2 * px + py], dst_ref=outs[i].at[ci], send_sem=send_sems.at[i, k],
                    recv_sem=recv_sems.at[i, k], device_id=(px, py, mc), device_id_type=MESH))
                sends[-1].start()
        for i in range(n):
            for k, (px, py) in enumerate(chips):
                pltpu.make_async_remote_copy(
                    src_ref=ins[i].at[ci], dst_ref=outs[i].at[2 * px + py], send_sem=send_sems.at[i, k],
                    recv_sem=recv_sems.at[i, k], device_id=(px, py, mc), device_id_type=MESH).wait_recv()
        for cp in sends:
            cp.wait_send()

    return pl.pallas_call(
        body, name=name, out_shape=[jax.ShapeDtypeStruct(p.shape, p.dtype) for p in ps], in_specs=[_ANY] * n,
        out_specs=[_ANY] * n, scratch_shapes=[pltpu.SemaphoreType.DMA((n, 3)), pltpu.SemaphoreType.DMA((n, 3))],
    )(*ps)


def grad_half_exchange(name, shards):
    n = len(shards)

    def body(*refs):
        outs = refs[n:2 * n]
        send_sems, recv_sems = refs[2 * n:]
        mx, my, mc = _me()

        def copy(i, core):
            hr = shards[i].shape[1] // 2
            rows = outs[i].at[:, pl.ds(core * hr, hr), :]
            return pltpu.make_async_remote_copy(src_ref=rows, dst_ref=rows, send_sem=send_sems.at[i],
                                                recv_sem=recv_sems.at[i], device_id=(mx, my, 1 - mc), device_id_type=MESH)

        sends = [copy(i, mc) for i in range(n)]
        for cp in sends:
            cp.start()
        for i in range(n):
            copy(i, 1 - mc).wait_recv()
        for cp in sends:
            cp.wait_send()

    return pl.pallas_call(
        body, name=name, out_shape=[jax.ShapeDtypeStruct(s.shape, s.dtype) for s in shards], in_specs=[_ANY] * n,
        out_specs=[_ANY] * n, input_output_aliases={i: i for i in range(n)},
        scratch_shapes=[pltpu.SemaphoreType.DMA((n,)), pltpu.SemaphoreType.DMA((n,))],
    )(*shards)


WEIGHTS = ['w_mod', 'b_mod', 'norm_mix_g', 'w_in', 'gdn_conv_w', 'gdn_a_log', 'gdn_dt_bias', 'gdn_norm_g', 'rg_conv_w',
           'rg_conv_b', 'rg_w_a', 'rg_b_a', 'rg_w_x', 'rg_b_x', 'rg_lambda', 'mla_q_norm_g', 'mla_w_qb', 'mla_kv_norm_g',
           'mla_w_kvb', 'w_out', 'norm_mlp_g', 'w_mlp_in', 'w_mlp_out', 'final_norm_g']
SHARDED = {'w_in': 2, 'gdn_conv_w': 2, 'rg_conv_w': 2, 'mla_w_qb': 2, 'mla_w_kvb': 2, 'w_out': 1, 'w_mlp_in': 2, 'w_mlp_out': 1}
GATHER_BF16 = ('w_in', 'mla_w_qb', 'mla_w_kvb', 'w_out', 'w_mlp_in', 'w_mlp_out')
GATHER_FIRST = GATHER_BF16[:4]
REPLICATED = [n for n in WEIGHTS if n not in SHARDED and n != 'w_mod']
PACK_COLS = 1024


def _pack(arrays, multiple):
    flat = jnp.concatenate([a.reshape(-1) for a in arrays])
    pad = (-flat.shape[0]) % multiple
    return jnp.pad(flat, (0, pad)) if pad else flat


def _unpack(flat, shapes):
    out, o = [], 0
    for shp in shapes:
        n = int(np.prod(shp))
        out.append(flat[o:o + n].reshape(shp))
        o += n
    return out


def _pack_rows(arrays):
    rows = []
    for a in arrays:
        flat = a.reshape(-1)
        pad = (-flat.shape[0]) % PACK_COLS
        rows.append((jnp.pad(flat, (0, pad)) if pad else flat).reshape(-1, PACK_COLS))
    out = jnp.concatenate(rows, axis=0)
    pad = (-out.shape[0]) % 8
    return jnp.pad(out, ((0, pad), (0, 0))) if pad else out


def _unpack_rows(packed, shapes):
    out, r = [], 0
    for shp in shapes:
        n = int(np.prod(shp))
        nr = -(-n // PACK_COLS)
        piece = packed[r:r + nr]
        out.append((piece if n == nr * PACK_COLS else piece.reshape(-1)[:n]).reshape(shp))
        r += nr
    return out


def _unshard(stacked, axis):
    moved = jnp.moveaxis(stacked, 0, axis)
    shp = list(moved.shape)
    shp[axis:axis + 2] = [shp[axis] * shp[axis + 1]]
    return moved.reshape(shp)


def _shard(full, axis):
    shp = list(full.shape)
    shp[axis:axis + 1] = [N_CHIP, shp[axis] // N_CHIP]
    return jnp.moveaxis(full.reshape(shp), axis, 0)


def _proj_cols(w):
    pad = jnp.zeros(w.shape[:-1] + (PROJ_WIDTH - w.shape[-1],), w.dtype)
    return jnp.concatenate([w[..., 0:1024], w[..., 1032:2472], w[..., 1024:1032], pad], axis=-1)


def _proj_cols_back(d):
    return jnp.concatenate([d[..., 0:1024], d[..., 2464:2472], d[..., 1024:2464]], axis=-1)


def _heads_split(w, heads, first):
    per = w.shape[-1] // heads
    r = w.reshape(w.shape[:-1] + (heads, per))
    lead = w.shape[:-1]
    return jnp.concatenate([r[..., :first].reshape(lead + (heads * first,)),
                            r[..., first:].reshape(lead + (heads * (per - first),))], axis=-1)


def _heads_merge(d, heads, first):
    lead = d.shape[:-1]
    per = d.shape[-1] // heads
    a = d[..., :heads * first].reshape(lead + (heads, first))
    b = d[..., heads * first:].reshape(lead + (heads, per - first))
    return jnp.concatenate([a, b], axis=-1).reshape(lead + (heads * per,))


def _block_diag(w):
    nl = w.shape[0]
    eye = jnp.eye(2, dtype=w.dtype)
    return jnp.einsum('lcoij,op->lcoipj', w.reshape(nl, 4, 2, 64, 64), eye).reshape(nl, 4, 128, 128)


def _block_diag_back(g):
    nl = g.shape[0]
    return jnp.einsum('lcoipj,op->lcoij', g.reshape(nl, 4, 2, 64, 2, 64), jnp.eye(2, dtype=g.dtype)).reshape(nl, 8, 64, 64)


def kernel(x, c, positions, w_mod, b_mod, norm_mix_g, w_in, gdn_conv_w, gdn_a_log, gdn_dt_bias, gdn_norm_g, rg_conv_w, rg_conv_b, rg_w_a, rg_b_a, rg_w_x, rg_b_x, rg_lambda, mla_q_norm_g, mla_w_qb, mla_kv_norm_g, mla_w_kvb, w_out, norm_mlp_g, w_mlp_in, w_mlp_out, final_norm_g, loss_target, m_w_mod, m_b_mod, m_norm_mix_g, m_w_in, m_gdn_conv_w, m_gdn_a_log, m_gdn_dt_bias, m_gdn_norm_g, m_rg_conv_w, m_rg_conv_b, m_rg_w_a, m_rg_b_a, m_rg_w_x, m_rg_b_x, m_rg_lambda, m_mla_q_norm_g, m_mla_w_qb, m_mla_kv_norm_g, m_mla_w_kvb, m_w_out, m_norm_mlp_g, m_w_mlp_in, m_w_mlp_out, m_final_norm_g, v_w_mod, v_b_mod, v_norm_mix_g, v_w_in, v_gdn_conv_w, v_gdn_a_log, v_gdn_dt_bias, v_gdn_norm_g, v_rg_conv_w, v_rg_conv_b, v_rg_w_a, v_rg_b_a, v_rg_w_x, v_rg_b_x, v_rg_lambda, v_mla_q_norm_g, v_mla_w_qb, v_mla_kv_norm_g, v_mla_w_kvb, v_w_out, v_norm_mlp_g, v_w_mlp_in, v_w_mlp_out, v_final_norm_g):
    given = dict(locals())
    wts = {n: given[n] for n in WEIGHTS}
    mom_m = {n: given["m_" + n] for n in WEIGHTS}
    mom_v = {n: given["v_" + n] for n in WEIGHTS}
    bsz, seq, d = x.shape
    depth = w_mod.shape[0]
    mx, my, mc = lax.axis_index("x"), lax.axis_index("y"), lax.axis_index("c")
    chip = 2 * mx + my
    dev = 2 * chip + mc

    conv_shapes = [wts['gdn_conv_w'].shape, wts['rg_conv_w'].shape]
    conv_flat = _pack([wts['gdn_conv_w'], wts['rg_conv_w']], d)
    conv_rows = conv_flat.shape[0] // d
    assert bsz + conv_rows <= 8
    c_pad = jnp.concatenate([c, conv_flat.reshape(conv_rows, d), jnp.zeros((8 - bsz - conv_rows, d), F32)], axis=0)
    gath = all_gather8("gather_c", c_pad, True).reshape(N_DEV, 8, d)
    c_all = gath[:, :bsz].reshape(N_DEV * bsz, d)
    conv_all = gath[0::2, bsz:bsz + conv_rows].reshape(N_CHIP, conv_rows * d)
    gdn_conv_full, rg_conv_full = [
        _unshard(jnp.stack([_unpack(conv_all[s], conv_shapes)[i] for s in range(N_CHIP)]), 2) for i in range(2)]

    n_half = N_DEV * bsz // 2
    mod_cols = w_mod.shape[2]
    c_rows = lax.dynamic_slice(c_all, (n_half * mc, 0), (n_half, d))
    b_mod_mine = lax.dynamic_slice(b_mod, (0, chip * mod_cols), (depth, mod_cols)).reshape(depth, 1, mod_cols)
    mod_piece = mod_matmul(c_rows, w_mod, b_mod_mine)
    mod_g = all_gather8("gather_mod", mod_piece.reshape(depth * n_half, mod_cols), True)
    mod_all = mod_g.reshape(N_CHIP, 2, depth, n_half, mod_cols).transpose(2, 1, 3, 0, 4).reshape(depth, 2 * n_half, 6 * d)
    mod_mine = lax.dynamic_slice(mod_all, (0, bsz * dev, 0), (depth, bsz, 6 * d)).reshape(depth, bsz, 6, 1, d)

    ids = jnp.stack([chip, mc]).astype(jnp.int32)
    slabs = dict(zip(GATHER_FIRST, all_gather_weights(
        "gather_weights", [cast_into_slab("cast_" + n, wts[n], ids) for n in GATHER_FIRST])))

    def columns(g):
        return g.transpose(0, 2, 1, 3).reshape(g.shape[0], g.shape[2], N_CHIP * g.shape[3])

    def rows_of(g):
        return g.reshape(g.shape[0], N_CHIP * g.shape[2], g.shape[3])

    w_cat = _proj_cols(columns(slabs['w_in']))
    w_q = columns(slabs['mla_w_qb']).astype(F32)
    w_kv = columns(slabs['mla_w_kvb']).astype(F32)
    w_out_full = rows_of(slabs['w_out'])
    bd_a, bd_x = _block_diag(rg_w_a), _block_diag(rg_w_x)

    inv_freq = ROPE_THETA ** (-jnp.arange(0, 32, 2, dtype=F32) / 32.0)
    ang = positions.astype(F32)[..., None] * inv_freq
    cs = jnp.concatenate([jnp.cos(ang), jnp.sin(ang)], axis=-1)

    proj_ch = [w for _, w in PROJ_PIECES]

    def row(a, l):
        return a[l].reshape(1, -1)

    def layer_args(l):
        sh_m, sc_m, gt_m, sh_f, sc_f, gt_f = (mod_mine[l, :, k] for k in range(6))
        return dict(
            mods=(sh_m, sc_m, gt_m, sh_f, sc_f, gt_f),
            mixer_in=dict(ex=[sc_m, sh_m], par=[row(norm_mix_g, l)], big=[(w_cat, l)], out_ch=proj_ch, ts=512),
            gdn_conv=dict(par_tiled=[gdn_conv_full[l]], out_ch=[768], ts=seq, nc=3),
            gdn_local=dict(par=[row(gdn_a_log, l), row(gdn_dt_bias, l)], out_ch=[256] * 7, ts=512),
            rglru=dict(par_tiled=[rg_conv_full[l], row(rg_conv_b, l), row(rg_b_a, l), row(rg_b_x, l), row(rg_lambda, l),
                                  bd_a[l], bd_x[l]], out_ch=[512], ts=seq, nc=4),
            mla_pre=dict(tok_nd=[cs], par=[row(mla_q_norm_g, l), row(mla_kv_norm_g, l), w_q[l], w_kv[l]],
                         out_ch=[ATTN_QW, ATTN_QW, ATTN_VW], ts=512),
            out_proj=dict(ex=[gt_m], big=[(w_out_full, l)], out_ch=[d], ts=512),
            mlp_in=dict(ex=[sc_f, sh_f], par=[row(norm_mlp_g, l)], big=[w_mi.get(l)], out_ch=[4 * d], ts=256),
            mlp_out=dict(ex=[gt_f], big=[w_mo.get(l)], out_ch=[d], ts=256),
        )

    mi_buf = [cast_into_slab("cast_w_mlp_in%d" % l, wts['w_mlp_in'], ids, layer=l) for l in range(depth)]
    mo_buf = [cast_into_slab("cast_w_mlp_out%d" % l, wts['w_mlp_out'], ids, layer=l) for l in range(depth)]
    w_mi, w_mo = {}, {}

    def staged(name, fn, jobs, **kw):
        return run_stage(name, fn, side=jobs, **kw) if jobs else (run_stage(name, fn, **kw), [])

    saved = []
    h = x
    for l in range(depth):
        a = layer_args(l)
        sfx = str(l)
        first, more = l == 0, l + 1 < depth
        (qkv_raw, z, rx, rgate, mq, mkv, misc), bufs = staged(
            "mixer_in" + sfx, fn_mixer_in, [] if first else [gather_over_d2d(mo_buf[l])], tok=[h], **a['mixer_in'])
        if not first:
            w_mo[l] = bufs[0].reshape(N_CHIP * d, d)
        (qkv_act,) = run_stage("gdn_conv" + sfx, fn_gdn_conv, tok=[qkv_raw], **a['gdn_conv'])
        (*xs, inverses), bufs = staged("gdn_local" + sfx, fn_gdn_local, [gather_over_ici(mi_buf[l])] if first else [],
                                       tok=[qkv_act, misc], **a['gdn_local'])
        if first:
            mi_buf[l] = bufs[0]
        o_a, st_in = gdn_scan(xs, z, row(gdn_norm_g, l))
        (o_b,), bufs = staged("rglru" + sfx, fn_rglru, [gather_over_d2d(mi_buf[l]), gather_over_ici(mo_buf[l])] if first else [],
                              tok=[rx, rgate], **a['rglru'])
        if first:
            w_mi[l], mo_buf[l] = bufs
        q_at, k_at, v_at = run_stage("mla_pre" + sfx, fn_mla_pre, tok=[mq, mkv, misc], **a['mla_pre'])
        o_c = mla_attention(q_at, k_at, v_at)
        (h_mid,), bufs = staged("out_proj" + sfx, fn_out_proj, [gather_over_d2d(mo_buf[l])] if first else [],
                                tok=[h, o_a, o_b, o_c], **a['out_proj'])
        if first:
            w_mo[l] = bufs[0].reshape(N_CHIP * d, d)
        a = layer_args(l)
        (a_mlp,), bufs = staged("mlp_in" + sfx, fn_mlp_in, [gather_over_ici(mi_buf[l + 1])] if more else [],
                                tok=[h_mid], **a['mlp_in'])
        if more:
            mi_buf[l + 1] = bufs[0]
        (h_out,), bufs = staged("mlp_out" + sfx, fn_mlp_out,
                                [gather_over_d2d(mi_buf[l + 1]), gather_over_ici(mo_buf[l + 1])] if more else [],
                                tok=[h_mid, a_mlp], **a['mlp_out'])
        if more:
            w_mi[l + 1], mo_buf[l + 1] = bufs
        saved.append(dict(h=h, qkv_raw=qkv_raw, z=z, rx=rx, rgate=rgate, mq=mq, mkv=mkv, misc=misc, qkv_act=qkv_act, xs=xs,
                          inverses=inverses,
                          st_in=st_in, o_a=o_a, o_b=o_b, o_c=o_c, q_at=q_at, k_at=k_at, v_at=v_at, h_mid=h_mid, a_mlp=a_mlp))
        h = h_out

    loss_part, dh, d_final_g = loss_head(h, final_norm_g.reshape(1, d), loss_target)
    loss = lax.psum(loss_part[0, 0], ("x", "y", "c"))

    g_full = {n: [None] * depth for n in SHARDED}
    g_rep = {n: [None] * depth for n in REPLICATED if n not in ('final_norm_g', 'b_mod')}

    def column_slabs(g):
        return g.reshape(g.shape[0], N_CHIP, g.shape[1] // N_CHIP).transpose(1, 0, 2)

    def row_slabs(g):
        return g.reshape(N_CHIP, g.shape[0] // N_CHIP, g.shape[1])
    core_id = mc.reshape(1).astype(jnp.int32)
    shards = [None] * len(GATHER_BF16)
    mixer_units, mlp_in_unit, mlp_out_unit = [0, 1, 2, 3], [4], [5]

    def reduce_begin(tag, idxs, l):
        gs = [g_full[GATHER_BF16[i]][l] for i in idxs]
        from_sibling = grad_sibling_exchange("grad_sibling_exchange_" + tag, gs)
        pairs = [add_half("grad_add_%s%d" % (GATHER_BF16[i], l), g, s, core_id) for i, g, s in zip(idxs, gs, from_sibling)]
        return [p[0] for p in pairs], [p[1] for p in pairs]

    def reduce_end(idxs, l, sums32, landed):
        for i, p, r in zip(idxs, sums32, landed):
            n = GATHER_BF16[i]
            shards[i] = sum_peers("grad_sum_%s%d" % (n, l), p, r, ids, wts[n].shape, l, acc=shards[i])

    def staged_bwd(name, fn, idxs, l_units, pair, **kw):
        if pair is None:
            return run_stage(name, fn, **kw)
        groups, bufs = run_stage(name, fn, side=[chip_exchange_job(pair[1])], **kw)
        reduce_end(idxs, l_units, pair[0], bufs[len(idxs):])
        return groups

    dmod = [None] * depth
    carried = None
    for l in reversed(range(depth)):
        a, sv = layer_args(l), saved[l]
        sfx = str(l)
        mlp_out_tok = dict(tok=[sv['h_mid'], sv['a_mlp']], cot=[dh])
        (dh_mid, da_mlp), (dgt_f,), _, _, _ = staged_bwd(
            "mlp_out" + sfx, fn_mlp_out, mixer_units, l + 1, carried, which="small", dtok_dtype={1: BF16}, **mlp_out_tok,
            **{**a['mlp_out'], 'ts': 256})
        _, _, _, _, (dw_mlp_out,) = run_stage(
            "mlp_out" + sfx, fn_mlp_out, which="big", **mlp_out_tok, **{**a['mlp_out'], 'ts': 512})
        g_full['w_mlp_out'][l] = row_slabs(dw_mlp_out)
        _, _, _, _, (g_full['w_mlp_in'][l],) = run_stage(
            "mlp_in" + sfx, fn_mlp_in, tok=[sv['h_mid']], cot=[da_mlp], which="big", **{**a['mlp_in'], 'ts': 512})
        (dh_mid,), (dsc_f, dsh_f), (g_rep['norm_mlp_g'][l],), _, _ = run_stage(
            "mlp_in" + sfx, fn_mlp_in, tok=[sv['h_mid']], cot=[da_mlp], addin=dh_mid, which="small", **a['mlp_in'])
        mlp_sums32, mlp_sums16 = reduce_begin("mlp" + sfx, mlp_in_unit + mlp_out_unit, l)
        (dh_in, do_a, do_b, do_c), (dgt_m,), _, _, (dw_out,) = run_stage(
            "out_proj" + sfx, fn_out_proj, tok=[sv['h'], sv['o_a'], sv['o_b'], sv['o_c']], cot=[dh_mid], **a['out_proj'])
        g_full['w_out'][l] = row_slabs(dw_out)
        attn_cot = mla_attention_bwd(sv['q_at'], sv['k_at'], sv['v_at'], do_c)
        (dmq, dmkv, dmisc_c), _, (g_rep['mla_q_norm_g'][l], g_rep['mla_kv_norm_g'][l], dw_q, dw_kv), _, _ = run_stage(
            "mla_pre" + sfx, fn_mla_pre, tok=[sv['mq'], sv['mkv'], sv['misc']], cot=attn_cot, **a['mla_pre'])
        g_full['mla_w_qb'][l] = column_slabs(dw_q)
        g_full['mla_w_kvb'][l] = column_slabs(dw_kv)
        (drx, drgate), _, _, rg_g, _ = staged_bwd(
            "rglru" + sfx, fn_rglru, mlp_in_unit, l, (mlp_sums32[:1], mlp_sums16[:1]), tok=[sv['rx'], sv['rgate']], cot=[do_b],
            **a['rglru'])
        (g_full['rg_conv_w'][l], g_rep['rg_conv_b'][l], g_rep['rg_b_a'][l], g_rep['rg_b_x'][l], g_rep['rg_lambda'][l],
         g_rep['rg_w_a'][l], g_rep['rg_w_x'][l]) = rg_g
        dxs, dz, g_rep['gdn_norm_g'][l] = gdn_scan_bwd(sv['xs'], sv['z'], row(gdn_norm_g, l), sv['st_in'], do_a)
        (dqkv_act, dmisc_a), _, (g_rep['gdn_a_log'][l], g_rep['gdn_dt_bias'][l]), _, _ = staged_bwd(
            "gdn_local" + sfx, fn_gdn_local, mlp_out_unit, l, (mlp_sums32[1:], mlp_sums16[1:]),
            tok=[sv['qkv_act'], sv['misc']], tok_nd=[sv['inverses']], cot=dxs, **a['gdn_local'])
        (dqkv_raw,), _, _, (g_full['gdn_conv_w'][l],), _ = run_stage(
            "gdn_conv" + sfx, fn_gdn_conv, tok=[sv['qkv_raw']], cot=[dqkv_act], **a['gdn_conv'])
        (dh,), (dsc_m, dsh_m), (g_rep['norm_mix_g'][l],), _, (dw_cat,) = run_stage(
            "mixer_in" + sfx, fn_mixer_in, tok=[sv['h']], cot=[dqkv_raw, dz, drx, drgate, dmq, dmkv, dmisc_a + dmisc_c],
            addin=dh_in, **a['mixer_in'])
        g_full['w_in'][l] = column_slabs(_proj_cols_back(dw_cat))
        dmod[l] = jnp.concatenate([dsh_m, dsc_m, dgt_m, dsh_f, dsc_f, dgt_f], axis=-1).reshape(bsz, 6 * d)
        carried = reduce_begin("mixer" + sfx, mixer_units, l)
    grad_x = dh
    reduce_end(mixer_units, 0, carried[0], grad_chip_exchange("grad_chip_exchange", carried[1]))

    dmod = jnp.stack(dmod)
    dmod_pad = jnp.concatenate([dmod.reshape(depth * bsz, 6 * d), jnp.zeros((8 - depth * bsz, 6 * d), F32)], axis=0)
    dmod_all = all_gather8("gather_dmod", dmod_pad, True).reshape(N_DEV, 8, 6 * d)[:, :depth * bsz]
    dmod_all = dmod_all.reshape(N_DEV, depth, bsz, 6 * d).transpose(1, 0, 2, 3).reshape(depth, N_DEV * bsz, 6 * d)
    g_w_mod = mod_weight_grad(c_all, lax.dynamic_slice(dmod_all, (0, 0, chip * mod_cols), (depth, N_DEV * bsz, mod_cols)))

    g_rep = {n: jnp.stack(v) for n, v in g_rep.items()}
    g_rep['rg_w_a'] = _block_diag_back(g_rep['rg_w_a'])
    g_rep['rg_w_x'] = _block_diag_back(g_rep['rg_w_x'])
    g_rep['final_norm_g'] = d_final_g
    g_rep['b_mod'] = jnp.sum(dmod, axis=1)
    conv_names = ['gdn_conv_w', 'rg_conv_w']
    conv_full_shapes = [(depth,) + g_full[n][0].shape for n in conv_names]
    small_shapes = [wts[n].shape for n in REPLICATED] + conv_full_shapes
    rep_part = _pack_rows([g_rep[n].reshape(wts[n].shape) for n in REPLICATED] + [jnp.stack(g_full[n]) for n in conv_names])
    rep_rows = rep_part.shape[0]
    rep_all = all_gather8("gather_small_grads", rep_part, True).reshape(N_DEV, rep_rows, PACK_COLS)
    conv_zeros = [jnp.zeros(s, F32) for s in conv_full_shapes]
    rep_out = adamw_reduce("adamw_small", rep_all, *[
        _pack_rows([src[n] for n in REPLICATED] + conv_zeros) for src in (wts, mom_m, mom_v)])
    small_names = REPLICATED + conv_names
    rep_g, rep_d, rep_m, rep_v = [dict(zip(small_names, _unpack_rows(o, small_shapes))) for o in rep_out]
    sh_g = {}
    for n in conv_names:
        cols = wts[n].shape[2]
        sh_g[n] = lax.dynamic_slice(rep_g.pop(n), (0, 0, chip * cols), wts[n].shape)
        for dct in (rep_d, rep_m, rep_v):
            dct.pop(n)

    sh_g.update(zip(GATHER_BF16, grad_half_exchange("grad_half_exchange", shards)))
    sh_names = list(SHARDED)

    def as2d(t):
        return t.reshape(-1, t.shape[-1])

    sh_d, sh_m, sh_v = {}, {}, {}
    for n in sh_names + ['w_mod']:
        g = g_w_mod if n == 'w_mod' else sh_g[n]
        res = adamw("adamw_" + n, as2d(wts[n]), as2d(g), as2d(mom_m[n]), as2d(mom_v[n]))
        sh_d[n], sh_m[n], sh_v[n] = (r.reshape(wts[n].shape) for r in res)
    sh_g['w_mod'] = g_w_mod

    def pick(shd, rep):
        return [shd[n] if n in shd else rep[n] for n in WEIGHTS]

    return (loss, grad_x, *pick(sh_g, rep_g), *pick(sh_d, rep_d), *pick(sh_m, rep_m), *pick(sh_v, rep_v))
```

```python
import functools

import jax
import jax.numpy as jnp
import numpy as np
from jax import lax
from jax.experimental import pallas as pl
from jax.experimental.pallas import tpu as pltpu

F32, BF16 = jnp.float32, jnp.bfloat16
HI = lax.Precision.HIGH
MESH = pl.DeviceIdType.MESH

EPS = 1e-6
CHUNK = 64
GDN_HEADS = 4
MLA_HEADS = 4
RG_C = 8.0
ROPE_THETA = 10000.0
N_DEV = 8
N_CHIP = 4
V7X_VMEM_LIMIT = 60 * 1024 * 1024
ADAM_LR, ADAM_B1, ADAM_B2, ADAM_EPS, ADAM_WD, ADAM_STEP = 0.001, 0.9, 0.999, 1e-08, 0.01, 10


def _params(n_grid):
    return pltpu.CompilerParams(dimension_semantics=("arbitrary",) * n_grid, vmem_limit_bytes=V7X_VMEM_LIMIT)


def _dot(a, b, dims=(((1,), (0,)), ((), ()))):
    return lax.dot_general(a.astype(BF16), b.astype(BF16), dims, preferred_element_type=F32)


@jax.custom_vjp
def _mm_probe(x, w, probe):
    return _dot(x, w)


def _mm_probe_fwd(x, w, probe):
    return _dot(x, w), (x, w)


def _mm_probe_bwd(res, dy):
    x, w = res
    dx = _dot(dy, w, (((1,), (1,)), ((), ())))
    dw = _dot(x, dy, (((0,), (0,)), ((), ())))
    return dx, jnp.zeros_like(w), dw


_mm_probe.defvjp(_mm_probe_fwd, _mm_probe_bwd)


@jax.custom_vjp
def _probe_only(x, probe):
    return jnp.zeros((x.shape[0], probe.shape[1]), F32)


def _probe_only_fwd(x, probe):
    return jnp.zeros((x.shape[0], probe.shape[1]), F32), x


def _probe_only_bwd(x, dy):
    return jnp.zeros_like(x), _dot(x, dy, (((0,), (0,)), ((), ())))


_probe_only.defvjp(_probe_only_fwd, _probe_only_bwd)


@jax.custom_vjp
def _mm_t(x, wt):
    return jnp.zeros((x.shape[0], wt.shape[0]), F32)


_mm_t.defvjp(lambda x, wt: (jnp.zeros((x.shape[0], wt.shape[0]), F32), wt),
             lambda wt, dy: (_dot(dy, wt), jnp.zeros_like(wt)))


@jax.custom_vjp
def _mm_both(x, w, wt):
    return _dot(x, w)


_mm_both.defvjp(lambda x, w, wt: (_dot(x, w), wt),
                lambda wt, dy: (_dot(dy, wt), jnp.zeros((wt.shape[1], wt.shape[0]), wt.dtype), jnp.zeros_like(wt)))


@jax.custom_vjp
def _mm_both_probe(x, w, wt, probe):
    return _dot(x, w)


_mm_both_probe.defvjp(
    lambda x, w, wt, probe: (_dot(x, w), (x, wt)),
    lambda res, dy: (_dot(dy, res[1]), jnp.zeros((res[1].shape[1], res[1].shape[0]), res[1].dtype), jnp.zeros_like(res[1]),
                     _dot(res[0], dy, (((0,), (0,)), ((), ())))))


@jax.custom_vjp
def _mm_t_probe(x, wt, probe):
    return jnp.zeros((x.shape[0], wt.shape[0]), F32)


_mm_t_probe.defvjp(lambda x, wt, probe: (jnp.zeros((x.shape[0], wt.shape[0]), F32), (x, wt)),
                   lambda res, dy: (_dot(dy, res[1]), jnp.zeros_like(res[1]), _dot(res[0], dy, (((0,), (0,)), ((), ())))))


@jax.custom_vjp
def mmw(x, w):
    return _dot(x, w)


def _mmw_fwd(x, w):
    return _dot(x, w), (x, w)


def _mmw_bwd(res, dy):
    x, w = res
    return _dot(dy, w, (((1,), (1,)), ((), ()))), _dot(x, dy, (((0,), (0,)), ((), ())))


mmw.defvjp(_mmw_fwd, _mmw_bwd)


def rms(x, g):
    return x * lax.rsqrt(jnp.mean(x * x, axis=-1, keepdims=True) + EPS) * g


def _rows(shape):
    return lax.broadcasted_iota(jnp.int32, shape, 0)


def _shift_down(x, s, fill):
    return jnp.where(_rows(x.shape) < s, fill, pltpu.roll(x, s, 0))


def _shift_up(x, s, fill):
    n = x.shape[0]
    return jnp.where(_rows(x.shape) >= n - s, fill, pltpu.roll(x, n - s, 0))


def _make_tshift(s):
    @jax.custom_vjp
    def tshift(x):
        return _shift_down(x, s, 0.0)

    tshift.defvjp(lambda x: (_shift_down(x, s, 0.0), None), lambda _, dy: (_shift_up(dy, s, 0.0),))
    return tshift


_TSHIFT = {s: _make_tshift(s) for s in (1, 2, 3)}


def causal_conv4(x, w):
    y = x * w[3:4, :]
    for j in range(3):
        y = y + _TSHIFT[3 - j](x) * w[j:j + 1, :]
    return y


def _scan_steps(n):
    d = 1
    while d < n:
        yield d
        d *= 2


@jax.custom_vjp
def linscan(a, b):
    return _linscan_fwd_impl(a, b)


def _linscan_fwd_impl(a, b):
    for d in _scan_steps(a.shape[0]):
        b = a * _shift_down(b, d, 0.0) + b
        a = a * _shift_down(a, d, 1.0)
    return b


def _linscan_fwd(a, b):
    h = _linscan_fwd_impl(a, b)
    return h, (a, h)


def _linscan_bwd(res, dh):
    a, h = res
    an = _shift_up(a, 1, 0.0)
    lam = dh
    for d in _scan_steps(a.shape[0]):
        lam = an * _shift_up(lam, d, 0.0) + lam
        an = an * _shift_up(an, d, 1.0)
    return lam * _shift_down(h, 1, 0.0), lam


linscan.defvjp(_linscan_fwd, _linscan_bwd)


def _chunk_scan(x, reverse):
    pos = _rows(x.shape) % CHUNK
    n = x.shape[0]
    d = 1
    while d < CHUNK:
        if reverse:
            x = x + jnp.where(pos < CHUNK - d, pltpu.roll(x, n - d, 0), 0.0)
        else:
            x = x + jnp.where(pos >= d, pltpu.roll(x, d, 0), 0.0)
        d *= 2
    return x


@jax.custom_vjp
def chunk_cumsum(x):
    return _chunk_scan(x, False)


@jax.custom_vjp
def chunk_revcumsum(x):
    return _chunk_scan(x, True)


chunk_cumsum.defvjp(lambda x: (_chunk_scan(x, False), None), lambda _, g: (_chunk_scan(g, True),))
chunk_revcumsum.defvjp(lambda x: (_chunk_scan(x, True), None), lambda _, g: (_chunk_scan(g, False),))


def _bmm(a, b, precision=None):
    return jnp.einsum('nij,njk->nik', a, b, precision=precision, preferred_element_type=F32)


@jax.custom_vjp
def inv_unit_lower(l):
    return _inv_impl(l)


def _inv_impl(l):
    n = l.shape[-1]
    eye = (_rows((n, n)) == lax.broadcasted_iota(jnp.int32, (n, n), 1)).astype(F32)
    p = -l
    a = eye + p
    k = 1
    while 2 * k < n:
        p = _bmm(p, p, HI)
        a = a + _bmm(a, p, HI)
        k *= 2
    return a


def _inv_fwd(l):
    a = _inv_impl(l)
    return a, a


def _inv_bwd(a, da):
    at = jnp.swapaxes(a, 1, 2)
    return (-_bmm(_bmm(at, da, HI), at, HI),)


inv_unit_lower.defvjp(_inv_fwd, _inv_bwd)


@jax.custom_vjp
def inv_unit_lower_known(l, a):
    return a


inv_unit_lower_known.defvjp(lambda l, a: (a, a), lambda a, da: (_inv_bwd(a, da)[0], jnp.zeros_like(a)))


def neg_expm1(y):
    series = -(y * (1.0 + y * (0.5 + y * (1.0 / 6.0 + y * (1.0 / 24.0)))))
    return jnp.where(y > -0.05, series, 1.0 - jnp.exp(y))


def run_stage(name, fn, *, tok, tok_nd=(), ex=(), par=(), par_tiled=(), big=(), out_ch, ts, nc=1, cot=None, addin=None,
              which="all", dtok_dtype=None, side=None, big_t=None, mm_value=False):
    tok, tok_nd, ex, par, par_tiled, big = map(list, (tok, tok_nd, ex, par, par_tiled, big))
    big_layer = [b[1] if isinstance(b, tuple) else None for b in big]
    big_arrays = [b[0] if isinstance(b, tuple) else b for b in big]
    big = [jax.ShapeDtypeStruct(a.shape if lyr is None else a.shape[1:], a.dtype) for a, lyr in zip(big_arrays, big_layer)]
    bsz, seq, _ = tok[0].shape
    ts = min(ts, seq)
    ns = seq // ts
    grid = (nc, bsz, ns)

    def tok_spec(a):
        cb = a.shape[-1] // nc
        return pl.BlockSpec((None, ts, cb), lambda c, b, s: (b, s, c))

    def ex_spec(a):
        cb = a.shape[-1] // nc
        return pl.BlockSpec((None, 1, cb), lambda c, b, s: (b, 0, c))

    def full_spec(a, single=False):
        nd = a.ndim
        kw = dict(pipeline_mode=pl.Buffered(1)) if single else {}
        return pl.BlockSpec(a.shape, lambda c, b, s: (0,) * nd, **kw)

    def tiled_spec(a):
        if a.ndim == 2:
            return pl.BlockSpec((a.shape[0], a.shape[1] // nc), lambda c, b, s: (0, c))
        return pl.BlockSpec((None,) + a.shape[1:], lambda c, b, s: (c, 0, 0))

    def big_spec(a, lyr):
        if lyr is None:
            return full_spec(a, True)
        nd = a.ndim
        return pl.BlockSpec((None,) + a.shape[1:], lambda c, b, s: (lyr,) + (0,) * (nd - 1), pipeline_mode=pl.Buffered(1))

    n_tok, n_nd, n_ex, n_par, n_pt, n_big = map(len, (tok, tok_nd, ex, par, par_tiled, big))
    in_arrays = tok + tok_nd + ex + par + par_tiled + big_arrays
    in_specs = ([tok_spec(a) for a in tok + tok_nd] + [ex_spec(a) for a in ex] + [full_spec(a) for a in par]
                + [tiled_spec(a) for a in par_tiled] + [big_spec(a, lyr) for a, lyr in zip(big_arrays, big_layer)])
    out_tok_shapes = [jax.ShapeDtypeStruct((bsz, seq, ch), F32) for ch in out_ch]
    n_in = len(in_arrays)

    def split(vals):
        i = 0
        groups = []
        for n in (n_tok, n_nd, n_ex, n_par, n_pt, n_big):
            groups.append(list(vals[i:i + n]))
            i += n
        return groups

    def split_grads(vals):
        i = 0
        groups = []
        for n in (n_tok, n_ex, n_par, n_pt, n_big):
            groups.append(list(vals[i:i + n]))
            i += n
        return groups

    side = list(side or [])
    side_arrays = [a for job in side for a in job[0]]
    n_side = len(side_arrays)
    side_shapes = [jax.ShapeDtypeStruct(a.shape, a.dtype) for a in side_arrays]
    side_scratch = [pltpu.SemaphoreType.DMA((job[1],)) for job in side for _ in range(2)]

    def side_jobs(side_refs, sems):
        o = 0
        for j, (arrs, _, issue) in enumerate(side):
            yield issue(side_refs[o:o + len(arrs)], sems[2 * j], sems[2 * j + 1])
            o += len(arrs)

    def side_start(side_refs, sems):
        if side:
            c, b, s = pl.program_id(0), pl.program_id(1), pl.program_id(2)

            @pl.when(jnp.logical_and(jnp.logical_and(c == 0, b == 0), s == 0))
            def _():
                for starts, _, _ in side_jobs(side_refs, sems):
                    for cp in starts:
                        cp.start()

    def side_finish(side_refs, sems):
        if side:
            c, b, s = pl.program_id(0), pl.program_id(1), pl.program_id(2)

            @pl.when(jnp.logical_and(jnp.logical_and(c == nc - 1, b == bsz - 1), s == ns - 1))
            def _():
                for _, recv_waits, send_waits in side_jobs(side_refs, sems):
                    for cp in recv_waits:
                        cp.wait_recv()
                    for cp in send_waits:
                        cp.wait_send()

    if cot is None:
        n_out = len(out_tok_shapes)

        def body(*refs):
            tv, ndv, ev, pv, ptv, _ = split([r[...] for r in refs[:n_in - n_big]] + [None] * n_big)
            b_refs = refs[n_in - n_big:n_in]
            side_refs = refs[n_in + n_side + n_out:n_in + 2 * n_side + n_out]
            sems = refs[n_in + 2 * n_side + n_out:]
            side_start(side_refs, sems)
            outs = fn(tv, ndv, ev, pv, ptv, lambda x, i, j=None: _dot(x, b_refs[i][...] if j is None else b_refs[i][j]))
            for r, o in zip(refs[n_in + n_side:], outs):
                r[...] = o
            side_finish(side_refs, sems)

        res = pl.pallas_call(
            body, name=name, grid=grid, in_specs=in_specs + [_ANY] * n_side,
            out_specs=[tok_spec(a) for a in out_tok_shapes] + [_ANY] * n_side,
            out_shape=out_tok_shapes + side_shapes, input_output_aliases={n_in + j: n_out + j for j in range(n_side)},
            scratch_shapes=side_scratch, compiler_params=_params(3))(*in_arrays, *side_arrays)
        return (res[:n_out], res[n_out:]) if side else res

    cot = list(cot)
    has_addin = addin is not None
    extra = cot + ([addin] if has_addin else [])
    n_cot = len(cot)
    want_small, want_big = which in ("all", "small"), which in ("all", "big")
    if not want_small:
        in_arrays, in_specs, n_in = in_arrays[:n_in - n_big], in_specs[:n_in - n_big], n_in - n_big
    elif big_t is not None:
        t_layer = [b[1] if isinstance(b, tuple) else None for b in big_t]
        t_arrays = [b[0] if isinstance(b, tuple) else b for b in big_t]
        keep = n_in if mm_value else n_in - n_big
        in_arrays = in_arrays[:keep] + t_arrays
        in_specs = in_specs[:keep] + [big_spec(a, lyr) for a, lyr in zip(t_arrays, t_layer)]
        n_in = len(in_arrays)
    small_arrays = tok + ex + par + par_tiled
    g_shapes = [jax.ShapeDtypeStruct(a.shape, F32) for a in (small_arrays if want_small else []) + (big if want_big else [])]
    for i, dt_ in (dtok_dtype or {}).items():
        g_shapes[i] = jax.ShapeDtypeStruct(g_shapes[i].shape, dt_)
    g_specs = (([tok_spec(a) for a in tok] + [ex_spec(a) for a in ex] + [full_spec(a) for a in par]
                + [tiled_spec(a) for a in par_tiled]) if want_small else []) + (
                    [full_spec(a, True) for a in big] if want_big else [])

    def body(*refs):
        c, b, s = pl.program_id(0), pl.program_id(1), pl.program_id(2)
        n_small_in = n_tok + n_nd + n_ex + n_par + n_pt
        tv, ndv, ev, pv, ptv, _ = split([r[...] for r in refs[:n_small_in]] + [None] * n_big)
        b_refs = refs[n_small_in:n_in]
        cots = [r[...].astype(F32) for r in refs[n_in:n_in + n_cot]]
        n_g = len(g_shapes)
        g_refs = list(refs[n_in + len(extra) + n_side:n_in + len(extra) + n_side + n_g])
        side_refs = refs[n_in + len(extra) + n_side + n_g:n_in + len(extra) + 2 * n_side + n_g]
        sems = refs[n_in + len(extra) + 2 * n_side + n_g:]
        side_start(side_refs, sems)
        probes = [jnp.zeros(w.shape, F32) if w.ndim == 2 else [jnp.zeros(w.shape[1:], F32) for _ in range(w.shape[0])]
                  for w in big]

        def f(tv_, ev_, pv_, ptv_, probes_):
            def mm(x, i, j=None):
                probe = None if probes_ is None else (probes_[i] if j is None else probes_[i][j])
                if not want_small:
                    return _probe_only(x, probe)
                w = b_refs[i][...] if j is None else b_refs[i][j]
                if big_t is not None and mm_value:
                    wt = b_refs[n_big + i][...] if j is None else b_refs[n_big + i][j]
                    return _mm_both(x, w, wt) if probe is None else _mm_both_probe(x, w, wt, probe)
                if big_t is not None:
                    return _mm_t(x, w) if probe is None else _mm_t_probe(x, w, probe)
                return _dot(x, w) if probe is None else _mm_probe(x, w, probe)

            return fn(tv_, ndv, ev_, pv_, ptv_, mm)

        dt = de = dp = dpt = dbg = ()
        if which == "all":
            dt, de, dp, dpt, dbg = jax.vjp(f, tv, ev, pv, ptv, probes)[1](cots)
        elif which == "small":
            dt, de, dp, dpt = jax.vjp(lambda *a: f(*a, None), tv, ev, pv, ptv)[1](cots)
        else:
            (dbg,) = jax.vjp(lambda p: f(tv, ev, pv, ptv, p), probes)[1](cots)
        if has_addin:
            dt = [dt[0] + refs[n_in + n_cot][...]] + list(dt[1:])
        if want_small:
            gt_r, ge_r, gp_r, gpt_r, gb_r = split_grads(g_refs + ([] if want_big else [None] * n_big))
        else:
            gt_r, ge_r, gp_r, gpt_r, gb_r = [], [], [], [], g_refs
        for r, g in zip(gt_r, dt):
            r[...] = g.astype(r.dtype)

        def accumulate(r, g, first):
            @pl.when(first)
            def _():
                r[...] = g

            @pl.when(jnp.logical_not(first))
            def _():
                r[...] += g

        for r, g in zip(ge_r, de):
            accumulate(r, g, s == 0)
        first_all = jnp.logical_and(jnp.logical_and(c == 0, b == 0), s == 0)
        for r, g in zip(gp_r, dp):
            accumulate(r, g, first_all)
        for r, g in zip(gpt_r, dpt):
            accumulate(r, g, jnp.logical_and(b == 0, s == 0))
        for r, g in zip(gb_r, dbg):
            if isinstance(g, (list, tuple)):
                for j, gj in enumerate(g):
                    accumulate(r.at[j], gj, first_all)
            else:
                accumulate(r, g, first_all)
        side_finish(side_refs, sems)

    n_args = n_in + len(extra)
    res = pl.pallas_call(
        body, name=name + "_bwd" + ("" if which == "all" else "_" + which), grid=grid,
        in_specs=in_specs + [tok_spec(a) for a in extra] + [_ANY] * n_side, out_specs=g_specs + [_ANY] * n_side,
        out_shape=g_shapes + side_shapes, input_output_aliases={n_args + j: len(g_shapes) + j for j in range(n_side)},
        scratch_shapes=side_scratch, compiler_params=_params(3))(*in_arrays, *extra, *side_arrays)
    res, side_out = list(res[:len(g_shapes)]), list(res[len(g_shapes):])
    groups = [[], [], [], [], res] if not want_small else split_grads(res + ([] if want_big else [None] * n_big))
    return (groups, side_out) if side else groups


PROJ_PIECES = (("qkv", 768), ("z", 256), ("rx", 512), ("rgate", 512), ("mq", 256), ("mkv", 128), ("misc", 128))
PROJ_WIDTH = sum(w for _, w in PROJ_PIECES)
MISC_KR, MISC_A, MISC_B = 0, 32, 36


def fn_mixer_in(tok, nd, ex, par, pt, mm):
    (h,), (sc, sh), (g,) = tok, ex, par
    proj = mm(rms(h, g) * (1.0 + sc) + sh, 0)
    outs, o = [], 0
    for _, w in PROJ_PIECES:
        outs.append(proj[:, o:o + w])
        o += w
    return outs


def fn_gdn_conv(tok, nd, ex, par, pt, mm):
    return [jax.nn.silu(causal_conv4(tok[0], pt[0]))]


def _tri_masks():
    r = _rows((CHUNK, CHUNK))
    c = lax.broadcasted_iota(jnp.int32, (CHUNK, CHUNK), 1)
    return (c <= r), (c < r)


def fn_gdn_local(tok, nd, ex, par, pt, mm):
    (qkv, misc), (a_log, dt_bias) = tok, par
    known = nd[0] if nd else None
    ts = qkv.shape[0]
    nb = ts // CHUNK
    lower, strict = _tri_masks()
    g_all = -jnp.exp(a_log) * jax.nn.softplus(misc[:, MISC_A:MISC_A + GDN_HEADS] + dt_bias)
    g_cum = chunk_cumsum(g_all)
    g_tot = g_cum + chunk_revcumsum(g_all) - g_all
    outs = [[] for _ in range(7)]
    for hd in range(GDN_HEADS):
        def head(x, base):
            return x[:, base + 64 * hd: base + 64 * hd + 64]

        def l2n(x):
            return x * lax.rsqrt(jnp.sum(x * x, axis=-1, keepdims=True) + EPS)

        q = (l2n(head(qkv, 0)) * (64.0 ** -0.5)).reshape(nb, CHUNK, 64)
        k = l2n(head(qkv, 256)).reshape(nb, CHUNK, 64)
        v = head(qkv, 512).reshape(nb, CHUNK, 64)
        b = misc[:, MISC_B + hd: MISC_B + hd + 1]
        beta = jax.nn.sigmoid(b).reshape(nb, CHUNK, 1)
        gi = jnp.broadcast_to(g_cum[:, hd:hd + 1].reshape(nb, CHUNK, 1), (nb, CHUNK, CHUNK))
        gl = jnp.broadcast_to(g_tot[:, hd:hd + 1].reshape(nb, CHUNK, 1), (nb, CHUNK, CHUNK))
        diff = gi - jnp.swapaxes(gi, 1, 2)
        decay = jnp.where(lower, jnp.exp(jnp.where(lower, diff, 0.0)), 0.0)
        kb = k * beta
        vb = v * beta
        kk = jnp.einsum('ncd,nsd->ncs', kb.astype(BF16), k.astype(BF16), preferred_element_type=F32)
        lmat = jnp.where(strict, kk * decay, 0.0)
        if known is None:
            amat = inv_unit_lower(lmat)
        else:
            amat = inv_unit_lower_known(lmat, known[:, 64 * hd: 64 * hd + 64].reshape(nb, CHUNK, 64))
        eg = jnp.exp(gi)
        u = _bmm(amat, vb, HI)
        w = _bmm(amat, kb * eg, HI)
        qk = jnp.einsum('ncd,nsd->ncs', q.astype(BF16), k.astype(BF16), preferred_element_type=F32) * decay
        qd = q * eg
        kt = k * jnp.exp(gl - gi)
        cd = jnp.exp(gl)
        for lst, val in zip(outs, (qk, qd, u, w, kt, cd) + (() if known is not None else (amat,))):
            lst.append(val.reshape(ts, 64))
    return [jnp.concatenate(lst, axis=-1) for lst in outs if lst]


def fn_rglru(tok, nd, ex, par, pt, mm):
    (rx, rgate), (conv_w, conv_b, b_a, b_x, lam, bd_a, bd_x) = tok, pt
    xc = causal_conv4(rx, conv_w) + conv_b
    r = jax.nn.sigmoid(mmw(xc, bd_a) + b_a)
    i = jax.nn.sigmoid(mmw(xc, bd_x) + b_x)
    log_a = -RG_C * r * jax.nn.softplus(-lam)
    a = jnp.exp(log_a)
    bterm = jnp.sqrt(neg_expm1(2.0 * log_a)) * (i * xc)
    return [linscan(a, bterm) * jax.nn.gelu(rgate)]


def _rope32(x, cos, sin):
    x1, x2 = x[:, :16], x[:, 16:32]
    return jnp.concatenate([x1 * cos - x2 * sin, x2 * cos + x1 * sin], axis=-1)


MLA_QK = 96


def fn_mla_pre(tok, nd, ex, par, pt, mm):
    (mq, mkv, misc), (cs,), (g_q, g_kv, w_q, w_kv) = tok, nd, par
    q = mmw(rms(mq, g_q), w_q)
    kv = mmw(rms(mkv, g_kv), w_kv)
    cos, sin = cs[:, 0:16], cs[:, 16:32]
    kp = _rope32(misc[:, MISC_KR:MISC_KR + 32], cos, sin)
    qs, ks, vs = [], [], []
    for h in range(MLA_HEADS):
        qs += [q[:, MLA_QK * h: MLA_QK * h + 64], _rope32(q[:, MLA_QK * h + 64: MLA_QK * h + 96], cos, sin)]
        ks += [kv[:, 128 * h: 128 * h + 64], kp]
        vs.append(kv[:, 128 * h + 64: 128 * h + 128])
    return [jnp.concatenate(qs, axis=-1), jnp.concatenate(ks, axis=-1), jnp.concatenate(vs, axis=-1)]


def fn_out_proj(tok, nd, ex, par, pt, mm):
    (h, o_a, o_b, o_c), (gt,) = tok, ex
    return [h + gt * mm(jnp.concatenate([o_a, o_b, o_c], axis=-1), 0)]


def fn_mlp_in(tok, nd, ex, par, pt, mm):
    (h,), (sc, sh), (g,) = tok, ex, par
    u = rms(h, g) * (1.0 + sc) + sh
    return [jnp.concatenate([mm(u, 0, j) for j in range(N_CHIP)], axis=-1)]


def fn_mlp_out(tok, nd, ex, par, pt, mm):
    (h, a), (gt,) = tok, ex
    return [h + gt * mm(jnp.square(jax.nn.relu(a)), 0)]


GDN_W = GDN_HEADS * 64


def _head_mask():
    r = _rows((GDN_W, GDN_W)) // 64
    c = lax.broadcasted_iota(jnp.int32, (GDN_W, GDN_W), 1) // 64
    return r == c


def _heads_diag(x):
    return jnp.where(_head_mask(), jnp.concatenate([x] * GDN_HEADS, axis=0), 0.0)


def _heads_compact(s):
    return s[0:64] + s[64:128] + s[128:192] + s[192:256]


def _gdn_step(state, qk, qd, u, w, kt, cd, z, norm_g):
    v_new = u - _dot(w, state)
    o = _dot(qd, state) + _dot(qk, _heads_diag(v_new))
    update = _dot(kt, v_new, (((0,), (0,)), ((), ())))
    new_state = state * jnp.broadcast_to(cd[0:1, :], (GDN_W, GDN_W)) + jnp.where(_head_mask(), update, 0.0)
    outs = [rms(o[:, 64 * hd: 64 * hd + 64], norm_g) * jax.nn.silu(z[:, 64 * hd: 64 * hd + 64]) for hd in range(GDN_HEADS)]
    return new_state, jnp.concatenate(outs, axis=-1)


def gdn_scan(xs, z, norm_g):
    bsz, seq, _ = z.shape
    n = seq // CHUNK
    blk = pl.BlockSpec((bsz, CHUNK, 256), lambda i: (0, i, 0))

    def body(qk, qd, u, w, kt, cd, z_ref, g_ref, o_ref, st_out, st):
        @pl.when(pl.program_id(0) == 0)
        def _():
            st[...] = jnp.zeros_like(st)

        for b in range(bsz):
            state = st[b]
            st_out[b] = _heads_compact(state)
            st[b], o_ref[b] = _gdn_step(state, qk[b], qd[b], u[b], w[b], kt[b], cd[b], z_ref[b], g_ref[...])

    return pl.pallas_call(
        body, name="gdn_scan", grid=(n,), in_specs=[blk] * 7 + [pl.BlockSpec((1, 64), lambda i: (0, 0))],
        out_specs=[blk, blk], out_shape=[jax.ShapeDtypeStruct((bsz, seq, 256), F32)] * 2,
        scratch_shapes=[pltpu.VMEM((bsz, GDN_W, GDN_W), F32)], compiler_params=_params(1))(*xs, z, norm_g)


def gdn_scan_bwd(xs, z, norm_g, st_in, do):
    bsz, seq, _ = z.shape
    n = seq // CHUNK
    blk = pl.BlockSpec((bsz, CHUNK, 256), lambda i: (0, n - 1 - i, 0))
    gspec = pl.BlockSpec((1, 64), lambda i: (0, 0))

    def body(qk, qd, u, w, kt, cd, z_ref, g_ref, st_ref, do_ref, dqk, dqd, du, dw, dkt, dcd, dz, dg, dst):
        first = pl.program_id(0) == 0

        @pl.when(first)
        def _():
            dst[...] = jnp.zeros_like(dst)

        dg_sum = None
        for b in range(bsz):
            _, vjp = jax.vjp(_gdn_step, _heads_diag(st_ref[b]), qk[b], qd[b], u[b], w[b], kt[b], cd[b], z_ref[b], g_ref[...])
            grads = vjp((dst[b], do_ref[b]))
            dst[b] = jnp.where(_head_mask(), grads[0], 0.0)
            for r, g in zip((dqk, dqd, du, dw, dkt, dcd, dz), grads[1:8]):
                r[b] = g
            dg_sum = grads[8] if dg_sum is None else dg_sum + grads[8]

        @pl.when(first)
        def _():
            dg[...] = dg_sum

        @pl.when(jnp.logical_not(first))
        def _():
            dg[...] += dg_sum

    res = pl.pallas_call(
        body, name="gdn_scan_bwd", grid=(n,), in_specs=[blk] * 7 + [gspec, blk, blk],
        out_specs=[blk] * 7 + [gspec], out_shape=[jax.ShapeDtypeStruct((bsz, seq, 256), F32)] * 7
        + [jax.ShapeDtypeStruct((1, 64), F32)],
        scratch_shapes=[pltpu.VMEM((bsz, GDN_W, GDN_W), F32)], compiler_params=_params(1))(*xs, z, norm_g, st_in, do)
    return list(res[:6]), res[6], res[7]


ATTN_TQ = 256
ATTN_SCALE = 96.0 ** -0.5
ATTN_KEY_FRACTIONS = (4, 2, 1)


def _attn_head(q, k, v, q0):
    s = _dot(q, k, (((1,), (1,)), ((), ()))) * ATTN_SCALE
    qc = (q0 + _rows(s.shape)) // CHUNK
    kc = lax.broadcasted_iota(jnp.int32, s.shape, 1) // CHUNK
    s = jnp.where(kc <= qc, s, -1e30)
    p = jnp.exp(s - jnp.max(s, axis=-1, keepdims=True))
    p = p / jnp.sum(p, axis=-1, keepdims=True)
    return _dot(p, v)


def _key_lengths(seq):
    return sorted({max(ATTN_TQ, seq // f) for f in ATTN_KEY_FRACTIONS})


def _key_variant(i, seq):
    need = (i + 1) * ATTN_TQ
    return sum(((need > klen).astype(jnp.int32) for klen in _key_lengths(seq)[:-1]), jnp.int32(0))


ATTN_QW, ATTN_VW = MLA_HEADS * MLA_QK, MLA_HEADS * 64


def _attn_specs(seq):
    def qspec(ch):
        return pl.BlockSpec((None, ATTN_TQ, ch), lambda b, i: (b, i, 0))

    def kspec(ch):
        return pl.BlockSpec((None, seq, ch), lambda b, i: (b, 0, 0))

    return qspec, kspec


def mla_attention(q, k, v):
    bsz, seq, _ = q.shape
    qspec, kspec = _attn_specs(seq)

    def body(q_r, k_r, v_r, o_r):
        i = pl.program_id(1)
        q0 = i * ATTN_TQ

        def with_keys(klen):
            outs = [_attn_head(q_r[:, MLA_QK * h: MLA_QK * h + MLA_QK], k_r[0:klen, MLA_QK * h: MLA_QK * h + MLA_QK],
                               v_r[0:klen, 64 * h: 64 * h + 64], q0) for h in range(MLA_HEADS)]
            o_r[...] = jnp.concatenate(outs, axis=-1)

        for j, klen in enumerate(_key_lengths(seq)):
            pl.when(_key_variant(i, seq) == j)(functools.partial(with_keys, klen))

    return pl.pallas_call(
        body, name="mla_attention", grid=(bsz, seq // ATTN_TQ), in_specs=[qspec(ATTN_QW), kspec(ATTN_QW), kspec(ATTN_VW)],
        out_specs=qspec(ATTN_VW), out_shape=jax.ShapeDtypeStruct((bsz, seq, ATTN_VW), F32), compiler_params=_params(2))(
            q, k, v)


def mla_attention_bwd(q, k, v, do):
    bsz, seq, _ = q.shape
    qspec, kspec = _attn_specs(seq)

    def body(q_r, k_r, v_r, do_r, dq_r, dk_r, dv_r):
        i = pl.program_id(1)
        q0 = i * ATTN_TQ

        @pl.when(i == 0)
        def _():
            dk_r[...] = jnp.zeros_like(dk_r)
            dv_r[...] = jnp.zeros_like(dv_r)

        def with_keys(klen):
            dq, dk, dv = [], [], []
            for h in range(MLA_HEADS):
                qk = slice(MLA_QK * h, MLA_QK * h + MLA_QK)
                sl = slice(64 * h, 64 * h + 64)
                _, vjp = jax.vjp(functools.partial(_attn_head, q0=q0), q_r[:, qk], k_r[0:klen, qk], v_r[0:klen, sl])
                a, b, c = vjp(do_r[:, sl])
                dq.append(a)
                dk.append(b)
                dv.append(c)
            dq_r[...] = jnp.concatenate(dq, axis=-1)
            dk_r[0:klen, :] += jnp.concatenate(dk, axis=-1)
            dv_r[0:klen, :] += jnp.concatenate(dv, axis=-1)

        for j, klen in enumerate(_key_lengths(seq)):
            pl.when(_key_variant(i, seq) == j)(functools.partial(with_keys, klen))

    shp = lambda ch: jax.ShapeDtypeStruct((bsz, seq, ch), F32)
    return pl.pallas_call(
        body, name="mla_attention_bwd", grid=(bsz, seq // ATTN_TQ),
        in_specs=[qspec(ATTN_QW), kspec(ATTN_QW), kspec(ATTN_VW), qspec(ATTN_VW)],
        out_specs=[qspec(ATTN_QW), kspec(ATTN_QW), kspec(ATTN_VW)],
        out_shape=[shp(ATTN_QW), shp(ATTN_QW), shp(ATTN_VW)], compiler_params=_params(2))(q, k, v, do)


LOSS_TS = 512


def loss_head(h, g, target):
    bsz, seq, d = h.shape
    ts = min(LOSS_TS, seq)
    tok = pl.BlockSpec((None, ts, d), lambda b, s: (b, s, 0))
    gspec = pl.BlockSpec((1, d), lambda b, s: (0, 0))
    lspec = pl.BlockSpec((1, 128), lambda b, s: (0, 0))

    def body(h_r, g_r, t_r, loss_r, dh_r, dg_r):
        first = jnp.logical_and(pl.program_id(0) == 0, pl.program_id(1) == 0)
        tv = t_r[...]

        def f(hv, gv):
            return 0.5 * jnp.sum(jnp.mean(jnp.square(rms(hv, gv) - tv), axis=-1, keepdims=True), axis=0, keepdims=True)

        val, vjp = jax.vjp(f, h_r[...], g_r[...])
        dh, dg = vjp(jnp.ones((1, 1), F32))
        dh_r[...] = dh
        lv = jnp.broadcast_to(val, (1, 128))

        @pl.when(first)
        def _():
            loss_r[...] = lv
            dg_r[...] = dg

        @pl.when(jnp.logical_not(first))
        def _():
            loss_r[...] += lv
            dg_r[...] += dg

    return pl.pallas_call(
        body, name="loss_head", grid=(bsz, seq // ts), in_specs=[tok, gspec, tok], out_specs=[lspec, tok, gspec],
        out_shape=[jax.ShapeDtypeStruct((1, 128), F32), jax.ShapeDtypeStruct(h.shape, F32), jax.ShapeDtypeStruct((1, d), F32)],
        compiler_params=_params(2))(h, g, target)


def _adamw_math(w, g, m, v):
    m = ADAM_B1 * m + (1.0 - ADAM_B1) * g
    v = ADAM_B2 * v + (1.0 - ADAM_B2) * jnp.square(g)
    m_hat = m / (1.0 - ADAM_B1 ** ADAM_STEP)
    v_hat = v / (1.0 - ADAM_B2 ** ADAM_STEP)
    return -ADAM_LR * (m_hat / (jnp.sqrt(v_hat) + ADAM_EPS) + ADAM_WD * w), m, v


def _row_block(rows, cols):
    want = max(8, (1 << 18) // cols)
    best = rows
    for r in range(8, rows + 1, 8):
        if rows % r == 0 and r <= want:
            best = r
    return best if rows % 8 == 0 else rows


def adamw(name, w, g, m, v):
    rows, cols = w.shape
    rb = _row_block(rows, cols)
    spec = pl.BlockSpec((rb, cols), lambda i: (i, 0))

    def body(w_r, g_r, m_r, v_r, d_o, m_o, v_o):
        d, mn, vn = _adamw_math(w_r[...], g_r[...], m_r[...], v_r[...])
        d_o[...] = d
        m_o[...] = mn
        v_o[...] = vn

    return pl.pallas_call(body, name=name, grid=(rows // rb,), in_specs=[spec] * 4, out_specs=[spec] * 3,
                          out_shape=[jax.ShapeDtypeStruct(w.shape, F32)] * 3, compiler_params=_params(1))(w, g, m, v)


def adamw_reduce(name, parts, w, m, v):
    rows, cols = w.shape
    rb = _row_block(rows, cols)
    spec = pl.BlockSpec((rb, cols), lambda i: (i, 0))
    pspec = pl.BlockSpec((N_DEV, rb, cols), lambda i: (0, i, 0))

    def body(p_r, w_r, m_r, v_r, g_o, d_o, m_o, v_o):
        g = p_r[0]
        for k in range(1, N_DEV):
            g = g + p_r[k]
        d, mn, vn = _adamw_math(w_r[...], g, m_r[...], v_r[...])
        g_o[...] = g
        d_o[...] = d
        m_o[...] = mn
        v_o[...] = vn

    return pl.pallas_call(body, name=name, grid=(rows // rb,), in_specs=[pspec, spec, spec, spec], out_specs=[spec] * 4,
                          out_shape=[jax.ShapeDtypeStruct(w.shape, F32)] * 4, compiler_params=_params(1))(parts, w, m, v)


MOD_CB = 512


def mod_matmul(c_rows, w_mod, b_mod):
    nl, d, cols = w_mod.shape

    def body(c_r, w_r, b_r, o_r):
        o_r[...] = _dot(jax.nn.silu(c_r[...]), w_r[...]) + b_r[...]

    return pl.pallas_call(
        body, name="mod_matmul", grid=(nl, cols // MOD_CB),
        in_specs=[pl.BlockSpec((8, d), lambda l, j: (0, 0)), pl.BlockSpec((None, d, MOD_CB), lambda l, j: (l, 0, j)),
                  pl.BlockSpec((None, 1, MOD_CB), lambda l, j: (l, 0, j))],
        out_specs=pl.BlockSpec((None, 8, MOD_CB), lambda l, j: (l, 0, j)),
        out_shape=jax.ShapeDtypeStruct((nl, 8, cols), F32), compiler_params=_params(2))(c_rows, w_mod, b_mod)


def mod_weight_grad(c_all, dmod):
    nl, nb, cols = dmod.shape
    d = c_all.shape[1]

    def body(c_r, g_r, o_r):
        o_r[...] = _dot(jax.nn.silu(c_r[...]), g_r[...], (((0,), (0,)), ((), ())))

    return pl.pallas_call(
        body, name="mod_weight_grad", grid=(nl, cols // MOD_CB),
        in_specs=[pl.BlockSpec((nb, d), lambda l, j: (0, 0)), pl.BlockSpec((None, nb, MOD_CB), lambda l, j: (l, 0, j))],
        out_specs=pl.BlockSpec((None, d, MOD_CB), lambda l, j: (l, 0, j)),
        out_shape=jax.ShapeDtypeStruct((nl, d, cols), F32), compiler_params=_params(2))(c_all, dmod)


def _half_block(hr, cols):
    rb = _row_block(hr, cols)
    return rb if rb % 16 == 0 else hr


def add_half(name, g, s, core):
    _, r, cols = g.shape
    hr = r // 2
    rb = _half_block(hr, cols)
    nblk = hr // rb
    gspec = pl.BlockSpec((None, rb, cols), lambda k, i, c: (k, c[0] * nblk + i, 0))
    spec = pl.BlockSpec((None, rb, cols), lambda k, i, c: (k, i, 0))

    def body(c_r, g_r, s_r, o_r, ob_r):
        t = g_r[...] + s_r[...]
        o_r[...] = t
        ob_r[...] = t.astype(BF16)

    return pl.pallas_call(
        body, name=name, grid_spec=pltpu.PrefetchScalarGridSpec(num_scalar_prefetch=1, grid=(N_CHIP, nblk),
                                                                in_specs=[gspec, spec], out_specs=[spec, spec]),
        out_shape=[jax.ShapeDtypeStruct((N_CHIP, hr, cols), F32), jax.ShapeDtypeStruct((N_CHIP, hr, cols), BF16)],
        compiler_params=_params(2))(core, g, s)


def sum_peers(name, p32, recv, ids, shard_shape, layer, acc=None):
    _, hr, cols = p32.shape
    rb = _half_block(hr, cols)
    nblk = hr // rb

    def slot(k):
        return pl.BlockSpec((None, rb, cols), lambda i, c: ((c[0] + k) % N_CHIP, i, 0))

    def body(c_r, o_r, r1, r2, r3, *rest):
        rest[-1][...] = ((o_r[...] + r1[...].astype(F32)) + r2[...].astype(F32)) + r3[...].astype(F32)

    args = (ids, p32, recv, recv, recv) + (() if acc is None else (acc,))
    return pl.pallas_call(
        body, name=name, grid_spec=pltpu.PrefetchScalarGridSpec(
            num_scalar_prefetch=1, grid=(nblk,),
            in_specs=[slot(0), slot(1), slot(2), slot(3)] + ([] if acc is None else [_ANY]),
            out_specs=pl.BlockSpec((None, rb, cols), lambda i, c: (layer, c[1] * nblk + i, 0))),
        out_shape=jax.ShapeDtypeStruct(shard_shape, F32), input_output_aliases={} if acc is None else {5: 0},
        compiler_params=_params(1))(*args)


def cast_into_slab(name, w, ids, layer=None):
    nl, r, cols = w.shape
    hr = r // 2
    rb = _half_block(hr, cols)
    nblk = hr // rb

    def body(c_r, w_r, o_r):
        o_r[...] = w_r[...].astype(BF16)

    if layer is not None:
        return pl.pallas_call(
            body, name=name, grid_spec=pltpu.PrefetchScalarGridSpec(
                num_scalar_prefetch=1, grid=(nblk,),
                in_specs=[pl.BlockSpec((None, rb, cols), lambda i, c: (layer, c[1] * nblk + i, 0))],
                out_specs=pl.BlockSpec((None, rb, cols), lambda i, c: (c[0], c[1] * nblk + i, 0))),
            out_shape=jax.ShapeDtypeStruct((N_CHIP, r, cols), BF16), compiler_params=_params(1))(ids, w)
    return pl.pallas_call(
        body, name=name, grid_spec=pltpu.PrefetchScalarGridSpec(
            num_scalar_prefetch=1, grid=(nl, nblk),
            in_specs=[pl.BlockSpec((None, rb, cols), lambda l, i, c: (l, c[1] * nblk + i, 0))],
            out_specs=pl.BlockSpec((None, None, rb, cols), lambda l, i, c: (l, c[0], c[1] * nblk + i, 0))),
        out_shape=jax.ShapeDtypeStruct((nl, N_CHIP, r, cols), BF16), compiler_params=_params(2))(ids, w)


def _me():
    return lax.axis_index("x"), lax.axis_index("y"), lax.axis_index("c")


def all_gather8(name, x_shard, in_vmem):
    m_per, n = x_shard.shape
    space = pltpu.VMEM if in_vmem else pl.ANY

    def body(x_ref, out_ref, send_sems, recv_sems, local_sem):
        x, y, c = _me()
        me, sibling = (x, y, c), (x, y, 1 - c)
        chips = [(1 - x, y), (x, 1 - y), (1 - x, 1 - y)]

        def rows(px, py, pc):
            return out_ref.at[pl.ds((4 * px + 2 * py + pc) * m_per, m_per), :]

        def copy(k, block, to, src=None):
            return pltpu.make_async_remote_copy(
                src_ref=rows(*block) if src is None else src, dst_ref=rows(*block), send_sem=send_sems.at[k],
                recv_sem=recv_sems.at[k], device_id=to, device_id_type=MESH)

        mine = pltpu.make_async_copy(x_ref, rows(*me), local_sem)
        mine.start()
        first = [copy(0, me, sibling, src=x_ref)]
        first += [copy(1 + j, me, (*chip, c), src=x_ref) for j, chip in enumerate(chips)]
        for cp in first:
            cp.start()
        passed = [copy(4 + j, (*chip, c), sibling) for j, chip in enumerate(chips)]
        for j, chip in enumerate(chips):
            copy(1 + j, (*chip, c), me).wait_recv()
            passed[j].start()
        copy(0, sibling, me).wait_recv()
        for j, chip in enumerate(chips):
            copy(4 + j, (*chip, 1 - c), me).wait_recv()
        for cp in first + passed:
            cp.wait_send()
        mine.wait()

    return pl.pallas_call(
        body, name=name, out_shape=jax.ShapeDtypeStruct((N_DEV * m_per, n), x_shard.dtype),
        in_specs=[pl.BlockSpec(memory_space=space)], out_specs=pl.BlockSpec(memory_space=space),
        scratch_shapes=[pltpu.SemaphoreType.DMA((7,)), pltpu.SemaphoreType.DMA((7,)), pltpu.SemaphoreType.DMA],
    )(x_shard)


_ANY = pl.BlockSpec(memory_space=pl.ANY)


def all_gather_weights(name, slabs):
    n = len(slabs)

    def body(*refs):
        outs = refs[n:2 * n]
        send_sems, recv_sems = refs[2 * n:]
        x, y, c = _me()
        me, sibling = (x, y, c), (x, y, 1 - c)
        chips = [(1 - x, y), (x, 1 - y), (1 - x, 1 - y)]

        def view(i, px, py, pc):
            hr = slabs[i].shape[2] // 2
            return outs[i].at[:, 2 * px + py, pl.ds(pc * hr, hr), :]

        def copy(i, k, block, to):
            return pltpu.make_async_remote_copy(
                src_ref=view(i, *block), dst_ref=view(i, *block), send_sem=send_sems.at[i, k],
                recv_sem=recv_sems.at[i, k], device_id=to, device_id_type=MESH)

        first = []
        for i in range(n):
            first.append(copy(i, 0, me, sibling))
            first += [copy(i, 1 + j, me, (*chip, c)) for j, chip in enumerate(chips)]
        for cp in first:
            cp.start()
        passed = []
        for j, chip in enumerate(chips):
            for i in range(n):
                copy(i, 1 + j, (*chip, c), me).wait_recv()
                passed.append(copy(i, 4 + j, (*chip, c), sibling))
                passed[-1].start()
        for i in range(n):
            copy(i, 0, sibling, me).wait_recv()
            for j, chip in enumerate(chips):
                copy(i, 4 + j, (*chip, 1 - c), me).wait_recv()
        for cp in first + passed:
            cp.wait_send()

    return pl.pallas_call(
        body, name=name, out_shape=[jax.ShapeDtypeStruct(s.shape, s.dtype) for s in slabs],
        in_specs=[_ANY] * n, out_specs=[_ANY] * n, input_output_aliases={i: i for i in range(n)},
        scratch_shapes=[pltpu.SemaphoreType.DMA((n, 7)), pltpu.SemaphoreType.DMA((n, 7))],
    )(*slabs)


def _slab_block(slab, px, py, pc):
    hr = slab.shape[1] // 2
    return slab.at[2 * px + py, pl.ds(pc * hr, hr), :]


def gather_over_ici(slab):
    def issue(refs, send_sems, recv_sems):
        (buf,) = refs
        x, y, c = _me()
        peers = [(x, y, 1 - c), (1 - x, y, c), (x, 1 - y, c), (1 - x, 1 - y, c)]

        def copy(k, block, to):
            return pltpu.make_async_remote_copy(
                src_ref=_slab_block(buf, *block), dst_ref=_slab_block(buf, *block), send_sem=send_sems.at[k],
                recv_sem=recv_sems.at[k], device_id=to, device_id_type=MESH)

        sends = [copy(k, (x, y, c), p) for k, p in enumerate(peers)]
        arrivals = [copy(k, p, (x, y, c)) for k, p in enumerate(peers)]
        return sends, arrivals, sends

    return ([slab], 4, issue)


def gather_over_d2d(slab):
    def issue(refs, send_sems, recv_sems):
        (buf,) = refs
        x, y, c = _me()
        chips = [(1 - x, y), (x, 1 - y), (1 - x, 1 - y)]

        def copy(k, block):
            return pltpu.make_async_remote_copy(
                src_ref=_slab_block(buf, *block), dst_ref=_slab_block(buf, *block), send_sem=send_sems.at[k],
                recv_sem=recv_sems.at[k], device_id=(x, y, 1 - c), device_id_type=MESH)

        sends = [copy(k, (*chip, c)) for k, chip in enumerate(chips)]
        arrivals = [copy(k, (*chip, 1 - c)) for k, chip in enumerate(chips)]
        return sends, arrivals, sends

    return ([slab], 3, issue)


def chip_exchange_job(ps):
    n = len(ps)

    def issue(refs, send_sems, recv_sems):
        ins, outs = refs[:n], refs[n:]
        mx, my, mc = _me()
        ci = 2 * mx + my
        chips = [(1 - mx, my), (mx, 1 - my), (1 - mx, 1 - my)]
        sends, arrivals = [], []
        for i in range(n):
            for k, (px, py) in enumerate(chips):
                sem = 3 * i + k
                sends.append(pltpu.make_async_remote_copy(
                    src_ref=ins[i].at[2 * px + py], dst_ref=outs[i].at[ci], send_sem=send_sems.at[sem],
                    recv_sem=recv_sems.at[sem], device_id=(px, py, mc), device_id_type=MESH))
                arrivals.append(pltpu.make_async_remote_copy(
                    src_ref=ins[i].at[ci], dst_ref=outs[i].at[2 * px + py], send_sem=send_sems.at[sem],
                    recv_sem=recv_sems.at[sem], device_id=(px, py, mc), device_id_type=MESH))
        return sends, arrivals, sends

    return (list(ps) + [lax.empty(p.shape, p.dtype) for p in ps], 3 * n, issue)


def grad_sibling_exchange(name, gs):
    n = len(gs)

    def body(*refs):
        ins, outs = refs[:n], refs[n:2 * n]
        send_sems, recv_sems = refs[2 * n:]
        mx, my, mc = _me()
        cps = []
        for i in range(n):
            hr = gs[i].shape[1] // 2
            cps.append(pltpu.make_async_remote_copy(
                src_ref=ins[i].at[:, pl.ds((1 - mc) * hr, hr), :], dst_ref=outs[i], send_sem=send_sems.at[i],
                recv_sem=recv_sems.at[i], device_id=(mx, my, 1 - mc), device_id_type=MESH))
            cps[-1].start()
        for cp in cps:
            cp.wait()

    return pl.pallas_call(
        body, name=name, out_shape=[jax.ShapeDtypeStruct((N_CHIP, g.shape[1] // 2, g.shape[2]), g.dtype) for g in gs],
        in_specs=[_ANY] * n, out_specs=[_ANY] * n,
        scratch_shapes=[pltpu.SemaphoreType.DMA((n,)), pltpu.SemaphoreType.DMA((n,))],
    )(*gs)


def grad_chip_exchange(name, ps):
    n = len(ps)

    def body(*refs):
        ins, outs = refs[:n], refs[n:2 * n]
        send_sems, recv_sems = refs[2 * n:]
        mx, my, mc = _me()
        ci = 2 * mx + my
        chips = [(1 - mx, my), (mx, 1 - my), (1 - mx, 1 - my)]
        sends = []
        for i in range(n):
            for k, (px, py) in enumerate(chips):
                sends.append(pltpu.make_async_remote_copy(
                    src_ref=ins[i].at[2 * px + py], dst_ref=outs[i].at[ci], send_sem=send_sems.at[i, k],
                    recv_sem=recv_sems.at[i, k], device_id=(px, py, mc), device_id_type=MESH))
                sends[-1].start()
        for i in range(n):
            for k, (px, py) in enumerate(chips):
                pltpu.make_async_remote_copy(
                    src_ref=ins[i].at[ci], dst_ref=outs[i].at[2 * px + py], send_sem=send_sems.at[i, k],
                    recv_sem=recv_sems.at[i, k], device_id=(px, py, mc), device_id_type=MESH).wait_recv()
        for cp in sends:
            cp.wait_send()

    return pl.pallas_call(
        body, name=name, out_shape=[jax.ShapeDtypeStruct(p.shape, p.dtype) for p in ps], in_specs=[_ANY] * n,
        out_specs=[_ANY] * n, scratch_shapes=[pltpu.SemaphoreType.DMA((n, 3)), pltpu.SemaphoreType.DMA((n, 3))],
    )(*ps)


def grad_half_exchange(name, shards):
    n = len(shards)

    def body(*refs):
        outs = refs[n:2 * n]
        send_sems, recv_sems = refs[2 * n:]
        mx, my, mc = _me()

        def copy(i, core):
            hr = shards[i].shape[1] // 2
            rows = outs[i].at[:, pl.ds(core * hr, hr), :]
            return pltpu.make_async_remote_copy(src_ref=rows, dst_ref=rows, send_sem=send_sems.at[i],
                                                recv_sem=recv_sems.at[i], device_id=(mx, my, 1 - mc), device_id_type=MESH)

        sends = [copy(i, mc) for i in range(n)]
        for cp in sends:
            cp.start()
        for i in range(n):
            copy(i, 1 - mc).wait_recv()
        for cp in sends:
            cp.wait_send()

    return pl.pallas_call(
        body, name=name, out_shape=[jax.ShapeDtypeStruct(s.shape, s.dtype) for s in shards], in_specs=[_ANY] * n,
        out_specs=[_ANY] * n, input_output_aliases={i: i for i in range(n)},
        scratch_shapes=[pltpu.SemaphoreType.DMA((n,)), pltpu.SemaphoreType.DMA((n,))],
    )(*shards)


WEIGHTS = ['w_mod', 'b_mod', 'norm_mix_g', 'w_in', 'gdn_conv_w', 'gdn_a_log', 'gdn_dt_bias', 'gdn_norm_g', 'rg_conv_w',
           'rg_conv_b', 'rg_w_a', 'rg_b_a', 'rg_w_x', 'rg_b_x', 'rg_lambda', 'mla_q_norm_g', 'mla_w_qb', 'mla_kv_norm_g',
           'mla_w_kvb', 'w_out', 'norm_mlp_g', 'w_mlp_in', 'w_mlp_out', 'final_norm_g']
SHARDED = {'w_in': 2, 'gdn_conv_w': 2, 'rg_conv_w': 2, 'mla_w_qb': 2, 'mla_w_kvb': 2, 'w_out': 1, 'w_mlp_in': 2, 'w_mlp_out': 1}
GATHER_BF16 = ('w_in', 'mla_w_qb', 'mla_w_kvb', 'w_out', 'w_mlp_in', 'w_mlp_out')
GATHER_FIRST = GATHER_BF16[:4]
REPLICATED = [n for n in WEIGHTS if n not in SHARDED and n != 'w_mod']
PACK_COLS = 1024


def _pack(arrays, multiple):
    flat = jnp.concatenate([a.reshape(-1) for a in arrays])
    pad = (-flat.shape[0]) % multiple
    return jnp.pad(flat, (0, pad)) if pad else flat


def _unpack(flat, shapes):
    out, o = [], 0
    for shp in shapes:
        n = int(np.prod(shp))
        out.append(flat[o:o + n].reshape(shp))
        o += n
    return out


def _pack_rows(arrays):
    rows = []
    for a in arrays:
        flat = a.reshape(-1)
        pad = (-flat.shape[0]) % PACK_COLS
        rows.append((jnp.pad(flat, (0, pad)) if pad else flat).reshape(-1, PACK_COLS))
    out = jnp.concatenate(rows, axis=0)
    pad = (-out.shape[0]) % 8
    return jnp.pad(out, ((0, pad), (0, 0))) if pad else out


def _unpack_rows(packed, shapes):
    out, r = [], 0
    for shp in shapes:
        n = int(np.prod(shp))
        nr = -(-n // PACK_COLS)
        piece = packed[r:r + nr]
        out.append((piece if n == nr * PACK_COLS else piece.reshape(-1)[:n]).reshape(shp))
        r += nr
    return out


def _unshard(stacked, axis):
    moved = jnp.moveaxis(stacked, 0, axis)
    shp = list(moved.shape)
    shp[axis:axis + 2] = [shp[axis] * shp[axis + 1]]
    return moved.reshape(shp)


def _shard(full, axis):
    shp = list(full.shape)
    shp[axis:axis + 1] = [N_CHIP, shp[axis] // N_CHIP]
    return jnp.moveaxis(full.reshape(shp), axis, 0)


def _proj_cols(w):
    pad = jnp.zeros(w.shape[:-1] + (PROJ_WIDTH - w.shape[-1],), w.dtype)
    return jnp.concatenate([w[..., 0:1024], w[..., 1032:2472], w[..., 1024:1032], pad], axis=-1)


def _proj_cols_back(d):
    return jnp.concatenate([d[..., 0:1024], d[..., 2464:2472], d[..., 1024:2464]], axis=-1)


def _heads_split(w, heads, first):
    per = w.shape[-1] // heads
    r = w.reshape(w.shape[:-1] + (heads, per))
    lead = w.shape[:-1]
    return jnp.concatenate([r[..., :first].reshape(lead + (heads * first,)),
                            r[..., first:].reshape(lead + (heads * (per - first),))], axis=-1)


def _heads_merge(d, heads, first):
    lead = d.shape[:-1]
    per = d.shape[-1] // heads
    a = d[..., :heads * first].reshape(lead + (heads, first))
    b = d[..., heads * first:].reshape(lead + (heads, per - first))
    return jnp.concatenate([a, b], axis=-1).reshape(lead + (heads * per,))


def _block_diag(w):
    nl = w.shape[0]
    eye = jnp.eye(2, dtype=w.dtype)
    return jnp.einsum('lcoij,op->lcoipj', w.reshape(nl, 4, 2, 64, 64), eye).reshape(nl, 4, 128, 128)


def _block_diag_back(g):
    nl = g.shape[0]
    return jnp.einsum('lcoipj,op->lcoij', g.reshape(nl, 4, 2, 64, 2, 64), jnp.eye(2, dtype=g.dtype)).reshape(nl, 8, 64, 64)


def kernel(x, c, positions, w_mod, b_mod, norm_mix_g, w_in, gdn_conv_w, gdn_a_log, gdn_dt_bias, gdn_norm_g, rg_conv_w, rg_conv_b, rg_w_a, rg_b_a, rg_w_x, rg_b_x, rg_lambda, mla_q_norm_g, mla_w_qb, mla_kv_norm_g, mla_w_kvb, w_out, norm_mlp_g, w_mlp_in, w_mlp_out, final_norm_g, loss_target, m_w_mod, m_b_mod, m_norm_mix_g, m_w_in, m_gdn_conv_w, m_gdn_a_log, m_gdn_dt_bias, m_gdn_norm_g, m_rg_conv_w, m_rg_conv_b, m_rg_w_a, m_rg_b_a, m_rg_w_x, m_rg_b_x, m_rg_lambda, m_mla_q_norm_g, m_mla_w_qb, m_mla_kv_norm_g, m_mla_w_kvb, m_w_out, m_norm_mlp_g, m_w_mlp_in, m_w_mlp_out, m_final_norm_g, v_w_mod, v_b_mod, v_norm_mix_g, v_w_in, v_gdn_conv_w, v_gdn_a_log, v_gdn_dt_bias, v_gdn_norm_g, v_rg_conv_w, v_rg_conv_b, v_rg_w_a, v_rg_b_a, v_rg_w_x, v_rg_b_x, v_rg_lambda, v_mla_q_norm_g, v_mla_w_qb, v_mla_kv_norm_g, v_mla_w_kvb, v_w_out, v_norm_mlp_g, v_w_mlp_in, v_w_mlp_out, v_final_norm_g):
    given = dict(locals())
    wts = {n: given[n] for n in WEIGHTS}
    mom_m = {n: given["m_" + n] for n in WEIGHTS}
    mom_v = {n: given["v_" + n] for n in WEIGHTS}
    bsz, seq, d = x.shape
    depth = w_mod.shape[0]
    mx, my, mc = lax.axis_index("x"), lax.axis_index("y"), lax.axis_index("c")
    chip = 2 * mx + my
    dev = 2 * chip + mc

    conv_shapes = [wts['gdn_conv_w'].shape, wts['rg_conv_w'].shape]
    conv_flat = _pack([wts['gdn_conv_w'], wts['rg_conv_w']], d)
    conv_rows = conv_flat.shape[0] // d
    assert bsz + conv_rows <= 8
    c_pad = jnp.concatenate([c, conv_flat.reshape(conv_rows, d), jnp.zeros((8 - bsz - conv_rows, d), F32)], axis=0)
    gath = all_gather8("gather_c", c_pad, True).reshape(N_DEV, 8, d)
    c_all = gath[:, :bsz].reshape(N_DEV * bsz, d)
    conv_all = gath[0::2, bsz:bsz + conv_rows].reshape(N_CHIP, conv_rows * d)
    gdn_conv_full, rg_conv_full = [
        _unshard(jnp.stack([_unpack(conv_all[s], conv_shapes)[i] for s in range(N_CHIP)]), 2) for i in range(2)]

    n_half = N_DEV * bsz // 2
    mod_cols = w_mod.shape[2]
    c_rows = lax.dynamic_slice(c_all, (n_half * mc, 0), (n_half, d))
    b_mod_mine = lax.dynamic_slice(b_mod, (0, chip * mod_cols), (depth, mod_cols)).reshape(depth, 1, mod_cols)
    mod_piece = mod_matmul(c_rows, w_mod, b_mod_mine)
    mod_g = all_gather8("gather_mod", mod_piece.reshape(depth * n_half, mod_cols), True)
    mod_all = mod_g.reshape(N_CHIP, 2, depth, n_half, mod_cols).transpose(2, 1, 3, 0, 4).reshape(depth, 2 * n_half, 6 * d)
    mod_mine = lax.dynamic_slice(mod_all, (0, bsz * dev, 0), (depth, bsz, 6 * d)).reshape(depth, bsz, 6, 1, d)

    ids = jnp.stack([chip, mc]).astype(jnp.int32)
    slabs = dict(zip(GATHER_FIRST, all_gather_weights(
        "gather_weights", [cast_into_slab("cast_" + n, wts[n], ids) for n in GATHER_FIRST])))

    def columns(g):
        return g.transpose(0, 2, 1, 3).reshape(g.shape[0], g.shape[2], N_CHIP * g.shape[3])

    def rows_of(g):
        return g.reshape(g.shape[0], N_CHIP * g.shape[2], g.shape[3])

    w_cat = _proj_cols(columns(slabs['w_in']))
    w_q = columns(slabs['mla_w_qb']).astype(F32)
    w_kv = columns(slabs['mla_w_kvb']).astype(F32)
    w_out_full = rows_of(slabs['w_out'])
    w_cat_t, w_out_t = jnp.swapaxes(w_cat, 1, 2), jnp.swapaxes(w_out_full, 1, 2)
    bd_a, bd_x = _block_diag(rg_w_a), _block_diag(rg_w_x)

    inv_freq = ROPE_THETA ** (-jnp.arange(0, 32, 2, dtype=F32) / 32.0)
    ang = positions.astype(F32)[..., None] * inv_freq
    cs = jnp.concatenate([jnp.cos(ang), jnp.sin(ang)], axis=-1)

    proj_ch = [w for _, w in PROJ_PIECES]

    def row(a, l):
        return a[l].reshape(1, -1)

    def layer_args(l):
        sh_m, sc_m, gt_m, sh_f, sc_f, gt_f = (mod_mine[l, :, k] for k in range(6))
        return dict(
            mods=(sh_m, sc_m, gt_m, sh_f, sc_f, gt_f),
            mixer_in=dict(ex=[sc_m, sh_m], par=[row(norm_mix_g, l)], big=[(w_cat, l)], out_ch=proj_ch, ts=512),
            gdn_conv=dict(par_tiled=[gdn_conv_full[l]], out_ch=[768], ts=seq, nc=3),
            gdn_local=dict(par=[row(gdn_a_log, l), row(gdn_dt_bias, l)], out_ch=[256] * 7, ts=512),
            rglru=dict(par_tiled=[rg_conv_full[l], row(rg_conv_b, l), row(rg_b_a, l), row(rg_b_x, l), row(rg_lambda, l),
                                  bd_a[l], bd_x[l]], out_ch=[512], ts=seq, nc=4),
            mla_pre=dict(tok_nd=[cs], par=[row(mla_q_norm_g, l), row(mla_kv_norm_g, l), w_q[l], w_kv[l]],
                         out_ch=[ATTN_QW, ATTN_QW, ATTN_VW], ts=512),
            out_proj=dict(ex=[gt_m], big=[(w_out_full, l)], out_ch=[d], ts=512),
            mlp_in=dict(ex=[sc_f, sh_f], par=[row(norm_mlp_g, l)], big=[w_mi.get(l)], out_ch=[4 * d], ts=256),
            mlp_out=dict(ex=[gt_f], big=[w_mo.get(l)], out_ch=[d], ts=256),
        )

    mi_buf = [cast_into_slab("cast_w_mlp_in%d" % l, wts['w_mlp_in'], ids, layer=l) for l in range(depth)]
    mo_buf = [cast_into_slab("cast_w_mlp_out%d" % l, wts['w_mlp_out'], ids, layer=l) for l in range(depth)]
    w_mi, w_mo = {}, {}

    def staged(name, fn, jobs, **kw):
        return run_stage(name, fn, side=jobs, **kw) if jobs else (run_stage(name, fn, **kw), [])

    saved = []
    h = x
    for l in range(depth):
        a = layer_args(l)
        sfx = str(l)
        first, more = l == 0, l + 1 < depth
        (qkv_raw, z, rx, rgate, mq, mkv, misc), bufs = staged(
            "mixer_in" + sfx, fn_mixer_in, [] if first else [gather_over_d2d(mo_buf[l])], tok=[h], **a['mixer_in'])
        if not first:
            w_mo[l] = bufs[0].reshape(N_CHIP * d, d)
        (qkv_act,) = run_stage("gdn_conv" + sfx, fn_gdn_conv, tok=[qkv_raw], **a['gdn_conv'])
        (*xs, inverses), bufs = staged("gdn_local" + sfx, fn_gdn_local, [gather_over_ici(mi_buf[l])] if first else [],
                                       tok=[qkv_act, misc], **a['gdn_local'])
        if first:
            mi_buf[l] = bufs[0]
        o_a, st_in = gdn_scan(xs, z, row(gdn_norm_g, l))
        (o_b,), bufs = staged("rglru" + sfx, fn_rglru, [gather_over_d2d(mi_buf[l]), gather_over_ici(mo_buf[l])] if first else [],
                              tok=[rx, rgate], **a['rglru'])
        if first:
            w_mi[l], mo_buf[l] = bufs
        q_at, k_at, v_at = run_stage("mla_pre" + sfx, fn_mla_pre, tok=[mq, mkv, misc], **a['mla_pre'])
        o_c = mla_attention(q_at, k_at, v_at)
        (h_mid,), bufs = staged("out_proj" + sfx, fn_out_proj, [gather_over_d2d(mo_buf[l])] if first else [],
                                tok=[h, o_a, o_b, o_c], **a['out_proj'])
        if first:
            w_mo[l] = bufs[0].reshape(N_CHIP * d, d)
        a = layer_args(l)
        (a_mlp,), bufs = staged("mlp_in" + sfx, fn_mlp_in, [gather_over_ici(mi_buf[l + 1])] if more else [],
                                tok=[h_mid], **a['mlp_in'])
        if more:
            mi_buf[l + 1] = bufs[0]
        (h_out,), bufs = staged("mlp_out" + sfx, fn_mlp_out,
                                [gather_over_d2d(mi_buf[l + 1]), gather_over_ici(mo_buf[l + 1])] if more else [],
                                tok=[h_mid, a_mlp], **a['mlp_out'])
        if more:
            w_mi[l + 1], mo_buf[l + 1] = bufs
        saved.append(dict(h=h, qkv_raw=qkv_raw, z=z, rx=rx, rgate=rgate, mq=mq, mkv=mkv, misc=misc, qkv_act=qkv_act, xs=xs,
                          inverses=inverses,
                          st_in=st_in, o_a=o_a, o_b=o_b, o_c=o_c, q_at=q_at, k_at=k_at, v_at=v_at, h_mid=h_mid, a_mlp=a_mlp))
        h = h_out

    loss_part, dh, d_final_g = loss_head(h, final_norm_g.reshape(1, d), loss_target)
    loss = lax.psum(loss_part[0, 0], ("x", "y", "c"))

    g_full = {n: [None] * depth for n in SHARDED}
    g_rep = {n: [None] * depth for n in REPLICATED if n not in ('final_norm_g', 'b_mod')}

    def column_slabs(g):
        return g.reshape(g.shape[0], N_CHIP, g.shape[1] // N_CHIP).transpose(1, 0, 2)

    def row_slabs(g):
        return g.reshape(N_CHIP, g.shape[0] // N_CHIP, g.shape[1])
    core_id = mc.reshape(1).astype(jnp.int32)
    shards = [None] * len(GATHER_BF16)
    mixer_units, mlp_in_unit, mlp_out_unit = [0, 1, 2, 3], [4], [5]

    def reduce_begin(tag, idxs, l):
        gs = [g_full[GATHER_BF16[i]][l] for i in idxs]
        from_sibling = grad_sibling_exchange("grad_sibling_exchange_" + tag, gs)
        pairs = [add_half("grad_add_%s%d" % (GATHER_BF16[i], l), g, s, core_id) for i, g, s in zip(idxs, gs, from_sibling)]
        return [p[0] for p in pairs], [p[1] for p in pairs]

    def reduce_end(idxs, l, sums32, landed):
        for i, p, r in zip(idxs, sums32, landed):
            n = GATHER_BF16[i]
            shards[i] = sum_peers("grad_sum_%s%d" % (n, l), p, r, ids, wts[n].shape, l, acc=shards[i])

    def staged_bwd(name, fn, idxs, l_units, pair, **kw):
        if pair is None:
            return run_stage(name, fn, **kw)
        groups, bufs = run_stage(name, fn, side=[chip_exchange_job(pair[1])], **kw)
        reduce_end(idxs, l_units, pair[0], bufs[len(idxs):])
        return groups

    dmod = [None] * depth
    carried = None
    for l in reversed(range(depth)):
        a, sv = layer_args(l), saved[l]
        sfx = str(l)
        mlp_out_tok = dict(tok=[sv['h_mid'], sv['a_mlp']], cot=[dh])
        (dh_mid, da_mlp), (dgt_f,), _, _, _ = staged_bwd(
            "mlp_out" + sfx, fn_mlp_out, mixer_units, l + 1, carried, which="small", dtok_dtype={1: BF16},
            big_t=[jnp.swapaxes(w_mo[l], 0, 1)], mm_value=True, **mlp_out_tok, **{**a['mlp_out'], 'ts': 256})
        _, _, _, _, (dw_mlp_out,) = run_stage(
            "mlp_out" + sfx, fn_mlp_out, which="big", **mlp_out_tok, **{**a['mlp_out'], 'ts': 512})
        g_full['w_mlp_out'][l] = row_slabs(dw_mlp_out)
        _, _, _, _, (g_full['w_mlp_in'][l],) = run_stage(
            "mlp_in" + sfx, fn_mlp_in, tok=[sv['h_mid']], cot=[da_mlp], which="big", **{**a['mlp_in'], 'ts': 512})
        (dh_mid,), (dsc_f, dsh_f), (g_rep['norm_mlp_g'][l],), _, _ = run_stage(
            "mlp_in" + sfx, fn_mlp_in, tok=[sv['h_mid']], cot=[da_mlp], addin=dh_mid, which="small",
            big_t=[jnp.swapaxes(w_mi[l], 1, 2)], **a['mlp_in'])
        mlp_sums32, mlp_sums16 = reduce_begin("mlp" + sfx, mlp_in_unit + mlp_out_unit, l)
        (dh_in, do_a, do_b, do_c), (dgt_m,), _, _, (dw_out,) = run_stage(
            "out_proj" + sfx, fn_out_proj, tok=[sv['h'], sv['o_a'], sv['o_b'], sv['o_c']], cot=[dh_mid],
            big_t=[(w_out_t, l)], mm_value=True, **a['out_proj'])
        g_full['w_out'][l] = row_slabs(dw_out)
        attn_cot = mla_attention_bwd(sv['q_at'], sv['k_at'], sv['v_at'], do_c)
        (dmq, dmkv, dmisc_c), _, (g_rep['mla_q_norm_g'][l], g_rep['mla_kv_norm_g'][l], dw_q, dw_kv), _, _ = run_stage(
            "mla_pre" + sfx, fn_mla_pre, tok=[sv['mq'], sv['mkv'], sv['misc']], cot=attn_cot, **a['mla_pre'])
        g_full['mla_w_qb'][l] = column_slabs(dw_q)
        g_full['mla_w_kvb'][l] = column_slabs(dw_kv)
        (drx, drgate), _, _, rg_g, _ = staged_bwd(
            "rglru" + sfx, fn_rglru, mlp_in_unit, l, (mlp_sums32[:1], mlp_sums16[:1]), tok=[sv['rx'], sv['rgate']], cot=[do_b],
            **a['rglru'])
        (g_full['rg_conv_w'][l], g_rep['rg_conv_b'][l], g_rep['rg_b_a'][l], g_rep['rg_b_x'][l], g_rep['rg_lambda'][l],
         g_rep['rg_w_a'][l], g_rep['rg_w_x'][l]) = rg_g
        dxs, dz, g_rep['gdn_norm_g'][l] = gdn_scan_bwd(sv['xs'], sv['z'], row(gdn_norm_g, l), sv['st_in'], do_a)
        (dqkv_act, dmisc_a), _, (g_rep['gdn_a_log'][l], g_rep['gdn_dt_bias'][l]), _, _ = staged_bwd(
            "gdn_local" + sfx, fn_gdn_local, mlp_out_unit, l, (mlp_sums32[1:], mlp_sums16[1:]),
            tok=[sv['qkv_act'], sv['misc']], tok_nd=[sv['inverses']], cot=dxs, **a['gdn_local'])
        (dqkv_raw,), _, _, (g_full['gdn_conv_w'][l],), _ = run_stage(
            "gdn_conv" + sfx, fn_gdn_conv, tok=[sv['qkv_raw']], cot=[dqkv_act], **a['gdn_conv'])
        (dh,), (dsc_m, dsh_m), (g_rep['norm_mix_g'][l],), _, (dw_cat,) = run_stage(
            "mixer_in" + sfx, fn_mixer_in, tok=[sv['h']], cot=[dqkv_raw, dz, drx, drgate, dmq, dmkv, dmisc_a + dmisc_c],
            addin=dh_in, big_t=[(w_cat_t, l)], **a['mixer_in'])
        g_full['w_in'][l] = column_slabs(_proj_cols_back(dw_cat))
        dmod[l] = jnp.concatenate([dsh_m, dsc_m, dgt_m, dsh_f, dsc_f, dgt_f], axis=-1).reshape(bsz, 6 * d)
        carried = reduce_begin("mixer" + sfx, mixer_units, l)
    grad_x = dh
    reduce_end(mixer_units, 0, carried[0], grad_chip_exchange("grad_chip_exchange", carried[1]))

    dmod = jnp.stack(dmod)
    dmod_pad = jnp.concatenate([dmod.reshape(depth * bsz, 6 * d), jnp.zeros((8 - depth * bsz, 6 * d), F32)], axis=0)
    dmod_all = all_gather8("gather_dmod", dmod_pad, True).reshape(N_DEV, 8, 6 * d)[:, :depth * bsz]
    dmod_all = dmod_all.reshape(N_DEV, depth, bsz, 6 * d).transpose(1, 0, 2, 3).reshape(depth, N_DEV * bsz, 6 * d)
    g_w_mod = mod_weight_grad(c_all, lax.dynamic_slice(dmod_all, (0, 0, chip * mod_cols), (depth, N_DEV * bsz, mod_cols)))

    g_rep = {n: jnp.stack(v) for n, v in g_rep.items()}
    g_rep['rg_w_a'] = _block_diag_back(g_rep['rg_w_a'])
    g_rep['rg_w_x'] = _block_diag_back(g_rep['rg_w_x'])
    g_rep['final_norm_g'] = d_final_g
    g_rep['b_mod'] = jnp.sum(dmod, axis=1)
    conv_names = ['gdn_conv_w', 'rg_conv_w']
    conv_full_shapes = [(depth,) + g_full[n][0].shape for n in conv_names]
    small_shapes = [wts[n].shape for n in REPLICATED] + conv_full_shapes
    rep_part = _pack_rows([g_rep[n].reshape(wts[n].shape) for n in REPLICATED] + [jnp.stack(g_full[n]) for n in conv_names])
    rep_rows = rep_part.shape[0]
    rep_all = all_gather8("gather_small_grads", rep_part, True).reshape(N_DEV, rep_rows, PACK_COLS)
    conv_zeros = [jnp.zeros(s, F32) for s in conv_full_shapes]
    rep_out = adamw_reduce("adamw_small", rep_all, *[
        _pack_rows([src[n] for n in REPLICATED] + conv_zeros) for src in (wts, mom_m, mom_v)])
    small_names = REPLICATED + conv_names
    rep_g, rep_d, rep_m, rep_v = [dict(zip(small_names, _unpack_rows(o, small_shapes))) for o in rep_out]
    sh_g = {}
    for n in conv_names:
        cols = wts[n].shape[2]
        sh_g[n] = lax.dynamic_slice(rep_g.pop(n), (0, 0, chip * cols), wts[n].shape)
        for dct in (rep_d, rep_m, rep_v):
            dct.pop(n)

    sh_g.update(zip(GATHER_BF16, grad_half_exchange("grad_half_exchange", shards)))
    sh_names = list(SHARDED)

    def as2d(t):
        return t.reshape(-1, t.shape[-1])

    sh_d, sh_m, sh_v = {}, {}, {}
    for n in sh_names + ['w_mod']:
        g = g_w_mod if n == 'w_mod' else sh_g[n]
        res = adamw("adamw_" + n, as2d(wts[n]), as2d(g), as2d(mom_m[n]), as2d(mom_v[n]))
        sh_d[n], sh_m[n], sh_v[n] = (r.reshape(wts[n].shape) for r in res)
    sh_g['w_mod'] = g_w_mod

    def pick(shd, rep):
        return [shd[n] if n in shd else rep[n] for n in WEIGHTS]

    return (loss, grad_x, *pick(sh_g, rep_g), *pick(sh_d, rep_d), *pick(sh_m, rep_m), *pick(sh_v, rep_v))
```

```python
import functools

import jax
import jax.numpy as jnp
import numpy as np
from jax import lax
from jax.experimental import pallas as pl
from jax.experimental.pallas import tpu as pltpu

F32, BF16 = jnp.float32, jnp.bfloat16
HI = lax.Precision.HIGH
MESH = pl.DeviceIdType.MESH

EPS = 1e-6
CHUNK = 64
GDN_HEADS = 4
MLA_HEADS = 4
RG_C = 8.0
ROPE_THETA = 10000.0
N_DEV = 8
N_CHIP = 4
V7X_VMEM_LIMIT = 60 * 1024 * 1024
ADAM_LR, ADAM_B1, ADAM_B2, ADAM_EPS, ADAM_WD, ADAM_STEP = 0.001, 0.9, 0.999, 1e-08, 0.01, 10


def _params(n_grid):
    return pltpu.CompilerParams(dimension_semantics=("arbitrary",) * n_grid, vmem_limit_bytes=V7X_VMEM_LIMIT)


def _dot(a, b, dims=(((1,), (0,)), ((), ()))):
    return lax.dot_general(a.astype(BF16), b.astype(BF16), dims, preferred_element_type=F32)


@jax.custom_vjp
def _mm_probe(x, w, probe):
    return _dot(x, w)


def _mm_probe_fwd(x, w, probe):
    return _dot(x, w), (x, w)


def _mm_probe_bwd(res, dy):
    x, w = res
    dx = _dot(dy, w, (((1,), (1,)), ((), ())))
    dw = _dot(x, dy, (((0,), (0,)), ((), ())))
    return dx, jnp.zeros_like(w), dw


_mm_probe.defvjp(_mm_probe_fwd, _mm_probe_bwd)


@jax.custom_vjp
def _probe_only(x, probe):
    return jnp.zeros((x.shape[0], probe.shape[1]), F32)


def _probe_only_fwd(x, probe):
    return jnp.zeros((x.shape[0], probe.shape[1]), F32), x


def _probe_only_bwd(x, dy):
    return jnp.zeros_like(x), _dot(x, dy, (((0,), (0,)), ((), ())))


_probe_only.defvjp(_probe_only_fwd, _probe_only_bwd)


@jax.custom_vjp
def _mm_known(x, w, y):
    return y


_mm_known.defvjp(lambda x, w, y: (y, w),
                 lambda w, dy: (_dot(dy, w, (((1,), (1,)), ((), ()))), jnp.zeros_like(w), jnp.zeros_like(dy)))


@jax.custom_vjp
def _mm_known_probe(x, w, y, probe):
    return y


_mm_known_probe.defvjp(
    lambda x, w, y, probe: (y, (x, w)),
    lambda res, dy: (_dot(dy, res[1], (((1,), (1,)), ((), ()))), jnp.zeros_like(res[1]), jnp.zeros_like(dy),
                     _dot(res[0], dy, (((0,), (0,)), ((), ())))))


@jax.custom_vjp
def mmw(x, w):
    return _dot(x, w)


def _mmw_fwd(x, w):
    return _dot(x, w), (x, w)


def _mmw_bwd(res, dy):
    x, w = res
    return _dot(dy, w, (((1,), (1,)), ((), ()))), _dot(x, dy, (((0,), (0,)), ((), ())))


mmw.defvjp(_mmw_fwd, _mmw_bwd)


def rms(x, g):
    return x * lax.rsqrt(jnp.mean(x * x, axis=-1, keepdims=True) + EPS) * g


def _rows(shape):
    return lax.broadcasted_iota(jnp.int32, shape, 0)


def _shift_down(x, s, fill):
    return jnp.where(_rows(x.shape) < s, fill, pltpu.roll(x, s, 0))


def _shift_up(x, s, fill):
    n = x.shape[0]
    return jnp.where(_rows(x.shape) >= n - s, fill, pltpu.roll(x, n - s, 0))


def _make_tshift(s):
    @jax.custom_vjp
    def tshift(x):
        return _shift_down(x, s, 0.0)

    tshift.defvjp(lambda x: (_shift_down(x, s, 0.0), None), lambda _, dy: (_shift_up(dy, s, 0.0),))
    return tshift


_TSHIFT = {s: _make_tshift(s) for s in (1, 2, 3)}


def causal_conv4(x, w):
    y = x * w[3:4, :]
    for j in range(3):
        y = y + _TSHIFT[3 - j](x) * w[j:j + 1, :]
    return y


def _scan_steps(n):
    d = 1
    while d < n:
        yield d
        d *= 2


@jax.custom_vjp
def linscan(a, b):
    return _linscan_fwd_impl(a, b)


def _linscan_fwd_impl(a, b):
    for d in _scan_steps(a.shape[0]):
        b = a * _shift_down(b, d, 0.0) + b
        a = a * _shift_down(a, d, 1.0)
    return b


def _linscan_fwd(a, b):
    h = _linscan_fwd_impl(a, b)
    return h, (a, h)


def _linscan_bwd(res, dh):
    a, h = res
    an = _shift_up(a, 1, 0.0)
    lam = dh
    for d in _scan_steps(a.shape[0]):
        lam = an * _shift_up(lam, d, 0.0) + lam
        an = an * _shift_up(an, d, 1.0)
    return lam * _shift_down(h, 1, 0.0), lam


linscan.defvjp(_linscan_fwd, _linscan_bwd)


def _chunk_scan(x, reverse):
    pos = _rows(x.shape) % CHUNK
    n = x.shape[0]
    d = 1
    while d < CHUNK:
        if reverse:
            x = x + jnp.where(pos < CHUNK - d, pltpu.roll(x, n - d, 0), 0.0)
        else:
            x = x + jnp.where(pos >= d, pltpu.roll(x, d, 0), 0.0)
        d *= 2
    return x


@jax.custom_vjp
def chunk_cumsum(x):
    return _chunk_scan(x, False)


@jax.custom_vjp
def chunk_revcumsum(x):
    return _chunk_scan(x, True)


chunk_cumsum.defvjp(lambda x: (_chunk_scan(x, False), None), lambda _, g: (_chunk_scan(g, True),))
chunk_revcumsum.defvjp(lambda x: (_chunk_scan(x, True), None), lambda _, g: (_chunk_scan(g, False),))


def _bmm(a, b, precision=None):
    return jnp.einsum('nij,njk->nik', a, b, precision=precision, preferred_element_type=F32)


@jax.custom_vjp
def inv_unit_lower(l):
    return _inv_impl(l)


def _inv_impl(l):
    n = l.shape[-1]
    eye = (_rows((n, n)) == lax.broadcasted_iota(jnp.int32, (n, n), 1)).astype(F32)
    p = -l
    a = eye + p
    k = 1
    while 2 * k < n:
        p = _bmm(p, p, HI)
        a = a + _bmm(a, p, HI)
        k *= 2
    return a


def _inv_fwd(l):
    a = _inv_impl(l)
    return a, a


def _inv_bwd(a, da):
    at = jnp.swapaxes(a, 1, 2)
    return (-_bmm(_bmm(at, da, HI), at, HI),)


inv_unit_lower.defvjp(_inv_fwd, _inv_bwd)


@jax.custom_vjp
def inv_unit_lower_known(l, a):
    return a


inv_unit_lower_known.defvjp(lambda l, a: (a, a), lambda a, da: (_inv_bwd(a, da)[0], jnp.zeros_like(a)))


def neg_expm1(y):
    series = -(y * (1.0 + y * (0.5 + y * (1.0 / 6.0 + y * (1.0 / 24.0)))))
    return jnp.where(y > -0.05, series, 1.0 - jnp.exp(y))


def run_stage(name, fn, *, tok, tok_nd=(), ex=(), par=(), par_tiled=(), big=(), out_ch, ts, nc=1, cot=None, addin=None,
              which="all", dtok_dtype=None, side=None):
    tok, tok_nd, ex, par, par_tiled, big = map(list, (tok, tok_nd, ex, par, par_tiled, big))
    big_layer = [b[1] if isinstance(b, tuple) else None for b in big]
    big_arrays = [b[0] if isinstance(b, tuple) else b for b in big]
    big = [jax.ShapeDtypeStruct(a.shape if lyr is None else a.shape[1:], a.dtype) for a, lyr in zip(big_arrays, big_layer)]
    bsz, seq, _ = tok[0].shape
    ts = min(ts, seq)
    ns = seq // ts
    grid = (nc, bsz, ns)

    def tok_spec(a):
        cb = a.shape[-1] // nc
        return pl.BlockSpec((None, ts, cb), lambda c, b, s: (b, s, c))

    def ex_spec(a):
        cb = a.shape[-1] // nc
        return pl.BlockSpec((None, 1, cb), lambda c, b, s: (b, 0, c))

    def full_spec(a, single=False):
        nd = a.ndim
        kw = dict(pipeline_mode=pl.Buffered(1)) if single else {}
        return pl.BlockSpec(a.shape, lambda c, b, s: (0,) * nd, **kw)

    def tiled_spec(a):
        if a.ndim == 2:
            return pl.BlockSpec((a.shape[0], a.shape[1] // nc), lambda c, b, s: (0, c))
        return pl.BlockSpec((None,) + a.shape[1:], lambda c, b, s: (c, 0, 0))

    def big_spec(a, lyr):
        if lyr is None:
            return full_spec(a, True)
        nd = a.ndim
        return pl.BlockSpec((None,) + a.shape[1:], lambda c, b, s: (lyr,) + (0,) * (nd - 1), pipeline_mode=pl.Buffered(1))

    n_tok, n_nd, n_ex, n_par, n_pt, n_big = map(len, (tok, tok_nd, ex, par, par_tiled, big))
    in_arrays = tok + tok_nd + ex + par + par_tiled + big_arrays
    in_specs = ([tok_spec(a) for a in tok + tok_nd] + [ex_spec(a) for a in ex] + [full_spec(a) for a in par]
                + [tiled_spec(a) for a in par_tiled] + [big_spec(a, lyr) for a, lyr in zip(big_arrays, big_layer)])
    out_tok_shapes = [jax.ShapeDtypeStruct((bsz, seq, ch), F32) for ch in out_ch]
    n_in = len(in_arrays)

    def split(vals):
        i = 0
        groups = []
        for n in (n_tok, n_nd, n_ex, n_par, n_pt, n_big):
            groups.append(list(vals[i:i + n]))
            i += n
        return groups

    def split_grads(vals):
        i = 0
        groups = []
        for n in (n_tok, n_ex, n_par, n_pt, n_big):
            groups.append(list(vals[i:i + n]))
            i += n
        return groups

    side = list(side or [])
    side_arrays = [a for job in side for a in job[0]]
    n_side = len(side_arrays)
    side_shapes = [jax.ShapeDtypeStruct(a.shape, a.dtype) for a in side_arrays]
    side_scratch = [pltpu.SemaphoreType.DMA((job[1],)) for job in side for _ in range(2)]

    def side_jobs(side_refs, sems):
        o = 0
        for j, (arrs, _, issue) in enumerate(side):
            yield issue(side_refs[o:o + len(arrs)], sems[2 * j], sems[2 * j + 1])
            o += len(arrs)

    def side_start(side_refs, sems):
        if side:
            c, b, s = pl.program_id(0), pl.program_id(1), pl.program_id(2)

            @pl.when(jnp.logical_and(jnp.logical_and(c == 0, b == 0), s == 0))
            def _():
                for starts, _, _ in side_jobs(side_refs, sems):
                    for cp in starts:
                        cp.start()

    def side_finish(side_refs, sems):
        if side:
            c, b, s = pl.program_id(0), pl.program_id(1), pl.program_id(2)

            @pl.when(jnp.logical_and(jnp.logical_and(c == nc - 1, b == bsz - 1), s == ns - 1))
            def _():
                for _, recv_waits, send_waits in side_jobs(side_refs, sems):
                    for cp in recv_waits:
                        cp.wait_recv()
                    for cp in send_waits:
                        cp.wait_send()

    if cot is None:
        n_out = len(out_tok_shapes)

        def body(*refs):
            tv, ndv, ev, pv, ptv, _ = split([r[...] for r in refs[:n_in - n_big]] + [None] * n_big)
            b_refs = refs[n_in - n_big:n_in]
            side_refs = refs[n_in + n_side + n_out:n_in + 2 * n_side + n_out]
            sems = refs[n_in + 2 * n_side + n_out:]
            side_start(side_refs, sems)
            outs = fn(tv, ndv, ev, pv, ptv,
                      lambda x, i, j=None, known=None: _dot(x, b_refs[i][...] if j is None else b_refs[i][j]))
            for r, o in zip(refs[n_in + n_side:], outs):
                r[...] = o
            side_finish(side_refs, sems)

        res = pl.pallas_call(
            body, name=name, grid=grid, in_specs=in_specs + [_ANY] * n_side,
            out_specs=[tok_spec(a) for a in out_tok_shapes] + [_ANY] * n_side,
            out_shape=out_tok_shapes + side_shapes, input_output_aliases={n_in + j: n_out + j for j in range(n_side)},
            scratch_shapes=side_scratch, compiler_params=_params(3))(*in_arrays, *side_arrays)
        return (res[:n_out], res[n_out:]) if side else res

    cot = list(cot)
    has_addin = addin is not None
    extra = cot + ([addin] if has_addin else [])
    n_cot = len(cot)
    want_small, want_big = which in ("all", "small"), which in ("all", "big")
    if not want_small:
        keep = [i for i in range(n_in - n_big) if not n_tok <= i < n_tok + n_nd]
        in_arrays, in_specs, n_in = [in_arrays[i] for i in keep], [in_specs[i] for i in keep], len(keep)
    small_arrays = tok + ex + par + par_tiled
    g_shapes = [jax.ShapeDtypeStruct(a.shape, F32) for a in (small_arrays if want_small else []) + (big if want_big else [])]
    for i, dt_ in (dtok_dtype or {}).items():
        g_shapes[i] = jax.ShapeDtypeStruct(g_shapes[i].shape, dt_)
    g_specs = (([tok_spec(a) for a in tok] + [ex_spec(a) for a in ex] + [full_spec(a) for a in par]
                + [tiled_spec(a) for a in par_tiled]) if want_small else []) + (
                    [full_spec(a, True) for a in big] if want_big else [])

    def body(*refs):
        c, b, s = pl.program_id(0), pl.program_id(1), pl.program_id(2)
        if want_small:
            n_small_in = n_tok + n_nd + n_ex + n_par + n_pt
            tv, ndv, ev, pv, ptv, _ = split([r[...] for r in refs[:n_small_in]] + [None] * n_big)
            b_refs = refs[n_small_in:n_in]
        else:
            vals = [r[...] for r in refs[:n_in]]
            tv, ndv, ev, pv, ptv, _ = split(vals[:n_tok] + [None] * n_nd + vals[n_tok:] + [None] * n_big)
            b_refs = []
        cots = [r[...].astype(F32) for r in refs[n_in:n_in + n_cot]]
        n_g = len(g_shapes)
        g_refs = list(refs[n_in + len(extra) + n_side:n_in + len(extra) + n_side + n_g])
        side_refs = refs[n_in + len(extra) + n_side + n_g:n_in + len(extra) + 2 * n_side + n_g]
        sems = refs[n_in + len(extra) + 2 * n_side + n_g:]
        side_start(side_refs, sems)
        probes = [jnp.zeros(w.shape, F32) if w.ndim == 2 else [jnp.zeros(w.shape[1:], F32) for _ in range(w.shape[0])]
                  for w in big]

        def f(tv_, ev_, pv_, ptv_, probes_):
            def mm(x, i, j=None, known=None):
                probe = None if probes_ is None else (probes_[i] if j is None else probes_[i][j])
                if not want_small:
                    return _probe_only(x, probe)
                w = b_refs[i][...] if j is None else b_refs[i][j]
                if known is not None:
                    return _mm_known(x, w, known) if probe is None else _mm_known_probe(x, w, known, probe)
                return _dot(x, w) if probe is None else _mm_probe(x, w, probe)

            return fn(tv_, ndv, ev_, pv_, ptv_, mm)

        dt = de = dp = dpt = dbg = ()
        if which == "all":
            dt, de, dp, dpt, dbg = jax.vjp(f, tv, ev, pv, ptv, probes)[1](cots)
        elif which == "small":
            dt, de, dp, dpt = jax.vjp(lambda *a: f(*a, None), tv, ev, pv, ptv)[1](cots)
        else:
            (dbg,) = jax.vjp(lambda p: f(tv, ev, pv, ptv, p), probes)[1](cots)
        if has_addin:
            dt = [dt[0] + refs[n_in + n_cot][...]] + list(dt[1:])
        if want_small:
            gt_r, ge_r, gp_r, gpt_r, gb_r = split_grads(g_refs + ([] if want_big else [None] * n_big))
        else:
            gt_r, ge_r, gp_r, gpt_r, gb_r = [], [], [], [], g_refs
        for r, g in zip(gt_r, dt):
            r[...] = g.astype(r.dtype)

        def accumulate(r, g, first):
            @pl.when(first)
            def _():
                r[...] = g

            @pl.when(jnp.logical_not(first))
            def _():
                r[...] += g

        for r, g in zip(ge_r, de):
            accumulate(r, g, s == 0)
        first_all = jnp.logical_and(jnp.logical_and(c == 0, b == 0), s == 0)
        for r, g in zip(gp_r, dp):
            accumulate(r, g, first_all)
        for r, g in zip(gpt_r, dpt):
            accumulate(r, g, jnp.logical_and(b == 0, s == 0))
        for r, g in zip(gb_r, dbg):
            if isinstance(g, (list, tuple)):
                for j, gj in enumerate(g):
                    accumulate(r.at[j], gj, first_all)
            else:
                accumulate(r, g, first_all)
        side_finish(side_refs, sems)

    n_args = n_in + len(extra)
    res = pl.pallas_call(
        body, name=name + "_bwd" + ("" if which == "all" else "_" + which), grid=grid,
        in_specs=in_specs + [tok_spec(a) for a in extra] + [_ANY] * n_side, out_specs=g_specs + [_ANY] * n_side,
        out_shape=g_shapes + side_shapes, input_output_aliases={n_args + j: len(g_shapes) + j for j in range(n_side)},
        scratch_shapes=side_scratch, compiler_params=_params(3))(*in_arrays, *extra, *side_arrays)
    res, side_out = list(res[:len(g_shapes)]), list(res[len(g_shapes):])
    groups = [[], [], [], [], res] if not want_small else split_grads(res + ([] if want_big else [None] * n_big))
    return (groups, side_out) if side else groups


PROJ_PIECES = (("qkv", 768), ("z", 256), ("rx", 512), ("rgate", 512), ("mq", 256), ("mkv", 128), ("misc", 128))
PROJ_WIDTH = sum(w for _, w in PROJ_PIECES)
MISC_KR, MISC_A, MISC_B = 0, 32, 36


def fn_mixer_in(tok, nd, ex, par, pt, mm):
    (h,), (sc, sh), (g,) = tok, ex, par
    proj = mm(rms(h, g) * (1.0 + sc) + sh, 0)
    outs, o = [], 0
    for _, w in PROJ_PIECES:
        outs.append(proj[:, o:o + w])
        o += w
    return outs


def fn_gdn_conv(tok, nd, ex, par, pt, mm):
    return [jax.nn.silu(causal_conv4(tok[0], pt[0]))]


def _tri_masks():
    r = _rows((CHUNK, CHUNK))
    c = lax.broadcasted_iota(jnp.int32, (CHUNK, CHUNK), 1)
    return (c <= r), (c < r)


def fn_gdn_local(tok, nd, ex, par, pt, mm):
    (qkv, misc), (a_log, dt_bias) = tok, par
    known = nd[0] if nd else None
    ts = qkv.shape[0]
    nb = ts // CHUNK
    lower, strict = _tri_masks()
    g_all = -jnp.exp(a_log) * jax.nn.softplus(misc[:, MISC_A:MISC_A + GDN_HEADS] + dt_bias)
    g_cum = chunk_cumsum(g_all)
    g_tot = g_cum + chunk_revcumsum(g_all) - g_all
    outs = [[] for _ in range(7)]
    for hd in range(GDN_HEADS):
        def head(x, base):
            return x[:, base + 64 * hd: base + 64 * hd + 64]

        def l2n(x):
            return x * lax.rsqrt(jnp.sum(x * x, axis=-1, keepdims=True) + EPS)

        q = (l2n(head(qkv, 0)) * (64.0 ** -0.5)).reshape(nb, CHUNK, 64)
        k = l2n(head(qkv, 256)).reshape(nb, CHUNK, 64)
        v = head(qkv, 512).reshape(nb, CHUNK, 64)
        b = misc[:, MISC_B + hd: MISC_B + hd + 1]
        beta = jax.nn.sigmoid(b).reshape(nb, CHUNK, 1)
        gi = jnp.broadcast_to(g_cum[:, hd:hd + 1].reshape(nb, CHUNK, 1), (nb, CHUNK, CHUNK))
        gl = jnp.broadcast_to(g_tot[:, hd:hd + 1].reshape(nb, CHUNK, 1), (nb, CHUNK, CHUNK))
        diff = gi - jnp.swapaxes(gi, 1, 2)
        decay = jnp.where(lower, jnp.exp(jnp.where(lower, diff, 0.0)), 0.0)
        kb = k * beta
        vb = v * beta
        kk = jnp.einsum('ncd,nsd->ncs', kb.astype(BF16), k.astype(BF16), preferred_element_type=F32)
        lmat = jnp.where(strict, kk * decay, 0.0)
        if known is None:
            amat = inv_unit_lower(lmat)
        else:
            amat = inv_unit_lower_known(lmat, known[:, 64 * hd: 64 * hd + 64].reshape(nb, CHUNK, 64))
        eg = jnp.exp(gi)
        u = _bmm(amat, vb, HI)
        w = _bmm(amat, kb * eg, HI)
        qk = jnp.einsum('ncd,nsd->ncs', q.astype(BF16), k.astype(BF16), preferred_element_type=F32) * decay
        qd = q * eg
        kt = k * jnp.exp(gl - gi)
        cd = jnp.exp(gl)
        for lst, val in zip(outs, (qk, qd, u, w, kt, cd) + (() if known is not None else (amat,))):
            lst.append(val.reshape(ts, 64))
    return [jnp.concatenate(lst, axis=-1) for lst in outs if lst]


def fn_rglru(tok, nd, ex, par, pt, mm):
    (rx, rgate), (conv_w, conv_b, b_a, b_x, lam, bd_a, bd_x) = tok, pt
    xc = causal_conv4(rx, conv_w) + conv_b
    r = jax.nn.sigmoid(mmw(xc, bd_a) + b_a)
    i = jax.nn.sigmoid(mmw(xc, bd_x) + b_x)
    log_a = -RG_C * r * jax.nn.softplus(-lam)
    a = jnp.exp(log_a)
    bterm = jnp.sqrt(neg_expm1(2.0 * log_a)) * (i * xc)
    return [linscan(a, bterm) * jax.nn.gelu(rgate)]


def _rope32(x, cos, sin):
    x1, x2 = x[:, :16], x[:, 16:32]
    return jnp.concatenate([x1 * cos - x2 * sin, x2 * cos + x1 * sin], axis=-1)


MLA_QK = 96


def fn_mla_pre(tok, nd, ex, par, pt, mm):
    (mq, mkv, misc), (cs,), (g_q, g_kv, w_q, w_kv) = tok, nd, par
    q = mmw(rms(mq, g_q), w_q)
    kv = mmw(rms(mkv, g_kv), w_kv)
    cos, sin = cs[:, 0:16], cs[:, 16:32]
    kp = _rope32(misc[:, MISC_KR:MISC_KR + 32], cos, sin)
    qs, ks, vs = [], [], []
    for h in range(MLA_HEADS):
        qs += [q[:, MLA_QK * h: MLA_QK * h + 64], _rope32(q[:, MLA_QK * h + 64: MLA_QK * h + 96], cos, sin)]
        ks += [kv[:, 128 * h: 128 * h + 64], kp]
        vs.append(kv[:, 128 * h + 64: 128 * h + 128])
    return [jnp.concatenate(qs, axis=-1), jnp.concatenate(ks, axis=-1), jnp.concatenate(vs, axis=-1)]


def fn_out_proj(tok, nd, ex, par, pt, mm):
    (h, o_a, o_b, o_c), (gt,) = tok, ex
    mix = mm(jnp.concatenate([o_a, o_b, o_c], axis=-1), 0, known=nd[0] if nd else None)
    return [h + gt * mix] + ([] if nd else [mix])


def fn_mlp_in(tok, nd, ex, par, pt, mm):
    (h,), (sc, sh), (g,) = tok, ex, par
    u = rms(h, g) * (1.0 + sc) + sh
    return [jnp.concatenate([mm(u, 0, j) for j in range(N_CHIP)], axis=-1)]


def fn_mlp_out(tok, nd, ex, par, pt, mm):
    (h, a), (gt,) = tok, ex
    f = mm(jnp.square(jax.nn.relu(a)), 0, known=nd[0] if nd else None)
    return [h + gt * f] + ([] if nd else [f])


GDN_W = GDN_HEADS * 64


def _head_mask():
    r = _rows((GDN_W, GDN_W)) // 64
    c = lax.broadcasted_iota(jnp.int32, (GDN_W, GDN_W), 1) // 64
    return r == c


def _heads_diag(x):
    return jnp.where(_head_mask(), jnp.concatenate([x] * GDN_HEADS, axis=0), 0.0)


def _heads_compact(s):
    return s[0:64] + s[64:128] + s[128:192] + s[192:256]


def _gdn_step(state, qk, qd, u, w, kt, cd, z, norm_g):
    v_new = u - _dot(w, state)
    o = _dot(qd, state) + _dot(qk, _heads_diag(v_new))
    update = _dot(kt, v_new, (((0,), (0,)), ((), ())))
    new_state = state * jnp.broadcast_to(cd[0:1, :], (GDN_W, GDN_W)) + jnp.where(_head_mask(), update, 0.0)
    outs = [rms(o[:, 64 * hd: 64 * hd + 64], norm_g) * jax.nn.silu(z[:, 64 * hd: 64 * hd + 64]) for hd in range(GDN_HEADS)]
    return new_state, jnp.concatenate(outs, axis=-1)


def gdn_scan(xs, z, norm_g):
    bsz, seq, _ = z.shape
    n = seq // CHUNK
    blk = pl.BlockSpec((bsz, CHUNK, 256), lambda i: (0, i, 0))

    def body(qk, qd, u, w, kt, cd, z_ref, g_ref, o_ref, st_out, st):
        @pl.when(pl.program_id(0) == 0)
        def _():
            st[...] = jnp.zeros_like(st)

        for b in range(bsz):
            state = st[b]
            st_out[b] = _heads_compact(state)
            st[b], o_ref[b] = _gdn_step(state, qk[b], qd[b], u[b], w[b], kt[b], cd[b], z_ref[b], g_ref[...])

    return pl.pallas_call(
        body, name="gdn_scan", grid=(n,), in_specs=[blk] * 7 + [pl.BlockSpec((1, 64), lambda i: (0, 0))],
        out_specs=[blk, blk], out_shape=[jax.ShapeDtypeStruct((bsz, seq, 256), F32)] * 2,
        scratch_shapes=[pltpu.VMEM((bsz, GDN_W, GDN_W), F32)], compiler_params=_params(1))(*xs, z, norm_g)


def gdn_scan_bwd(xs, z, norm_g, st_in, do):
    bsz, seq, _ = z.shape
    n = seq // CHUNK
    blk = pl.BlockSpec((bsz, CHUNK, 256), lambda i: (0, n - 1 - i, 0))
    gspec = pl.BlockSpec((1, 64), lambda i: (0, 0))

    def body(qk, qd, u, w, kt, cd, z_ref, g_ref, st_ref, do_ref, dqk, dqd, du, dw, dkt, dcd, dz, dg, dst):
        first = pl.program_id(0) == 0

        @pl.when(first)
        def _():
            dst[...] = jnp.zeros_like(dst)

        dg_sum = None
        for b in range(bsz):
            _, vjp = jax.vjp(_gdn_step, _heads_diag(st_ref[b]), qk[b], qd[b], u[b], w[b], kt[b], cd[b], z_ref[b], g_ref[...])
            grads = vjp((dst[b], do_ref[b]))
            dst[b] = jnp.where(_head_mask(), grads[0], 0.0)
            for r, g in zip((dqk, dqd, du, dw, dkt, dcd, dz), grads[1:8]):
                r[b] = g
            dg_sum = grads[8] if dg_sum is None else dg_sum + grads[8]

        @pl.when(first)
        def _():
            dg[...] = dg_sum

        @pl.when(jnp.logical_not(first))
        def _():
            dg[...] += dg_sum

    res = pl.pallas_call(
        body, name="gdn_scan_bwd", grid=(n,), in_specs=[blk] * 7 + [gspec, blk, blk],
        out_specs=[blk] * 7 + [gspec], out_shape=[jax.ShapeDtypeStruct((bsz, seq, 256), F32)] * 7
        + [jax.ShapeDtypeStruct((1, 64), F32)],
        scratch_shapes=[pltpu.VMEM((bsz, GDN_W, GDN_W), F32)], compiler_params=_params(1))(*xs, z, norm_g, st_in, do)
    return list(res[:6]), res[6], res[7]


ATTN_TQ = 256
ATTN_SCALE = 96.0 ** -0.5
ATTN_KEY_FRACTIONS = (4, 2, 1)


def _attn_head(q, k, v, q0):
    s = _dot(q, k, (((1,), (1,)), ((), ()))) * ATTN_SCALE
    qc = (q0 + _rows(s.shape)) // CHUNK
    kc = lax.broadcasted_iota(jnp.int32, s.shape, 1) // CHUNK
    s = jnp.where(kc <= qc, s, -1e30)
    p = jnp.exp(s - jnp.max(s, axis=-1, keepdims=True))
    p = p / jnp.sum(p, axis=-1, keepdims=True)
    return _dot(p, v)


def _key_lengths(seq):
    return sorted({max(ATTN_TQ, seq // f) for f in ATTN_KEY_FRACTIONS})


def _key_variant(i, seq):
    need = (i + 1) * ATTN_TQ
    return sum(((need > klen).astype(jnp.int32) for klen in _key_lengths(seq)[:-1]), jnp.int32(0))


ATTN_QW, ATTN_VW = MLA_HEADS * MLA_QK, MLA_HEADS * 64


def _attn_specs(seq):
    def qspec(ch):
        return pl.BlockSpec((None, ATTN_TQ, ch), lambda b, i: (b, i, 0))

    def kspec(ch):
        return pl.BlockSpec((None, seq, ch), lambda b, i: (b, 0, 0))

    return qspec, kspec


def mla_attention(q, k, v):
    bsz, seq, _ = q.shape
    qspec, kspec = _attn_specs(seq)

    def body(q_r, k_r, v_r, o_r):
        i = pl.program_id(1)
        q0 = i * ATTN_TQ

        def with_keys(klen):
            outs = [_attn_head(q_r[:, MLA_QK * h: MLA_QK * h + MLA_QK], k_r[0:klen, MLA_QK * h: MLA_QK * h + MLA_QK],
                               v_r[0:klen, 64 * h: 64 * h + 64], q0) for h in range(MLA_HEADS)]
            o_r[...] = jnp.concatenate(outs, axis=-1)

        for j, klen in enumerate(_key_lengths(seq)):
            pl.when(_key_variant(i, seq) == j)(functools.partial(with_keys, klen))

    return pl.pallas_call(
        body, name="mla_attention", grid=(bsz, seq // ATTN_TQ), in_specs=[qspec(ATTN_QW), kspec(ATTN_QW), kspec(ATTN_VW)],
        out_specs=qspec(ATTN_VW), out_shape=jax.ShapeDtypeStruct((bsz, seq, ATTN_VW), F32), compiler_params=_params(2))(
            q, k, v)


def mla_attention_bwd(q, k, v, do):
    bsz, seq, _ = q.shape
    qspec, kspec = _attn_specs(seq)

    def body(q_r, k_r, v_r, do_r, dq_r, dk_r, dv_r):
        i = pl.program_id(1)
        q0 = i * ATTN_TQ

        @pl.when(i == 0)
        def _():
            dk_r[...] = jnp.zeros_like(dk_r)
            dv_r[...] = jnp.zeros_like(dv_r)

        def with_keys(klen):
            dq, dk, dv = [], [], []
            for h in range(MLA_HEADS):
                qk = slice(MLA_QK * h, MLA_QK * h + MLA_QK)
                sl = slice(64 * h, 64 * h + 64)
                _, vjp = jax.vjp(functools.partial(_attn_head, q0=q0), q_r[:, qk], k_r[0:klen, qk], v_r[0:klen, sl])
                a, b, c = vjp(do_r[:, sl])
                dq.append(a)
                dk.append(b)
                dv.append(c)
            dq_r[...] = jnp.concatenate(dq, axis=-1)
            dk_r[0:klen, :] += jnp.concatenate(dk, axis=-1)
            dv_r[0:klen, :] += jnp.concatenate(dv, axis=-1)

        for j, klen in enumerate(_key_lengths(seq)):
            pl.when(_key_variant(i, seq) == j)(functools.partial(with_keys, klen))

    shp = lambda ch: jax.ShapeDtypeStruct((bsz, seq, ch), F32)
    return pl.pallas_call(
        body, name="mla_attention_bwd", grid=(bsz, seq // ATTN_TQ),
        in_specs=[qspec(ATTN_QW), kspec(ATTN_QW), kspec(ATTN_VW), qspec(ATTN_VW)],
        out_specs=[qspec(ATTN_QW), kspec(ATTN_QW), kspec(ATTN_VW)],
        out_shape=[shp(ATTN_QW), shp(ATTN_QW), shp(ATTN_VW)], compiler_params=_params(2))(q, k, v, do)


LOSS_TS = 512


def loss_head(h, g, target):
    bsz, seq, d = h.shape
    ts = min(LOSS_TS, seq)
    tok = pl.BlockSpec((None, ts, d), lambda b, s: (b, s, 0))
    gspec = pl.BlockSpec((1, d), lambda b, s: (0, 0))
    lspec = pl.BlockSpec((1, 128), lambda b, s: (0, 0))

    def body(h_r, g_r, t_r, loss_r, dh_r, dg_r):
        first = jnp.logical_and(pl.program_id(0) == 0, pl.program_id(1) == 0)
        tv = t_r[...]

        def f(hv, gv):
            return 0.5 * jnp.sum(jnp.mean(jnp.square(rms(hv, gv) - tv), axis=-1, keepdims=True), axis=0, keepdims=True)

        val, vjp = jax.vjp(f, h_r[...], g_r[...])
        dh, dg = vjp(jnp.ones((1, 1), F32))
        dh_r[...] = dh
        lv = jnp.broadcast_to(val, (1, 128))

        @pl.when(first)
        def _():
            loss_r[...] = lv
            dg_r[...] = dg

        @pl.when(jnp.logical_not(first))
        def _():
            loss_r[...] += lv
            dg_r[...] += dg

    return pl.pallas_call(
        body, name="loss_head", grid=(bsz, seq // ts), in_specs=[tok, gspec, tok], out_specs=[lspec, tok, gspec],
        out_shape=[jax.ShapeDtypeStruct((1, 128), F32), jax.ShapeDtypeStruct(h.shape, F32), jax.ShapeDtypeStruct((1, d), F32)],
        compiler_params=_params(2))(h, g, target)


def _adamw_math(w, g, m, v):
    m = ADAM_B1 * m + (1.0 - ADAM_B1) * g
    v = ADAM_B2 * v + (1.0 - ADAM_B2) * jnp.square(g)
    m_hat = m / (1.0 - ADAM_B1 ** ADAM_STEP)
    v_hat = v / (1.0 - ADAM_B2 ** ADAM_STEP)
    return -ADAM_LR * (m_hat / (jnp.sqrt(v_hat) + ADAM_EPS) + ADAM_WD * w), m, v


def _row_block(rows, cols):
    want = max(8, (1 << 18) // cols)
    best = rows
    for r in range(8, rows + 1, 8):
        if rows % r == 0 and r <= want:
            best = r
    return best if rows % 8 == 0 else rows


def adamw(name, w, g, m, v):
    rows, cols = w.shape
    rb = _row_block(rows, cols)
    spec = pl.BlockSpec((rb, cols), lambda i: (i, 0))

    def body(w_r, g_r, m_r, v_r, d_o, m_o, v_o):
        d, mn, vn = _adamw_math(w_r[...], g_r[...], m_r[...], v_r[...])
        d_o[...] = d
        m_o[...] = mn
        v_o[...] = vn

    return pl.pallas_call(body, name=name, grid=(rows // rb,), in_specs=[spec] * 4, out_specs=[spec] * 3,
                          out_shape=[jax.ShapeDtypeStruct(w.shape, F32)] * 3, compiler_params=_params(1))(w, g, m, v)


def adamw_reduce(name, parts, w, m, v):
    rows, cols = w.shape
    rb = _row_block(rows, cols)
    spec = pl.BlockSpec((rb, cols), lambda i: (i, 0))
    pspec = pl.BlockSpec((N_DEV, rb, cols), lambda i: (0, i, 0))

    def body(p_r, w_r, m_r, v_r, g_o, d_o, m_o, v_o):
        g = p_r[0]
        for k in range(1, N_DEV):
            g = g + p_r[k]
        d, mn, vn = _adamw_math(w_r[...], g, m_r[...], v_r[...])
        g_o[...] = g
        d_o[...] = d
        m_o[...] = mn
        v_o[...] = vn

    return pl.pallas_call(body, name=name, grid=(rows // rb,), in_specs=[pspec, spec, spec, spec], out_specs=[spec] * 4,
                          out_shape=[jax.ShapeDtypeStruct(w.shape, F32)] * 4, compiler_params=_params(1))(parts, w, m, v)


MOD_CB = 512


def mod_matmul(c_rows, w_mod, b_mod):
    nl, d, cols = w_mod.shape

    def body(c_r, w_r, b_r, o_r):
        o_r[...] = _dot(jax.nn.silu(c_r[...]), w_r[...]) + b_r[...]

    return pl.pallas_call(
        body, name="mod_matmul", grid=(nl, cols // MOD_CB),
        in_specs=[pl.BlockSpec((8, d), lambda l, j: (0, 0)), pl.BlockSpec((None, d, MOD_CB), lambda l, j: (l, 0, j)),
                  pl.BlockSpec((None, 1, MOD_CB), lambda l, j: (l, 0, j))],
        out_specs=pl.BlockSpec((None, 8, MOD_CB), lambda l, j: (l, 0, j)),
        out_shape=jax.ShapeDtypeStruct((nl, 8, cols), F32), compiler_params=_params(2))(c_rows, w_mod, b_mod)


def mod_weight_grad(c_all, dmod):
    nl, nb, cols = dmod.shape
    d = c_all.shape[1]

    def body(c_r, g_r, o_r):
        o_r[...] = _dot(jax.nn.silu(c_r[...]), g_r[...], (((0,), (0,)), ((), ())))

    return pl.pallas_call(
        body, name="mod_weight_grad", grid=(nl, cols // MOD_CB),
        in_specs=[pl.BlockSpec((nb, d), lambda l, j: (0, 0)), pl.BlockSpec((None, nb, MOD_CB), lambda l, j: (l, 0, j))],
        out_specs=pl.BlockSpec((None, d, MOD_CB), lambda l, j: (l, 0, j)),
        out_shape=jax.ShapeDtypeStruct((nl, d, cols), F32), compiler_params=_params(2))(c_all, dmod)


def _half_block(hr, cols):
    rb = _row_block(hr, cols)
    return rb if rb % 16 == 0 else hr


def add_half(name, g, s, core):
    _, r, cols = g.shape
    hr = r // 2
    rb = _half_block(hr, cols)
    nblk = hr // rb
    gspec = pl.BlockSpec((None, rb, cols), lambda k, i, c: (k, c[0] * nblk + i, 0))
    spec = pl.BlockSpec((None, rb, cols), lambda k, i, c: (k, i, 0))

    def body(c_r, g_r, s_r, o_r, ob_r):
        t = g_r[...] + s_r[...]
        o_r[...] = t
        ob_r[...] = t.astype(BF16)

    return pl.pallas_call(
        body, name=name, grid_spec=pltpu.PrefetchScalarGridSpec(num_scalar_prefetch=1, grid=(N_CHIP, nblk),
                                                                in_specs=[gspec, spec], out_specs=[spec, spec]),
        out_shape=[jax.ShapeDtypeStruct((N_CHIP, hr, cols), F32), jax.ShapeDtypeStruct((N_CHIP, hr, cols), BF16)],
        compiler_params=_params(2))(core, g, s)


def sum_peers(name, p32, recv, ids, shard_shape, layer, acc=None):
    _, hr, cols = p32.shape
    rb = _half_block(hr, cols)
    nblk = hr // rb

    def slot(k):
        return pl.BlockSpec((None, rb, cols), lambda i, c: ((c[0] + k) % N_CHIP, i, 0))

    def body(c_r, o_r, r1, r2, r3, *rest):
        rest[-1][...] = ((o_r[...] + r1[...].astype(F32)) + r2[...].astype(F32)) + r3[...].astype(F32)

    args = (ids, p32, recv, recv, recv) + (() if acc is None else (acc,))
    return pl.pallas_call(
        body, name=name, grid_spec=pltpu.PrefetchScalarGridSpec(
            num_scalar_prefetch=1, grid=(nblk,),
            in_specs=[slot(0), slot(1), slot(2), slot(3)] + ([] if acc is None else [_ANY]),
            out_specs=pl.BlockSpec((None, rb, cols), lambda i, c: (layer, c[1] * nblk + i, 0))),
        out_shape=jax.ShapeDtypeStruct(shard_shape, F32), input_output_aliases={} if acc is None else {5: 0},
        compiler_params=_params(1))(*args)


def cast_into_slab(name, w, ids, layer=None):
    nl, r, cols = w.shape
    hr = r // 2
    rb = _half_block(hr, cols)
    nblk = hr // rb

    def body(c_r, w_r, o_r):
        o_r[...] = w_r[...].astype(BF16)

    if layer is not None:
        return pl.pallas_call(
            body, name=name, grid_spec=pltpu.PrefetchScalarGridSpec(
                num_scalar_prefetch=1, grid=(nblk,),
                in_specs=[pl.BlockSpec((None, rb, cols), lambda i, c: (layer, c[1] * nblk + i, 0))],
                out_specs=pl.BlockSpec((None, rb, cols), lambda i, c: (c[0], c[1] * nblk + i, 0))),
            out_shape=jax.ShapeDtypeStruct((N_CHIP, r, cols), BF16), compiler_params=_params(1))(ids, w)
    return pl.pallas_call(
        body, name=name, grid_spec=pltpu.PrefetchScalarGridSpec(
            num_scalar_prefetch=1, grid=(nl, nblk),
            in_specs=[pl.BlockSpec((None, rb, cols), lambda l, i, c: (l, c[1] * nblk + i, 0))],
            out_specs=pl.BlockSpec((None, None, rb, cols), lambda l, i, c: (l, c[0], c[1] * nblk + i, 0))),
        out_shape=jax.ShapeDtypeStruct((nl, N_CHIP, r, cols), BF16), compiler_params=_params(2))(ids, w)


def _me():
    return lax.axis_index("x"), lax.axis_index("y"), lax.axis_index("c")


def all_gather8(name, x_shard, in_vmem):
    m_per, n = x_shard.shape
    space = pltpu.VMEM if in_vmem else pl.ANY

    def body(x_ref, out_ref, send_sems, recv_sems, local_sem):
        x, y, c = _me()
        me, sibling = (x, y, c), (x, y, 1 - c)
        chips = [(1 - x, y), (x, 1 - y), (1 - x, 1 - y)]

        def rows(px, py, pc):
            return out_ref.at[pl.ds((4 * px + 2 * py + pc) * m_per, m_per), :]

        def copy(k, block, to, src=None):
            return pltpu.make_async_remote_copy(
                src_ref=rows(*block) if src is None else src, dst_ref=rows(*block), send_sem=send_sems.at[k],
                recv_sem=recv_sems.at[k], device_id=to, device_id_type=MESH)

        mine = pltpu.make_async_copy(x_ref, rows(*me), local_sem)
        mine.start()
        first = [copy(0, me, sibling, src=x_ref)]
        first += [copy(1 + j, me, (*chip, c), src=x_ref) for j, chip in enumerate(chips)]
        for cp in first:
            cp.start()
        passed = [copy(4 + j, (*chip, c), sibling) for j, chip in enumerate(chips)]
        for j, chip in enumerate(chips):
            copy(1 + j, (*chip, c), me).wait_recv()
            passed[j].start()
        copy(0, sibling, me).wait_recv()
        for j, chip in enumerate(chips):
            copy(4 + j, (*chip, 1 - c), me).wait_recv()
        for cp in first + passed:
            cp.wait_send()
        mine.wait()

    return pl.pallas_call(
        body, name=name, out_shape=jax.ShapeDtypeStruct((N_DEV * m_per, n), x_shard.dtype),
        in_specs=[pl.BlockSpec(memory_space=space)], out_specs=pl.BlockSpec(memory_space=space),
        scratch_shapes=[pltpu.SemaphoreType.DMA((7,)), pltpu.SemaphoreType.DMA((7,)), pltpu.SemaphoreType.DMA],
    )(x_shard)


_ANY = pl.BlockSpec(memory_space=pl.ANY)


def all_gather_weights(name, slabs):
    n = len(slabs)

    def body(*refs):
        outs = refs[n:2 * n]
        send_sems, recv_sems = refs[2 * n:]
        x, y, c = _me()
        me, sibling = (x, y, c), (x, y, 1 - c)
        chips = [(1 - x, y), (x, 1 - y), (1 - x, 1 - y)]

        def view(i, px, py, pc):
            hr = slabs[i].shape[2] // 2
            return outs[i].at[:, 2 * px + py, pl.ds(pc * hr, hr), :]

        def copy(i, k, block, to):
            return pltpu.make_async_remote_copy(
                src_ref=view(i, *block), dst_ref=view(i, *block), send_sem=send_sems.at[i, k],
                recv_sem=recv_sems.at[i, k], device_id=to, device_id_type=MESH)

        first = []
        for i in range(n):
            first.append(copy(i, 0, me, sibling))
            first += [copy(i, 1 + j, me, (*chip, c)) for j, chip in enumerate(chips)]
        for cp in first:
            cp.start()
        passed = []
        for j, chip in enumerate(chips):
            for i in range(n):
                copy(i, 1 + j, (*chip, c), me).wait_recv()
                passed.append(copy(i, 4 + j, (*chip, c), sibling))
                passed[-1].start()
        for i in range(n):
            copy(i, 0, sibling, me).wait_recv()
            for j, chip in enumerate(chips):
                copy(i, 4 + j, (*chip, 1 - c), me).wait_recv()
        for cp in first + passed:
            cp.wait_send()

    return pl.pallas_call(
        body, name=name, out_shape=[jax.ShapeDtypeStruct(s.shape, s.dtype) for s in slabs],
        in_specs=[_ANY] * n, out_specs=[_ANY] * n, input_output_aliases={i: i for i in range(n)},
        scratch_shapes=[pltpu.SemaphoreType.DMA((n, 7)), pltpu.SemaphoreType.DMA((n, 7))],
    )(*slabs)


def _slab_block(slab, px, py, pc):
    hr = slab.shape[1] // 2
    return slab.at[2 * px + py, pl.ds(pc * hr, hr), :]


def gather_over_ici(slab):
    def issue(refs, send_sems, recv_sems):
        (buf,) = refs
        x, y, c = _me()
        peers = [(x, y, 1 - c), (1 - x, y, c), (x, 1 - y, c), (1 - x, 1 - y, c)]

        def copy(k, block, to):
            return pltpu.make_async_remote_copy(
                src_ref=_slab_block(buf, *block), dst_ref=_slab_block(buf, *block), send_sem=send_sems.at[k],
                recv_sem=recv_sems.at[k], device_id=to, device_id_type=MESH)

        sends = [copy(k, (x, y, c), p) for k, p in enumerate(peers)]
        arrivals = [copy(k, p, (x, y, c)) for k, p in enumerate(peers)]
        return sends, arrivals, sends

    return ([slab], 4, issue)


def gather_over_d2d(slab):
    def issue(refs, send_sems, recv_sems):
        (buf,) = refs
        x, y, c = _me()
        chips = [(1 - x, y), (x, 1 - y), (1 - x, 1 - y)]

        def copy(k, block):
            return pltpu.make_async_remote_copy(
                src_ref=_slab_block(buf, *block), dst_ref=_slab_block(buf, *block), send_sem=send_sems.at[k],
                recv_sem=recv_sems.at[k], device_id=(x, y, 1 - c), device_id_type=MESH)

        sends = [copy(k, (*chip, c)) for k, chip in enumerate(chips)]
        arrivals = [copy(k, (*chip, 1 - c)) for k, chip in enumerate(chips)]
        return sends, arrivals, sends

    return ([slab], 3, issue)


def chip_exchange_job(ps):
    n = len(ps)

    def issue(refs, send_sems, recv_sems):
        ins, outs = refs[:n], refs[n:]
        mx, my, mc = _me()
        ci = 2 * mx + my
        chips = [(1 - mx, my), (mx, 1 - my), (1 - mx, 1 - my)]
        sends, arrivals = [], []
        for i in range(n):
            for k, (px, py) in enumerate(chips):
                sem = 3 * i + k
                sends.append(pltpu.make_async_remote_copy(
                    src_ref=ins[i].at[2 * px + py], dst_ref=outs[i].at[ci], send_sem=send_sems.at[sem],
                    recv_sem=recv_sems.at[sem], device_id=(px, py, mc), device_id_type=MESH))
                arrivals.append(pltpu.make_async_remote_copy(
                    src_ref=ins[i].at[ci], dst_ref=outs[i].at[2 * px + py], send_sem=send_sems.at[sem],
                    recv_sem=recv_sems.at[sem], device_id=(px, py, mc), device_id_type=MESH))
        return sends, arrivals, sends

    return (list(ps) + [lax.empty(p.shape, p.dtype) for p in ps], 3 * n, issue)


def grad_sibling_exchange(name, gs):
    n = len(gs)

    def body(*refs):
        ins, outs = refs[:n], refs[n:2 * n]
        send_sems, recv_sems = refs[2 * n:]
        mx, my, mc = _me()
        cps = []
        for i in range(n):
            hr = gs[i].shape[1] // 2
            cps.append(pltpu.make_async_remote_copy(
                src_ref=ins[i].at[:, pl.ds((1 - mc) * hr, hr), :], dst_ref=outs[i], send_sem=send_sems.at[i],
                recv_sem=recv_sems.at[i], device_id=(mx, my, 1 - mc), device_id_type=MESH))
            cps[-1].start()
        for cp in cps:
            cp.wait()

    return pl.pallas_call(
        body, name=name, out_shape=[jax.ShapeDtypeStruct((N_CHIP, g.shape[1] // 2, g.shape[2]), g.dtype) for g in gs],
        in_specs=[_ANY] * n, out_specs=[_ANY] * n,
        scratch_shapes=[pltpu.SemaphoreType.DMA((n,)), pltpu.SemaphoreType.DMA((n,))],
    )(*gs)


def grad_chip_exchange(name, ps):
    n = len(ps)

    def body(*refs):
        ins, outs = refs[:n], refs[n:2 * n]
        send_sems, recv_sems = refs[2 * n:]
        mx, my, mc = _me()
        ci = 2 * mx + my
        chips = [(1 - mx, my), (mx, 1 - my), (1 - mx, 1 - my)]
        sends = []
        for i in range(n):
            for k, (px, py) in enumerate(chips):
                sends.append(pltpu.make_async_remote_copy(
                    src_ref=ins[i].at[2 * px + py], dst_ref=outs[i].at[ci], send_sem=send_sems.at[i, k],
                    recv_sem=recv_sems.at[i, k], device_id=(px, py, mc), device_id_type=MESH))
                sends[-1].start()
        for i in range(n):
            for k, (px, py) in enumerate(chips):
                pltpu.make_async_remote_copy(
                    src_ref=ins[i].at[ci], dst_ref=outs[i].at[2 * px + py], send_sem=send_sems.at[i, k],
                    recv_sem=recv_sems.at[i, k], device_id=(px, py, mc), device_id_type=MESH).wait_recv()
        for cp in sends:
            cp.wait_send()

    return pl.pallas_call(
        body, name=name, out_shape=[jax.ShapeDtypeStruct(p.shape, p.dtype) for p in ps], in_specs=[_ANY] * n,
        out_specs=[_ANY] * n, scratch_shapes=[pltpu.SemaphoreType.DMA((n, 3)), pltpu.SemaphoreType.DMA((n, 3))],
    )(*ps)


def grad_half_exchange(name, shards):
    n = len(shards)

    def body(*refs):
        outs = refs[n:2 * n]
        send_sems, recv_sems = refs[2 * n:]
        mx, my, mc = _me()

        def copy(i, core):
            hr = shards[i].shape[1] // 2
            rows = outs[i].at[:, pl.ds(core * hr, hr), :]
            return pltpu.make_async_remote_copy(src_ref=rows, dst_ref=rows, send_sem=send_sems.at[i],
                                                recv_sem=recv_sems.at[i], device_id=(mx, my, 1 - mc), device_id_type=MESH)

        sends = [copy(i, mc) for i in range(n)]
        for cp in sends:
            cp.start()
        for i in range(n):
            copy(i, 1 - mc).wait_recv()
        for cp in sends:
            cp.wait_send()

    return pl.pallas_call(
        body, name=name, out_shape=[jax.ShapeDtypeStruct(s.shape, s.dtype) for s in shards], in_specs=[_ANY] * n,
        out_specs=[_ANY] * n, input_output_aliases={i: i for i in range(n)},
        scratch_shapes=[pltpu.SemaphoreType.DMA((n,)), pltpu.SemaphoreType.DMA((n,))],
    )(*shards)


WEIGHTS = ['w_mod', 'b_mod', 'norm_mix_g', 'w_in', 'gdn_conv_w', 'gdn_a_log', 'gdn_dt_bias', 'gdn_norm_g', 'rg_conv_w',
           'rg_conv_b', 'rg_w_a', 'rg_b_a', 'rg_w_x', 'rg_b_x', 'rg_lambda', 'mla_q_norm_g', 'mla_w_qb', 'mla_kv_norm_g',
           'mla_w_kvb', 'w_out', 'norm_mlp_g', 'w_mlp_in', 'w_mlp_out', 'final_norm_g']
SHARDED = {'w_in': 2, 'gdn_conv_w': 2, 'rg_conv_w': 2, 'mla_w_qb': 2, 'mla_w_kvb': 2, 'w_out': 1, 'w_mlp_in': 2, 'w_mlp_out': 1}
GATHER_BF16 = ('w_in', 'mla_w_qb', 'mla_w_kvb', 'w_out', 'w_mlp_in', 'w_mlp_out')
GATHER_FIRST = GATHER_BF16[:4]
REPLICATED = [n for n in WEIGHTS if n not in SHARDED and n != 'w_mod']
PACK_COLS = 1024


def _pack(arrays, multiple):
    flat = jnp.concatenate([a.reshape(-1) for a in arrays])
    pad = (-flat.shape[0]) % multiple
    return jnp.pad(flat, (0, pad)) if pad else flat


def _unpack(flat, shapes):
    out, o = [], 0
    for shp in shapes:
        n = int(np.prod(shp))
        out.append(flat[o:o + n].reshape(shp))
        o += n
    return out


def _pack_rows(arrays):
    rows = []
    for a in arrays:
        flat = a.reshape(-1)
        pad = (-flat.shape[0]) % PACK_COLS
        rows.append((jnp.pad(flat, (0, pad)) if pad else flat).reshape(-1, PACK_COLS))
    out = jnp.concatenate(rows, axis=0)
    pad = (-out.shape[0]) % 8
    return jnp.pad(out, ((0, pad), (0, 0))) if pad else out


def _unpack_rows(packed, shapes):
    out, r = [], 0
    for shp in shapes:
        n = int(np.prod(shp))
        nr = -(-n // PACK_COLS)
        piece = packed[r:r + nr]
        out.append((piece if n == nr * PACK_COLS else piece.reshape(-1)[:n]).reshape(shp))
        r += nr
    return out


def _unshard(stacked, axis):
    moved = jnp.moveaxis(stacked, 0, axis)
    shp = list(moved.shape)
    shp[axis:axis + 2] = [shp[axis] * shp[axis + 1]]
    return moved.reshape(shp)


def _shard(full, axis):
    shp = list(full.shape)
    shp[axis:axis + 1] = [N_CHIP, shp[axis] // N_CHIP]
    return jnp.moveaxis(full.reshape(shp), axis, 0)


def _proj_cols(w):
    pad = jnp.zeros(w.shape[:-1] + (PROJ_WIDTH - w.shape[-1],), w.dtype)
    return jnp.concatenate([w[..., 0:1024], w[..., 1032:2472], w[..., 1024:1032], pad], axis=-1)


def _proj_cols_back(d):
    return jnp.concatenate([d[..., 0:1024], d[..., 2464:2472], d[..., 1024:2464]], axis=-1)


def _heads_split(w, heads, first):
    per = w.shape[-1] // heads
    r = w.reshape(w.shape[:-1] + (heads, per))
    lead = w.shape[:-1]
    return jnp.concatenate([r[..., :first].reshape(lead + (heads * first,)),
                            r[..., first:].reshape(lead + (heads * (per - first),))], axis=-1)


def _heads_merge(d, heads, first):
    lead = d.shape[:-1]
    per = d.shape[-1] // heads
    a = d[..., :heads * first].reshape(lead + (heads, first))
    b = d[..., heads * first:].reshape(lead + (heads, per - first))
    return jnp.concatenate([a, b], axis=-1).reshape(lead + (heads * per,))


def _block_diag(w):
    nl = w.shape[0]
    eye = jnp.eye(2, dtype=w.dtype)
    return jnp.einsum('lcoij,op->lcoipj', w.reshape(nl, 4, 2, 64, 64), eye).reshape(nl, 4, 128, 128)


def _block_diag_back(g):
    nl = g.shape[0]
    return jnp.einsum('lcoipj,op->lcoij', g.reshape(nl, 4, 2, 64, 2, 64), jnp.eye(2, dtype=g.dtype)).reshape(nl, 8, 64, 64)


def kernel(x, c, positions, w_mod, b_mod, norm_mix_g, w_in, gdn_conv_w, gdn_a_log, gdn_dt_bias, gdn_norm_g, rg_conv_w, rg_conv_b, rg_w_a, rg_b_a, rg_w_x, rg_b_x, rg_lambda, mla_q_norm_g, mla_w_qb, mla_kv_norm_g, mla_w_kvb, w_out, norm_mlp_g, w_mlp_in, w_mlp_out, final_norm_g, loss_target, m_w_mod, m_b_mod, m_norm_mix_g, m_w_in, m_gdn_conv_w, m_gdn_a_log, m_gdn_dt_bias, m_gdn_norm_g, m_rg_conv_w, m_rg_conv_b, m_rg_w_a, m_rg_b_a, m_rg_w_x, m_rg_b_x, m_rg_lambda, m_mla_q_norm_g, m_mla_w_qb, m_mla_kv_norm_g, m_mla_w_kvb, m_w_out, m_norm_mlp_g, m_w_mlp_in, m_w_mlp_out, m_final_norm_g, v_w_mod, v_b_mod, v_norm_mix_g, v_w_in, v_gdn_conv_w, v_gdn_a_log, v_gdn_dt_bias, v_gdn_norm_g, v_rg_conv_w, v_rg_conv_b, v_rg_w_a, v_rg_b_a, v_rg_w_x, v_rg_b_x, v_rg_lambda, v_mla_q_norm_g, v_mla_w_qb, v_mla_kv_norm_g, v_mla_w_kvb, v_w_out, v_norm_mlp_g, v_w_mlp_in, v_w_mlp_out, v_final_norm_g):
    given = dict(locals())
    wts = {n: given[n] for n in WEIGHTS}
    mom_m = {n: given["m_" + n] for n in WEIGHTS}
    mom_v = {n: given["v_" + n] for n in WEIGHTS}
    bsz, seq, d = x.shape
    depth = w_mod.shape[0]
    mx, my, mc = lax.axis_index("x"), lax.axis_index("y"), lax.axis_index("c")
    chip = 2 * mx + my
    dev = 2 * chip + mc

    conv_shapes = [wts['gdn_conv_w'].shape, wts['rg_conv_w'].shape]
    conv_flat = _pack([wts['gdn_conv_w'], wts['rg_conv_w']], d)
    conv_rows = conv_flat.shape[0] // d
    assert bsz + conv_rows <= 8
    c_pad = jnp.concatenate([c, conv_flat.reshape(conv_rows, d), jnp.zeros((8 - bsz - conv_rows, d), F32)], axis=0)
    gath = all_gather8("gather_c", c_pad, True).reshape(N_DEV, 8, d)
    c_all = gath[:, :bsz].reshape(N_DEV * bsz, d)
    conv_all = gath[0::2, bsz:bsz + conv_rows].reshape(N_CHIP, conv_rows * d)
    gdn_conv_full, rg_conv_full = [
        _unshard(jnp.stack([_unpack(conv_all[s], conv_shapes)[i] for s in range(N_CHIP)]), 2) for i in range(2)]

    n_half = N_DEV * bsz // 2
    mod_cols = w_mod.shape[2]
    c_rows = lax.dynamic_slice(c_all, (n_half * mc, 0), (n_half, d))
    b_mod_mine = lax.dynamic_slice(b_mod, (0, chip * mod_cols), (depth, mod_cols)).reshape(depth, 1, mod_cols)
    mod_piece = mod_matmul(c_rows, w_mod, b_mod_mine)
    mod_g = all_gather8("gather_mod", mod_piece.reshape(depth * n_half, mod_cols), True)
    mod_all = mod_g.reshape(N_CHIP, 2, depth, n_half, mod_cols).transpose(2, 1, 3, 0, 4).reshape(depth, 2 * n_half, 6 * d)
    mod_mine = lax.dynamic_slice(mod_all, (0, bsz * dev, 0), (depth, bsz, 6 * d)).reshape(depth, bsz, 6, 1, d)

    ids = jnp.stack([chip, mc]).astype(jnp.int32)
    slabs = dict(zip(GATHER_FIRST, all_gather_weights(
        "gather_weights", [cast_into_slab("cast_" + n, wts[n], ids) for n in GATHER_FIRST])))

    def columns(g):
        return g.transpose(0, 2, 1, 3).reshape(g.shape[0], g.shape[2], N_CHIP * g.shape[3])

    def rows_of(g):
        return g.reshape(g.shape[0], N_CHIP * g.shape[2], g.shape[3])

    w_cat = _proj_cols(columns(slabs['w_in']))
    w_q = columns(slabs['mla_w_qb']).astype(F32)
    w_kv = columns(slabs['mla_w_kvb']).astype(F32)
    w_out_full = rows_of(slabs['w_out'])
    bd_a, bd_x = _block_diag(rg_w_a), _block_diag(rg_w_x)

    inv_freq = ROPE_THETA ** (-jnp.arange(0, 32, 2, dtype=F32) / 32.0)
    ang = positions.astype(F32)[..., None] * inv_freq
    cs = jnp.concatenate([jnp.cos(ang), jnp.sin(ang)], axis=-1)

    proj_ch = [w for _, w in PROJ_PIECES]

    def row(a, l):
        return a[l].reshape(1, -1)

    def layer_args(l):
        sh_m, sc_m, gt_m, sh_f, sc_f, gt_f = (mod_mine[l, :, k] for k in range(6))
        return dict(
            mods=(sh_m, sc_m, gt_m, sh_f, sc_f, gt_f),
            mixer_in=dict(ex=[sc_m, sh_m], par=[row(norm_mix_g, l)], big=[(w_cat, l)], out_ch=proj_ch, ts=512),
            gdn_conv=dict(par_tiled=[gdn_conv_full[l]], out_ch=[768], ts=seq, nc=3),
            gdn_local=dict(par=[row(gdn_a_log, l), row(gdn_dt_bias, l)], out_ch=[256] * 7, ts=512),
            rglru=dict(par_tiled=[rg_conv_full[l], row(rg_conv_b, l), row(rg_b_a, l), row(rg_b_x, l), row(rg_lambda, l),
                                  bd_a[l], bd_x[l]], out_ch=[512], ts=seq, nc=4),
            mla_pre=dict(tok_nd=[cs], par=[row(mla_q_norm_g, l), row(mla_kv_norm_g, l), w_q[l], w_kv[l]],
                         out_ch=[ATTN_QW, ATTN_QW, ATTN_VW], ts=512),
            out_proj=dict(ex=[gt_m], big=[(w_out_full, l)], out_ch=[d, d], ts=512),
            mlp_in=dict(ex=[sc_f, sh_f], par=[row(norm_mlp_g, l)], big=[w_mi.get(l)], out_ch=[4 * d], ts=256),
            mlp_out=dict(ex=[gt_f], big=[w_mo.get(l)], out_ch=[d, d], ts=256),
        )

    mi_buf = [cast_into_slab("cast_w_mlp_in%d" % l, wts['w_mlp_in'], ids, layer=l) for l in range(depth)]
    mo_buf = [cast_into_slab("cast_w_mlp_out%d" % l, wts['w_mlp_out'], ids, layer=l) for l in range(depth)]
    w_mi, w_mo = {}, {}

    def staged(name, fn, jobs, **kw):
        return run_stage(name, fn, side=jobs, **kw) if jobs else (run_stage(name, fn, **kw), [])

    saved = []
    h = x
    for l in range(depth):
        a = layer_args(l)
        sfx = str(l)
        first, more = l == 0, l + 1 < depth
        (qkv_raw, z, rx, rgate, mq, mkv, misc), bufs = staged(
            "mixer_in" + sfx, fn_mixer_in, [] if first else [gather_over_d2d(mo_buf[l])], tok=[h], **a['mixer_in'])
        if not first:
            w_mo[l] = bufs[0].reshape(N_CHIP * d, d)
        (qkv_act,) = run_stage("gdn_conv" + sfx, fn_gdn_conv, tok=[qkv_raw], **a['gdn_conv'])
        (*xs, inverses), bufs = staged("gdn_local" + sfx, fn_gdn_local, [gather_over_ici(mi_buf[l])] if first else [],
                                       tok=[qkv_act, misc], **a['gdn_local'])
        if first:
            mi_buf[l] = bufs[0]
        o_a, st_in = gdn_scan(xs, z, row(gdn_norm_g, l))
        (o_b,), bufs = staged("rglru" + sfx, fn_rglru, [gather_over_d2d(mi_buf[l]), gather_over_ici(mo_buf[l])] if first else [],
                              tok=[rx, rgate], **a['rglru'])
        if first:
            w_mi[l], mo_buf[l] = bufs
        q_at, k_at, v_at = run_stage("mla_pre" + sfx, fn_mla_pre, tok=[mq, mkv, misc], **a['mla_pre'])
        o_c = mla_attention(q_at, k_at, v_at)
        (h_mid, mix_out), bufs = staged("out_proj" + sfx, fn_out_proj, [gather_over_d2d(mo_buf[l])] if first else [],
                                tok=[h, o_a, o_b, o_c], **a['out_proj'])
        if first:
            w_mo[l] = bufs[0].reshape(N_CHIP * d, d)
        a = layer_args(l)
        (a_mlp,), bufs = staged("mlp_in" + sfx, fn_mlp_in, [gather_over_ici(mi_buf[l + 1])] if more else [],
                                tok=[h_mid], **a['mlp_in'])
        if more:
            mi_buf[l + 1] = bufs[0]
        (h_out, f_out), bufs = staged("mlp_out" + sfx, fn_mlp_out,
                                [gather_over_d2d(mi_buf[l + 1]), gather_over_ici(mo_buf[l + 1])] if more else [],
                                tok=[h_mid, a_mlp], **a['mlp_out'])
        if more:
            w_mi[l + 1], mo_buf[l + 1] = bufs
        saved.append(dict(h=h, qkv_raw=qkv_raw, z=z, rx=rx, rgate=rgate, mq=mq, mkv=mkv, misc=misc, qkv_act=qkv_act, xs=xs,
                          inverses=inverses,
                          st_in=st_in, o_a=o_a, o_b=o_b, o_c=o_c, q_at=q_at, k_at=k_at, v_at=v_at, h_mid=h_mid, a_mlp=a_mlp,
                          mix_out=mix_out, f_out=f_out))
        h = h_out

    loss_part, dh, d_final_g = loss_head(h, final_norm_g.reshape(1, d), loss_target)
    loss = lax.psum(loss_part[0, 0], ("x", "y", "c"))

    g_full = {n: [None] * depth for n in SHARDED}
    g_rep = {n: [None] * depth for n in REPLICATED if n not in ('final_norm_g', 'b_mod')}

    def column_slabs(g):
        return g.reshape(g.shape[0], N_CHIP, g.shape[1] // N_CHIP).transpose(1, 0, 2)

    def row_slabs(g):
        return g.reshape(N_CHIP, g.shape[0] // N_CHIP, g.shape[1])
    core_id = mc.reshape(1).astype(jnp.int32)
    shards = [None] * len(GATHER_BF16)
    mixer_units, mlp_in_unit, mlp_out_unit = [0, 1, 2, 3], [4], [5]

    def reduce_begin(tag, idxs, l):
        gs = [g_full[GATHER_BF16[i]][l] for i in idxs]
        from_sibling = grad_sibling_exchange("grad_sibling_exchange_" + tag, gs)
        pairs = [add_half("grad_add_%s%d" % (GATHER_BF16[i], l), g, s, core_id) for i, g, s in zip(idxs, gs, from_sibling)]
        return [p[0] for p in pairs], [p[1] for p in pairs]

    def reduce_end(idxs, l, sums32, landed):
        for i, p, r in zip(idxs, sums32, landed):
            n = GATHER_BF16[i]
            shards[i] = sum_peers("grad_sum_%s%d" % (n, l), p, r, ids, wts[n].shape, l, acc=shards[i])

    def staged_bwd(name, fn, idxs, l_units, pair, **kw):
        if pair is None:
            return run_stage(name, fn, **kw)
        groups, bufs = run_stage(name, fn, side=[chip_exchange_job(pair[1])], **kw)
        reduce_end(idxs, l_units, pair[0], bufs[len(idxs):])
        return groups

    dmod = [None] * depth
    carried = None
    for l in reversed(range(depth)):
        a, sv = layer_args(l), saved[l]
        sfx = str(l)
        mlp_out_tok = dict(tok=[sv['h_mid'], sv['a_mlp']], tok_nd=[sv['f_out']], cot=[dh])
        (dh_mid, da_mlp), (dgt_f,), _, _, _ = staged_bwd(
            "mlp_out" + sfx, fn_mlp_out, mixer_units, l + 1, carried, which="small", dtok_dtype={1: BF16},
            **mlp_out_tok, **{**a['mlp_out'], 'ts': 256})
        _, _, _, _, (dw_mlp_out,) = run_stage(
            "mlp_out" + sfx, fn_mlp_out, which="big", **mlp_out_tok, **{**a['mlp_out'], 'ts': 512})
        g_full['w_mlp_out'][l] = row_slabs(dw_mlp_out)
        _, _, _, _, (g_full['w_mlp_in'][l],) = run_stage(
            "mlp_in" + sfx, fn_mlp_in, tok=[sv['h_mid']], cot=[da_mlp], which="big", **{**a['mlp_in'], 'ts': 512})
        (dh_mid,), (dsc_f, dsh_f), (g_rep['norm_mlp_g'][l],), _, _ = run_stage(
            "mlp_in" + sfx, fn_mlp_in, tok=[sv['h_mid']], cot=[da_mlp], addin=dh_mid, which="small", **a['mlp_in'])
        mlp_sums32, mlp_sums16 = reduce_begin("mlp" + sfx, mlp_in_unit + mlp_out_unit, l)
        (dh_in, do_a, do_b, do_c), (dgt_m,), _, _, (dw_out,) = run_stage(
            "out_proj" + sfx, fn_out_proj, tok=[sv['h'], sv['o_a'], sv['o_b'], sv['o_c']], tok_nd=[sv['mix_out']],
            cot=[dh_mid], **a['out_proj'])
        g_full['w_out'][l] = row_slabs(dw_out)
        attn_cot = mla_attention_bwd(sv['q_at'], sv['k_at'], sv['v_at'], do_c)
        (dmq, dmkv, dmisc_c), _, (g_rep['mla_q_norm_g'][l], g_rep['mla_kv_norm_g'][l], dw_q, dw_kv), _, _ = run_stage(
            "mla_pre" + sfx, fn_mla_pre, tok=[sv['mq'], sv['mkv'], sv['misc']], cot=attn_cot, **a['mla_pre'])
        g_full['mla_w_qb'][l] = column_slabs(dw_q)
        g_full['mla_w_kvb'][l] = column_slabs(dw_kv)
        (drx, drgate), _, _, rg_g, _ = staged_bwd(
            "rglru" + sfx, fn_rglru, mlp_in_unit, l, (mlp_sums32[:1], mlp_sums16[:1]), tok=[sv['rx'], sv['rgate']], cot=[do_b],
            **a['rglru'])
        (g_full['rg_conv_w'][l], g_rep['rg_conv_b'][l], g_rep['rg_b_a'][l], g_rep['rg_b_x'][l], g_rep['rg_lambda'][l],
         g_rep['rg_w_a'][l], g_rep['rg_w_x'][l]) = rg_g
        dxs, dz, g_rep['gdn_norm_g'][l] = gdn_scan_bwd(sv['xs'], sv['z'], row(gdn_norm_g, l), sv['st_in'], do_a)
        (dqkv_act, dmisc_a), _, (g_rep['gdn_a_log'][l], g_rep['gdn_dt_bias'][l]), _, _ = staged_bwd(
            "gdn_local" + sfx, fn_gdn_local, mlp_out_unit, l, (mlp_sums32[1:], mlp_sums16[1:]),
            tok=[sv['qkv_act'], sv['misc']], tok_nd=[sv['inverses']], cot=dxs, **a['gdn_local'])
        (dqkv_raw,), _, _, (g_full['gdn_conv_w'][l],), _ = run_stage(
            "gdn_conv" + sfx, fn_gdn_conv, tok=[sv['qkv_raw']], cot=[dqkv_act], **a['gdn_conv'])
        (dh,), (dsc_m, dsh_m), (g_rep['norm_mix_g'][l],), _, (dw_cat,) = run_stage(
            "mixer_in" + sfx, fn_mixer_in, tok=[sv['h']], cot=[dqkv_raw, dz, drx, drgate, dmq, dmkv, dmisc_a + dmisc_c],
            addin=dh_in, **a['mixer_in'])
        g_full['w_in'][l] = column_slabs(_proj_cols_back(dw_cat))
        dmod[l] = jnp.concatenate([dsh_m, dsc_m, dgt_m, dsh_f, dsc_f, dgt_f], axis=-1).reshape(bsz, 6 * d)
        carried = reduce_begin("mixer" + sfx, mixer_units, l)
    grad_x = dh
    reduce_end(mixer_units, 0, carried[0], grad_chip_exchange("grad_chip_exchange", carried[1]))

    dmod = jnp.stack(dmod)
    dmod_pad = jnp.concatenate([dmod.reshape(depth * bsz, 6 * d), jnp.zeros((8 - depth * bsz, 6 * d), F32)], axis=0)
    dmod_all = all_gather8("gather_dmod", dmod_pad, True).reshape(N_DEV, 8, 6 * d)[:, :depth * bsz]
    dmod_all = dmod_all.reshape(N_DEV, depth, bsz, 6 * d).transpose(1, 0, 2, 3).reshape(depth, N_DEV * bsz, 6 * d)
    g_w_mod = mod_weight_grad(c_all, lax.dynamic_slice(dmod_all, (0, 0, chip * mod_cols), (depth, N_DEV * bsz, mod_cols)))

    g_rep = {n: jnp.stack(v) for n, v in g_rep.items()}
    g_rep['rg_w_a'] = _block_diag_back(g_rep['rg_w_a'])
    g_rep['rg_w_x'] = _block_diag_back(g_rep['rg_w_x'])
    g_rep['final_norm_g'] = d_final_g
    g_rep['b_mod'] = jnp.sum(dmod, axis=1)
    conv_names = ['gdn_conv_w', 'rg_conv_w']
    conv_full_shapes = [(depth,) + g_full[n][0].shape for n in conv_names]
    small_shapes = [wts[n].shape for n in REPLICATED] + conv_full_shapes
    rep_part = _pack_rows([g_rep[n].reshape(wts[n].shape) for n in REPLICATED] + [jnp.stack(g_full[n]) for n in conv_names])
    rep_rows = rep_part.shape[0]
    rep_all = all_gather8("gather_small_grads", rep_part, True).reshape(N_DEV, rep_rows, PACK_COLS)
    conv_zeros = [jnp.zeros(s, F32) for s in conv_full_shapes]
    rep_out = adamw_reduce("adamw_small", rep_all, *[
        _pack_rows([src[n] for n in REPLICATED] + conv_zeros) for src in (wts, mom_m, mom_v)])
    small_names = REPLICATED + conv_names
    rep_g, rep_d, rep_m, rep_v = [dict(zip(small_names, _unpack_rows(o, small_shapes))) for o in rep_out]
    sh_g = {}
    for n in conv_names:
        cols = wts[n].shape[2]
        sh_g[n] = lax.dynamic_slice(rep_g.pop(n), (0, 0, chip * cols), wts[n].shape)
        for dct in (rep_d, rep_m, rep_v):
            dct.pop(n)

    sh_g.update(zip(GATHER_BF16, grad_half_exchange("grad_half_exchange", shards)))
    sh_names = list(SHARDED)

    def as2d(t):
        return t.reshape(-1, t.shape[-1])

    sh_d, sh_m, sh_v = {}, {}, {}
    for n in sh_names + ['w_mod']:
        g = g_w_mod if n == 'w_mod' else sh_g[n]
        res = adamw("adamw_" + n, as2d(wts[n]), as2d(g), as2d(mom_m[n]), as2d(mom_v[n]))
        sh_d[n], sh_m[n], sh_v[n] = (r.reshape(wts[n].shape) for r in res)
    sh_g['w_mod'] = g_w_mod

    def pick(shd, rep):
        return [shd[n] if n in shd else rep[n] for n in WEIGHTS]

    return (loss, grad_x, *pick(sh_g, rep_g), *pick(sh_d, rep_d), *pick(sh_m, rep_m), *pick(sh_v, rep_v))
```

```python
import functools

import jax
import jax.numpy as jnp
import numpy as np
from jax import lax
from jax.experimental import pallas as pl
from jax.experimental.pallas import tpu as pltpu

F32, BF16 = jnp.float32, jnp.bfloat16
HI = lax.Precision.HIGH
MESH = pl.DeviceIdType.MESH

EPS = 1e-6
CHUNK = 64
GDN_HEADS = 4
MLA_HEADS = 4
RG_C = 8.0
ROPE_THETA = 10000.0
N_DEV = 8
N_CHIP = 4
V7X_VMEM_LIMIT = 60 * 1024 * 1024
ADAM_LR, ADAM_B1, ADAM_B2, ADAM_EPS, ADAM_WD, ADAM_STEP = 0.001, 0.9, 0.999, 1e-08, 0.01, 10


def _params(n_grid):
    return pltpu.CompilerParams(dimension_semantics=("arbitrary",) * n_grid, vmem_limit_bytes=V7X_VMEM_LIMIT)


def _dot(a, b, dims=(((1,), (0,)), ((), ()))):
    return lax.dot_general(a.astype(BF16), b.astype(BF16), dims, preferred_element_type=F32)


@jax.custom_vjp
def _mm_probe(x, w, probe):
    return _dot(x, w)


def _mm_probe_fwd(x, w, probe):
    return _dot(x, w), (x, w)


def _mm_probe_bwd(res, dy):
    x, w = res
    dx = _dot(dy, w, (((1,), (1,)), ((), ())))
    dw = _dot(x, dy, (((0,), (0,)), ((), ())))
    return dx, jnp.zeros_like(w), dw


_mm_probe.defvjp(_mm_probe_fwd, _mm_probe_bwd)


@jax.custom_vjp
def _probe_only(x, probe):
    return jnp.zeros((x.shape[0], probe.shape[1]), F32)


def _probe_only_fwd(x, probe):
    return jnp.zeros((x.shape[0], probe.shape[1]), F32), x


def _probe_only_bwd(x, dy):
    return jnp.zeros_like(x), _dot(x, dy, (((0,), (0,)), ((), ())))


_probe_only.defvjp(_probe_only_fwd, _probe_only_bwd)


@jax.custom_vjp
def _mm_known(x, w, y):
    return y


_mm_known.defvjp(lambda x, w, y: (y, w),
                 lambda w, dy: (_dot(dy, w, (((1,), (1,)), ((), ()))), jnp.zeros_like(w), jnp.zeros_like(dy)))


@jax.custom_vjp
def _mm_known_probe(x, w, y, probe):
    return y


_mm_known_probe.defvjp(
    lambda x, w, y, probe: (y, (x, w)),
    lambda res, dy: (_dot(dy, res[1], (((1,), (1,)), ((), ()))), jnp.zeros_like(res[1]), jnp.zeros_like(dy),
                     _dot(res[0], dy, (((0,), (0,)), ((), ())))))


@jax.custom_vjp
def mmw(x, w):
    return _dot(x, w)


def _mmw_fwd(x, w):
    return _dot(x, w), (x, w)


def _mmw_bwd(res, dy):
    x, w = res
    return _dot(dy, w, (((1,), (1,)), ((), ()))), _dot(x, dy, (((0,), (0,)), ((), ())))


mmw.defvjp(_mmw_fwd, _mmw_bwd)


def rms(x, g):
    return x * lax.rsqrt(jnp.mean(x * x, axis=-1, keepdims=True) + EPS) * g


def _rows(shape):
    return lax.broadcasted_iota(jnp.int32, shape, 0)


def _shift_down(x, s, fill):
    return jnp.where(_rows(x.shape) < s, fill, pltpu.roll(x, s, 0))


def _shift_up(x, s, fill):
    n = x.shape[0]
    return jnp.where(_rows(x.shape) >= n - s, fill, pltpu.roll(x, n - s, 0))


def _make_tshift(s):
    @jax.custom_vjp
    def tshift(x):
        return _shift_down(x, s, 0.0)

    tshift.defvjp(lambda x: (_shift_down(x, s, 0.0), None), lambda _, dy: (_shift_up(dy, s, 0.0),))
    return tshift


_TSHIFT = {s: _make_tshift(s) for s in (1, 2, 3)}


def causal_conv4(x, w):
    y = x * w[3:4, :]
    for j in range(3):
        y = y + _TSHIFT[3 - j](x) * w[j:j + 1, :]
    return y


def _scan_steps(n):
    d = 1
    while d < n:
        yield d
        d *= 2


@jax.custom_vjp
def linscan(a, b):
    return _linscan_fwd_impl(a, b)


def _linscan_fwd_impl(a, b):
    for d in _scan_steps(a.shape[0]):
        b = a * _shift_down(b, d, 0.0) + b
        a = a * _shift_down(a, d, 1.0)
    return b


def _linscan_fwd(a, b):
    h = _linscan_fwd_impl(a, b)
    return h, (a, h)


def _linscan_bwd(res, dh):
    a, h = res
    an = _shift_up(a, 1, 0.0)
    lam = dh
    for d in _scan_steps(a.shape[0]):
        lam = an * _shift_up(lam, d, 0.0) + lam
        an = an * _shift_up(an, d, 1.0)
    return lam * _shift_down(h, 1, 0.0), lam


linscan.defvjp(_linscan_fwd, _linscan_bwd)


@jax.custom_vjp
def linscan_known(a, b, h):
    return h


linscan_known.defvjp(lambda a, b, h: (h, (a, h)), lambda res, dh: _linscan_bwd(res, dh) + (jnp.zeros_like(dh),))


def _chunk_scan(x, reverse):
    pos = _rows(x.shape) % CHUNK
    n = x.shape[0]
    d = 1
    while d < CHUNK:
        if reverse:
            x = x + jnp.where(pos < CHUNK - d, pltpu.roll(x, n - d, 0), 0.0)
        else:
            x = x + jnp.where(pos >= d, pltpu.roll(x, d, 0), 0.0)
        d *= 2
    return x


@jax.custom_vjp
def chunk_cumsum(x):
    return _chunk_scan(x, False)


@jax.custom_vjp
def chunk_revcumsum(x):
    return _chunk_scan(x, True)


chunk_cumsum.defvjp(lambda x: (_chunk_scan(x, False), None), lambda _, g: (_chunk_scan(g, True),))
chunk_revcumsum.defvjp(lambda x: (_chunk_scan(x, True), None), lambda _, g: (_chunk_scan(g, False),))


def _bmm(a, b, precision=None):
    return jnp.einsum('nij,njk->nik', a, b, precision=precision, preferred_element_type=F32)


@jax.custom_vjp
def inv_unit_lower(l):
    return _inv_impl(l)


def _inv_impl(l):
    n = l.shape[-1]
    eye = (_rows((n, n)) == lax.broadcasted_iota(jnp.int32, (n, n), 1)).astype(F32)
    p = -l
    a = eye + p
    k = 1
    while 2 * k < n:
        p = _bmm(p, p, HI)
        a = a + _bmm(a, p, HI)
        k *= 2
    return a


def _inv_fwd(l):
    a = _inv_impl(l)
    return a, a


def _inv_bwd(a, da):
    at = jnp.swapaxes(a, 1, 2)
    return (-_bmm(_bmm(at, da, HI), at, HI),)


inv_unit_lower.defvjp(_inv_fwd, _inv_bwd)


@jax.custom_vjp
def inv_unit_lower_known(l, a):
    return a


inv_unit_lower_known.defvjp(lambda l, a: (a, a), lambda a, da: (_inv_bwd(a, da)[0], jnp.zeros_like(a)))


def neg_expm1(y):
    series = -(y * (1.0 + y * (0.5 + y * (1.0 / 6.0 + y * (1.0 / 24.0)))))
    return jnp.where(y > -0.05, series, 1.0 - jnp.exp(y))


def run_stage(name, fn, *, tok, tok_nd=(), ex=(), par=(), par_tiled=(), big=(), out_ch, ts, nc=1, cot=None, addin=None,
              which="all", dtok_dtype=None, side=None):
    tok, tok_nd, ex, par, par_tiled, big = map(list, (tok, tok_nd, ex, par, par_tiled, big))
    big_layer = [b[1] if isinstance(b, tuple) else None for b in big]
    big_arrays = [b[0] if isinstance(b, tuple) else b for b in big]
    big = [jax.ShapeDtypeStruct(a.shape if lyr is None else a.shape[1:], a.dtype) for a, lyr in zip(big_arrays, big_layer)]
    bsz, seq, _ = tok[0].shape
    ts = min(ts, seq)
    ns = seq // ts
    grid = (nc, bsz, ns)

    def tok_spec(a):
        cb = a.shape[-1] // nc
        return pl.BlockSpec((None, ts, cb), lambda c, b, s: (b, s, c))

    def ex_spec(a):
        cb = a.shape[-1] // nc
        return pl.BlockSpec((None, 1, cb), lambda c, b, s: (b, 0, c))

    def full_spec(a, single=False):
        nd = a.ndim
        kw = dict(pipeline_mode=pl.Buffered(1)) if single else {}
        return pl.BlockSpec(a.shape, lambda c, b, s: (0,) * nd, **kw)

    def tiled_spec(a):
        if a.ndim == 2:
            return pl.BlockSpec((a.shape[0], a.shape[1] // nc), lambda c, b, s: (0, c))
        return pl.BlockSpec((None,) + a.shape[1:], lambda c, b, s: (c, 0, 0))

    def big_spec(a, lyr):
        if lyr is None:
            return full_spec(a, True)
        nd = a.ndim
        return pl.BlockSpec((None,) + a.shape[1:], lambda c, b, s: (lyr,) + (0,) * (nd - 1), pipeline_mode=pl.Buffered(1))

    n_tok, n_nd, n_ex, n_par, n_pt, n_big = map(len, (tok, tok_nd, ex, par, par_tiled, big))
    in_arrays = tok + tok_nd + ex + par + par_tiled + big_arrays
    in_specs = ([tok_spec(a) for a in tok + tok_nd] + [ex_spec(a) for a in ex] + [full_spec(a) for a in par]
                + [tiled_spec(a) for a in par_tiled] + [big_spec(a, lyr) for a, lyr in zip(big_arrays, big_layer)])
    out_tok_shapes = [jax.ShapeDtypeStruct((bsz, seq, ch), F32) for ch in out_ch]
    n_in = len(in_arrays)

    def split(vals):
        i = 0
        groups = []
        for n in (n_tok, n_nd, n_ex, n_par, n_pt, n_big):
            groups.append(list(vals[i:i + n]))
            i += n
        return groups

    def split_grads(vals):
        i = 0
        groups = []
        for n in (n_tok, n_ex, n_par, n_pt, n_big):
            groups.append(list(vals[i:i + n]))
            i += n
        return groups

    side = list(side or [])
    side_arrays = [a for job in side for a in job[0]]
    n_side = len(side_arrays)
    side_shapes = [jax.ShapeDtypeStruct(a.shape, a.dtype) for a in side_arrays]
    side_scratch = [pltpu.SemaphoreType.DMA((job[1],)) for job in side for _ in range(2)]

    def side_jobs(side_refs, sems):
        o = 0
        for j, (arrs, _, issue) in enumerate(side):
            yield issue(side_refs[o:o + len(arrs)], sems[2 * j], sems[2 * j + 1])
            o += len(arrs)

    def side_start(side_refs, sems):
        if side:
            c, b, s = pl.program_id(0), pl.program_id(1), pl.program_id(2)

            @pl.when(jnp.logical_and(jnp.logical_and(c == 0, b == 0), s == 0))
            def _():
                for starts, _, _ in side_jobs(side_refs, sems):
                    for cp in starts:
                        cp.start()

    def side_finish(side_refs, sems):
        if side:
            c, b, s = pl.program_id(0), pl.program_id(1), pl.program_id(2)

            @pl.when(jnp.logical_and(jnp.logical_and(c == nc - 1, b == bsz - 1), s == ns - 1))
            def _():
                for _, recv_waits, send_waits in side_jobs(side_refs, sems):
                    for cp in recv_waits:
                        cp.wait_recv()
                    for cp in send_waits:
                        cp.wait_send()

    if cot is None:
        n_out = len(out_tok_shapes)

        def body(*refs):
            tv, ndv, ev, pv, ptv, _ = split([r[...] for r in refs[:n_in - n_big]] + [None] * n_big)
            b_refs = refs[n_in - n_big:n_in]
            side_refs = refs[n_in + n_side + n_out:n_in + 2 * n_side + n_out]
            sems = refs[n_in + 2 * n_side + n_out:]
            side_start(side_refs, sems)
            outs = fn(tv, ndv, ev, pv, ptv,
                      lambda x, i, j=None, known=None: _dot(x, b_refs[i][...] if j is None else b_refs[i][j]))
            for r, o in zip(refs[n_in + n_side:], outs):
                r[...] = o
            side_finish(side_refs, sems)

        res = pl.pallas_call(
            body, name=name, grid=grid, in_specs=in_specs + [_ANY] * n_side,
            out_specs=[tok_spec(a) for a in out_tok_shapes] + [_ANY] * n_side,
            out_shape=out_tok_shapes + side_shapes, input_output_aliases={n_in + j: n_out + j for j in range(n_side)},
            scratch_shapes=side_scratch, compiler_params=_params(3))(*in_arrays, *side_arrays)
        return (res[:n_out], res[n_out:]) if side else res

    cot = list(cot)
    has_addin = addin is not None
    extra = cot + ([addin] if has_addin else [])
    n_cot = len(cot)
    want_small, want_big = which in ("all", "small"), which in ("all", "big")
    if not want_small:
        keep = [i for i in range(n_in - n_big) if not n_tok <= i < n_tok + n_nd]
        in_arrays, in_specs, n_in = [in_arrays[i] for i in keep], [in_specs[i] for i in keep], len(keep)
    small_arrays = tok + ex + par + par_tiled
    g_shapes = [jax.ShapeDtypeStruct(a.shape, F32) for a in (small_arrays if want_small else []) + (big if want_big else [])]
    for i, dt_ in (dtok_dtype or {}).items():
        g_shapes[i] = jax.ShapeDtypeStruct(g_shapes[i].shape, dt_)
    g_specs = (([tok_spec(a) for a in tok] + [ex_spec(a) for a in ex] + [full_spec(a) for a in par]
                + [tiled_spec(a) for a in par_tiled]) if want_small else []) + (
                    [full_spec(a, True) for a in big] if want_big else [])

    def body(*refs):
        c, b, s = pl.program_id(0), pl.program_id(1), pl.program_id(2)
        if want_small:
            n_small_in = n_tok + n_nd + n_ex + n_par + n_pt
            tv, ndv, ev, pv, ptv, _ = split([r[...] for r in refs[:n_small_in]] + [None] * n_big)
            b_refs = refs[n_small_in:n_in]
        else:
            vals = [r[...] for r in refs[:n_in]]
            tv, ndv, ev, pv, ptv, _ = split(vals[:n_tok] + [None] * n_nd + vals[n_tok:] + [None] * n_big)
            b_refs = []
        cots = [r[...].astype(F32) for r in refs[n_in:n_in + n_cot]]
        n_g = len(g_shapes)
        g_refs = list(refs[n_in + len(extra) + n_side:n_in + len(extra) + n_side + n_g])
        side_refs = refs[n_in + len(extra) + n_side + n_g:n_in + len(extra) + 2 * n_side + n_g]
        sems = refs[n_in + len(extra) + 2 * n_side + n_g:]
        side_start(side_refs, sems)
        probes = [jnp.zeros(w.shape, F32) if w.ndim == 2 else [jnp.zeros(w.shape[1:], F32) for _ in range(w.shape[0])]
                  for w in big]

        def f(tv_, ev_, pv_, ptv_, probes_):
            def mm(x, i, j=None, known=None):
                probe = None if probes_ is None else (probes_[i] if j is None else probes_[i][j])
                if not want_small:
                    return _probe_only(x, probe)
                w = b_refs[i][...] if j is None else b_refs[i][j]
                if known is not None:
                    return _mm_known(x, w, known) if probe is None else _mm_known_probe(x, w, known, probe)
                return _dot(x, w) if probe is None else _mm_probe(x, w, probe)

            return fn(tv_, ndv, ev_, pv_, ptv_, mm)

        dt = de = dp = dpt = dbg = ()
        if which == "all":
            dt, de, dp, dpt, dbg = jax.vjp(f, tv, ev, pv, ptv, probes)[1](cots)
        elif which == "small":
            dt, de, dp, dpt = jax.vjp(lambda *a: f(*a, None), tv, ev, pv, ptv)[1](cots)
        else:
            (dbg,) = jax.vjp(lambda p: f(tv, ev, pv, ptv, p), probes)[1](cots)
        if has_addin:
            dt = [dt[0] + refs[n_in + n_cot][...]] + list(dt[1:])
        if want_small:
            gt_r, ge_r, gp_r, gpt_r, gb_r = split_grads(g_refs + ([] if want_big else [None] * n_big))
        else:
            gt_r, ge_r, gp_r, gpt_r, gb_r = [], [], [], [], g_refs
        for r, g in zip(gt_r, dt):
            r[...] = g.astype(r.dtype)

        def accumulate(r, g, first):
            @pl.when(first)
            def _():
                r[...] = g

            @pl.when(jnp.logical_not(first))
            def _():
                r[...] += g

        for r, g in zip(ge_r, de):
            accumulate(r, g, s == 0)
        first_all = jnp.logical_and(jnp.logical_and(c == 0, b == 0), s == 0)
        for r, g in zip(gp_r, dp):
            accumulate(r, g, first_all)
        for r, g in zip(gpt_r, dpt):
            accumulate(r, g, jnp.logical_and(b == 0, s == 0))
        for r, g in zip(gb_r, dbg):
            if isinstance(g, (list, tuple)):
                for j, gj in enumerate(g):
                    accumulate(r.at[j], gj, first_all)
            else:
                accumulate(r, g, first_all)
        side_finish(side_refs, sems)

    n_args = n_in + len(extra)
    res = pl.pallas_call(
        body, name=name + "_bwd" + ("" if which == "all" else "_" + which), grid=grid,
        in_specs=in_specs + [tok_spec(a) for a in extra] + [_ANY] * n_side, out_specs=g_specs + [_ANY] * n_side,
        out_shape=g_shapes + side_shapes, input_output_aliases={n_args + j: len(g_shapes) + j for j in range(n_side)},
        scratch_shapes=side_scratch, compiler_params=_params(3))(*in_arrays, *extra, *side_arrays)
    res, side_out = list(res[:len(g_shapes)]), list(res[len(g_shapes):])
    groups = [[], [], [], [], res] if not want_small else split_grads(res + ([] if want_big else [None] * n_big))
    return (groups, side_out) if side else groups


PROJ_PIECES = (("qkv", 768), ("z", 256), ("rx", 512), ("rgate", 512), ("mq", 256), ("mkv", 128), ("misc", 128))
PROJ_WIDTH = sum(w for _, w in PROJ_PIECES)
MISC_KR, MISC_A, MISC_B = 0, 32, 36


def fn_mixer_in(tok, nd, ex, par, pt, mm):
    (h,), (sc, sh), (g,) = tok, ex, par
    proj = mm(rms(h, g) * (1.0 + sc) + sh, 0)
    outs, o = [], 0
    for _, w in PROJ_PIECES:
        outs.append(proj[:, o:o + w])
        o += w
    return outs


def fn_gdn_conv(tok, nd, ex, par, pt, mm):
    return [jax.nn.silu(causal_conv4(tok[0], pt[0]))]


def _tri_masks():
    r = _rows((CHUNK, CHUNK))
    c = lax.broadcasted_iota(jnp.int32, (CHUNK, CHUNK), 1)
    return (c <= r), (c < r)


def fn_gdn_local(tok, nd, ex, par, pt, mm):
    (qkv, misc), (a_log, dt_bias) = tok, par
    known = nd[0] if nd else None
    ts = qkv.shape[0]
    nb = ts // CHUNK
    lower, strict = _tri_masks()
    g_all = -jnp.exp(a_log) * jax.nn.softplus(misc[:, MISC_A:MISC_A + GDN_HEADS] + dt_bias)
    g_cum = chunk_cumsum(g_all)
    g_tot = g_cum + chunk_revcumsum(g_all) - g_all
    outs = [[] for _ in range(7)]
    for hd in range(GDN_HEADS):
        def head(x, base):
            return x[:, base + 64 * hd: base + 64 * hd + 64]

        def l2n(x):
            return x * lax.rsqrt(jnp.sum(x * x, axis=-1, keepdims=True) + EPS)

        q = (l2n(head(qkv, 0)) * (64.0 ** -0.5)).reshape(nb, CHUNK, 64)
        k = l2n(head(qkv, 256)).reshape(nb, CHUNK, 64)
        v = head(qkv, 512).reshape(nb, CHUNK, 64)
        b = misc[:, MISC_B + hd: MISC_B + hd + 1]
        beta = jax.nn.sigmoid(b).reshape(nb, CHUNK, 1)
        gi = jnp.broadcast_to(g_cum[:, hd:hd + 1].reshape(nb, CHUNK, 1), (nb, CHUNK, CHUNK))
        gl = jnp.broadcast_to(g_tot[:, hd:hd + 1].reshape(nb, CHUNK, 1), (nb, CHUNK, CHUNK))
        diff = gi - jnp.swapaxes(gi, 1, 2)
        decay = jnp.where(lower, jnp.exp(jnp.where(lower, diff, 0.0)), 0.0)
        kb = k * beta
        vb = v * beta
        kk = jnp.einsum('ncd,nsd->ncs', kb.astype(BF16), k.astype(BF16), preferred_element_type=F32)
        lmat = jnp.where(strict, kk * decay, 0.0)
        if known is None:
            amat = inv_unit_lower(lmat)
        else:
            amat = inv_unit_lower_known(lmat, known[:, 64 * hd: 64 * hd + 64].reshape(nb, CHUNK, 64))
        eg = jnp.exp(gi)
        u = _bmm(amat, vb, HI)
        w = _bmm(amat, kb * eg, HI)
        qk = jnp.einsum('ncd,nsd->ncs', q.astype(BF16), k.astype(BF16), preferred_element_type=F32) * decay
        qd = q * eg
        kt = k * jnp.exp(gl - gi)
        cd = jnp.exp(gl)
        for lst, val in zip(outs, (qk, qd, u, w, kt, cd) + (() if known is not None else (amat,))):
            lst.append(val.reshape(ts, 64))
    return [jnp.concatenate(lst, axis=-1) for lst in outs if lst]


def fn_rglru(tok, nd, ex, par, pt, mm):
    (rx, rgate), (conv_w, conv_b, b_a, b_x, lam, bd_a, bd_x) = tok, pt
    xc = causal_conv4(rx, conv_w) + conv_b
    r = jax.nn.sigmoid(mmw(xc, bd_a) + b_a)
    i = jax.nn.sigmoid(mmw(xc, bd_x) + b_x)
    log_a = -RG_C * r * jax.nn.softplus(-lam)
    a = jnp.exp(log_a)
    bterm = jnp.sqrt(neg_expm1(2.0 * log_a)) * (i * xc)
    hidden = linscan_known(a, bterm, nd[0]) if nd else linscan(a, bterm)
    return [hidden * jax.nn.gelu(rgate)] + ([] if nd else [hidden])


def _rope32(x, cos, sin):
    x1, x2 = x[:, :16], x[:, 16:32]
    return jnp.concatenate([x1 * cos - x2 * sin, x2 * cos + x1 * sin], axis=-1)


MLA_QK = 96


def fn_mla_pre(tok, nd, ex, par, pt, mm):
    (mq, mkv, misc), (cs,), (g_q, g_kv, w_q, w_kv) = tok, nd, par
    q = mmw(rms(mq, g_q), w_q)
    kv = mmw(rms(mkv, g_kv), w_kv)
    cos, sin = cs[:, 0:16], cs[:, 16:32]
    kp = _rope32(misc[:, MISC_KR:MISC_KR + 32], cos, sin)
    qs, ks, vs = [], [], []
    for h in range(MLA_HEADS):
        qs += [q[:, MLA_QK * h: MLA_QK * h + 64], _rope32(q[:, MLA_QK * h + 64: MLA_QK * h + 96], cos, sin)]
        ks += [kv[:, 128 * h: 128 * h + 64], kp]
        vs.append(kv[:, 128 * h + 64: 128 * h + 128])
    return [jnp.concatenate(qs, axis=-1), jnp.concatenate(ks, axis=-1), jnp.concatenate(vs, axis=-1)]


def fn_out_proj(tok, nd, ex, par, pt, mm):
    (h, o_a, o_b, o_c), (gt,) = tok, ex
    mix = mm(jnp.concatenate([o_a, o_b, o_c], axis=-1), 0, known=nd[0] if nd else None)
    return [h + gt * mix] + ([] if nd else [mix])


def fn_mlp_in(tok, nd, ex, par, pt, mm):
    (h,), (sc, sh), (g,) = tok, ex, par
    u = rms(h, g) * (1.0 + sc) + sh
    return [jnp.concatenate([mm(u, 0, j) for j in range(N_CHIP)], axis=-1)]


def fn_mlp_out(tok, nd, ex, par, pt, mm):
    (h, a), (gt,) = tok, ex
    f = mm(jnp.square(jax.nn.relu(a)), 0, known=nd[0] if nd else None)
    return [h + gt * f] + ([] if nd else [f])


GDN_W = GDN_HEADS * 64


def _head_mask():
    r = _rows((GDN_W, GDN_W)) // 64
    c = lax.broadcasted_iota(jnp.int32, (GDN_W, GDN_W), 1) // 64
    return r == c


def _heads_diag(x):
    return jnp.where(_head_mask(), jnp.concatenate([x] * GDN_HEADS, axis=0), 0.0)


def _heads_compact(s):
    return s[0:64] + s[64:128] + s[128:192] + s[192:256]


def _gdn_step(state, qk, qd, u, w, kt, cd, z, norm_g):
    v_new = u - _dot(w, state)
    o = _dot(qd, state) + _dot(qk, _heads_diag(v_new))
    update = _dot(kt, v_new, (((0,), (0,)), ((), ())))
    new_state = state * jnp.broadcast_to(cd[0:1, :], (GDN_W, GDN_W)) + jnp.where(_head_mask(), update, 0.0)
    outs = [rms(o[:, 64 * hd: 64 * hd + 64], norm_g) * jax.nn.silu(z[:, 64 * hd: 64 * hd + 64]) for hd in range(GDN_HEADS)]
    return new_state, jnp.concatenate(outs, axis=-1)


def gdn_scan(xs, z, norm_g):
    bsz, seq, _ = z.shape
    n = seq // CHUNK
    blk = pl.BlockSpec((bsz, CHUNK, 256), lambda i: (0, i, 0))

    def body(qk, qd, u, w, kt, cd, z_ref, g_ref, o_ref, st_out, st):
        @pl.when(pl.program_id(0) == 0)
        def _():
            st[...] = jnp.zeros_like(st)

        for b in range(bsz):
            state = st[b]
            st_out[b] = _heads_compact(state)
            st[b], o_ref[b] = _gdn_step(state, qk[b], qd[b], u[b], w[b], kt[b], cd[b], z_ref[b], g_ref[...])

    return pl.pallas_call(
        body, name="gdn_scan", grid=(n,), in_specs=[blk] * 7 + [pl.BlockSpec((1, 64), lambda i: (0, 0))],
        out_specs=[blk, blk], out_shape=[jax.ShapeDtypeStruct((bsz, seq, 256), F32)] * 2,
        scratch_shapes=[pltpu.VMEM((bsz, GDN_W, GDN_W), F32)], compiler_params=_params(1))(*xs, z, norm_g)


def gdn_scan_bwd(xs, z, norm_g, st_in, do):
    bsz, seq, _ = z.shape
    n = seq // CHUNK
    blk = pl.BlockSpec((bsz, CHUNK, 256), lambda i: (0, n - 1 - i, 0))
    gspec = pl.BlockSpec((1, 64), lambda i: (0, 0))

    def body(qk, qd, u, w, kt, cd, z_ref, g_ref, st_ref, do_ref, dqk, dqd, du, dw, dkt, dcd, dz, dg, dst):
        first = pl.program_id(0) == 0

        @pl.when(first)
        def _():
            dst[...] = jnp.zeros_like(dst)

        dg_sum = None
        for b in range(bsz):
            _, vjp = jax.vjp(_gdn_step, _heads_diag(st_ref[b]), qk[b], qd[b], u[b], w[b], kt[b], cd[b], z_ref[b], g_ref[...])
            grads = vjp((dst[b], do_ref[b]))
            dst[b] = jnp.where(_head_mask(), grads[0], 0.0)
            for r, g in zip((dqk, dqd, du, dw, dkt, dcd, dz), grads[1:8]):
                r[b] = g
            dg_sum = grads[8] if dg_sum is None else dg_sum + grads[8]

        @pl.when(first)
        def _():
            dg[...] = dg_sum

        @pl.when(jnp.logical_not(first))
        def _():
            dg[...] += dg_sum

    res = pl.pallas_call(
        body, name="gdn_scan_bwd", grid=(n,), in_specs=[blk] * 7 + [gspec, blk, blk],
        out_specs=[blk] * 7 + [gspec], out_shape=[jax.ShapeDtypeStruct((bsz, seq, 256), F32)] * 7
        + [jax.ShapeDtypeStruct((1, 64), F32)],
        scratch_shapes=[pltpu.VMEM((bsz, GDN_W, GDN_W), F32)], compiler_params=_params(1))(*xs, z, norm_g, st_in, do)
    return list(res[:6]), res[6], res[7]


ATTN_TQ = 256
ATTN_SCALE = 96.0 ** -0.5
ATTN_KEY_FRACTIONS = (4, 2, 1)


def _attn_head(q, k, v, q0):
    s = _dot(q, k, (((1,), (1,)), ((), ()))) * ATTN_SCALE
    qc = (q0 + _rows(s.shape)) // CHUNK
    kc = lax.broadcasted_iota(jnp.int32, s.shape, 1) // CHUNK
    s = jnp.where(kc <= qc, s, -1e30)
    p = jnp.exp(s - jnp.max(s, axis=-1, keepdims=True))
    p = p / jnp.sum(p, axis=-1, keepdims=True)
    return _dot(p, v)


def _key_lengths(seq):
    return sorted({max(ATTN_TQ, seq // f) for f in ATTN_KEY_FRACTIONS})


def _key_variant(i, seq):
    need = (i + 1) * ATTN_TQ
    return sum(((need > klen).astype(jnp.int32) for klen in _key_lengths(seq)[:-1]), jnp.int32(0))


ATTN_QW, ATTN_VW = MLA_HEADS * MLA_QK, MLA_HEADS * 64


def _attn_specs(seq):
    def qspec(ch):
        return pl.BlockSpec((None, ATTN_TQ, ch), lambda b, i: (b, i, 0))

    def kspec(ch):
        return pl.BlockSpec((None, seq, ch), lambda b, i: (b, 0, 0))

    return qspec, kspec


def mla_attention(q, k, v):
    bsz, seq, _ = q.shape
    qspec, kspec = _attn_specs(seq)

    def body(q_r, k_r, v_r, o_r):
        i = pl.program_id(1)
        q0 = i * ATTN_TQ

        def with_keys(klen):
            outs = [_attn_head(q_r[:, MLA_QK * h: MLA_QK * h + MLA_QK], k_r[0:klen, MLA_QK * h: MLA_QK * h + MLA_QK],
                               v_r[0:klen, 64 * h: 64 * h + 64], q0) for h in range(MLA_HEADS)]
            o_r[...] = jnp.concatenate(outs, axis=-1)

        for j, klen in enumerate(_key_lengths(seq)):
            pl.when(_key_variant(i, seq) == j)(functools.partial(with_keys, klen))

    return pl.pallas_call(
        body, name="mla_attention", grid=(bsz, seq // ATTN_TQ), in_specs=[qspec(ATTN_QW), kspec(ATTN_QW), kspec(ATTN_VW)],
        out_specs=qspec(ATTN_VW), out_shape=jax.ShapeDtypeStruct((bsz, seq, ATTN_VW), F32), compiler_params=_params(2))(
            q, k, v)


def mla_attention_bwd(q, k, v, do):
    bsz, seq, _ = q.shape
    qspec, kspec = _attn_specs(seq)

    def body(q_r, k_r, v_r, do_r, dq_r, dk_r, dv_r):
        i = pl.program_id(1)
        q0 = i * ATTN_TQ

        @pl.when(i == 0)
        def _():
            dk_r[...] = jnp.zeros_like(dk_r)
            dv_r[...] = jnp.zeros_like(dv_r)

        def with_keys(klen):
            dq, dk, dv = [], [], []
            for h in range(MLA_HEADS):
                qk = slice(MLA_QK * h, MLA_QK * h + MLA_QK)
                sl = slice(64 * h, 64 * h + 64)
                _, vjp = jax.vjp(functools.partial(_attn_head, q0=q0), q_r[:, qk], k_r[0:klen, qk], v_r[0:klen, sl])
                a, b, c = vjp(do_r[:, sl])
                dq.append(a)
                dk.append(b)
                dv.append(c)
            dq_r[...] = jnp.concatenate(dq, axis=-1)
            dk_r[0:klen, :] += jnp.concatenate(dk, axis=-1)
            dv_r[0:klen, :] += jnp.concatenate(dv, axis=-1)

        for j, klen in enumerate(_key_lengths(seq)):
            pl.when(_key_variant(i, seq) == j)(functools.partial(with_keys, klen))

    shp = lambda ch: jax.ShapeDtypeStruct((bsz, seq, ch), F32)
    return pl.pallas_call(
        body, name="mla_attention_bwd", grid=(bsz, seq // ATTN_TQ),
        in_specs=[qspec(ATTN_QW), kspec(ATTN_QW), kspec(ATTN_VW), qspec(ATTN_VW)],
        out_specs=[qspec(ATTN_QW), kspec(ATTN_QW), kspec(ATTN_VW)],
        out_shape=[shp(ATTN_QW), shp(ATTN_QW), shp(ATTN_VW)], compiler_params=_params(2))(q, k, v, do)


LOSS_TS = 512


def loss_head(h, g, target):
    bsz, seq, d = h.shape
    ts = min(LOSS_TS, seq)
    tok = pl.BlockSpec((None, ts, d), lambda b, s: (b, s, 0))
    gspec = pl.BlockSpec((1, d), lambda b, s: (0, 0))
    lspec = pl.BlockSpec((1, 128), lambda b, s: (0, 0))

    def body(h_r, g_r, t_r, loss_r, dh_r, dg_r):
        first = jnp.logical_and(pl.program_id(0) == 0, pl.program_id(1) == 0)
        tv = t_r[...]

        def f(hv, gv):
            return 0.5 * jnp.sum(jnp.mean(jnp.square(rms(hv, gv) - tv), axis=-1, keepdims=True), axis=0, keepdims=True)

        val, vjp = jax.vjp(f, h_r[...], g_r[...])
        dh, dg = vjp(jnp.ones((1, 1), F32))
        dh_r[...] = dh
        lv = jnp.broadcast_to(val, (1, 128))

        @pl.when(first)
        def _():
            loss_r[...] = lv
            dg_r[...] = dg

        @pl.when(jnp.logical_not(first))
        def _():
            loss_r[...] += lv
            dg_r[...] += dg

    return pl.pallas_call(
        body, name="loss_head", grid=(bsz, seq // ts), in_specs=[tok, gspec, tok], out_specs=[lspec, tok, gspec],
        out_shape=[jax.ShapeDtypeStruct((1, 128), F32), jax.ShapeDtypeStruct(h.shape, F32), jax.ShapeDtypeStruct((1, d), F32)],
        compiler_params=_params(2))(h, g, target)


def _adamw_math(w, g, m, v):
    m = ADAM_B1 * m + (1.0 - ADAM_B1) * g
    v = ADAM_B2 * v + (1.0 - ADAM_B2) * jnp.square(g)
    m_hat = m / (1.0 - ADAM_B1 ** ADAM_STEP)
    v_hat = v / (1.0 - ADAM_B2 ** ADAM_STEP)
    return -ADAM_LR * (m_hat / (jnp.sqrt(v_hat) + ADAM_EPS) + ADAM_WD * w), m, v


def _row_block(rows, cols):
    want = max(8, (1 << 18) // cols)
    best = rows
    for r in range(8, rows + 1, 8):
        if rows % r == 0 and r <= want:
            best = r
    return best if rows % 8 == 0 else rows


def adamw(name, w, g, m, v):
    rows, cols = w.shape
    rb = _row_block(rows, cols)
    spec = pl.BlockSpec((rb, cols), lambda i: (i, 0))

    def body(w_r, g_r, m_r, v_r, d_o, m_o, v_o):
        d, mn, vn = _adamw_math(w_r[...], g_r[...], m_r[...], v_r[...])
        d_o[...] = d
        m_o[...] = mn
        v_o[...] = vn

    return pl.pallas_call(body, name=name, grid=(rows // rb,), in_specs=[spec] * 4, out_specs=[spec] * 3,
                          out_shape=[jax.ShapeDtypeStruct(w.shape, F32)] * 3, compiler_params=_params(1))(w, g, m, v)


def adamw_reduce(name, parts, w, m, v):
    rows, cols = w.shape
    rb = _row_block(rows, cols)
    spec = pl.BlockSpec((rb, cols), lambda i: (i, 0))
    pspec = pl.BlockSpec((N_DEV, rb, cols), lambda i: (0, i, 0))

    def body(p_r, w_r, m_r, v_r, g_o, d_o, m_o, v_o):
        g = p_r[0]
        for k in range(1, N_DEV):
            g = g + p_r[k]
        d, mn, vn = _adamw_math(w_r[...], g, m_r[...], v_r[...])
        g_o[...] = g
        d_o[...] = d
        m_o[...] = mn
        v_o[...] = vn

    return pl.pallas_call(body, name=name, grid=(rows // rb,), in_specs=[pspec, spec, spec, spec], out_specs=[spec] * 4,
                          out_shape=[jax.ShapeDtypeStruct(w.shape, F32)] * 4, compiler_params=_params(1))(parts, w, m, v)


MOD_CB = 512


def mod_matmul(c_rows, w_mod, b_mod):
    nl, d, cols = w_mod.shape

    def body(c_r, w_r, b_r, o_r):
        o_r[...] = _dot(jax.nn.silu(c_r[...]), w_r[...]) + b_r[...]

    return pl.pallas_call(
        body, name="mod_matmul", grid=(nl, cols // MOD_CB),
        in_specs=[pl.BlockSpec((8, d), lambda l, j: (0, 0)), pl.BlockSpec((None, d, MOD_CB), lambda l, j: (l, 0, j)),
                  pl.BlockSpec((None, 1, MOD_CB), lambda l, j: (l, 0, j))],
        out_specs=pl.BlockSpec((None, 8, MOD_CB), lambda l, j: (l, 0, j)),
        out_shape=jax.ShapeDtypeStruct((nl, 8, cols), F32), compiler_params=_params(2))(c_rows, w_mod, b_mod)


def mod_weight_grad(c_all, dmod):
    nl, nb, cols = dmod.shape
    d = c_all.shape[1]

    def body(c_r, g_r, o_r):
        o_r[...] = _dot(jax.nn.silu(c_r[...]), g_r[...], (((0,), (0,)), ((), ())))

    return pl.pallas_call(
        body, name="mod_weight_grad", grid=(nl, cols // MOD_CB),
        in_specs=[pl.BlockSpec((nb, d), lambda l, j: (0, 0)), pl.BlockSpec((None, nb, MOD_CB), lambda l, j: (l, 0, j))],
        out_specs=pl.BlockSpec((None, d, MOD_CB), lambda l, j: (l, 0, j)),
        out_shape=jax.ShapeDtypeStruct((nl, d, cols), F32), compiler_params=_params(2))(c_all, dmod)


def _half_block(hr, cols):
    rb = _row_block(hr, cols)
    return rb if rb % 16 == 0 else hr


def add_half(name, g, s, core):
    _, r, cols = g.shape
    hr = r // 2
    rb = _half_block(hr, cols)
    nblk = hr // rb
    gspec = pl.BlockSpec((None, rb, cols), lambda k, i, c: (k, c[0] * nblk + i, 0))
    spec = pl.BlockSpec((None, rb, cols), lambda k, i, c: (k, i, 0))

    def body(c_r, g_r, s_r, o_r, ob_r):
        t = g_r[...] + s_r[...]
        o_r[...] = t
        ob_r[...] = t.astype(BF16)

    return pl.pallas_call(
        body, name=name, grid_spec=pltpu.PrefetchScalarGridSpec(num_scalar_prefetch=1, grid=(N_CHIP, nblk),
                                                                in_specs=[gspec, spec], out_specs=[spec, spec]),
        out_shape=[jax.ShapeDtypeStruct((N_CHIP, hr, cols), F32), jax.ShapeDtypeStruct((N_CHIP, hr, cols), BF16)],
        compiler_params=_params(2))(core, g, s)


def sum_peers(name, p32, recv, ids, shard_shape, layer, acc=None):
    _, hr, cols = p32.shape
    rb = _half_block(hr, cols)
    nblk = hr // rb

    def slot(k):
        return pl.BlockSpec((None, rb, cols), lambda i, c: ((c[0] + k) % N_CHIP, i, 0))

    def body(c_r, o_r, r1, r2, r3, *rest):
        rest[-1][...] = ((o_r[...] + r1[...].astype(F32)) + r2[...].astype(F32)) + r3[...].astype(F32)

    args = (ids, p32, recv, recv, recv) + (() if acc is None else (acc,))
    return pl.pallas_call(
        body, name=name, grid_spec=pltpu.PrefetchScalarGridSpec(
            num_scalar_prefetch=1, grid=(nblk,),
            in_specs=[slot(0), slot(1), slot(2), slot(3)] + ([] if acc is None else [_ANY]),
            out_specs=pl.BlockSpec((None, rb, cols), lambda i, c: (layer, c[1] * nblk + i, 0))),
        out_shape=jax.ShapeDtypeStruct(shard_shape, F32), input_output_aliases={} if acc is None else {5: 0},
        compiler_params=_params(1))(*args)


def cast_into_slab(name, w, ids, layer=None):
    nl, r, cols = w.shape
    hr = r // 2
    rb = _half_block(hr, cols)
    nblk = hr // rb

    def body(c_r, w_r, o_r):
        o_r[...] = w_r[...].astype(BF16)

    if layer is not None:
        return pl.pallas_call(
            body, name=name, grid_spec=pltpu.PrefetchScalarGridSpec(
                num_scalar_prefetch=1, grid=(nblk,),
                in_specs=[pl.BlockSpec((None, rb, cols), lambda i, c: (layer, c[1] * nblk + i, 0))],
                out_specs=pl.BlockSpec((None, rb, cols), lambda i, c: (c[0], c[1] * nblk + i, 0))),
            out_shape=jax.ShapeDtypeStruct((N_CHIP, r, cols), BF16), compiler_params=_params(1))(ids, w)
    return pl.pallas_call(
        body, name=name, grid_spec=pltpu.PrefetchScalarGridSpec(
            num_scalar_prefetch=1, grid=(nl, nblk),
            in_specs=[pl.BlockSpec((None, rb, cols), lambda l, i, c: (l, c[1] * nblk + i, 0))],
            out_specs=pl.BlockSpec((None, None, rb, cols), lambda l, i, c: (l, c[0], c[1] * nblk + i, 0))),
        out_shape=jax.ShapeDtypeStruct((nl, N_CHIP, r, cols), BF16), compiler_params=_params(2))(ids, w)


def _me():
    return lax.axis_index("x"), lax.axis_index("y"), lax.axis_index("c")


def all_gather8(name, x_shard, in_vmem):
    m_per, n = x_shard.shape
    space = pltpu.VMEM if in_vmem else pl.ANY

    def body(x_ref, out_ref, send_sems, recv_sems, local_sem):
        x, y, c = _me()
        me, sibling = (x, y, c), (x, y, 1 - c)
        chips = [(1 - x, y), (x, 1 - y), (1 - x, 1 - y)]

        def rows(px, py, pc):
            return out_ref.at[pl.ds((4 * px + 2 * py + pc) * m_per, m_per), :]

        def copy(k, block, to, src=None):
            return pltpu.make_async_remote_copy(
                src_ref=rows(*block) if src is None else src, dst_ref=rows(*block), send_sem=send_sems.at[k],
                recv_sem=recv_sems.at[k], device_id=to, device_id_type=MESH)

        mine = pltpu.make_async_copy(x_ref, rows(*me), local_sem)
        mine.start()
        first = [copy(0, me, sibling, src=x_ref)]
        first += [copy(1 + j, me, (*chip, c), src=x_ref) for j, chip in enumerate(chips)]
        for cp in first:
            cp.start()
        passed = [copy(4 + j, (*chip, c), sibling) for j, chip in enumerate(chips)]
        for j, chip in enumerate(chips):
            copy(1 + j, (*chip, c), me).wait_recv()
            passed[j].start()
        copy(0, sibling, me).wait_recv()
        for j, chip in enumerate(chips):
            copy(4 + j, (*chip, 1 - c), me).wait_recv()
        for cp in first + passed:
            cp.wait_send()
        mine.wait()

    return pl.pallas_call(
        body, name=name, out_shape=jax.ShapeDtypeStruct((N_DEV * m_per, n), x_shard.dtype),
        in_specs=[pl.BlockSpec(memory_space=space)], out_specs=pl.BlockSpec(memory_space=space),
        scratch_shapes=[pltpu.SemaphoreType.DMA((7,)), pltpu.SemaphoreType.DMA((7,)), pltpu.SemaphoreType.DMA],
    )(x_shard)


_ANY = pl.BlockSpec(memory_space=pl.ANY)


def all_gather_weights(name, slabs):
    n = len(slabs)

    def body(*refs):
        outs = refs[n:2 * n]
        send_sems, recv_sems = refs[2 * n:]
        x, y, c = _me()
        me, sibling = (x, y, c), (x, y, 1 - c)
        chips = [(1 - x, y), (x, 1 - y), (1 - x, 1 - y)]

        def view(i, px, py, pc):
            hr = slabs[i].shape[2] // 2
            return outs[i].at[:, 2 * px + py, pl.ds(pc * hr, hr), :]

        def copy(i, k, block, to):
            return pltpu.make_async_remote_copy(
                src_ref=view(i, *block), dst_ref=view(i, *block), send_sem=send_sems.at[i, k],
                recv_sem=recv_sems.at[i, k], device_id=to, device_id_type=MESH)

        first = []
        for i in range(n):
            first.append(copy(i, 0, me, sibling))
            first += [copy(i, 1 + j, me, (*chip, c)) for j, chip in enumerate(chips)]
        for cp in first:
            cp.start()
        passed = []
        for j, chip in enumerate(chips):
            for i in range(n):
                copy(i, 1 + j, (*chip, c), me).wait_recv()
                passed.append(copy(i, 4 + j, (*chip, c), sibling))
                passed[-1].start()
        for i in range(n):
            copy(i, 0, sibling, me).wait_recv()
            for j, chip in enumerate(chips):
                copy(i, 4 + j, (*chip, 1 - c), me).wait_recv()
        for cp in first + passed:
            cp.wait_send()

    return pl.pallas_call(
        body, name=name, out_shape=[jax.ShapeDtypeStruct(s.shape, s.dtype) for s in slabs],
        in_specs=[_ANY] * n, out_specs=[_ANY] * n, input_output_aliases={i: i for i in range(n)},
        scratch_shapes=[pltpu.SemaphoreType.DMA((n, 7)), pltpu.SemaphoreType.DMA((n, 7))],
    )(*slabs)


def _slab_block(slab, px, py, pc):
    hr = slab.shape[1] // 2
    return slab.at[2 * px + py, pl.ds(pc * hr, hr), :]


def gather_over_ici(slab):
    def issue(refs, send_sems, recv_sems):
        (buf,) = refs
        x, y, c = _me()
        peers = [(x, y, 1 - c), (1 - x, y, c), (x, 1 - y, c), (1 - x, 1 - y, c)]

        def copy(k, block, to):
            return pltpu.make_async_remote_copy(
                src_ref=_slab_block(buf, *block), dst_ref=_slab_block(buf, *block), send_sem=send_sems.at[k],
                recv_sem=recv_sems.at[k], device_id=to, device_id_type=MESH)

        sends = [copy(k, (x, y, c), p) for k, p in enumerate(peers)]
        arrivals = [copy(k, p, (x, y, c)) for k, p in enumerate(peers)]
        return sends, arrivals, sends

    return ([slab], 4, issue)


def gather_over_d2d(slab):
    def issue(refs, send_sems, recv_sems):
        (buf,) = refs
        x, y, c = _me()
        chips = [(1 - x, y), (x, 1 - y), (1 - x, 1 - y)]

        def copy(k, block):
            return pltpu.make_async_remote_copy(
                src_ref=_slab_block(buf, *block), dst_ref=_slab_block(buf, *block), send_sem=send_sems.at[k],
                recv_sem=recv_sems.at[k], device_id=(x, y, 1 - c), device_id_type=MESH)

        sends = [copy(k, (*chip, c)) for k, chip in enumerate(chips)]
        arrivals = [copy(k, (*chip, 1 - c)) for k, chip in enumerate(chips)]
        return sends, arrivals, sends

    return ([slab], 3, issue)


def chip_exchange_job(ps):
    n = len(ps)

    def issue(refs, send_sems, recv_sems):
        ins, outs = refs[:n], refs[n:]
        mx, my, mc = _me()
        ci = 2 * mx + my
        chips = [(1 - mx, my), (mx, 1 - my), (1 - mx, 1 - my)]
        sends, arrivals = [], []
        for i in range(n):
            for k, (px, py) in enumerate(chips):
                sem = 3 * i + k
                sends.append(pltpu.make_async_remote_copy(
                    src_ref=ins[i].at[2 * px + py], dst_ref=outs[i].at[ci], send_sem=send_sems.at[sem],
                    recv_sem=recv_sems.at[sem], device_id=(px, py, mc), device_id_type=MESH))
                arrivals.append(pltpu.make_async_remote_copy(
                    src_ref=ins[i].at[ci], dst_ref=outs[i].at[2 * px + py], send_sem=send_sems.at[sem],
                    recv_sem=recv_sems.at[sem], device_id=(px, py, mc), device_id_type=MESH))
        return sends, arrivals, sends

    return (list(ps) + [lax.empty(p.shape, p.dtype) for p in ps], 3 * n, issue)


def grad_sibling_exchange(name, gs):
    n = len(gs)

    def body(*refs):
        ins, outs = refs[:n], refs[n:2 * n]
        send_sems, recv_sems = refs[2 * n:]
        mx, my, mc = _me()
        cps = []
        for i in range(n):
            hr = gs[i].shape[1] // 2
            cps.append(pltpu.make_async_remote_copy(
                src_ref=ins[i].at[:, pl.ds((1 - mc) * hr, hr), :], dst_ref=outs[i], send_sem=send_sems.at[i],
                recv_sem=recv_sems.at[i], device_id=(mx, my, 1 - mc), device_id_type=MESH))
            cps[-1].start()
        for cp in cps:
            cp.wait()

    return pl.pallas_call(
        body, name=name, out_shape=[jax.ShapeDtypeStruct((N_CHIP, g.shape[1] // 2, g.shape[2]), g.dtype) for g in gs],
        in_specs=[_ANY] * n, out_specs=[_ANY] * n,
        scratch_shapes=[pltpu.SemaphoreType.DMA((n,)), pltpu.SemaphoreType.DMA((n,))],
    )(*gs)


def grad_chip_exchange(name, ps):
    n = len(ps)

    def body(*refs):
        ins, outs = refs[:n], refs[n:2 * n]
        send_sems, recv_sems = refs[2 * n:]
        mx, my, mc = _me()
        ci = 2 * mx + my
        chips = [(1 - mx, my), (mx, 1 - my), (1 - mx, 1 - my)]
        sends = []
        for i in range(n):
            for k, (px, py) in enumerate(chips):
                sends.append(pltpu.make_async_remote_copy(
                    src_ref=ins[i].at[2 * px + py], dst_ref=outs[i].at[ci], send_sem=send_sems.at[i, k],
                    recv_sem=recv_sems.at[i, k], device_id=(px, py, mc), device_id_type=MESH))
                sends[-1].start()
        for i in range(n):
            for k, (px, py) in enumerate(chips):
                pltpu.make_async_remote_copy(
                    src_ref=ins[i].at[ci], dst_ref=outs[i].at[2 * px + py], send_sem=send_sems.at[i, k],
                    recv_sem=recv_sems.at[i, k], device_id=(px, py, mc), device_id_type=MESH).wait_recv()
        for cp in sends:
            cp.wait_send()

    return pl.pallas_call(
        body, name=name, out_shape=[jax.ShapeDtypeStruct(p.shape, p.dtype) for p in ps], in_specs=[_ANY] * n,
        out_specs=[_ANY] * n, scratch_shapes=[pltpu.SemaphoreType.DMA((n, 3)), pltpu.SemaphoreType.DMA((n, 3))],
    )(*ps)


def grad_half_exchange(name, shards):
    n = len(shards)

    def body(*refs):
        outs = refs[n:2 * n]
        send_sems, recv_sems = refs[2 * n:]
        mx, my, mc = _me()

        def copy(i, core):
            hr = shards[i].shape[1] // 2
            rows = outs[i].at[:, pl.ds(core * hr, hr), :]
            return pltpu.make_async_remote_copy(src_ref=rows, dst_ref=rows, send_sem=send_sems.at[i],
                                                recv_sem=recv_sems.at[i], device_id=(mx, my, 1 - mc), device_id_type=MESH)

        sends = [copy(i, mc) for i in range(n)]
        for cp in sends:
            cp.start()
        for i in range(n):
            copy(i, 1 - mc).wait_recv()
        for cp in sends:
            cp.wait_send()

    return pl.pallas_call(
        body, name=name, out_shape=[jax.ShapeDtypeStruct(s.shape, s.dtype) for s in shards], in_specs=[_ANY] * n,
        out_specs=[_ANY] * n, input_output_aliases={i: i for i in range(n)},
        scratch_shapes=[pltpu.SemaphoreType.DMA((n,)), pltpu.SemaphoreType.DMA((n,))],
    )(*shards)


WEIGHTS = ['w_mod', 'b_mod', 'norm_mix_g', 'w_in', 'gdn_conv_w', 'gdn_a_log', 'gdn_dt_bias', 'gdn_norm_g', 'rg_conv_w',
           'rg_conv_b', 'rg_w_a', 'rg_b_a', 'rg_w_x', 'rg_b_x', 'rg_lambda', 'mla_q_norm_g', 'mla_w_qb', 'mla_kv_norm_g',
           'mla_w_kvb', 'w_out', 'norm_mlp_g', 'w_mlp_in', 'w_mlp_out', 'final_norm_g']
SHARDED = {'w_in': 2, 'gdn_conv_w': 2, 'rg_conv_w': 2, 'mla_w_qb': 2, 'mla_w_kvb': 2, 'w_out': 1, 'w_mlp_in': 2, 'w_mlp_out': 1}
GATHER_BF16 = ('w_in', 'mla_w_qb', 'mla_w_kvb', 'w_out', 'w_mlp_in', 'w_mlp_out')
GATHER_FIRST = GATHER_BF16[:4]
REPLICATED = [n for n in WEIGHTS if n not in SHARDED and n != 'w_mod']
PACK_COLS = 1024


def _pack(arrays, multiple):
    flat = jnp.concatenate([a.reshape(-1) for a in arrays])
    pad = (-flat.shape[0]) % multiple
    return jnp.pad(flat, (0, pad)) if pad else flat


def _unpack(flat, shapes):
    out, o = [], 0
    for shp in shapes:
        n = int(np.prod(shp))
        out.append(flat[o:o + n].reshape(shp))
        o += n
    return out


def _pack_rows(arrays):
    rows = []
    for a in arrays:
        flat = a.reshape(-1)
        pad = (-flat.shape[0]) % PACK_COLS
        rows.append((jnp.pad(flat, (0, pad)) if pad else flat).reshape(-1, PACK_COLS))
    out = jnp.concatenate(rows, axis=0)
    pad = (-out.shape[0]) % 8
    return jnp.pad(out, ((0, pad), (0, 0))) if pad else out


def _unpack_rows(packed, shapes):
    out, r = [], 0
    for shp in shapes:
        n = int(np.prod(shp))
        nr = -(-n // PACK_COLS)
        piece = packed[r:r + nr]
        out.append((piece if n == nr * PACK_COLS else piece.reshape(-1)[:n]).reshape(shp))
        r += nr
    return out


def _unshard(stacked, axis):
    moved = jnp.moveaxis(stacked, 0, axis)
    shp = list(moved.shape)
    shp[axis:axis + 2] = [shp[axis] * shp[axis + 1]]
    return moved.reshape(shp)


def _shard(full, axis):
    shp = list(full.shape)
    shp[axis:axis + 1] = [N_CHIP, shp[axis] // N_CHIP]
    return jnp.moveaxis(full.reshape(shp), axis, 0)


_PROJ_SEGMENTS = ((0, 1024, 0), (1032, 2472, 1024), (1024, 1032, 2464))


def _proj_cols_from_slabs(slabs):
    w = slabs.shape[-1]
    pieces = []
    for o0, o1, _ in sorted(_PROJ_SEGMENTS, key=lambda s: s[2]):
        for k in range(N_CHIP):
            a, b = max(o0, k * w), min(o1, (k + 1) * w)
            if a < b:
                pieces.append(slabs[:, k, :, a - k * w: b - k * w])
    used = sum(p.shape[-1] for p in pieces)
    pieces.append(jnp.zeros(slabs.shape[:1] + slabs.shape[2:3] + (PROJ_WIDTH - used,), slabs.dtype))
    return jnp.concatenate(pieces, axis=-1)


def _proj_slabs(d, w):
    slabs = []
    for k in range(N_CHIP):
        pieces = []
        for o0, o1, c0 in sorted(_PROJ_SEGMENTS):
            a, b = max(o0, k * w), min(o1, (k + 1) * w)
            if a < b:
                pieces.append(d[:, c0 + a - o0: c0 + b - o0])
        slabs.append(jnp.concatenate(pieces, axis=-1))
    return jnp.stack(slabs)


def _heads_split(w, heads, first):
    per = w.shape[-1] // heads
    r = w.reshape(w.shape[:-1] + (heads, per))
    lead = w.shape[:-1]
    return jnp.concatenate([r[..., :first].reshape(lead + (heads * first,)),
                            r[..., first:].reshape(lead + (heads * (per - first),))], axis=-1)


def _heads_merge(d, heads, first):
    lead = d.shape[:-1]
    per = d.shape[-1] // heads
    a = d[..., :heads * first].reshape(lead + (heads, first))
    b = d[..., heads * first:].reshape(lead + (heads, per - first))
    return jnp.concatenate([a, b], axis=-1).reshape(lead + (heads * per,))


def _block_diag(w):
    nl = w.shape[0]
    eye = jnp.eye(2, dtype=w.dtype)
    return jnp.einsum('lcoij,op->lcoipj', w.reshape(nl, 4, 2, 64, 64), eye).reshape(nl, 4, 128, 128)


def _block_diag_back(g):
    nl = g.shape[0]
    return jnp.einsum('lcoipj,op->lcoij', g.reshape(nl, 4, 2, 64, 2, 64), jnp.eye(2, dtype=g.dtype)).reshape(nl, 8, 64, 64)


def kernel(x, c, positions, w_mod, b_mod, norm_mix_g, w_in, gdn_conv_w, gdn_a_log, gdn_dt_bias, gdn_norm_g, rg_conv_w, rg_conv_b, rg_w_a, rg_b_a, rg_w_x, rg_b_x, rg_lambda, mla_q_norm_g, mla_w_qb, mla_kv_norm_g, mla_w_kvb, w_out, norm_mlp_g, w_mlp_in, w_mlp_out, final_norm_g, loss_target, m_w_mod, m_b_mod, m_norm_mix_g, m_w_in, m_gdn_conv_w, m_gdn_a_log, m_gdn_dt_bias, m_gdn_norm_g, m_rg_conv_w, m_rg_conv_b, m_rg_w_a, m_rg_b_a, m_rg_w_x, m_rg_b_x, m_rg_lambda, m_mla_q_norm_g, m_mla_w_qb, m_mla_kv_norm_g, m_mla_w_kvb, m_w_out, m_norm_mlp_g, m_w_mlp_in, m_w_mlp_out, m_final_norm_g, v_w_mod, v_b_mod, v_norm_mix_g, v_w_in, v_gdn_conv_w, v_gdn_a_log, v_gdn_dt_bias, v_gdn_norm_g, v_rg_conv_w, v_rg_conv_b, v_rg_w_a, v_rg_b_a, v_rg_w_x, v_rg_b_x, v_rg_lambda, v_mla_q_norm_g, v_mla_w_qb, v_mla_kv_norm_g, v_mla_w_kvb, v_w_out, v_norm_mlp_g, v_w_mlp_in, v_w_mlp_out, v_final_norm_g):
    given = dict(locals())
    wts = {n: given[n] for n in WEIGHTS}
    mom_m = {n: given["m_" + n] for n in WEIGHTS}
    mom_v = {n: given["v_" + n] for n in WEIGHTS}
    bsz, seq, d = x.shape
    depth = w_mod.shape[0]
    mx, my, mc = lax.axis_index("x"), lax.axis_index("y"), lax.axis_index("c")
    chip = 2 * mx + my
    dev = 2 * chip + mc

    conv_shapes = [wts['gdn_conv_w'].shape, wts['rg_conv_w'].shape]
    conv_flat = _pack([wts['gdn_conv_w'], wts['rg_conv_w']], d)
    conv_rows = conv_flat.shape[0] // d
    assert bsz + conv_rows <= 8
    c_pad = jnp.concatenate([c, conv_flat.reshape(conv_rows, d), jnp.zeros((8 - bsz - conv_rows, d), F32)], axis=0)
    gath = all_gather8("gather_c", c_pad, True).reshape(N_DEV, 8, d)
    c_all = gath[:, :bsz].reshape(N_DEV * bsz, d)
    conv_all = gath[0::2, bsz:bsz + conv_rows].reshape(N_CHIP, conv_rows * d)
    gdn_conv_full, rg_conv_full = [
        _unshard(jnp.stack([_unpack(conv_all[s], conv_shapes)[i] for s in range(N_CHIP)]), 2) for i in range(2)]

    n_half = N_DEV * bsz // 2
    mod_cols = w_mod.shape[2]
    c_rows = lax.dynamic_slice(c_all, (n_half * mc, 0), (n_half, d))
    b_mod_mine = lax.dynamic_slice(b_mod, (0, chip * mod_cols), (depth, mod_cols)).reshape(depth, 1, mod_cols)
    mod_piece = mod_matmul(c_rows, w_mod, b_mod_mine)
    mod_g = all_gather8("gather_mod", mod_piece.reshape(depth * n_half, mod_cols), True)
    mod_all = mod_g.reshape(N_CHIP, 2, depth, n_half, mod_cols).transpose(2, 1, 3, 0, 4).reshape(depth, 2 * n_half, 6 * d)
    mod_mine = lax.dynamic_slice(mod_all, (0, bsz * dev, 0), (depth, bsz, 6 * d)).reshape(depth, bsz, 6, 1, d)

    ids = jnp.stack([chip, mc]).astype(jnp.int32)
    slabs = dict(zip(GATHER_FIRST, all_gather_weights(
        "gather_weights", [cast_into_slab("cast_" + n, wts[n], ids) for n in GATHER_FIRST])))

    def columns(g):
        return g.transpose(0, 2, 1, 3).reshape(g.shape[0], g.shape[2], N_CHIP * g.shape[3])

    def rows_of(g):
        return g.reshape(g.shape[0], N_CHIP * g.shape[2], g.shape[3])

    w_cat = _proj_cols_from_slabs(slabs['w_in'])
    w_q = columns(slabs['mla_w_qb']).astype(F32)
    w_kv = columns(slabs['mla_w_kvb']).astype(F32)
    w_out_full = rows_of(slabs['w_out'])
    bd_a, bd_x = _block_diag(rg_w_a), _block_diag(rg_w_x)

    inv_freq = ROPE_THETA ** (-jnp.arange(0, 32, 2, dtype=F32) / 32.0)
    ang = positions.astype(F32)[..., None] * inv_freq
    cs = jnp.concatenate([jnp.cos(ang), jnp.sin(ang)], axis=-1)

    proj_ch = [w for _, w in PROJ_PIECES]

    def row(a, l):
        return a[l].reshape(1, -1)

    def layer_args(l):
        sh_m, sc_m, gt_m, sh_f, sc_f, gt_f = (mod_mine[l, :, k] for k in range(6))
        return dict(
            mods=(sh_m, sc_m, gt_m, sh_f, sc_f, gt_f),
            mixer_in=dict(ex=[sc_m, sh_m], par=[row(norm_mix_g, l)], big=[(w_cat, l)], out_ch=proj_ch, ts=512),
            gdn_conv=dict(par_tiled=[gdn_conv_full[l]], out_ch=[768], ts=seq, nc=3),
            gdn_local=dict(par=[row(gdn_a_log, l), row(gdn_dt_bias, l)], out_ch=[256] * 7, ts=512),
            rglru=dict(par_tiled=[rg_conv_full[l], row(rg_conv_b, l), row(rg_b_a, l), row(rg_b_x, l), row(rg_lambda, l),
                                  bd_a[l], bd_x[l]], out_ch=[512, 512], ts=seq, nc=4),
            mla_pre=dict(tok_nd=[cs], par=[row(mla_q_norm_g, l), row(mla_kv_norm_g, l), w_q[l], w_kv[l]],
                         out_ch=[ATTN_QW, ATTN_QW, ATTN_VW], ts=512),
            out_proj=dict(ex=[gt_m], big=[(w_out_full, l)], out_ch=[d, d], ts=512),
            mlp_in=dict(ex=[sc_f, sh_f], par=[row(norm_mlp_g, l)], big=[w_mi.get(l)], out_ch=[4 * d], ts=256),
            mlp_out=dict(ex=[gt_f], big=[w_mo.get(l)], out_ch=[d, d], ts=256),
        )

    mi_buf = [cast_into_slab("cast_w_mlp_in%d" % l, wts['w_mlp_in'], ids, layer=l) for l in range(depth)]
    mo_buf = [cast_into_slab("cast_w_mlp_out%d" % l, wts['w_mlp_out'], ids, layer=l) for l in range(depth)]
    w_mi, w_mo = {}, {}

    def staged(name, fn, jobs, **kw):
        return run_stage(name, fn, side=jobs, **kw) if jobs else (run_stage(name, fn, **kw), [])

    saved = []
    h = x
    for l in range(depth):
        a = layer_args(l)
        sfx = str(l)
        first, more = l == 0, l + 1 < depth
        (qkv_raw, z, rx, rgate, mq, mkv, misc), bufs = staged(
            "mixer_in" + sfx, fn_mixer_in, [] if first else [gather_over_d2d(mo_buf[l])], tok=[h], **a['mixer_in'])
        if not first:
            w_mo[l] = bufs[0].reshape(N_CHIP * d, d)
        (qkv_act,) = run_stage("gdn_conv" + sfx, fn_gdn_conv, tok=[qkv_raw], **a['gdn_conv'])
        (*xs, inverses), bufs = staged("gdn_local" + sfx, fn_gdn_local, [gather_over_ici(mi_buf[l])] if first else [],
                                       tok=[qkv_act, misc], **a['gdn_local'])
        if first:
            mi_buf[l] = bufs[0]
        o_a, st_in = gdn_scan(xs, z, row(gdn_norm_g, l))
        (o_b, rg_hidden), bufs = staged("rglru" + sfx, fn_rglru, [gather_over_d2d(mi_buf[l]), gather_over_ici(mo_buf[l])] if first else [],
                              tok=[rx, rgate], **a['rglru'])
        if first:
            w_mi[l], mo_buf[l] = bufs
        q_at, k_at, v_at = run_stage("mla_pre" + sfx, fn_mla_pre, tok=[mq, mkv, misc], **a['mla_pre'])
        o_c = mla_attention(q_at, k_at, v_at)
        (h_mid, mix_out), bufs = staged("out_proj" + sfx, fn_out_proj, [gather_over_d2d(mo_buf[l])] if first else [],
                                tok=[h, o_a, o_b, o_c], **a['out_proj'])
        if first:
            w_mo[l] = bufs[0].reshape(N_CHIP * d, d)
        a = layer_args(l)
        (a_mlp,), bufs = staged("mlp_in" + sfx, fn_mlp_in, [gather_over_ici(mi_buf[l + 1])] if more else [],
                                tok=[h_mid], **a['mlp_in'])
        if more:
            mi_buf[l + 1] = bufs[0]
        (h_out, f_out), bufs = staged("mlp_out" + sfx, fn_mlp_out,
                                [gather_over_d2d(mi_buf[l + 1]), gather_over_ici(mo_buf[l + 1])] if more else [],
                                tok=[h_mid, a_mlp], **a['mlp_out'])
        if more:
            w_mi[l + 1], mo_buf[l + 1] = bufs
        saved.append(dict(h=h, qkv_raw=qkv_raw, z=z, rx=rx, rgate=rgate, mq=mq, mkv=mkv, misc=misc, qkv_act=qkv_act, xs=xs,
                          inverses=inverses,
                          st_in=st_in, o_a=o_a, o_b=o_b, o_c=o_c, q_at=q_at, k_at=k_at, v_at=v_at, h_mid=h_mid, a_mlp=a_mlp,
                          rg_hidden=rg_hidden,
                          mix_out=mix_out, f_out=f_out))
        h = h_out

    loss_part, dh, d_final_g = loss_head(h, final_norm_g.reshape(1, d), loss_target)
    loss = lax.psum(loss_part[0, 0], ("x", "y", "c"))

    g_full = {n: [None] * depth for n in SHARDED}
    g_rep = {n: [None] * depth for n in REPLICATED if n not in ('final_norm_g', 'b_mod')}

    def column_slabs(g):
        return g.reshape(g.shape[0], N_CHIP, g.shape[1] // N_CHIP).transpose(1, 0, 2)

    def row_slabs(g):
        return g.reshape(N_CHIP, g.shape[0] // N_CHIP, g.shape[1])
    core_id = mc.reshape(1).astype(jnp.int32)
    shards = [None] * len(GATHER_BF16)
    mixer_units, mlp_in_unit, mlp_out_unit = [0, 1, 2, 3], [4], [5]

    def reduce_begin(tag, idxs, l):
        gs = [g_full[GATHER_BF16[i]][l] for i in idxs]
        from_sibling = grad_sibling_exchange("grad_sibling_exchange_" + tag, gs)
        pairs = [add_half("grad_add_%s%d" % (GATHER_BF16[i], l), g, s, core_id) for i, g, s in zip(idxs, gs, from_sibling)]
        return [p[0] for p in pairs], [p[1] for p in pairs]

    def reduce_end(idxs, l, sums32, landed):
        for i, p, r in zip(idxs, sums32, landed):
            n = GATHER_BF16[i]
            shards[i] = sum_peers("grad_sum_%s%d" % (n, l), p, r, ids, wts[n].shape, l, acc=shards[i])

    def staged_bwd(name, fn, idxs, l_units, pair, **kw):
        if pair is None:
            return run_stage(name, fn, **kw)
        groups, bufs = run_stage(name, fn, side=[chip_exchange_job(pair[1])], **kw)
        reduce_end(idxs, l_units, pair[0], bufs[len(idxs):])
        return groups

    dmod = [None] * depth
    carried = None
    for l in reversed(range(depth)):
        a, sv = layer_args(l), saved[l]
        sfx = str(l)
        mlp_out_tok = dict(tok=[sv['h_mid'], sv['a_mlp']], tok_nd=[sv['f_out']], cot=[dh])
        (dh_mid, da_mlp), (dgt_f,), _, _, _ = staged_bwd(
            "mlp_out" + sfx, fn_mlp_out, mixer_units, l + 1, carried, which="small", dtok_dtype={1: BF16},
            **mlp_out_tok, **{**a['mlp_out'], 'ts': 256})
        _, _, _, _, (dw_mlp_out,) = run_stage(
            "mlp_out" + sfx, fn_mlp_out, which="big", **mlp_out_tok, **{**a['mlp_out'], 'ts': 512})
        g_full['w_mlp_out'][l] = row_slabs(dw_mlp_out)
        _, _, _, _, (g_full['w_mlp_in'][l],) = run_stage(
            "mlp_in" + sfx, fn_mlp_in, tok=[sv['h_mid']], cot=[da_mlp], which="big", **{**a['mlp_in'], 'ts': 512})
        (dh_mid,), (dsc_f, dsh_f), (g_rep['norm_mlp_g'][l],), _, _ = run_stage(
            "mlp_in" + sfx, fn_mlp_in, tok=[sv['h_mid']], cot=[da_mlp], addin=dh_mid, which="small", **a['mlp_in'])
        mlp_sums32, mlp_sums16 = reduce_begin("mlp" + sfx, mlp_in_unit + mlp_out_unit, l)
        (dh_in, do_a, do_b, do_c), (dgt_m,), _, _, (dw_out,) = run_stage(
            "out_proj" + sfx, fn_out_proj, tok=[sv['h'], sv['o_a'], sv['o_b'], sv['o_c']], tok_nd=[sv['mix_out']],
            cot=[dh_mid], **a['out_proj'])
        g_full['w_out'][l] = row_slabs(dw_out)
        attn_cot = mla_attention_bwd(sv['q_at'], sv['k_at'], sv['v_at'], do_c)
        (dmq, dmkv, dmisc_c), _, (g_rep['mla_q_norm_g'][l], g_rep['mla_kv_norm_g'][l], dw_q, dw_kv), _, _ = run_stage(
            "mla_pre" + sfx, fn_mla_pre, tok=[sv['mq'], sv['mkv'], sv['misc']], cot=attn_cot, **a['mla_pre'])
        g_full['mla_w_qb'][l] = column_slabs(dw_q)
        g_full['mla_w_kvb'][l] = column_slabs(dw_kv)
        (drx, drgate), _, _, rg_g, _ = staged_bwd(
            "rglru" + sfx, fn_rglru, mlp_in_unit, l, (mlp_sums32[:1], mlp_sums16[:1]), tok=[sv['rx'], sv['rgate']],
            tok_nd=[sv['rg_hidden']], cot=[do_b],
            **a['rglru'])
        (g_full['rg_conv_w'][l], g_rep['rg_conv_b'][l], g_rep['rg_b_a'][l], g_rep['rg_b_x'][l], g_rep['rg_lambda'][l],
         g_rep['rg_w_a'][l], g_rep['rg_w_x'][l]) = rg_g
        dxs, dz, g_rep['gdn_norm_g'][l] = gdn_scan_bwd(sv['xs'], sv['z'], row(gdn_norm_g, l), sv['st_in'], do_a)
        (dqkv_act, dmisc_a), _, (g_rep['gdn_a_log'][l], g_rep['gdn_dt_bias'][l]), _, _ = staged_bwd(
            "gdn_local" + sfx, fn_gdn_local, mlp_out_unit, l, (mlp_sums32[1:], mlp_sums16[1:]),
            tok=[sv['qkv_act'], sv['misc']], tok_nd=[sv['inverses']], cot=dxs, **a['gdn_local'])
        (dqkv_raw,), _, _, (g_full['gdn_conv_w'][l],), _ = run_stage(
            "gdn_conv" + sfx, fn_gdn_conv, tok=[sv['qkv_raw']], cot=[dqkv_act], **a['gdn_conv'])
        (dh,), (dsc_m, dsh_m), (g_rep['norm_mix_g'][l],), _, (dw_cat,) = run_stage(
            "mixer_in" + sfx, fn_mixer_in, tok=[sv['h']], cot=[dqkv_raw, dz, drx, drgate, dmq, dmkv, dmisc_a + dmisc_c],
            addin=dh_in, **a['mixer_in'])
        g_full['w_in'][l] = _proj_slabs(dw_cat, wts['w_in'].shape[2])
        dmod[l] = jnp.concatenate([dsh_m, dsc_m, dgt_m, dsh_f, dsc_f, dgt_f], axis=-1).reshape(bsz, 6 * d)
        carried = reduce_begin("mixer" + sfx, mixer_units, l)
    grad_x = dh
    reduce_end(mixer_units, 0, carried[0], grad_chip_exchange("grad_chip_exchange", carried[1]))

    dmod = jnp.stack(dmod)
    dmod_pad = jnp.concatenate([dmod.reshape(depth * bsz, 6 * d), jnp.zeros((8 - depth * bsz, 6 * d), F32)], axis=0)
    dmod_all = all_gather8("gather_dmod", dmod_pad, True).reshape(N_DEV, 8, 6 * d)[:, :depth * bsz]
    dmod_all = dmod_all.reshape(N_DEV, depth, bsz, 6 * d).transpose(1, 0, 2, 3).reshape(depth, N_DEV * bsz, 6 * d)
    g_w_mod = mod_weight_grad(c_all, lax.dynamic_slice(dmod_all, (0, 0, chip * mod_cols), (depth, N_DEV * bsz, mod_cols)))

    g_rep = {n: jnp.stack(v) for n, v in g_rep.items()}
    g_rep['rg_w_a'] = _block_diag_back(g_rep['rg_w_a'])
    g_rep['rg_w_x'] = _block_diag_back(g_rep['rg_w_x'])
    g_rep['final_norm_g'] = d_final_g
    g_rep['b_mod'] = jnp.sum(dmod, axis=1)
    conv_names = ['gdn_conv_w', 'rg_conv_w']
    conv_full_shapes = [(depth,) + g_full[n][0].shape for n in conv_names]
    small_shapes = [wts[n].shape for n in REPLICATED] + conv_full_shapes
    rep_part = _pack_rows([g_rep[n].reshape(wts[n].shape) for n in REPLICATED] + [jnp.stack(g_full[n]) for n in conv_names])
    rep_rows = rep_part.shape[0]
    rep_all = all_gather8("gather_small_grads", rep_part, True).reshape(N_DEV, rep_rows, PACK_COLS)
    conv_zeros = [jnp.zeros(s, F32) for s in conv_full_shapes]
    rep_out = adamw_reduce("adamw_small", rep_all, *[
        _pack_rows([src[n] for n in REPLICATED] + conv_zeros) for src in (wts, mom_m, mom_v)])
    small_names = REPLICATED + conv_names
    rep_g, rep_d, rep_m, rep_v = [dict(zip(small_names, _unpack_rows(o, small_shapes))) for o in rep_out]
    sh_g = {}
    for n in conv_names:
        cols = wts[n].shape[2]
        sh_g[n] = lax.dynamic_slice(rep_g.pop(n), (0, 0, chip * cols), wts[n].shape)
        for dct in (rep_d, rep_m, rep_v):
            dct.pop(n)

    sh_g.update(zip(GATHER_BF16, grad_half_exchange("grad_half_exchange", shards)))
    sh_names = list(SHARDED)

    def as2d(t):
        return t.reshape(-1, t.shape[-1])

    sh_d, sh_m, sh_v = {}, {}, {}
    for n in sh_names + ['w_mod']:
        g = g_w_mod if n == 'w_mod' else sh_g[n]
        res = adamw("adamw_" + n, as2d(wts[n]), as2d(g), as2d(mom_m[n]), as2d(mom_v[n]))
        sh_d[n], sh_m[n], sh_v[n] = (r.reshape(wts[n].shape) for r in res)
    sh_g['w_mod'] = g_w_mod

    def pick(shd, rep):
        return [shd[n] if n in shd else rep[n] for n in WEIGHTS]

    return (loss, grad_x, *pick(sh_g, rep_g), *pick(sh_d, rep_d), *pick(sh_m, rep_m), *pick(sh_v, rep_v))
```

```python
import functools

import jax
import jax.numpy as jnp
import numpy as np
from jax import lax
from jax.experimental import pallas as pl
from jax.experimental.pallas import tpu as pltpu

F32, BF16 = jnp.float32, jnp.bfloat16
HI = lax.Precision.HIGH
MESH = pl.DeviceIdType.MESH

EPS = 1e-6
CHUNK = 64
GDN_HEADS = 4
MLA_HEADS = 4
RG_C = 8.0
ROPE_THETA = 10000.0
N_DEV = 8
N_CHIP = 4
V7X_VMEM_LIMIT = 60 * 1024 * 1024
ADAM_LR, ADAM_B1, ADAM_B2, ADAM_EPS, ADAM_WD, ADAM_STEP = 0.001, 0.9, 0.999, 1e-08, 0.01, 10


def _params(n_grid):
    return pltpu.CompilerParams(dimension_semantics=("arbitrary",) * n_grid, vmem_limit_bytes=V7X_VMEM_LIMIT)


def _dot(a, b, dims=(((1,), (0,)), ((), ()))):
    return lax.dot_general(a.astype(BF16), b.astype(BF16), dims, preferred_element_type=F32)


@jax.custom_vjp
def _mm_probe(x, w, probe):
    return _dot(x, w)


def _mm_probe_fwd(x, w, probe):
    return _dot(x, w), (x, w)


def _mm_probe_bwd(res, dy):
    x, w = res
    dx = _dot(dy, w, (((1,), (1,)), ((), ())))
    dw = _dot(x, dy, (((0,), (0,)), ((), ())))
    return dx, jnp.zeros_like(w), dw


_mm_probe.defvjp(_mm_probe_fwd, _mm_probe_bwd)


@jax.custom_vjp
def _probe_only(x, probe):
    return jnp.zeros((x.shape[0], probe.shape[1]), F32)


def _probe_only_fwd(x, probe):
    return jnp.zeros((x.shape[0], probe.shape[1]), F32), x


def _probe_only_bwd(x, dy):
    return jnp.zeros_like(x), _dot(x, dy, (((0,), (0,)), ((), ())))


_probe_only.defvjp(_probe_only_fwd, _probe_only_bwd)


@jax.custom_vjp
def _mm_known(x, w, y):
    return y


_mm_known.defvjp(lambda x, w, y: (y, w),
                 lambda w, dy: (_dot(dy, w, (((1,), (1,)), ((), ()))), jnp.zeros_like(w), jnp.zeros_like(dy)))


@jax.custom_vjp
def _mm_known_probe(x, w, y, probe):
    return y


_mm_known_probe.defvjp(
    lambda x, w, y, probe: (y, (x, w)),
    lambda res, dy: (_dot(dy, res[1], (((1,), (1,)), ((), ()))), jnp.zeros_like(res[1]), jnp.zeros_like(dy),
                     _dot(res[0], dy, (((0,), (0,)), ((), ())))))


@jax.custom_vjp
def mmw(x, w):
    return _dot(x, w)


def _mmw_fwd(x, w):
    return _dot(x, w), (x, w)


def _mmw_bwd(res, dy):
    x, w = res
    return _dot(dy, w, (((1,), (1,)), ((), ()))), _dot(x, dy, (((0,), (0,)), ((), ())))


mmw.defvjp(_mmw_fwd, _mmw_bwd)


def rms(x, g):
    return x * lax.rsqrt(jnp.mean(x * x, axis=-1, keepdims=True) + EPS) * g


def _rows(shape):
    return lax.broadcasted_iota(jnp.int32, shape, 0)


def _shift_down(x, s, fill):
    return jnp.where(_rows(x.shape) < s, fill, pltpu.roll(x, s, 0))


def _shift_up(x, s, fill):
    n = x.shape[0]
    return jnp.where(_rows(x.shape) >= n - s, fill, pltpu.roll(x, n - s, 0))


def _make_tshift(s):
    @jax.custom_vjp
    def tshift(x):
        return _shift_down(x, s, 0.0)

    tshift.defvjp(lambda x: (_shift_down(x, s, 0.0), None), lambda _, dy: (_shift_up(dy, s, 0.0),))
    return tshift


_TSHIFT = {s: _make_tshift(s) for s in (1, 2, 3)}


def causal_conv4(x, w):
    y = x * w[3:4, :]
    for j in range(3):
        y = y + _TSHIFT[3 - j](x) * w[j:j + 1, :]
    return y


def _scan_steps(n):
    d = 1
    while d < n:
        yield d
        d *= 2


@jax.custom_vjp
def linscan(a, b):
    return _linscan_fwd_impl(a, b)


def _linscan_fwd_impl(a, b):
    for d in _scan_steps(a.shape[0]):
        b = a * _shift_down(b, d, 0.0) + b
        a = a * _shift_down(a, d, 1.0)
    return b


def _linscan_fwd(a, b):
    h = _linscan_fwd_impl(a, b)
    return h, (a, h)


def _linscan_bwd(res, dh):
    a, h = res
    an = _shift_up(a, 1, 0.0)
    lam = dh
    for d in _scan_steps(a.shape[0]):
        lam = an * _shift_up(lam, d, 0.0) + lam
        an = an * _shift_up(an, d, 1.0)
    return lam * _shift_down(h, 1, 0.0), lam


linscan.defvjp(_linscan_fwd, _linscan_bwd)


@jax.custom_vjp
def linscan_known(a, b, h):
    return h


linscan_known.defvjp(lambda a, b, h: (h, (a, h)), lambda res, dh: _linscan_bwd(res, dh) + (jnp.zeros_like(dh),))


def _chunk_scan(x, reverse):
    pos = _rows(x.shape) % CHUNK
    n = x.shape[0]
    d = 1
    while d < CHUNK:
        if reverse:
            x = x + jnp.where(pos < CHUNK - d, pltpu.roll(x, n - d, 0), 0.0)
        else:
            x = x + jnp.where(pos >= d, pltpu.roll(x, d, 0), 0.0)
        d *= 2
    return x


@jax.custom_vjp
def chunk_cumsum(x):
    return _chunk_scan(x, False)


@jax.custom_vjp
def chunk_revcumsum(x):
    return _chunk_scan(x, True)


chunk_cumsum.defvjp(lambda x: (_chunk_scan(x, False), None), lambda _, g: (_chunk_scan(g, True),))
chunk_revcumsum.defvjp(lambda x: (_chunk_scan(x, True), None), lambda _, g: (_chunk_scan(g, False),))


def _bmm(a, b, precision=None):
    return jnp.einsum('nij,njk->nik', a, b, precision=precision, preferred_element_type=F32)


@jax.custom_vjp
def inv_unit_lower(l):
    return _inv_impl(l)


def _inv_impl(l):
    n = l.shape[-1]
    eye = (_rows((n, n)) == lax.broadcasted_iota(jnp.int32, (n, n), 1)).astype(F32)
    p = -l
    a = eye + p
    k = 1
    while 2 * k < n:
        p = _bmm(p, p, HI)
        a = a + _bmm(a, p, HI)
        k *= 2
    return a


def _inv_fwd(l):
    a = _inv_impl(l)
    return a, a


def _inv_bwd(a, da):
    at = jnp.swapaxes(a, 1, 2)
    return (-_bmm(_bmm(at, da, HI), at, HI),)


inv_unit_lower.defvjp(_inv_fwd, _inv_bwd)


@jax.custom_vjp
def inv_unit_lower_known(l, a):
    return a


inv_unit_lower_known.defvjp(lambda l, a: (a, a), lambda a, da: (_inv_bwd(a, da)[0], jnp.zeros_like(a)))


def neg_expm1(y):
    series = -(y * (1.0 + y * (0.5 + y * (1.0 / 6.0 + y * (1.0 / 24.0)))))
    return jnp.where(y > -0.05, series, 1.0 - jnp.exp(y))


def run_stage(name, fn, *, tok, tok_nd=(), ex=(), par=(), par_tiled=(), big=(), out_ch, ts, nc=1, cot=None, addin=None,
              which="all", dtok_dtype=None, side=None):
    tok, tok_nd, ex, par, par_tiled, big = map(list, (tok, tok_nd, ex, par, par_tiled, big))
    big_layer = [b[1] if isinstance(b, tuple) else None for b in big]
    big_arrays = [b[0] if isinstance(b, tuple) else b for b in big]
    big = [jax.ShapeDtypeStruct(a.shape if lyr is None else a.shape[1:], a.dtype) for a, lyr in zip(big_arrays, big_layer)]
    bsz, seq, _ = tok[0].shape
    ts = min(ts, seq)
    ns = seq // ts
    grid = (nc, bsz, ns)

    def tok_spec(a):
        cb = a.shape[-1] // nc
        return pl.BlockSpec((None, ts, cb), lambda c, b, s: (b, s, c))

    def ex_spec(a):
        cb = a.shape[-1] // nc
        return pl.BlockSpec((None, 1, cb), lambda c, b, s: (b, 0, c))

    def full_spec(a, single=False):
        nd = a.ndim
        kw = dict(pipeline_mode=pl.Buffered(1)) if single else {}
        return pl.BlockSpec(a.shape, lambda c, b, s: (0,) * nd, **kw)

    def tiled_spec(a):
        if a.ndim == 2:
            return pl.BlockSpec((a.shape[0], a.shape[1] // nc), lambda c, b, s: (0, c))
        return pl.BlockSpec((None,) + a.shape[1:], lambda c, b, s: (c, 0, 0))

    def big_spec(a, lyr):
        if lyr is None:
            return full_spec(a, True)
        nd = a.ndim
        return pl.BlockSpec((None,) + a.shape[1:], lambda c, b, s: (lyr,) + (0,) * (nd - 1), pipeline_mode=pl.Buffered(1))

    n_tok, n_nd, n_ex, n_par, n_pt, n_big = map(len, (tok, tok_nd, ex, par, par_tiled, big))
    in_arrays = tok + tok_nd + ex + par + par_tiled + big_arrays
    in_specs = ([tok_spec(a) for a in tok + tok_nd] + [ex_spec(a) for a in ex] + [full_spec(a) for a in par]
                + [tiled_spec(a) for a in par_tiled] + [big_spec(a, lyr) for a, lyr in zip(big_arrays, big_layer)])
    out_tok_shapes = [jax.ShapeDtypeStruct((bsz, seq, ch), F32) for ch in out_ch]
    n_in = len(in_arrays)

    def split(vals):
        i = 0
        groups = []
        for n in (n_tok, n_nd, n_ex, n_par, n_pt, n_big):
            groups.append(list(vals[i:i + n]))
            i += n
        return groups

    def split_grads(vals):
        i = 0
        groups = []
        for n in (n_tok, n_ex, n_par, n_pt, n_big):
            groups.append(list(vals[i:i + n]))
            i += n
        return groups

    side = list(side or [])
    side_arrays = [a for job in side for a in job[0]]
    n_side = len(side_arrays)
    side_shapes = [jax.ShapeDtypeStruct(a.shape, a.dtype) for a in side_arrays]
    side_scratch = [pltpu.SemaphoreType.DMA((job[1],)) for job in side for _ in range(2)]

    def side_jobs(side_refs, sems):
        o = 0
        for j, (arrs, _, issue) in enumerate(side):
            yield issue(side_refs[o:o + len(arrs)], sems[2 * j], sems[2 * j + 1])
            o += len(arrs)

    def side_start(side_refs, sems):
        if side:
            c, b, s = pl.program_id(0), pl.program_id(1), pl.program_id(2)

            @pl.when(jnp.logical_and(jnp.logical_and(c == 0, b == 0), s == 0))
            def _():
                for starts, _, _ in side_jobs(side_refs, sems):
                    for cp in starts:
                        cp.start()

    def side_finish(side_refs, sems):
        if side:
            c, b, s = pl.program_id(0), pl.program_id(1), pl.program_id(2)

            @pl.when(jnp.logical_and(jnp.logical_and(c == nc - 1, b == bsz - 1), s == ns - 1))
            def _():
                for _, recv_waits, send_waits in side_jobs(side_refs, sems):
                    for cp in recv_waits:
                        cp.wait_recv()
                    for cp in send_waits:
                        cp.wait_send()

    if cot is None:
        n_out = len(out_tok_shapes)

        def body(*refs):
            tv, ndv, ev, pv, ptv, _ = split([r[...] for r in refs[:n_in - n_big]] + [None] * n_big)
            b_refs = refs[n_in - n_big:n_in]
            side_refs = refs[n_in + n_side + n_out:n_in + 2 * n_side + n_out]
            sems = refs[n_in + 2 * n_side + n_out:]
            side_start(side_refs, sems)
            outs = fn(tv, ndv, ev, pv, ptv,
                      lambda x, i, j=None, known=None: _dot(x, b_refs[i][...] if j is None else b_refs[i][j]))
            for r, o in zip(refs[n_in + n_side:], outs):
                r[...] = o
            side_finish(side_refs, sems)

        res = pl.pallas_call(
            body, name=name, grid=grid, in_specs=in_specs + [_ANY] * n_side,
            out_specs=[tok_spec(a) for a in out_tok_shapes] + [_ANY] * n_side,
            out_shape=out_tok_shapes + side_shapes, input_output_aliases={n_in + j: n_out + j for j in range(n_side)},
            scratch_shapes=side_scratch, compiler_params=_params(3))(*in_arrays, *side_arrays)
        return (res[:n_out], res[n_out:]) if side else res

    cot = list(cot)
    has_addin = addin is not None
    extra = cot + ([addin] if has_addin else [])
    n_cot = len(cot)
    want_small, want_big = which in ("all", "small"), which in ("all", "big")
    if not want_small:
        keep = [i for i in range(n_in - n_big) if not n_tok <= i < n_tok + n_nd]
        in_arrays, in_specs, n_in = [in_arrays[i] for i in keep], [in_specs[i] for i in keep], len(keep)
    small_arrays = tok + ex + par + par_tiled
    g_shapes = [jax.ShapeDtypeStruct(a.shape, F32) for a in (small_arrays if want_small else []) + (big if want_big else [])]
    for i, dt_ in (dtok_dtype or {}).items():
        g_shapes[i] = jax.ShapeDtypeStruct(g_shapes[i].shape, dt_)
    g_specs = (([tok_spec(a) for a in tok] + [ex_spec(a) for a in ex] + [full_spec(a) for a in par]
                + [tiled_spec(a) for a in par_tiled]) if want_small else []) + (
                    [full_spec(a, True) for a in big] if want_big else [])

    def body(*refs):
        c, b, s = pl.program_id(0), pl.program_id(1), pl.program_id(2)
        if want_small:
            n_small_in = n_tok + n_nd + n_ex + n_par + n_pt
            tv, ndv, ev, pv, ptv, _ = split([r[...] for r in refs[:n_small_in]] + [None] * n_big)
            b_refs = refs[n_small_in:n_in]
        else:
            vals = [r[...] for r in refs[:n_in]]
            tv, ndv, ev, pv, ptv, _ = split(vals[:n_tok] + [None] * n_nd + vals[n_tok:] + [None] * n_big)
            b_refs = []
        cots = [r[...].astype(F32) for r in refs[n_in:n_in + n_cot]]
        n_g = len(g_shapes)
        g_refs = list(refs[n_in + len(extra) + n_side:n_in + len(extra) + n_side + n_g])
        side_refs = refs[n_in + len(extra) + n_side + n_g:n_in + len(extra) + 2 * n_side + n_g]
        sems = refs[n_in + len(extra) + 2 * n_side + n_g:]
        side_start(side_refs, sems)
        probes = [jnp.zeros(w.shape, F32) if w.ndim == 2 else [jnp.zeros(w.shape[1:], F32) for _ in range(w.shape[0])]
                  for w in big]

        def f(tv_, ev_, pv_, ptv_, probes_):
            def mm(x, i, j=None, known=None):
                probe = None if probes_ is None else (probes_[i] if j is None else probes_[i][j])
                if not want_small:
                    return _probe_only(x, probe)
                w = b_refs[i][...] if j is None else b_refs[i][j]
                if known is not None:
                    return _mm_known(x, w, known) if probe is None else _mm_known_probe(x, w, known, probe)
                return _dot(x, w) if probe is None else _mm_probe(x, w, probe)

            return fn(tv_, ndv, ev_, pv_, ptv_, mm)

        dt = de = dp = dpt = dbg = ()
        if which == "all":
            dt, de, dp, dpt, dbg = jax.vjp(f, tv, ev, pv, ptv, probes)[1](cots)
        elif which == "small":
            dt, de, dp, dpt = jax.vjp(lambda *a: f(*a, None), tv, ev, pv, ptv)[1](cots)
        else:
            (dbg,) = jax.vjp(lambda p: f(tv, ev, pv, ptv, p), probes)[1](cots)
        if has_addin:
            dt = [dt[0] + refs[n_in + n_cot][...]] + list(dt[1:])
        if want_small:
            gt_r, ge_r, gp_r, gpt_r, gb_r = split_grads(g_refs + ([] if want_big else [None] * n_big))
        else:
            gt_r, ge_r, gp_r, gpt_r, gb_r = [], [], [], [], g_refs
        for r, g in zip(gt_r, dt):
            r[...] = g.astype(r.dtype)

        def accumulate(r, g, first):
            @pl.when(first)
            def _():
                r[...] = g

            @pl.when(jnp.logical_not(first))
            def _():
                r[...] += g

        for r, g in zip(ge_r, de):
            accumulate(r, g, s == 0)
        first_all = jnp.logical_and(jnp.logical_and(c == 0, b == 0), s == 0)
        for r, g in zip(gp_r, dp):
            accumulate(r, g, first_all)
        for r, g in zip(gpt_r, dpt):
            accumulate(r, g, jnp.logical_and(b == 0, s == 0))
        for r, g in zip(gb_r, dbg):
            if isinstance(g, (list, tuple)):
                for j, gj in enumerate(g):
                    accumulate(r.at[j], gj, first_all)
            else:
                accumulate(r, g, first_all)
        side_finish(side_refs, sems)

    n_args = n_in + len(extra)
    res = pl.pallas_call(
        body, name=name + "_bwd" + ("" if which == "all" else "_" + which), grid=grid,
        in_specs=in_specs + [tok_spec(a) for a in extra] + [_ANY] * n_side, out_specs=g_specs + [_ANY] * n_side,
        out_shape=g_shapes + side_shapes, input_output_aliases={n_args + j: len(g_shapes) + j for j in range(n_side)},
        scratch_shapes=side_scratch, compiler_params=_params(3))(*in_arrays, *extra, *side_arrays)
    res, side_out = list(res[:len(g_shapes)]), list(res[len(g_shapes):])
    groups = [[], [], [], [], res] if not want_small else split_grads(res + ([] if want_big else [None] * n_big))
    return (groups, side_out) if side else groups


PROJ_PIECES = (("qkv", 768), ("z", 256), ("rx", 512), ("rgate", 512), ("mq", 256), ("mkv", 128), ("misc", 128))
PROJ_WIDTH = sum(w for _, w in PROJ_PIECES)
MISC_KR, MISC_A, MISC_B = 0, 32, 36


def fn_mixer_in(tok, nd, ex, par, pt, mm):
    (h,), (sc, sh), (g,) = tok, ex, par
    proj = mm(rms(h, g) * (1.0 + sc) + sh, 0)
    outs, o = [], 0
    for _, w in PROJ_PIECES:
        outs.append(proj[:, o:o + w])
        o += w
    return outs


def fn_gdn_conv(tok, nd, ex, par, pt, mm):
    return [jax.nn.silu(causal_conv4(tok[0], pt[0]))]


def _tri_masks():
    r = _rows((CHUNK, CHUNK))
    c = lax.broadcasted_iota(jnp.int32, (CHUNK, CHUNK), 1)
    return (c <= r), (c < r)


def fn_gdn_local(tok, nd, ex, par, pt, mm):
    (qkv, misc), (a_log, dt_bias) = tok, par
    known = nd[0] if nd else None
    ts = qkv.shape[0]
    nb = ts // CHUNK
    lower, strict = _tri_masks()
    g_all = -jnp.exp(a_log) * jax.nn.softplus(misc[:, MISC_A:MISC_A + GDN_HEADS] + dt_bias)
    g_cum = chunk_cumsum(g_all)
    g_tot = g_cum + chunk_revcumsum(g_all) - g_all
    outs = [[] for _ in range(7)]
    for hd in range(GDN_HEADS):
        def head(x, base):
            return x[:, base + 64 * hd: base + 64 * hd + 64]

        def l2n(x):
            return x * lax.rsqrt(jnp.sum(x * x, axis=-1, keepdims=True) + EPS)

        q = (l2n(head(qkv, 0)) * (64.0 ** -0.5)).reshape(nb, CHUNK, 64)
        k = l2n(head(qkv, 256)).reshape(nb, CHUNK, 64)
        v = head(qkv, 512).reshape(nb, CHUNK, 64)
        b = misc[:, MISC_B + hd: MISC_B + hd + 1]
        beta = jax.nn.sigmoid(b).reshape(nb, CHUNK, 1)
        gi = jnp.broadcast_to(g_cum[:, hd:hd + 1].reshape(nb, CHUNK, 1), (nb, CHUNK, CHUNK))
        gl = jnp.broadcast_to(g_tot[:, hd:hd + 1].reshape(nb, CHUNK, 1), (nb, CHUNK, CHUNK))
        diff = gi - jnp.swapaxes(gi, 1, 2)
        decay = jnp.where(lower, jnp.exp(jnp.where(lower, diff, 0.0)), 0.0)
        kb = k * beta
        vb = v * beta
        kk = jnp.einsum('ncd,nsd->ncs', kb.astype(BF16), k.astype(BF16), preferred_element_type=F32)
        lmat = jnp.where(strict, kk * decay, 0.0)
        if known is None:
            amat = inv_unit_lower(lmat)
        else:
            amat = inv_unit_lower_known(lmat, known[:, 64 * hd: 64 * hd + 64].reshape(nb, CHUNK, 64))
        eg = jnp.exp(gi)
        u = _bmm(amat, vb, HI)
        w = _bmm(amat, kb * eg, HI)
        qk = jnp.einsum('ncd,nsd->ncs', q.astype(BF16), k.astype(BF16), preferred_element_type=F32) * decay
        qd = q * eg
        kt = k * jnp.exp(gl - gi)
        cd = jnp.exp(gl)
        for lst, val in zip(outs, (qk, qd, u, w, kt, cd) + (() if known is not None else (amat,))):
            lst.append(val.reshape(ts, 64))
    return [jnp.concatenate(lst, axis=-1) for lst in outs if lst]


def fn_rglru(tok, nd, ex, par, pt, mm):
    (rx, rgate), (conv_w, conv_b, b_a, b_x, lam, bd_a, bd_x) = tok, pt
    xc = causal_conv4(rx, conv_w) + conv_b
    r = jax.nn.sigmoid(mmw(xc, bd_a) + b_a)
    i = jax.nn.sigmoid(mmw(xc, bd_x) + b_x)
    log_a = -RG_C * r * jax.nn.softplus(-lam)
    a = jnp.exp(log_a)
    bterm = jnp.sqrt(neg_expm1(2.0 * log_a)) * (i * xc)
    hidden = linscan_known(a, bterm, nd[0]) if nd else linscan(a, bterm)
    return [hidden * jax.nn.gelu(rgate)] + ([] if nd else [hidden])


def _rope32(x, cos, sin):
    x1, x2 = x[:, :16], x[:, 16:32]
    return jnp.concatenate([x1 * cos - x2 * sin, x2 * cos + x1 * sin], axis=-1)


MLA_QK = 96


def fn_mla_pre(tok, nd, ex, par, pt, mm):
    (mq, mkv, misc), (cs,), (g_q, g_kv, w_q, w_kv) = tok, nd, par
    q = mmw(rms(mq, g_q), w_q)
    kv = mmw(rms(mkv, g_kv), w_kv)
    cos, sin = cs[:, 0:16], cs[:, 16:32]
    kp = _rope32(misc[:, MISC_KR:MISC_KR + 32], cos, sin)
    qs, ks, vs = [], [], []
    for h in range(MLA_HEADS):
        qs += [q[:, MLA_QK * h: MLA_QK * h + 64], _rope32(q[:, MLA_QK * h + 64: MLA_QK * h + 96], cos, sin)]
        ks += [kv[:, 128 * h: 128 * h + 64], kp]
        vs.append(kv[:, 128 * h + 64: 128 * h + 128])
    return [jnp.concatenate(qs, axis=-1), jnp.concatenate(ks, axis=-1), jnp.concatenate(vs, axis=-1)]


def fn_out_proj(tok, nd, ex, par, pt, mm):
    (h, o_a, o_b, o_c), (gt,) = tok, ex
    mix = mm(jnp.concatenate([o_a, o_b, o_c], axis=-1), 0, known=nd[0] if nd else None)
    return [h + gt * mix] + ([] if nd else [mix])


def fn_mlp_in(tok, nd, ex, par, pt, mm):
    (h,), (sc, sh), (g,) = tok, ex, par
    u = rms(h, g) * (1.0 + sc) + sh
    return [jnp.concatenate([mm(u, 0, j) for j in range(N_CHIP)], axis=-1)]


def fn_mlp_out(tok, nd, ex, par, pt, mm):
    (h, a), (gt,) = tok, ex
    f = mm(jnp.square(jax.nn.relu(a)), 0, known=nd[0] if nd else None)
    return [h + gt * f] + ([] if nd else [f])


GDN_W = GDN_HEADS * 64


def _head_mask():
    r = _rows((GDN_W, GDN_W)) // 64
    c = lax.broadcasted_iota(jnp.int32, (GDN_W, GDN_W), 1) // 64
    return r == c


def _heads_diag(x):
    return jnp.where(_head_mask(), jnp.concatenate([x] * GDN_HEADS, axis=0), 0.0)


def _heads_compact(s):
    return s[0:64] + s[64:128] + s[128:192] + s[192:256]


def _gdn_step(state, qk, qd, u, w, kt, cd, z, norm_g):
    v_new = u - _dot(w, state)
    o = _dot(qd, state) + _dot(qk, _heads_diag(v_new))
    update = _dot(kt, v_new, (((0,), (0,)), ((), ())))
    new_state = state * jnp.broadcast_to(cd[0:1, :], (GDN_W, GDN_W)) + jnp.where(_head_mask(), update, 0.0)
    outs = [rms(o[:, 64 * hd: 64 * hd + 64], norm_g) * jax.nn.silu(z[:, 64 * hd: 64 * hd + 64]) for hd in range(GDN_HEADS)]
    return new_state, jnp.concatenate(outs, axis=-1)


def gdn_scan(xs, z, norm_g):
    bsz, seq, _ = z.shape
    n = seq // CHUNK
    blk = pl.BlockSpec((bsz, CHUNK, 256), lambda i: (0, i, 0))

    def body(qk, qd, u, w, kt, cd, z_ref, g_ref, o_ref, st_out, st):
        @pl.when(pl.program_id(0) == 0)
        def _():
            st[...] = jnp.zeros_like(st)

        for b in range(bsz):
            state = st[b]
            st_out[b] = _heads_compact(state)
            st[b], o_ref[b] = _gdn_step(state, qk[b], qd[b], u[b], w[b], kt[b], cd[b], z_ref[b], g_ref[...])

    return pl.pallas_call(
        body, name="gdn_scan", grid=(n,), in_specs=[blk] * 7 + [pl.BlockSpec((1, 64), lambda i: (0, 0))],
        out_specs=[blk, blk], out_shape=[jax.ShapeDtypeStruct((bsz, seq, 256), F32)] * 2,
        scratch_shapes=[pltpu.VMEM((bsz, GDN_W, GDN_W), F32)], compiler_params=_params(1))(*xs, z, norm_g)


def gdn_scan_bwd(xs, z, norm_g, st_in, do):
    bsz, seq, _ = z.shape
    n = seq // CHUNK
    blk = pl.BlockSpec((bsz, CHUNK, 256), lambda i: (0, n - 1 - i, 0))
    gspec = pl.BlockSpec((1, 64), lambda i: (0, 0))

    def body(qk, qd, u, w, kt, cd, z_ref, g_ref, st_ref, do_ref, dqk, dqd, du, dw, dkt, dcd, dz, dg, dst):
        first = pl.program_id(0) == 0

        @pl.when(first)
        def _():
            dst[...] = jnp.zeros_like(dst)

        dg_sum = None
        for b in range(bsz):
            _, vjp = jax.vjp(_gdn_step, _heads_diag(st_ref[b]), qk[b], qd[b], u[b], w[b], kt[b], cd[b], z_ref[b], g_ref[...])
            grads = vjp((dst[b], do_ref[b]))
            dst[b] = jnp.where(_head_mask(), grads[0], 0.0)
            for r, g in zip((dqk, dqd, du, dw, dkt, dcd, dz), grads[1:8]):
                r[b] = g
            dg_sum = grads[8] if dg_sum is None else dg_sum + grads[8]

        @pl.when(first)
        def _():
            dg[...] = dg_sum

        @pl.when(jnp.logical_not(first))
        def _():
            dg[...] += dg_sum

    res = pl.pallas_call(
        body, name="gdn_scan_bwd", grid=(n,), in_specs=[blk] * 7 + [gspec, blk, blk],
        out_specs=[blk] * 7 + [gspec], out_shape=[jax.ShapeDtypeStruct((bsz, seq, 256), F32)] * 7
        + [jax.ShapeDtypeStruct((1, 64), F32)],
        scratch_shapes=[pltpu.VMEM((bsz, GDN_W, GDN_W), F32)], compiler_params=_params(1))(*xs, z, norm_g, st_in, do)
    return list(res[:6]), res[6], res[7]


ATTN_TQ = 256
ATTN_SCALE = 96.0 ** -0.5
ATTN_KEY_QUARTERS_FWD, ATTN_KEY_QUARTERS_BWD = (1, 2, 3, 4), (1, 2, 4)


def _attn_head(q, k, v, q0):
    s = _dot(q, k, (((1,), (1,)), ((), ()))) * ATTN_SCALE
    qc = (q0 + _rows(s.shape)) // CHUNK
    kc = lax.broadcasted_iota(jnp.int32, s.shape, 1) // CHUNK
    s = jnp.where(kc <= qc, s, -1e30)
    p = jnp.exp(s - jnp.max(s, axis=-1, keepdims=True))
    p = p / jnp.sum(p, axis=-1, keepdims=True)
    return _dot(p, v)


def _key_lengths(seq, quarters):
    return sorted({max(ATTN_TQ, seq * q // 4 // ATTN_TQ * ATTN_TQ) for q in quarters})


def _key_variant(i, seq, quarters):
    need = (i + 1) * ATTN_TQ
    return sum(((need > klen).astype(jnp.int32) for klen in _key_lengths(seq, quarters)[:-1]), jnp.int32(0))


ATTN_QW, ATTN_VW = MLA_HEADS * MLA_QK, MLA_HEADS * 64


def _attn_specs(seq):
    def qspec(ch):
        return pl.BlockSpec((None, ATTN_TQ, ch), lambda b, i: (b, i, 0))

    def kspec(ch):
        return pl.BlockSpec((None, seq, ch), lambda b, i: (b, 0, 0))

    return qspec, kspec


def mla_attention(q, k, v):
    bsz, seq, _ = q.shape
    qspec, kspec = _attn_specs(seq)

    def body(q_r, k_r, v_r, o_r):
        i = pl.program_id(1)
        q0 = i * ATTN_TQ

        def with_keys(klen):
            outs = [_attn_head(q_r[:, MLA_QK * h: MLA_QK * h + MLA_QK], k_r[0:klen, MLA_QK * h: MLA_QK * h + MLA_QK],
                               v_r[0:klen, 64 * h: 64 * h + 64], q0) for h in range(MLA_HEADS)]
            o_r[...] = jnp.concatenate(outs, axis=-1)

        for j, klen in enumerate(_key_lengths(seq, ATTN_KEY_QUARTERS_FWD)):
            pl.when(_key_variant(i, seq, ATTN_KEY_QUARTERS_FWD) == j)(functools.partial(with_keys, klen))

    return pl.pallas_call(
        body, name="mla_attention", grid=(bsz, seq // ATTN_TQ), in_specs=[qspec(ATTN_QW), kspec(ATTN_QW), kspec(ATTN_VW)],
        out_specs=qspec(ATTN_VW), out_shape=jax.ShapeDtypeStruct((bsz, seq, ATTN_VW), F32), compiler_params=_params(2))(
            q, k, v)


def mla_attention_bwd(q, k, v, do):
    bsz, seq, _ = q.shape
    qspec, kspec = _attn_specs(seq)

    def body(q_r, k_r, v_r, do_r, dq_r, dk_r, dv_r):
        i = pl.program_id(1)
        q0 = i * ATTN_TQ

        @pl.when(i == 0)
        def _():
            dk_r[...] = jnp.zeros_like(dk_r)
            dv_r[...] = jnp.zeros_like(dv_r)

        def with_keys(klen):
            dq, dk, dv = [], [], []
            for h in range(MLA_HEADS):
                qk = slice(MLA_QK * h, MLA_QK * h + MLA_QK)
                sl = slice(64 * h, 64 * h + 64)
                _, vjp = jax.vjp(functools.partial(_attn_head, q0=q0), q_r[:, qk], k_r[0:klen, qk], v_r[0:klen, sl])
                a, b, c = vjp(do_r[:, sl])
                dq.append(a)
                dk.append(b)
                dv.append(c)
            dq_r[...] = jnp.concatenate(dq, axis=-1)
            dk_r[0:klen, :] += jnp.concatenate(dk, axis=-1)
            dv_r[0:klen, :] += jnp.concatenate(dv, axis=-1)

        for j, klen in enumerate(_key_lengths(seq, ATTN_KEY_QUARTERS_BWD)):
            pl.when(_key_variant(i, seq, ATTN_KEY_QUARTERS_BWD) == j)(functools.partial(with_keys, klen))

    shp = lambda ch: jax.ShapeDtypeStruct((bsz, seq, ch), F32)
    return pl.pallas_call(
        body, name="mla_attention_bwd", grid=(bsz, seq // ATTN_TQ),
        in_specs=[qspec(ATTN_QW), kspec(ATTN_QW), kspec(ATTN_VW), qspec(ATTN_VW)],
        out_specs=[qspec(ATTN_QW), kspec(ATTN_QW), kspec(ATTN_VW)],
        out_shape=[shp(ATTN_QW), shp(ATTN_QW), shp(ATTN_VW)], compiler_params=_params(2))(q, k, v, do)


LOSS_TS = 512


def loss_head(h, g, target):
    bsz, seq, d = h.shape
    ts = min(LOSS_TS, seq)
    tok = pl.BlockSpec((None, ts, d), lambda b, s: (b, s, 0))
    gspec = pl.BlockSpec((1, d), lambda b, s: (0, 0))
    lspec = pl.BlockSpec((1, 128), lambda b, s: (0, 0))

    def body(h_r, g_r, t_r, loss_r, dh_r, dg_r):
        first = jnp.logical_and(pl.program_id(0) == 0, pl.program_id(1) == 0)
        tv = t_r[...]

        def f(hv, gv):
            return 0.5 * jnp.sum(jnp.mean(jnp.square(rms(hv, gv) - tv), axis=-1, keepdims=True), axis=0, keepdims=True)

        val, vjp = jax.vjp(f, h_r[...], g_r[...])
        dh, dg = vjp(jnp.ones((1, 1), F32))
        dh_r[...] = dh
        lv = jnp.broadcast_to(val, (1, 128))

        @pl.when(first)
        def _():
            loss_r[...] = lv
            dg_r[...] = dg

        @pl.when(jnp.logical_not(first))
        def _():
            loss_r[...] += lv
            dg_r[...] += dg

    return pl.pallas_call(
        body, name="loss_head", grid=(bsz, seq // ts), in_specs=[tok, gspec, tok], out_specs=[lspec, tok, gspec],
        out_shape=[jax.ShapeDtypeStruct((1, 128), F32), jax.ShapeDtypeStruct(h.shape, F32), jax.ShapeDtypeStruct((1, d), F32)],
        compiler_params=_params(2))(h, g, target)


def _adamw_math(w, g, m, v):
    m = ADAM_B1 * m + (1.0 - ADAM_B1) * g
    v = ADAM_B2 * v + (1.0 - ADAM_B2) * jnp.square(g)
    m_hat = m / (1.0 - ADAM_B1 ** ADAM_STEP)
    v_hat = v / (1.0 - ADAM_B2 ** ADAM_STEP)
    return -ADAM_LR * (m_hat / (jnp.sqrt(v_hat) + ADAM_EPS) + ADAM_WD * w), m, v


def _row_block(rows, cols):
    want = max(8, (1 << 18) // cols)
    best = rows
    for r in range(8, rows + 1, 8):
        if rows % r == 0 and r <= want:
            best = r
    return best if rows % 8 == 0 else rows


def adamw(name, w, g, m, v):
    rows, cols = w.shape
    rb = _row_block(rows, cols)
    spec = pl.BlockSpec((rb, cols), lambda i: (i, 0))

    def body(w_r, g_r, m_r, v_r, d_o, m_o, v_o):
        d, mn, vn = _adamw_math(w_r[...], g_r[...], m_r[...], v_r[...])
        d_o[...] = d
        m_o[...] = mn
        v_o[...] = vn

    return pl.pallas_call(body, name=name, grid=(rows // rb,), in_specs=[spec] * 4, out_specs=[spec] * 3,
                          out_shape=[jax.ShapeDtypeStruct(w.shape, F32)] * 3, compiler_params=_params(1))(w, g, m, v)


def adamw_reduce(name, parts, w, m, v):
    rows, cols = w.shape
    rb = _row_block(rows, cols)
    spec = pl.BlockSpec((rb, cols), lambda i: (i, 0))
    pspec = pl.BlockSpec((N_DEV, rb, cols), lambda i: (0, i, 0))

    def body(p_r, w_r, m_r, v_r, g_o, d_o, m_o, v_o):
        g = p_r[0]
        for k in range(1, N_DEV):
            g = g + p_r[k]
        d, mn, vn = _adamw_math(w_r[...], g, m_r[...], v_r[...])
        g_o[...] = g
        d_o[...] = d
        m_o[...] = mn
        v_o[...] = vn

    return pl.pallas_call(body, name=name, grid=(rows // rb,), in_specs=[pspec, spec, spec, spec], out_specs=[spec] * 4,
                          out_shape=[jax.ShapeDtypeStruct(w.shape, F32)] * 4, compiler_params=_params(1))(parts, w, m, v)


MOD_CB = 512


def mod_matmul(c_rows, w_mod, b_mod):
    nl, d, cols = w_mod.shape

    def body(c_r, w_r, b_r, o_r):
        o_r[...] = _dot(jax.nn.silu(c_r[...]), w_r[...]) + b_r[...]

    return pl.pallas_call(
        body, name="mod_matmul", grid=(nl, cols // MOD_CB),
        in_specs=[pl.BlockSpec((8, d), lambda l, j: (0, 0)), pl.BlockSpec((None, d, MOD_CB), lambda l, j: (l, 0, j)),
                  pl.BlockSpec((None, 1, MOD_CB), lambda l, j: (l, 0, j))],
        out_specs=pl.BlockSpec((None, 8, MOD_CB), lambda l, j: (l, 0, j)),
        out_shape=jax.ShapeDtypeStruct((nl, 8, cols), F32), compiler_params=_params(2))(c_rows, w_mod, b_mod)


def mod_weight_grad(c_all, dmod):
    nl, nb, cols = dmod.shape
    d = c_all.shape[1]

    def body(c_r, g_r, o_r):
        o_r[...] = _dot(jax.nn.silu(c_r[...]), g_r[...], (((0,), (0,)), ((), ())))

    return pl.pallas_call(
        body, name="mod_weight_grad", grid=(nl, cols // MOD_CB),
        in_specs=[pl.BlockSpec((nb, d), lambda l, j: (0, 0)), pl.BlockSpec((None, nb, MOD_CB), lambda l, j: (l, 0, j))],
        out_specs=pl.BlockSpec((None, d, MOD_CB), lambda l, j: (l, 0, j)),
        out_shape=jax.ShapeDtypeStruct((nl, d, cols), F32), compiler_params=_params(2))(c_all, dmod)


def _half_block(hr, cols):
    rb = _row_block(hr, cols)
    return rb if rb % 16 == 0 else hr


def add_half(name, g, s, core):
    _, r, cols = g.shape
    hr = r // 2
    rb = _half_block(hr, cols)
    nblk = hr // rb
    gspec = pl.BlockSpec((None, rb, cols), lambda k, i, c: (k, c[0] * nblk + i, 0))
    spec = pl.BlockSpec((None, rb, cols), lambda k, i, c: (k, i, 0))

    def body(c_r, g_r, s_r, o_r, ob_r):
        t = g_r[...] + s_r[...]
        o_r[...] = t
        ob_r[...] = t.astype(BF16)

    return pl.pallas_call(
        body, name=name, grid_spec=pltpu.PrefetchScalarGridSpec(num_scalar_prefetch=1, grid=(N_CHIP, nblk),
                                                                in_specs=[gspec, spec], out_specs=[spec, spec]),
        out_shape=[jax.ShapeDtypeStruct((N_CHIP, hr, cols), F32), jax.ShapeDtypeStruct((N_CHIP, hr, cols), BF16)],
        compiler_params=_params(2))(core, g, s)


def sum_peers(name, p32, recv, ids, shard_shape, layer, acc=None):
    _, hr, cols = p32.shape
    rb = _half_block(hr, cols)
    nblk = hr // rb

    def slot(k):
        return pl.BlockSpec((None, rb, cols), lambda i, c: ((c[0] + k) % N_CHIP, i, 0))

    def body(c_r, o_r, r1, r2, r3, *rest):
        rest[-1][...] = ((o_r[...] + r1[...].astype(F32)) + r2[...].astype(F32)) + r3[...].astype(F32)

    args = (ids, p32, recv, recv, recv) + (() if acc is None else (acc,))
    return pl.pallas_call(
        body, name=name, grid_spec=pltpu.PrefetchScalarGridSpec(
            num_scalar_prefetch=1, grid=(nblk,),
            in_specs=[slot(0), slot(1), slot(2), slot(3)] + ([] if acc is None else [_ANY]),
            out_specs=pl.BlockSpec((None, rb, cols), lambda i, c: (layer, c[1] * nblk + i, 0))),
        out_shape=jax.ShapeDtypeStruct(shard_shape, F32), input_output_aliases={} if acc is None else {5: 0},
        compiler_params=_params(1))(*args)


def cast_into_slab(name, w, ids, layer=None):
    nl, r, cols = w.shape
    hr = r // 2
    rb = _half_block(hr, cols)
    nblk = hr // rb

    def body(c_r, w_r, o_r):
        o_r[...] = w_r[...].astype(BF16)

    if layer is not None:
        return pl.pallas_call(
            body, name=name, grid_spec=pltpu.PrefetchScalarGridSpec(
                num_scalar_prefetch=1, grid=(nblk,),
                in_specs=[pl.BlockSpec((None, rb, cols), lambda i, c: (layer, c[1] * nblk + i, 0))],
                out_specs=pl.BlockSpec((None, rb, cols), lambda i, c: (c[0], c[1] * nblk + i, 0))),
            out_shape=jax.ShapeDtypeStruct((N_CHIP, r, cols), BF16), compiler_params=_params(1))(ids, w)
    return pl.pallas_call(
        body, name=name, grid_spec=pltpu.PrefetchScalarGridSpec(
            num_scalar_prefetch=1, grid=(nl, nblk),
            in_specs=[pl.BlockSpec((None, rb, cols), lambda l, i, c: (l, c[1] * nblk + i, 0))],
            out_specs=pl.BlockSpec((None, None, rb, cols), lambda l, i, c: (l, c[0], c[1] * nblk + i, 0))),
        out_shape=jax.ShapeDtypeStruct((nl, N_CHIP, r, cols), BF16), compiler_params=_params(2))(ids, w)


def _me():
    return lax.axis_index("x"), lax.axis_index("y"), lax.axis_index("c")


def all_gather8(name, x_shard, in_vmem):
    m_per, n = x_shard.shape
    space = pltpu.VMEM if in_vmem else pl.ANY

    def body(x_ref, out_ref, send_sems, recv_sems, local_sem):
        x, y, c = _me()
        me, sibling = (x, y, c), (x, y, 1 - c)
        chips = [(1 - x, y), (x, 1 - y), (1 - x, 1 - y)]

        def rows(px, py, pc):
            return out_ref.at[pl.ds((4 * px + 2 * py + pc) * m_per, m_per), :]

        def copy(k, block, to, src=None):
            return pltpu.make_async_remote_copy(
                src_ref=rows(*block) if src is None else src, dst_ref=rows(*block), send_sem=send_sems.at[k],
                recv_sem=recv_sems.at[k], device_id=to, device_id_type=MESH)

        mine = pltpu.make_async_copy(x_ref, rows(*me), local_sem)
        mine.start()
        first = [copy(0, me, sibling, src=x_ref)]
        first += [copy(1 + j, me, (*chip, c), src=x_ref) for j, chip in enumerate(chips)]
        for cp in first:
            cp.start()
        passed = [copy(4 + j, (*chip, c), sibling) for j, chip in enumerate(chips)]
        for j, chip in enumerate(chips):
            copy(1 + j, (*chip, c), me).wait_recv()
            passed[j].start()
        copy(0, sibling, me).wait_recv()
        for j, chip in enumerate(chips):
            copy(4 + j, (*chip, 1 - c), me).wait_recv()
        for cp in first + passed:
            cp.wait_send()
        mine.wait()

    return pl.pallas_call(
        body, name=name, out_shape=jax.ShapeDtypeStruct((N_DEV * m_per, n), x_shard.dtype),
        in_specs=[pl.BlockSpec(memory_space=space)], out_specs=pl.BlockSpec(memory_space=space),
        scratch_shapes=[pltpu.SemaphoreType.DMA((7,)), pltpu.SemaphoreType.DMA((7,)), pltpu.SemaphoreType.DMA],
    )(x_shard)


_ANY = pl.BlockSpec(memory_space=pl.ANY)


def all_gather_weights(name, slabs):
    n = len(slabs)

    def body(*refs):
        outs = refs[n:2 * n]
        send_sems, recv_sems = refs[2 * n:]
        x, y, c = _me()
        me, sibling = (x, y, c), (x, y, 1 - c)
        chips = [(1 - x, y), (x, 1 - y), (1 - x, 1 - y)]

        def view(i, px, py, pc):
            hr = slabs[i].shape[2] // 2
            return outs[i].at[:, 2 * px + py, pl.ds(pc * hr, hr), :]

        def copy(i, k, block, to):
            return pltpu.make_async_remote_copy(
                src_ref=view(i, *block), dst_ref=view(i, *block), send_sem=send_sems.at[i, k],
                recv_sem=recv_sems.at[i, k], device_id=to, device_id_type=MESH)

        first = []
        for i in range(n):
            first.append(copy(i, 0, me, sibling))
            first += [copy(i, 1 + j, me, (*chip, c)) for j, chip in enumerate(chips)]
        for cp in first:
            cp.start()
        passed = []
        for j, chip in enumerate(chips):
            for i in range(n):
                copy(i, 1 + j, (*chip, c), me).wait_recv()
                passed.append(copy(i, 4 + j, (*chip, c), sibling))
                passed[-1].start()
        for i in range(n):
            copy(i, 0, sibling, me).wait_recv()
            for j, chip in enumerate(chips):
                copy(i, 4 + j, (*chip, 1 - c), me).wait_recv()
        for cp in first + passed:
            cp.wait_send()

    return pl.pallas_call(
        body, name=name, out_shape=[jax.ShapeDtypeStruct(s.shape, s.dtype) for s in slabs],
        in_specs=[_ANY] * n, out_specs=[_ANY] * n, input_output_aliases={i: i for i in range(n)},
        scratch_shapes=[pltpu.SemaphoreType.DMA((n, 7)), pltpu.SemaphoreType.DMA((n, 7))],
    )(*slabs)


def _slab_block(slab, px, py, pc):
    hr = slab.shape[1] // 2
    return slab.at[2 * px + py, pl.ds(pc * hr, hr), :]


def gather_over_ici(slab):
    def issue(refs, send_sems, recv_sems):
        (buf,) = refs
        x, y, c = _me()
        peers = [(x, y, 1 - c), (1 - x, y, c), (x, 1 - y, c), (1 - x, 1 - y, c)]

        def copy(k, block, to):
            return pltpu.make_async_remote_copy(
                src_ref=_slab_block(buf, *block), dst_ref=_slab_block(buf, *block), send_sem=send_sems.at[k],
                recv_sem=recv_sems.at[k], device_id=to, device_id_type=MESH)

        sends = [copy(k, (x, y, c), p) for k, p in enumerate(peers)]
        arrivals = [copy(k, p, (x, y, c)) for k, p in enumerate(peers)]
        return sends, arrivals, sends

    return ([slab], 4, issue)


def gather_over_d2d(slab):
    def issue(refs, send_sems, recv_sems):
        (buf,) = refs
        x, y, c = _me()
        chips = [(1 - x, y), (x, 1 - y), (1 - x, 1 - y)]

        def copy(k, block):
            return pltpu.make_async_remote_copy(
                src_ref=_slab_block(buf, *block), dst_ref=_slab_block(buf, *block), send_sem=send_sems.at[k],
                recv_sem=recv_sems.at[k], device_id=(x, y, 1 - c), device_id_type=MESH)

        sends = [copy(k, (*chip, c)) for k, chip in enumerate(chips)]
        arrivals = [copy(k, (*chip, 1 - c)) for k, chip in enumerate(chips)]
        return sends, arrivals, sends

    return ([slab], 3, issue)


def chip_exchange_job(ps):
    n = len(ps)

    def issue(refs, send_sems, recv_sems):
        ins, outs = refs[:n], refs[n:]
        mx, my, mc = _me()
        ci = 2 * mx + my
        chips = [(1 - mx, my), (mx, 1 - my), (1 - mx, 1 - my)]
        sends, arrivals = [], []
        for i in range(n):
            for k, (px, py) in enumerate(chips):
                sem = 3 * i + k
                sends.append(pltpu.make_async_remote_copy(
                    src_ref=ins[i].at[2 * px + py], dst_ref=outs[i].at[ci], send_sem=send_sems.at[sem],
                    recv_sem=recv_sems.at[sem], device_id=(px, py, mc), device_id_type=MESH))
                arrivals.append(pltpu.make_async_remote_copy(
                    src_ref=ins[i].at[ci], dst_ref=outs[i].at[2 * px + py], send_sem=send_sems.at[sem],
                    recv_sem=recv_sems.at[sem], device_id=(px, py, mc), device_id_type=MESH))
        return sends, arrivals, sends

    return (list(ps) + [lax.empty(p.shape, p.dtype) for p in ps], 3 * n, issue)


def grad_sibling_exchange(name, gs):
    n = len(gs)

    def body(*refs):
        ins, outs = refs[:n], refs[n:2 * n]
        send_sems, recv_sems = refs[2 * n:]
        mx, my, mc = _me()
        cps = []
        for i in range(n):
            hr = gs[i].shape[1] // 2
            cps.append(pltpu.make_async_remote_copy(
                src_ref=ins[i].at[:, pl.ds((1 - mc) * hr, hr), :], dst_ref=outs[i], send_sem=send_sems.at[i],
                recv_sem=recv_sems.at[i], device_id=(mx, my, 1 - mc), device_id_type=MESH))
            cps[-1].start()
        for cp in cps:
            cp.wait()

    return pl.pallas_call(
        body, name=name, out_shape=[jax.ShapeDtypeStruct((N_CHIP, g.shape[1] // 2, g.shape[2]), g.dtype) for g in gs],
        in_specs=[_ANY] * n, out_specs=[_ANY] * n,
        scratch_shapes=[pltpu.SemaphoreType.DMA((n,)), pltpu.SemaphoreType.DMA((n,))],
    )(*gs)


def grad_chip_exchange(name, ps):
    n = len(ps)

    def body(*refs):
        ins, outs = refs[:n], refs[n:2 * n]
        send_sems, recv_sems = refs[2 * n:]
        mx, my, mc = _me()
        ci = 2 * mx + my
        chips = [(1 - mx, my), (mx, 1 - my), (1 - mx, 1 - my)]
        sends = []
        for i in range(n):
            for k, (px, py) in enumerate(chips):
                sends.append(pltpu.make_async_remote_copy(
                    src_ref=ins[i].at[2 * px + py], dst_ref=outs[i].at[ci], send_sem=send_sems.at[i, k],
                    recv_sem=recv_sems.at[i, k], device_id=(px, py, mc), device_id_type=MESH))
                sends[-1].start()
        for i in range(n):
            for k, (px, py) in enumerate(chips):
                pltpu.make_async_remote_copy(
                    src_ref=ins[i].at[ci], dst_ref=outs[i].at[2 * px + py], send_sem=send_sems.at[i, k],
                    recv_sem=recv_sems.at[i, k], device_id=(px, py, mc), device_id_type=MESH).wait_recv()
        for cp in sends:
            cp.wait_send()

    return pl.pallas_call(
        body, name=name, out_shape=[jax.ShapeDtypeStruct(p.shape, p.dtype) for p in ps], in_specs=[_ANY] * n,
        out_specs=[_ANY] * n, scratch_shapes=[pltpu.SemaphoreType.DMA((n, 3)), pltpu.SemaphoreType.DMA((n, 3))],
    )(*ps)


def grad_half_exchange(name, shards):
    n = len(shards)

    def body(*refs):
        outs = refs[n:2 * n]
        send_sems, recv_sems = refs[2 * n:]
        mx, my, mc = _me()

        def copy(i, core):
            hr = shards[i].shape[1] // 2
            rows = outs[i].at[:, pl.ds(core * hr, hr), :]
            return pltpu.make_async_remote_copy(src_ref=rows, dst_ref=rows, send_sem=send_sems.at[i],
                                                recv_sem=recv_sems.at[i], device_id=(mx, my, 1 - mc), device_id_type=MESH)

        sends = [copy(i, mc) for i in range(n)]
        for cp in sends:
            cp.start()
        for i in range(n):
            copy(i, 1 - mc).wait_recv()
        for cp in sends:
            cp.wait_send()

    return pl.pallas_call(
        body, name=name, out_shape=[jax.ShapeDtypeStruct(s.shape, s.dtype) for s in shards], in_specs=[_ANY] * n,
        out_specs=[_ANY] * n, input_output_aliases={i: i for i in range(n)},
        scratch_shapes=[pltpu.SemaphoreType.DMA((n,)), pltpu.SemaphoreType.DMA((n,))],
    )(*shards)


WEIGHTS = ['w_mod', 'b_mod', 'norm_mix_g', 'w_in', 'gdn_conv_w', 'gdn_a_log', 'gdn_dt_bias', 'gdn_norm_g', 'rg_conv_w',
           'rg_conv_b', 'rg_w_a', 'rg_b_a', 'rg_w_x', 'rg_b_x', 'rg_lambda', 'mla_q_norm_g', 'mla_w_qb', 'mla_kv_norm_g',
           'mla_w_kvb', 'w_out', 'norm_mlp_g', 'w_mlp_in', 'w_mlp_out', 'final_norm_g']
SHARDED = {'w_in': 2, 'gdn_conv_w': 2, 'rg_conv_w': 2, 'mla_w_qb': 2, 'mla_w_kvb': 2, 'w_out': 1, 'w_mlp_in': 2, 'w_mlp_out': 1}
GATHER_BF16 = ('w_in', 'mla_w_qb', 'mla_w_kvb', 'w_out', 'w_mlp_in', 'w_mlp_out')
GATHER_FIRST = GATHER_BF16[:4]
REPLICATED = [n for n in WEIGHTS if n not in SHARDED and n != 'w_mod']
PACK_COLS = 1024


def _pack(arrays, multiple):
    flat = jnp.concatenate([a.reshape(-1) for a in arrays])
    pad = (-flat.shape[0]) % multiple
    return jnp.pad(flat, (0, pad)) if pad else flat


def _unpack(flat, shapes):
    out, o = [], 0
    for shp in shapes:
        n = int(np.prod(shp))
        out.append(flat[o:o + n].reshape(shp))
        o += n
    return out


def _pack_rows(arrays):
    rows = []
    for a in arrays:
        flat = a.reshape(-1)
        pad = (-flat.shape[0]) % PACK_COLS
        rows.append((jnp.pad(flat, (0, pad)) if pad else flat).reshape(-1, PACK_COLS))
    out = jnp.concatenate(rows, axis=0)
    pad = (-out.shape[0]) % 8
    return jnp.pad(out, ((0, pad), (0, 0))) if pad else out


def _unpack_rows(packed, shapes):
    out, r = [], 0
    for shp in shapes:
        n = int(np.prod(shp))
        nr = -(-n // PACK_COLS)
        piece = packed[r:r + nr]
        out.append((piece if n == nr * PACK_COLS else piece.reshape(-1)[:n]).reshape(shp))
        r += nr
    return out


def _unshard(stacked, axis):
    moved = jnp.moveaxis(stacked, 0, axis)
    shp = list(moved.shape)
    shp[axis:axis + 2] = [shp[axis] * shp[axis + 1]]
    return moved.reshape(shp)


_PROJ_SEGMENTS = ((0, 1024, 0), (1032, 2472, 1024), (1024, 1032, 2464))


def _proj_cols_from_slabs(slabs):
    w = slabs.shape[-1]
    pieces = []
    for o0, o1, _ in sorted(_PROJ_SEGMENTS, key=lambda s: s[2]):
        for k in range(N_CHIP):
            a, b = max(o0, k * w), min(o1, (k + 1) * w)
            if a < b:
                pieces.append(slabs[:, k, :, a - k * w: b - k * w])
    used = sum(p.shape[-1] for p in pieces)
    pieces.append(jnp.zeros(slabs.shape[:1] + slabs.shape[2:3] + (PROJ_WIDTH - used,), slabs.dtype))
    return jnp.concatenate(pieces, axis=-1)


def _proj_slabs(d, w):
    slabs = []
    for k in range(N_CHIP):
        pieces = []
        for o0, o1, c0 in sorted(_PROJ_SEGMENTS):
            a, b = max(o0, k * w), min(o1, (k + 1) * w)
            if a < b:
                pieces.append(d[:, c0 + a - o0: c0 + b - o0])
        slabs.append(jnp.concatenate(pieces, axis=-1))
    return jnp.stack(slabs)


def _block_diag(w):
    nl = w.shape[0]
    eye = jnp.eye(2, dtype=w.dtype)
    return jnp.einsum('lcoij,op->lcoipj', w.reshape(nl, 4, 2, 64, 64), eye).reshape(nl, 4, 128, 128)


def _block_diag_back(g):
    nl = g.shape[0]
    return jnp.einsum('lcoipj,op->lcoij', g.reshape(nl, 4, 2, 64, 2, 64), jnp.eye(2, dtype=g.dtype)).reshape(nl, 8, 64, 64)


def kernel(x, c, positions, w_mod, b_mod, norm_mix_g, w_in, gdn_conv_w, gdn_a_log, gdn_dt_bias, gdn_norm_g, rg_conv_w, rg_conv_b, rg_w_a, rg_b_a, rg_w_x, rg_b_x, rg_lambda, mla_q_norm_g, mla_w_qb, mla_kv_norm_g, mla_w_kvb, w_out, norm_mlp_g, w_mlp_in, w_mlp_out, final_norm_g, loss_target, m_w_mod, m_b_mod, m_norm_mix_g, m_w_in, m_gdn_conv_w, m_gdn_a_log, m_gdn_dt_bias, m_gdn_norm_g, m_rg_conv_w, m_rg_conv_b, m_rg_w_a, m_rg_b_a, m_rg_w_x, m_rg_b_x, m_rg_lambda, m_mla_q_norm_g, m_mla_w_qb, m_mla_kv_norm_g, m_mla_w_kvb, m_w_out, m_norm_mlp_g, m_w_mlp_in, m_w_mlp_out, m_final_norm_g, v_w_mod, v_b_mod, v_norm_mix_g, v_w_in, v_gdn_conv_w, v_gdn_a_log, v_gdn_dt_bias, v_gdn_norm_g, v_rg_conv_w, v_rg_conv_b, v_rg_w_a, v_rg_b_a, v_rg_w_x, v_rg_b_x, v_rg_lambda, v_mla_q_norm_g, v_mla_w_qb, v_mla_kv_norm_g, v_mla_w_kvb, v_w_out, v_norm_mlp_g, v_w_mlp_in, v_w_mlp_out, v_final_norm_g):
    given = dict(locals())
    wts = {n: given[n] for n in WEIGHTS}
    mom_m = {n: given["m_" + n] for n in WEIGHTS}
    mom_v = {n: given["v_" + n] for n in WEIGHTS}
    bsz, seq, d = x.shape
    depth = w_mod.shape[0]
    mx, my, mc = lax.axis_index("x"), lax.axis_index("y"), lax.axis_index("c")
    chip = 2 * mx + my
    dev = 2 * chip + mc

    conv_shapes = [wts['gdn_conv_w'].shape, wts['rg_conv_w'].shape]
    conv_flat = _pack([wts['gdn_conv_w'], wts['rg_conv_w']], d)
    conv_rows = conv_flat.shape[0] // d
    assert bsz + conv_rows <= 8
    c_pad = jnp.concatenate([c, conv_flat.reshape(conv_rows, d), jnp.zeros((8 - bsz - conv_rows, d), F32)], axis=0)
    gath = all_gather8("gather_c", c_pad, True).reshape(N_DEV, 8, d)
    c_all = gath[:, :bsz].reshape(N_DEV * bsz, d)
    conv_all = gath[0::2, bsz:bsz + conv_rows].reshape(N_CHIP, conv_rows * d)
    gdn_conv_full, rg_conv_full = [
        _unshard(jnp.stack([_unpack(conv_all[s], conv_shapes)[i] for s in range(N_CHIP)]), 2) for i in range(2)]

    n_half = N_DEV * bsz // 2
    mod_cols = w_mod.shape[2]
    c_rows = lax.dynamic_slice(c_all, (n_half * mc, 0), (n_half, d))
    b_mod_mine = lax.dynamic_slice(b_mod, (0, chip * mod_cols), (depth, mod_cols)).reshape(depth, 1, mod_cols)
    mod_piece = mod_matmul(c_rows, w_mod, b_mod_mine)
    mod_g = all_gather8("gather_mod", mod_piece.reshape(depth * n_half, mod_cols), True)
    mod_all = mod_g.reshape(N_CHIP, 2, depth, n_half, mod_cols).transpose(2, 1, 3, 0, 4).reshape(depth, 2 * n_half, 6 * d)
    mod_mine = lax.dynamic_slice(mod_all, (0, bsz * dev, 0), (depth, bsz, 6 * d)).reshape(depth, bsz, 6, 1, d)

    ids = jnp.stack([chip, mc]).astype(jnp.int32)
    slabs = dict(zip(GATHER_FIRST, all_gather_weights(
        "gather_weights", [cast_into_slab("cast_" + n, wts[n], ids) for n in GATHER_FIRST])))

    def columns(g):
        return g.transpose(0, 2, 1, 3).reshape(g.shape[0], g.shape[2], N_CHIP * g.shape[3])

    def rows_of(g):
        return g.reshape(g.shape[0], N_CHIP * g.shape[2], g.shape[3])

    w_cat = _proj_cols_from_slabs(slabs['w_in'])
    w_q = columns(slabs['mla_w_qb']).astype(F32)
    w_kv = columns(slabs['mla_w_kvb']).astype(F32)
    w_out_full = rows_of(slabs['w_out'])
    bd_a, bd_x = _block_diag(rg_w_a), _block_diag(rg_w_x)

    inv_freq = ROPE_THETA ** (-jnp.arange(0, 32, 2, dtype=F32) / 32.0)
    ang = positions.astype(F32)[..., None] * inv_freq
    cs = jnp.concatenate([jnp.cos(ang), jnp.sin(ang)], axis=-1)

    proj_ch = [w for _, w in PROJ_PIECES]

    def row(a, l):
        return a[l].reshape(1, -1)

    def layer_args(l):
        sh_m, sc_m, gt_m, sh_f, sc_f, gt_f = (mod_mine[l, :, k] for k in range(6))
        return dict(
            mods=(sh_m, sc_m, gt_m, sh_f, sc_f, gt_f),
            mixer_in=dict(ex=[sc_m, sh_m], par=[row(norm_mix_g, l)], big=[(w_cat, l)], out_ch=proj_ch, ts=512),
            gdn_conv=dict(par_tiled=[gdn_conv_full[l]], out_ch=[768], ts=seq, nc=3),
            gdn_local=dict(par=[row(gdn_a_log, l), row(gdn_dt_bias, l)], out_ch=[256] * 7, ts=512),
            rglru=dict(par_tiled=[rg_conv_full[l], row(rg_conv_b, l), row(rg_b_a, l), row(rg_b_x, l), row(rg_lambda, l),
                                  bd_a[l], bd_x[l]], out_ch=[512, 512], ts=seq, nc=4),
            mla_pre=dict(tok_nd=[cs], par=[row(mla_q_norm_g, l), row(mla_kv_norm_g, l), w_q[l], w_kv[l]],
                         out_ch=[ATTN_QW, ATTN_QW, ATTN_VW], ts=512),
            out_proj=dict(ex=[gt_m], big=[(w_out_full, l)], out_ch=[d, d], ts=512),
            mlp_in=dict(ex=[sc_f, sh_f], par=[row(norm_mlp_g, l)], big=[w_mi.get(l)], out_ch=[4 * d], ts=256),
            mlp_out=dict(ex=[gt_f], big=[w_mo.get(l)], out_ch=[d, d], ts=256),
        )

    mi_buf = [cast_into_slab("cast_w_mlp_in%d" % l, wts['w_mlp_in'], ids, layer=l) for l in range(depth)]
    mo_buf = [cast_into_slab("cast_w_mlp_out%d" % l, wts['w_mlp_out'], ids, layer=l) for l in range(depth)]
    w_mi, w_mo = {}, {}

    def staged(name, fn, jobs, **kw):
        return run_stage(name, fn, side=jobs, **kw) if jobs else (run_stage(name, fn, **kw), [])

    saved = []
    h = x
    for l in range(depth):
        a = layer_args(l)
        sfx = str(l)
        first, more = l == 0, l + 1 < depth
        (qkv_raw, z, rx, rgate, mq, mkv, misc), bufs = staged(
            "mixer_in" + sfx, fn_mixer_in, [] if first else [gather_over_d2d(mo_buf[l])], tok=[h], **a['mixer_in'])
        if not first:
            w_mo[l] = bufs[0].reshape(N_CHIP * d, d)
        (qkv_act,) = run_stage("gdn_conv" + sfx, fn_gdn_conv, tok=[qkv_raw], **a['gdn_conv'])
        (*xs, inverses), bufs = staged("gdn_local" + sfx, fn_gdn_local, [gather_over_ici(mi_buf[l])] if first else [],
                                       tok=[qkv_act, misc], **a['gdn_local'])
        if first:
            mi_buf[l] = bufs[0]
        o_a, st_in = gdn_scan(xs, z, row(gdn_norm_g, l))
        (o_b, rg_hidden), bufs = staged("rglru" + sfx, fn_rglru, [gather_over_d2d(mi_buf[l]), gather_over_ici(mo_buf[l])] if first else [],
                              tok=[rx, rgate], **a['rglru'])
        if first:
            w_mi[l], mo_buf[l] = bufs
        q_at, k_at, v_at = run_stage("mla_pre" + sfx, fn_mla_pre, tok=[mq, mkv, misc], **a['mla_pre'])
        o_c = mla_attention(q_at, k_at, v_at)
        (h_mid, mix_out), bufs = staged("out_proj" + sfx, fn_out_proj, [gather_over_d2d(mo_buf[l])] if first else [],
                                tok=[h, o_a, o_b, o_c], **a['out_proj'])
        if first:
            w_mo[l] = bufs[0].reshape(N_CHIP * d, d)
        a = layer_args(l)
        (a_mlp,), bufs = staged("mlp_in" + sfx, fn_mlp_in, [gather_over_ici(mi_buf[l + 1])] if more else [],
                                tok=[h_mid], **a['mlp_in'])
        if more:
            mi_buf[l + 1] = bufs[0]
        (h_out, f_out), bufs = staged("mlp_out" + sfx, fn_mlp_out,
                                [gather_over_d2d(mi_buf[l + 1]), gather_over_ici(mo_buf[l + 1])] if more else [],
                                tok=[h_mid, a_mlp], **a['mlp_out'])
        if more:
            w_mi[l + 1], mo_buf[l + 1] = bufs
        saved.append(dict(h=h, qkv_raw=qkv_raw, z=z, rx=rx, rgate=rgate, mq=mq, mkv=mkv, misc=misc, qkv_act=qkv_act, xs=xs,
                          inverses=inverses,
                          st_in=st_in, o_a=o_a, o_b=o_b, o_c=o_c, q_at=q_at, k_at=k_at, v_at=v_at, h_mid=h_mid, a_mlp=a_mlp,
                          rg_hidden=rg_hidden,
                          mix_out=mix_out, f_out=f_out))
        h = h_out

    loss_part, dh, d_final_g = loss_head(h, final_norm_g.reshape(1, d), loss_target)
    loss = lax.psum(loss_part[0, 0], ("x", "y", "c"))

    g_full = {n: [None] * depth for n in SHARDED}
    g_rep = {n: [None] * depth for n in REPLICATED if n not in ('final_norm_g', 'b_mod')}

    def column_slabs(g):
        return g.reshape(g.shape[0], N_CHIP, g.shape[1] // N_CHIP).transpose(1, 0, 2)

    def row_slabs(g):
        return g.reshape(N_CHIP, g.shape[0] // N_CHIP, g.shape[1])
    core_id = mc.reshape(1).astype(jnp.int32)
    shards = [None] * len(GATHER_BF16)
    mixer_units, mlp_in_unit, mlp_out_unit = [0, 1, 2, 3], [4], [5]

    def reduce_begin(tag, idxs, l):
        gs = [g_full[GATHER_BF16[i]][l] for i in idxs]
        from_sibling = grad_sibling_exchange("grad_sibling_exchange_" + tag, gs)
        pairs = [add_half("grad_add_%s%d" % (GATHER_BF16[i], l), g, s, core_id) for i, g, s in zip(idxs, gs, from_sibling)]
        return [p[0] for p in pairs], [p[1] for p in pairs]

    def reduce_end(idxs, l, sums32, landed):
        for i, p, r in zip(idxs, sums32, landed):
            n = GATHER_BF16[i]
            shards[i] = sum_peers("grad_sum_%s%d" % (n, l), p, r, ids, wts[n].shape, l, acc=shards[i])

    def staged_bwd(name, fn, idxs, l_units, pair, **kw):
        if pair is None:
            return run_stage(name, fn, **kw)
        groups, bufs = run_stage(name, fn, side=[chip_exchange_job(pair[1])], **kw)
        reduce_end(idxs, l_units, pair[0], bufs[len(idxs):])
        return groups

    dmod = [None] * depth
    carried = None
    for l in reversed(range(depth)):
        a, sv = layer_args(l), saved[l]
        sfx = str(l)
        mlp_out_tok = dict(tok=[sv['h_mid'], sv['a_mlp']], tok_nd=[sv['f_out']], cot=[dh])
        (dh_mid, da_mlp), (dgt_f,), _, _, _ = staged_bwd(
            "mlp_out" + sfx, fn_mlp_out, mixer_units, l + 1, carried, which="small", dtok_dtype={1: BF16},
            **mlp_out_tok, **{**a['mlp_out'], 'ts': 256})
        _, _, _, _, (dw_mlp_out,) = run_stage(
            "mlp_out" + sfx, fn_mlp_out, which="big", **mlp_out_tok, **{**a['mlp_out'], 'ts': 512})
        g_full['w_mlp_out'][l] = row_slabs(dw_mlp_out)
        _, _, _, _, (g_full['w_mlp_in'][l],) = run_stage(
            "mlp_in" + sfx, fn_mlp_in, tok=[sv['h_mid']], cot=[da_mlp], which="big", **{**a['mlp_in'], 'ts': 512})
        (dh_mid,), (dsc_f, dsh_f), (g_rep['norm_mlp_g'][l],), _, _ = run_stage(
            "mlp_in" + sfx, fn_mlp_in, tok=[sv['h_mid']], cot=[da_mlp], addin=dh_mid, which="small", **a['mlp_in'])
        mlp_sums32, mlp_sums16 = reduce_begin("mlp" + sfx, mlp_in_unit + mlp_out_unit, l)
        (dh_in, do_a, do_b, do_c), (dgt_m,), _, _, (dw_out,) = run_stage(
            "out_proj" + sfx, fn_out_proj, tok=[sv['h'], sv['o_a'], sv['o_b'], sv['o_c']], tok_nd=[sv['mix_out']],
            cot=[dh_mid], **a['out_proj'])
        g_full['w_out'][l] = row_slabs(dw_out)
        attn_cot = mla_attention_bwd(sv['q_at'], sv['k_at'], sv['v_at'], do_c)
        (dmq, dmkv, dmisc_c), _, (g_rep['mla_q_norm_g'][l], g_rep['mla_kv_norm_g'][l], dw_q, dw_kv), _, _ = run_stage(
            "mla_pre" + sfx, fn_mla_pre, tok=[sv['mq'], sv['mkv'], sv['misc']], cot=attn_cot, **a['mla_pre'])
        g_full['mla_w_qb'][l] = column_slabs(dw_q)
        g_full['mla_w_kvb'][l] = column_slabs(dw_kv)
        (drx, drgate), _, _, rg_g, _ = staged_bwd(
            "rglru" + sfx, fn_rglru, mlp_in_unit, l, (mlp_sums32[:1], mlp_sums16[:1]), tok=[sv['rx'], sv['rgate']],
            tok_nd=[sv['rg_hidden']], cot=[do_b],
            **a['rglru'])
        (g_full['rg_conv_w'][l], g_rep['rg_conv_b'][l], g_rep['rg_b_a'][l], g_rep['rg_b_x'][l], g_rep['rg_lambda'][l],
         g_rep['rg_w_a'][l], g_rep['rg_w_x'][l]) = rg_g
        dxs, dz, g_rep['gdn_norm_g'][l] = gdn_scan_bwd(sv['xs'], sv['z'], row(gdn_norm_g, l), sv['st_in'], do_a)
        (dqkv_act, dmisc_a), _, (g_rep['gdn_a_log'][l], g_rep['gdn_dt_bias'][l]), _, _ = staged_bwd(
            "gdn_local" + sfx, fn_gdn_local, mlp_out_unit, l, (mlp_sums32[1:], mlp_sums16[1:]),
            tok=[sv['qkv_act'], sv['misc']], tok_nd=[sv['inverses']], cot=dxs, **a['gdn_local'])
        (dqkv_raw,), _, _, (g_full['gdn_conv_w'][l],), _ = run_stage(
            "gdn_conv" + sfx, fn_gdn_conv, tok=[sv['qkv_raw']], cot=[dqkv_act], **a['gdn_conv'])
        (dh,), (dsc_m, dsh_m), (g_rep['norm_mix_g'][l],), _, (dw_cat,) = run_stage(
            "mixer_in" + sfx, fn_mixer_in, tok=[sv['h']], cot=[dqkv_raw, dz, drx, drgate, dmq, dmkv, dmisc_a + dmisc_c],
            addin=dh_in, **a['mixer_in'])
        g_full['w_in'][l] = _proj_slabs(dw_cat, wts['w_in'].shape[2])
        dmod[l] = jnp.concatenate([dsh_m, dsc_m, dgt_m, dsh_f, dsc_f, dgt_f], axis=-1).reshape(bsz, 6 * d)
        carried = reduce_begin("mixer" + sfx, mixer_units, l)
    grad_x = dh
    reduce_end(mixer_units, 0, carried[0], grad_chip_exchange("grad_chip_exchange", carried[1]))

    dmod = jnp.stack(dmod)
    dmod_pad = jnp.concatenate([dmod.reshape(depth * bsz, 6 * d), jnp.zeros((8 - depth * bsz, 6 * d), F32)], axis=0)
    dmod_all = all_gather8("gather_dmod", dmod_pad, True).reshape(N_DEV, 8, 6 * d)[:, :depth * bsz]
    dmod_all = dmod_all.reshape(N_DEV, depth, bsz, 6 * d).transpose(1, 0, 2, 3).reshape(depth, N_DEV * bsz, 6 * d)
    g_w_mod = mod_weight_grad(c_all, lax.dynamic_slice(dmod_all, (0, 0, chip * mod_cols), (depth, N_DEV * bsz, mod_cols)))

    g_rep = {n: jnp.stack(v) for n, v in g_rep.items()}
    g_rep['rg_w_a'] = _block_diag_back(g_rep['rg_w_a'])
    g_rep['rg_w_x'] = _block_diag_back(g_rep['rg_w_x'])
    g_rep['final_norm_g'] = d_final_g
    g_rep['b_mod'] = jnp.sum(dmod, axis=1)
    conv_names = ['gdn_conv_w', 'rg_conv_w']
    conv_full_shapes = [(depth,) + g_full[n][0].shape for n in conv_names]
    small_shapes = [wts[n].shape for n in REPLICATED] + conv_full_shapes
    rep_part = _pack_rows([g_rep[n].reshape(wts[n].shape) for n in REPLICATED] + [jnp.stack(g_full[n]) for n in conv_names])
    rep_rows = rep_part.shape[0]
    rep_all = all_gather8("gather_small_grads", rep_part, True).reshape(N_DEV, rep_rows, PACK_COLS)
    conv_zeros = [jnp.zeros(s, F32) for s in conv_full_shapes]
    rep_out = adamw_reduce("adamw_small", rep_all, *[
        _pack_rows([src[n] for n in REPLICATED] + conv_zeros) for src in (wts, mom_m, mom_v)])
    small_names = REPLICATED + conv_names
    rep_g, rep_d, rep_m, rep_v = [dict(zip(small_names, _unpack_rows(o, small_shapes))) for o in rep_out]
    sh_g = {}
    for n in conv_names:
        cols = wts[n].shape[2]
        sh_g[n] = lax.dynamic_slice(rep_g.pop(n), (0, 0, chip * cols), wts[n].shape)
        for dct in (rep_d, rep_m, rep_v):
            dct.pop(n)

    sh_g.update(zip(GATHER_BF16, grad_half_exchange("grad_half_exchange", shards)))
    sh_names = list(SHARDED)

    def as2d(t):
        return t.reshape(-1, t.shape[-1])

    sh_d, sh_m, sh_v = {}, {}, {}
    for n in sh_names + ['w_mod']:
        g = g_w_mod if n == 'w_mod' else sh_g[n]
        res = adamw("adamw_" + n, as2d(wts[n]), as2d(g), as2d(mom_m[n]), as2d(mom_v[n]))
        sh_d[n], sh_m[n], sh_v[n] = (r.reshape(wts[n].shape) for r in res)
    sh_g['w_mod'] = g_w_mod

    def pick(shd, rep):
        return [shd[n] if n in shd else rep[n] for n in WEIGHTS]

    return (loss, grad_x, *pick(sh_g, rep_g), *pick(sh_d, rep_d), *pick(sh_m, rep_m), *pick(sh_v, rep_v))
```

```python
import functools

import jax
import jax.numpy as jnp
import numpy as np
from jax import lax
from jax.experimental import pallas as pl
from jax.experimental.pallas import tpu as pltpu

F32, BF16 = jnp.float32, jnp.bfloat16
HI = lax.Precision.HIGH
MESH = pl.DeviceIdType.MESH

EPS = 1e-6
CHUNK = 64
GDN_HEADS = 4
MLA_HEADS = 4
RG_C = 8.0
ROPE_THETA = 10000.0
N_DEV = 8
N_CHIP = 4
V7X_VMEM_LIMIT = 60 * 1024 * 1024
ADAM_LR, ADAM_B1, ADAM_B2, ADAM_EPS, ADAM_WD, ADAM_STEP = 0.001, 0.9, 0.999, 1e-08, 0.01, 10


def _params(n_grid):
    return pltpu.CompilerParams(dimension_semantics=("arbitrary",) * n_grid, vmem_limit_bytes=V7X_VMEM_LIMIT)


def _dot(a, b, dims=(((1,), (0,)), ((), ()))):
    return lax.dot_general(a.astype(BF16), b.astype(BF16), dims, preferred_element_type=F32)


@jax.custom_vjp
def _mm_probe(x, w, probe):
    return _dot(x, w)


def _mm_probe_fwd(x, w, probe):
    return _dot(x, w), (x, w)


def _mm_probe_bwd(res, dy):
    x, w = res
    dx = _dot(dy, w, (((1,), (1,)), ((), ())))
    dw = _dot(x, dy, (((0,), (0,)), ((), ())))
    return dx, jnp.zeros_like(w), dw


_mm_probe.defvjp(_mm_probe_fwd, _mm_probe_bwd)


@jax.custom_vjp
def _probe_only(x, probe):
    return jnp.zeros((x.shape[0], probe.shape[1]), F32)


def _probe_only_fwd(x, probe):
    return jnp.zeros((x.shape[0], probe.shape[1]), F32), x


def _probe_only_bwd(x, dy):
    return jnp.zeros_like(x), _dot(x, dy, (((0,), (0,)), ((), ())))


_probe_only.defvjp(_probe_only_fwd, _probe_only_bwd)


@jax.custom_vjp
def _mm_known(x, w, y):
    return y


_mm_known.defvjp(lambda x, w, y: (y, w),
                 lambda w, dy: (_dot(dy, w, (((1,), (1,)), ((), ()))), jnp.zeros_like(w), jnp.zeros_like(dy)))


@jax.custom_vjp
def _mm_known_probe(x, w, y, probe):
    return y


_mm_known_probe.defvjp(
    lambda x, w, y, probe: (y, (x, w)),
    lambda res, dy: (_dot(dy, res[1], (((1,), (1,)), ((), ()))), jnp.zeros_like(res[1]), jnp.zeros_like(dy),
                     _dot(res[0], dy, (((0,), (0,)), ((), ())))))


@jax.custom_vjp
def mmw(x, w):
    return _dot(x, w)


def _mmw_fwd(x, w):
    return _dot(x, w), (x, w)


def _mmw_bwd(res, dy):
    x, w = res
    return _dot(dy, w, (((1,), (1,)), ((), ()))), _dot(x, dy, (((0,), (0,)), ((), ())))


mmw.defvjp(_mmw_fwd, _mmw_bwd)


def rms(x, g):
    return x * lax.rsqrt(jnp.mean(x * x, axis=-1, keepdims=True) + EPS) * g


def _rows(shape):
    return lax.broadcasted_iota(jnp.int32, shape, 0)


def _shift_down(x, s, fill):
    return jnp.where(_rows(x.shape) < s, fill, pltpu.roll(x, s, 0))


def _shift_up(x, s, fill):
    n = x.shape[0]
    return jnp.where(_rows(x.shape) >= n - s, fill, pltpu.roll(x, n - s, 0))


def _make_tshift(s):
    @jax.custom_vjp
    def tshift(x):
        return _shift_down(x, s, 0.0)

    tshift.defvjp(lambda x: (_shift_down(x, s, 0.0), None), lambda _, dy: (_shift_up(dy, s, 0.0),))
    return tshift


_TSHIFT = {s: _make_tshift(s) for s in (1, 2, 3)}


def causal_conv4(x, w):
    y = x * w[3:4, :]
    for j in range(3):
        y = y + _TSHIFT[3 - j](x) * w[j:j + 1, :]
    return y


def _scan_steps(n):
    d = 1
    while d < n:
        yield d
        d *= 2


@jax.custom_vjp
def linscan(a, b):
    return _linscan_fwd_impl(a, b)


def _linscan_fwd_impl(a, b):
    for d in _scan_steps(a.shape[0]):
        b = a * _shift_down(b, d, 0.0) + b
        a = a * _shift_down(a, d, 1.0)
    return b


def _linscan_fwd(a, b):
    h = _linscan_fwd_impl(a, b)
    return h, (a, h)


def _linscan_bwd(res, dh):
    a, h = res
    an = _shift_up(a, 1, 0.0)
    lam = dh
    for d in _scan_steps(a.shape[0]):
        lam = an * _shift_up(lam, d, 0.0) + lam
        an = an * _shift_up(an, d, 1.0)
    return lam * _shift_down(h, 1, 0.0), lam


linscan.defvjp(_linscan_fwd, _linscan_bwd)


@jax.custom_vjp
def linscan_known(a, b, h):
    return h


linscan_known.defvjp(lambda a, b, h: (h, (a, h)), lambda res, dh: _linscan_bwd(res, dh) + (jnp.zeros_like(dh),))


def _chunk_scan(x, reverse):
    pos = _rows(x.shape) % CHUNK
    n = x.shape[0]
    d = 1
    while d < CHUNK:
        if reverse:
            x = x + jnp.where(pos < CHUNK - d, pltpu.roll(x, n - d, 0), 0.0)
        else:
            x = x + jnp.where(pos >= d, pltpu.roll(x, d, 0), 0.0)
        d *= 2
    return x


@jax.custom_vjp
def chunk_cumsum(x):
    return _chunk_scan(x, False)


@jax.custom_vjp
def chunk_revcumsum(x):
    return _chunk_scan(x, True)


chunk_cumsum.defvjp(lambda x: (_chunk_scan(x, False), None), lambda _, g: (_chunk_scan(g, True),))
chunk_revcumsum.defvjp(lambda x: (_chunk_scan(x, True), None), lambda _, g: (_chunk_scan(g, False),))


def _bmm(a, b, precision=None):
    return jnp.einsum('nij,njk->nik', a, b, precision=precision, preferred_element_type=F32)


@jax.custom_vjp
def inv_unit_lower(l):
    return _inv_impl(l)


def _inv_impl(l):
    n = l.shape[-1]
    eye = (_rows((n, n)) == lax.broadcasted_iota(jnp.int32, (n, n), 1)).astype(F32)
    p = -l
    a = eye + p
    k = 1
    while 2 * k < n:
        p = _bmm(p, p, HI)
        a = a + _bmm(a, p, HI)
        k *= 2
    return a


def _inv_fwd(l):
    a = _inv_impl(l)
    return a, a


def _inv_bwd(a, da):
    at = jnp.swapaxes(a, 1, 2)
    return (-_bmm(_bmm(at, da, HI), at, HI),)


inv_unit_lower.defvjp(_inv_fwd, _inv_bwd)


@jax.custom_vjp
def inv_unit_lower_known(l, a):
    return a


inv_unit_lower_known.defvjp(lambda l, a: (a, a), lambda a, da: (_inv_bwd(a, da)[0], jnp.zeros_like(a)))


def neg_expm1(y):
    series = -(y * (1.0 + y * (0.5 + y * (1.0 / 6.0 + y * (1.0 / 24.0)))))
    return jnp.where(y > -0.05, series, 1.0 - jnp.exp(y))


def run_stage(name, fn, *, tok, tok_nd=(), ex=(), par=(), par_tiled=(), big=(), out_ch, ts, nc=1, cot=None, addin=None,
              which="all", dtok_dtype=None, side=None):
    tok, tok_nd, ex, par, par_tiled, big = map(list, (tok, tok_nd, ex, par, par_tiled, big))
    big_layer = [b[1] if isinstance(b, tuple) else None for b in big]
    big_arrays = [b[0] if isinstance(b, tuple) else b for b in big]
    big = [jax.ShapeDtypeStruct(a.shape if lyr is None else a.shape[1:], a.dtype) for a, lyr in zip(big_arrays, big_layer)]
    bsz, seq, _ = tok[0].shape
    ts = min(ts, seq)
    ns = seq // ts
    grid = (nc, bsz, ns)

    def tok_spec(a):
        cb = a.shape[-1] // nc
        return pl.BlockSpec((None, ts, cb), lambda c, b, s: (b, s, c))

    def ex_spec(a):
        cb = a.shape[-1] // nc
        return pl.BlockSpec((None, 1, cb), lambda c, b, s: (b, 0, c))

    def full_spec(a, single=False):
        nd = a.ndim
        kw = dict(pipeline_mode=pl.Buffered(1)) if single else {}
        return pl.BlockSpec(a.shape, lambda c, b, s: (0,) * nd, **kw)

    def tiled_spec(a):
        if a.ndim == 2:
            return pl.BlockSpec((a.shape[0], a.shape[1] // nc), lambda c, b, s: (0, c))
        return pl.BlockSpec((None,) + a.shape[1:], lambda c, b, s: (c, 0, 0))

    def big_spec(a, lyr):
        if lyr is None:
            return full_spec(a, True)
        nd = a.ndim
        return pl.BlockSpec((None,) + a.shape[1:], lambda c, b, s: (lyr,) + (0,) * (nd - 1), pipeline_mode=pl.Buffered(1))

    n_tok, n_nd, n_ex, n_par, n_pt, n_big = map(len, (tok, tok_nd, ex, par, par_tiled, big))
    in_arrays = tok + tok_nd + ex + par + par_tiled + big_arrays
    in_specs = ([tok_spec(a) for a in tok + tok_nd] + [ex_spec(a) for a in ex] + [full_spec(a) for a in par]
                + [tiled_spec(a) for a in par_tiled] + [big_spec(a, lyr) for a, lyr in zip(big_arrays, big_layer)])
    out_tok_shapes = [jax.ShapeDtypeStruct((bsz, seq, ch), F32) for ch in out_ch]
    n_in = len(in_arrays)

    def split(vals):
        i = 0
        groups = []
        for n in (n_tok, n_nd, n_ex, n_par, n_pt, n_big):
            groups.append(list(vals[i:i + n]))
            i += n
        return groups

    def split_grads(vals):
        i = 0
        groups = []
        for n in (n_tok, n_ex, n_par, n_pt, n_big):
            groups.append(list(vals[i:i + n]))
            i += n
        return groups

    side = list(side or [])
    side_arrays = [a for job in side for a in job[0]]
    n_side = len(side_arrays)
    side_shapes = [jax.ShapeDtypeStruct(a.shape, a.dtype) for a in side_arrays]
    side_scratch = [pltpu.SemaphoreType.DMA((job[1],)) for job in side for _ in range(2)]

    def side_jobs(side_refs, sems):
        o = 0
        for j, (arrs, _, issue) in enumerate(side):
            yield issue(side_refs[o:o + len(arrs)], sems[2 * j], sems[2 * j + 1])
            o += len(arrs)

    def side_start(side_refs, sems):
        if side:
            c, b, s = pl.program_id(0), pl.program_id(1), pl.program_id(2)

            @pl.when(jnp.logical_and(jnp.logical_and(c == 0, b == 0), s == 0))
            def _():
                for starts, _, _ in side_jobs(side_refs, sems):
                    for cp in starts:
                        cp.start()

    def side_finish(side_refs, sems):
        if side:
            c, b, s = pl.program_id(0), pl.program_id(1), pl.program_id(2)

            @pl.when(jnp.logical_and(jnp.logical_and(c == nc - 1, b == bsz - 1), s == ns - 1))
            def _():
                for _, recv_waits, send_waits in side_jobs(side_refs, sems):
                    for cp in recv_waits:
                        cp.wait_recv()
                    for cp in send_waits:
                        cp.wait_send()

    if cot is None:
        n_out = len(out_tok_shapes)

        def body(*refs):
            tv, ndv, ev, pv, ptv, _ = split([r[...] for r in refs[:n_in - n_big]] + [None] * n_big)
            b_refs = refs[n_in - n_big:n_in]
            side_refs = refs[n_in + n_side + n_out:n_in + 2 * n_side + n_out]
            sems = refs[n_in + 2 * n_side + n_out:]
            side_start(side_refs, sems)
            outs = fn(tv, ndv, ev, pv, ptv,
                      lambda x, i, j=None, known=None: _dot(x, b_refs[i][...] if j is None else b_refs[i][j]))
            for r, o in zip(refs[n_in + n_side:], outs):
                r[...] = o
            side_finish(side_refs, sems)

        res = pl.pallas_call(
            body, name=name, grid=grid, in_specs=in_specs + [_ANY] * n_side,
            out_specs=[tok_spec(a) for a in out_tok_shapes] + [_ANY] * n_side,
            out_shape=out_tok_shapes + side_shapes, input_output_aliases={n_in + j: n_out + j for j in range(n_side)},
            scratch_shapes=side_scratch, compiler_params=_params(3))(*in_arrays, *side_arrays)
        return (res[:n_out], res[n_out:]) if side else res

    cot = list(cot)
    has_addin = addin is not None
    extra = cot + ([addin] if has_addin else [])
    n_cot = len(cot)
    want_small, want_big = which in ("all", "small"), which in ("all", "big")
    if not want_small:
        keep = [i for i in range(n_in - n_big) if not n_tok <= i < n_tok + n_nd]
        in_arrays, in_specs, n_in = [in_arrays[i] for i in keep], [in_specs[i] for i in keep], len(keep)
    small_arrays = tok + ex + par + par_tiled
    g_shapes = [jax.ShapeDtypeStruct(a.shape, F32) for a in (small_arrays if want_small else []) + (big if want_big else [])]
    for i, dt_ in (dtok_dtype or {}).items():
        g_shapes[i] = jax.ShapeDtypeStruct(g_shapes[i].shape, dt_)
    g_specs = (([tok_spec(a) for a in tok] + [ex_spec(a) for a in ex] + [full_spec(a) for a in par]
                + [tiled_spec(a) for a in par_tiled]) if want_small else []) + (
                    [full_spec(a, True) for a in big] if want_big else [])

    def body(*refs):
        c, b, s = pl.program_id(0), pl.program_id(1), pl.program_id(2)
        if want_small:
            n_small_in = n_tok + n_nd + n_ex + n_par + n_pt
            tv, ndv, ev, pv, ptv, _ = split([r[...] for r in refs[:n_small_in]] + [None] * n_big)
            b_refs = refs[n_small_in:n_in]
        else:
            vals = [r[...] for r in refs[:n_in]]
            tv, ndv, ev, pv, ptv, _ = split(vals[:n_tok] + [None] * n_nd + vals[n_tok:] + [None] * n_big)
            b_refs = []
        cots = [r[...].astype(F32) for r in refs[n_in:n_in + n_cot]]
        n_g = len(g_shapes)
        g_refs = list(refs[n_in + len(extra) + n_side:n_in + len(extra) + n_side + n_g])
        side_refs = refs[n_in + len(extra) + n_side + n_g:n_in + len(extra) + 2 * n_side + n_g]
        sems = refs[n_in + len(extra) + 2 * n_side + n_g:]
        side_start(side_refs, sems)
        probes = [jnp.zeros(w.shape, F32) if w.ndim == 2 else [jnp.zeros(w.shape[1:], F32) for _ in range(w.shape[0])]
                  for w in big]

        def f(tv_, ev_, pv_, ptv_, probes_):
            def mm(x, i, j=None, known=None):
                probe = None if probes_ is None else (probes_[i] if j is None else probes_[i][j])
                if not want_small:
                    return _probe_only(x, probe)
                w = b_refs[i][...] if j is None else b_refs[i][j]
                if known is not None:
                    return _mm_known(x, w, known) if probe is None else _mm_known_probe(x, w, known, probe)
                return _dot(x, w) if probe is None else _mm_probe(x, w, probe)

            return fn(tv_, ndv, ev_, pv_, ptv_, mm)

        dt = de = dp = dpt = dbg = ()
        if which == "all":
            dt, de, dp, dpt, dbg = jax.vjp(f, tv, ev, pv, ptv, probes)[1](cots)
        elif which == "small":
            dt, de, dp, dpt = jax.vjp(lambda *a: f(*a, None), tv, ev, pv, ptv)[1](cots)
        else:
            (dbg,) = jax.vjp(lambda p: f(tv, ev, pv, ptv, p), probes)[1](cots)
        if has_addin:
            dt = [dt[0] + refs[n_in + n_cot][...]] + list(dt[1:])
        if want_small:
            gt_r, ge_r, gp_r, gpt_r, gb_r = split_grads(g_refs + ([] if want_big else [None] * n_big))
        else:
            gt_r, ge_r, gp_r, gpt_r, gb_r = [], [], [], [], g_refs
        for r, g in zip(gt_r, dt):
            r[...] = g.astype(r.dtype)

        def accumulate(r, g, first):
            @pl.when(first)
            def _():
                r[...] = g

            @pl.when(jnp.logical_not(first))
            def _():
                r[...] += g

        for r, g in zip(ge_r, de):
            accumulate(r, g, s == 0)
        first_all = jnp.logical_and(jnp.logical_and(c == 0, b == 0), s == 0)
        for r, g in zip(gp_r, dp):
            accumulate(r, g, first_all)
        for r, g in zip(gpt_r, dpt):
            accumulate(r, g, jnp.logical_and(b == 0, s == 0))
        for r, g in zip(gb_r, dbg):
            if isinstance(g, (list, tuple)):
                for j, gj in enumerate(g):
                    accumulate(r.at[j], gj, first_all)
            else:
                accumulate(r, g, first_all)
        side_finish(side_refs, sems)

    n_args = n_in + len(extra)
    res = pl.pallas_call(
        body, name=name + "_bwd" + ("" if which == "all" else "_" + which), grid=grid,
        in_specs=in_specs + [tok_spec(a) for a in extra] + [_ANY] * n_side, out_specs=g_specs + [_ANY] * n_side,
        out_shape=g_shapes + side_shapes, input_output_aliases={n_args + j: len(g_shapes) + j for j in range(n_side)},
        scratch_shapes=side_scratch, compiler_params=_params(3))(*in_arrays, *extra, *side_arrays)
    res, side_out = list(res[:len(g_shapes)]), list(res[len(g_shapes):])
    groups = [[], [], [], [], res] if not want_small else split_grads(res + ([] if want_big else [None] * n_big))
    return (groups, side_out) if side else groups


PROJ_PIECES = (("qkv", 768), ("z", 256), ("rx", 512), ("rgate", 512), ("mq", 256), ("mkv", 128), ("misc", 128))
PROJ_WIDTH = sum(w for _, w in PROJ_PIECES)
MISC_KR, MISC_A, MISC_B = 0, 32, 36


def fn_mixer_in(tok, nd, ex, par, pt, mm):
    (h,), (sc, sh), (g,) = tok, ex, par
    proj = mm(rms(h, g) * (1.0 + sc) + sh, 0)
    outs, o = [], 0
    for _, w in PROJ_PIECES:
        outs.append(proj[:, o:o + w])
        o += w
    return outs


def fn_gdn_conv(tok, nd, ex, par, pt, mm):
    return [jax.nn.silu(causal_conv4(tok[0], pt[0]))]


def _tri_masks():
    r = _rows((CHUNK, CHUNK))
    c = lax.broadcasted_iota(jnp.int32, (CHUNK, CHUNK), 1)
    return (c <= r), (c < r)


def fn_gdn_local(tok, nd, ex, par, pt, mm):
    (qkv, misc), (a_log, dt_bias) = tok, par
    known = nd[0] if nd else None
    ts = qkv.shape[0]
    nb = ts // CHUNK
    lower, strict = _tri_masks()
    g_all = -jnp.exp(a_log) * jax.nn.softplus(misc[:, MISC_A:MISC_A + GDN_HEADS] + dt_bias)
    g_cum = chunk_cumsum(g_all)
    g_tot = g_cum + chunk_revcumsum(g_all) - g_all
    outs = [[] for _ in range(7)]
    for hd in range(GDN_HEADS):
        def head(x, base):
            return x[:, base + 64 * hd: base + 64 * hd + 64]

        def l2n(x):
            return x * lax.rsqrt(jnp.sum(x * x, axis=-1, keepdims=True) + EPS)

        q = (l2n(head(qkv, 0)) * (64.0 ** -0.5)).reshape(nb, CHUNK, 64)
        k = l2n(head(qkv, 256)).reshape(nb, CHUNK, 64)
        v = head(qkv, 512).reshape(nb, CHUNK, 64)
        b = misc[:, MISC_B + hd: MISC_B + hd + 1]
        beta = jax.nn.sigmoid(b).reshape(nb, CHUNK, 1)
        gi = jnp.broadcast_to(g_cum[:, hd:hd + 1].reshape(nb, CHUNK, 1), (nb, CHUNK, CHUNK))
        gl = jnp.broadcast_to(g_tot[:, hd:hd + 1].reshape(nb, CHUNK, 1), (nb, CHUNK, CHUNK))
        diff = gi - jnp.swapaxes(gi, 1, 2)
        decay = jnp.where(lower, jnp.exp(jnp.where(lower, diff, 0.0)), 0.0)
        kb = k * beta
        vb = v * beta
        kk = jnp.einsum('ncd,nsd->ncs', kb.astype(BF16), k.astype(BF16), preferred_element_type=F32)
        lmat = jnp.where(strict, kk * decay, 0.0)
        if known is None:
            amat = inv_unit_lower(lmat)
        else:
            amat = inv_unit_lower_known(lmat, known[:, 64 * hd: 64 * hd + 64].reshape(nb, CHUNK, 64))
        eg = jnp.exp(gi)
        u = _bmm(amat, vb, HI)
        w = _bmm(amat, kb * eg, HI)
        qk = jnp.einsum('ncd,nsd->ncs', q.astype(BF16), k.astype(BF16), preferred_element_type=F32) * decay
        qd = q * eg
        kt = k * jnp.exp(gl - gi)
        cd = jnp.exp(gl)
        for lst, val in zip(outs, (qk, qd, u, w, kt, cd) + (() if known is not None else (amat,))):
            lst.append(val.reshape(ts, 64))
    return [jnp.concatenate(lst, axis=-1) for lst in outs if lst]


def fn_rglru(tok, nd, ex, par, pt, mm):
    (rx, rgate), (conv_w, conv_b, b_a, b_x, lam, bd_a, bd_x) = tok, pt
    xc = causal_conv4(rx, conv_w) + conv_b
    r = jax.nn.sigmoid(mmw(xc, bd_a) + b_a)
    i = jax.nn.sigmoid(mmw(xc, bd_x) + b_x)
    log_a = -RG_C * r * jax.nn.softplus(-lam)
    a = jnp.exp(log_a)
    bterm = jnp.sqrt(neg_expm1(2.0 * log_a)) * (i * xc)
    hidden = linscan_known(a, bterm, nd[0]) if nd else linscan(a, bterm)
    return [hidden * jax.nn.gelu(rgate)] + ([] if nd else [hidden])


def _rope32(x, cos, sin):
    x1, x2 = x[:, :16], x[:, 16:32]
    return jnp.concatenate([x1 * cos - x2 * sin, x2 * cos + x1 * sin], axis=-1)


MLA_QK = 96


def fn_mla_pre(tok, nd, ex, par, pt, mm):
    (mq, mkv, misc), (cs,), (g_q, g_kv, w_q, w_kv) = tok, nd, par
    q = mmw(rms(mq, g_q), w_q)
    kv = mmw(rms(mkv, g_kv), w_kv)
    cos, sin = cs[:, 0:16], cs[:, 16:32]
    kp = _rope32(misc[:, MISC_KR:MISC_KR + 32], cos, sin)
    qs, ks, vs = [], [], []
    for h in range(MLA_HEADS):
        qs += [q[:, MLA_QK * h: MLA_QK * h + 64], _rope32(q[:, MLA_QK * h + 64: MLA_QK * h + 96], cos, sin)]
        ks += [kv[:, 128 * h: 128 * h + 64], kp]
        vs.append(kv[:, 128 * h + 64: 128 * h + 128])
    return [jnp.concatenate(qs, axis=-1), jnp.concatenate(ks, axis=-1), jnp.concatenate(vs, axis=-1)]


def fn_out_proj(tok, nd, ex, par, pt, mm):
    (h, o_a, o_b, o_c), (gt,) = tok, ex
    mix = mm(jnp.concatenate([o_a, o_b, o_c], axis=-1), 0, known=nd[0] if nd else None)
    return [h + gt * mix] + ([] if nd else [mix])


def fn_mlp_in(tok, nd, ex, par, pt, mm):
    (h,), (sc, sh), (g,) = tok, ex, par
    u = rms(h, g) * (1.0 + sc) + sh
    return [jnp.concatenate([mm(u, 0, j) for j in range(N_CHIP)], axis=-1)]


def fn_mlp_out(tok, nd, ex, par, pt, mm):
    (h, a), (gt,) = tok, ex
    f = mm(jnp.square(jax.nn.relu(a)), 0, known=nd[0] if nd else None)
    return [h + gt * f] + ([] if nd else [f])


GDN_W = GDN_HEADS * 64


def _head_mask():
    r = _rows((GDN_W, GDN_W)) // 64
    c = lax.broadcasted_iota(jnp.int32, (GDN_W, GDN_W), 1) // 64
    return r == c


def _heads_diag(x):
    return jnp.where(_head_mask(), jnp.concatenate([x] * GDN_HEADS, axis=0), 0.0)


def _heads_compact(s):
    return s[0:64] + s[64:128] + s[128:192] + s[192:256]


def _gdn_step(state, qk, qd, u, w, kt, cd, z, norm_g):
    v_new = u - _dot(w, state)
    o = _dot(qd, state) + _dot(qk, _heads_diag(v_new))
    update = _dot(kt, v_new, (((0,), (0,)), ((), ())))
    new_state = state * jnp.broadcast_to(cd[0:1, :], (GDN_W, GDN_W)) + jnp.where(_head_mask(), update, 0.0)
    outs = [rms(o[:, 64 * hd: 64 * hd + 64], norm_g) * jax.nn.silu(z[:, 64 * hd: 64 * hd + 64]) for hd in range(GDN_HEADS)]
    return new_state, jnp.concatenate(outs, axis=-1)


def gdn_scan(xs, z, norm_g):
    bsz, seq, _ = z.shape
    n = seq // CHUNK
    blk = pl.BlockSpec((bsz, CHUNK, 256), lambda i: (0, i, 0))

    def body(qk, qd, u, w, kt, cd, z_ref, g_ref, o_ref, st_out, st):
        @pl.when(pl.program_id(0) == 0)
        def _():
            st[...] = jnp.zeros_like(st)

        for b in range(bsz):
            state = st[b]
            st_out[b] = _heads_compact(state)
            st[b], o_ref[b] = _gdn_step(state, qk[b], qd[b], u[b], w[b], kt[b], cd[b], z_ref[b], g_ref[...])

    return pl.pallas_call(
        body, name="gdn_scan", grid=(n,), in_specs=[blk] * 7 + [pl.BlockSpec((1, 64), lambda i: (0, 0))],
        out_specs=[blk, blk], out_shape=[jax.ShapeDtypeStruct((bsz, seq, 256), F32)] * 2,
        scratch_shapes=[pltpu.VMEM((bsz, GDN_W, GDN_W), F32)], compiler_params=_params(1))(*xs, z, norm_g)


def gdn_scan_bwd(xs, z, norm_g, st_in, do):
    bsz, seq, _ = z.shape
    n = seq // CHUNK
    blk = pl.BlockSpec((bsz, CHUNK, 256), lambda i: (0, n - 1 - i, 0))
    gspec = pl.BlockSpec((1, 64), lambda i: (0, 0))

    def body(qk, qd, u, w, kt, cd, z_ref, g_ref, st_ref, do_ref, dqk, dqd, du, dw, dkt, dcd, dz, dg, dst):
        first = pl.program_id(0) == 0

        @pl.when(first)
        def _():
            dst[...] = jnp.zeros_like(dst)

        dg_sum = None
        for b in range(bsz):
            _, vjp = jax.vjp(_gdn_step, _heads_diag(st_ref[b]), qk[b], qd[b], u[b], w[b], kt[b], cd[b], z_ref[b], g_ref[...])
            grads = vjp((dst[b], do_ref[b]))
            dst[b] = jnp.where(_head_mask(), grads[0], 0.0)
            for r, g in zip((dqk, dqd, du, dw, dkt, dcd, dz), grads[1:8]):
                r[b] = g
            dg_sum = grads[8] if dg_sum is None else dg_sum + grads[8]

        @pl.when(first)
        def _():
            dg[...] = dg_sum

        @pl.when(jnp.logical_not(first))
        def _():
            dg[...] += dg_sum

    res = pl.pallas_call(
        body, name="gdn_scan_bwd", grid=(n,), in_specs=[blk] * 7 + [gspec, blk, blk],
        out_specs=[blk] * 7 + [gspec], out_shape=[jax.ShapeDtypeStruct((bsz, seq, 256), F32)] * 7
        + [jax.ShapeDtypeStruct((1, 64), F32)],
        scratch_shapes=[pltpu.VMEM((bsz, GDN_W, GDN_W), F32)], compiler_params=_params(1))(*xs, z, norm_g, st_in, do)
    return list(res[:6]), res[6], res[7]


ATTN_TQ = 256
ATTN_SCALE = 96.0 ** -0.5
ATTN_KEY_QUARTERS_FWD, ATTN_KEY_QUARTERS_BWD = (1, 2, 3, 4), (1, 2, 4)


def _attn_head(q, k, v, q0):
    s = _dot(q, k, (((1,), (1,)), ((), ()))) * ATTN_SCALE
    qc = (q0 + _rows(s.shape)) // CHUNK
    kc = lax.broadcasted_iota(jnp.int32, s.shape, 1) // CHUNK
    s = jnp.where(kc <= qc, s, -1e30)
    p = jnp.exp(s - jnp.max(s, axis=-1, keepdims=True))
    p = p / jnp.sum(p, axis=-1, keepdims=True)
    return _dot(p, v)


def _key_lengths(seq, quarters):
    return sorted({max(ATTN_TQ, seq * q // 4 // ATTN_TQ * ATTN_TQ) for q in quarters})


def _key_variant(i, seq, quarters):
    need = (i + 1) * ATTN_TQ
    return sum(((need > klen).astype(jnp.int32) for klen in _key_lengths(seq, quarters)[:-1]), jnp.int32(0))


ATTN_QW, ATTN_VW = MLA_HEADS * MLA_QK, MLA_HEADS * 64


def _attn_specs(seq):
    def qspec(ch):
        return pl.BlockSpec((None, ATTN_TQ, ch), lambda b, i: (b, i, 0))

    def kspec(ch):
        return pl.BlockSpec((None, seq, ch), lambda b, i: (b, 0, 0))

    return qspec, kspec


def mla_attention(q, k, v):
    bsz, seq, _ = q.shape
    qspec, kspec = _attn_specs(seq)

    def body(q_r, k_r, v_r, o_r):
        i = pl.program_id(1)
        q0 = i * ATTN_TQ

        def with_keys(klen):
            outs = [_attn_head(q_r[:, MLA_QK * h: MLA_QK * h + MLA_QK], k_r[0:klen, MLA_QK * h: MLA_QK * h + MLA_QK],
                               v_r[0:klen, 64 * h: 64 * h + 64], q0) for h in range(MLA_HEADS)]
            o_r[...] = jnp.concatenate(outs, axis=-1)

        for j, klen in enumerate(_key_lengths(seq, ATTN_KEY_QUARTERS_FWD)):
            pl.when(_key_variant(i, seq, ATTN_KEY_QUARTERS_FWD) == j)(functools.partial(with_keys, klen))

    return pl.pallas_call(
        body, name="mla_attention", grid=(bsz, seq // ATTN_TQ), in_specs=[qspec(ATTN_QW), kspec(ATTN_QW), kspec(ATTN_VW)],
        out_specs=qspec(ATTN_VW), out_shape=jax.ShapeDtypeStruct((bsz, seq, ATTN_VW), F32), compiler_params=_params(2))(
            q, k, v)


def mla_attention_bwd(q, k, v, do):
    bsz, seq, _ = q.shape
    qspec, kspec = _attn_specs(seq)

    def body(q_r, k_r, v_r, do_r, dq_r, dk_r, dv_r):
        i = pl.program_id(1)
        q0 = i * ATTN_TQ

        @pl.when(i == 0)
        def _():
            dk_r[...] = jnp.zeros_like(dk_r)
            dv_r[...] = jnp.zeros_like(dv_r)

        def with_keys(klen):
            dq, dk, dv = [], [], []
            for h in range(MLA_HEADS):
                qk = slice(MLA_QK * h, MLA_QK * h + MLA_QK)
                sl = slice(64 * h, 64 * h + 64)
                _, vjp = jax.vjp(functools.partial(_attn_head, q0=q0), q_r[:, qk], k_r[0:klen, qk], v_r[0:klen, sl])
                a, b, c = vjp(do_r[:, sl])
                dq.append(a)
                dk.append(b)
                dv.append(c)
            dq_r[...] = jnp.concatenate(dq, axis=-1)
            dk_r[0:klen, :] += jnp.concatenate(dk, axis=-1)
            dv_r[0:klen, :] += jnp.concatenate(dv, axis=-1)

        for j, klen in enumerate(_key_lengths(seq, ATTN_KEY_QUARTERS_BWD)):
            pl.when(_key_variant(i, seq, ATTN_KEY_QUARTERS_BWD) == j)(functools.partial(with_keys, klen))

    shp = lambda ch: jax.ShapeDtypeStruct((bsz, seq, ch), F32)
    return pl.pallas_call(
        body, name="mla_attention_bwd", grid=(bsz, seq // ATTN_TQ),
        in_specs=[qspec(ATTN_QW), kspec(ATTN_QW), kspec(ATTN_VW), qspec(ATTN_VW)],
        out_specs=[qspec(ATTN_QW), kspec(ATTN_QW), kspec(ATTN_VW)],
        out_shape=[shp(ATTN_QW), shp(ATTN_QW), shp(ATTN_VW)], compiler_params=_params(2))(q, k, v, do)


LOSS_TS = 512


def loss_head(h, g, target):
    bsz, seq, d = h.shape
    ts = min(LOSS_TS, seq)
    tok = pl.BlockSpec((None, ts, d), lambda b, s: (b, s, 0))
    gspec = pl.BlockSpec((1, d), lambda b, s: (0, 0))
    lspec = pl.BlockSpec((1, 128), lambda b, s: (0, 0))

    def body(h_r, g_r, t_r, loss_r, dh_r, dg_r):
        first = jnp.logical_and(pl.program_id(0) == 0, pl.program_id(1) == 0)
        tv = t_r[...]

        def f(hv, gv):
            return 0.5 * jnp.sum(jnp.mean(jnp.square(rms(hv, gv) - tv), axis=-1, keepdims=True), axis=0, keepdims=True)

        val, vjp = jax.vjp(f, h_r[...], g_r[...])
        dh, dg = vjp(jnp.ones((1, 1), F32))
        dh_r[...] = dh
        lv = jnp.broadcast_to(val, (1, 128))

        @pl.when(first)
        def _():
            loss_r[...] = lv
            dg_r[...] = dg

        @pl.when(jnp.logical_not(first))
        def _():
            loss_r[...] += lv
            dg_r[...] += dg

    return pl.pallas_call(
        body, name="loss_head", grid=(bsz, seq // ts), in_specs=[tok, gspec, tok], out_specs=[lspec, tok, gspec],
        out_shape=[jax.ShapeDtypeStruct((1, 128), F32), jax.ShapeDtypeStruct(h.shape, F32), jax.ShapeDtypeStruct((1, d), F32)],
        compiler_params=_params(2))(h, g, target)


def _adamw_math(w, g, m, v):
    m = ADAM_B1 * m + (1.0 - ADAM_B1) * g
    v = ADAM_B2 * v + (1.0 - ADAM_B2) * jnp.square(g)
    m_hat = m / (1.0 - ADAM_B1 ** ADAM_STEP)
    v_hat = v / (1.0 - ADAM_B2 ** ADAM_STEP)
    return -ADAM_LR * (m_hat / (jnp.sqrt(v_hat) + ADAM_EPS) + ADAM_WD * w), m, v


def _row_block(rows, cols):
    want = max(8, (1 << 18) // cols)
    best = rows
    for r in range(8, rows + 1, 8):
        if rows % r == 0 and r <= want:
            best = r
    return best if rows % 8 == 0 else rows


def adamw(name, w, g, m, v):
    rows, cols = w.shape
    rb = _row_block(rows, cols)
    spec = pl.BlockSpec((rb, cols), lambda i: (i, 0))

    def body(w_r, g_r, m_r, v_r, g_o, d_o, m_o, v_o):
        g = g_r[...]
        d, mn, vn = _adamw_math(w_r[...], g, m_r[...], v_r[...])
        g_o[...] = g
        d_o[...] = d
        m_o[...] = mn
        v_o[...] = vn

    return pl.pallas_call(body, name=name, grid=(rows // rb,), in_specs=[spec] * 4, out_specs=[spec] * 4,
                          out_shape=[jax.ShapeDtypeStruct(w.shape, F32)] * 4, compiler_params=_params(1))(w, g, m, v)


def adamw_reduce(name, parts, w, m, v):
    rows, cols = w.shape
    rb = _row_block(rows, cols)
    spec = pl.BlockSpec((rb, cols), lambda i: (i, 0))
    pspec = pl.BlockSpec((N_DEV, rb, cols), lambda i: (0, i, 0))

    def body(p_r, w_r, m_r, v_r, g_o, d_o, m_o, v_o):
        g = p_r[0]
        for k in range(1, N_DEV):
            g = g + p_r[k]
        d, mn, vn = _adamw_math(w_r[...], g, m_r[...], v_r[...])
        g_o[...] = g
        d_o[...] = d
        m_o[...] = mn
        v_o[...] = vn

    return pl.pallas_call(body, name=name, grid=(rows // rb,), in_specs=[pspec, spec, spec, spec], out_specs=[spec] * 4,
                          out_shape=[jax.ShapeDtypeStruct(w.shape, F32)] * 4, compiler_params=_params(1))(parts, w, m, v)


MOD_CB = 512


def mod_matmul(c_rows, w_mod, b_mod):
    nl, d, cols = w_mod.shape

    def body(c_r, w_r, b_r, o_r):
        o_r[...] = _dot(jax.nn.silu(c_r[...]), w_r[...]) + b_r[...]

    return pl.pallas_call(
        body, name="mod_matmul", grid=(nl, cols // MOD_CB),
        in_specs=[pl.BlockSpec((8, d), lambda l, j: (0, 0)), pl.BlockSpec((None, d, MOD_CB), lambda l, j: (l, 0, j)),
                  pl.BlockSpec((None, 1, MOD_CB), lambda l, j: (l, 0, j))],
        out_specs=pl.BlockSpec((None, 8, MOD_CB), lambda l, j: (l, 0, j)),
        out_shape=jax.ShapeDtypeStruct((nl, 8, cols), F32), compiler_params=_params(2))(c_rows, w_mod, b_mod)


def mod_weight_grad(c_all, dmod):
    nl, nb, cols = dmod.shape
    d = c_all.shape[1]

    def body(c_r, g_r, o_r):
        o_r[...] = _dot(jax.nn.silu(c_r[...]), g_r[...], (((0,), (0,)), ((), ())))

    return pl.pallas_call(
        body, name="mod_weight_grad", grid=(nl, cols // MOD_CB),
        in_specs=[pl.BlockSpec((nb, d), lambda l, j: (0, 0)), pl.BlockSpec((None, nb, MOD_CB), lambda l, j: (l, 0, j))],
        out_specs=pl.BlockSpec((None, d, MOD_CB), lambda l, j: (l, 0, j)),
        out_shape=jax.ShapeDtypeStruct((nl, d, cols), F32), compiler_params=_params(2))(c_all, dmod)


def _half_block(hr, cols):
    rb = _row_block(hr, cols)
    return rb if rb % 16 == 0 else hr


def add_half(name, g, s, core):
    _, r, cols = g.shape
    hr = r // 2
    rb = _half_block(hr, cols)
    nblk = hr // rb
    gspec = pl.BlockSpec((None, rb, cols), lambda k, i, c: (k, c[0] * nblk + i, 0))
    spec = pl.BlockSpec((None, rb, cols), lambda k, i, c: (k, i, 0))

    def body(c_r, g_r, s_r, o_r, ob_r):
        t = g_r[...] + s_r[...]
        o_r[...] = t
        ob_r[...] = t.astype(BF16)

    return pl.pallas_call(
        body, name=name, grid_spec=pltpu.PrefetchScalarGridSpec(num_scalar_prefetch=1, grid=(N_CHIP, nblk),
                                                                in_specs=[gspec, spec], out_specs=[spec, spec]),
        out_shape=[jax.ShapeDtypeStruct((N_CHIP, hr, cols), F32), jax.ShapeDtypeStruct((N_CHIP, hr, cols), BF16)],
        compiler_params=_params(2))(core, g, s)


def sum_peers(name, p32, recv, ids, shard_shape, layer, acc=None):
    _, hr, cols = p32.shape
    rb = _half_block(hr, cols)
    nblk = hr // rb

    def slot(k):
        return pl.BlockSpec((None, rb, cols), lambda i, c: ((c[0] + k) % N_CHIP, i, 0))

    def body(c_r, o_r, r1, r2, r3, *rest):
        rest[-1][...] = ((o_r[...] + r1[...].astype(F32)) + r2[...].astype(F32)) + r3[...].astype(F32)

    args = (ids, p32, recv, recv, recv) + (() if acc is None else (acc,))
    return pl.pallas_call(
        body, name=name, grid_spec=pltpu.PrefetchScalarGridSpec(
            num_scalar_prefetch=1, grid=(nblk,),
            in_specs=[slot(0), slot(1), slot(2), slot(3)] + ([] if acc is None else [_ANY]),
            out_specs=pl.BlockSpec((None, rb, cols), lambda i, c: (layer, c[1] * nblk + i, 0))),
        out_shape=jax.ShapeDtypeStruct(shard_shape, F32), input_output_aliases={} if acc is None else {5: 0},
        compiler_params=_params(1))(*args)


def cast_into_slab(name, w, ids, layer=None):
    nl, r, cols = w.shape
    hr = r // 2
    rb = _half_block(hr, cols)
    nblk = hr // rb

    def body(c_r, w_r, o_r):
        o_r[...] = w_r[...].astype(BF16)

    if layer is not None:
        return pl.pallas_call(
            body, name=name, grid_spec=pltpu.PrefetchScalarGridSpec(
                num_scalar_prefetch=1, grid=(nblk,),
                in_specs=[pl.BlockSpec((None, rb, cols), lambda i, c: (layer, c[1] * nblk + i, 0))],
                out_specs=pl.BlockSpec((None, rb, cols), lambda i, c: (c[0], c[1] * nblk + i, 0))),
            out_shape=jax.ShapeDtypeStruct((N_CHIP, r, cols), BF16), compiler_params=_params(1))(ids, w)
    return pl.pallas_call(
        body, name=name, grid_spec=pltpu.PrefetchScalarGridSpec(
            num_scalar_prefetch=1, grid=(nl, nblk),
            in_specs=[pl.BlockSpec((None, rb, cols), lambda l, i, c: (l, c[1] * nblk + i, 0))],
            out_specs=pl.BlockSpec((None, None, rb, cols), lambda l, i, c: (l, c[0], c[1] * nblk + i, 0))),
        out_shape=jax.ShapeDtypeStruct((nl, N_CHIP, r, cols), BF16), compiler_params=_params(2))(ids, w)


def _me():
    return lax.axis_index("x"), lax.axis_index("y"), lax.axis_index("c")


def all_gather8(name, x_shard, in_vmem):
    m_per, n = x_shard.shape
    space = pltpu.VMEM if in_vmem else pl.ANY

    def body(x_ref, out_ref, send_sems, recv_sems, local_sem):
        x, y, c = _me()
        me, sibling = (x, y, c), (x, y, 1 - c)
        chips = [(1 - x, y), (x, 1 - y), (1 - x, 1 - y)]

        def rows(px, py, pc):
            return out_ref.at[pl.ds((4 * px + 2 * py + pc) * m_per, m_per), :]

        def copy(k, block, to, src=None):
            return pltpu.make_async_remote_copy(
                src_ref=rows(*block) if src is None else src, dst_ref=rows(*block), send_sem=send_sems.at[k],
                recv_sem=recv_sems.at[k], device_id=to, device_id_type=MESH)

        mine = pltpu.make_async_copy(x_ref, rows(*me), local_sem)
        mine.start()
        first = [copy(0, me, sibling, src=x_ref)]
        first += [copy(1 + j, me, (*chip, c), src=x_ref) for j, chip in enumerate(chips)]
        for cp in first:
            cp.start()
        passed = [copy(4 + j, (*chip, c), sibling) for j, chip in enumerate(chips)]
        for j, chip in enumerate(chips):
            copy(1 + j, (*chip, c), me).wait_recv()
            passed[j].start()
        copy(0, sibling, me).wait_recv()
        for j, chip in enumerate(chips):
            copy(4 + j, (*chip, 1 - c), me).wait_recv()
        for cp in first + passed:
            cp.wait_send()
        mine.wait()

    return pl.pallas_call(
        body, name=name, out_shape=jax.ShapeDtypeStruct((N_DEV * m_per, n), x_shard.dtype),
        in_specs=[pl.BlockSpec(memory_space=space)], out_specs=pl.BlockSpec(memory_space=space),
        scratch_shapes=[pltpu.SemaphoreType.DMA((7,)), pltpu.SemaphoreType.DMA((7,)), pltpu.SemaphoreType.DMA],
    )(x_shard)


_ANY = pl.BlockSpec(memory_space=pl.ANY)


def all_gather_weights(name, slabs):
    n = len(slabs)

    def body(*refs):
        outs = refs[n:2 * n]
        send_sems, recv_sems = refs[2 * n:]
        x, y, c = _me()
        me, sibling = (x, y, c), (x, y, 1 - c)
        chips = [(1 - x, y), (x, 1 - y), (1 - x, 1 - y)]

        def view(i, px, py, pc):
            hr = slabs[i].shape[2] // 2
            return outs[i].at[:, 2 * px + py, pl.ds(pc * hr, hr), :]

        def copy(i, k, block, to):
            return pltpu.make_async_remote_copy(
                src_ref=view(i, *block), dst_ref=view(i, *block), send_sem=send_sems.at[i, k],
                recv_sem=recv_sems.at[i, k], device_id=to, device_id_type=MESH)

        first = []
        for i in range(n):
            first.append(copy(i, 0, me, sibling))
            first += [copy(i, 1 + j, me, (*chip, c)) for j, chip in enumerate(chips)]
        for cp in first:
            cp.start()
        passed = []
        for j, chip in enumerate(chips):
            for i in range(n):
                copy(i, 1 + j, (*chip, c), me).wait_recv()
                passed.append(copy(i, 4 + j, (*chip, c), sibling))
                passed[-1].start()
        for i in range(n):
            copy(i, 0, sibling, me).wait_recv()
            for j, chip in enumerate(chips):
                copy(i, 4 + j, (*chip, 1 - c), me).wait_recv()
        for cp in first + passed:
            cp.wait_send()

    return pl.pallas_call(
        body, name=name, out_shape=[jax.ShapeDtypeStruct(s.shape, s.dtype) for s in slabs],
        in_specs=[_ANY] * n, out_specs=[_ANY] * n, input_output_aliases={i: i for i in range(n)},
        scratch_shapes=[pltpu.SemaphoreType.DMA((n, 7)), pltpu.SemaphoreType.DMA((n, 7))],
    )(*slabs)


def _slab_block(slab, px, py, pc):
    hr = slab.shape[1] // 2
    return slab.at[2 * px + py, pl.ds(pc * hr, hr), :]


def gather_over_ici(slab):
    def issue(refs, send_sems, recv_sems):
        (buf,) = refs
        x, y, c = _me()
        peers = [(x, y, 1 - c), (1 - x, y, c), (x, 1 - y, c), (1 - x, 1 - y, c)]

        def copy(k, block, to):
            return pltpu.make_async_remote_copy(
                src_ref=_slab_block(buf, *block), dst_ref=_slab_block(buf, *block), send_sem=send_sems.at[k],
                recv_sem=recv_sems.at[k], device_id=to, device_id_type=MESH)

        sends = [copy(k, (x, y, c), p) for k, p in enumerate(peers)]
        arrivals = [copy(k, p, (x, y, c)) for k, p in enumerate(peers)]
        return sends, arrivals, sends

    return ([slab], 4, issue)


def gather_over_d2d(slab):
    def issue(refs, send_sems, recv_sems):
        (buf,) = refs
        x, y, c = _me()
        chips = [(1 - x, y), (x, 1 - y), (1 - x, 1 - y)]

        def copy(k, block):
            return pltpu.make_async_remote_copy(
                src_ref=_slab_block(buf, *block), dst_ref=_slab_block(buf, *block), send_sem=send_sems.at[k],
                recv_sem=recv_sems.at[k], device_id=(x, y, 1 - c), device_id_type=MESH)

        sends = [copy(k, (*chip, c)) for k, chip in enumerate(chips)]
        arrivals = [copy(k, (*chip, 1 - c)) for k, chip in enumerate(chips)]
        return sends, arrivals, sends

    return ([slab], 3, issue)


def chip_exchange_job(ps):
    n = len(ps)

    def issue(refs, send_sems, recv_sems):
        ins, outs = refs[:n], refs[n:]
        mx, my, mc = _me()
        ci = 2 * mx + my
        chips = [(1 - mx, my), (mx, 1 - my), (1 - mx, 1 - my)]
        sends, arrivals = [], []
        for i in range(n):
            for k, (px, py) in enumerate(chips):
                sem = 3 * i + k
                sends.append(pltpu.make_async_remote_copy(
                    src_ref=ins[i].at[2 * px + py], dst_ref=outs[i].at[ci], send_sem=send_sems.at[sem],
                    recv_sem=recv_sems.at[sem], device_id=(px, py, mc), device_id_type=MESH))
                arrivals.append(pltpu.make_async_remote_copy(
                    src_ref=ins[i].at[ci], dst_ref=outs[i].at[2 * px + py], send_sem=send_sems.at[sem],
                    recv_sem=recv_sems.at[sem], device_id=(px, py, mc), device_id_type=MESH))
        return sends, arrivals, sends

    return (list(ps) + [lax.empty(p.shape, p.dtype) for p in ps], 3 * n, issue)


def grad_sibling_exchange(name, gs):
    n = len(gs)

    def body(*refs):
        ins, outs = refs[:n], refs[n:2 * n]
        send_sems, recv_sems = refs[2 * n:]
        mx, my, mc = _me()
        cps = []
        for i in range(n):
            hr = gs[i].shape[1] // 2
            cps.append(pltpu.make_async_remote_copy(
                src_ref=ins[i].at[:, pl.ds((1 - mc) * hr, hr), :], dst_ref=outs[i], send_sem=send_sems.at[i],
                recv_sem=recv_sems.at[i], device_id=(mx, my, 1 - mc), device_id_type=MESH))
            cps[-1].start()
        for cp in cps:
            cp.wait()

    return pl.pallas_call(
        body, name=name, out_shape=[jax.ShapeDtypeStruct((N_CHIP, g.shape[1] // 2, g.shape[2]), g.dtype) for g in gs],
        in_specs=[_ANY] * n, out_specs=[_ANY] * n,
        scratch_shapes=[pltpu.SemaphoreType.DMA((n,)), pltpu.SemaphoreType.DMA((n,))],
    )(*gs)


def grad_chip_exchange(name, ps):
    n = len(ps)

    def body(*refs):
        ins, outs = refs[:n], refs[n:2 * n]
        send_sems, recv_sems = refs[2 * n:]
        mx, my, mc = _me()
        ci = 2 * mx + my
        chips = [(1 - mx, my), (mx, 1 - my), (1 - mx, 1 - my)]
        sends = []
        for i in range(n):
            for k, (px, py) in enumerate(chips):
                sends.append(pltpu.make_async_remote_copy(
                    src_ref=ins[i].at[2 * px + py], dst_ref=outs[i].at[ci], send_sem=send_sems.at[i, k],
                    recv_sem=recv_sems.at[i, k], device_id=(px, py, mc), device_id_type=MESH))
                sends[-1].start()
        for i in range(n):
            for k, (px, py) in enumerate(chips):
                pltpu.make_async_remote_copy(
                    src_ref=ins[i].at[ci], dst_ref=outs[i].at[2 * px + py], send_sem=send_sems.at[i, k],
                    recv_sem=recv_sems.at[i, k], device_id=(px, py, mc), device_id_type=MESH).wait_recv()
        for cp in sends:
            cp.wait_send()

    return pl.pallas_call(
        body, name=name, out_shape=[jax.ShapeDtypeStruct(p.shape, p.dtype) for p in ps], in_specs=[_ANY] * n,
        out_specs=[_ANY] * n, scratch_shapes=[pltpu.SemaphoreType.DMA((n, 3)), pltpu.SemaphoreType.DMA((n, 3))],
    )(*ps)


def grad_half_exchange(name, shards):
    n = len(shards)

    def body(*refs):
        outs = refs[n:2 * n]
        send_sems, recv_sems = refs[2 * n:]
        mx, my, mc = _me()

        def copy(i, core):
            hr = shards[i].shape[1] // 2
            rows = outs[i].at[:, pl.ds(core * hr, hr), :]
            return pltpu.make_async_remote_copy(src_ref=rows, dst_ref=rows, send_sem=send_sems.at[i],
                                                recv_sem=recv_sems.at[i], device_id=(mx, my, 1 - mc), device_id_type=MESH)

        sends = [copy(i, mc) for i in range(n)]
        for cp in sends:
            cp.start()
        for i in range(n):
            copy(i, 1 - mc).wait_recv()
        for cp in sends:
            cp.wait_send()

    return pl.pallas_call(
        body, name=name, out_shape=[jax.ShapeDtypeStruct(s.shape, s.dtype) for s in shards], in_specs=[_ANY] * n,
        out_specs=[_ANY] * n, input_output_aliases={i: i for i in range(n)},
        scratch_shapes=[pltpu.SemaphoreType.DMA((n,)), pltpu.SemaphoreType.DMA((n,))],
    )(*shards)


WEIGHTS = ['w_mod', 'b_mod', 'norm_mix_g', 'w_in', 'gdn_conv_w', 'gdn_a_log', 'gdn_dt_bias', 'gdn_norm_g', 'rg_conv_w',
           'rg_conv_b', 'rg_w_a', 'rg_b_a', 'rg_w_x', 'rg_b_x', 'rg_lambda', 'mla_q_norm_g', 'mla_w_qb', 'mla_kv_norm_g',
           'mla_w_kvb', 'w_out', 'norm_mlp_g', 'w_mlp_in', 'w_mlp_out', 'final_norm_g']
SHARDED = {'w_in': 2, 'gdn_conv_w': 2, 'rg_conv_w': 2, 'mla_w_qb': 2, 'mla_w_kvb': 2, 'w_out': 1, 'w_mlp_in': 2, 'w_mlp_out': 1}
GATHER_BF16 = ('w_in', 'mla_w_qb', 'mla_w_kvb', 'w_out', 'w_mlp_in', 'w_mlp_out')
GATHER_FIRST = GATHER_BF16[:4]
REPLICATED = [n for n in WEIGHTS if n not in SHARDED and n != 'w_mod']
PACK_COLS = 1024


def _pack(arrays, multiple):
    flat = jnp.concatenate([a.reshape(-1) for a in arrays])
    pad = (-flat.shape[0]) % multiple
    return jnp.pad(flat, (0, pad)) if pad else flat


def _unpack(flat, shapes):
    out, o = [], 0
    for shp in shapes:
        n = int(np.prod(shp))
        out.append(flat[o:o + n].reshape(shp))
        o += n
    return out


def _pack_rows(arrays):
    rows = []
    for a in arrays:
        flat = a.reshape(-1)
        pad = (-flat.shape[0]) % PACK_COLS
        rows.append((jnp.pad(flat, (0, pad)) if pad else flat).reshape(-1, PACK_COLS))
    out = jnp.concatenate(rows, axis=0)
    pad = (-out.shape[0]) % 8
    return jnp.pad(out, ((0, pad), (0, 0))) if pad else out


def _unpack_rows(packed, shapes):
    out, r = [], 0
    for shp in shapes:
        n = int(np.prod(shp))
        nr = -(-n // PACK_COLS)
        piece = packed[r:r + nr]
        out.append((piece if n == nr * PACK_COLS else piece.reshape(-1)[:n]).reshape(shp))
        r += nr
    return out


def _unshard(stacked, axis):
    moved = jnp.moveaxis(stacked, 0, axis)
    shp = list(moved.shape)
    shp[axis:axis + 2] = [shp[axis] * shp[axis + 1]]
    return moved.reshape(shp)


_PROJ_SEGMENTS = ((0, 1024, 0), (1032, 2472, 1024), (1024, 1032, 2464))


def _proj_cols_from_slabs(slabs):
    w = slabs.shape[-1]
    pieces = []
    for o0, o1, _ in sorted(_PROJ_SEGMENTS, key=lambda s: s[2]):
        for k in range(N_CHIP):
            a, b = max(o0, k * w), min(o1, (k + 1) * w)
            if a < b:
                pieces.append(slabs[:, k, :, a - k * w: b - k * w])
    used = sum(p.shape[-1] for p in pieces)
    pieces.append(jnp.zeros(slabs.shape[:1] + slabs.shape[2:3] + (PROJ_WIDTH - used,), slabs.dtype))
    return jnp.concatenate(pieces, axis=-1)


def _proj_slabs(d, w):
    slabs = []
    for k in range(N_CHIP):
        pieces = []
        for o0, o1, c0 in sorted(_PROJ_SEGMENTS):
            a, b = max(o0, k * w), min(o1, (k + 1) * w)
            if a < b:
                pieces.append(d[:, c0 + a - o0: c0 + b - o0])
        slabs.append(jnp.concatenate(pieces, axis=-1))
    return jnp.stack(slabs)


def _block_diag(w):
    nl = w.shape[0]
    eye = jnp.eye(2, dtype=w.dtype)
    return jnp.einsum('lcoij,op->lcoipj', w.reshape(nl, 4, 2, 64, 64), eye).reshape(nl, 4, 128, 128)


def _block_diag_back(g):
    nl = g.shape[0]
    return jnp.einsum('lcoipj,op->lcoij', g.reshape(nl, 4, 2, 64, 2, 64), jnp.eye(2, dtype=g.dtype)).reshape(nl, 8, 64, 64)


def kernel(x, c, positions, w_mod, b_mod, norm_mix_g, w_in, gdn_conv_w, gdn_a_log, gdn_dt_bias, gdn_norm_g, rg_conv_w, rg_conv_b, rg_w_a, rg_b_a, rg_w_x, rg_b_x, rg_lambda, mla_q_norm_g, mla_w_qb, mla_kv_norm_g, mla_w_kvb, w_out, norm_mlp_g, w_mlp_in, w_mlp_out, final_norm_g, loss_target, m_w_mod, m_b_mod, m_norm_mix_g, m_w_in, m_gdn_conv_w, m_gdn_a_log, m_gdn_dt_bias, m_gdn_norm_g, m_rg_conv_w, m_rg_conv_b, m_rg_w_a, m_rg_b_a, m_rg_w_x, m_rg_b_x, m_rg_lambda, m_mla_q_norm_g, m_mla_w_qb, m_mla_kv_norm_g, m_mla_w_kvb, m_w_out, m_norm_mlp_g, m_w_mlp_in, m_w_mlp_out, m_final_norm_g, v_w_mod, v_b_mod, v_norm_mix_g, v_w_in, v_gdn_conv_w, v_gdn_a_log, v_gdn_dt_bias, v_gdn_norm_g, v_rg_conv_w, v_rg_conv_b, v_rg_w_a, v_rg_b_a, v_rg_w_x, v_rg_b_x, v_rg_lambda, v_mla_q_norm_g, v_mla_w_qb, v_mla_kv_norm_g, v_mla_w_kvb, v_w_out, v_norm_mlp_g, v_w_mlp_in, v_w_mlp_out, v_final_norm_g):
    given = dict(locals())
    wts = {n: given[n] for n in WEIGHTS}
    mom_m = {n: given["m_" + n] for n in WEIGHTS}
    mom_v = {n: given["v_" + n] for n in WEIGHTS}
    bsz, seq, d = x.shape
    depth = w_mod.shape[0]
    mx, my, mc = lax.axis_index("x"), lax.axis_index("y"), lax.axis_index("c")
    chip = 2 * mx + my
    dev = 2 * chip + mc

    conv_shapes = [wts['gdn_conv_w'].shape, wts['rg_conv_w'].shape]
    conv_flat = _pack([wts['gdn_conv_w'], wts['rg_conv_w']], d)
    conv_rows = conv_flat.shape[0] // d
    assert bsz + conv_rows <= 8
    c_pad = jnp.concatenate([c, conv_flat.reshape(conv_rows, d), jnp.zeros((8 - bsz - conv_rows, d), F32)], axis=0)
    gath = all_gather8("gather_c", c_pad, True).reshape(N_DEV, 8, d)
    c_all = gath[:, :bsz].reshape(N_DEV * bsz, d)
    conv_all = gath[0::2, bsz:bsz + conv_rows].reshape(N_CHIP, conv_rows * d)
    gdn_conv_full, rg_conv_full = [
        _unshard(jnp.stack([_unpack(conv_all[s], conv_shapes)[i] for s in range(N_CHIP)]), 2) for i in range(2)]

    n_half = N_DEV * bsz // 2
    mod_cols = w_mod.shape[2]
    c_rows = lax.dynamic_slice(c_all, (n_half * mc, 0), (n_half, d))
    b_mod_mine = lax.dynamic_slice(b_mod, (0, chip * mod_cols), (depth, mod_cols)).reshape(depth, 1, mod_cols)
    mod_piece = mod_matmul(c_rows, w_mod, b_mod_mine)
    mod_g = all_gather8("gather_mod", mod_piece.reshape(depth * n_half, mod_cols), True)
    mod_all = mod_g.reshape(N_CHIP, 2, depth, n_half, mod_cols).transpose(2, 1, 3, 0, 4).reshape(depth, 2 * n_half, 6 * d)
    mod_mine = lax.dynamic_slice(mod_all, (0, bsz * dev, 0), (depth, bsz, 6 * d)).reshape(depth, bsz, 6, 1, d)

    ids = jnp.stack([chip, mc]).astype(jnp.int32)
    slabs = dict(zip(GATHER_FIRST, all_gather_weights(
        "gather_weights", [cast_into_slab("cast_" + n, wts[n], ids) for n in GATHER_FIRST])))

    def columns(g):
        return g.transpose(0, 2, 1, 3).reshape(g.shape[0], g.shape[2], N_CHIP * g.shape[3])

    def rows_of(g):
        return g.reshape(g.shape[0], N_CHIP * g.shape[2], g.shape[3])

    w_cat = _proj_cols_from_slabs(slabs['w_in'])
    w_q = columns(slabs['mla_w_qb']).astype(F32)
    w_kv = columns(slabs['mla_w_kvb']).astype(F32)
    w_out_full = rows_of(slabs['w_out'])
    bd_a, bd_x = _block_diag(rg_w_a), _block_diag(rg_w_x)

    inv_freq = ROPE_THETA ** (-jnp.arange(0, 32, 2, dtype=F32) / 32.0)
    ang = positions.astype(F32)[..., None] * inv_freq
    cs = jnp.concatenate([jnp.cos(ang), jnp.sin(ang)], axis=-1)

    proj_ch = [w for _, w in PROJ_PIECES]

    def row(a, l):
        return a[l].reshape(1, -1)

    def layer_args(l):
        sh_m, sc_m, gt_m, sh_f, sc_f, gt_f = (mod_mine[l, :, k] for k in range(6))
        return dict(
            mods=(sh_m, sc_m, gt_m, sh_f, sc_f, gt_f),
            mixer_in=dict(ex=[sc_m, sh_m], par=[row(norm_mix_g, l)], big=[(w_cat, l)], out_ch=proj_ch, ts=512),
            gdn_conv=dict(par_tiled=[gdn_conv_full[l]], out_ch=[768], ts=seq, nc=3),
            gdn_local=dict(par=[row(gdn_a_log, l), row(gdn_dt_bias, l)], out_ch=[256] * 7, ts=512),
            rglru=dict(par_tiled=[rg_conv_full[l], row(rg_conv_b, l), row(rg_b_a, l), row(rg_b_x, l), row(rg_lambda, l),
                                  bd_a[l], bd_x[l]], out_ch=[512, 512], ts=seq, nc=4),
            mla_pre=dict(tok_nd=[cs], par=[row(mla_q_norm_g, l), row(mla_kv_norm_g, l), w_q[l], w_kv[l]],
                         out_ch=[ATTN_QW, ATTN_QW, ATTN_VW], ts=512),
            out_proj=dict(ex=[gt_m], big=[(w_out_full, l)], out_ch=[d, d], ts=512),
            mlp_in=dict(ex=[sc_f, sh_f], par=[row(norm_mlp_g, l)], big=[w_mi.get(l)], out_ch=[4 * d], ts=256),
            mlp_out=dict(ex=[gt_f], big=[w_mo.get(l)], out_ch=[d, d], ts=256),
        )

    mi_buf = [cast_into_slab("cast_w_mlp_in%d" % l, wts['w_mlp_in'], ids, layer=l) for l in range(depth)]
    mo_buf = [cast_into_slab("cast_w_mlp_out%d" % l, wts['w_mlp_out'], ids, layer=l) for l in range(depth)]
    w_mi, w_mo = {}, {}

    def staged(name, fn, jobs, **kw):
        return run_stage(name, fn, side=jobs, **kw) if jobs else (run_stage(name, fn, **kw), [])

    saved = []
    h = x
    for l in range(depth):
        a = layer_args(l)
        sfx = str(l)
        first, more = l == 0, l + 1 < depth
        (qkv_raw, z, rx, rgate, mq, mkv, misc), bufs = staged(
            "mixer_in" + sfx, fn_mixer_in, [] if first else [gather_over_d2d(mo_buf[l])], tok=[h], **a['mixer_in'])
        if not first:
            w_mo[l] = bufs[0].reshape(N_CHIP * d, d)
        (qkv_act,) = run_stage("gdn_conv" + sfx, fn_gdn_conv, tok=[qkv_raw], **a['gdn_conv'])
        (*xs, inverses), bufs = staged("gdn_local" + sfx, fn_gdn_local, [gather_over_ici(mi_buf[l])] if first else [],
                                       tok=[qkv_act, misc], **a['gdn_local'])
        if first:
            mi_buf[l] = bufs[0]
        o_a, st_in = gdn_scan(xs, z, row(gdn_norm_g, l))
        (o_b, rg_hidden), bufs = staged("rglru" + sfx, fn_rglru, [gather_over_d2d(mi_buf[l]), gather_over_ici(mo_buf[l])] if first else [],
                              tok=[rx, rgate], **a['rglru'])
        if first:
            w_mi[l], mo_buf[l] = bufs
        q_at, k_at, v_at = run_stage("mla_pre" + sfx, fn_mla_pre, tok=[mq, mkv, misc], **a['mla_pre'])
        o_c = mla_attention(q_at, k_at, v_at)
        (h_mid, mix_out), bufs = staged("out_proj" + sfx, fn_out_proj, [gather_over_d2d(mo_buf[l])] if first else [],
                                tok=[h, o_a, o_b, o_c], **a['out_proj'])
        if first:
            w_mo[l] = bufs[0].reshape(N_CHIP * d, d)
        a = layer_args(l)
        (a_mlp,), bufs = staged("mlp_in" + sfx, fn_mlp_in, [gather_over_ici(mi_buf[l + 1])] if more else [],
                                tok=[h_mid], **a['mlp_in'])
        if more:
            mi_buf[l + 1] = bufs[0]
        (h_out, f_out), bufs = staged("mlp_out" + sfx, fn_mlp_out,
                                [gather_over_d2d(mi_buf[l + 1]), gather_over_ici(mo_buf[l + 1])] if more else [],
                                tok=[h_mid, a_mlp], **a['mlp_out'])
        if more:
            w_mi[l + 1], mo_buf[l + 1] = bufs
        saved.append(dict(h=h, qkv_raw=qkv_raw, z=z, rx=rx, rgate=rgate, mq=mq, mkv=mkv, misc=misc, qkv_act=qkv_act, xs=xs,
                          inverses=inverses,
                          st_in=st_in, o_a=o_a, o_b=o_b, o_c=o_c, q_at=q_at, k_at=k_at, v_at=v_at, h_mid=h_mid, a_mlp=a_mlp,
                          rg_hidden=rg_hidden,
                          mix_out=mix_out, f_out=f_out))
        h = h_out

    loss_part, dh, d_final_g = loss_head(h, final_norm_g.reshape(1, d), loss_target)
    loss = lax.psum(loss_part[0, 0], ("x", "y", "c"))

    g_full = {n: [None] * depth for n in SHARDED}
    g_rep = {n: [None] * depth for n in REPLICATED if n not in ('final_norm_g', 'b_mod')}

    def column_slabs(g):
        return g.reshape(g.shape[0], N_CHIP, g.shape[1] // N_CHIP).transpose(1, 0, 2)

    def row_slabs(g):
        return g.reshape(N_CHIP, g.shape[0] // N_CHIP, g.shape[1])
    core_id = mc.reshape(1).astype(jnp.int32)
    shards = [None] * len(GATHER_BF16)
    mixer_units, mlp_in_unit, mlp_out_unit = [0, 1, 2, 3], [4], [5]

    def reduce_begin(tag, idxs, l):
        gs = [g_full[GATHER_BF16[i]][l] for i in idxs]
        from_sibling = grad_sibling_exchange("grad_sibling_exchange_" + tag, gs)
        pairs = [add_half("grad_add_%s%d" % (GATHER_BF16[i], l), g, s, core_id) for i, g, s in zip(idxs, gs, from_sibling)]
        return [p[0] for p in pairs], [p[1] for p in pairs]

    def reduce_end(idxs, l, sums32, landed):
        for i, p, r in zip(idxs, sums32, landed):
            n = GATHER_BF16[i]
            shards[i] = sum_peers("grad_sum_%s%d" % (n, l), p, r, ids, wts[n].shape, l, acc=shards[i])

    def staged_bwd(name, fn, idxs, l_units, pair, **kw):
        if pair is None:
            return run_stage(name, fn, **kw)
        groups, bufs = run_stage(name, fn, side=[chip_exchange_job(pair[1])], **kw)
        reduce_end(idxs, l_units, pair[0], bufs[len(idxs):])
        return groups

    dmod = [None] * depth
    carried = None
    for l in reversed(range(depth)):
        a, sv = layer_args(l), saved[l]
        sfx = str(l)
        mlp_out_tok = dict(tok=[sv['h_mid'], sv['a_mlp']], tok_nd=[sv['f_out']], cot=[dh])
        (dh_mid, da_mlp), (dgt_f,), _, _, _ = staged_bwd(
            "mlp_out" + sfx, fn_mlp_out, mixer_units, l + 1, carried, which="small", dtok_dtype={1: BF16},
            **mlp_out_tok, **{**a['mlp_out'], 'ts': 256})
        _, _, _, _, (dw_mlp_out,) = run_stage(
            "mlp_out" + sfx, fn_mlp_out, which="big", **mlp_out_tok, **{**a['mlp_out'], 'ts': 512})
        g_full['w_mlp_out'][l] = row_slabs(dw_mlp_out)
        _, _, _, _, (g_full['w_mlp_in'][l],) = run_stage(
            "mlp_in" + sfx, fn_mlp_in, tok=[sv['h_mid']], cot=[da_mlp], which="big", **{**a['mlp_in'], 'ts': 512})
        (dh_mid,), (dsc_f, dsh_f), (g_rep['norm_mlp_g'][l],), _, _ = run_stage(
            "mlp_in" + sfx, fn_mlp_in, tok=[sv['h_mid']], cot=[da_mlp], addin=dh_mid, which="small", **a['mlp_in'])
        mlp_sums32, mlp_sums16 = reduce_begin("mlp" + sfx, mlp_in_unit + mlp_out_unit, l)
        (dh_in, do_a, do_b, do_c), (dgt_m,), _, _, (dw_out,) = run_stage(
            "out_proj" + sfx, fn_out_proj, tok=[sv['h'], sv['o_a'], sv['o_b'], sv['o_c']], tok_nd=[sv['mix_out']],
            cot=[dh_mid], **a['out_proj'])
        g_full['w_out'][l] = row_slabs(dw_out)
        attn_cot = mla_attention_bwd(sv['q_at'], sv['k_at'], sv['v_at'], do_c)
        (dmq, dmkv, dmisc_c), _, (g_rep['mla_q_norm_g'][l], g_rep['mla_kv_norm_g'][l], dw_q, dw_kv), _, _ = run_stage(
            "mla_pre" + sfx, fn_mla_pre, tok=[sv['mq'], sv['mkv'], sv['misc']], cot=attn_cot, **a['mla_pre'])
        g_full['mla_w_qb'][l] = column_slabs(dw_q)
        g_full['mla_w_kvb'][l] = column_slabs(dw_kv)
        (drx, drgate), _, _, rg_g, _ = staged_bwd(
            "rglru" + sfx, fn_rglru, mlp_in_unit, l, (mlp_sums32[:1], mlp_sums16[:1]), tok=[sv['rx'], sv['rgate']],
            tok_nd=[sv['rg_hidden']], cot=[do_b],
            **a['rglru'])
        (g_full['rg_conv_w'][l], g_rep['rg_conv_b'][l], g_rep['rg_b_a'][l], g_rep['rg_b_x'][l], g_rep['rg_lambda'][l],
         g_rep['rg_w_a'][l], g_rep['rg_w_x'][l]) = rg_g
        dxs, dz, g_rep['gdn_norm_g'][l] = gdn_scan_bwd(sv['xs'], sv['z'], row(gdn_norm_g, l), sv['st_in'], do_a)
        (dqkv_act, dmisc_a), _, (g_rep['gdn_a_log'][l], g_rep['gdn_dt_bias'][l]), _, _ = staged_bwd(
            "gdn_local" + sfx, fn_gdn_local, mlp_out_unit, l, (mlp_sums32[1:], mlp_sums16[1:]),
            tok=[sv['qkv_act'], sv['misc']], tok_nd=[sv['inverses']], cot=dxs, **a['gdn_local'])
        (dqkv_raw,), _, _, (g_full['gdn_conv_w'][l],), _ = run_stage(
            "gdn_conv" + sfx, fn_gdn_conv, tok=[sv['qkv_raw']], cot=[dqkv_act], **a['gdn_conv'])
        (dh,), (dsc_m, dsh_m), (g_rep['norm_mix_g'][l],), _, (dw_cat,) = run_stage(
            "mixer_in" + sfx, fn_mixer_in, tok=[sv['h']], cot=[dqkv_raw, dz, drx, drgate, dmq, dmkv, dmisc_a + dmisc_c],
            addin=dh_in, **a['mixer_in'])
        g_full['w_in'][l] = _proj_slabs(dw_cat, wts['w_in'].shape[2])
        dmod[l] = jnp.concatenate([dsh_m, dsc_m, dgt_m, dsh_f, dsc_f, dgt_f], axis=-1).reshape(bsz, 6 * d)
        carried = reduce_begin("mixer" + sfx, mixer_units, l)
    grad_x = dh
    reduce_end(mixer_units, 0, carried[0], grad_chip_exchange("grad_chip_exchange", carried[1]))

    dmod = jnp.stack(dmod)
    dmod_pad = jnp.concatenate([dmod.reshape(depth * bsz, 6 * d), jnp.zeros((8 - depth * bsz, 6 * d), F32)], axis=0)
    dmod_all = all_gather8("gather_dmod", dmod_pad, True).reshape(N_DEV, 8, 6 * d)[:, :depth * bsz]
    dmod_all = dmod_all.reshape(N_DEV, depth, bsz, 6 * d).transpose(1, 0, 2, 3).reshape(depth, N_DEV * bsz, 6 * d)
    g_w_mod = mod_weight_grad(c_all, lax.dynamic_slice(dmod_all, (0, 0, chip * mod_cols), (depth, N_DEV * bsz, mod_cols)))

    g_rep = {n: jnp.stack(v) for n, v in g_rep.items()}
    g_rep['rg_w_a'] = _block_diag_back(g_rep['rg_w_a'])
    g_rep['rg_w_x'] = _block_diag_back(g_rep['rg_w_x'])
    g_rep['final_norm_g'] = d_final_g
    g_rep['b_mod'] = jnp.sum(dmod, axis=1)
    conv_names = ['gdn_conv_w', 'rg_conv_w']
    conv_full_shapes = [(depth,) + g_full[n][0].shape for n in conv_names]
    small_shapes = [wts[n].shape for n in REPLICATED] + conv_full_shapes
    rep_part = _pack_rows([g_rep[n].reshape(wts[n].shape) for n in REPLICATED] + [jnp.stack(g_full[n]) for n in conv_names])
    rep_rows = rep_part.shape[0]
    rep_all = all_gather8("gather_small_grads", rep_part, True).reshape(N_DEV, rep_rows, PACK_COLS)
    conv_zeros = [jnp.zeros(s, F32) for s in conv_full_shapes]
    rep_out = adamw_reduce("adamw_small", rep_all, *[
        _pack_rows([src[n] for n in REPLICATED] + conv_zeros) for src in (wts, mom_m, mom_v)])
    small_names = REPLICATED + conv_names
    rep_g, rep_d, rep_m, rep_v = [dict(zip(small_names, _unpack_rows(o, small_shapes))) for o in rep_out]
    sh_g = {}
    for n in conv_names:
        cols = wts[n].shape[2]
        sh_g[n] = lax.dynamic_slice(rep_g.pop(n), (0, 0, chip * cols), wts[n].shape)
        for dct in (rep_d, rep_m, rep_v):
            dct.pop(n)

    sh_g.update(zip(GATHER_BF16, grad_half_exchange("grad_half_exchange", shards)))
    sh_names = list(SHARDED)

    def as2d(t):
        return t.reshape(-1, t.shape[-1])

    sh_d, sh_m, sh_v = {}, {}, {}
    for n in sh_names + ['w_mod']:
        g = g_w_mod if n == 'w_mod' else sh_g[n]
        res = adamw("adamw_" + n, as2d(wts[n]), as2d(g), as2d(mom_m[n]), as2d(mom_v[n]))
        sh_g[n], sh_d[n], sh_m[n], sh_v[n] = (r.reshape(wts[n].shape) for r in res)

    def pick(shd, rep):
        return [shd[n] if n in shd else rep[n] for n in WEIGHTS]

    return (loss, grad_x, *pick(sh_g, rep_g), *pick(sh_d, rep_d), *pick(sh_m, rep_m), *pick(sh_v, rep_v))
```
